```python
import math
import jax, jax.numpy as jnp
from jax import lax
import numpy as np

D_MODEL = 1024
BATCH = 8
SEQ = 8192
DEPTH = 1

D_MIX = D_MODEL
N_HEADS_A = 8
HEAD_DIM_A = 64
DIL_CONFIGS = ((128, 1), (512, 4), (2048, 16))
Q_BLOCK = 128
N_HEADS_B = 8
QK_NOPE = 64
QK_ROPE = 32
V_DIM = 64
Q_LORA = 384
KV_LORA = 256
ROPE_BASE = 10000.0
D_FF = 2816
CONV_WIDTH = 3
EPS = 1e-6
NEG = -1e30

WIDTH_A = N_HEADS_A * HEAD_DIM_A
WIDTH_B = N_HEADS_B * V_DIM
IN_SIZES = (WIDTH_A, WIDTH_A, WIDTH_A, Q_LORA, KV_LORA, QK_ROPE)
D_IN = sum(IN_SIZES)
SPLIT_POINTS = tuple(int(v) for v in np.cumsum(IN_SIZES)[:-1])

kernel_name = "hybrid_dilated_swa_mla_convffn_sandwich"


def _rmsnorm(x, g):
    xf = x.astype(jnp.float32)
    y = xf * lax.rsqrt(jnp.mean(xf * xf, axis=-1, keepdims=True) + EPS)
    return (y * g.astype(jnp.float32)).astype(x.dtype)


def _rope(x, cos, sin):
    xf = x.astype(jnp.float32)
    half = xf.shape[-1] // 2
    x1, x2 = xf[..., :half], xf[..., half:]
    out = jnp.concatenate([x1 * cos - x2 * sin, x2 * cos + x1 * sin], axis=-1)
    return out.astype(x.dtype)


def _dilated_branch(q, k, v, slopes, window, dilation):
    b, h, s, d = q.shape
    r = dilation
    half = window // (2 * dilation)
    L = s // r
    qb = min(Q_BLOCK, L)
    nblk = -(-L // qb)
    Lp = nblk * qb
    slab = qb + 2 * half

    def to_dilated(t):
        return t.reshape(b, h, L, r, d).transpose(0, 1, 3, 2, 4)

    qd, kd, vd = to_dilated(q), to_dilated(k), to_dilated(v)
    qd = jnp.pad(qd, ((0, 0), (0, 0), (0, 0), (0, Lp - L), (0, 0)))
    pad_kv = ((0, 0), (0, 0), (0, 0), (half, half + Lp - L), (0, 0))
    kd = jnp.pad(kd, pad_kv)
    vd = jnp.pad(vd, pad_kv)
    idx = jnp.arange(nblk)[:, None] * qb + jnp.arange(slab)[None, :]
    k_s = kd[:, :, :, idx, :].astype(jnp.float32)
    v_s = vd[:, :, :, idx, :].astype(jnp.float32)
    q_s = qd.reshape(b, h, r, nblk, qb, d).astype(jnp.float32)

    scores = jnp.einsum('bhcnqd,bhcnkd->bhcnqk', q_s, k_s) * (d ** -0.5)
    off = jnp.arange(slab)[None, :] - half - jnp.arange(qb)[:, None]
    key_pos = idx - half
    valid = (jnp.abs(off) <= half)[None, :, :] & ((key_pos >= 0) & (key_pos < L))[:, None, :]
    dist = (jnp.abs(off) * r).astype(jnp.float32)
    alibi = -slopes[:, None, None] * dist[None]
    scores = scores + alibi[None, :, None, None]
    scores = jnp.where(valid[None, None, None], scores, NEG)
    lse = jax.nn.logsumexp(scores, axis=-1)
    p = jnp.exp(scores - lse[..., None])
    o = jnp.einsum('bhcnqk,bhcnkd->bhcnqd', p, v_s)

    o = o.reshape(b, h, r, Lp, d)[:, :, :, :L].transpose(0, 1, 3, 2, 4).reshape(b, h, s, d)
    lse = lse.reshape(b, h, r, Lp)[:, :, :, :L].transpose(0, 1, 3, 2).reshape(b, h, s)
    return o, lse


def _dilated_attention(qa, ka, va):
    b, s, _ = qa.shape
    def heads(t):
        return t.reshape(b, s, N_HEADS_A, HEAD_DIM_A).transpose(0, 2, 1, 3)
    q, k, v = heads(qa), heads(ka), heads(va)
    slopes = jnp.exp2(-8.0 * jnp.arange(1, N_HEADS_A + 1, dtype=jnp.float32) / N_HEADS_A)
    outs, lses = [], []
    for window, dilation in DIL_CONFIGS:
        o, lse = _dilated_branch(q, k, v, slopes, window, dilation)
        outs.append(o)
        lses.append(lse)
    w = jax.nn.softmax(jnp.stack(lses, axis=0), axis=0)
    o = jnp.einsum('gbhs,gbhsd->bhsd', w, jnp.stack(outs, axis=0))
    return o.transpose(0, 2, 1, 3).reshape(b, s, WIDTH_A).astype(qa.dtype)


def _mla(c_q, c_kv, k_r, q_lat_norm, w_uq, kv_lat_norm, w_ukv):
    b, s, _ = c_q.shape
    pos = jnp.arange(s, dtype=jnp.float32)
    inv_freq = jnp.exp(-math.log(ROPE_BASE) * jnp.arange(0, QK_ROPE, 2, dtype=jnp.float32) / QK_ROPE)
    ang = pos[:, None] * inv_freq[None, :]
    cos, sin = jnp.cos(ang), jnp.sin(ang)

    q = (_rmsnorm(c_q, q_lat_norm) @ w_uq).reshape(b, s, N_HEADS_B, QK_NOPE + QK_ROPE)
    q_nope = q[..., :QK_NOPE]
    q_rope = _rope(q[..., QK_NOPE:], cos[:, None, :], sin[:, None, :])
    kv = (_rmsnorm(c_kv, kv_lat_norm) @ w_ukv).reshape(b, s, N_HEADS_B, QK_NOPE + V_DIM)
    k_nope = kv[..., :QK_NOPE].transpose(0, 2, 1, 3).astype(jnp.float32)
    v = kv[..., QK_NOPE:].transpose(0, 2, 1, 3).astype(jnp.float32)
    k_rope = _rope(k_r, cos, sin).astype(jnp.float32)
    scale = (QK_NOPE + QK_ROPE) ** -0.5

    nq = s // Q_BLOCK
    qn_blk = q_nope.reshape(b, nq, Q_BLOCK, N_HEADS_B, QK_NOPE).transpose(1, 0, 3, 2, 4)
    qr_blk = q_rope.reshape(b, nq, Q_BLOCK, N_HEADS_B, QK_ROPE).transpose(1, 0, 3, 2, 4)

    def block(args):
        qn, qr = args
        sc = (jnp.einsum('bhqd,bhkd->bhqk', qn.astype(jnp.float32), k_nope)
              + jnp.einsum('bhqr,bkr->bhqk', qr.astype(jnp.float32), k_rope)) * scale
        p = jax.nn.softmax(sc, axis=-1)
        return jnp.einsum('bhqk,bhkd->bhqd', p, v)

    o = lax.map(block, (qn_blk, qr_blk))
    return o.transpose(1, 0, 3, 2, 4).reshape(b, s, WIDTH_B).astype(c_q.dtype)


def _dwconv(u, w, bias):
    c = u.shape[-1]
    y = lax.conv_general_dilated(
        u, w[:, None, :].astype(u.dtype), window_strides=(1,),
        padding=((CONV_WIDTH // 2, CONV_WIDTH // 2),),
        dimension_numbers=('NWC', 'WIO', 'NWC'), feature_group_count=c)
    return y + bias.astype(u.dtype)


def _fwd_setup_inputs(seed: int = 0) -> dict:
    key = jax.random.key(seed)
    ks = jax.random.split(key, 17)
    f32 = jnp.float32

    def gain(k, n):
        return 1.0 + 0.1 * jax.random.normal(k, (DEPTH, n), f32)

    def dense(k, fan_in, fan_out):
        return jax.random.normal(k, (DEPTH, fan_in, fan_out), f32) * fan_in ** -0.5

    return {
        "x": jax.random.normal(ks[0], (BATCH, SEQ, D_MODEL), f32),
        "norm_mix_pre": gain(ks[1], D_MODEL),
        "w_in": dense(ks[2], D_MODEL, D_IN),
        "q_lat_norm": gain(ks[3], Q_LORA),
        "w_uq": dense(ks[4], Q_LORA, N_HEADS_B * (QK_NOPE + QK_ROPE)),
        "kv_lat_norm": gain(ks[5], KV_LORA),
        "w_ukv": dense(ks[6], KV_LORA, N_HEADS_B * (QK_NOPE + V_DIM)),
        "out_norm_a": gain(ks[7], WIDTH_A),
        "out_norm_b": gain(ks[8], WIDTH_B),
        "w_o": dense(ks[9], D_MIX, D_MODEL),
        "norm_mix_post": gain(ks[10], D_MODEL),
        "norm_ffn_pre": gain(ks[11], D_MODEL),
        "w_up": dense(ks[12], D_MODEL, 2 * D_FF),
        "conv_w": jax.random.normal(ks[13], (DEPTH, CONV_WIDTH, 2 * D_FF), f32) * CONV_WIDTH ** -0.5,
        "conv_b": 0.02 * jax.random.normal(ks[14], (DEPTH, 2 * D_FF), f32),
        "w_down": dense(ks[15], D_FF, D_MODEL),
        "norm_ffn_post": gain(ks[16], D_MODEL),
    }


def _fwd_reference(x, norm_mix_pre, w_in, q_lat_norm, w_uq, kv_lat_norm, w_ukv, out_norm_a,
              out_norm_b, w_o, norm_mix_post, norm_ffn_pre, w_up, conv_w, conv_b, w_down,
              norm_ffn_post):
    for l in range(DEPTH):
        h = _rmsnorm(x, norm_mix_pre[l])
        proj = h @ w_in[l]
        qa, ka, va, c_q, c_kv, k_r = jnp.split(proj, SPLIT_POINTS, axis=-1)
        ya = _dilated_attention(qa, ka, va)
        yb = _mla(c_q, c_kv, k_r, q_lat_norm[l], w_uq[l], kv_lat_norm[l], w_ukv[l])
        y = jnp.concatenate([_rmsnorm(ya, out_norm_a[l]), _rmsnorm(yb, out_norm_b[l])], axis=-1)
        y = y @ w_o[l]
        x = x + _rmsnorm(y, norm_mix_post[l])
        h = _rmsnorm(x, norm_ffn_pre[l])
        u = _dwconv(h @ w_up[l], conv_w[l], conv_b[l])
        g, v = u[..., :D_FF], u[..., D_FF:]
        y = (jax.nn.gelu(g, approximate=True) * v) @ w_down[l]
        x = x + _rmsnorm(y, norm_ffn_post[l])
    return x


import jax as _jax
import jax.numpy as _jnp

TWIN_FORMAT = 'train_step'
FWD_PARAMS = ['x', 'norm_mix_pre', 'w_in', 'q_lat_norm', 'w_uq', 'kv_lat_norm', 'w_ukv', 'out_norm_a', 'out_norm_b', 'w_o', 'norm_mix_post', 'norm_ffn_pre', 'w_up', 'conv_w', 'conv_b', 'w_down', 'norm_ffn_post']
TWIN_WEIGHTS = ['norm_mix_pre', 'w_in', 'q_lat_norm', 'w_uq', 'kv_lat_norm', 'w_ukv', 'out_norm_a', 'out_norm_b', 'w_o', 'norm_mix_post', 'norm_ffn_pre', 'w_up', 'conv_w', 'conv_b', 'w_down', 'norm_ffn_post']
TWIN_DIFF_INPUT = 'x'
TWIN_INPUTS = ['x', 'norm_mix_pre', 'w_in', 'q_lat_norm', 'w_uq', 'kv_lat_norm', 'w_ukv', 'out_norm_a', 'out_norm_b', 'w_o', 'norm_mix_post', 'norm_ffn_pre', 'w_up', 'conv_w', 'conv_b', 'w_down', 'norm_ffn_post', 'loss_target', 'm_norm_mix_pre', 'm_w_in', 'm_q_lat_norm', 'm_w_uq', 'm_kv_lat_norm', 'm_w_ukv', 'm_out_norm_a', 'm_out_norm_b', 'm_w_o', 'm_norm_mix_post', 'm_norm_ffn_pre', 'm_w_up', 'm_conv_w', 'm_conv_b', 'm_w_down', 'm_norm_ffn_post', 'v_norm_mix_pre', 'v_w_in', 'v_q_lat_norm', 'v_w_uq', 'v_kv_lat_norm', 'v_w_ukv', 'v_out_norm_a', 'v_out_norm_b', 'v_w_o', 'v_norm_mix_post', 'v_norm_ffn_pre', 'v_w_up', 'v_conv_w', 'v_conv_b', 'v_w_down', 'v_norm_ffn_post']
TWIN_OUTPUTS = ['loss', 'grad_x', 'grad_norm_mix_pre', 'grad_w_in', 'grad_q_lat_norm', 'grad_w_uq', 'grad_kv_lat_norm', 'grad_w_ukv', 'grad_out_norm_a', 'grad_out_norm_b', 'grad_w_o', 'grad_norm_mix_post', 'grad_norm_ffn_pre', 'grad_w_up', 'grad_conv_w', 'grad_conv_b', 'grad_w_down', 'grad_norm_ffn_post', 'delta_norm_mix_pre', 'delta_w_in', 'delta_q_lat_norm', 'delta_w_uq', 'delta_kv_lat_norm', 'delta_w_ukv', 'delta_out_norm_a', 'delta_out_norm_b', 'delta_w_o', 'delta_norm_mix_post', 'delta_norm_ffn_pre', 'delta_w_up', 'delta_conv_w', 'delta_conv_b', 'delta_w_down', 'delta_norm_ffn_post', 'new_m_norm_mix_pre', 'new_m_w_in', 'new_m_q_lat_norm', 'new_m_w_uq', 'new_m_kv_lat_norm', 'new_m_w_ukv', 'new_m_out_norm_a', 'new_m_out_norm_b', 'new_m_w_o', 'new_m_norm_mix_post', 'new_m_norm_ffn_pre', 'new_m_w_up', 'new_m_conv_w', 'new_m_conv_b', 'new_m_w_down', 'new_m_norm_ffn_post', 'new_v_norm_mix_pre', 'new_v_w_in', 'new_v_q_lat_norm', 'new_v_w_uq', 'new_v_kv_lat_norm', 'new_v_w_ukv', 'new_v_out_norm_a', 'new_v_out_norm_b', 'new_v_w_o', 'new_v_norm_mix_post', 'new_v_norm_ffn_pre', 'new_v_w_up', 'new_v_conv_w', 'new_v_conv_b', 'new_v_w_down', 'new_v_norm_ffn_post']
TWIN_LEAF_KINDS = {'loss': 'loss', 'grad_x': 'grad_x', 'grad_norm_mix_pre': 'grad_w', 'grad_w_in': 'grad_w', 'grad_q_lat_norm': 'grad_w', 'grad_w_uq': 'grad_w', 'grad_kv_lat_norm': 'grad_w', 'grad_w_ukv': 'grad_w', 'grad_out_norm_a': 'grad_w', 'grad_out_norm_b': 'grad_w', 'grad_w_o': 'grad_w', 'grad_norm_mix_post': 'grad_w', 'grad_norm_ffn_pre': 'grad_w', 'grad_w_up': 'grad_w', 'grad_conv_w': 'grad_w', 'grad_conv_b': 'grad_w', 'grad_w_down': 'grad_w', 'grad_norm_ffn_post': 'grad_w', 'delta_norm_mix_pre': 'delta_w', 'delta_w_in': 'delta_w', 'delta_q_lat_norm': 'delta_w', 'delta_w_uq': 'delta_w', 'delta_kv_lat_norm': 'delta_w', 'delta_w_ukv': 'delta_w', 'delta_out_norm_a': 'delta_w', 'delta_out_norm_b': 'delta_w', 'delta_w_o': 'delta_w', 'delta_norm_mix_post': 'delta_w', 'delta_norm_ffn_pre': 'delta_w', 'delta_w_up': 'delta_w', 'delta_conv_w': 'delta_w', 'delta_conv_b': 'delta_w', 'delta_w_down': 'delta_w', 'delta_norm_ffn_post': 'delta_w', 'new_m_norm_mix_pre': 'new_m', 'new_m_w_in': 'new_m', 'new_m_q_lat_norm': 'new_m', 'new_m_w_uq': 'new_m', 'new_m_kv_lat_norm': 'new_m', 'new_m_w_ukv': 'new_m', 'new_m_out_norm_a': 'new_m', 'new_m_out_norm_b': 'new_m', 'new_m_w_o': 'new_m', 'new_m_norm_mix_post': 'new_m', 'new_m_norm_ffn_pre': 'new_m', 'new_m_w_up': 'new_m', 'new_m_conv_w': 'new_m', 'new_m_conv_b': 'new_m', 'new_m_w_down': 'new_m', 'new_m_norm_ffn_post': 'new_m', 'new_v_norm_mix_pre': 'new_v', 'new_v_w_in': 'new_v', 'new_v_q_lat_norm': 'new_v', 'new_v_w_uq': 'new_v', 'new_v_kv_lat_norm': 'new_v', 'new_v_w_ukv': 'new_v', 'new_v_out_norm_a': 'new_v', 'new_v_out_norm_b': 'new_v', 'new_v_w_o': 'new_v', 'new_v_norm_mix_post': 'new_v', 'new_v_norm_ffn_pre': 'new_v', 'new_v_w_up': 'new_v', 'new_v_conv_w': 'new_v', 'new_v_conv_b': 'new_v', 'new_v_w_down': 'new_v', 'new_v_norm_ffn_post': 'new_v'}


def _forward(args):
    return _fwd_reference(*[args[k] for k in FWD_PARAMS])


def _output_shape():
    def fwd():
        inp = _fwd_setup_inputs(0)
        return _fwd_reference(*[inp[k] for k in FWD_PARAMS])
    out = _jax.eval_shape(fwd)
    return out.shape, out.dtype

N_MICROBATCH = 1
ADAM_LR = 0.001
ADAM_B1 = 0.9
ADAM_B2 = 0.999
ADAM_EPS = 1e-08
ADAM_WD = 0.01
ADAM_STEP = 10
PER_EXAMPLE_BATCH_AXIS = {'x': 0, 'loss_target': 0}
SHARED_INPUTS = []
_WEIGHT_DTYPES = {'norm_mix_pre': _jnp.float32, 'w_in': _jnp.float32, 'q_lat_norm': _jnp.float32, 'w_uq': _jnp.float32, 'kv_lat_norm': _jnp.float32, 'w_ukv': _jnp.float32, 'out_norm_a': _jnp.float32, 'out_norm_b': _jnp.float32, 'w_o': _jnp.float32, 'norm_mix_post': _jnp.float32, 'norm_ffn_pre': _jnp.float32, 'w_up': _jnp.float32, 'conv_w': _jnp.float32, 'conv_b': _jnp.float32, 'w_down': _jnp.float32, 'norm_ffn_post': _jnp.float32}
MOMENT_SCALE = {'norm_mix_pre': 1.520927e+00, 'w_in': 9.152531e-01, 'q_lat_norm': 2.035050e+00, 'w_uq': 1.123846e+00, 'kv_lat_norm': 4.616520e+00, 'w_ukv': 1.375804e+00, 'out_norm_a': 1.439061e+00, 'out_norm_b': 1.365481e+00, 'w_o': 1.340465e+00, 'norm_mix_post': 6.418381e+01, 'norm_ffn_pre': 8.850506e-01, 'w_up': 3.918351e-01, 'conv_w': 4.851455e-01, 'conv_b': 1.111107e+00, 'w_down': 8.648561e-01, 'norm_ffn_post': 6.445398e+01}


def _to_microbatches(a, axis):
    t = _jnp.moveaxis(a, axis, 0)
    t = t.reshape((N_MICROBATCH, t.shape[0] // N_MICROBATCH) + t.shape[1:])
    return _jnp.moveaxis(t, 1, axis + 1)


def setup_inputs(seed: int = 0) -> dict:
    inp = _fwd_setup_inputs(seed)
    key = _jax.random.fold_in(_jax.random.key(seed), 7919)
    shape, _ = _output_shape()
    out = dict(inp)
    out["loss_target"] = _jax.random.normal(_jax.random.fold_in(key, 0), shape, _jnp.float32)
    for i, name in enumerate(TWIN_WEIGHTS):
        w = inp[name].astype(_jnp.float32)
        if MOMENT_SCALE is None:
            s = _jnp.sqrt(_jnp.mean(_jnp.square(w)) + 1e-30)
        else:
            s = MOMENT_SCALE[name]
        km, kv = _jax.random.split(_jax.random.fold_in(key, i + 1))
        out[name] = w
        out["m_" + name] = s * _jax.random.normal(km, w.shape, _jnp.float32)
        out["v_" + name] = (s * s) * _jax.random.uniform(kv, w.shape, _jnp.float32, 0.5, 1.5)
    if N_MICROBATCH > 1:
        for name, axis in PER_EXAMPLE_BATCH_AXIS.items():
            out[name] = _to_microbatches(out[name], axis)
    return {'x': out['x'], 'norm_mix_pre': out['norm_mix_pre'], 'w_in': out['w_in'], 'q_lat_norm': out['q_lat_norm'], 'w_uq': out['w_uq'], 'kv_lat_norm': out['kv_lat_norm'], 'w_ukv': out['w_ukv'], 'out_norm_a': out['out_norm_a'], 'out_norm_b': out['out_norm_b'], 'w_o': out['w_o'], 'norm_mix_post': out['norm_mix_post'], 'norm_ffn_pre': out['norm_ffn_pre'], 'w_up': out['w_up'], 'conv_w': out['conv_w'], 'conv_b': out['conv_b'], 'w_down': out['w_down'], 'norm_ffn_post': out['norm_ffn_post'], 'loss_target': out['loss_target'], 'm_norm_mix_pre': out['m_norm_mix_pre'], 'm_w_in': out['m_w_in'], 'm_q_lat_norm': out['m_q_lat_norm'], 'm_w_uq': out['m_w_uq'], 'm_kv_lat_norm': out['m_kv_lat_norm'], 'm_w_ukv': out['m_w_ukv'], 'm_out_norm_a': out['m_out_norm_a'], 'm_out_norm_b': out['m_out_norm_b'], 'm_w_o': out['m_w_o'], 'm_norm_mix_post': out['m_norm_mix_post'], 'm_norm_ffn_pre': out['m_norm_ffn_pre'], 'm_w_up': out['m_w_up'], 'm_conv_w': out['m_conv_w'], 'm_conv_b': out['m_conv_b'], 'm_w_down': out['m_w_down'], 'm_norm_ffn_post': out['m_norm_ffn_post'], 'v_norm_mix_pre': out['v_norm_mix_pre'], 'v_w_in': out['v_w_in'], 'v_q_lat_norm': out['v_q_lat_norm'], 'v_w_uq': out['v_w_uq'], 'v_kv_lat_norm': out['v_kv_lat_norm'], 'v_w_ukv': out['v_w_ukv'], 'v_out_norm_a': out['v_out_norm_a'], 'v_out_norm_b': out['v_out_norm_b'], 'v_w_o': out['v_w_o'], 'v_norm_mix_post': out['v_norm_mix_post'], 'v_norm_ffn_pre': out['v_norm_ffn_pre'], 'v_w_up': out['v_w_up'], 'v_conv_w': out['v_conv_w'], 'v_conv_b': out['v_conv_b'], 'v_w_down': out['v_w_down'], 'v_norm_ffn_post': out['v_norm_ffn_post']}


def _loss(weights, diff, rest, loss_target):
    with _jax.named_scope("forward"):
        args = {**rest, TWIN_DIFF_INPUT: diff, **{k: w.astype(_WEIGHT_DTYPES[k]) for k, w in weights.items()}}
        y = _forward(args)
    with _jax.named_scope("loss_head"):
        err = _jnp.square(y.astype(_jnp.float32) - loss_target)
        return 0.5 * _jnp.sum(_jnp.mean(err, axis=-1)) if err.ndim else 0.5 * err


def _adamw(w, g, m, v):
    m = ADAM_B1 * m + (1.0 - ADAM_B1) * g
    v = ADAM_B2 * v + (1.0 - ADAM_B2) * _jnp.square(g)
    m_hat = m / (1.0 - ADAM_B1 ** ADAM_STEP)
    v_hat = v / (1.0 - ADAM_B2 ** ADAM_STEP)
    delta = -ADAM_LR * (m_hat / (_jnp.sqrt(v_hat) + ADAM_EPS) + ADAM_WD * w)
    return delta, m, v


def reference(x, norm_mix_pre, w_in, q_lat_norm, w_uq, kv_lat_norm, w_ukv, out_norm_a, out_norm_b, w_o, norm_mix_post, norm_ffn_pre, w_up, conv_w, conv_b, w_down, norm_ffn_post, loss_target, m_norm_mix_pre, m_w_in, m_q_lat_norm, m_w_uq, m_kv_lat_norm, m_w_ukv, m_out_norm_a, m_out_norm_b, m_w_o, m_norm_mix_post, m_norm_ffn_pre, m_w_up, m_conv_w, m_conv_b, m_w_down, m_norm_ffn_post, v_norm_mix_pre, v_w_in, v_q_lat_norm, v_w_uq, v_kv_lat_norm, v_w_ukv, v_out_norm_a, v_out_norm_b, v_w_o, v_norm_mix_post, v_norm_ffn_pre, v_w_up, v_conv_w, v_conv_b, v_w_down, v_norm_ffn_post):
    given = dict(x=x, norm_mix_pre=norm_mix_pre, w_in=w_in, q_lat_norm=q_lat_norm, w_uq=w_uq, kv_lat_norm=kv_lat_norm, w_ukv=w_ukv, out_norm_a=out_norm_a, out_norm_b=out_norm_b, w_o=w_o, norm_mix_post=norm_mix_post, norm_ffn_pre=norm_ffn_pre, w_up=w_up, conv_w=conv_w, conv_b=conv_b, w_down=w_down, norm_ffn_post=norm_ffn_post, loss_target=loss_target, m_norm_mix_pre=m_norm_mix_pre, m_w_in=m_w_in, m_q_lat_norm=m_q_lat_norm, m_w_uq=m_w_uq, m_kv_lat_norm=m_kv_lat_norm, m_w_ukv=m_w_ukv, m_out_norm_a=m_out_norm_a, m_out_norm_b=m_out_norm_b, m_w_o=m_w_o, m_norm_mix_post=m_norm_mix_post, m_norm_ffn_pre=m_norm_ffn_pre, m_w_up=m_w_up, m_conv_w=m_conv_w, m_conv_b=m_conv_b, m_w_down=m_w_down, m_norm_ffn_post=m_norm_ffn_post, v_norm_mix_pre=v_norm_mix_pre, v_w_in=v_w_in, v_q_lat_norm=v_q_lat_norm, v_w_uq=v_w_uq, v_kv_lat_norm=v_kv_lat_norm, v_w_ukv=v_w_ukv, v_out_norm_a=v_out_norm_a, v_out_norm_b=v_out_norm_b, v_w_o=v_w_o, v_norm_mix_post=v_norm_mix_post, v_norm_ffn_pre=v_norm_ffn_pre, v_w_up=v_w_up, v_conv_w=v_conv_w, v_conv_b=v_conv_b, v_w_down=v_w_down, v_norm_ffn_post=v_norm_ffn_post)
    weights = {n: given[n] for n in TWIN_WEIGHTS}
    shared = {n: given[n] for n in SHARED_INPUTS}
    per_example = {n: given[n] for n in ['x']}
    grad_fn = _jax.value_and_grad(_loss, argnums=(0, 1))

    def one_microbatch(ex, loss_target):
        ex = dict(ex)
        diff = ex.pop(TWIN_DIFF_INPUT)
        return grad_fn(weights, diff, {**shared, **ex}, loss_target)

    if N_MICROBATCH == 1:
        loss, (grad_w, grad_x) = one_microbatch(per_example, given["loss_target"])
    else:
        def body(carry, xs):
            loss_sum, grad_sum = carry
            l_k, (gw_k, gx_k) = one_microbatch(xs[0], xs[1])
            with _jax.named_scope("update"):
                return (loss_sum + l_k, _jax.tree.map(_jnp.add, grad_sum, gw_k)), gx_k

        init = (_jnp.zeros((), _jnp.float32), _jax.tree.map(_jnp.zeros_like, weights))
        (loss, grad_w), grad_x = _jax.lax.scan(body, init, (per_example, given["loss_target"]))
    with _jax.named_scope("update"):
        delta_w, new_m, new_v = {}, {}, {}
        for n in TWIN_WEIGHTS:
            delta_w[n], new_m[n], new_v[n] = _adamw(weights[n], grad_w[n], given["m_" + n], given["v_" + n])
    return (loss, grad_x, *[grad_w[n] for n in TWIN_WEIGHTS], *[delta_w[n] for n in TWIN_WEIGHTS],
            *[new_m[n] for n in TWIN_WEIGHTS], *[new_v[n] for n in TWIN_WEIGHTS])
```

```python
import functools
import math

import numpy as np
import jax
import jax.numpy as jnp
from jax import lax
from jax.experimental import pallas as pl
from jax.experimental.pallas import tpu as pltpu

F32 = jnp.float32
BF16 = jnp.bfloat16

D_MODEL = 1024
N_DEV = 8
WIDTH_A = 512
N_HEADS = 8
Q_LORA = 384
KV_LORA = 256
QK_ROPE = 32
QK_NOPE = 64
D_FF = 2816
DIL_CONFIGS = ((128, 1), (512, 4), (2048, 16))
BAND_HALF = 64
ROPE_BASE = 10000.0
EPS = 1e-6
NEG = -1e30
MLA_SCALE = (QK_NOPE + QK_ROPE) ** -0.5
SCALE_A = 0.125

COL_CQ = 3 * WIDTH_A
COL_CKV = COL_CQ + Q_LORA
COL_KR = COL_CKV + KV_LORA
N_PROJ = COL_KR + 128
N_LAT = N_PROJ - COL_CQ
D_IN = COL_KR + QK_ROPE

ADAM_LR = 0.001
ADAM_B1 = 0.9
ADAM_B2 = 0.999
ADAM_EPS = 1e-08
ADAM_WD = 0.01
ADAM_STEP = 10

LANES = 128
VMEM_LIMIT = 56 * 1024 * 1024

SHARDED = ("w_in", "w_uq", "w_ukv", "w_o", "w_up", "conv_w", "w_down")
REPLICATED = ("norm_mix_pre", "q_lat_norm", "kv_lat_norm", "out_norm_a", "out_norm_b", "norm_mix_post",
              "norm_ffn_pre", "conv_b", "norm_ffn_post")
WEIGHTS = ("norm_mix_pre", "w_in", "q_lat_norm", "w_uq", "kv_lat_norm", "w_ukv", "out_norm_a", "out_norm_b", "w_o",
           "norm_mix_post", "norm_ffn_pre", "w_up", "conv_w", "conv_b", "w_down", "norm_ffn_post")


def _params(sem=None):
    return pltpu.CompilerParams(dimension_semantics=sem, vmem_limit_bytes=VMEM_LIMIT)


def _dot(a, b, dims):
    return lax.dot_general(a, b, (dims, ((), ())), preferred_element_type=F32)


NN = ((1,), (0,))
NT = ((1,), (1,))
TN = ((0,), (0,))


def _rstd(x):
    return lax.rsqrt(jnp.mean(x * x, axis=-1, keepdims=True) + EPS)


def _rms_bwd(dy, x, g):
    r = _rstd(x)
    z = x * r
    gy = dy * g
    dx = r * (gy - z * jnp.mean(gy * z, axis=-1, keepdims=True))
    return dx, dy * z


def _split_hi_lo(v):
    hi = v.astype(BF16)
    lo = (v - hi.astype(F32)).astype(BF16)
    return hi, lo


def _matmul(a, b, mode, out_dtype, tm, tn, tk, name):
    if mode == "nn":
        (m, k), n = a.shape, b.shape[1]
        a_spec = pl.BlockSpec((tm, tk), lambda i, j, s: (i, s))
        b_spec = pl.BlockSpec((tk, tn), lambda i, j, s: (s, j))
        dims = NN
    elif mode == "nt":
        (m, k), n = a.shape, b.shape[0]
        a_spec = pl.BlockSpec((tm, tk), lambda i, j, s: (i, s))
        b_spec = pl.BlockSpec((tn, tk), lambda i, j, s: (j, s))
        dims = NT
    else:
        (k, m), n = a.shape, b.shape[1]
        a_spec = pl.BlockSpec((tk, tm), lambda i, j, s: (s, i))
        b_spec = pl.BlockSpec((tk, tn), lambda i, j, s: (s, j))
        dims = TN
    assert m % tm == 0 and n % tn == 0 and k % tk == 0, (name, m, n, k, tm, tn, tk)
    nk = k // tk

    def body(a_ref, b_ref, o_ref, acc_ref):
        s = pl.program_id(2)

        @pl.when(s == 0)
        def _():
            acc_ref[...] = jnp.zeros_like(acc_ref)

        acc_ref[...] += _dot(a_ref[...].astype(BF16), b_ref[...].astype(BF16), dims)

        @pl.when(s == nk - 1)
        def _():
            o_ref[...] = acc_ref[...].astype(out_dtype)

    return pl.pallas_call(
        body, name=name, grid=(m // tm, n // tn, nk),
        in_specs=[a_spec, b_spec],
        out_specs=pl.BlockSpec((tm, tn), lambda i, j, s: (i, j)),
        out_shape=jax.ShapeDtypeStruct((m, n), out_dtype),
        scratch_shapes=[pltpu.VMEM((tm, tn), F32)],
        compiler_params=_params(("parallel", "parallel", "arbitrary")),
    )(a, b)


def _norm_matmul(x, g, w, tn, name):
    s, k = x.shape
    n = w.shape[1]
    tm = min(512, s)
    assert n % tn == 0

    def body(x_ref, g_ref, w_ref, o_ref, h_ref):
        @pl.when(pl.program_id(1) == 0)
        def _():
            xv = x_ref[...]
            h_ref[...] = (xv * _rstd(xv) * g_ref[...]).astype(BF16)

        o_ref[...] = _dot(h_ref[...], w_ref[...], NN).astype(BF16)

    return pl.pallas_call(
        body, name=name, grid=(s // tm, n // tn),
        in_specs=[pl.BlockSpec((tm, k), lambda i, j: (i, 0)),
                  pl.BlockSpec((1, k), lambda i, j: (0, 0)),
                  pl.BlockSpec((k, tn), lambda i, j: (0, j))],
        out_specs=[pl.BlockSpec((tm, tn), lambda i, j: (i, j)),
                   pl.BlockSpec((tm, k), lambda i, j: (i, 0))],
        out_shape=[jax.ShapeDtypeStruct((s, n), BF16), jax.ShapeDtypeStruct((s, k), BF16)],
        compiler_params=_params(("parallel", "arbitrary")),
    )(x, g, w)


def _band_bias(r):
    off = np.arange(256)[None, :] - BAND_HALF - np.arange(128)[:, None]
    slopes = np.exp2(-8.0 * np.arange(1, N_HEADS + 1, dtype=np.float32) / N_HEADS).astype(np.float32)
    dist = (np.abs(off) * r).astype(np.float32)
    bias = -slopes[:, None, None] * dist[None]
    bias = np.where((np.abs(off) <= BAND_HALF)[None], bias, np.float32(NEG))
    return jnp.asarray(bias, F32)


def _band_call(mode, r, center, window, bias, name):
    s = center[0].shape[0]
    seq = s // r
    tq = min(512, seq)
    nsub = tq // 128
    hb = tq // BAND_HALF
    nh = seq // BAND_HALF
    nc, nw = len(center), len(window)
    n_out = {"fwd": 2, "dq": 1, "dkv": 2}[mode]
    cview = [a.reshape(seq, r * WIDTH_A) for a in center]
    wview = [a.reshape(seq, r * WIDTH_A) for a in window]

    cspec = pl.BlockSpec((tq, WIDTH_A), lambda c, i: (i, c))
    pspec = pl.BlockSpec((BAND_HALF, WIDTH_A), lambda c, i: (jnp.maximum(i * hb - 1, 0), c))
    nspec = pl.BlockSpec((BAND_HALF, WIDTH_A), lambda c, i: (jnp.minimum((i + 1) * hb, nh - 1), c))
    in_specs = [cspec] * nc
    operands = list(cview)
    for w in wview:
        in_specs += [pspec, cspec, nspec]
        operands += [w, w, w]
    in_specs.append(pl.BlockSpec((N_HEADS, 128, 256), lambda c, i: (0, 0, 0)))
    operands.append(bias)

    def aug_stat(base, stat_sw, lane, act, e0):
        hi, lo = _split_hi_lo(stat_sw)
        return jnp.where(act, base, jnp.where(lane == e0, -hi, jnp.where(lane == e0 + 1, -lo, jnp.zeros_like(hi))))

    def aug_ones(base, lane, e0):
        return jnp.where((lane == e0) | (lane == e0 + 1), jnp.ones_like(base), base)

    def body(*refs):
        c_refs = refs[:nc]
        w_refs = refs[nc:nc + 3 * nw]
        bias_ref = refs[nc + 3 * nw]
        o_refs = refs[nc + 3 * nw + 1:nc + 3 * nw + 1 + n_out]
        wins = refs[nc + 3 * nw + 1 + n_out:]
        i = pl.program_id(1)
        for t in range(nw):
            wins[t][0:BAND_HALF, :] = w_refs[3 * t][...]
            wins[t][BAND_HALF:BAND_HALF + tq, :] = w_refs[3 * t + 1][...]
            wins[t][BAND_HALF + tq:BAND_HALF + tq + BAND_HALF, :] = w_refs[3 * t + 2][...]

        def sub(j, carry):
            r0 = pl.multiple_of(j * 128, 128)
            wpos = i * tq + j * 128 - BAND_HALF + lax.broadcasted_iota(jnp.int32, (128, 256), 1)
            valid = (wpos >= 0) & (wpos < seq)
            for p in range(4):
                cols = slice(p * 128, (p + 1) * 128)
                cs = [c[pl.ds(r0, 128), cols] for c in c_refs]
                ws = [w[pl.ds(r0, 256), cols] for w in wins]
                res = []
                for a in range(2):
                    e0 = 64 if a == 0 else 0
                    lane_c = lax.broadcasted_iota(jnp.int32, (128, 128), 1)
                    lane_w = lax.broadcasted_iota(jnp.int32, (256, 128), 1)
                    act_c = (lane_c < 64) if a == 0 else (lane_c >= 64)
                    act_w = (lane_w < 64) if a == 0 else (lane_w >= 64)
                    bias_a = bias_ref[2 * p + a]
                    if mode == "fwd":
                        qa = jnp.where(act_c, cs[0] * SCALE_A, jnp.zeros_like(cs[0]))
                        sc = _dot(qa, ws[0], NT) + bias_a
                        sc = jnp.where(valid, sc, NEG)
                        m = jnp.max(sc, axis=-1, keepdims=True)
                        e = jnp.exp(sc - m)
                        l = jnp.sum(e, axis=-1, keepdims=True)
                        o = _dot(e.astype(BF16), ws[1], NN) / l
                        res.append((o, jnp.broadcast_to(m + jnp.log(l), (128, 128))))
                    elif mode == "dq":
                        q2, dy2, l2, d2 = cs
                        k2, v2 = ws
                        q_aug = aug_stat(q2 * SCALE_A, pltpu.roll(l2, 64, 1), lane_c, act_c, e0)
                        dy_aug = aug_stat(dy2, pltpu.roll(d2, 64, 1), lane_c, act_c, e0)
                        k_aug = aug_ones(k2, lane_w, e0)
                        v_aug = aug_ones(v2, lane_w, e0)
                        sc = _dot(q_aug, k_aug, NT) + bias_a
                        pr = jnp.exp(jnp.where(valid, sc, NEG))
                        ds = pr * _dot(dy_aug, v_aug, NT)
                        res.append((_dot(ds.astype(BF16), k2, NN) * SCALE_A,))
                    else:
                        k2, v2 = cs
                        q2, dy2, l2, d2 = ws
                        q_aug = aug_stat(q2 * SCALE_A, pltpu.roll(l2, 64, 1), lane_w, act_w, e0)
                        dy_aug = aug_stat(dy2, pltpu.roll(d2, 64, 1), lane_w, act_w, e0)
                        k_aug = aug_ones(k2, lane_c, e0)
                        v_aug = aug_ones(v2, lane_c, e0)
                        sc = _dot(k_aug, q_aug, NT) + bias_a
                        pr = jnp.exp(jnp.where(valid, sc, NEG))
                        ds = pr * _dot(v_aug, dy_aug, NT)
                        res.append((_dot(ds.astype(BF16), q_aug, NN), _dot(pr.astype(BF16), dy_aug, NN)))
                lane_c = lax.broadcasted_iota(jnp.int32, (128, 128), 1)
                for t in range(n_out):
                    o_refs[t][pl.ds(r0, 128), cols] = jnp.where(lane_c < 64, res[0][t], res[1][t])
            return carry

        lax.fori_loop(0, nsub, sub, 0)

    outs = pl.pallas_call(
        body, name=name, grid=(r, seq // tq),
        in_specs=in_specs,
        out_specs=[cspec] * n_out,
        out_shape=[jax.ShapeDtypeStruct((seq, r * WIDTH_A), F32)] * n_out,
        scratch_shapes=[pltpu.VMEM((tq + 2 * BAND_HALF, WIDTH_A), w.dtype) for w in window],
        compiler_params=_params(("parallel", "parallel")),
    )(*operands)
    return [o.reshape(s, WIDTH_A) for o in outs]


def _band_combine(os_, lses, name):
    s = os_[0].shape[0]
    tm = min(512, s)

    def body(o0, o1, o2, l0, l1, l2, ya_ref, lse_ref):
        a0, a1, a2 = l0[...], l1[...], l2[...]
        m = jnp.maximum(jnp.maximum(a0, a1), a2)
        e0, e1, e2 = jnp.exp(a0 - m), jnp.exp(a1 - m), jnp.exp(a2 - m)
        den = e0 + e1 + e2
        ya_ref[...] = (e0 * o0[...] + e1 * o1[...] + e2 * o2[...]) / den
        lse_ref[...] = m + jnp.log(den)

    spec = pl.BlockSpec((tm, WIDTH_A), lambda i: (i, 0))
    return pl.pallas_call(
        body, name=name, grid=(s // tm,), in_specs=[spec] * 6, out_specs=[spec] * 2,
        out_shape=[jax.ShapeDtypeStruct((s, WIDTH_A), F32)] * 2,
        compiler_params=_params(("parallel",)),
    )(*os_, *lses)


def _rope_tables(s):
    pos = jnp.arange(s, dtype=F32)
    inv_freq = jnp.exp(-math.log(ROPE_BASE) * jnp.arange(0, QK_ROPE, 2, dtype=F32) / QK_ROPE)
    ang = pos[:, None] * inv_freq[None, :]
    cos, sin = jnp.cos(ang), jnp.sin(ang)
    one = jnp.ones((s, 64), F32)
    zero16 = jnp.zeros((s, 16), F32)
    c = jnp.concatenate([one, cos, cos, jnp.ones((s, 32), F32)], axis=1)
    sa = jnp.concatenate([jnp.zeros((s, 64), F32), -sin, zero16, jnp.zeros((s, 32), F32)], axis=1)
    sb = jnp.concatenate([jnp.zeros((s, 64), F32), zero16, sin, jnp.zeros((s, 32), F32)], axis=1)
    return c, sa, sb


def _rope_fwd(x, c, sa, sb):
    return x * c + pltpu.roll(x, 112, 1) * sa + pltpu.roll(x, 16, 1) * sb


def _rope_bwd(dy, c, sa, sb):
    return dy * c + pltpu.roll(dy * sa, 16, 1) + pltpu.roll(dy * sb, 112, 1)


def _mla_prep(proj, g_q, g_kv, w_uq, w_ukv, tabs, name):
    s = proj.shape[0]
    tm = min(512, s)
    width = N_HEADS * 128

    def body(lat_ref, gq_ref, gkv_ref, wq_ref, wkv_ref, c_ref, sa_ref, sb_ref,
             q_ref, k_ref, kv_ref, cqn_ref, ckvn_ref):
        c, sa, sb = c_ref[...], sa_ref[...], sb_ref[...]
        cq = lat_ref[:, 0:Q_LORA].astype(F32)
        cqn = (cq * _rstd(cq) * gq_ref[...]).astype(BF16)
        cqn_ref[...] = cqn
        q = _dot(cqn, wq_ref[...], NN)
        ckv = lat_ref[:, Q_LORA:Q_LORA + KV_LORA].astype(F32)
        ckvn = (ckv * _rstd(ckv) * gkv_ref[...]).astype(BF16)
        ckvn_ref[...] = ckvn
        kv = _dot(ckvn, wkv_ref[...], NN)
        kv_ref[...] = kv.astype(BF16)
        krr = _rope_fwd(lat_ref[:, Q_LORA + KV_LORA:].astype(F32), c, sa, sb)
        lane = lax.broadcasted_iota(jnp.int32, (tm, 128), 1)
        for h in range(N_HEADS):
            cols = slice(h * 128, (h + 1) * 128)
            q_ref[:, cols] = (_rope_fwd(q[:, cols], c, sa, sb) * MLA_SCALE).astype(BF16)
            k_ref[:, cols] = jnp.where(lane < 64, kv[:, cols], krr).astype(BF16)

    row = lambda n: pl.BlockSpec((tm, n), lambda i: (i, 0))
    full = lambda a: pl.BlockSpec(a.shape, lambda i: (0, 0))
    tab = pl.BlockSpec((tm, 128), lambda i: (i, 0))
    return pl.pallas_call(
        body, name=name, grid=(s // tm,),
        in_specs=[pl.BlockSpec((tm, N_LAT), lambda i: (i, COL_CQ // N_LAT)),
                  full(g_q), full(g_kv), full(w_uq), full(w_ukv), tab, tab, tab],
        out_specs=[row(width), row(width), row(width), row(Q_LORA), row(KV_LORA)],
        out_shape=[jax.ShapeDtypeStruct((s, width), BF16)] * 3
        + [jax.ShapeDtypeStruct((s, Q_LORA), BF16), jax.ShapeDtypeStruct((s, KV_LORA), BF16)],
        compiler_params=_params(("parallel",)),
    )(proj, g_q, g_kv, w_uq, w_ukv, *tabs)


def _mla_fwd(qcat, kcat, kvb, name):
    s = qcat.shape[0]
    tq = min(512, s)
    tk = min(512, s)
    nkc = s // tk

    def body(q_ref, k_ref, kv_ref, yb_ref, qaug_ref, m_ref, l_ref, acc_ref):
        lane = lax.broadcasted_iota(jnp.int32, (tq, 128), 1)
        outs = []
        for a in range(2):
            cols = slice(a * 128, (a + 1) * 128)
            qh = q_ref[:, cols]
            m_ref[...] = jnp.full((tq, 128), NEG, F32)
            l_ref[...] = jnp.zeros((tq, 128), F32)
            acc_ref[...] = jnp.zeros((tq, 128), F32)

            def chunk(cidx, carry):
                k0 = pl.multiple_of(cidx * tk, tk)
                sc = _dot(qh, k_ref[pl.ds(k0, tk), cols], NT)
                m_prev = m_ref[...]
                m_new = jnp.maximum(m_prev, jnp.max(sc, axis=-1, keepdims=True))
                alpha = jnp.exp(m_prev - m_new)
                pr = jnp.exp(sc - jnp.tile(m_new, (1, tk // 128)))
                l_ref[...] = alpha * l_ref[...] + jnp.sum(pr, axis=-1, keepdims=True)
                acc_ref[...] = alpha * acc_ref[...] + _dot(pr.astype(BF16), kv_ref[pl.ds(k0, tk), cols], NN)
                m_ref[...] = m_new
                return carry

            lax.fori_loop(0, nkc, chunk, 0)
            l = l_ref[...]
            outs.append(acc_ref[...] / l)
            hi, lo = _split_hi_lo(m_ref[...] + jnp.log(l))
            qaug_ref[:, cols] = jnp.where(lane == 96, -hi, jnp.where(lane == 97, -lo, qh))
        yb_ref[...] = jnp.where(lane < 64, pltpu.roll(outs[0], 64, 1), outs[1])

    return pl.pallas_call(
        body, name=name, grid=(4, s // tq),
        in_specs=[pl.BlockSpec((tq, 256), lambda p, i: (i, p)),
                  pl.BlockSpec((s, 256), lambda p, i: (0, p)),
                  pl.BlockSpec((s, 256), lambda p, i: (0, p))],
        out_specs=[pl.BlockSpec((tq, 128), lambda p, i: (i, p)),
                   pl.BlockSpec((tq, 256), lambda p, i: (i, p))],
        out_shape=[jax.ShapeDtypeStruct((s, WIDTH_A), F32), jax.ShapeDtypeStruct((s, N_HEADS * 128), BF16)],
        scratch_shapes=[pltpu.VMEM((tq, 128), F32)] * 3,
        compiler_params=_params(("parallel", "parallel")),
    )(qcat, kcat, kvb)


def _mla_bwd(qaug, kcat, kvb, doaug, name):
    s = qaug.shape[0]
    tq = min(512, s)
    tk = min(512, s)
    nqc = s // tq
    width = N_HEADS * 128

    def body(q_ref, do_ref, k_ref, kv_ref, dq_ref, dk_ref, dv_ref, dk_acc, dv_acc):
        j = pl.program_id(1)

        @pl.when(j == 0)
        def _():
            dq_ref[...] = jnp.zeros_like(dq_ref)

        lane = lax.broadcasted_iota(jnp.int32, (tk, 128), 1)
        kb = k_ref[...]
        k_aug = jnp.where((lane == 96) | (lane == 97), jnp.ones_like(kb), kb)
        kvv = kv_ref[...]
        v_aug = jnp.where(lane >= 64, kvv, jnp.where(lane < 2, jnp.ones_like(kvv), jnp.zeros_like(kvv)))
        dk_acc[...] = jnp.zeros_like(dk_acc)
        dv_acc[...] = jnp.zeros_like(dv_acc)

        def chunk(cidx, carry):
            q0 = pl.multiple_of(cidx * tq, tq)
            qi = q_ref[pl.ds(q0, tq), :]
            doi = do_ref[pl.ds(q0, tq), :]
            pt = jnp.exp(_dot(k_aug, qi, NT))
            dst = (pt * _dot(v_aug, doi, NT)).astype(BF16)
            dv_acc[...] += _dot(pt.astype(BF16), doi, NN)
            dk_acc[...] += _dot(dst, qi, NN)
            dq_ref[pl.ds(q0, tq), :] += _dot(dst, kb, TN)
            return carry

        lax.fori_loop(0, nqc, chunk, 0)
        dk_ref[...] = dk_acc[...]
        dv_ref[...] = dv_acc[...]

    return pl.pallas_call(
        body, name=name, grid=(N_HEADS, s // tk),
        in_specs=[pl.BlockSpec((s, 128), lambda h, j: (0, h)),
                  pl.BlockSpec((s, 128), lambda h, j: (0, h)),
                  pl.BlockSpec((tk, 128), lambda h, j: (j, h)),
                  pl.BlockSpec((tk, 128), lambda h, j: (j, h))],
        out_specs=[pl.BlockSpec((s, 128), lambda h, j: (0, h)),
                   pl.BlockSpec((tk, 128), lambda h, j: (j, h)),
                   pl.BlockSpec((tk, 128), lambda h, j: (j, h))],
        out_shape=[jax.ShapeDtypeStruct((s, width), F32)] * 3,
        scratch_shapes=[pltpu.VMEM((tk, 128), F32)] * 2,
        compiler_params=_params(("parallel", "arbitrary")),
    )(qaug, doaug, kcat, kvb)


def _mla_prep_bwd(dqc, dkc, dvp, proj, cqn, ckvn, g_q, g_kv, w_uq, w_ukv, tabs, name):
    s = proj.shape[0]
    tm = min(256, s)
    width = N_HEADS * 128
    n_out_cols = N_LAT

    def body(dq_ref, dk_ref, dv_ref, lat_ref, cqn_ref, ckvn_ref, gq_ref, gkv_ref, wq_ref, wkv_ref,
             c_ref, sa_ref, sb_ref, dproj_ref, dwq_ref, dwkv_ref, dgq_ref, dgkv_ref):
        i = pl.program_id(0)

        @pl.when(i == 0)
        def _():
            dwq_ref[...] = jnp.zeros_like(dwq_ref)
            dwkv_ref[...] = jnp.zeros_like(dwkv_ref)
            dgq_ref[...] = jnp.zeros_like(dgq_ref)
            dgkv_ref[...] = jnp.zeros_like(dgkv_ref)

        c, sa, sb = c_ref[...], sa_ref[...], sb_ref[...]
        lane = lax.broadcasted_iota(jnp.int32, (tm, 128), 1)
        dkr = jnp.zeros((tm, 128), F32)
        dq_parts, dkv_parts = [], []
        for h in range(N_HEADS):
            cols = slice(h * 128, (h + 1) * 128)
            dq_parts.append(_rope_bwd(dq_ref[:, cols] * MLA_SCALE, c, sa, sb).astype(BF16))
            dkh = dk_ref[:, cols]
            dkr = dkr + dkh
            dkv_parts.append(jnp.where(lane < 64, dkh, dv_ref[:, cols]).astype(BF16))
        dq = jnp.concatenate(dq_parts, axis=1)
        dkv = jnp.concatenate(dkv_parts, axis=1)
        dkr = _rope_bwd(jnp.where((lane >= 64) & (lane < 96), dkr, 0.0), c, sa, sb)

        dcqn = _dot(dq, wq_ref[...], NT)
        dwq_ref[...] += _dot(cqn_ref[...], dq, TN)
        dcq, dgq = _rms_bwd(dcqn, lat_ref[:, 0:Q_LORA].astype(F32), gq_ref[...])
        dgq_ref[...] += jnp.sum(dgq, axis=0, keepdims=True)

        dckvn = _dot(dkv, wkv_ref[...], NT)
        dwkv_ref[...] += _dot(ckvn_ref[...], dkv, TN)
        dckv, dgkv = _rms_bwd(dckvn, lat_ref[:, Q_LORA:Q_LORA + KV_LORA].astype(F32), gkv_ref[...])
        dgkv_ref[...] += jnp.sum(dgkv, axis=0, keepdims=True)

        dproj_ref[:, 0:Q_LORA] = dcq.astype(BF16)
        dproj_ref[:, Q_LORA:Q_LORA + KV_LORA] = dckv.astype(BF16)
        dproj_ref[:, Q_LORA + KV_LORA:] = dkr.astype(BF16)

    row = lambda n: pl.BlockSpec((tm, n), lambda i: (i, 0))
    full = lambda a: pl.BlockSpec(a.shape, lambda i: (0, 0))
    tab = pl.BlockSpec((tm, 128), lambda i: (i, 0))
    return pl.pallas_call(
        body, name=name, grid=(s // tm,),
        in_specs=[row(width), row(width), row(width),
                  pl.BlockSpec((tm, N_LAT), lambda i: (i, COL_CQ // N_LAT)),
                  row(Q_LORA), row(KV_LORA), full(g_q), full(g_kv), full(w_uq), full(w_ukv), tab, tab, tab],
        out_specs=[row(n_out_cols), full(w_uq), full(w_ukv), full(g_q), full(g_kv)],
        out_shape=[jax.ShapeDtypeStruct((s, n_out_cols), BF16),
                   jax.ShapeDtypeStruct(w_uq.shape, F32), jax.ShapeDtypeStruct(w_ukv.shape, F32),
                   jax.ShapeDtypeStruct(g_q.shape, F32), jax.ShapeDtypeStruct(g_kv.shape, F32)],
        compiler_params=_params(("arbitrary",)),
    )(dqc, dkc, dvp, proj, cqn, ckvn, g_q, g_kv, w_uq, w_ukv, *tabs)


def _mix_out(ya, yb, na, nb, w_o, g_post, x, name):
    s = x.shape[0]
    tm = min(512, s)

    def body(ya_ref, yb_ref, na_ref, nb_ref, w_ref, g_ref, x_ref, yn_ref, y2_ref, x1_ref):
        a, b = ya_ref[...], yb_ref[...]
        yn = jnp.concatenate([a * _rstd(a) * na_ref[...], b * _rstd(b) * nb_ref[...]], axis=1).astype(BF16)
        yn_ref[...] = yn
        y2 = _dot(yn, w_ref[...], NN)
        y2_ref[...] = y2
        x1_ref[...] = x_ref[...] + y2 * _rstd(y2) * g_ref[...]

    row = lambda n: pl.BlockSpec((tm, n), lambda i: (i, 0))
    full = lambda a: pl.BlockSpec(a.shape, lambda i: (0, 0))
    return pl.pallas_call(
        body, name=name, grid=(s // tm,),
        in_specs=[row(WIDTH_A), row(WIDTH_A), full(na), full(nb), full(w_o), full(g_post), row(D_MODEL)],
        out_specs=[row(D_MODEL)] * 3,
        out_shape=[jax.ShapeDtypeStruct((s, D_MODEL), BF16), jax.ShapeDtypeStruct((s, D_MODEL), F32),
                   jax.ShapeDtypeStruct((s, D_MODEL), F32)],
        compiler_params=_params(("parallel",)),
    )(ya, yb, na, nb, w_o, g_post, x)


def _head_ones():
    blk = np.kron(np.eye(N_HEADS, dtype=np.float32), np.ones((64, 64), np.float32))
    return jnp.asarray(blk, F32)


def _outnorm_bwd(dyn, ya, yb, na, nb, ones, name):
    s = ya.shape[0]
    tm = min(256, s)

    def body(dyn_ref, ya_ref, yb_ref, na_ref, nb_ref, ones_ref, dya_ref, da_ref, do_ref, dna_ref, dnb_ref):
        i = pl.program_id(0)

        @pl.when(i == 0)
        def _():
            dna_ref[...] = jnp.zeros_like(dna_ref)
            dnb_ref[...] = jnp.zeros_like(dnb_ref)

        a, b = ya_ref[...], yb_ref[...]
        dya, dna = _rms_bwd(dyn_ref[:, 0:WIDTH_A], a, na_ref[...])
        dyb, dnb = _rms_bwd(dyn_ref[:, WIDTH_A:], b, nb_ref[...])
        dna_ref[...] += jnp.sum(dna, axis=0, keepdims=True)
        dnb_ref[...] += jnp.sum(dnb, axis=0, keepdims=True)
        dya_b = dya.astype(BF16)
        dya_ref[...] = dya_b
        hp = lax.Precision.HIGHEST
        da_ref[...] = jnp.dot(dya_b.astype(F32) * a, ones_ref[...], precision=hp, preferred_element_type=F32)
        dyb_b = dyb.astype(BF16)
        db = jnp.dot(dyb_b.astype(F32) * b, ones_ref[...], precision=hp, preferred_element_type=F32)
        lane = lax.broadcasted_iota(jnp.int32, (tm, 128), 1)
        zero = jnp.zeros((tm, 128), BF16)
        for p in range(4):
            cols = slice(p * 128, (p + 1) * 128)
            dyp = dyb_b[:, cols]
            dbp = db[:, cols]
            for a_ in range(2):
                src = pltpu.roll(dyp.astype(F32), 64, 1).astype(BF16) if a_ == 0 else dyp
                dlt = dbp if a_ == 0 else pltpu.roll(dbp, 64, 1)
                hi, lo = _split_hi_lo(dlt)
                blk = jnp.where(lane >= 64, src, jnp.where(lane == 0, -hi, jnp.where(lane == 1, -lo, zero)))
                h = 2 * p + a_
                do_ref[:, h * 128:(h + 1) * 128] = blk

    row = lambda n: pl.BlockSpec((tm, n), lambda i: (i, 0))
    full = lambda a: pl.BlockSpec(a.shape, lambda i: (0, 0))
    return pl.pallas_call(
        body, name=name, grid=(s // tm,),
        in_specs=[row(D_MODEL), row(WIDTH_A), row(WIDTH_A), full(na), full(nb), full(ones)],
        out_specs=[row(WIDTH_A), row(WIDTH_A), row(N_HEADS * 128), full(na), full(nb)],
        out_shape=[jax.ShapeDtypeStruct((s, WIDTH_A), BF16), jax.ShapeDtypeStruct((s, WIDTH_A), F32),
                   jax.ShapeDtypeStruct((s, N_HEADS * 128), BF16),
                   jax.ShapeDtypeStruct(na.shape, F32), jax.ShapeDtypeStruct(nb.shape, F32)],
        compiler_params=_params(("arbitrary",)),
    )(dyn, ya, yb, na, nb, ones)


def _sum_cast(parts, name):
    s = parts[0][0].shape[0]
    tm = min(512, s)

    def body(*refs):
        o_ref = refs[-1]
        for t in range(3):
            acc = refs[t][...] + refs[3 + t][...] + refs[6 + t][...]
            o_ref[:, t * WIDTH_A:(t + 1) * WIDTH_A] = acc.astype(BF16)

    spec = pl.BlockSpec((tm, WIDTH_A), lambda i: (i, 0))
    flat = [parts[g][t] for g in range(3) for t in range(3)]
    return pl.pallas_call(
        body, name=name, grid=(s // tm,), in_specs=[spec] * 9,
        out_specs=pl.BlockSpec((tm, 3 * WIDTH_A), lambda i: (i, 0)),
        out_shape=jax.ShapeDtypeStruct((s, 3 * WIDTH_A), BF16),
        compiler_params=_params(("parallel",)),
    )(*flat)


HALO = 16


def _gelu(x):
    k = math.sqrt(2.0 / math.pi)
    t = jnp.tanh(k * (x + 0.044715 * x * x * x))
    return 0.5 * x * (1.0 + t), t


def _gelu_grad(x, t):
    k = math.sqrt(2.0 / math.pi)
    return 0.5 * (1.0 + t) + 0.5 * x * (1.0 - t * t) * k * (1.0 + 3 * 0.044715 * x * x)


def _conv_specs(s, tm, tn, col_off):
    nb = s // HALO
    hb = tm // HALO
    main = pl.BlockSpec((tm, tn), lambda j, i: (i, j + col_off))
    prev = pl.BlockSpec((HALO, tn), lambda j, i: (jnp.maximum(i * hb - 1, 0), j + col_off))
    nxt = pl.BlockSpec((HALO, tn), lambda j, i: (jnp.minimum((i + 1) * hb, nb - 1), j + col_off))
    return [prev, main, nxt]


def _fill_ext(ext, prev, main, nxt, i, tm, s):
    ext[0:HALO, :] = jnp.where(i > 0, prev[...].astype(F32), 0.0)
    ext[HALO:HALO + tm, :] = main[...].astype(F32)
    ext[HALO + tm:2 * HALO + tm, :] = jnp.where((i + 1) * tm < s, nxt[...].astype(F32), 0.0)


def _conv_gate(up, cw, cb, name):
    s = up.shape[0]
    tm = min(512, s)
    tn = 1408
    ncol = D_FF // tn

    def body(gp, gm, gn, vp, vm, vn, wg_ref, wv_ref, bg_ref, bv_ref, a_ref, eg, ev):
        i = pl.program_id(1)
        _fill_ext(eg, gp, gm, gn, i, tm, s)
        _fill_ext(ev, vp, vm, vn, i, tm, s)

        def conv(e, w_ref, b_ref):
            return (w_ref[0:1, :] * e[HALO - 1:HALO - 1 + tm, :] + w_ref[1:2, :] * e[HALO:HALO + tm, :]
                    + w_ref[2:3, :] * e[HALO + 1:HALO + 1 + tm, :] + b_ref[...])

        g, _ = _gelu(conv(eg, wg_ref, bg_ref))
        a_ref[...] = (g * conv(ev, wv_ref, bv_ref)).astype(BF16)

    wspec = lambda off: pl.BlockSpec((3, tn), lambda j, i: (0, j + off))
    bspec = lambda off: pl.BlockSpec((1, tn), lambda j, i: (0, j + off))
    return pl.pallas_call(
        body, name=name, grid=(ncol, s // tm),
        in_specs=_conv_specs(s, tm, tn, 0) + _conv_specs(s, tm, tn, ncol)
        + [wspec(0), wspec(ncol), bspec(0), bspec(ncol)],
        out_specs=pl.BlockSpec((tm, tn), lambda j, i: (i, j)),
        out_shape=jax.ShapeDtypeStruct((s, D_FF), BF16),
        scratch_shapes=[pltpu.VMEM((tm + 2 * HALO, tn), F32)] * 2,
        compiler_params=_params(("parallel", "parallel")),
    )(up, up, up, up, up, up, cw, cw, cb, cb)


def _conv_gate_bwd(up, da, cw, cb, name):
    s = up.shape[0]
    tm = min(256, s)
    tn = 1408
    ncol = D_FF // tn
    te = tm + HALO

    def body(gp, gm, gn, vp, vm, vn, dp, dm, dn, wg_ref, wv_ref, bg_ref, bv_ref,
             dup_g, dup_v, dwg_ref, dwv_ref, dbg_ref, dbv_ref, eg, ev, ed, dug, duv):
        i = pl.program_id(1)

        @pl.when(i == 0)
        def _():
            for r_ in (dwg_ref, dwv_ref, dbg_ref, dbv_ref):
                r_[...] = jnp.zeros_like(r_)

        _fill_ext(eg, gp, gm, gn, i, tm, s)
        _fill_ext(ev, vp, vm, vn, i, tm, s)
        _fill_ext(ed, dp, dm, dn, i, tm, s)
        o = HALO // 2

        def conv(e, w_ref, b_ref):
            return (w_ref[0:1, :] * e[o - 1:o - 1 + te, :] + w_ref[1:2, :] * e[o:o + te, :]
                    + w_ref[2:3, :] * e[o + 1:o + 1 + te, :] + b_ref[...])

        ug = conv(eg, wg_ref, bg_ref)
        uv = conv(ev, wv_ref, bv_ref)
        gl, t = _gelu(ug)
        pos = i * tm - o + lax.broadcasted_iota(jnp.int32, (te, 1), 0)
        dav = jnp.where((pos >= 0) & (pos < s), ed[o:o + te, :], 0.0)
        dug[...] = dav * uv * _gelu_grad(ug, t)
        duv[...] = dav * gl

        def back(du, e, w_ref, dup_ref, dw_ref, db_ref):
            c0 = du[o:o + tm, :]
            dup_ref[...] = (w_ref[0:1, :] * du[o + 1:o + 1 + tm, :] + w_ref[1:2, :] * c0
                            + w_ref[2:3, :] * du[o - 1:o - 1 + tm, :]).astype(BF16)
            dw_ref[0:1, :] += jnp.sum(e[HALO - 1:HALO - 1 + tm, :] * c0, axis=0, keepdims=True)
            dw_ref[1:2, :] += jnp.sum(e[HALO:HALO + tm, :] * c0, axis=0, keepdims=True)
            dw_ref[2:3, :] += jnp.sum(e[HALO + 1:HALO + 1 + tm, :] * c0, axis=0, keepdims=True)
            db_ref[...] += jnp.sum(c0, axis=0, keepdims=True)

        back(dug, eg, wg_ref, dup_g, dwg_ref, dbg_ref)
        back(duv, ev, wv_ref, dup_v, dwv_ref, dbv_ref)

    wspec = lambda off: pl.BlockSpec((3, tn), lambda j, i: (0, j + off))
    bspec = lambda off: pl.BlockSpec((1, tn), lambda j, i: (0, j + off))
    ospec = pl.BlockSpec((tm, tn), lambda j, i: (i, j))
    w3 = pl.BlockSpec((3, tn), lambda j, i: (0, j))
    b1 = pl.BlockSpec((1, tn), lambda j, i: (0, j))
    dup_g, dup_v, dwg, dwv, dbg, dbv = pl.pallas_call(
        body, name=name, grid=(ncol, s // tm),
        in_specs=_conv_specs(s, tm, tn, 0) + _conv_specs(s, tm, tn, ncol) + _conv_specs(s, tm, tn, 0)
        + [wspec(0), wspec(ncol), bspec(0), bspec(ncol)],
        out_specs=[ospec, ospec, w3, w3, b1, b1],
        out_shape=[jax.ShapeDtypeStruct((s, D_FF), BF16)] * 2 + [jax.ShapeDtypeStruct((3, D_FF), F32)] * 2
        + [jax.ShapeDtypeStruct((1, D_FF), F32)] * 2,
        scratch_shapes=[pltpu.VMEM((tm + 2 * HALO, tn), F32)] * 3 + [pltpu.VMEM((te, tn), F32)] * 2,
        compiler_params=_params(("parallel", "arbitrary")),
    )(up, up, up, up, up, up, da, da, da, cw, cw, cb, cb)
    return dup_g, dup_v, jnp.concatenate([dwg, dwv], axis=1), jnp.concatenate([dbg, dbv], axis=1)


def _ffn_out(a, w_down, g_post, x1, target, name):
    s = x1.shape[0]
    tm = min(256, s)

    def body(a_ref, w_ref, g_ref, x1_ref, t_ref, dy3_ref, dx2_ref, loss_ref, dg_ref):
        i = pl.program_id(0)

        @pl.when(i == 0)
        def _():
            loss_ref[...] = jnp.zeros_like(loss_ref)
            dg_ref[...] = jnp.zeros_like(dg_ref)

        y3 = _dot(a_ref[...], w_ref[...], NN)
        g = g_ref[...]
        x2 = x1_ref[...] + y3 * _rstd(y3) * g
        diff = x2 - t_ref[...]
        loss_ref[...] += jnp.sum(jnp.sum(diff * diff, axis=1, keepdims=True), axis=0, keepdims=True)
        dx2 = diff * (1.0 / D_MODEL)
        dx2_ref[...] = dx2
        dy3, dg = _rms_bwd(dx2, y3, g)
        dy3_ref[...] = dy3.astype(BF16)
        dg_ref[...] += jnp.sum(dg, axis=0, keepdims=True)

    row = lambda n: pl.BlockSpec((tm, n), lambda i: (i, 0))
    full = lambda t: pl.BlockSpec(t.shape, lambda i: (0, 0))
    return pl.pallas_call(
        body, name=name, grid=(s // tm,),
        in_specs=[row(D_FF), full(w_down), full(g_post), row(D_MODEL), row(D_MODEL)],
        out_specs=[row(D_MODEL), row(D_MODEL), pl.BlockSpec((8, 128), lambda i: (0, 0)), full(g_post)],
        out_shape=[jax.ShapeDtypeStruct((s, D_MODEL), BF16), jax.ShapeDtypeStruct((s, D_MODEL), F32),
                   jax.ShapeDtypeStruct((8, 128), F32), jax.ShapeDtypeStruct(g_post.shape, F32)],
        compiler_params=_params(("arbitrary",)),
    )(a, w_down, g_post, x1, target)


def _resnorm_bwd(dh2, x1, g_ffn_pre, dx2, y2, g_mix_post, name):
    s = x1.shape[0]
    tm = min(256, s)

    def body(dh_ref, x1_ref, gf_ref, dx2_ref, y2_ref, gp_ref, dx1_ref, dy2_ref, dgf_ref, dgp_ref):
        i = pl.program_id(0)

        @pl.when(i == 0)
        def _():
            dgf_ref[...] = jnp.zeros_like(dgf_ref)
            dgp_ref[...] = jnp.zeros_like(dgp_ref)

        dn, dgf = _rms_bwd(dh_ref[...], x1_ref[...], gf_ref[...])
        dx1 = dx2_ref[...] + dn
        dx1_ref[...] = dx1
        dgf_ref[...] += jnp.sum(dgf, axis=0, keepdims=True)
        dy2, dgp = _rms_bwd(dx1, y2_ref[...], gp_ref[...])
        dy2_ref[...] = dy2.astype(BF16)
        dgp_ref[...] += jnp.sum(dgp, axis=0, keepdims=True)

    row = pl.BlockSpec((tm, D_MODEL), lambda i: (i, 0))
    full = pl.BlockSpec((1, D_MODEL), lambda i: (0, 0))
    return pl.pallas_call(
        body, name=name, grid=(s // tm,),
        in_specs=[row, row, full, row, row, full],
        out_specs=[row, row, full, full],
        out_shape=[jax.ShapeDtypeStruct((s, D_MODEL), F32), jax.ShapeDtypeStruct((s, D_MODEL), BF16),
                   jax.ShapeDtypeStruct((1, D_MODEL), F32), jax.ShapeDtypeStruct((1, D_MODEL), F32)],
        compiler_params=_params(("arbitrary",)),
    )(dh2, x1, g_ffn_pre, dx2, y2, g_mix_post)


def _final_bwd(dh1, x, g_pre, dx1, name):
    s = x.shape[0]
    tm = min(256, s)

    def body(dh_ref, x_ref, g_ref, dx1_ref, dx_ref, dg_ref):
        @pl.when(pl.program_id(0) == 0)
        def _():
            dg_ref[...] = jnp.zeros_like(dg_ref)

        dn, dg = _rms_bwd(dh_ref[...], x_ref[...], g_ref[...])
        dx_ref[...] = dx1_ref[...] + dn
        dg_ref[...] += jnp.sum(dg, axis=0, keepdims=True)

    row = pl.BlockSpec((tm, D_MODEL), lambda i: (i, 0))
    full = pl.BlockSpec((1, D_MODEL), lambda i: (0, 0))
    return pl.pallas_call(
        body, name=name, grid=(s // tm,),
        in_specs=[row, row, full, row], out_specs=[row, full],
        out_shape=[jax.ShapeDtypeStruct((s, D_MODEL), F32), jax.ShapeDtypeStruct((1, D_MODEL), F32)],
        compiler_params=_params(("arbitrary",)),
    )(dh1, x, g_pre, dx1)


def _pad_w_in(w):
    k = w.shape[0]
    z = lambda n: jnp.zeros((k, n), w.dtype)
    return jnp.concatenate([w[:, :COL_KR], z(64), w[:, COL_KR:], z(32)], axis=1)


def _unpad_w_in(w):
    return jnp.concatenate([w[:, :COL_KR], w[:, COL_KR + 64:COL_KR + 96]], axis=1)


def _pad_w_uq(w):
    k = w.shape[0]
    w = w.reshape(k, N_HEADS, QK_NOPE + QK_ROPE)
    return jnp.pad(w, ((0, 0), (0, 0), (0, 32))).reshape(k, N_HEADS * 128)


def _unpad_w_uq(w):
    k = w.shape[0]
    return w.reshape(k, N_HEADS, 128)[:, :, :QK_NOPE + QK_ROPE].reshape(k, N_HEADS * (QK_NOPE + QK_ROPE))


def _local_step(x, target, fw, rep):
    s = x.shape[0]
    tabs = _rope_tables(s)
    w_in = _pad_w_in(fw["w_in"])
    w_uq = _pad_w_uq(fw["w_uq"])
    w_ukv, w_o, w_up, w_down, cw = fw["w_ukv"], fw["w_o"], fw["w_up"], fw["w_down"], fw["conv_w"]
    cb = rep["conv_b"]
    tr = min(512, s)
    tcon = min(512, s)

    proj, h1 = _norm_matmul(x, rep["norm_mix_pre"], w_in, N_PROJ, "in_proj")
    qa, ka, va = proj[:, 0:WIDTH_A], proj[:, WIDTH_A:2 * WIDTH_A], proj[:, 2 * WIDTH_A:3 * WIDTH_A]
    biases = [_band_bias(r) for _, r in DIL_CONFIGS]
    os_, lses = [], []
    for g, (_, r) in enumerate(DIL_CONFIGS):
        o, l = _band_call("fwd", r, [qa], [ka, va], biases[g], f"band_fwd_r{r}")
        os_.append(o)
        lses.append(l)
    ya, lse_a = _band_combine(os_, lses, "band_combine")
    qcat, kcat, kvb, cqn, ckvn = _mla_prep(proj, rep["q_lat_norm"], rep["kv_lat_norm"], w_uq, w_ukv, tabs, "mla_prep")
    yb, qaug = _mla_fwd(qcat, kcat, kvb, "mla_fwd")
    yn, y2, x1 = _mix_out(ya, yb, rep["out_norm_a"], rep["out_norm_b"], w_o, rep["norm_mix_post"], x, "mix_out")
    up, h2 = _norm_matmul(x1, rep["norm_ffn_pre"], w_up, 1408, "up_proj")
    act = _conv_gate(up, cw, cb, "conv_gate")
    dy3, dx2, loss_acc, dg_ffn_post = _ffn_out(act, w_down, rep["norm_ffn_post"], x1, target, "ffn_out")

    grads = {"norm_ffn_post": dg_ffn_post}
    dact = _matmul(dy3, w_down, "nt", BF16, tr, D_FF // 2, D_MODEL, "d_act")
    grads["w_down"] = _matmul(act, dy3, "tn", F32, D_FF // 2, D_MODEL, tcon, "dw_down")
    dup_g, dup_v, grads["conv_w"], grads["conv_b"] = _conv_gate_bwd(up, dact, cw, cb, "conv_gate_bwd")
    dup = jnp.concatenate([dup_g, dup_v], axis=1)
    dh2 = _matmul(dup, w_up, "nt", F32, tr, D_MODEL, 1408, "d_h2")
    grads["w_up"] = _matmul(h2, dup, "tn", F32, D_MODEL, 1408, tcon, "dw_up")
    dx1, dy2, grads["norm_ffn_pre"], grads["norm_mix_post"] = _resnorm_bwd(
        dh2, x1, rep["norm_ffn_pre"], dx2, y2, rep["norm_mix_post"], "resnorm_bwd")
    dyn = _matmul(dy2, w_o, "nt", F32, tr, D_MODEL, D_MODEL, "d_yn")
    grads["w_o"] = _matmul(yn, dy2, "tn", F32, D_MODEL, D_MODEL, tcon, "dw_o")
    dya, delta_a, doaug, grads["out_norm_a"], grads["out_norm_b"] = _outnorm_bwd(
        dyn, ya, yb, rep["out_norm_a"], rep["out_norm_b"], _head_ones(), "outnorm_bwd")
    parts = []
    for g, (_, r) in enumerate(DIL_CONFIGS):
        (dq,) = _band_call("dq", r, [qa, dya, lse_a, delta_a], [ka, va], biases[g], f"band_dq_r{r}")
        dk, dv = _band_call("dkv", r, [ka, va], [qa, dya, lse_a, delta_a], biases[g], f"band_dkv_r{r}")
        parts.append((dq, dk, dv))
    dproj_a = _sum_cast(parts, "band_grad_sum")
    dqc, dkc, dvp = _mla_bwd(qaug, kcat, kvb, doaug, "mla_bwd")
    dproj_b, dw_uq, grads["w_ukv"], grads["q_lat_norm"], grads["kv_lat_norm"] = _mla_prep_bwd(
        dqc, dkc, dvp, proj, cqn, ckvn, rep["q_lat_norm"], rep["kv_lat_norm"], w_uq, w_ukv, tabs, "mla_prep_bwd")
    grads["w_uq"] = _unpad_w_uq(dw_uq)
    dproj = jnp.concatenate([dproj_a, dproj_b], axis=1)
    dh1 = _matmul(dproj, w_in, "nt", F32, tr, D_MODEL, N_PROJ // 2, "d_h1")
    grads["w_in"] = _unpad_w_in(_matmul(h1, dproj, "tn", F32, D_MODEL, N_PROJ // 2, tcon, "dw_in"))
    grad_x, grads["norm_mix_pre"] = _final_bwd(dh1, x, rep["norm_mix_pre"], dx1, "final_bwd")
    loss = 0.5 / D_MODEL * loss_acc[0, 0]
    return loss, grad_x, grads


MESH = pl.DeviceIdType.MESH
HBM_SPEC = pl.BlockSpec(memory_space=pltpu.HBM)
PACK_ROWS = 12288
SMALL_ROWS = 96
COL_SHARDED = ("w_in", "w_uq", "w_ukv", "w_up", "conv_w")


def _all_gather(p, name):
    rows, lanes = p.shape

    def body(x_ref, out_ref, send_sems, recv_sems, local_sem):
        x, y, c = lax.axis_index("x"), lax.axis_index("y"), lax.axis_index("c")
        me, sibling = (x, y, c), (x, y, 1 - c)
        chips = [(1 - x, y), (x, 1 - y), (1 - x, 1 - y)]

        def slot(px, py, pc):
            return out_ref.at[4 * px + 2 * py + pc]

        def copy(k, block, to, src=None):
            return pltpu.make_async_remote_copy(
                src_ref=slot(*block) if src is None else src, dst_ref=slot(*block),
                send_sem=send_sems.at[k], recv_sem=recv_sems.at[k], device_id=to, device_id_type=MESH)

        mine = pltpu.make_async_copy(x_ref, slot(*me), local_sem)
        mine.start()
        first = [copy(0, me, sibling, src=x_ref)]
        first += [copy(1 + j, me, (*chip, c), src=x_ref) for j, chip in enumerate(chips)]
        for cp in first:
            cp.start()
        passed = [copy(4 + j, (*chip, c), sibling) for j, chip in enumerate(chips)]
        for j, chip in enumerate(chips):
            copy(1 + j, (*chip, c), me).wait_recv()
            passed[j].start()
        copy(0, sibling, me).wait_recv()
        for j, chip in enumerate(chips):
            copy(4 + j, (*chip, 1 - c), me).wait_recv()
        for cp in first + passed:
            cp.wait_send()
        mine.wait()

    return pl.pallas_call(
        body, name=name,
        out_shape=jax.ShapeDtypeStruct((N_DEV, rows, lanes), p.dtype),
        in_specs=[HBM_SPEC], out_specs=HBM_SPEC,
        scratch_shapes=[pltpu.SemaphoreType.DMA((7,)), pltpu.SemaphoreType.DMA((7,)), pltpu.SemaphoreType.DMA],
    )(p)


def _grad_exchange(big, small, name):
    flips = [(fx, fy, fc) for fx in (0, 1) for fy in (0, 1) for fc in (0, 1)][1:]

    def body(big_ref, small_ref, rbig_ref, rsmall_ref, send_sems, recv_sems, local_sems):
        x, y, c = lax.axis_index("x"), lax.axis_index("y"), lax.axis_index("c")
        my = 4 * x + 2 * y + c
        own_big = pltpu.make_async_copy(big_ref.at[my], rbig_ref.at[my], local_sems.at[0])
        own_small = pltpu.make_async_copy(small_ref, rsmall_ref.at[my], local_sems.at[1])
        own_big.start()
        own_small.start()
        copies = []
        for k, (fx, fy, fc) in enumerate(flips):
            px = 1 - x if fx else x
            py = 1 - y if fy else y
            pc = 1 - c if fc else c
            peer = (px, py, pc)
            copies.append(pltpu.make_async_remote_copy(
                src_ref=big_ref.at[4 * px + 2 * py + pc], dst_ref=rbig_ref.at[my],
                send_sem=send_sems.at[2 * k], recv_sem=recv_sems.at[2 * k], device_id=peer, device_id_type=MESH))
            copies.append(pltpu.make_async_remote_copy(
                src_ref=small_ref, dst_ref=rsmall_ref.at[my],
                send_sem=send_sems.at[2 * k + 1], recv_sem=recv_sems.at[2 * k + 1], device_id=peer, device_id_type=MESH))
        for cp in copies:
            cp.start()
        for cp in copies:
            cp.wait()
        own_big.wait()
        own_small.wait()

    return pl.pallas_call(
        body, name=name,
        out_shape=[jax.ShapeDtypeStruct(big.shape, big.dtype),
                   jax.ShapeDtypeStruct((N_DEV,) + small.shape, small.dtype)],
        in_specs=[HBM_SPEC, HBM_SPEC], out_specs=[HBM_SPEC, HBM_SPEC],
        scratch_shapes=[pltpu.SemaphoreType.DMA((14,)), pltpu.SemaphoreType.DMA((14,)), pltpu.SemaphoreType.DMA((2,))],
    )(big, small)


def _adamw(parts, w, m, v, name):
    rows = w.shape[0]
    tm = min(1024, rows)

    def body(p_ref, w_ref, m_ref, v_ref, g_ref, d_ref, m2_ref, v2_ref):
        g = p_ref[0].astype(F32)
        for s in range(1, N_DEV):
            g = g + p_ref[s].astype(F32)
        g_ref[...] = g
        m2 = ADAM_B1 * m_ref[...] + (1.0 - ADAM_B1) * g
        v2 = ADAM_B2 * v_ref[...] + (1.0 - ADAM_B2) * jnp.square(g)
        m2_ref[...] = m2
        v2_ref[...] = v2
        m_hat = m2 / (1.0 - ADAM_B1 ** ADAM_STEP)
        v_hat = v2 / (1.0 - ADAM_B2 ** ADAM_STEP)
        d_ref[...] = -ADAM_LR * (m_hat / (jnp.sqrt(v_hat) + ADAM_EPS) + ADAM_WD * w_ref[...])

    row = pl.BlockSpec((tm, LANES), lambda i: (i, 0))
    return pl.pallas_call(
        body, name=name, grid=(rows // tm,),
        in_specs=[pl.BlockSpec((N_DEV, tm, LANES), lambda i: (0, i, 0)), row, row, row],
        out_specs=[row] * 4,
        out_shape=[jax.ShapeDtypeStruct((rows, LANES), F32)] * 4,
        compiler_params=_params(("parallel",)),
    )(parts, w, m, v)


def _pack(flat_parts, rows):
    flat = jnp.concatenate(flat_parts, axis=-1)
    pad = rows * LANES - flat.shape[-1]
    flat = jnp.pad(flat, [(0, 0)] * (flat.ndim - 1) + [(0, pad)])
    return flat.reshape(flat.shape[:-1] + (rows, LANES))


def _unpack(packed, shapes):
    flat = packed.reshape(packed.shape[:-2] + (-1,))
    out, off = {}, 0
    for name, shape in shapes.items():
        n = int(np.prod(shape))
        out[name] = flat[..., off:off + n].reshape(flat.shape[:-1] + tuple(shape))
        off += n
    return out


def _to_slabs(name, full):
    k, n = full.shape
    if name in COL_SHARDED:
        return full.reshape(k, N_DEV, n // N_DEV).transpose(1, 0, 2).reshape(N_DEV, -1)
    return full.reshape(N_DEV, -1)


def _from_slabs(name, slabs, shard_shape):
    k, n = shard_shape
    if name in COL_SHARDED:
        return slabs.transpose(1, 0, 2).reshape(k, N_DEV * n)
    return slabs.reshape(N_DEV * k, n)


def kernel(x, norm_mix_pre, w_in, q_lat_norm, w_uq, kv_lat_norm, w_ukv, out_norm_a, out_norm_b, w_o, norm_mix_post, norm_ffn_pre, w_up, conv_w, conv_b, w_down, norm_ffn_post, loss_target, m_norm_mix_pre, m_w_in, m_q_lat_norm, m_w_uq, m_kv_lat_norm, m_w_ukv, m_out_norm_a, m_out_norm_b, m_w_o, m_norm_mix_post, m_norm_ffn_pre, m_w_up, m_conv_w, m_conv_b, m_w_down, m_norm_ffn_post, v_norm_mix_pre, v_w_in, v_q_lat_norm, v_w_uq, v_kv_lat_norm, v_w_ukv, v_out_norm_a, v_out_norm_b, v_w_o, v_norm_mix_post, v_norm_ffn_pre, v_w_up, v_conv_w, v_conv_b, v_w_down, v_norm_ffn_post):
    given = dict(locals())
    w = {n: given[n][0] for n in WEIGHTS}
    m = {n: given["m_" + n][0] for n in WEIGHTS}
    v = {n: given["v_" + n][0] for n in WEIGHTS}
    shard_shapes = {n: w[n].shape for n in SHARDED}
    rep_shapes = {n: w[n].shape for n in REPLICATED}

    send_shapes = dict(shard_shapes)
    send_shapes["conv_w"] = shard_shapes["conv_w"] + (2,)
    flat = []
    for n in SHARDED:
        if n == "conv_w":
            flat.append(lax.bitcast_convert_type(w[n], BF16).reshape(-1))
        else:
            flat.append(w[n].astype(BF16).reshape(-1))
    gathered = _unpack(_all_gather(_pack(flat, PACK_ROWS), "weight_all_gather"), send_shapes)
    fw = {}
    for n in SHARDED:
        if n == "conv_w":
            fw[n] = _from_slabs(n, lax.bitcast_convert_type(gathered[n], F32), shard_shapes[n])
        else:
            fw[n] = _from_slabs(n, gathered[n], shard_shapes[n])
    rep = {n: given[n] for n in REPLICATED}

    loss_local, grad_x, grads = _local_step(x[0], loss_target[0], fw, rep)

    big = _pack([_to_slabs(n, grads[n]) for n in SHARDED], PACK_ROWS).astype(BF16)
    small = _pack([grads[n].reshape(-1) for n in REPLICATED], SMALL_ROWS)
    rbig, rsmall = _grad_exchange(big, small, "grad_exchange")
    pk = lambda d, names, rows: _pack([d[n].reshape(-1) for n in names], rows)
    big_out = _adamw(rbig, pk(w, SHARDED, PACK_ROWS), pk(m, SHARDED, PACK_ROWS), pk(v, SHARDED, PACK_ROWS), "adamw_sharded")
    small_out = _adamw(rsmall, pk(w, REPLICATED, SMALL_ROWS), pk(m, REPLICATED, SMALL_ROWS), pk(v, REPLICATED, SMALL_ROWS),
                       "adamw_replicated")
    results = []
    for t in range(4):
        res = dict(_unpack(big_out[t], shard_shapes))
        res.update(_unpack(small_out[t], rep_shapes))
        results.append(res)

    loss = lax.psum(loss_local, ("x", "y", "c"))
    outs = [loss, grad_x[None]]
    for res in results:
        outs += [res[n][None] for n in WEIGHTS]
    return tuple(outs)
```

```python
import functools
import math

import numpy as np
import jax
import jax.numpy as jnp
from jax import lax
from jax.experimental import pallas as pl
from jax.experimental.pallas import tpu as pltpu

F32 = jnp.float32
BF16 = jnp.bfloat16

D_MODEL = 1024
N_DEV = 8
WIDTH_A = 512
N_HEADS = 8
Q_LORA = 384
KV_LORA = 256
QK_ROPE = 32
QK_NOPE = 64
D_FF = 2816
FF_SHARD = 2 * D_FF // N_DEV
FF_SLAB = 768
DIL_CONFIGS = ((128, 1), (512, 4), (2048, 16))
BAND_HALF = 64
ROPE_BASE = 10000.0
EPS = 1e-6
NEG = -1e30
MLA_SCALE = (QK_NOPE + QK_ROPE) ** -0.5
SCALE_A = 0.125
LOG2E = 1.0 / math.log(2.0)
LN2 = math.log(2.0)

COL_CQ = 3 * WIDTH_A
COL_CKV = COL_CQ + Q_LORA
COL_KR = COL_CKV + KV_LORA
N_PROJ = COL_KR + 128
N_LAT = N_PROJ - COL_CQ
D_IN = COL_KR + QK_ROPE

ADAM_LR = 0.001
ADAM_B1 = 0.9
ADAM_B2 = 0.999
ADAM_EPS = 1e-08
ADAM_WD = 0.01
ADAM_STEP = 10

LANES = 128
VMEM_LIMIT = 56 * 1024 * 1024

SHARDED = ("w_in", "w_uq", "w_ukv", "w_o", "w_up", "conv_w", "w_down")
REPLICATED = ("norm_mix_pre", "q_lat_norm", "kv_lat_norm", "out_norm_a", "out_norm_b", "norm_mix_post",
              "norm_ffn_pre", "conv_b", "norm_ffn_post")
WEIGHTS = ("norm_mix_pre", "w_in", "q_lat_norm", "w_uq", "kv_lat_norm", "w_ukv", "out_norm_a", "out_norm_b", "w_o",
           "norm_mix_post", "norm_ffn_pre", "w_up", "conv_w", "conv_b", "w_down", "norm_ffn_post")


def _params(sem=None):
    return pltpu.CompilerParams(dimension_semantics=sem, vmem_limit_bytes=VMEM_LIMIT)


def _dot(a, b, dims):
    return lax.dot_general(a, b, (dims, ((), ())), preferred_element_type=F32)


NN = ((1,), (0,))
NT = ((1,), (1,))
TN = ((0,), (0,))


def _rstd(x):
    return lax.rsqrt(jnp.mean(x * x, axis=-1, keepdims=True) + EPS)


def _rms_bwd(dy, x, g):
    r = _rstd(x)
    z = x * r
    gy = dy * g
    dx = r * (gy - z * jnp.mean(gy * z, axis=-1, keepdims=True))
    return dx, dy * z


def _split_hi_lo(v):
    hi = v.astype(BF16)
    lo = (v - hi.astype(F32)).astype(BF16)
    return hi, lo


def _matmul(a, b, mode, out_dtype, tm, tn, tk, name):
    if mode == "nn":
        (m, k), n = a.shape, b.shape[1]
        a_spec = pl.BlockSpec((tm, tk), lambda i, j, s: (i, s))
        b_spec = pl.BlockSpec((tk, tn), lambda i, j, s: (s, j))
        dims = NN
    elif mode == "nt":
        (m, k), n = a.shape, b.shape[0]
        a_spec = pl.BlockSpec((tm, tk), lambda i, j, s: (i, s))
        b_spec = pl.BlockSpec((tn, tk), lambda i, j, s: (j, s))
        dims = NT
    else:
        (k, m), n = a.shape, b.shape[1]
        a_spec = pl.BlockSpec((tk, tm), lambda i, j, s: (s, i))
        b_spec = pl.BlockSpec((tk, tn), lambda i, j, s: (s, j))
        dims = TN
    assert m % tm == 0 and n % tn == 0 and k % tk == 0, (name, m, n, k, tm, tn, tk)
    return _matmul_core(a, b, dims, (m // tm, n // tn, k // tk), a_spec, b_spec,
                        pl.BlockSpec((tm, tn), lambda i, j, s: (i, j)), jax.ShapeDtypeStruct((m, n), out_dtype),
                        (tm, tn), name)


def _matmul_core(a, b, dims, grid, a_spec, b_spec, o_spec, out_sds, acc_shape, name):
    nk = grid[2]

    def body(a_ref, b_ref, o_ref, acc_ref):
        s = pl.program_id(2)

        @pl.when(s == 0)
        def _():
            acc_ref[...] = jnp.zeros_like(acc_ref)

        acc_ref[...] += _dot(a_ref[...].astype(BF16), b_ref[...].astype(BF16), dims)

        @pl.when(s == nk - 1)
        def _():
            o_ref[...] = acc_ref[...].astype(out_sds.dtype)

    return pl.pallas_call(
        body, name=name, grid=grid, in_specs=[a_spec, b_spec], out_specs=o_spec, out_shape=out_sds,
        scratch_shapes=[pltpu.VMEM(acc_shape, F32)],
        compiler_params=_params(("parallel", "parallel", "arbitrary")),
    )(a, b)


def _norm_matmul(x, g, w, tn, name):
    s, k = x.shape
    tm = min(512, s)
    if w.ndim == 3:
        nj, _, tn = w.shape
        half = nj // 2
        w_spec = pl.BlockSpec((None, k, tn), lambda i, j: (j, 0, 0))
        o_spec = pl.BlockSpec((None, tm, tn), lambda i, j: (j // half, i, j % half))
        o_sds = jax.ShapeDtypeStruct((2, s, half * tn), BF16)
    else:
        n = w.shape[1]
        assert n % tn == 0
        nj = n // tn
        w_spec = pl.BlockSpec((k, tn), lambda i, j: (0, j))
        o_spec = pl.BlockSpec((tm, tn), lambda i, j: (i, j))
        o_sds = jax.ShapeDtypeStruct((s, n), BF16)

    def body(x_ref, g_ref, w_ref, o_ref, h_ref):
        @pl.when(pl.program_id(1) == 0)
        def _():
            xv = x_ref[...]
            h_ref[...] = (xv * _rstd(xv) * g_ref[...]).astype(BF16)

        o_ref[...] = _dot(h_ref[...], w_ref[...], NN).astype(BF16)

    return pl.pallas_call(
        body, name=name, grid=(s // tm, nj),
        in_specs=[pl.BlockSpec((tm, k), lambda i, j: (i, 0)),
                  pl.BlockSpec((1, k), lambda i, j: (0, 0)),
                  w_spec],
        out_specs=[o_spec, pl.BlockSpec((tm, k), lambda i, j: (i, 0))],
        out_shape=[o_sds, jax.ShapeDtypeStruct((s, k), BF16)],
        compiler_params=_params(("parallel", "arbitrary")),
    )(x, g, w)


def _band_bias(r):
    off = np.arange(256)[None, :] - BAND_HALF - np.arange(128)[:, None]
    slopes = np.exp2(-8.0 * np.arange(1, N_HEADS + 1, dtype=np.float32) / N_HEADS).astype(np.float32)
    dist = (np.abs(off) * r).astype(np.float32)
    bias = -slopes[:, None, None] * dist[None]
    bias = np.where((np.abs(off) <= BAND_HALF)[None], bias, np.float32(NEG))
    return jnp.asarray(bias, F32)


def _band_call(mode, r, center, window, bias, name):
    s = center[0].shape[0]
    seq = s // r
    tq = min(512, seq)
    nsub = tq // 128
    hb = tq // BAND_HALF
    nh = seq // BAND_HALF
    nc, nw = len(center), len(window)
    n_out = {"fwd": 2, "dq": 1, "dkv": 2}[mode]
    cview = [a.reshape(seq, r * WIDTH_A) for a in center]
    wview = [a.reshape(seq, r * WIDTH_A) for a in window]

    cspec = pl.BlockSpec((tq, WIDTH_A), lambda c, i: (i, c))
    pspec = pl.BlockSpec((BAND_HALF, WIDTH_A), lambda c, i: (jnp.maximum(i * hb - 1, 0), c))
    nspec = pl.BlockSpec((BAND_HALF, WIDTH_A), lambda c, i: (jnp.minimum((i + 1) * hb, nh - 1), c))
    in_specs = [cspec] * nc
    operands = list(cview)
    for w in wview:
        in_specs += [pspec, cspec, nspec]
        operands += [w, w, w]
    in_specs.append(pl.BlockSpec((N_HEADS, 128, 256), lambda c, i: (0, 0, 0)))
    operands.append(bias)

    def aug_stat(base, stat_sw, lane, act, e0):
        hi, lo = _split_hi_lo(stat_sw)
        return jnp.where(act, base, jnp.where(lane == e0, -hi, jnp.where(lane == e0 + 1, -lo, jnp.zeros_like(hi))))

    def aug_ones(base, lane, e0):
        return jnp.where((lane == e0) | (lane == e0 + 1), jnp.ones_like(base), base)

    def body(*refs):
        c_refs = refs[:nc]
        w_refs = refs[nc:nc + 3 * nw]
        bias_ref = refs[nc + 3 * nw]
        o_refs = refs[nc + 3 * nw + 1:nc + 3 * nw + 1 + n_out]
        wins = refs[nc + 3 * nw + 1 + n_out:]
        i = pl.program_id(1)
        for t in range(nw):
            wins[t][0:BAND_HALF, :] = w_refs[3 * t][...]
            wins[t][BAND_HALF:BAND_HALF + tq, :] = w_refs[3 * t + 1][...]
            wins[t][BAND_HALF + tq:BAND_HALF + tq + BAND_HALF, :] = w_refs[3 * t + 2][...]

        def sub(j, carry):
            r0 = pl.multiple_of(j * 128, 128)
            wpos = i * tq + j * 128 - BAND_HALF + lax.broadcasted_iota(jnp.int32, (128, 256), 1)
            valid = (wpos >= 0) & (wpos < seq)
            for p in range(4):
                cols = slice(p * 128, (p + 1) * 128)
                cs = [c[pl.ds(r0, 128), cols] for c in c_refs]
                ws = [w[pl.ds(r0, 256), cols] for w in wins]
                res = []
                for a in range(2):
                    e0 = 64 if a == 0 else 0
                    lane_c = lax.broadcasted_iota(jnp.int32, (128, 128), 1)
                    lane_w = lax.broadcasted_iota(jnp.int32, (256, 128), 1)
                    act_c = (lane_c < 64) if a == 0 else (lane_c >= 64)
                    act_w = (lane_w < 64) if a == 0 else (lane_w >= 64)
                    bias_a = bias_ref[2 * p + a]
                    if mode == "fwd":
                        qa = jnp.where(act_c, cs[0] * SCALE_A, jnp.zeros_like(cs[0]))
                        sc = _dot(qa, ws[0], NT) + bias_a
                        sc = jnp.where(valid, sc, NEG)
                        m = jnp.max(sc, axis=-1, keepdims=True)
                        e = jnp.exp(sc - m)
                        l = jnp.sum(e, axis=-1, keepdims=True)
                        o = _dot(e.astype(BF16), ws[1], NN) / l
                        res.append((o, jnp.broadcast_to(m + jnp.log(l), (128, 128))))
                    elif mode == "dq":
                        q2, dy2, l2, d2 = cs
                        k2, v2 = ws
                        q_aug = aug_stat(q2 * SCALE_A, pltpu.roll(l2, 64, 1), lane_c, act_c, e0)
                        dy_aug = aug_stat(dy2, pltpu.roll(d2, 64, 1), lane_c, act_c, e0)
                        k_aug = aug_ones(k2, lane_w, e0)
                        v_aug = aug_ones(v2, lane_w, e0)
                        sc = _dot(q_aug, k_aug, NT) + bias_a
                        pr = jnp.exp(jnp.where(valid, sc, NEG))
                        ds = pr * _dot(dy_aug, v_aug, NT)
                        res.append((_dot(ds.astype(BF16), k2, NN) * SCALE_A,))
                    else:
                        k2, v2 = cs
                        q2, dy2, l2, d2 = ws
                        q_aug = aug_stat(q2 * SCALE_A, pltpu.roll(l2, 64, 1), lane_w, act_w, e0)
                        dy_aug = aug_stat(dy2, pltpu.roll(d2, 64, 1), lane_w, act_w, e0)
                        k_aug = aug_ones(k2, lane_c, e0)
                        v_aug = aug_ones(v2, lane_c, e0)
                        sc = _dot(k_aug, q_aug, NT) + bias_a
                        pr = jnp.exp(jnp.where(valid, sc, NEG))
                        ds = pr * _dot(v_aug, dy_aug, NT)
                        res.append((_dot(ds.astype(BF16), q_aug, NN), _dot(pr.astype(BF16), dy_aug, NN)))
                lane_c = lax.broadcasted_iota(jnp.int32, (128, 128), 1)
                for t in range(n_out):
                    o_refs[t][pl.ds(r0, 128), cols] = jnp.where(lane_c < 64, res[0][t], res[1][t])
            return carry

        lax.fori_loop(0, nsub, sub, 0)

    outs = pl.pallas_call(
        body, name=name, grid=(r, seq // tq),
        in_specs=in_specs,
        out_specs=[cspec] * n_out,
        out_shape=[jax.ShapeDtypeStruct((seq, r * WIDTH_A), F32)] * n_out,
        scratch_shapes=[pltpu.VMEM((tq + 2 * BAND_HALF, WIDTH_A), w.dtype) for w in window],
        compiler_params=_params(("parallel", "parallel")),
    )(*operands)
    return [o.reshape(s, WIDTH_A) for o in outs]


def _band_combine(os_, lses, name):
    s = os_[0].shape[0]
    tm = min(512, s)

    def body(o0, o1, o2, l0, l1, l2, ya_ref, lse_ref):
        a0, a1, a2 = l0[...], l1[...], l2[...]
        m = jnp.maximum(jnp.maximum(a0, a1), a2)
        e0, e1, e2 = jnp.exp(a0 - m), jnp.exp(a1 - m), jnp.exp(a2 - m)
        den = e0 + e1 + e2
        ya_ref[...] = (e0 * o0[...] + e1 * o1[...] + e2 * o2[...]) / den
        lse_ref[...] = m + jnp.log(den)

    spec = pl.BlockSpec((tm, WIDTH_A), lambda i: (i, 0))
    return pl.pallas_call(
        body, name=name, grid=(s // tm,), in_specs=[spec] * 6, out_specs=[spec] * 2,
        out_shape=[jax.ShapeDtypeStruct((s, WIDTH_A), F32)] * 2,
        compiler_params=_params(("parallel",)),
    )(*os_, *lses)


def _rope_tables(s):
    pos = jnp.arange(s, dtype=F32)
    inv_freq = jnp.exp(-math.log(ROPE_BASE) * jnp.arange(0, QK_ROPE, 2, dtype=F32) / QK_ROPE)
    ang = pos[:, None] * inv_freq[None, :]
    cos, sin = jnp.cos(ang), jnp.sin(ang)
    one = jnp.ones((s, 64), F32)
    zero16 = jnp.zeros((s, 16), F32)
    c = jnp.concatenate([one, cos, cos, jnp.ones((s, 32), F32)], axis=1)
    sa = jnp.concatenate([jnp.zeros((s, 64), F32), -sin, zero16, jnp.zeros((s, 32), F32)], axis=1)
    sb = jnp.concatenate([jnp.zeros((s, 64), F32), zero16, sin, jnp.zeros((s, 32), F32)], axis=1)
    return c, sa, sb


def _rope_fwd(x, c, sa, sb):
    return x * c + pltpu.roll(x, 112, 1) * sa + pltpu.roll(x, 16, 1) * sb


def _rope_bwd(dy, c, sa, sb):
    return dy * c + pltpu.roll(dy * sa, 16, 1) + pltpu.roll(dy * sb, 112, 1)


def _mla_prep(proj, g_q, g_kv, w_uq, w_ukv, tabs, name):
    s = proj.shape[0]
    tm = min(512, s)
    width = N_HEADS * 128

    def body(lat_ref, gq_ref, gkv_ref, wq_ref, wkv_ref, c_ref, sa_ref, sb_ref,
             q_ref, k_ref, kv_ref, cqn_ref, ckvn_ref):
        c, sa, sb = c_ref[...], sa_ref[...], sb_ref[...]
        cq = lat_ref[:, 0:Q_LORA].astype(F32)
        cqn = (cq * _rstd(cq) * gq_ref[...]).astype(BF16)
        cqn_ref[...] = cqn
        q = _dot(cqn, wq_ref[...], NN)
        ckv = lat_ref[:, Q_LORA:Q_LORA + KV_LORA].astype(F32)
        ckvn = (ckv * _rstd(ckv) * gkv_ref[...]).astype(BF16)
        ckvn_ref[...] = ckvn
        kv = _dot(ckvn, wkv_ref[...], NN)
        lane = lax.broadcasted_iota(jnp.int32, (tm, 128), 1)
        krr = _rope_fwd(lat_ref[:, Q_LORA + KV_LORA:].astype(F32), c, sa, sb)
        krr = jnp.where((lane == 96) | (lane == 97), 1.0, krr)
        ones01 = jnp.where(lane < 2, 1.0, 0.0)
        for h in range(N_HEADS):
            cols = slice(h * 128, (h + 1) * 128)
            q_ref[:, cols] = (_rope_fwd(q[:, cols], c, sa, sb) * (MLA_SCALE * LOG2E)).astype(BF16)
            k_ref[:, cols] = jnp.where(lane < 64, kv[:, cols], krr).astype(BF16)
            kv_ref[:, cols] = jnp.where(lane < 64, ones01, kv[:, cols]).astype(BF16)

    row = lambda n: pl.BlockSpec((tm, n), lambda i: (i, 0))
    full = lambda a: pl.BlockSpec(a.shape, lambda i: (0, 0))
    tab = pl.BlockSpec((tm, 128), lambda i: (i, 0))
    return pl.pallas_call(
        body, name=name, grid=(s // tm,),
        in_specs=[pl.BlockSpec((tm, N_LAT), lambda i: (i, COL_CQ // N_LAT)),
                  full(g_q), full(g_kv), full(w_uq), full(w_ukv), tab, tab, tab],
        out_specs=[row(width), row(width), row(width), row(Q_LORA), row(KV_LORA)],
        out_shape=[jax.ShapeDtypeStruct((s, width), BF16)] * 3
        + [jax.ShapeDtypeStruct((s, Q_LORA), BF16), jax.ShapeDtypeStruct((s, KV_LORA), BF16)],
        compiler_params=_params(("parallel",)),
    )(proj, g_q, g_kv, w_uq, w_ukv, *tabs)


def _mla_fwd(qcat, kcat, kvb, name):
    s = qcat.shape[0]
    tq = min(512, s)
    tk = min(512, s)
    nkc = s // tk

    def body(q_ref, k_ref, v_ref, yb_ref, qaug_ref, m_ref, acc_ref):
        lane = lax.broadcasted_iota(jnp.int32, (tq, 128), 1)
        m_ref[...] = jnp.full((2, tq, 128), NEG, F32)
        acc_ref[...] = jnp.zeros((2, tq, 128), F32)

        def chunk(cidx, carry):
            k0 = pl.multiple_of(cidx * tk, tk)
            for a in range(2):
                cols = slice(a * 128, (a + 1) * 128)
                sc = _dot(q_ref[:, cols], k_ref[pl.ds(k0, tk), cols], NT)
                m_prev = m_ref[a]
                m_new = jnp.maximum(m_prev, jnp.max(sc, axis=-1, keepdims=True))
                alpha = jnp.exp2(m_prev - m_new)
                pr = jnp.exp2(sc - jnp.tile(m_new, (1, tk // 128)))
                acc_ref[a] = alpha * acc_ref[a] + _dot(pr.astype(BF16), v_ref[pl.ds(k0, tk), cols], NN)
                m_ref[a] = m_new
            return carry

        lax.fori_loop(0, nkc, chunk, 0, unroll=2)
        outs = []
        for a in range(2):
            cols = slice(a * 128, (a + 1) * 128)
            acc = acc_ref[a]
            l = acc[:, 0:1]
            outs.append(acc / l)
            hi, lo = _split_hi_lo(m_ref[a] + jnp.log(l) * LOG2E)
            qaug_ref[:, cols] = jnp.where(lane == 96, -hi, jnp.where(lane == 97, -lo, q_ref[:, cols]))
        yb_ref[...] = jnp.where(lane < 64, pltpu.roll(outs[0], 64, 1), outs[1])

    return pl.pallas_call(
        body, name=name, grid=(4, s // tq),
        in_specs=[pl.BlockSpec((tq, 256), lambda p, i: (i, p)),
                  pl.BlockSpec((s, 256), lambda p, i: (0, p)),
                  pl.BlockSpec((s, 256), lambda p, i: (0, p))],
        out_specs=[pl.BlockSpec((tq, 128), lambda p, i: (i, p)),
                   pl.BlockSpec((tq, 256), lambda p, i: (i, p))],
        out_shape=[jax.ShapeDtypeStruct((s, WIDTH_A), F32), jax.ShapeDtypeStruct((s, N_HEADS * 128), BF16)],
        scratch_shapes=[pltpu.VMEM((2, tq, 128), F32)] * 2,
        compiler_params=_params(("parallel", "parallel")),
    )(qcat, kcat, kvb)


def _mla_bwd(qaug, kcat, kvb, doaug, name):
    s = qaug.shape[0]
    tq = min(512, s)
    tk = min(512, s)
    nqc = s // tq
    width = N_HEADS * 128

    def body(q_ref, do_ref, k_ref, v_ref, dq_ref, dk_ref, dv_ref, dk_acc, dv_acc):
        j = pl.program_id(1)

        @pl.when(j == 0)
        def _():
            dq_ref[...] = jnp.zeros_like(dq_ref)

        dk_acc[...] = jnp.zeros_like(dk_acc)
        dv_acc[...] = jnp.zeros_like(dv_acc)

        def chunk(cidx, carry):
            q0 = pl.multiple_of(cidx * tq, tq)
            for a in range(2):
                cols = slice(a * 128, (a + 1) * 128)
                qi = q_ref[pl.ds(q0, tq), cols]
                doi = do_ref[pl.ds(q0, tq), cols]
                kb = k_ref[:, cols]
                pt = jnp.exp2(_dot(kb, qi, NT))
                dst = (pt * _dot(v_ref[:, cols], doi, NT)).astype(BF16)
                dv_acc[:, cols] += _dot(pt.astype(BF16), doi, NN)
                dk_acc[:, cols] += _dot(dst, qi, NN)
                dq_ref[pl.ds(q0, tq), cols] += _dot(dst, kb, TN)
            return carry

        lax.fori_loop(0, nqc, chunk, 0)
        dk_ref[...] = dk_acc[...] * LN2
        dv_ref[...] = dv_acc[...]

    return pl.pallas_call(
        body, name=name, grid=(N_HEADS // 2, s // tk),
        in_specs=[pl.BlockSpec((s, 256), lambda p, j: (0, p)),
                  pl.BlockSpec((s, 256), lambda p, j: (0, p)),
                  pl.BlockSpec((tk, 256), lambda p, j: (j, p)),
                  pl.BlockSpec((tk, 256), lambda p, j: (j, p))],
        out_specs=[pl.BlockSpec((s, 256), lambda p, j: (0, p)),
                   pl.BlockSpec((tk, 256), lambda p, j: (j, p)),
                   pl.BlockSpec((tk, 256), lambda p, j: (j, p))],
        out_shape=[jax.ShapeDtypeStruct((s, width), F32)] * 3,
        scratch_shapes=[pltpu.VMEM((tk, 256), F32)] * 2,
        compiler_params=_params(("parallel", "arbitrary")),
    )(qaug, doaug, kcat, kvb)


def _mla_prep_bwd(dqc, dkc, dvp, proj, cqn, ckvn, g_q, g_kv, w_uq, w_ukv, tabs, name):
    s = proj.shape[0]
    tm = min(256, s)
    width = N_HEADS * 128
    n_out_cols = N_LAT

    def body(dq_ref, dk_ref, dv_ref, lat_ref, cqn_ref, ckvn_ref, gq_ref, gkv_ref, wq_ref, wkv_ref,
             c_ref, sa_ref, sb_ref, dproj_ref, dwq_ref, dwkv_ref, dgq_ref, dgkv_ref):
        i = pl.program_id(0)

        @pl.when(i == 0)
        def _():
            dwq_ref[...] = jnp.zeros_like(dwq_ref)
            dwkv_ref[...] = jnp.zeros_like(dwkv_ref)
            dgq_ref[...] = jnp.zeros_like(dgq_ref)
            dgkv_ref[...] = jnp.zeros_like(dgkv_ref)

        c, sa, sb = c_ref[...], sa_ref[...], sb_ref[...]
        lane = lax.broadcasted_iota(jnp.int32, (tm, 128), 1)
        dkr = jnp.zeros((tm, 128), F32)
        dq_parts, dkv_parts = [], []
        for h in range(N_HEADS):
            cols = slice(h * 128, (h + 1) * 128)
            dq_parts.append(_rope_bwd(dq_ref[:, cols] * MLA_SCALE, c, sa, sb).astype(BF16))
            dkh = dk_ref[:, cols]
            dkr = dkr + dkh
            dkv_parts.append(jnp.where(lane < 64, dkh, dv_ref[:, cols]).astype(BF16))
        dq = jnp.concatenate(dq_parts, axis=1)
        dkv = jnp.concatenate(dkv_parts, axis=1)
        dkr = _rope_bwd(jnp.where((lane >= 64) & (lane < 96), dkr, 0.0), c, sa, sb)

        dcqn = _dot(dq, wq_ref[...], NT)
        dwq_ref[...] += _dot(cqn_ref[...], dq, TN)
        dcq, dgq = _rms_bwd(dcqn, lat_ref[:, 0:Q_LORA].astype(F32), gq_ref[...])
        dgq_ref[...] += jnp.sum(dgq, axis=0, keepdims=True)

        dckvn = _dot(dkv, wkv_ref[...], NT)
        dwkv_ref[...] += _dot(ckvn_ref[...], dkv, TN)
        dckv, dgkv = _rms_bwd(dckvn, lat_ref[:, Q_LORA:Q_LORA + KV_LORA].astype(F32), gkv_ref[...])
        dgkv_ref[...] += jnp.sum(dgkv, axis=0, keepdims=True)

        dproj_ref[:, 0:Q_LORA] = dcq.astype(BF16)
        dproj_ref[:, Q_LORA:Q_LORA + KV_LORA] = dckv.astype(BF16)
        dproj_ref[:, Q_LORA + KV_LORA:] = dkr.astype(BF16)

    row = lambda n: pl.BlockSpec((tm, n), lambda i: (i, 0))
    full = lambda a: pl.BlockSpec(a.shape, lambda i: (0, 0))
    tab = pl.BlockSpec((tm, 128), lambda i: (i, 0))
    return pl.pallas_call(
        body, name=name, grid=(s // tm,),
        in_specs=[row(width), row(width), row(width),
                  pl.BlockSpec((tm, N_LAT), lambda i: (i, COL_CQ // N_LAT)),
                  row(Q_LORA), row(KV_LORA), full(g_q), full(g_kv), full(w_uq), full(w_ukv), tab, tab, tab],
        out_specs=[row(n_out_cols), full(w_uq), full(w_ukv), full(g_q), full(g_kv)],
        out_shape=[jax.ShapeDtypeStruct((s, n_out_cols), BF16),
                   jax.ShapeDtypeStruct(w_uq.shape, F32), jax.ShapeDtypeStruct(w_ukv.shape, F32),
                   jax.ShapeDtypeStruct(g_q.shape, F32), jax.ShapeDtypeStruct(g_kv.shape, F32)],
        compiler_params=_params(("arbitrary",)),
    )(dqc, dkc, dvp, proj, cqn, ckvn, g_q, g_kv, w_uq, w_ukv, *tabs)


def _mix_out(ya, yb, na, nb, w_o, g_post, x, name):
    s = x.shape[0]
    tm = min(512, s)

    def body(ya_ref, yb_ref, na_ref, nb_ref, w_ref, g_ref, x_ref, yn_ref, y2_ref, x1_ref):
        a, b = ya_ref[...], yb_ref[...]
        yn = jnp.concatenate([a * _rstd(a) * na_ref[...], b * _rstd(b) * nb_ref[...]], axis=1).astype(BF16)
        yn_ref[...] = yn
        y2 = _dot(yn, w_ref[...], NN)
        y2_ref[...] = y2
        x1_ref[...] = x_ref[...] + y2 * _rstd(y2) * g_ref[...]

    row = lambda n: pl.BlockSpec((tm, n), lambda i: (i, 0))
    full = lambda a: pl.BlockSpec(a.shape, lambda i: (0, 0))
    return pl.pallas_call(
        body, name=name, grid=(s // tm,),
        in_specs=[row(WIDTH_A), row(WIDTH_A), full(na), full(nb), full(w_o), full(g_post), row(D_MODEL)],
        out_specs=[row(D_MODEL)] * 3,
        out_shape=[jax.ShapeDtypeStruct((s, D_MODEL), BF16), jax.ShapeDtypeStruct((s, D_MODEL), F32),
                   jax.ShapeDtypeStruct((s, D_MODEL), F32)],
        compiler_params=_params(("parallel",)),
    )(ya, yb, na, nb, w_o, g_post, x)


def _head_ones():
    blk = np.kron(np.eye(N_HEADS, dtype=np.float32), np.ones((64, 64), np.float32))
    return jnp.asarray(blk, F32)


def _outnorm_bwd(dyn, ya, yb, na, nb, ones, name):
    s = ya.shape[0]
    tm = min(256, s)

    def body(dyn_ref, ya_ref, yb_ref, na_ref, nb_ref, ones_ref, dya_ref, da_ref, do_ref, dna_ref, dnb_ref):
        i = pl.program_id(0)

        @pl.when(i == 0)
        def _():
            dna_ref[...] = jnp.zeros_like(dna_ref)
            dnb_ref[...] = jnp.zeros_like(dnb_ref)

        a, b = ya_ref[...], yb_ref[...]
        dya, dna = _rms_bwd(dyn_ref[:, 0:WIDTH_A], a, na_ref[...])
        dyb, dnb = _rms_bwd(dyn_ref[:, WIDTH_A:], b, nb_ref[...])
        dna_ref[...] += jnp.sum(dna, axis=0, keepdims=True)
        dnb_ref[...] += jnp.sum(dnb, axis=0, keepdims=True)
        dya_b = dya.astype(BF16)
        dya_ref[...] = dya_b
        hp = lax.Precision.HIGHEST
        da_ref[...] = jnp.dot(dya_b.astype(F32) * a, ones_ref[...], precision=hp, preferred_element_type=F32)
        dyb_b = dyb.astype(BF16)
        db = jnp.dot(dyb_b.astype(F32) * b, ones_ref[...], precision=hp, preferred_element_type=F32)
        lane = lax.broadcasted_iota(jnp.int32, (tm, 128), 1)
        zero = jnp.zeros((tm, 128), BF16)
        for p in range(4):
            cols = slice(p * 128, (p + 1) * 128)
            dyp = dyb_b[:, cols]
            dbp = db[:, cols]
            for a_ in range(2):
                src = pltpu.roll(dyp.astype(F32), 64, 1).astype(BF16) if a_ == 0 else dyp
                dlt = dbp if a_ == 0 else pltpu.roll(dbp, 64, 1)
                hi, lo = _split_hi_lo(dlt)
                blk = jnp.where(lane >= 64, src, jnp.where(lane == 0, -hi, jnp.where(lane == 1, -lo, zero)))
                h = 2 * p + a_
                do_ref[:, h * 128:(h + 1) * 128] = blk

    row = lambda n: pl.BlockSpec((tm, n), lambda i: (i, 0))
    full = lambda a: pl.BlockSpec(a.shape, lambda i: (0, 0))
    return pl.pallas_call(
        body, name=name, grid=(s // tm,),
        in_specs=[row(D_MODEL), row(WIDTH_A), row(WIDTH_A), full(na), full(nb), full(ones)],
        out_specs=[row(WIDTH_A), row(WIDTH_A), row(N_HEADS * 128), full(na), full(nb)],
        out_shape=[jax.ShapeDtypeStruct((s, WIDTH_A), BF16), jax.ShapeDtypeStruct((s, WIDTH_A), F32),
                   jax.ShapeDtypeStruct((s, N_HEADS * 128), BF16),
                   jax.ShapeDtypeStruct(na.shape, F32), jax.ShapeDtypeStruct(nb.shape, F32)],
        compiler_params=_params(("arbitrary",)),
    )(dyn, ya, yb, na, nb, ones)


def _sum_cast(parts, name):
    s = parts[0][0].shape[0]
    tm = min(512, s)

    def body(*refs):
        o_ref = refs[-1]
        for t in range(3):
            acc = refs[t][...] + refs[3 + t][...] + refs[6 + t][...]
            o_ref[:, t * WIDTH_A:(t + 1) * WIDTH_A] = acc.astype(BF16)

    spec = pl.BlockSpec((tm, WIDTH_A), lambda i: (i, 0))
    flat = [parts[g][t] for g in range(3) for t in range(3)]
    return pl.pallas_call(
        body, name=name, grid=(s // tm,), in_specs=[spec] * 9,
        out_specs=pl.BlockSpec((tm, 3 * WIDTH_A), lambda i: (i, 0)),
        out_shape=jax.ShapeDtypeStruct((s, 3 * WIDTH_A), BF16),
        compiler_params=_params(("parallel",)),
    )(*flat)


HALO = 16


def _gelu(x):
    k = math.sqrt(2.0 / math.pi)
    t = jnp.tanh(k * (x + 0.044715 * x * x * x))
    return 0.5 * x * (1.0 + t), t


def _gelu_grad(x, t):
    k = math.sqrt(2.0 / math.pi)
    return 0.5 * (1.0 + t) + 0.5 * x * (1.0 - t * t) * k * (1.0 + 3 * 0.044715 * x * x)


def _halo_specs(s, tm, tn, lead):
    nb = s // HALO
    hb = tm // HALO
    pre = (lead,) if lead else ()
    z = (0,) if lead else ()
    main = pl.BlockSpec(pre + (tm, tn), lambda j, i: z + (i, j))
    prev = pl.BlockSpec(pre + (HALO, tn), lambda j, i: z + (jnp.maximum(i * hb - 1, 0), j))
    nxt = pl.BlockSpec(pre + (HALO, tn), lambda j, i: z + (jnp.minimum((i + 1) * hb, nb - 1), j))
    return [prev, main, nxt]


def _fill_ext(ext, prev, main, nxt, i, tm, s):
    ext[0:HALO, :] = jnp.where(i > 0, prev.astype(F32), 0.0)
    ext[HALO:HALO + tm, :] = main.astype(F32)
    ext[HALO + tm:2 * HALO + tm, :] = jnp.where((i + 1) * tm < s, nxt.astype(F32), 0.0)


STRIP = 16


def _shifted(ref, row0):
    n = STRIP + 16
    win = ref[pl.ds(pl.multiple_of(row0 - 8, 8), n), :]
    return pltpu.roll(win, 1, 0)[8:8 + STRIP], win[8:8 + STRIP], pltpu.roll(win, n - 1, 0)[8:8 + STRIP]


def _conv3(e, row0, w_ref, b_ref, t):
    m1, c0, p1 = _shifted(e, row0)
    return w_ref[t, 0:1, :] * m1 + w_ref[t, 1:2, :] * c0 + w_ref[t, 2:3, :] * p1 + b_ref[t]


def _conv_gate(up, cw, cb, name):
    _, s, c = up.shape
    tm = min(512, s)
    tn = FF_SLAB

    def body(up_p, up_m, up_n, w_ref, b_ref, a_ref, eg, ev):
        i = pl.program_id(1)
        _fill_ext(eg, up_p[0], up_m[0], up_n[0], i, tm, s)
        _fill_ext(ev, up_p[1], up_m[1], up_n[1], i, tm, s)

        def strip(t, carry):
            r0 = pl.multiple_of(t * STRIP, STRIP)
            g, _ = _gelu(_conv3(eg, HALO + r0, w_ref, b_ref, 0))
            a_ref[pl.ds(r0, STRIP), :] = (g * _conv3(ev, HALO + r0, w_ref, b_ref, 1)).astype(BF16)
            return carry

        lax.fori_loop(0, tm // STRIP, strip, 0)

    return pl.pallas_call(
        body, name=name, grid=(c // tn, s // tm),
        in_specs=_halo_specs(s, tm, tn, 2)
        + [pl.BlockSpec((2, 3, tn), lambda j, i: (0, 0, j)), pl.BlockSpec((2, 1, tn), lambda j, i: (0, 0, j))],
        out_specs=pl.BlockSpec((tm, tn), lambda j, i: (i, j)),
        out_shape=jax.ShapeDtypeStruct((s, c), BF16),
        scratch_shapes=[pltpu.VMEM((tm + 2 * HALO, tn), F32)] * 2,
        compiler_params=_params(("parallel", "parallel")),
    )(up, up, up, cw, cb)


def _conv_gate_bwd(up, da, cw, cb, name):
    _, s, c = up.shape
    tm = min(256, s)
    tn = FF_SLAB
    te = tm + HALO

    def body(up_p, up_m, up_n, da_p, da_m, da_n, w_ref, b_ref, dup_ref, dw_ref, db_ref, eg, ev, ed, dug, duv):
        i = pl.program_id(1)

        @pl.when(i == 0)
        def _():
            dw_ref[...] = jnp.zeros_like(dw_ref)
            db_ref[...] = jnp.zeros_like(db_ref)

        _fill_ext(eg, up_p[0], up_m[0], up_n[0], i, tm, s)
        _fill_ext(ev, up_p[1], up_m[1], up_n[1], i, tm, s)
        _fill_ext(ed, da_p[...], da_m[...], da_n[...], i, tm, s)
        o = HALO // 2

        def du_strip(t, carry):
            r0 = pl.multiple_of(t * STRIP, STRIP)
            ug = _conv3(eg, o + r0, w_ref, b_ref, 0)
            uv = _conv3(ev, o + r0, w_ref, b_ref, 1)
            gl, th = _gelu(ug)
            pos = i * tm - o + r0 + lax.broadcasted_iota(jnp.int32, (STRIP, 1), 0)
            dav = jnp.where((pos >= 0) & (pos < s), ed[pl.ds(pl.multiple_of(o + r0, 8), STRIP), :], 0.0)
            dug[pl.ds(r0, STRIP), :] = dav * uv * _gelu_grad(ug, th)
            duv[pl.ds(r0, STRIP), :] = dav * gl
            return carry

        lax.fori_loop(0, te // STRIP, du_strip, 0)

        def back(du, e, t):
            def strip(k, acc):
                r0 = pl.multiple_of(k * STRIP, STRIP)
                dm1, c0, dp1 = _shifted(du, o + r0)
                dup_ref[t, pl.ds(r0, STRIP), :] = (w_ref[t, 0:1, :] * dp1 + w_ref[t, 1:2, :] * c0
                                                   + w_ref[t, 2:3, :] * dm1).astype(BF16)
                um1, u0, up1 = _shifted(e, HALO + r0)
                fold = lambda a: a[0:8] + a[8:16]
                return (acc[0] + fold(um1 * c0), acc[1] + fold(u0 * c0), acc[2] + fold(up1 * c0), acc[3] + fold(c0))

            zero = jnp.zeros((8, tn), F32)
            acc = lax.fori_loop(0, tm // STRIP, strip, (zero, zero, zero, zero))
            for k in range(3):
                dw_ref[t, k:k + 1, :] += jnp.sum(acc[k], axis=0, keepdims=True)
            db_ref[t] += jnp.sum(acc[3], axis=0, keepdims=True)

        back(dug, eg, 0)
        back(duv, ev, 1)

    wspec = pl.BlockSpec((2, 3, tn), lambda j, i: (0, 0, j))
    bspec = pl.BlockSpec((2, 1, tn), lambda j, i: (0, 0, j))
    return pl.pallas_call(
        body, name=name, grid=(c // tn, s // tm),
        in_specs=_halo_specs(s, tm, tn, 2) + _halo_specs(s, tm, tn, 0) + [wspec, bspec],
        out_specs=[pl.BlockSpec((2, tm, tn), lambda j, i: (0, i, j)), wspec, bspec],
        out_shape=[jax.ShapeDtypeStruct((2, s, c), BF16), jax.ShapeDtypeStruct((2, 3, c), F32),
                   jax.ShapeDtypeStruct((2, 1, c), F32)],
        scratch_shapes=[pltpu.VMEM((tm + 2 * HALO, tn), F32)] * 3 + [pltpu.VMEM((te, tn), F32)] * 2,
        compiler_params=_params(("parallel", "arbitrary")),
    )(up, up, up, da, da, da, cw, cb)


def _ffn_out(a, w_down, g_post, x1, target, name):
    s = x1.shape[0]
    tm = min(256, s)

    def body(a_ref, w_ref, g_ref, x1_ref, t_ref, dy3_ref, dx2_ref, loss_ref, dg_ref):
        i = pl.program_id(0)

        @pl.when(i == 0)
        def _():
            loss_ref[...] = jnp.zeros_like(loss_ref)
            dg_ref[...] = jnp.zeros_like(dg_ref)

        y3 = _dot(a_ref[...], w_ref[...], NN)
        g = g_ref[...]
        x2 = x1_ref[...] + y3 * _rstd(y3) * g
        diff = x2 - t_ref[...]
        loss_ref[...] += jnp.sum(jnp.sum(diff * diff, axis=1, keepdims=True), axis=0, keepdims=True)
        dx2 = diff * (1.0 / D_MODEL)
        dx2_ref[...] = dx2
        dy3, dg = _rms_bwd(dx2, y3, g)
        dy3_ref[...] = dy3.astype(BF16)
        dg_ref[...] += jnp.sum(dg, axis=0, keepdims=True)

    row = lambda n: pl.BlockSpec((tm, n), lambda i: (i, 0))
    full = lambda t: pl.BlockSpec(t.shape, lambda i: (0, 0))
    return pl.pallas_call(
        body, name=name, grid=(s // tm,),
        in_specs=[row(a.shape[1]), full(w_down), full(g_post), row(D_MODEL), row(D_MODEL)],
        out_specs=[row(D_MODEL), row(D_MODEL), pl.BlockSpec((8, 128), lambda i: (0, 0)), full(g_post)],
        out_shape=[jax.ShapeDtypeStruct((s, D_MODEL), BF16), jax.ShapeDtypeStruct((s, D_MODEL), F32),
                   jax.ShapeDtypeStruct((8, 128), F32), jax.ShapeDtypeStruct(g_post.shape, F32)],
        compiler_params=_params(("arbitrary",)),
    )(a, w_down, g_post, x1, target)


def _resnorm_bwd(dh2, x1, g_ffn_pre, dx2, y2, g_mix_post, name):
    s = x1.shape[0]
    tm = min(256, s)

    def body(dh_ref, x1_ref, gf_ref, dx2_ref, y2_ref, gp_ref, dx1_ref, dy2_ref, dgf_ref, dgp_ref):
        i = pl.program_id(0)

        @pl.when(i == 0)
        def _():
            dgf_ref[...] = jnp.zeros_like(dgf_ref)
            dgp_ref[...] = jnp.zeros_like(dgp_ref)

        dn, dgf = _rms_bwd(dh_ref[...], x1_ref[...], gf_ref[...])
        dx1 = dx2_ref[...] + dn
        dx1_ref[...] = dx1
        dgf_ref[...] += jnp.sum(dgf, axis=0, keepdims=True)
        dy2, dgp = _rms_bwd(dx1, y2_ref[...], gp_ref[...])
        dy2_ref[...] = dy2.astype(BF16)
        dgp_ref[...] += jnp.sum(dgp, axis=0, keepdims=True)

    row = pl.BlockSpec((tm, D_MODEL), lambda i: (i, 0))
    full = pl.BlockSpec((1, D_MODEL), lambda i: (0, 0))
    return pl.pallas_call(
        body, name=name, grid=(s // tm,),
        in_specs=[row, row, full, row, row, full],
        out_specs=[row, row, full, full],
        out_shape=[jax.ShapeDtypeStruct((s, D_MODEL), F32), jax.ShapeDtypeStruct((s, D_MODEL), BF16),
                   jax.ShapeDtypeStruct((1, D_MODEL), F32), jax.ShapeDtypeStruct((1, D_MODEL), F32)],
        compiler_params=_params(("arbitrary",)),
    )(dh2, x1, g_ffn_pre, dx2, y2, g_mix_post)


def _final_bwd(dh1, x, g_pre, dx1, name):
    s = x.shape[0]
    tm = min(256, s)

    def body(dh_ref, x_ref, g_ref, dx1_ref, dx_ref, dg_ref):
        @pl.when(pl.program_id(0) == 0)
        def _():
            dg_ref[...] = jnp.zeros_like(dg_ref)

        dn, dg = _rms_bwd(dh_ref[...], x_ref[...], g_ref[...])
        dx_ref[...] = dx1_ref[...] + dn
        dg_ref[...] += jnp.sum(dg, axis=0, keepdims=True)

    row = pl.BlockSpec((tm, D_MODEL), lambda i: (i, 0))
    full = pl.BlockSpec((1, D_MODEL), lambda i: (0, 0))
    return pl.pallas_call(
        body, name=name, grid=(s // tm,),
        in_specs=[row, row, full, row], out_specs=[row, full],
        out_shape=[jax.ShapeDtypeStruct((s, D_MODEL), F32), jax.ShapeDtypeStruct((1, D_MODEL), F32)],
        compiler_params=_params(("arbitrary",)),
    )(dh1, x, g_pre, dx1)


def _local_step(x, target, fw, rep):
    s = x.shape[0]
    tabs = _rope_tables(s)
    w_in, w_uq, w_ukv, w_o, w_up, w_down = (fw[n] for n in ("w_in", "w_uq", "w_ukv", "w_o", "w_up", "w_down"))
    cw, cb = fw["conv_w"], fw["conv_b"]
    tr = min(512, s)
    tcon = min(512, s)
    ff = w_down.shape[0]

    proj, h1 = _norm_matmul(x, rep["norm_mix_pre"], w_in, N_PROJ, "in_proj")
    qa, ka, va = proj[:, 0:WIDTH_A], proj[:, WIDTH_A:2 * WIDTH_A], proj[:, 2 * WIDTH_A:3 * WIDTH_A]
    biases = [_band_bias(r) for _, r in DIL_CONFIGS]
    os_, lses = [], []
    for g, (_, r) in enumerate(DIL_CONFIGS):
        o, l = _band_call("fwd", r, [qa], [ka, va], biases[g], f"band_fwd_r{r}")
        os_.append(o)
        lses.append(l)
    ya, lse_a = _band_combine(os_, lses, "band_combine")
    qcat, kcat, kvb, cqn, ckvn = _mla_prep(proj, rep["q_lat_norm"], rep["kv_lat_norm"], w_uq, w_ukv, tabs, "mla_prep")
    yb, qaug = _mla_fwd(qcat, kcat, kvb, "mla_fwd")
    yn, y2, x1 = _mix_out(ya, yb, rep["out_norm_a"], rep["out_norm_b"], w_o, rep["norm_mix_post"], x, "mix_out")
    up, h2 = _norm_matmul(x1, rep["norm_ffn_pre"], w_up, None, "up_proj")
    act = _conv_gate(up, cw, cb, "conv_gate")
    dy3, dx2, loss_acc, dg_ffn_post = _ffn_out(act, w_down, rep["norm_ffn_post"], x1, target, "ffn_out")

    grads = {"norm_ffn_post": dg_ffn_post}
    dact = _matmul(dy3, w_down, "nt", BF16, tr, ff // 2, D_MODEL, "d_act")
    grads["w_down"] = _matmul(act, dy3, "tn", F32, ff // 2, D_MODEL, tcon, "dw_down")
    dup, grads["conv_w"], grads["conv_b"] = _conv_gate_bwd(up, dact, cw, cb, "conv_gate_bwd")
    half = N_DEV // 2
    dh2 = _matmul_core(
        dup, w_up, NT, (s // tr, 1, N_DEV),
        pl.BlockSpec((None, tr, FF_SLAB), lambda i, j, t: (t // half, i, t % half)),
        pl.BlockSpec((None, D_MODEL, FF_SLAB), lambda i, j, t: (t, 0, 0)),
        pl.BlockSpec((tr, D_MODEL), lambda i, j, t: (i, 0)),
        jax.ShapeDtypeStruct((s, D_MODEL), F32), (tr, D_MODEL), "d_h2")
    grads["w_up"] = _matmul_core(
        h2, dup, TN, (1, N_DEV, s // tcon),
        pl.BlockSpec((tcon, D_MODEL), lambda i, j, t: (t, 0)),
        pl.BlockSpec((None, tcon, FF_SLAB), lambda i, j, t: (j // half, t, j % half)),
        pl.BlockSpec((None, D_MODEL, FF_SLAB), lambda i, j, t: (j, 0, 0)),
        jax.ShapeDtypeStruct((N_DEV, D_MODEL, FF_SLAB), F32), (D_MODEL, FF_SLAB), "dw_up")
    dx1, dy2, grads["norm_ffn_pre"], grads["norm_mix_post"] = _resnorm_bwd(
        dh2, x1, rep["norm_ffn_pre"], dx2, y2, rep["norm_mix_post"], "resnorm_bwd")
    dyn = _matmul(dy2, w_o, "nt", F32, tr, D_MODEL, D_MODEL, "d_yn")
    grads["w_o"] = _matmul(yn, dy2, "tn", F32, D_MODEL, D_MODEL, tcon, "dw_o")
    dya, delta_a, doaug, grads["out_norm_a"], grads["out_norm_b"] = _outnorm_bwd(
        dyn, ya, yb, rep["out_norm_a"], rep["out_norm_b"], _head_ones(), "outnorm_bwd")
    parts = []
    for g, (_, r) in enumerate(DIL_CONFIGS):
        (dq,) = _band_call("dq", r, [qa, dya, lse_a, delta_a], [ka, va], biases[g], f"band_dq_r{r}")
        dk, dv = _band_call("dkv", r, [ka, va], [qa, dya, lse_a, delta_a], biases[g], f"band_dkv_r{r}")
        parts.append((dq, dk, dv))
    dproj_a = _sum_cast(parts, "band_grad_sum")
    dqc, dkc, dvp = _mla_bwd(qaug, kcat, kvb, doaug, "mla_bwd")
    dproj_b, grads["w_uq"], grads["w_ukv"], grads["q_lat_norm"], grads["kv_lat_norm"] = _mla_prep_bwd(
        dqc, dkc, dvp, proj, cqn, ckvn, rep["q_lat_norm"], rep["kv_lat_norm"], w_uq, w_ukv, tabs, "mla_prep_bwd")
    dproj = jnp.concatenate([dproj_a, dproj_b], axis=1)
    dh1 = _matmul(dproj, w_in, "nt", F32, tr, D_MODEL, N_PROJ // 2, "d_h1")
    grads["w_in"] = _matmul(h1, dproj, "tn", F32, D_MODEL, N_PROJ // 2, tcon, "dw_in")
    grad_x, grads["norm_mix_pre"] = _final_bwd(dh1, x, rep["norm_mix_pre"], dx1, "final_bwd")
    loss = 0.5 / D_MODEL * loss_acc[0, 0]
    return loss, grad_x, grads


MESH = pl.DeviceIdType.MESH
HBM_SPEC = pl.BlockSpec(memory_space=pltpu.HBM)
SMALL_ROWS = 96
BUF_SHAPES = {"w_up": (D_MODEL, FF_SLAB), "w_in": (D_MODEL, 384), "w_down": (D_FF // N_DEV, D_MODEL),
              "w_o": (D_MODEL // N_DEV, D_MODEL), "w_uq": (Q_LORA, 128), "w_ukv": (KV_LORA, 128), "conv_w": (8, FF_SLAB)}
BUF_ORDER = tuple(BUF_SHAPES)


def _all_gather(bufs, name):
    nb = len(bufs)

    def body(*refs):
        x_refs, out_refs = refs[:nb], refs[nb:2 * nb]
        send_sems, recv_sems, local_sems = refs[2 * nb:]
        x, y, c = lax.axis_index("x"), lax.axis_index("y"), lax.axis_index("c")
        me, sibling = (x, y, c), (x, y, 1 - c)
        chips = [(1 - x, y), (x, 1 - y), (1 - x, 1 - y)]

        def copy(b, k, block, to, own=False):
            px, py, pc = block
            slot = out_refs[b].at[4 * px + 2 * py + pc]
            return pltpu.make_async_remote_copy(
                src_ref=x_refs[b] if own else slot, dst_ref=slot,
                send_sem=send_sems.at[7 * b + k], recv_sem=recv_sems.at[7 * b + k], device_id=to, device_id_type=MESH)

        mine = [pltpu.make_async_copy(x_refs[b], out_refs[b].at[4 * x + 2 * y + c], local_sems.at[b]) for b in range(nb)]
        sends = []
        for b in range(nb):
            mine[b].start()
            first = [copy(b, 0, me, sibling, own=True)]
            first += [copy(b, 1 + j, me, (*chip, c), own=True) for j, chip in enumerate(chips)]
            for cp in first:
                cp.start()
            sends += first
        for j, chip in enumerate(chips):
            for b in range(nb):
                copy(b, 1 + j, (*chip, c), me).wait_recv()
                passed = copy(b, 4 + j, (*chip, c), sibling)
                passed.start()
                sends.append(passed)
        for b in range(nb):
            copy(b, 0, sibling, me).wait_recv()
            for j, chip in enumerate(chips):
                copy(b, 4 + j, (*chip, 1 - c), me).wait_recv()
        for cp in sends:
            cp.wait_send()
        for cp in mine:
            cp.wait()

    return pl.pallas_call(
        body, name=name,
        out_shape=[jax.ShapeDtypeStruct((N_DEV,) + p.shape, p.dtype) for p in bufs],
        in_specs=[HBM_SPEC] * nb, out_specs=[HBM_SPEC] * nb,
        scratch_shapes=[pltpu.SemaphoreType.DMA((7 * nb,)), pltpu.SemaphoreType.DMA((7 * nb,)),
                        pltpu.SemaphoreType.DMA((nb,))],
    )(*bufs)


def _grad_exchange(bigs, small, name):
    flips = [(fx, fy, fc) for fx in (0, 1) for fy in (0, 1) for fc in (0, 1)][1:]
    nb = len(bigs)

    def body(*refs):
        big_refs, small_ref = refs[:nb], refs[nb]
        rbig_refs, rsmall_ref = refs[nb + 1:2 * nb + 1], refs[2 * nb + 1]
        send_sems, recv_sems, local_sems = refs[2 * nb + 2:]
        x, y, c = lax.axis_index("x"), lax.axis_index("y"), lax.axis_index("c")
        my = 4 * x + 2 * y + c
        own = [pltpu.make_async_copy(big_refs[b].at[my], rbig_refs[b].at[my], local_sems.at[b]) for b in range(nb)]
        own.append(pltpu.make_async_copy(small_ref, rsmall_ref.at[my], local_sems.at[nb]))
        for cp in own:
            cp.start()
        copies = []
        for b in range(nb + 1):
            for k, (fx, fy, fc) in enumerate(flips):
                px = 1 - x if fx else x
                py = 1 - y if fy else y
                pc = 1 - c if fc else c
                src = small_ref if b == nb else big_refs[b].at[4 * px + 2 * py + pc]
                dst = rsmall_ref.at[my] if b == nb else rbig_refs[b].at[my]
                copies.append(pltpu.make_async_remote_copy(
                    src_ref=src, dst_ref=dst, send_sem=send_sems.at[7 * b + k], recv_sem=recv_sems.at[7 * b + k],
                    device_id=(px, py, pc), device_id_type=MESH))
        for cp in copies:
            cp.start()
        for cp in copies:
            cp.wait()
        for cp in own:
            cp.wait()

    nsem = 7 * (nb + 1)
    return pl.pallas_call(
        body, name=name,
        out_shape=[jax.ShapeDtypeStruct(b.shape, b.dtype) for b in bigs]
        + [jax.ShapeDtypeStruct((N_DEV,) + small.shape, small.dtype)],
        in_specs=[HBM_SPEC] * (nb + 1), out_specs=[HBM_SPEC] * (nb + 1),
        scratch_shapes=[pltpu.SemaphoreType.DMA((nsem,)), pltpu.SemaphoreType.DMA((nsem,)),
                        pltpu.SemaphoreType.DMA((nb + 1,))],
    )(*bigs, small)


def _adamw(parts, w, m, v, name):
    rows, n = w.shape
    tm = rows if rows <= 384 else 256
    assert rows % tm == 0

    def body(p_ref, w_ref, m_ref, v_ref, g_ref, d_ref, m2_ref, v2_ref):
        g = p_ref[0, :, 0:n].astype(F32)
        for s in range(1, N_DEV):
            g = g + p_ref[s, :, 0:n].astype(F32)
        g_ref[...] = g
        m2 = ADAM_B1 * m_ref[...] + (1.0 - ADAM_B1) * g
        v2 = ADAM_B2 * v_ref[...] + (1.0 - ADAM_B2) * jnp.square(g)
        m2_ref[...] = m2
        v2_ref[...] = v2
        m_hat = m2 / (1.0 - ADAM_B1 ** ADAM_STEP)
        v_hat = v2 / (1.0 - ADAM_B2 ** ADAM_STEP)
        d_ref[...] = -ADAM_LR * (m_hat / (jnp.sqrt(v_hat) + ADAM_EPS) + ADAM_WD * w_ref[...])

    row = pl.BlockSpec((tm, n), lambda i: (i, 0))
    return pl.pallas_call(
        body, name=name, grid=(rows // tm,),
        in_specs=[pl.BlockSpec((N_DEV, tm, parts.shape[2]), lambda i: (0, i, 0)), row, row, row],
        out_specs=[row] * 4,
        out_shape=[jax.ShapeDtypeStruct((rows, n), F32)] * 4,
        compiler_params=_params(("parallel",)),
    )(parts, w, m, v)


def _pack(flat_parts, rows):
    flat = jnp.concatenate(flat_parts, axis=-1)
    pad = rows * LANES - flat.shape[-1]
    flat = jnp.pad(flat, [(0, 0)] * (flat.ndim - 1) + [(0, pad)])
    return flat.reshape(flat.shape[:-1] + (rows, LANES))


def _unpack(packed, shapes):
    flat = packed.reshape(packed.shape[:-2] + (-1,))
    out, off = {}, 0
    for name, shape in shapes.items():
        n = int(np.prod(shape))
        out[name] = flat[..., off:off + n].reshape(flat.shape[:-1] + tuple(shape))
        off += n
    return out


def _pad_to(a, shape):
    return jnp.pad(a, [(0, t - d) for d, t in zip(a.shape, shape)])


def _pad_w_in(w):
    k = w.shape[0]
    z = lambda n: jnp.zeros((k, n), w.dtype)
    return jnp.concatenate([w[:, :COL_KR], z(64), w[:, COL_KR:], z(32)], axis=1)


def _unpad_w_in(w):
    return jnp.concatenate([w[:, :COL_KR], w[:, COL_KR + 64:COL_KR + 96]], axis=1)


def _assemble_weights(g, conv_b):
    half = N_DEV // 2
    cols = lambda a: a.transpose(1, 0, 2).reshape(a.shape[1], N_DEV * a.shape[2])
    fw = {
        "w_in": _pad_w_in(cols(g["w_in"][:, :, :D_IN // N_DEV])),
        "w_uq": cols(g["w_uq"]),
        "w_ukv": cols(g["w_ukv"]),
        "w_o": g["w_o"].reshape(D_MODEL, D_MODEL),
        "w_up": g["w_up"],
        "w_down": _pad_to(g["w_down"].reshape(half, FF_SHARD, D_MODEL), (half, FF_SLAB, D_MODEL)).reshape(half * FF_SLAB, D_MODEL),
        "conv_w": g["conv_w"][:, :3].reshape(2, half, 3, FF_SLAB).transpose(0, 2, 1, 3).reshape(2, 3, half * FF_SLAB),
        "conv_b": _pad_to(conv_b.reshape(2, 1, half, FF_SHARD), (2, 1, half, FF_SLAB)).reshape(2, 1, half * FF_SLAB),
    }
    return fw


def _grad_bufs(grads):
    half = N_DEV // 2
    slabs = lambda a: a.reshape(a.shape[0], N_DEV, a.shape[1] // N_DEV).transpose(1, 0, 2)
    out = {
        "w_in": _pad_to(slabs(_unpad_w_in(grads["w_in"])), (N_DEV,) + BUF_SHAPES["w_in"]),
        "w_uq": slabs(grads["w_uq"]),
        "w_ukv": slabs(grads["w_ukv"]),
        "w_o": grads["w_o"].reshape((N_DEV,) + BUF_SHAPES["w_o"]),
        "w_up": grads["w_up"],
        "w_down": grads["w_down"].reshape(half, FF_SLAB, D_MODEL)[:, :FF_SHARD].reshape((N_DEV,) + BUF_SHAPES["w_down"]),
        "conv_w": _pad_to(grads["conv_w"].reshape(2, 3, half, FF_SLAB).transpose(0, 2, 1, 3).reshape(N_DEV, 3, FF_SLAB),
                          (N_DEV,) + BUF_SHAPES["conv_w"]),
    }
    return {n: (a if n == "conv_w" else a.astype(BF16)) for n, a in out.items()}


def kernel(x, norm_mix_pre, w_in, q_lat_norm, w_uq, kv_lat_norm, w_ukv, out_norm_a, out_norm_b, w_o, norm_mix_post, norm_ffn_pre, w_up, conv_w, conv_b, w_down, norm_ffn_post, loss_target, m_norm_mix_pre, m_w_in, m_q_lat_norm, m_w_uq, m_kv_lat_norm, m_w_ukv, m_out_norm_a, m_out_norm_b, m_w_o, m_norm_mix_post, m_norm_ffn_pre, m_w_up, m_conv_w, m_conv_b, m_w_down, m_norm_ffn_post, v_norm_mix_pre, v_w_in, v_q_lat_norm, v_w_uq, v_kv_lat_norm, v_w_ukv, v_out_norm_a, v_out_norm_b, v_w_o, v_norm_mix_post, v_norm_ffn_pre, v_w_up, v_conv_w, v_conv_b, v_w_down, v_norm_ffn_post):
    given = dict(locals())
    w = {n: given[n][0] for n in WEIGHTS}
    m = {n: given["m_" + n][0] for n in WEIGHTS}
    v = {n: given["v_" + n][0] for n in WEIGHTS}
    rep_shapes = {n: w[n].shape for n in REPLICATED}

    bufs = [_pad_to(w[n] if n == "conv_w" else w[n].astype(BF16), BUF_SHAPES[n]) for n in BUF_ORDER]
    gathered = dict(zip(BUF_ORDER, _all_gather(bufs, "weight_all_gather")))
    fw = _assemble_weights(gathered, conv_b)
    rep = {n: given[n] for n in REPLICATED}

    loss_local, grad_x, grads = _local_step(x[0], loss_target[0], fw, rep)

    gb = _grad_bufs(grads)
    grads["conv_b"] = grads["conv_b"].reshape(N_DEV, FF_SLAB)[:, :FF_SHARD]
    small = _pack([grads[n].reshape(-1) for n in REPLICATED], SMALL_ROWS)
    received = _grad_exchange([gb[n] for n in BUF_ORDER], small, "grad_exchange")
    results = [{}, {}, {}, {}]
    for n, parts in zip(BUF_ORDER, received[:-1]):
        if n == "conv_w":
            args = [_pad_to(t[n], BUF_SHAPES[n]) for t in (w, m, v)]
        else:
            args = [w[n], m[n], v[n]]
        outs = _adamw(parts, *args, f"adamw_{n}")
        for t in range(4):
            results[t][n] = outs[t][:w[n].shape[0], :w[n].shape[1]] if n == "conv_w" else outs[t]
    pk = lambda d: _pack([d[n].reshape(-1) for n in REPLICATED], SMALL_ROWS)
    small_out = _adamw(received[-1], pk(w), pk(m), pk(v), "adamw_replicated")
    for t in range(4):
        results[t].update(_unpack(small_out[t], rep_shapes))

    loss = lax.psum(loss_local, ("x", "y", "c"))
    outs = [loss, grad_x[None]]
    for res in results:
        outs += [res[n][None] for n in WEIGHTS]
    return tuple(outs)
```

```python
import functools
import math

import numpy as np
import jax
import jax.numpy as jnp
from jax import lax
from jax.experimental import pallas as pl
from jax.experimental.pallas import tpu as pltpu

F32 = jnp.float32
BF16 = jnp.bfloat16

D_MODEL = 1024
N_DEV = 8
WIDTH_A = 512
N_HEADS = 8
Q_LORA = 384
KV_LORA = 256
QK_ROPE = 32
QK_NOPE = 64
D_FF = 2816
FF_SHARD = 2 * D_FF // N_DEV
FF_SLAB = 768
DIL_CONFIGS = ((128, 1), (512, 4), (2048, 16))
BAND_HALF = 64
ROPE_BASE = 10000.0
EPS = 1e-6
NEG = -1e30
MLA_SCALE = (QK_NOPE + QK_ROPE) ** -0.5
SCALE_A = 0.125
LOG2E = 1.0 / math.log(2.0)
LN2 = math.log(2.0)

COL_CQ = 3 * WIDTH_A
COL_CKV = COL_CQ + Q_LORA
COL_KR = COL_CKV + KV_LORA
N_PROJ = COL_KR + 128
N_LAT = N_PROJ - COL_CQ
D_IN = COL_KR + QK_ROPE

ADAM_LR = 0.001
ADAM_B1 = 0.9
ADAM_B2 = 0.999
ADAM_EPS = 1e-08
ADAM_WD = 0.01
ADAM_STEP = 10

LANES = 128
VMEM_LIMIT = 56 * 1024 * 1024

SHARDED = ("w_in", "w_uq", "w_ukv", "w_o", "w_up", "conv_w", "w_down")
REPLICATED = ("norm_mix_pre", "q_lat_norm", "kv_lat_norm", "out_norm_a", "out_norm_b", "norm_mix_post",
              "norm_ffn_pre", "conv_b", "norm_ffn_post")
WEIGHTS = ("norm_mix_pre", "w_in", "q_lat_norm", "w_uq", "kv_lat_norm", "w_ukv", "out_norm_a", "out_norm_b", "w_o",
           "norm_mix_post", "norm_ffn_pre", "w_up", "conv_w", "conv_b", "w_down", "norm_ffn_post")


def _params(sem=None):
    return pltpu.CompilerParams(dimension_semantics=sem, vmem_limit_bytes=VMEM_LIMIT)


def _dot(a, b, dims):
    return lax.dot_general(a, b, (dims, ((), ())), preferred_element_type=F32)


NN = ((1,), (0,))
NT = ((1,), (1,))
TN = ((0,), (0,))


def _rstd(x):
    return lax.rsqrt(jnp.mean(x * x, axis=-1, keepdims=True) + EPS)


def _rms_bwd(dy, x, g):
    r = _rstd(x)
    z = x * r
    gy = dy * g
    dx = r * (gy - z * jnp.mean(gy * z, axis=-1, keepdims=True))
    return dx, dy * z


def _split_hi_lo(v):
    hi = v.astype(BF16)
    lo = (v - hi.astype(F32)).astype(BF16)
    return hi, lo


def _matmul(a, b, mode, out_dtype, tm, tn, tk, name):
    if mode == "nn":
        (m, k), n = a.shape, b.shape[1]
        a_spec = pl.BlockSpec((tm, tk), lambda i, j, s: (i, s))
        b_spec = pl.BlockSpec((tk, tn), lambda i, j, s: (s, j))
        dims = NN
    elif mode == "nt":
        (m, k), n = a.shape, b.shape[0]
        a_spec = pl.BlockSpec((tm, tk), lambda i, j, s: (i, s))
        b_spec = pl.BlockSpec((tn, tk), lambda i, j, s: (j, s))
        dims = NT
    else:
        (k, m), n = a.shape, b.shape[1]
        a_spec = pl.BlockSpec((tk, tm), lambda i, j, s: (s, i))
        b_spec = pl.BlockSpec((tk, tn), lambda i, j, s: (s, j))
        dims = TN
    assert m % tm == 0 and n % tn == 0 and k % tk == 0, (name, m, n, k, tm, tn, tk)
    return _matmul_core(a, b, dims, (m // tm, n // tn, k // tk), a_spec, b_spec,
                        pl.BlockSpec((tm, tn), lambda i, j, s: (i, j)), jax.ShapeDtypeStruct((m, n), out_dtype),
                        (tm, tn), name)


def _matmul_core(a, b, dims, grid, a_spec, b_spec, o_spec, out_sds, acc_shape, name):
    nk = grid[2]

    def body(a_ref, b_ref, o_ref, acc_ref):
        s = pl.program_id(2)

        @pl.when(s == 0)
        def _():
            acc_ref[...] = jnp.zeros_like(acc_ref)

        acc_ref[...] += _dot(a_ref[...].astype(BF16), b_ref[...].astype(BF16), dims)

        @pl.when(s == nk - 1)
        def _():
            o_ref[...] = acc_ref[...].astype(out_sds.dtype)

    return pl.pallas_call(
        body, name=name, grid=grid, in_specs=[a_spec, b_spec], out_specs=o_spec, out_shape=out_sds,
        scratch_shapes=[pltpu.VMEM(acc_shape, F32)],
        compiler_params=_params(("parallel", "parallel", "arbitrary")),
    )(a, b)


def _norm_matmul(x, g, w, tn, name):
    s, k = x.shape
    tm = min(512, s)
    if w.ndim == 3:
        nj, _, tn = w.shape
        half = nj // 2
        w_spec = pl.BlockSpec((None, k, tn), lambda i, j: (j, 0, 0))
        o_spec = pl.BlockSpec((None, tm, tn), lambda i, j: (j // half, i, j % half))
        o_sds = jax.ShapeDtypeStruct((2, s, half * tn), BF16)
    else:
        n = w.shape[1]
        assert n % tn == 0
        nj = n // tn
        w_spec = pl.BlockSpec((k, tn), lambda i, j: (0, j))
        o_spec = pl.BlockSpec((tm, tn), lambda i, j: (i, j))
        o_sds = jax.ShapeDtypeStruct((s, n), BF16)

    def body(x_ref, g_ref, w_ref, o_ref, h_ref):
        @pl.when(pl.program_id(1) == 0)
        def _():
            xv = x_ref[...]
            h_ref[...] = (xv * _rstd(xv) * g_ref[...]).astype(BF16)

        o_ref[...] = _dot(h_ref[...], w_ref[...], NN).astype(BF16)

    return pl.pallas_call(
        body, name=name, grid=(s // tm, nj),
        in_specs=[pl.BlockSpec((tm, k), lambda i, j: (i, 0)),
                  pl.BlockSpec((1, k), lambda i, j: (0, 0)),
                  w_spec],
        out_specs=[o_spec, pl.BlockSpec((tm, k), lambda i, j: (i, 0))],
        out_shape=[o_sds, jax.ShapeDtypeStruct((s, k), BF16)],
        compiler_params=_params(("parallel", "arbitrary")),
    )(x, g, w)


def _band_bias(r):
    off = np.arange(256)[None, :] - BAND_HALF - np.arange(128)[:, None]
    slopes = np.exp2(-8.0 * np.arange(1, N_HEADS + 1, dtype=np.float32) / N_HEADS).astype(np.float32)
    dist = (np.abs(off) * r).astype(np.float32)
    bias = -slopes[:, None, None] * dist[None]
    bias = np.where((np.abs(off) <= BAND_HALF)[None], bias, np.float32(NEG))
    return jnp.asarray(bias, F32)


def _band_call(mode, r, center, window, bias, name):
    seq = center[0][0].shape[0]
    tq = min(512, seq)
    nsub = tq // 128
    hb = tq // BAND_HALF
    nh = seq // BAND_HALF
    nc, nw = len(center), len(window)
    n_out = {"fwd": 2, "dq": 1, "dkv": 2}[mode]

    def specs(col):
        return (pl.BlockSpec((BAND_HALF, WIDTH_A), lambda c, i: (jnp.maximum(i * hb - 1, 0), col(c))),
                pl.BlockSpec((tq, WIDTH_A), lambda c, i: (i, col(c))),
                pl.BlockSpec((BAND_HALF, WIDTH_A), lambda c, i: (jnp.minimum((i + 1) * hb, nh - 1), col(c))))

    cspec = pl.BlockSpec((tq, WIDTH_A), lambda c, i: (i, c))
    in_specs = [specs(col)[1] for _, col in center]
    operands = [a for a, _ in center]
    for a, col in window:
        in_specs += list(specs(col))
        operands += [a, a, a]
    in_specs.append(pl.BlockSpec((N_HEADS, 128, 256), lambda c, i: (0, 0, 0)))
    operands.append(bias)
    window = [a for a, _ in window]

    def aug_stat(base, stat_sw, lane, act, e0):
        hi, lo = _split_hi_lo(stat_sw)
        return jnp.where(act, base, jnp.where(lane == e0, -hi, jnp.where(lane == e0 + 1, -lo, jnp.zeros_like(hi))))

    def aug_ones(base, lane, e0):
        return jnp.where((lane == e0) | (lane == e0 + 1), jnp.ones_like(base), base)

    def body(*refs):
        c_refs = refs[:nc]
        w_refs = refs[nc:nc + 3 * nw]
        bias_ref = refs[nc + 3 * nw]
        o_refs = refs[nc + 3 * nw + 1:nc + 3 * nw + 1 + n_out]
        wins = refs[nc + 3 * nw + 1 + n_out:]
        i = pl.program_id(1)
        for t in range(nw):
            wins[t][0:BAND_HALF, :] = w_refs[3 * t][...]
            wins[t][BAND_HALF:BAND_HALF + tq, :] = w_refs[3 * t + 1][...]
            wins[t][BAND_HALF + tq:BAND_HALF + tq + BAND_HALF, :] = w_refs[3 * t + 2][...]

        def sub(j, carry):
            r0 = pl.multiple_of(j * 128, 128)
            wpos = i * tq + j * 128 - BAND_HALF + lax.broadcasted_iota(jnp.int32, (128, 256), 1)
            valid = (wpos >= 0) & (wpos < seq)
            lane_c = lax.broadcasted_iota(jnp.int32, (128, 128), 1)
            lane_w = lax.broadcasted_iota(jnp.int32, (256, 128), 1)
            heads = [(p, a) for p in range(4) for a in range(2)]
            first, last_ops = [], []
            for p, a in heads:
                cols = slice(p * 128, (p + 1) * 128)
                cs = [c[pl.ds(r0, 128), cols] for c in c_refs]
                ws = [w[pl.ds(r0, 256), cols] for w in wins]
                e0 = 64 if a == 0 else 0
                act_c = (lane_c < 64) if a == 0 else (lane_c >= 64)
                act_w = (lane_w < 64) if a == 0 else (lane_w >= 64)
                bias_a = bias_ref[2 * p + a]
                if mode == "fwd":
                    qa = jnp.where(act_c, cs[0] * SCALE_A, jnp.zeros_like(cs[0]))
                    first.append((_dot(qa, ws[0], NT) + bias_a, None))
                    last_ops.append((ws[1],))
                elif mode == "dq":
                    q2, dy2, l2, d2 = cs
                    k2, v2 = ws
                    q_aug = aug_stat(q2 * SCALE_A, pltpu.roll(l2, 64, 1), lane_c, act_c, e0)
                    dy_aug = aug_stat(dy2, pltpu.roll(d2, 64, 1), lane_c, act_c, e0)
                    first.append((_dot(q_aug, aug_ones(k2, lane_w, e0), NT) + bias_a,
                                  _dot(dy_aug, aug_ones(v2, lane_w, e0), NT)))
                    last_ops.append((k2,))
                else:
                    k2, v2 = cs
                    q2, dy2, l2, d2 = ws
                    q_aug = aug_stat(q2 * SCALE_A, pltpu.roll(l2, 64, 1), lane_w, act_w, e0)
                    dy_aug = aug_stat(dy2, pltpu.roll(d2, 64, 1), lane_w, act_w, e0)
                    first.append((_dot(aug_ones(k2, lane_c, e0), q_aug, NT) + bias_a,
                                  _dot(aug_ones(v2, lane_c, e0), dy_aug, NT)))
                    last_ops.append((q_aug, dy_aug))
            mid = []
            for sc, dp in first:
                sc = jnp.where(valid, sc, NEG)
                if mode == "fwd":
                    m = jnp.max(sc, axis=-1, keepdims=True)
                    e = jnp.exp(sc - m)
                    l = jnp.sum(e, axis=-1, keepdims=True)
                    mid.append((e.astype(BF16), l, m + jnp.log(l)))
                else:
                    pr = jnp.exp(sc)
                    mid.append((pr.astype(BF16), (pr * dp).astype(BF16)))
            res = []
            for md, ops in zip(mid, last_ops):
                if mode == "fwd":
                    res.append((_dot(md[0], ops[0], NN) / md[1], jnp.broadcast_to(md[2], (128, 128))))
                elif mode == "dq":
                    res.append((_dot(md[1], ops[0], NN) * SCALE_A,))
                else:
                    res.append((_dot(md[1], ops[0], NN), _dot(md[0], ops[1], NN)))
            for t in range(n_out):
                pairs = [jnp.where(lane_c < 64, res[2 * p][t], res[2 * p + 1][t]) for p in range(4)]
                o_refs[t][pl.ds(r0, 128), :] = jnp.concatenate(pairs, axis=1)
            return carry

        lax.fori_loop(0, nsub, sub, 0)

    outs = pl.pallas_call(
        body, name=name, grid=(r, seq // tq),
        in_specs=in_specs,
        out_specs=[cspec] * n_out,
        out_shape=[jax.ShapeDtypeStruct((seq, r * WIDTH_A), F32)] * n_out,
        scratch_shapes=[pltpu.VMEM((tq + 2 * BAND_HALF, WIDTH_A), w.dtype) for w in window],
        compiler_params=_params(("parallel", "parallel")),
    )(*operands)
    return outs


def _slab_scratch(tm, w):
    return pltpu.VMEM((w // 128, tm, 128), F32)


def _put(scr, val):
    for j in range(scr.shape[0]):
        scr[j] = val[:, j * 128:(j + 1) * 128].astype(F32)


def _get(scr):
    return jnp.concatenate([scr[j] for j in range(scr.shape[0])], axis=1)


def _dilate_store(dst_ref, scr, r):
    nb, tm, _ = scr.shape
    w = nb * 128
    for c in range(r):
        for j in range(nb):
            dst_ref[:, c * w + j * 128:c * w + (j + 1) * 128] = scr[j, pl.ds(c, tm // r, stride=r), :].astype(dst_ref.dtype)


def _undilate(scr, src_ref, r):
    nb, tm, _ = scr.shape
    w = nb * 128
    for c in range(r):
        for j in range(nb):
            scr[j, pl.ds(c, tm // r, stride=r), :] = src_ref[:, c * w + j * 128:c * w + (j + 1) * 128].astype(F32)


def _dil_spec(tm, r, w):
    return pl.BlockSpec((tm // r, r * w), lambda i: (i, 0))


def _dil_shape(s, r, w, dtype):
    return jax.ShapeDtypeStruct((s // r, r * w), dtype)


def _in_proj(x, g, w, name):
    s, k = x.shape
    n = w.shape[1]
    tm = min(512, s)
    qkv = 3 * WIDTH_A

    def body(x_ref, g_ref, w_ref, o_ref, h_ref, d4_ref, d16_ref, scr):
        xv = x_ref[...]
        h = (xv * _rstd(xv) * g_ref[...]).astype(BF16)
        h_ref[...] = h
        acc = _dot(h, w_ref[...], NN)
        o_ref[...] = acc.astype(BF16)
        _put(scr, acc[:, 0:qkv])
        _dilate_store(d4_ref, scr, 4)
        _dilate_store(d16_ref, scr, 16)

    row = lambda c: pl.BlockSpec((tm, c), lambda i: (i, 0))
    return pl.pallas_call(
        body, name=name, grid=(s // tm,),
        in_specs=[row(k), pl.BlockSpec((1, k), lambda i: (0, 0)), pl.BlockSpec((k, n), lambda i: (0, 0))],
        out_specs=[row(n), row(k), _dil_spec(tm, 4, qkv), _dil_spec(tm, 16, qkv)],
        out_shape=[jax.ShapeDtypeStruct((s, n), BF16), jax.ShapeDtypeStruct((s, k), BF16),
                   _dil_shape(s, 4, qkv, BF16), _dil_shape(s, 16, qkv, BF16)],
        scratch_shapes=[_slab_scratch(tm, qkv)],
        compiler_params=_params(("parallel",)),
    )(x, g, w)


def _band_combine(os_, lses, name):
    s = os_[0].shape[0]
    tm = min(512, s)

    def body(o1, o4, o16, l1, l4, l16, ya_ref, lse_ref, lse4_ref, lse16_ref, so4, sl4, so16, sl16):
        _undilate(so4, o4, 4)
        _undilate(sl4, l4, 4)
        _undilate(so16, o16, 16)
        _undilate(sl16, l16, 16)
        a0, a1, a2 = l1[...], _get(sl4), _get(sl16)
        m = jnp.maximum(jnp.maximum(a0, a1), a2)
        e0, e1, e2 = jnp.exp(a0 - m), jnp.exp(a1 - m), jnp.exp(a2 - m)
        den = e0 + e1 + e2
        ya_ref[...] = (e0 * o1[...] + e1 * _get(so4) + e2 * _get(so16)) / den
        lse = m + jnp.log(den)
        lse_ref[...] = lse
        _put(sl4, lse)
        _dilate_store(lse4_ref, sl4, 4)
        _dilate_store(lse16_ref, sl4, 16)

    nat = pl.BlockSpec((tm, WIDTH_A), lambda i: (i, 0))
    d4, d16 = _dil_spec(tm, 4, WIDTH_A), _dil_spec(tm, 16, WIDTH_A)
    return pl.pallas_call(
        body, name=name, grid=(s // tm,), in_specs=[nat, d4, d16] * 2, out_specs=[nat, nat, d4, d16],
        out_shape=[jax.ShapeDtypeStruct((s, WIDTH_A), F32)] * 2
        + [_dil_shape(s, 4, WIDTH_A, F32), _dil_shape(s, 16, WIDTH_A, F32)],
        scratch_shapes=[_slab_scratch(tm, WIDTH_A)] * 4,
        compiler_params=_params(("parallel",)),
    )(*os_, *lses)


def _rope_tables(s):
    pos = jnp.arange(s, dtype=F32)
    inv_freq = jnp.exp(-math.log(ROPE_BASE) * jnp.arange(0, QK_ROPE, 2, dtype=F32) / QK_ROPE)
    ang = pos[:, None] * inv_freq[None, :]
    cos, sin = jnp.cos(ang), jnp.sin(ang)
    one = jnp.ones((s, 64), F32)
    zero16 = jnp.zeros((s, 16), F32)
    c = jnp.concatenate([one, cos, cos, jnp.ones((s, 32), F32)], axis=1)
    sa = jnp.concatenate([jnp.zeros((s, 64), F32), -sin, zero16, jnp.zeros((s, 32), F32)], axis=1)
    sb = jnp.concatenate([jnp.zeros((s, 64), F32), zero16, sin, jnp.zeros((s, 32), F32)], axis=1)
    return c, sa, sb


def _rope_fwd(x, c, sa, sb):
    return x * c + pltpu.roll(x, 112, 1) * sa + pltpu.roll(x, 16, 1) * sb


def _rope_bwd(dy, c, sa, sb):
    return dy * c + pltpu.roll(dy * sa, 16, 1) + pltpu.roll(dy * sb, 112, 1)


def _mla_prep(proj, g_q, g_kv, w_uq, w_ukv, tabs, name):
    s = proj.shape[0]
    tm = min(512, s)
    width = N_HEADS * 128

    def body(lat_ref, gq_ref, gkv_ref, wq_ref, wkv_ref, c_ref, sa_ref, sb_ref,
             q_ref, k_ref, kv_ref, cqn_ref, ckvn_ref):
        c, sa, sb = c_ref[...], sa_ref[...], sb_ref[...]
        cq = lat_ref[:, 0:Q_LORA].astype(F32)
        cqn = (cq * _rstd(cq) * gq_ref[...]).astype(BF16)
        cqn_ref[...] = cqn
        q = _dot(cqn, wq_ref[...], NN)
        ckv = lat_ref[:, Q_LORA:Q_LORA + KV_LORA].astype(F32)
        ckvn = (ckv * _rstd(ckv) * gkv_ref[...]).astype(BF16)
        ckvn_ref[...] = ckvn
        kv = _dot(ckvn, wkv_ref[...], NN)
        lane = lax.broadcasted_iota(jnp.int32, (tm, 128), 1)
        krr = _rope_fwd(lat_ref[:, Q_LORA + KV_LORA:].astype(F32), c, sa, sb)
        krr = jnp.where((lane == 96) | (lane == 97), 1.0, krr)
        ones01 = jnp.where(lane < 2, 1.0, 0.0)
        for h in range(N_HEADS):
            cols = slice(h * 128, (h + 1) * 128)
            q_ref[:, cols] = (_rope_fwd(q[:, cols], c, sa, sb) * (MLA_SCALE * LOG2E)).astype(BF16)
            k_ref[:, cols] = jnp.where(lane < 64, kv[:, cols], krr).astype(BF16)
            kv_ref[:, cols] = jnp.where(lane < 64, ones01, kv[:, cols]).astype(BF16)

    row = lambda n: pl.BlockSpec((tm, n), lambda i: (i, 0))
    full = lambda a: pl.BlockSpec(a.shape, lambda i: (0, 0))
    tab = pl.BlockSpec((tm, 128), lambda i: (i, 0))
    return pl.pallas_call(
        body, name=name, grid=(s // tm,),
        in_specs=[pl.BlockSpec((tm, N_LAT), lambda i: (i, COL_CQ // N_LAT)),
                  full(g_q), full(g_kv), full(w_uq), full(w_ukv), tab, tab, tab],
        out_specs=[row(width), row(width), row(width), row(Q_LORA), row(KV_LORA)],
        out_shape=[jax.ShapeDtypeStruct((s, width), BF16)] * 3
        + [jax.ShapeDtypeStruct((s, Q_LORA), BF16), jax.ShapeDtypeStruct((s, KV_LORA), BF16)],
        compiler_params=_params(("parallel",)),
    )(proj, g_q, g_kv, w_uq, w_ukv, *tabs)


def _mla_fwd(qcat, kcat, kvb, name):
    s = qcat.shape[0]
    tq = min(512, s)
    tk = min(1024, s)
    nkc = s // tk

    def body(q_ref, k_ref, v_ref, yb_ref, qaug_ref, m_ref, acc_ref):
        lane = lax.broadcasted_iota(jnp.int32, (tq, 128), 1)
        m_ref[...] = jnp.full((2, tq, 128), NEG, F32)
        acc_ref[...] = jnp.zeros((2, tq, 128), F32)

        def chunk(cidx, carry):
            k0 = pl.multiple_of(cidx * tk, tk)
            cols = [slice(a * 128, (a + 1) * 128) for a in range(2)]
            scs = [_dot(q_ref[:, c], k_ref[pl.ds(k0, tk), c], NT) for c in cols]
            prs, alphas = [], []
            for a, sc in enumerate(scs):
                m_prev = m_ref[a]
                m_new = jnp.maximum(m_prev, jnp.max(sc, axis=-1, keepdims=True))
                alphas.append(jnp.exp2(m_prev - m_new))
                prs.append(jnp.exp2(sc - jnp.tile(m_new, (1, tk // 128))).astype(BF16))
                m_ref[a] = m_new
            for a, c in enumerate(cols):
                acc_ref[a] = alphas[a] * acc_ref[a] + _dot(prs[a], v_ref[pl.ds(k0, tk), c], NN)
            return carry

        lax.fori_loop(0, nkc, chunk, 0)
        outs = []
        for a in range(2):
            cols = slice(a * 128, (a + 1) * 128)
            acc = acc_ref[a]
            l = acc[:, 0:1]
            outs.append(acc / l)
            hi, lo = _split_hi_lo(m_ref[a] + jnp.log(l) * LOG2E)
            qaug_ref[:, cols] = jnp.where(lane == 96, -hi, jnp.where(lane == 97, -lo, q_ref[:, cols]))
        yb_ref[...] = jnp.where(lane < 64, pltpu.roll(outs[0], 64, 1), outs[1])

    return pl.pallas_call(
        body, name=name, grid=(4, s // tq),
        in_specs=[pl.BlockSpec((tq, 256), lambda p, i: (i, p)),
                  pl.BlockSpec((s, 256), lambda p, i: (0, p)),
                  pl.BlockSpec((s, 256), lambda p, i: (0, p))],
        out_specs=[pl.BlockSpec((tq, 128), lambda p, i: (i, p)),
                   pl.BlockSpec((tq, 256), lambda p, i: (i, p))],
        out_shape=[jax.ShapeDtypeStruct((s, WIDTH_A), F32), jax.ShapeDtypeStruct((s, N_HEADS * 128), BF16)],
        scratch_shapes=[pltpu.VMEM((2, tq, 128), F32)] * 2,
        compiler_params=_params(("parallel", "parallel")),
    )(qcat, kcat, kvb)


def _mla_bwd(qaug, kcat, kvb, doaug, name):
    s = qaug.shape[0]
    tq = min(512, s)
    tk = min(512, s)
    nqc = s // tq
    width = N_HEADS * 128

    def body(q_ref, do_ref, k_ref, v_ref, dq_ref, dk_ref, dv_ref, dk_acc, dv_acc):
        j = pl.program_id(1)

        @pl.when(j == 0)
        def _():
            dq_ref[...] = jnp.zeros_like(dq_ref)

        dk_acc[...] = jnp.zeros_like(dk_acc)
        dv_acc[...] = jnp.zeros_like(dv_acc)

        def chunk(cidx, carry):
            q0 = pl.multiple_of(cidx * tq, tq)
            cols = [slice(a * 128, (a + 1) * 128) for a in range(2)]
            qs = [q_ref[pl.ds(q0, tq), c] for c in cols]
            dos = [do_ref[pl.ds(q0, tq), c] for c in cols]
            kbs = [k_ref[:, c] for c in cols]
            sts = [_dot(kbs[a], qs[a], NT) for a in range(2)]
            dps = [_dot(v_ref[:, cols[a]], dos[a], NT) for a in range(2)]
            pts, dsts = [], []
            for a in range(2):
                pt = jnp.exp2(sts[a])
                pts.append(pt.astype(BF16))
                dsts.append((pt * dps[a]).astype(BF16))
            for a, c in enumerate(cols):
                dv_acc[:, c] += _dot(pts[a], dos[a], NN)
                dk_acc[:, c] += _dot(dsts[a], qs[a], NN)
                dq_ref[pl.ds(q0, tq), c] += _dot(dsts[a], kbs[a], TN)
            return carry

        lax.fori_loop(0, nqc, chunk, 0)
        dk_ref[...] = dk_acc[...] * LN2
        dv_ref[...] = dv_acc[...]

    return pl.pallas_call(
        body, name=name, grid=(N_HEADS // 2, s // tk),
        in_specs=[pl.BlockSpec((s, 256), lambda p, j: (0, p)),
                  pl.BlockSpec((s, 256), lambda p, j: (0, p)),
                  pl.BlockSpec((tk, 256), lambda p, j: (j, p)),
                  pl.BlockSpec((tk, 256), lambda p, j: (j, p))],
        out_specs=[pl.BlockSpec((s, 256), lambda p, j: (0, p)),
                   pl.BlockSpec((tk, 256), lambda p, j: (j, p)),
                   pl.BlockSpec((tk, 256), lambda p, j: (j, p))],
        out_shape=[jax.ShapeDtypeStruct((s, width), F32)] * 3,
        scratch_shapes=[pltpu.VMEM((tk, 256), F32)] * 2,
        compiler_params=_params(("parallel", "arbitrary")),
    )(qaug, doaug, kcat, kvb)


def _mla_prep_bwd(dqc, dkc, dvp, proj, cqn, ckvn, g_q, g_kv, w_uq, w_ukv, tabs, name):
    s = proj.shape[0]
    tm = min(256, s)
    width = N_HEADS * 128
    n_out_cols = N_LAT

    def body(dq_ref, dk_ref, dv_ref, lat_ref, cqn_ref, ckvn_ref, gq_ref, gkv_ref, wq_ref, wkv_ref,
             c_ref, sa_ref, sb_ref, dproj_ref, dwq_ref, dwkv_ref, dgq_ref, dgkv_ref):
        i = pl.program_id(0)

        @pl.when(i == 0)
        def _():
            dwq_ref[...] = jnp.zeros_like(dwq_ref)
            dwkv_ref[...] = jnp.zeros_like(dwkv_ref)
            dgq_ref[...] = jnp.zeros_like(dgq_ref)
            dgkv_ref[...] = jnp.zeros_like(dgkv_ref)

        c, sa, sb = c_ref[...], sa_ref[...], sb_ref[...]
        lane = lax.broadcasted_iota(jnp.int32, (tm, 128), 1)
        dkr = jnp.zeros((tm, 128), F32)
        dq_parts, dkv_parts = [], []
        for h in range(N_HEADS):
            cols = slice(h * 128, (h + 1) * 128)
            dq_parts.append(_rope_bwd(dq_ref[:, cols] * MLA_SCALE, c, sa, sb).astype(BF16))
            dkh = dk_ref[:, cols]
            dkr = dkr + dkh
            dkv_parts.append(jnp.where(lane < 64, dkh, dv_ref[:, cols]).astype(BF16))
        dq = jnp.concatenate(dq_parts, axis=1)
        dkv = jnp.concatenate(dkv_parts, axis=1)
        dkr = _rope_bwd(jnp.where((lane >= 64) & (lane < 96), dkr, 0.0), c, sa, sb)

        dcqn = _dot(dq, wq_ref[...], NT)
        dwq_ref[...] += _dot(cqn_ref[...], dq, TN)
        dcq, dgq = _rms_bwd(dcqn, lat_ref[:, 0:Q_LORA].astype(F32), gq_ref[...])
        dgq_ref[...] += jnp.sum(dgq, axis=0, keepdims=True)

        dckvn = _dot(dkv, wkv_ref[...], NT)
        dwkv_ref[...] += _dot(ckvn_ref[...], dkv, TN)
        dckv, dgkv = _rms_bwd(dckvn, lat_ref[:, Q_LORA:Q_LORA + KV_LORA].astype(F32), gkv_ref[...])
        dgkv_ref[...] += jnp.sum(dgkv, axis=0, keepdims=True)

        dproj_ref[:, 0:Q_LORA] = dcq.astype(BF16)
        dproj_ref[:, Q_LORA:Q_LORA + KV_LORA] = dckv.astype(BF16)
        dproj_ref[:, Q_LORA + KV_LORA:] = dkr.astype(BF16)

    row = lambda n: pl.BlockSpec((tm, n), lambda i: (i, 0))
    full = lambda a: pl.BlockSpec(a.shape, lambda i: (0, 0))
    tab = pl.BlockSpec((tm, 128), lambda i: (i, 0))
    return pl.pallas_call(
        body, name=name, grid=(s // tm,),
        in_specs=[row(width), row(width), row(width),
                  pl.BlockSpec((tm, N_LAT), lambda i: (i, COL_CQ // N_LAT)),
                  row(Q_LORA), row(KV_LORA), full(g_q), full(g_kv), full(w_uq), full(w_ukv), tab, tab, tab],
        out_specs=[row(n_out_cols), full(w_uq), full(w_ukv), full(g_q), full(g_kv)],
        out_shape=[jax.ShapeDtypeStruct((s, n_out_cols), BF16),
                   jax.ShapeDtypeStruct(w_uq.shape, F32), jax.ShapeDtypeStruct(w_ukv.shape, F32),
                   jax.ShapeDtypeStruct(g_q.shape, F32), jax.ShapeDtypeStruct(g_kv.shape, F32)],
        compiler_params=_params(("arbitrary",)),
    )(dqc, dkc, dvp, proj, cqn, ckvn, g_q, g_kv, w_uq, w_ukv, *tabs)


def _mix_out(ya, yb, na, nb, w_o, g_post, x, name):
    s = x.shape[0]
    tm = min(512, s)

    def body(ya_ref, yb_ref, na_ref, nb_ref, w_ref, g_ref, x_ref, yn_ref, y2_ref, x1_ref):
        a, b = ya_ref[...], yb_ref[...]
        yn = jnp.concatenate([a * _rstd(a) * na_ref[...], b * _rstd(b) * nb_ref[...]], axis=1).astype(BF16)
        yn_ref[...] = yn
        y2 = _dot(yn, w_ref[...], NN)
        y2_ref[...] = y2
        x1_ref[...] = x_ref[...] + y2 * _rstd(y2) * g_ref[...]

    row = lambda n: pl.BlockSpec((tm, n), lambda i: (i, 0))
    full = lambda a: pl.BlockSpec(a.shape, lambda i: (0, 0))
    return pl.pallas_call(
        body, name=name, grid=(s // tm,),
        in_specs=[row(WIDTH_A), row(WIDTH_A), full(na), full(nb), full(w_o), full(g_post), row(D_MODEL)],
        out_specs=[row(D_MODEL)] * 3,
        out_shape=[jax.ShapeDtypeStruct((s, D_MODEL), BF16), jax.ShapeDtypeStruct((s, D_MODEL), F32),
                   jax.ShapeDtypeStruct((s, D_MODEL), F32)],
        compiler_params=_params(("parallel",)),
    )(ya, yb, na, nb, w_o, g_post, x)


def _head_ones():
    blk = np.kron(np.eye(N_HEADS, dtype=np.float32), np.ones((64, 64), np.float32))
    return jnp.asarray(blk, F32)


def _outnorm_bwd(dyn, ya, yb, na, nb, ones, name):
    s = ya.shape[0]
    tm = min(256, s)

    def body(dyn_ref, ya_ref, yb_ref, na_ref, nb_ref, ones_ref, dya_ref, da_ref, do_ref, dna_ref, dnb_ref,
             dya4_ref, dya16_ref, da4_ref, da16_ref, scr):
        i = pl.program_id(0)

        @pl.when(i == 0)
        def _():
            dna_ref[...] = jnp.zeros_like(dna_ref)
            dnb_ref[...] = jnp.zeros_like(dnb_ref)

        a, b = ya_ref[...], yb_ref[...]
        dya, dna = _rms_bwd(dyn_ref[:, 0:WIDTH_A], a, na_ref[...])
        dyb, dnb = _rms_bwd(dyn_ref[:, WIDTH_A:], b, nb_ref[...])
        dna_ref[...] += jnp.sum(dna, axis=0, keepdims=True)
        dnb_ref[...] += jnp.sum(dnb, axis=0, keepdims=True)
        dya_b = dya.astype(BF16)
        dya_ref[...] = dya_b
        hp = lax.Precision.HIGHEST
        delta_a = jnp.dot(dya_b.astype(F32) * a, ones_ref[...], precision=hp, preferred_element_type=F32)
        da_ref[...] = delta_a
        _put(scr, dya_b)
        _dilate_store(dya4_ref, scr, 4)
        _dilate_store(dya16_ref, scr, 16)
        _put(scr, delta_a)
        _dilate_store(da4_ref, scr, 4)
        _dilate_store(da16_ref, scr, 16)
        dyb_b = dyb.astype(BF16)
        db = jnp.dot(dyb_b.astype(F32) * b, ones_ref[...], precision=hp, preferred_element_type=F32)
        lane = lax.broadcasted_iota(jnp.int32, (tm, 128), 1)
        zero = jnp.zeros((tm, 128), BF16)
        for p in range(4):
            cols = slice(p * 128, (p + 1) * 128)
            dyp = dyb_b[:, cols]
            dbp = db[:, cols]
            for a_ in range(2):
                src = pltpu.roll(dyp.astype(F32), 64, 1).astype(BF16) if a_ == 0 else dyp
                dlt = dbp if a_ == 0 else pltpu.roll(dbp, 64, 1)
                hi, lo = _split_hi_lo(dlt)
                blk = jnp.where(lane >= 64, src, jnp.where(lane == 0, -hi, jnp.where(lane == 1, -lo, zero)))
                h = 2 * p + a_
                do_ref[:, h * 128:(h + 1) * 128] = blk

    row = lambda n: pl.BlockSpec((tm, n), lambda i: (i, 0))
    full = lambda a: pl.BlockSpec(a.shape, lambda i: (0, 0))
    return pl.pallas_call(
        body, name=name, grid=(s // tm,),
        in_specs=[row(D_MODEL), row(WIDTH_A), row(WIDTH_A), full(na), full(nb), full(ones)],
        out_specs=[row(WIDTH_A), row(WIDTH_A), row(N_HEADS * 128), full(na), full(nb),
                   _dil_spec(tm, 4, WIDTH_A), _dil_spec(tm, 16, WIDTH_A), _dil_spec(tm, 4, WIDTH_A), _dil_spec(tm, 16, WIDTH_A)],
        out_shape=[jax.ShapeDtypeStruct((s, WIDTH_A), BF16), jax.ShapeDtypeStruct((s, WIDTH_A), F32),
                   jax.ShapeDtypeStruct((s, N_HEADS * 128), BF16),
                   jax.ShapeDtypeStruct(na.shape, F32), jax.ShapeDtypeStruct(nb.shape, F32),
                   _dil_shape(s, 4, WIDTH_A, BF16), _dil_shape(s, 16, WIDTH_A, BF16),
                   _dil_shape(s, 4, WIDTH_A, F32), _dil_shape(s, 16, WIDTH_A, F32)],
        scratch_shapes=[_slab_scratch(tm, WIDTH_A)],
        compiler_params=_params(("arbitrary",)),
    )(dyn, ya, yb, na, nb, ones)


def _sum_cast(parts, name):
    s = parts[0][0].shape[0]
    tm = min(512, s)

    def body(*refs):
        o_ref, s4, s16 = refs[9:]
        for t in range(3):
            _undilate(s4, refs[3 + t], 4)
            _undilate(s16, refs[6 + t], 16)
            acc = refs[t][...] + _get(s4) + _get(s16)
            o_ref[:, t * WIDTH_A:(t + 1) * WIDTH_A] = acc.astype(BF16)

    nat = pl.BlockSpec((tm, WIDTH_A), lambda i: (i, 0))
    flat = [parts[g][t] for g in range(3) for t in range(3)]
    return pl.pallas_call(
        body, name=name, grid=(s // tm,),
        in_specs=[nat] * 3 + [_dil_spec(tm, 4, WIDTH_A)] * 3 + [_dil_spec(tm, 16, WIDTH_A)] * 3,
        out_specs=pl.BlockSpec((tm, 3 * WIDTH_A), lambda i: (i, 0)),
        out_shape=jax.ShapeDtypeStruct((s, 3 * WIDTH_A), BF16),
        scratch_shapes=[_slab_scratch(tm, WIDTH_A)] * 2,
        compiler_params=_params(("parallel",)),
    )(*flat)


HALO = 16


def _gelu(x):
    k = math.sqrt(2.0 / math.pi)
    t = jnp.tanh(k * (x + 0.044715 * x * x * x))
    return 0.5 * x * (1.0 + t), t


def _gelu_grad(x, t):
    k = math.sqrt(2.0 / math.pi)
    return 0.5 * (1.0 + t) + 0.5 * x * (1.0 - t * t) * k * (1.0 + 3 * 0.044715 * x * x)


def _halo_specs(s, tm, tn, lead):
    nb = s // HALO
    hb = tm // HALO
    pre = (lead,) if lead else ()
    z = (0,) if lead else ()
    main = pl.BlockSpec(pre + (tm, tn), lambda j, i: z + (i, j))
    prev = pl.BlockSpec(pre + (HALO, tn), lambda j, i: z + (jnp.maximum(i * hb - 1, 0), j))
    nxt = pl.BlockSpec(pre + (HALO, tn), lambda j, i: z + (jnp.minimum((i + 1) * hb, nb - 1), j))
    return [prev, main, nxt]


def _fill_ext(ext, prev, main, nxt, i, tm, s):
    ext[0:HALO, :] = jnp.where(i > 0, prev.astype(F32), 0.0)
    ext[HALO:HALO + tm, :] = main.astype(F32)
    ext[HALO + tm:2 * HALO + tm, :] = jnp.where((i + 1) * tm < s, nxt.astype(F32), 0.0)


STRIP = 16


def _shifted(ref, row0):
    n = STRIP + 16
    win = ref[pl.ds(pl.multiple_of(row0 - 8, 8), n), :]
    return pltpu.roll(win, 1, 0)[8:8 + STRIP], win[8:8 + STRIP], pltpu.roll(win, n - 1, 0)[8:8 + STRIP]


def _conv3(e, row0, w_ref, b_ref, t):
    m1, c0, p1 = _shifted(e, row0)
    return w_ref[t, 0:1, :] * m1 + w_ref[t, 1:2, :] * c0 + w_ref[t, 2:3, :] * p1 + b_ref[t]


def _conv_gate(up, cw, cb, name):
    _, s, c = up.shape
    tm = min(512, s)
    tn = FF_SLAB

    def body(up_p, up_m, up_n, w_ref, b_ref, a_ref, eg, ev):
        i = pl.program_id(1)
        _fill_ext(eg, up_p[0], up_m[0], up_n[0], i, tm, s)
        _fill_ext(ev, up_p[1], up_m[1], up_n[1], i, tm, s)

        def strip(t, carry):
            r0 = pl.multiple_of(t * STRIP, STRIP)
            g, _ = _gelu(_conv3(eg, HALO + r0, w_ref, b_ref, 0))
            a_ref[pl.ds(r0, STRIP), :] = (g * _conv3(ev, HALO + r0, w_ref, b_ref, 1)).astype(BF16)
            return carry

        lax.fori_loop(0, tm // STRIP, strip, 0)

    return pl.pallas_call(
        body, name=name, grid=(c // tn, s // tm),
        in_specs=_halo_specs(s, tm, tn, 2)
        + [pl.BlockSpec((2, 3, tn), lambda j, i: (0, 0, j)), pl.BlockSpec((2, 1, tn), lambda j, i: (0, 0, j))],
        out_specs=pl.BlockSpec((tm, tn), lambda j, i: (i, j)),
        out_shape=jax.ShapeDtypeStruct((s, c), BF16),
        scratch_shapes=[pltpu.VMEM((tm + 2 * HALO, tn), F32)] * 2,
        compiler_params=_params(("parallel", "parallel")),
    )(up, up, up, cw, cb)


def _conv_gate_bwd(up, da, cw, cb, name):
    _, s, c = up.shape
    tm = min(256, s)
    tn = FF_SLAB
    te = tm + HALO

    def body(up_p, up_m, up_n, da_p, da_m, da_n, w_ref, b_ref, dup_ref, dw_ref, db_ref, eg, ev, ed, dug, duv):
        i = pl.program_id(1)

        @pl.when(i == 0)
        def _():
            dw_ref[...] = jnp.zeros_like(dw_ref)
            db_ref[...] = jnp.zeros_like(db_ref)

        _fill_ext(eg, up_p[0], up_m[0], up_n[0], i, tm, s)
        _fill_ext(ev, up_p[1], up_m[1], up_n[1], i, tm, s)
        _fill_ext(ed, da_p[...], da_m[...], da_n[...], i, tm, s)
        o = HALO // 2

        def du_strip(t, carry):
            r0 = pl.multiple_of(t * STRIP, STRIP)
            ug = _conv3(eg, o + r0, w_ref, b_ref, 0)
            uv = _conv3(ev, o + r0, w_ref, b_ref, 1)
            gl, th = _gelu(ug)
            pos = i * tm - o + r0 + lax.broadcasted_iota(jnp.int32, (STRIP, 1), 0)
            dav = jnp.where((pos >= 0) & (pos < s), ed[pl.ds(pl.multiple_of(o + r0, 8), STRIP), :], 0.0)
            dug[pl.ds(r0, STRIP), :] = dav * uv * _gelu_grad(ug, th)
            duv[pl.ds(r0, STRIP), :] = dav * gl
            return carry

        lax.fori_loop(0, te // STRIP, du_strip, 0)

        def back(du, e, t):
            def strip(k, acc):
                r0 = pl.multiple_of(k * STRIP, STRIP)
                dm1, c0, dp1 = _shifted(du, o + r0)
                dup_ref[t, pl.ds(r0, STRIP), :] = (w_ref[t, 0:1, :] * dp1 + w_ref[t, 1:2, :] * c0
                                                   + w_ref[t, 2:3, :] * dm1).astype(BF16)
                um1, u0, up1 = _shifted(e, HALO + r0)
                fold = lambda a: a[0:8] + a[8:16]
                return (acc[0] + fold(um1 * c0), acc[1] + fold(u0 * c0), acc[2] + fold(up1 * c0), acc[3] + fold(c0))

            zero = jnp.zeros((8, tn), F32)
            acc = lax.fori_loop(0, tm // STRIP, strip, (zero, zero, zero, zero))
            for k in range(3):
                dw_ref[t, k:k + 1, :] += jnp.sum(acc[k], axis=0, keepdims=True)
            db_ref[t] += jnp.sum(acc[3], axis=0, keepdims=True)

        back(dug, eg, 0)
        back(duv, ev, 1)

    wspec = pl.BlockSpec((2, 3, tn), lambda j, i: (0, 0, j))
    bspec = pl.BlockSpec((2, 1, tn), lambda j, i: (0, 0, j))
    return pl.pallas_call(
        body, name=name, grid=(c // tn, s // tm),
        in_specs=_halo_specs(s, tm, tn, 2) + _halo_specs(s, tm, tn, 0) + [wspec, bspec],
        out_specs=[pl.BlockSpec((2, tm, tn), lambda j, i: (0, i, j)), wspec, bspec],
        out_shape=[jax.ShapeDtypeStruct((2, s, c), BF16), jax.ShapeDtypeStruct((2, 3, c), F32),
                   jax.ShapeDtypeStruct((2, 1, c), F32)],
        scratch_shapes=[pltpu.VMEM((tm + 2 * HALO, tn), F32)] * 3 + [pltpu.VMEM((te, tn), F32)] * 2,
        compiler_params=_params(("parallel", "arbitrary")),
    )(up, up, up, da, da, da, cw, cb)


def _ffn_out(a, w_down, g_post, x1, target, name):
    s = x1.shape[0]
    tm = min(256, s)

    def body(a_ref, w_ref, g_ref, x1_ref, t_ref, dy3_ref, dx2_ref, loss_ref, dg_ref):
        i = pl.program_id(0)

        @pl.when(i == 0)
        def _():
            loss_ref[...] = jnp.zeros_like(loss_ref)
            dg_ref[...] = jnp.zeros_like(dg_ref)

        y3 = _dot(a_ref[...], w_ref[...], NN)
        g = g_ref[...]
        x2 = x1_ref[...] + y3 * _rstd(y3) * g
        diff = x2 - t_ref[...]
        loss_ref[...] += jnp.sum(jnp.sum(diff * diff, axis=1, keepdims=True), axis=0, keepdims=True)
        dx2 = diff * (1.0 / D_MODEL)
        dx2_ref[...] = dx2
        dy3, dg = _rms_bwd(dx2, y3, g)
        dy3_ref[...] = dy3.astype(BF16)
        dg_ref[...] += jnp.sum(dg, axis=0, keepdims=True)

    row = lambda n: pl.BlockSpec((tm, n), lambda i: (i, 0))
    full = lambda t: pl.BlockSpec(t.shape, lambda i: (0, 0))
    return pl.pallas_call(
        body, name=name, grid=(s // tm,),
        in_specs=[row(a.shape[1]), full(w_down), full(g_post), row(D_MODEL), row(D_MODEL)],
        out_specs=[row(D_MODEL), row(D_MODEL), pl.BlockSpec((8, 128), lambda i: (0, 0)), full(g_post)],
        out_shape=[jax.ShapeDtypeStruct((s, D_MODEL), BF16), jax.ShapeDtypeStruct((s, D_MODEL), F32),
                   jax.ShapeDtypeStruct((8, 128), F32), jax.ShapeDtypeStruct(g_post.shape, F32)],
        compiler_params=_params(("arbitrary",)),
    )(a, w_down, g_post, x1, target)


def _resnorm_bwd(dh2, x1, g_ffn_pre, dx2, y2, g_mix_post, name):
    s = x1.shape[0]
    tm = min(256, s)

    def body(dh_ref, x1_ref, gf_ref, dx2_ref, y2_ref, gp_ref, dx1_ref, dy2_ref, dgf_ref, dgp_ref):
        i = pl.program_id(0)

        @pl.when(i == 0)
        def _():
            dgf_ref[...] = jnp.zeros_like(dgf_ref)
            dgp_ref[...] = jnp.zeros_like(dgp_ref)

        dn, dgf = _rms_bwd(dh_ref[...], x1_ref[...], gf_ref[...])
        dx1 = dx2_ref[...] + dn
        dx1_ref[...] = dx1
        dgf_ref[...] += jnp.sum(dgf, axis=0, keepdims=True)
        dy2, dgp = _rms_bwd(dx1, y2_ref[...], gp_ref[...])
        dy2_ref[...] = dy2.astype(BF16)
        dgp_ref[...] += jnp.sum(dgp, axis=0, keepdims=True)

    row = pl.BlockSpec((tm, D_MODEL), lambda i: (i, 0))
    full = pl.BlockSpec((1, D_MODEL), lambda i: (0, 0))
    return pl.pallas_call(
        body, name=name, grid=(s // tm,),
        in_specs=[row, row, full, row, row, full],
        out_specs=[row, row, full, full],
        out_shape=[jax.ShapeDtypeStruct((s, D_MODEL), F32), jax.ShapeDtypeStruct((s, D_MODEL), BF16),
                   jax.ShapeDtypeStruct((1, D_MODEL), F32), jax.ShapeDtypeStruct((1, D_MODEL), F32)],
        compiler_params=_params(("arbitrary",)),
    )(dh2, x1, g_ffn_pre, dx2, y2, g_mix_post)


def _final_bwd(dh1, x, g_pre, dx1, name):
    s = x.shape[0]
    tm = min(256, s)

    def body(dh_ref, x_ref, g_ref, dx1_ref, dx_ref, dg_ref):
        @pl.when(pl.program_id(0) == 0)
        def _():
            dg_ref[...] = jnp.zeros_like(dg_ref)

        dn, dg = _rms_bwd(dh_ref[...], x_ref[...], g_ref[...])
        dx_ref[...] = dx1_ref[...] + dn
        dg_ref[...] += jnp.sum(dg, axis=0, keepdims=True)

    row = pl.BlockSpec((tm, D_MODEL), lambda i: (i, 0))
    full = pl.BlockSpec((1, D_MODEL), lambda i: (0, 0))
    return pl.pallas_call(
        body, name=name, grid=(s // tm,),
        in_specs=[row, row, full, row], out_specs=[row, full],
        out_shape=[jax.ShapeDtypeStruct((s, D_MODEL), F32), jax.ShapeDtypeStruct((1, D_MODEL), F32)],
        compiler_params=_params(("arbitrary",)),
    )(dh1, x, g_pre, dx1)


def _local_step(x, target, fw, rep):
    s = x.shape[0]
    tabs = _rope_tables(s)
    w_in, w_uq, w_ukv, w_o, w_up, w_down = (fw[n] for n in ("w_in", "w_uq", "w_ukv", "w_o", "w_up", "w_down"))
    cw, cb = fw["conv_w"], fw["conv_b"]
    tr = min(512, s)
    tcon = min(512, s)
    ff = w_down.shape[0]

    proj, h1, qkv4, qkv16 = _in_proj(x, rep["norm_mix_pre"], w_in, "in_proj")
    qkv = {1: proj, 4: qkv4, 16: qkv16}
    q_of = lambda r: (qkv[r], lambda c: 3 * c)
    k_of = lambda r: (qkv[r], lambda c: 3 * c + 1)
    v_of = lambda r: (qkv[r], lambda c: 3 * c + 2)
    own = lambda a: (a, lambda c: c)
    biases = [_band_bias(r) for _, r in DIL_CONFIGS]
    os_, lses = [], []
    for g, (_, r) in enumerate(DIL_CONFIGS):
        o, l = _band_call("fwd", r, [q_of(r)], [k_of(r), v_of(r)], biases[g], f"band_fwd_r{r}")
        os_.append(o)
        lses.append(l)
    ya, lse_a, lse4, lse16 = _band_combine(os_, lses, "band_combine")
    qcat, kcat, kvb, cqn, ckvn = _mla_prep(proj, rep["q_lat_norm"], rep["kv_lat_norm"], w_uq, w_ukv, tabs, "mla_prep")
    yb, qaug = _mla_fwd(qcat, kcat, kvb, "mla_fwd")
    yn, y2, x1 = _mix_out(ya, yb, rep["out_norm_a"], rep["out_norm_b"], w_o, rep["norm_mix_post"], x, "mix_out")
    up, h2 = _norm_matmul(x1, rep["norm_ffn_pre"], w_up, None, "up_proj")
    act = _conv_gate(up, cw, cb, "conv_gate")
    dy3, dx2, loss_acc, dg_ffn_post = _ffn_out(act, w_down, rep["norm_ffn_post"], x1, target, "ffn_out")

    grads = {"norm_ffn_post": dg_ffn_post}
    dact = _matmul(dy3, w_down, "nt", BF16, tr, ff // 2, D_MODEL, "d_act")
    grads["w_down"] = _matmul(act, dy3, "tn", F32, ff // 2, D_MODEL, tcon, "dw_down")
    dup, grads["conv_w"], grads["conv_b"] = _conv_gate_bwd(up, dact, cw, cb, "conv_gate_bwd")
    half = N_DEV // 2
    dh2 = _matmul_core(
        dup, w_up, NT, (s // tr, 1, N_DEV),
        pl.BlockSpec((None, tr, FF_SLAB), lambda i, j, t: (t // half, i, t % half)),
        pl.BlockSpec((None, D_MODEL, FF_SLAB), lambda i, j, t: (t, 0, 0)),
        pl.BlockSpec((tr, D_MODEL), lambda i, j, t: (i, 0)),
        jax.ShapeDtypeStruct((s, D_MODEL), F32), (tr, D_MODEL), "d_h2")
    grads["w_up"] = _matmul_core(
        h2, dup, TN, (1, N_DEV, s // tcon),
        pl.BlockSpec((tcon, D_MODEL), lambda i, j, t: (t, 0)),
        pl.BlockSpec((None, tcon, FF_SLAB), lambda i, j, t: (j // half, t, j % half)),
        pl.BlockSpec((None, D_MODEL, FF_SLAB), lambda i, j, t: (j, 0, 0)),
        jax.ShapeDtypeStruct((N_DEV, D_MODEL, FF_SLAB), F32), (D_MODEL, FF_SLAB), "dw_up")
    dx1, dy2, grads["norm_ffn_pre"], grads["norm_mix_post"] = _resnorm_bwd(
        dh2, x1, rep["norm_ffn_pre"], dx2, y2, rep["norm_mix_post"], "resnorm_bwd")
    dyn = _matmul(dy2, w_o, "nt", F32, tr, D_MODEL, D_MODEL, "d_yn")
    grads["w_o"] = _matmul(yn, dy2, "tn", F32, D_MODEL, D_MODEL, tcon, "dw_o")
    dya, delta_a, doaug, grads["out_norm_a"], grads["out_norm_b"], dya4, dya16, delta4, delta16 = _outnorm_bwd(
        dyn, ya, yb, rep["out_norm_a"], rep["out_norm_b"], _head_ones(), "outnorm_bwd")
    stats = {1: (dya, lse_a, delta_a), 4: (dya4, lse4, delta4), 16: (dya16, lse16, delta16)}
    parts = []
    for g, (_, r) in enumerate(DIL_CONFIGS):
        qside = [q_of(r)] + [own(a) for a in stats[r]]
        kside = [k_of(r), v_of(r)]
        (dq,) = _band_call("dq", r, qside, kside, biases[g], f"band_dq_r{r}")
        dk, dv = _band_call("dkv", r, kside, qside, biases[g], f"band_dkv_r{r}")
        parts.append((dq, dk, dv))
    dproj_a = _sum_cast(parts, "band_grad_sum")
    dqc, dkc, dvp = _mla_bwd(qaug, kcat, kvb, doaug, "mla_bwd")
    dproj_b, grads["w_uq"], grads["w_ukv"], grads["q_lat_norm"], grads["kv_lat_norm"] = _mla_prep_bwd(
        dqc, dkc, dvp, proj, cqn, ckvn, rep["q_lat_norm"], rep["kv_lat_norm"], w_uq, w_ukv, tabs, "mla_prep_bwd")
    dproj = jnp.concatenate([dproj_a, dproj_b], axis=1)
    dh1 = _matmul(dproj, w_in, "nt", F32, tr, D_MODEL, N_PROJ // 2, "d_h1")
    grads["w_in"] = _matmul(h1, dproj, "tn", F32, D_MODEL, N_PROJ // 2, tcon, "dw_in")
    grad_x, grads["norm_mix_pre"] = _final_bwd(dh1, x, rep["norm_mix_pre"], dx1, "final_bwd")
    loss = 0.5 / D_MODEL * loss_acc[0, 0]
    return loss, grad_x, grads


MESH = pl.DeviceIdType.MESH
HBM_SPEC = pl.BlockSpec(memory_space=pltpu.HBM)
SMALL_ROWS = 96
BUF_SHAPES = {"w_up": (D_MODEL, FF_SLAB), "w_in": (D_MODEL, 384), "w_down": (D_FF // N_DEV, D_MODEL),
              "w_o": (D_MODEL // N_DEV, D_MODEL), "w_uq": (Q_LORA, 128), "w_ukv": (KV_LORA, 128), "conv_w": (8, FF_SLAB)}
BUF_ORDER = tuple(BUF_SHAPES)


def _all_gather(bufs, name):
    nb = len(bufs)

    def body(*refs):
        x_refs, out_refs = refs[:nb], refs[nb:2 * nb]
        send_sems, recv_sems, local_sems = refs[2 * nb:]
        x, y, c = lax.axis_index("x"), lax.axis_index("y"), lax.axis_index("c")
        me, sibling = (x, y, c), (x, y, 1 - c)
        chips = [(1 - x, y), (x, 1 - y), (1 - x, 1 - y)]

        def copy(b, k, block, to, own=False):
            px, py, pc = block
            slot = out_refs[b].at[4 * px + 2 * py + pc]
            return pltpu.make_async_remote_copy(
                src_ref=x_refs[b] if own else slot, dst_ref=slot,
                send_sem=send_sems.at[7 * b + k], recv_sem=recv_sems.at[7 * b + k], device_id=to, device_id_type=MESH)

        mine = [pltpu.make_async_copy(x_refs[b], out_refs[b].at[4 * x + 2 * y + c], local_sems.at[b]) for b in range(nb)]
        sends = []
        for b in range(nb):
            mine[b].start()
            first = [copy(b, 0, me, sibling, own=True)]
            first += [copy(b, 1 + j, me, (*chip, c), own=True) for j, chip in enumerate(chips)]
            for cp in first:
                cp.start()
            sends += first
        for j, chip in enumerate(chips):
            for b in range(nb):
                copy(b, 1 + j, (*chip, c), me).wait_recv()
                passed = copy(b, 4 + j, (*chip, c), sibling)
                passed.start()
                sends.append(passed)
        for b in range(nb):
            copy(b, 0, sibling, me).wait_recv()
            for j, chip in enumerate(chips):
                copy(b, 4 + j, (*chip, 1 - c), me).wait_recv()
        for cp in sends:
            cp.wait_send()
        for cp in mine:
            cp.wait()

    return pl.pallas_call(
        body, name=name,
        out_shape=[jax.ShapeDtypeStruct((N_DEV,) + p.shape, p.dtype) for p in bufs],
        in_specs=[HBM_SPEC] * nb, out_specs=[HBM_SPEC] * nb,
        scratch_shapes=[pltpu.SemaphoreType.DMA((7 * nb,)), pltpu.SemaphoreType.DMA((7 * nb,)),
                        pltpu.SemaphoreType.DMA((nb,))],
    )(*bufs)


def _grad_exchange(bigs, small, name):
    flips = [(fx, fy, fc) for fx in (0, 1) for fy in (0, 1) for fc in (0, 1)][1:]
    nb = len(bigs)

    def body(*refs):
        big_refs, small_ref = refs[:nb], refs[nb]
        rbig_refs, rsmall_ref = refs[nb + 1:2 * nb + 1], refs[2 * nb + 1]
        send_sems, recv_sems, local_sems = refs[2 * nb + 2:]
        x, y, c = lax.axis_index("x"), lax.axis_index("y"), lax.axis_index("c")
        my = 4 * x + 2 * y + c
        own = [pltpu.make_async_copy(big_refs[b].at[my], rbig_refs[b].at[my], local_sems.at[b]) for b in range(nb)]
        own.append(pltpu.make_async_copy(small_ref, rsmall_ref.at[my], local_sems.at[nb]))
        for cp in own:
            cp.start()
        copies = []
        for b in range(nb + 1):
            for k, (fx, fy, fc) in enumerate(flips):
                px = 1 - x if fx else x
                py = 1 - y if fy else y
                pc = 1 - c if fc else c
                src = small_ref if b == nb else big_refs[b].at[4 * px + 2 * py + pc]
                dst = rsmall_ref.at[my] if b == nb else rbig_refs[b].at[my]
                copies.append(pltpu.make_async_remote_copy(
                    src_ref=src, dst_ref=dst, send_sem=send_sems.at[7 * b + k], recv_sem=recv_sems.at[7 * b + k],
                    device_id=(px, py, pc), device_id_type=MESH))
        for cp in copies:
            cp.start()
        for cp in copies:
            cp.wait()
        for cp in own:
            cp.wait()

    nsem = 7 * (nb + 1)
    return pl.pallas_call(
        body, name=name,
        out_shape=[jax.ShapeDtypeStruct(b.shape, b.dtype) for b in bigs]
        + [jax.ShapeDtypeStruct((N_DEV,) + small.shape, small.dtype)],
        in_specs=[HBM_SPEC] * (nb + 1), out_specs=[HBM_SPEC] * (nb + 1),
        scratch_shapes=[pltpu.SemaphoreType.DMA((nsem,)), pltpu.SemaphoreType.DMA((nsem,)),
                        pltpu.SemaphoreType.DMA((nb + 1,))],
    )(*bigs, small)


def _adamw(parts, w, m, v, name):
    rows, n = w.shape
    tm = rows if rows <= 384 else 256
    assert rows % tm == 0

    def body(p_ref, w_ref, m_ref, v_ref, g_ref, d_ref, m2_ref, v2_ref):
        g = p_ref[0, :, 0:n].astype(F32)
        for s in range(1, N_DEV):
            g = g + p_ref[s, :, 0:n].astype(F32)
        g_ref[...] = g
        m2 = ADAM_B1 * m_ref[...] + (1.0 - ADAM_B1) * g
        v2 = ADAM_B2 * v_ref[...] + (1.0 - ADAM_B2) * jnp.square(g)
        m2_ref[...] = m2
        v2_ref[...] = v2
        m_hat = m2 / (1.0 - ADAM_B1 ** ADAM_STEP)
        v_hat = v2 / (1.0 - ADAM_B2 ** ADAM_STEP)
        d_ref[...] = -ADAM_LR * (m_hat / (jnp.sqrt(v_hat) + ADAM_EPS) + ADAM_WD * w_ref[...])

    row = pl.BlockSpec((tm, n), lambda i: (i, 0))
    return pl.pallas_call(
        body, name=name, grid=(rows // tm,),
        in_specs=[pl.BlockSpec((N_DEV, tm, parts.shape[2]), lambda i: (0, i, 0)), row, row, row],
        out_specs=[row] * 4,
        out_shape=[jax.ShapeDtypeStruct((rows, n), F32)] * 4,
        compiler_params=_params(("parallel",)),
    )(parts, w, m, v)


def _pack(flat_parts, rows):
    flat = jnp.concatenate(flat_parts, axis=-1)
    pad = rows * LANES - flat.shape[-1]
    flat = jnp.pad(flat, [(0, 0)] * (flat.ndim - 1) + [(0, pad)])
    return flat.reshape(flat.shape[:-1] + (rows, LANES))


def _unpack(packed, shapes):
    flat = packed.reshape(packed.shape[:-2] + (-1,))
    out, off = {}, 0
    for name, shape in shapes.items():
        n = int(np.prod(shape))
        out[name] = flat[..., off:off + n].reshape(flat.shape[:-1] + tuple(shape))
        off += n
    return out


def _pad_to(a, shape):
    return jnp.pad(a, [(0, t - d) for d, t in zip(a.shape, shape)])


def _pad_w_in(w):
    k = w.shape[0]
    z = lambda n: jnp.zeros((k, n), w.dtype)
    return jnp.concatenate([w[:, :COL_KR], z(64), w[:, COL_KR:], z(32)], axis=1)


def _unpad_w_in(w):
    return jnp.concatenate([w[:, :COL_KR], w[:, COL_KR + 64:COL_KR + 96]], axis=1)


def _assemble_weights(g, conv_b):
    half = N_DEV // 2
    cols = lambda a: a.transpose(1, 0, 2).reshape(a.shape[1], N_DEV * a.shape[2])
    fw = {
        "w_in": _pad_w_in(cols(g["w_in"][:, :, :D_IN // N_DEV])),
        "w_uq": cols(g["w_uq"]),
        "w_ukv": cols(g["w_ukv"]),
        "w_o": g["w_o"].reshape(D_MODEL, D_MODEL),
        "w_up": g["w_up"],
        "w_down": _pad_to(g["w_down"].reshape(half, FF_SHARD, D_MODEL), (half, FF_SLAB, D_MODEL)).reshape(half * FF_SLAB, D_MODEL),
        "conv_w": g["conv_w"][:, :3].reshape(2, half, 3, FF_SLAB).transpose(0, 2, 1, 3).reshape(2, 3, half * FF_SLAB),
        "conv_b": _pad_to(conv_b.reshape(2, 1, half, FF_SHARD), (2, 1, half, FF_SLAB)).reshape(2, 1, half * FF_SLAB),
    }
    return fw


def _grad_bufs(grads):
    half = N_DEV // 2
    slabs = lambda a: a.reshape(a.shape[0], N_DEV, a.shape[1] // N_DEV).transpose(1, 0, 2)
    out = {
        "w_in": _pad_to(slabs(_unpad_w_in(grads["w_in"])), (N_DEV,) + BUF_SHAPES["w_in"]),
        "w_uq": slabs(grads["w_uq"]),
        "w_ukv": slabs(grads["w_ukv"]),
        "w_o": grads["w_o"].reshape((N_DEV,) + BUF_SHAPES["w_o"]),
        "w_up": grads["w_up"],
        "w_down": grads["w_down"].reshape(half, FF_SLAB, D_MODEL)[:, :FF_SHARD].reshape((N_DEV,) + BUF_SHAPES["w_down"]),
        "conv_w": _pad_to(grads["conv_w"].reshape(2, 3, half, FF_SLAB).transpose(0, 2, 1, 3).reshape(N_DEV, 3, FF_SLAB),
                          (N_DEV,) + BUF_SHAPES["conv_w"]),
    }
    return {n: (a if n == "conv_w" else a.astype(BF16)) for n, a in out.items()}


def kernel(x, norm_mix_pre, w_in, q_lat_norm, w_uq, kv_lat_norm, w_ukv, out_norm_a, out_norm_b, w_o, norm_mix_post, norm_ffn_pre, w_up, conv_w, conv_b, w_down, norm_ffn_post, loss_target, m_norm_mix_pre, m_w_in, m_q_lat_norm, m_w_uq, m_kv_lat_norm, m_w_ukv, m_out_norm_a, m_out_norm_b, m_w_o, m_norm_mix_post, m_norm_ffn_pre, m_w_up, m_conv_w, m_conv_b, m_w_down, m_norm_ffn_post, v_norm_mix_pre, v_w_in, v_q_lat_norm, v_w_uq, v_kv_lat_norm, v_w_ukv, v_out_norm_a, v_out_norm_b, v_w_o, v_norm_mix_post, v_norm_ffn_pre, v_w_up, v_conv_w, v_conv_b, v_w_down, v_norm_ffn_post):
    given = dict(locals())
    w = {n: given[n][0] for n in WEIGHTS}
    m = {n: given["m_" + n][0] for n in WEIGHTS}
    v = {n: given["v_" + n][0] for n in WEIGHTS}
    rep_shapes = {n: w[n].shape for n in REPLICATED}

    bufs = [_pad_to(w[n] if n == "conv_w" else w[n].astype(BF16), BUF_SHAPES[n]) for n in BUF_ORDER]
    gathered = dict(zip(BUF_ORDER, _all_gather(bufs, "weight_all_gather")))
    fw = _assemble_weights(gathered, conv_b)
    rep = {n: given[n] for n in REPLICATED}

    loss_local, grad_x, grads = _local_step(x[0], loss_target[0], fw, rep)

    gb = _grad_bufs(grads)
    grads["conv_b"] = grads["conv_b"].reshape(N_DEV, FF_SLAB)[:, :FF_SHARD]
    small = _pack([grads[n].reshape(-1) for n in REPLICATED], SMALL_ROWS)
    received = _grad_exchange([gb[n] for n in BUF_ORDER], small, "grad_exchange")
    results = [{}, {}, {}, {}]
    for n, parts in zip(BUF_ORDER, received[:-1]):
        if n == "conv_w":
            args = [_pad_to(t[n], BUF_SHAPES[n]) for t in (w, m, v)]
        else:
            args = [w[n], m[n], v[n]]
        outs = _adamw(parts, *args, f"adamw_{n}")
        for t in range(4):
            results[t][n] = outs[t][:w[n].shape[0], :w[n].shape[1]] if n == "conv_w" else outs[t]
    pk = lambda d: _pack([d[n].reshape(-1) for n in REPLICATED], SMALL_ROWS)
    small_out = _adamw(received[-1], pk(w), pk(m), pk(v), "adamw_replicated")
    for t in range(4):
        results[t].update(_unpack(small_out[t], rep_shapes))

    loss = lax.psum(loss_local, ("x", "y", "c"))
    outs = [loss, grad_x[None]]
    for res in results:
        outs += [res[n][None] for n in WEIGHTS]
    return tuple(outs)
```

```python
import functools
import math

import numpy as np
import jax
import jax.numpy as jnp
from jax import lax
from jax.experimental import pallas as pl
from jax.experimental.pallas import tpu as pltpu

F32 = jnp.float32
BF16 = jnp.bfloat16

D_MODEL = 1024
N_DEV = 8
WIDTH_A = 512
N_HEADS = 8
Q_LORA = 384
KV_LORA = 256
QK_ROPE = 32
QK_NOPE = 64
D_FF = 2816
FF_SHARD = 2 * D_FF // N_DEV
FF_SLAB = 768
DIL_CONFIGS = ((128, 1), (512, 4), (2048, 16))
BAND_HALF = 64
ROPE_BASE = 10000.0
EPS = 1e-6
NEG = -1e30
MLA_SCALE = (QK_NOPE + QK_ROPE) ** -0.5
SCALE_A = 0.125
LOG2E = 1.0 / math.log(2.0)
LN2 = math.log(2.0)

COL_CQ = 3 * WIDTH_A
COL_CKV = COL_CQ + Q_LORA
COL_KR = COL_CKV + KV_LORA
N_PROJ = COL_KR + 128
N_LAT = N_PROJ - COL_CQ
D_IN = COL_KR + QK_ROPE

ADAM_LR = 0.001
ADAM_B1 = 0.9
ADAM_B2 = 0.999
ADAM_EPS = 1e-08
ADAM_WD = 0.01
ADAM_STEP = 10

LANES = 128
VMEM_LIMIT = 56 * 1024 * 1024

SHARDED = ("w_in", "w_uq", "w_ukv", "w_o", "w_up", "conv_w", "w_down")
REPLICATED = ("norm_mix_pre", "q_lat_norm", "kv_lat_norm", "out_norm_a", "out_norm_b", "norm_mix_post",
              "norm_ffn_pre", "conv_b", "norm_ffn_post")
WEIGHTS = ("norm_mix_pre", "w_in", "q_lat_norm", "w_uq", "kv_lat_norm", "w_ukv", "out_norm_a", "out_norm_b", "w_o",
           "norm_mix_post", "norm_ffn_pre", "w_up", "conv_w", "conv_b", "w_down", "norm_ffn_post")


def _params(sem=None):
    return pltpu.CompilerParams(dimension_semantics=sem, vmem_limit_bytes=VMEM_LIMIT)


def _dot(a, b, dims):
    return lax.dot_general(a, b, (dims, ((), ())), preferred_element_type=F32)


NN = ((1,), (0,))
NT = ((1,), (1,))
TN = ((0,), (0,))


def _rstd(x):
    return lax.rsqrt(jnp.mean(x * x, axis=-1, keepdims=True) + EPS)


def _rms_bwd(dy, x, g):
    r = _rstd(x)
    z = x * r
    gy = dy * g
    dx = r * (gy - z * jnp.mean(gy * z, axis=-1, keepdims=True))
    return dx, dy * z


def _split_hi_lo(v):
    hi = v.astype(BF16)
    lo = (v - hi.astype(F32)).astype(BF16)
    return hi, lo


def _matmul(a, b, mode, out_dtype, tm, tn, tk, name):
    if mode == "nn":
        (m, k), n = a.shape, b.shape[1]
        a_spec = pl.BlockSpec((tm, tk), lambda i, j, s: (i, s))
        b_spec = pl.BlockSpec((tk, tn), lambda i, j, s: (s, j))
        dims = NN
    elif mode == "nt":
        (m, k), n = a.shape, b.shape[0]
        a_spec = pl.BlockSpec((tm, tk), lambda i, j, s: (i, s))
        b_spec = pl.BlockSpec((tn, tk), lambda i, j, s: (j, s))
        dims = NT
    else:
        (k, m), n = a.shape, b.shape[1]
        a_spec = pl.BlockSpec((tk, tm), lambda i, j, s: (s, i))
        b_spec = pl.BlockSpec((tk, tn), lambda i, j, s: (s, j))
        dims = TN
    assert m % tm == 0 and n % tn == 0 and k % tk == 0, (name, m, n, k, tm, tn, tk)
    return _matmul_core(a, b, dims, (m // tm, n // tn, k // tk), a_spec, b_spec,
                        pl.BlockSpec((tm, tn), lambda i, j, s: (i, j)), jax.ShapeDtypeStruct((m, n), out_dtype),
                        (tm, tn), name)


def _matmul_core(a, b, dims, grid, a_spec, b_spec, o_spec, out_sds, acc_shape, name):
    nk = grid[2]

    def body(a_ref, b_ref, o_ref, acc_ref):
        s = pl.program_id(2)

        @pl.when(s == 0)
        def _():
            acc_ref[...] = jnp.zeros_like(acc_ref)

        acc_ref[...] += _dot(a_ref[...].astype(BF16), b_ref[...].astype(BF16), dims)

        @pl.when(s == nk - 1)
        def _():
            o_ref[...] = acc_ref[...].astype(out_sds.dtype)

    return pl.pallas_call(
        body, name=name, grid=grid, in_specs=[a_spec, b_spec], out_specs=o_spec, out_shape=out_sds,
        scratch_shapes=[pltpu.VMEM(acc_shape, F32)],
        compiler_params=_params(("parallel", "parallel", "arbitrary")),
    )(a, b)


def _norm_matmul(x, g, w, tn, name):
    s, k = x.shape
    tm = min(1024, s)
    if w.ndim == 3:
        nj, _, tn = w.shape
        half = nj // 2
        w_spec = pl.BlockSpec((None, k, tn), lambda i, j: (j, 0, 0))
        o_spec = pl.BlockSpec((None, tm, tn), lambda i, j: (j // half, i, j % half))
        o_sds = jax.ShapeDtypeStruct((2, s, half * tn), BF16)
    else:
        n = w.shape[1]
        assert n % tn == 0
        nj = n // tn
        w_spec = pl.BlockSpec((k, tn), lambda i, j: (0, j))
        o_spec = pl.BlockSpec((tm, tn), lambda i, j: (i, j))
        o_sds = jax.ShapeDtypeStruct((s, n), BF16)

    def body(x_ref, g_ref, w_ref, o_ref, h_ref):
        @pl.when(pl.program_id(1) == 0)
        def _():
            xv = x_ref[...]
            h_ref[...] = (xv * _rstd(xv) * g_ref[...]).astype(BF16)

        o_ref[...] = _dot(h_ref[...], w_ref[...], NN).astype(BF16)

    return pl.pallas_call(
        body, name=name, grid=(s // tm, nj),
        in_specs=[pl.BlockSpec((tm, k), lambda i, j: (i, 0)),
                  pl.BlockSpec((1, k), lambda i, j: (0, 0)),
                  w_spec],
        out_specs=[o_spec, pl.BlockSpec((tm, k), lambda i, j: (i, 0))],
        out_shape=[o_sds, jax.ShapeDtypeStruct((s, k), BF16)],
        compiler_params=_params(("parallel", "arbitrary")),
    )(x, g, w)


def _band_bias(r):
    off = np.arange(256)[None, :] - BAND_HALF - np.arange(128)[:, None]
    slopes = np.exp2(-8.0 * np.arange(1, N_HEADS + 1, dtype=np.float32) / N_HEADS).astype(np.float32)
    dist = (np.abs(off) * r).astype(np.float32)
    bias = -slopes[:, None, None] * dist[None]
    bias = np.where((np.abs(off) <= BAND_HALF)[None], bias, np.float32(NEG))
    return jnp.asarray(bias, F32)


def _band_call(mode, r, center, window, bias, name):
    seq = center[0][0].shape[0]
    tq = min(512, seq)
    nsub = tq // 128
    hb = tq // BAND_HALF
    nh = seq // BAND_HALF
    nc, nw = len(center), len(window)
    n_out = {"fwd": 2, "dq": 1, "dkv": 2}[mode]

    def specs(col):
        return (pl.BlockSpec((BAND_HALF, WIDTH_A), lambda c, i: (jnp.maximum(i * hb - 1, 0), col(c))),
                pl.BlockSpec((tq, WIDTH_A), lambda c, i: (i, col(c))),
                pl.BlockSpec((BAND_HALF, WIDTH_A), lambda c, i: (jnp.minimum((i + 1) * hb, nh - 1), col(c))))

    cspec = pl.BlockSpec((tq, WIDTH_A), lambda c, i: (i, c))
    in_specs = [specs(col)[1] for _, col in center]
    operands = [a for a, _ in center]
    for a, col in window:
        in_specs += list(specs(col))
        operands += [a, a, a]
    in_specs.append(pl.BlockSpec((N_HEADS, 128, 256), lambda c, i: (0, 0, 0)))
    operands.append(bias)
    window = [a for a, _ in window]

    def aug_stat(base, stat_sw, lane, act, e0):
        hi, lo = _split_hi_lo(stat_sw)
        return jnp.where(act, base, jnp.where(lane == e0, -hi, jnp.where(lane == e0 + 1, -lo, jnp.zeros_like(hi))))

    def aug_ones(base, lane, e0):
        return jnp.where((lane == e0) | (lane == e0 + 1), jnp.ones_like(base), base)

    def body(*refs):
        c_refs = refs[:nc]
        w_refs = refs[nc:nc + 3 * nw]
        bias_ref = refs[nc + 3 * nw]
        o_refs = refs[nc + 3 * nw + 1:nc + 3 * nw + 1 + n_out]
        wins = refs[nc + 3 * nw + 1 + n_out:]
        i = pl.program_id(1)
        for t in range(nw):
            wins[t][0:BAND_HALF, :] = w_refs[3 * t][...]
            wins[t][BAND_HALF:BAND_HALF + tq, :] = w_refs[3 * t + 1][...]
            wins[t][BAND_HALF + tq:BAND_HALF + tq + BAND_HALF, :] = w_refs[3 * t + 2][...]

        def sub(j, carry):
            r0 = pl.multiple_of(j * 128, 128)
            wpos = i * tq + j * 128 - BAND_HALF + lax.broadcasted_iota(jnp.int32, (128, 256), 1)
            valid = (wpos >= 0) & (wpos < seq)
            lane_c = lax.broadcasted_iota(jnp.int32, (128, 128), 1)
            lane_w = lax.broadcasted_iota(jnp.int32, (256, 128), 1)
            heads = [(p, a) for p in range(4) for a in range(2)]
            first, last_ops = [], []
            for p, a in heads:
                cols = slice(p * 128, (p + 1) * 128)
                cs = [c[pl.ds(r0, 128), cols] for c in c_refs]
                ws = [w[pl.ds(r0, 256), cols] for w in wins]
                e0 = 64 if a == 0 else 0
                act_c = (lane_c < 64) if a == 0 else (lane_c >= 64)
                act_w = (lane_w < 64) if a == 0 else (lane_w >= 64)
                bias_a = bias_ref[2 * p + a]
                if mode == "fwd":
                    qa = jnp.where(act_c, cs[0] * SCALE_A, jnp.zeros_like(cs[0]))
                    first.append((_dot(qa, ws[0], NT) + bias_a, None))
                    last_ops.append((ws[1],))
                elif mode == "dq":
                    q2, dy2, l2, d2 = cs
                    k2, v2 = ws
                    q_aug = aug_stat(q2 * SCALE_A, pltpu.roll(l2, 64, 1), lane_c, act_c, e0)
                    dy_aug = aug_stat(dy2, pltpu.roll(d2, 64, 1), lane_c, act_c, e0)
                    first.append((_dot(q_aug, aug_ones(k2, lane_w, e0), NT) + bias_a,
                                  _dot(dy_aug, aug_ones(v2, lane_w, e0), NT)))
                    last_ops.append((k2,))
                else:
                    k2, v2 = cs
                    q2, dy2, l2, d2 = ws
                    q_aug = aug_stat(q2 * SCALE_A, pltpu.roll(l2, 64, 1), lane_w, act_w, e0)
                    dy_aug = aug_stat(dy2, pltpu.roll(d2, 64, 1), lane_w, act_w, e0)
                    first.append((_dot(aug_ones(k2, lane_c, e0), q_aug, NT) + bias_a,
                                  _dot(aug_ones(v2, lane_c, e0), dy_aug, NT)))
                    last_ops.append((q_aug, dy_aug))
            mid = []
            for sc, dp in first:
                sc = jnp.where(valid, sc, NEG)
                if mode == "fwd":
                    m = jnp.max(sc, axis=-1, keepdims=True)
                    e = jnp.exp(sc - m)
                    l = jnp.sum(e, axis=-1, keepdims=True)
                    mid.append((e.astype(BF16), l, m + jnp.log(l)))
                else:
                    pr = jnp.exp(sc)
                    mid.append((pr.astype(BF16), (pr * dp).astype(BF16)))
            res = []
            for md, ops in zip(mid, last_ops):
                if mode == "fwd":
                    res.append((_dot(md[0], ops[0], NN) / md[1], jnp.broadcast_to(md[2], (128, 128))))
                elif mode == "dq":
                    res.append((_dot(md[1], ops[0], NN) * SCALE_A,))
                else:
                    res.append((_dot(md[1], ops[0], NN), _dot(md[0], ops[1], NN)))
            for t in range(n_out):
                pairs = [jnp.where(lane_c < 64, res[2 * p][t], res[2 * p + 1][t]) for p in range(4)]
                o_refs[t][pl.ds(r0, 128), :] = jnp.concatenate(pairs, axis=1)
            return carry

        lax.fori_loop(0, nsub, sub, 0)

    outs = pl.pallas_call(
        body, name=name, grid=(r, seq // tq),
        in_specs=in_specs,
        out_specs=[cspec] * n_out,
        out_shape=[jax.ShapeDtypeStruct((seq, r * WIDTH_A), F32)] * n_out,
        scratch_shapes=[pltpu.VMEM((tq + 2 * BAND_HALF, WIDTH_A), w.dtype) for w in window],
        compiler_params=_params(("parallel", "parallel")),
    )(*operands)
    return outs


def _slab_scratch(tm, w):
    return pltpu.VMEM((w // 128, tm, 128), F32)


def _put(scr, val):
    for j in range(scr.shape[0]):
        scr[j] = val[:, j * 128:(j + 1) * 128].astype(F32)


def _get(scr):
    return jnp.concatenate([scr[j] for j in range(scr.shape[0])], axis=1)


def _dilate_store(dst_ref, scr, r):
    nb, tm, _ = scr.shape
    w = nb * 128
    for c in range(r):
        for j in range(nb):
            dst_ref[:, c * w + j * 128:c * w + (j + 1) * 128] = scr[j, pl.ds(c, tm // r, stride=r), :].astype(dst_ref.dtype)


def _undilate(scr, src_ref, r):
    nb, tm, _ = scr.shape
    w = nb * 128
    for c in range(r):
        for j in range(nb):
            scr[j, pl.ds(c, tm // r, stride=r), :] = src_ref[:, c * w + j * 128:c * w + (j + 1) * 128].astype(F32)


def _dil_spec(tm, r, w):
    return pl.BlockSpec((tm // r, r * w), lambda i: (i, 0))


def _dil_shape(s, r, w, dtype):
    return jax.ShapeDtypeStruct((s // r, r * w), dtype)


def _in_proj(x, g, w, name):
    s, k = x.shape
    n = w.shape[1]
    tm = min(512, s)
    qkv = 3 * WIDTH_A

    def body(x_ref, g_ref, w_ref, o_ref, h_ref, d4_ref, d16_ref, scr):
        xv = x_ref[...]
        h = (xv * _rstd(xv) * g_ref[...]).astype(BF16)
        h_ref[...] = h
        acc = _dot(h, w_ref[...], NN)
        o_ref[...] = acc.astype(BF16)
        _put(scr, acc[:, 0:qkv])
        _dilate_store(d4_ref, scr, 4)
        _dilate_store(d16_ref, scr, 16)

    row = lambda c: pl.BlockSpec((tm, c), lambda i: (i, 0))
    return pl.pallas_call(
        body, name=name, grid=(s // tm,),
        in_specs=[row(k), pl.BlockSpec((1, k), lambda i: (0, 0)), pl.BlockSpec((k, n), lambda i: (0, 0))],
        out_specs=[row(n), row(k), _dil_spec(tm, 4, qkv), _dil_spec(tm, 16, qkv)],
        out_shape=[jax.ShapeDtypeStruct((s, n), BF16), jax.ShapeDtypeStruct((s, k), BF16),
                   _dil_shape(s, 4, qkv, BF16), _dil_shape(s, 16, qkv, BF16)],
        scratch_shapes=[_slab_scratch(tm, qkv)],
        compiler_params=_params(("parallel",)),
    )(x, g, w)


def _band_combine(os_, lses, name):
    s = os_[0].shape[0]
    tm = min(512, s)

    def body(o1, o4, o16, l1, l4, l16, ya_ref, lse_ref, lse4_ref, lse16_ref, so4, sl4, so16, sl16):
        _undilate(so4, o4, 4)
        _undilate(sl4, l4, 4)
        _undilate(so16, o16, 16)
        _undilate(sl16, l16, 16)
        a0, a1, a2 = l1[...], _get(sl4), _get(sl16)
        m = jnp.maximum(jnp.maximum(a0, a1), a2)
        e0, e1, e2 = jnp.exp(a0 - m), jnp.exp(a1 - m), jnp.exp(a2 - m)
        den = e0 + e1 + e2
        ya_ref[...] = (e0 * o1[...] + e1 * _get(so4) + e2 * _get(so16)) / den
        lse = m + jnp.log(den)
        lse_ref[...] = lse
        _put(sl4, lse)
        _dilate_store(lse4_ref, sl4, 4)
        _dilate_store(lse16_ref, sl4, 16)

    nat = pl.BlockSpec((tm, WIDTH_A), lambda i: (i, 0))
    d4, d16 = _dil_spec(tm, 4, WIDTH_A), _dil_spec(tm, 16, WIDTH_A)
    return pl.pallas_call(
        body, name=name, grid=(s // tm,), in_specs=[nat, d4, d16] * 2, out_specs=[nat, nat, d4, d16],
        out_shape=[jax.ShapeDtypeStruct((s, WIDTH_A), F32)] * 2
        + [_dil_shape(s, 4, WIDTH_A, F32), _dil_shape(s, 16, WIDTH_A, F32)],
        scratch_shapes=[_slab_scratch(tm, WIDTH_A)] * 4,
        compiler_params=_params(("parallel",)),
    )(*os_, *lses)


def _rope_tables(s):
    pos = jnp.arange(s, dtype=F32)
    inv_freq = jnp.exp(-math.log(ROPE_BASE) * jnp.arange(0, QK_ROPE, 2, dtype=F32) / QK_ROPE)
    ang = pos[:, None] * inv_freq[None, :]
    cos, sin = jnp.cos(ang), jnp.sin(ang)
    one = jnp.ones((s, 64), F32)
    zero16 = jnp.zeros((s, 16), F32)
    c = jnp.concatenate([one, cos, cos, jnp.ones((s, 32), F32)], axis=1)
    sa = jnp.concatenate([jnp.zeros((s, 64), F32), -sin, zero16, jnp.zeros((s, 32), F32)], axis=1)
    sb = jnp.concatenate([jnp.zeros((s, 64), F32), zero16, sin, jnp.zeros((s, 32), F32)], axis=1)
    return c, sa, sb


def _rope_fwd(x, c, sa, sb):
    return x * c + pltpu.roll(x, 112, 1) * sa + pltpu.roll(x, 16, 1) * sb


def _rope_bwd(dy, c, sa, sb):
    return dy * c + pltpu.roll(dy * sa, 16, 1) + pltpu.roll(dy * sb, 112, 1)


def _mla_prep(proj, g_q, g_kv, w_uq, w_ukv, tabs, name):
    s = proj.shape[0]
    tm = min(512, s)
    width = N_HEADS * 128

    def body(lat_ref, gq_ref, gkv_ref, wq_ref, wkv_ref, c_ref, sa_ref, sb_ref,
             q_ref, k_ref, kv_ref, cqn_ref, ckvn_ref):
        c, sa, sb = c_ref[...], sa_ref[...], sb_ref[...]
        cq = lat_ref[:, 0:Q_LORA].astype(F32)
        cqn = (cq * _rstd(cq) * gq_ref[...]).astype(BF16)
        cqn_ref[...] = cqn
        q = _dot(cqn, wq_ref[...], NN)
        ckv = lat_ref[:, Q_LORA:Q_LORA + KV_LORA].astype(F32)
        ckvn = (ckv * _rstd(ckv) * gkv_ref[...]).astype(BF16)
        ckvn_ref[...] = ckvn
        kv = _dot(ckvn, wkv_ref[...], NN)
        lane = lax.broadcasted_iota(jnp.int32, (tm, 128), 1)
        krr = _rope_fwd(lat_ref[:, Q_LORA + KV_LORA:].astype(F32), c, sa, sb)
        krr = jnp.where((lane == 96) | (lane == 97), 1.0, krr)
        ones01 = jnp.where(lane < 2, 1.0, 0.0)
        for h in range(N_HEADS):
            cols = slice(h * 128, (h + 1) * 128)
            q_ref[:, cols] = (_rope_fwd(q[:, cols], c, sa, sb) * (MLA_SCALE * LOG2E)).astype(BF16)
            k_ref[:, cols] = jnp.where(lane < 64, kv[:, cols], krr).astype(BF16)
            kv_ref[:, cols] = jnp.where(lane < 64, ones01, kv[:, cols]).astype(BF16)

    row = lambda n: pl.BlockSpec((tm, n), lambda i: (i, 0))
    full = lambda a: pl.BlockSpec(a.shape, lambda i: (0, 0))
    tab = pl.BlockSpec((tm, 128), lambda i: (i, 0))
    return pl.pallas_call(
        body, name=name, grid=(s // tm,),
        in_specs=[pl.BlockSpec((tm, N_LAT), lambda i: (i, COL_CQ // N_LAT)),
                  full(g_q), full(g_kv), full(w_uq), full(w_ukv), tab, tab, tab],
        out_specs=[row(width), row(width), row(width), row(Q_LORA), row(KV_LORA)],
        out_shape=[jax.ShapeDtypeStruct((s, width), BF16)] * 3
        + [jax.ShapeDtypeStruct((s, Q_LORA), BF16), jax.ShapeDtypeStruct((s, KV_LORA), BF16)],
        compiler_params=_params(("parallel",)),
    )(proj, g_q, g_kv, w_uq, w_ukv, *tabs)


def _mla_fwd(qcat, kcat, kvb, name):
    s = qcat.shape[0]
    tq = min(512, s)
    tk = min(1024, s)
    nkc = s // tk

    def body(q_ref, k_ref, v_ref, yb_ref, qaug_ref, m_ref, acc_ref):
        lane = lax.broadcasted_iota(jnp.int32, (tq, 128), 1)
        m_ref[...] = jnp.full((2, tq, 128), NEG, F32)
        acc_ref[...] = jnp.zeros((2, tq, 128), F32)

        def chunk(cidx, carry):
            k0 = pl.multiple_of(cidx * tk, tk)
            cols = [slice(a * 128, (a + 1) * 128) for a in range(2)]
            scs = [_dot(q_ref[:, c], k_ref[pl.ds(k0, tk), c], NT) for c in cols]
            prs, alphas = [], []
            for a, sc in enumerate(scs):
                m_prev = m_ref[a]
                m_new = jnp.maximum(m_prev, jnp.max(sc, axis=-1, keepdims=True))
                alphas.append(jnp.exp2(m_prev - m_new))
                prs.append(jnp.exp2(sc - jnp.tile(m_new, (1, tk // 128))).astype(BF16))
                m_ref[a] = m_new
            for a, c in enumerate(cols):
                acc_ref[a] = alphas[a] * acc_ref[a] + _dot(prs[a], v_ref[pl.ds(k0, tk), c], NN)
            return carry

        lax.fori_loop(0, nkc, chunk, 0)
        outs = []
        for a in range(2):
            cols = slice(a * 128, (a + 1) * 128)
            acc = acc_ref[a]
            l = acc[:, 0:1]
            outs.append(acc / l)
            hi, lo = _split_hi_lo(m_ref[a] + jnp.log(l) * LOG2E)
            qaug_ref[:, cols] = jnp.where(lane == 96, -hi, jnp.where(lane == 97, -lo, q_ref[:, cols]))
        yb_ref[...] = jnp.where(lane < 64, pltpu.roll(outs[0], 64, 1), outs[1])

    return pl.pallas_call(
        body, name=name, grid=(4, s // tq),
        in_specs=[pl.BlockSpec((tq, 256), lambda p, i: (i, p)),
                  pl.BlockSpec((s, 256), lambda p, i: (0, p)),
                  pl.BlockSpec((s, 256), lambda p, i: (0, p))],
        out_specs=[pl.BlockSpec((tq, 128), lambda p, i: (i, p)),
                   pl.BlockSpec((tq, 256), lambda p, i: (i, p))],
        out_shape=[jax.ShapeDtypeStruct((s, WIDTH_A), F32), jax.ShapeDtypeStruct((s, N_HEADS * 128), BF16)],
        scratch_shapes=[pltpu.VMEM((2, tq, 128), F32)] * 2,
        compiler_params=_params(("parallel", "parallel")),
    )(qcat, kcat, kvb)


def _mla_bwd(qaug, kcat, kvb, doaug, name):
    s = qaug.shape[0]
    tq = min(512, s)
    tk = min(512, s)
    nqc = s // tq
    width = N_HEADS * 128

    def body(q_ref, do_ref, k_ref, v_ref, dq_ref, dk_ref, dv_ref, dk_acc, dv_acc):
        j = pl.program_id(1)

        @pl.when(j == 0)
        def _():
            dq_ref[...] = jnp.zeros_like(dq_ref)

        dk_acc[...] = jnp.zeros_like(dk_acc)
        dv_acc[...] = jnp.zeros_like(dv_acc)

        def chunk(cidx, carry):
            q0 = pl.multiple_of(cidx * tq, tq)
            cols = [slice(a * 128, (a + 1) * 128) for a in range(2)]
            qs = [q_ref[pl.ds(q0, tq), c] for c in cols]
            dos = [do_ref[pl.ds(q0, tq), c] for c in cols]
            kbs = [k_ref[:, c] for c in cols]
            sts = [_dot(kbs[a], qs[a], NT) for a in range(2)]
            dps = [_dot(v_ref[:, cols[a]], dos[a], NT) for a in range(2)]
            pts, dsts = [], []
            for a in range(2):
                pt = jnp.exp2(sts[a])
                pts.append(pt.astype(BF16))
                dsts.append((pt * dps[a]).astype(BF16))
            for a, c in enumerate(cols):
                dv_acc[:, c] += _dot(pts[a], dos[a], NN)
                dk_acc[:, c] += _dot(dsts[a], qs[a], NN)
                dq_ref[pl.ds(q0, tq), c] += _dot(dsts[a], kbs[a], TN)
            return carry

        lax.fori_loop(0, nqc, chunk, 0)
        dk_ref[...] = dk_acc[...] * LN2
        dv_ref[...] = dv_acc[...]

    return pl.pallas_call(
        body, name=name, grid=(N_HEADS // 2, s // tk),
        in_specs=[pl.BlockSpec((s, 256), lambda p, j: (0, p)),
                  pl.BlockSpec((s, 256), lambda p, j: (0, p)),
                  pl.BlockSpec((tk, 256), lambda p, j: (j, p)),
                  pl.BlockSpec((tk, 256), lambda p, j: (j, p))],
        out_specs=[pl.BlockSpec((s, 256), lambda p, j: (0, p)),
                   pl.BlockSpec((tk, 256), lambda p, j: (j, p)),
                   pl.BlockSpec((tk, 256), lambda p, j: (j, p))],
        out_shape=[jax.ShapeDtypeStruct((s, width), F32)] * 3,
        scratch_shapes=[pltpu.VMEM((tk, 256), F32)] * 2,
        compiler_params=_params(("parallel", "arbitrary")),
    )(qaug, doaug, kcat, kvb)


def _mla_prep_bwd(dqc, dkc, dvp, proj, cqn, ckvn, g_q, g_kv, w_uq, w_ukv, tabs, name):
    s = proj.shape[0]
    tm = min(256, s)
    width = N_HEADS * 128
    n_out_cols = N_LAT

    def body(dq_ref, dk_ref, dv_ref, lat_ref, cqn_ref, ckvn_ref, gq_ref, gkv_ref, wq_ref, wkv_ref,
             c_ref, sa_ref, sb_ref, dproj_ref, dwq_ref, dwkv_ref, dgq_ref, dgkv_ref):
        i = pl.program_id(0)

        @pl.when(i == 0)
        def _():
            dwq_ref[...] = jnp.zeros_like(dwq_ref)
            dwkv_ref[...] = jnp.zeros_like(dwkv_ref)
            dgq_ref[...] = jnp.zeros_like(dgq_ref)
            dgkv_ref[...] = jnp.zeros_like(dgkv_ref)

        c, sa, sb = c_ref[...], sa_ref[...], sb_ref[...]
        lane = lax.broadcasted_iota(jnp.int32, (tm, 128), 1)
        dkr = jnp.zeros((tm, 128), F32)
        dq_parts, dkv_parts = [], []
        for h in range(N_HEADS):
            cols = slice(h * 128, (h + 1) * 128)
            dq_parts.append(_rope_bwd(dq_ref[:, cols] * MLA_SCALE, c, sa, sb).astype(BF16))
            dkh = dk_ref[:, cols]
            dkr = dkr + dkh
            dkv_parts.append(jnp.where(lane < 64, dkh, dv_ref[:, cols]).astype(BF16))
        dq = jnp.concatenate(dq_parts, axis=1)
        dkv = jnp.concatenate(dkv_parts, axis=1)
        dkr = _rope_bwd(jnp.where((lane >= 64) & (lane < 96), dkr, 0.0), c, sa, sb)

        dcqn = _dot(dq, wq_ref[...], NT)
        dwq_ref[...] += _dot(cqn_ref[...], dq, TN)
        dcq, dgq = _rms_bwd(dcqn, lat_ref[:, 0:Q_LORA].astype(F32), gq_ref[...])
        dgq_ref[...] += jnp.sum(dgq, axis=0, keepdims=True)

        dckvn = _dot(dkv, wkv_ref[...], NT)
        dwkv_ref[...] += _dot(ckvn_ref[...], dkv, TN)
        dckv, dgkv = _rms_bwd(dckvn, lat_ref[:, Q_LORA:Q_LORA + KV_LORA].astype(F32), gkv_ref[...])
        dgkv_ref[...] += jnp.sum(dgkv, axis=0, keepdims=True)

        dproj_ref[:, 0:Q_LORA] = dcq.astype(BF16)
        dproj_ref[:, Q_LORA:Q_LORA + KV_LORA] = dckv.astype(BF16)
        dproj_ref[:, Q_LORA + KV_LORA:] = dkr.astype(BF16)

    row = lambda n: pl.BlockSpec((tm, n), lambda i: (i, 0))
    full = lambda a: pl.BlockSpec(a.shape, lambda i: (0, 0))
    tab = pl.BlockSpec((tm, 128), lambda i: (i, 0))
    return pl.pallas_call(
        body, name=name, grid=(s // tm,),
        in_specs=[row(width), row(width), row(width),
                  pl.BlockSpec((tm, N_LAT), lambda i: (i, COL_CQ // N_LAT)),
                  row(Q_LORA), row(KV_LORA), full(g_q), full(g_kv), full(w_uq), full(w_ukv), tab, tab, tab],
        out_specs=[row(n_out_cols), full(w_uq), full(w_ukv), full(g_q), full(g_kv)],
        out_shape=[jax.ShapeDtypeStruct((s, n_out_cols), BF16),
                   jax.ShapeDtypeStruct(w_uq.shape, F32), jax.ShapeDtypeStruct(w_ukv.shape, F32),
                   jax.ShapeDtypeStruct(g_q.shape, F32), jax.ShapeDtypeStruct(g_kv.shape, F32)],
        compiler_params=_params(("arbitrary",)),
    )(dqc, dkc, dvp, proj, cqn, ckvn, g_q, g_kv, w_uq, w_ukv, *tabs)


def _mix_out(ya, yb, na, nb, w_o, g_post, x, name):
    s = x.shape[0]
    tm = min(512, s)

    def body(ya_ref, yb_ref, na_ref, nb_ref, w_ref, g_ref, x_ref, yn_ref, y2_ref, x1_ref):
        a, b = ya_ref[...], yb_ref[...]
        yn = jnp.concatenate([a * _rstd(a) * na_ref[...], b * _rstd(b) * nb_ref[...]], axis=1).astype(BF16)
        yn_ref[...] = yn
        y2 = _dot(yn, w_ref[...], NN)
        y2_ref[...] = y2
        x1_ref[...] = x_ref[...] + y2 * _rstd(y2) * g_ref[...]

    row = lambda n: pl.BlockSpec((tm, n), lambda i: (i, 0))
    full = lambda a: pl.BlockSpec(a.shape, lambda i: (0, 0))
    return pl.pallas_call(
        body, name=name, grid=(s // tm,),
        in_specs=[row(WIDTH_A), row(WIDTH_A), full(na), full(nb), full(w_o), full(g_post), row(D_MODEL)],
        out_specs=[row(D_MODEL)] * 3,
        out_shape=[jax.ShapeDtypeStruct((s, D_MODEL), BF16), jax.ShapeDtypeStruct((s, D_MODEL), F32),
                   jax.ShapeDtypeStruct((s, D_MODEL), F32)],
        compiler_params=_params(("parallel",)),
    )(ya, yb, na, nb, w_o, g_post, x)


def _head_ones():
    blk = np.kron(np.eye(N_HEADS, dtype=np.float32), np.ones((64, 64), np.float32))
    return jnp.asarray(blk, F32)


def _outnorm_bwd(dyn, ya, yb, na, nb, ones, name):
    s = ya.shape[0]
    tm = min(256, s)

    def body(dyn_ref, ya_ref, yb_ref, na_ref, nb_ref, ones_ref, dya_ref, da_ref, do_ref, dna_ref, dnb_ref,
             dya4_ref, dya16_ref, da4_ref, da16_ref, scr):
        i = pl.program_id(0)

        @pl.when(i == 0)
        def _():
            dna_ref[...] = jnp.zeros_like(dna_ref)
            dnb_ref[...] = jnp.zeros_like(dnb_ref)

        a, b = ya_ref[...], yb_ref[...]
        dya, dna = _rms_bwd(dyn_ref[:, 0:WIDTH_A], a, na_ref[...])
        dyb, dnb = _rms_bwd(dyn_ref[:, WIDTH_A:], b, nb_ref[...])
        dna_ref[...] += jnp.sum(dna, axis=0, keepdims=True)
        dnb_ref[...] += jnp.sum(dnb, axis=0, keepdims=True)
        dya_b = dya.astype(BF16)
        dya_ref[...] = dya_b
        hp = lax.Precision.HIGHEST
        delta_a = jnp.dot(dya_b.astype(F32) * a, ones_ref[...], precision=hp, preferred_element_type=F32)
        da_ref[...] = delta_a
        _put(scr, dya_b)
        _dilate_store(dya4_ref, scr, 4)
        _dilate_store(dya16_ref, scr, 16)
        _put(scr, delta_a)
        _dilate_store(da4_ref, scr, 4)
        _dilate_store(da16_ref, scr, 16)
        dyb_b = dyb.astype(BF16)
        db = jnp.dot(dyb_b.astype(F32) * b, ones_ref[...], precision=hp, preferred_element_type=F32)
        lane = lax.broadcasted_iota(jnp.int32, (tm, 128), 1)
        zero = jnp.zeros((tm, 128), BF16)
        for p in range(4):
            cols = slice(p * 128, (p + 1) * 128)
            dyp = dyb_b[:, cols]
            dbp = db[:, cols]
            for a_ in range(2):
                src = pltpu.roll(dyp.astype(F32), 64, 1).astype(BF16) if a_ == 0 else dyp
                dlt = dbp if a_ == 0 else pltpu.roll(dbp, 64, 1)
                hi, lo = _split_hi_lo(dlt)
                blk = jnp.where(lane >= 64, src, jnp.where(lane == 0, -hi, jnp.where(lane == 1, -lo, zero)))
                h = 2 * p + a_
                do_ref[:, h * 128:(h + 1) * 128] = blk

    row = lambda n: pl.BlockSpec((tm, n), lambda i: (i, 0))
    full = lambda a: pl.BlockSpec(a.shape, lambda i: (0, 0))
    return pl.pallas_call(
        body, name=name, grid=(s // tm,),
        in_specs=[row(D_MODEL), row(WIDTH_A), row(WIDTH_A), full(na), full(nb), full(ones)],
        out_specs=[row(WIDTH_A), row(WIDTH_A), row(N_HEADS * 128), full(na), full(nb),
                   _dil_spec(tm, 4, WIDTH_A), _dil_spec(tm, 16, WIDTH_A), _dil_spec(tm, 4, WIDTH_A), _dil_spec(tm, 16, WIDTH_A)],
        out_shape=[jax.ShapeDtypeStruct((s, WIDTH_A), BF16), jax.ShapeDtypeStruct((s, WIDTH_A), F32),
                   jax.ShapeDtypeStruct((s, N_HEADS * 128), BF16),
                   jax.ShapeDtypeStruct(na.shape, F32), jax.ShapeDtypeStruct(nb.shape, F32),
                   _dil_shape(s, 4, WIDTH_A, BF16), _dil_shape(s, 16, WIDTH_A, BF16),
                   _dil_shape(s, 4, WIDTH_A, F32), _dil_shape(s, 16, WIDTH_A, F32)],
        scratch_shapes=[_slab_scratch(tm, WIDTH_A)],
        compiler_params=_params(("arbitrary",)),
    )(dyn, ya, yb, na, nb, ones)


def _sum_cast(parts, name):
    s = parts[0][0].shape[0]
    tm = min(512, s)

    def body(*refs):
        o_ref, s4, s16 = refs[9:]
        for t in range(3):
            _undilate(s4, refs[3 + t], 4)
            _undilate(s16, refs[6 + t], 16)
            acc = refs[t][...] + _get(s4) + _get(s16)
            o_ref[:, t * WIDTH_A:(t + 1) * WIDTH_A] = acc.astype(BF16)

    nat = pl.BlockSpec((tm, WIDTH_A), lambda i: (i, 0))
    flat = [parts[g][t] for g in range(3) for t in range(3)]
    return pl.pallas_call(
        body, name=name, grid=(s // tm,),
        in_specs=[nat] * 3 + [_dil_spec(tm, 4, WIDTH_A)] * 3 + [_dil_spec(tm, 16, WIDTH_A)] * 3,
        out_specs=pl.BlockSpec((tm, 3 * WIDTH_A), lambda i: (i, 0)),
        out_shape=jax.ShapeDtypeStruct((s, 3 * WIDTH_A), BF16),
        scratch_shapes=[_slab_scratch(tm, WIDTH_A)] * 2,
        compiler_params=_params(("parallel",)),
    )(*flat)


HALO = 16


def _gelu(x):
    k = math.sqrt(2.0 / math.pi)
    t = jnp.tanh(k * (x + 0.044715 * x * x * x))
    return 0.5 * x * (1.0 + t), t


def _gelu_grad(x, t):
    k = math.sqrt(2.0 / math.pi)
    return 0.5 * (1.0 + t) + 0.5 * x * (1.0 - t * t) * k * (1.0 + 3 * 0.044715 * x * x)


def _halo_specs(s, tm, tn, lead):
    nb = s // HALO
    hb = tm // HALO
    pre = (lead,) if lead else ()
    z = (0,) if lead else ()
    main = pl.BlockSpec(pre + (tm, tn), lambda j, i: z + (i, j))
    prev = pl.BlockSpec(pre + (HALO, tn), lambda j, i: z + (jnp.maximum(i * hb - 1, 0), j))
    nxt = pl.BlockSpec(pre + (HALO, tn), lambda j, i: z + (jnp.minimum((i + 1) * hb, nb - 1), j))
    return [prev, main, nxt]


def _fill_ext(ext, prev, main, nxt, i, tm, s):
    ext[0:HALO, :] = jnp.where(i > 0, prev.astype(F32), 0.0)
    ext[HALO:HALO + tm, :] = main.astype(F32)
    ext[HALO + tm:2 * HALO + tm, :] = jnp.where((i + 1) * tm < s, nxt.astype(F32), 0.0)


STRIP = 16


def _shifted(ref, row0):
    n = STRIP + 16
    win = ref[pl.ds(pl.multiple_of(row0 - 8, 8), n), :]
    return pltpu.roll(win, 1, 0)[8:8 + STRIP], win[8:8 + STRIP], pltpu.roll(win, n - 1, 0)[8:8 + STRIP]


def _conv3(e, row0, w_ref, b_ref, t):
    m1, c0, p1 = _shifted(e, row0)
    return w_ref[t, 0:1, :] * m1 + w_ref[t, 1:2, :] * c0 + w_ref[t, 2:3, :] * p1 + b_ref[t]


def _conv_gate(up, cw, cb, name):
    _, s, c = up.shape
    tm = min(512, s)
    tn = FF_SLAB

    def body(up_p, up_m, up_n, w_ref, b_ref, a_ref, eg, ev):
        i = pl.program_id(1)
        _fill_ext(eg, up_p[0], up_m[0], up_n[0], i, tm, s)
        _fill_ext(ev, up_p[1], up_m[1], up_n[1], i, tm, s)

        def strip(t, carry):
            r0 = pl.multiple_of(t * STRIP, STRIP)
            g, _ = _gelu(_conv3(eg, HALO + r0, w_ref, b_ref, 0))
            a_ref[pl.ds(r0, STRIP), :] = (g * _conv3(ev, HALO + r0, w_ref, b_ref, 1)).astype(BF16)
            return carry

        lax.fori_loop(0, tm // STRIP, strip, 0)

    return pl.pallas_call(
        body, name=name, grid=(c // tn, s // tm),
        in_specs=_halo_specs(s, tm, tn, 2)
        + [pl.BlockSpec((2, 3, tn), lambda j, i: (0, 0, j)), pl.BlockSpec((2, 1, tn), lambda j, i: (0, 0, j))],
        out_specs=pl.BlockSpec((tm, tn), lambda j, i: (i, j)),
        out_shape=jax.ShapeDtypeStruct((s, c), BF16),
        scratch_shapes=[pltpu.VMEM((tm + 2 * HALO, tn), F32)] * 2,
        compiler_params=_params(("parallel", "parallel")),
    )(up, up, up, cw, cb)


def _conv_gate_bwd(up, da, cw, cb, name):
    _, s, c = up.shape
    tm = min(256, s)
    tn = FF_SLAB
    te = tm + HALO

    def body(up_p, up_m, up_n, da_p, da_m, da_n, w_ref, b_ref, dup_ref, dw_ref, db_ref, eg, ev, ed, dug, duv):
        i = pl.program_id(1)

        @pl.when(i == 0)
        def _():
            dw_ref[...] = jnp.zeros_like(dw_ref)
            db_ref[...] = jnp.zeros_like(db_ref)

        _fill_ext(eg, up_p[0], up_m[0], up_n[0], i, tm, s)
        _fill_ext(ev, up_p[1], up_m[1], up_n[1], i, tm, s)
        _fill_ext(ed, da_p[...], da_m[...], da_n[...], i, tm, s)
        o = HALO // 2

        def du_strip(t, carry):
            r0 = pl.multiple_of(t * STRIP, STRIP)
            ug = _conv3(eg, o + r0, w_ref, b_ref, 0)
            uv = _conv3(ev, o + r0, w_ref, b_ref, 1)
            gl, th = _gelu(ug)
            dav = ed[pl.ds(pl.multiple_of(o + r0, 8), STRIP), :]
            dug[pl.ds(r0, STRIP), :] = dav * uv * _gelu_grad(ug, th)
            duv[pl.ds(r0, STRIP), :] = dav * gl
            return carry

        lax.fori_loop(0, te // STRIP, du_strip, 0)

        def back(du, e, t):
            def strip(k, acc):
                r0 = pl.multiple_of(k * STRIP, STRIP)
                dm1, c0, dp1 = _shifted(du, o + r0)
                dup_ref[t, pl.ds(r0, STRIP), :] = (w_ref[t, 0:1, :] * dp1 + w_ref[t, 1:2, :] * c0
                                                   + w_ref[t, 2:3, :] * dm1).astype(BF16)
                um1, u0, up1 = _shifted(e, HALO + r0)
                fold = lambda a: a[0:8] + a[8:16]
                return (acc[0] + fold(um1 * c0), acc[1] + fold(u0 * c0), acc[2] + fold(up1 * c0), acc[3] + fold(c0))

            zero = jnp.zeros((8, tn), F32)
            acc = lax.fori_loop(0, tm // STRIP, strip, (zero, zero, zero, zero))
            for k in range(3):
                dw_ref[t, k:k + 1, :] += jnp.sum(acc[k], axis=0, keepdims=True)
            db_ref[t] += jnp.sum(acc[3], axis=0, keepdims=True)

        back(dug, eg, 0)
        back(duv, ev, 1)

    wspec = pl.BlockSpec((2, 3, tn), lambda j, i: (0, 0, j))
    bspec = pl.BlockSpec((2, 1, tn), lambda j, i: (0, 0, j))
    return pl.pallas_call(
        body, name=name, grid=(c // tn, s // tm),
        in_specs=_halo_specs(s, tm, tn, 2) + _halo_specs(s, tm, tn, 0) + [wspec, bspec],
        out_specs=[pl.BlockSpec((2, tm, tn), lambda j, i: (0, i, j)), wspec, bspec],
        out_shape=[jax.ShapeDtypeStruct((2, s, c), BF16), jax.ShapeDtypeStruct((2, 3, c), F32),
                   jax.ShapeDtypeStruct((2, 1, c), F32)],
        scratch_shapes=[pltpu.VMEM((tm + 2 * HALO, tn), F32)] * 3 + [pltpu.VMEM((te, tn), F32)] * 2,
        compiler_params=_params(("parallel", "arbitrary")),
    )(up, up, up, da, da, da, cw, cb)


def _ffn_out(a, w_down, g_post, x1, target, name):
    s = x1.shape[0]
    tm = min(256, s)

    def body(a_ref, w_ref, g_ref, x1_ref, t_ref, dy3_ref, dx2_ref, loss_ref, dg_ref):
        i = pl.program_id(0)

        @pl.when(i == 0)
        def _():
            loss_ref[...] = jnp.zeros_like(loss_ref)
            dg_ref[...] = jnp.zeros_like(dg_ref)

        y3 = _dot(a_ref[...], w_ref[...], NN)
        g = g_ref[...]
        x2 = x1_ref[...] + y3 * _rstd(y3) * g
        diff = x2 - t_ref[...]
        loss_ref[...] += jnp.sum(jnp.sum(diff * diff, axis=1, keepdims=True), axis=0, keepdims=True)
        dx2 = diff * (1.0 / D_MODEL)
        dx2_ref[...] = dx2
        dy3, dg = _rms_bwd(dx2, y3, g)
        dy3_ref[...] = dy3.astype(BF16)
        dg_ref[...] += jnp.sum(dg, axis=0, keepdims=True)

    row = lambda n: pl.BlockSpec((tm, n), lambda i: (i, 0))
    full = lambda t: pl.BlockSpec(t.shape, lambda i: (0, 0))
    return pl.pallas_call(
        body, name=name, grid=(s // tm,),
        in_specs=[row(a.shape[1]), full(w_down), full(g_post), row(D_MODEL), row(D_MODEL)],
        out_specs=[row(D_MODEL), row(D_MODEL), pl.BlockSpec((8, 128), lambda i: (0, 0)), full(g_post)],
        out_shape=[jax.ShapeDtypeStruct((s, D_MODEL), BF16), jax.ShapeDtypeStruct((s, D_MODEL), F32),
                   jax.ShapeDtypeStruct((8, 128), F32), jax.ShapeDtypeStruct(g_post.shape, F32)],
        compiler_params=_params(("arbitrary",)),
    )(a, w_down, g_post, x1, target)


def _resnorm_bwd(dh2, x1, g_ffn_pre, dx2, y2, g_mix_post, name):
    s = x1.shape[0]
    tm = min(256, s)

    def body(dh_ref, x1_ref, gf_ref, dx2_ref, y2_ref, gp_ref, dx1_ref, dy2_ref, dgf_ref, dgp_ref):
        i = pl.program_id(0)

        @pl.when(i == 0)
        def _():
            dgf_ref[...] = jnp.zeros_like(dgf_ref)
            dgp_ref[...] = jnp.zeros_like(dgp_ref)

        dn, dgf = _rms_bwd(dh_ref[...], x1_ref[...], gf_ref[...])
        dx1 = dx2_ref[...] + dn
        dx1_ref[...] = dx1
        dgf_ref[...] += jnp.sum(dgf, axis=0, keepdims=True)
        dy2, dgp = _rms_bwd(dx1, y2_ref[...], gp_ref[...])
        dy2_ref[...] = dy2.astype(BF16)
        dgp_ref[...] += jnp.sum(dgp, axis=0, keepdims=True)

    row = pl.BlockSpec((tm, D_MODEL), lambda i: (i, 0))
    full = pl.BlockSpec((1, D_MODEL), lambda i: (0, 0))
    return pl.pallas_call(
        body, name=name, grid=(s // tm,),
        in_specs=[row, row, full, row, row, full],
        out_specs=[row, row, full, full],
        out_shape=[jax.ShapeDtypeStruct((s, D_MODEL), F32), jax.ShapeDtypeStruct((s, D_MODEL), BF16),
                   jax.ShapeDtypeStruct((1, D_MODEL), F32), jax.ShapeDtypeStruct((1, D_MODEL), F32)],
        compiler_params=_params(("arbitrary",)),
    )(dh2, x1, g_ffn_pre, dx2, y2, g_mix_post)


def _final_bwd(dh1, x, g_pre, dx1, name):
    s = x.shape[0]
    tm = min(256, s)

    def body(dh_ref, x_ref, g_ref, dx1_ref, dx_ref, dg_ref):
        @pl.when(pl.program_id(0) == 0)
        def _():
            dg_ref[...] = jnp.zeros_like(dg_ref)

        dn, dg = _rms_bwd(dh_ref[...], x_ref[...], g_ref[...])
        dx_ref[...] = dx1_ref[...] + dn
        dg_ref[...] += jnp.sum(dg, axis=0, keepdims=True)

    row = pl.BlockSpec((tm, D_MODEL), lambda i: (i, 0))
    full = pl.BlockSpec((1, D_MODEL), lambda i: (0, 0))
    return pl.pallas_call(
        body, name=name, grid=(s // tm,),
        in_specs=[row, row, full, row], out_specs=[row, full],
        out_shape=[jax.ShapeDtypeStruct((s, D_MODEL), F32), jax.ShapeDtypeStruct((1, D_MODEL), F32)],
        compiler_params=_params(("arbitrary",)),
    )(dh1, x, g_pre, dx1)


def _local_step(x, target, fw, rep, mixer_weights, early_grads):
    s = x.shape[0]
    tabs = _rope_tables(s)
    w_in, w_uq, w_ukv, w_o = (fw[n] for n in ("w_in", "w_uq", "w_ukv", "w_o"))
    tr = min(1024, s)
    tcon = min(1024, s)

    proj, h1, qkv4, qkv16 = _in_proj(x, rep["norm_mix_pre"], w_in, "in_proj")
    qkv = {1: proj, 4: qkv4, 16: qkv16}
    q_of = lambda r: (qkv[r], lambda c: 3 * c)
    k_of = lambda r: (qkv[r], lambda c: 3 * c + 1)
    v_of = lambda r: (qkv[r], lambda c: 3 * c + 2)
    own = lambda a: (a, lambda c: c)
    biases = [_band_bias(r) for _, r in DIL_CONFIGS]
    os_, lses = [], []
    for g, (_, r) in enumerate(DIL_CONFIGS):
        o, l = _band_call("fwd", r, [q_of(r)], [k_of(r), v_of(r)], biases[g], f"band_fwd_r{r}")
        os_.append(o)
        lses.append(l)
    ya, lse_a, lse4, lse16 = _band_combine(os_, lses, "band_combine")
    qcat, kcat, kvb, cqn, ckvn = _mla_prep(proj, rep["q_lat_norm"], rep["kv_lat_norm"], w_uq, w_ukv, tabs, "mla_prep")
    yb, qaug = _mla_fwd(qcat, kcat, kvb, "mla_fwd")
    yn, y2, x1 = _mix_out(ya, yb, rep["out_norm_a"], rep["out_norm_b"], w_o, rep["norm_mix_post"], x, "mix_out")
    mw = mixer_weights(x1)
    w_up, w_down, cw, cb = mw["w_up"], mw["w_down"], mw["conv_w"], mw["conv_b"]
    ff = w_down.shape[0]
    up, h2 = _norm_matmul(x1, rep["norm_ffn_pre"], w_up, None, "up_proj")
    act = _conv_gate(up, cw, cb, "conv_gate")
    dy3, dx2, loss_acc, dg_ffn_post = _ffn_out(act, w_down, rep["norm_ffn_post"], x1, target, "ffn_out")

    grads = {"norm_ffn_post": dg_ffn_post}
    dact = _matmul(dy3, w_down, "nt", BF16, tr, ff // 2, D_MODEL, "d_act")
    grads["w_down"] = _matmul(act, dy3, "tn", F32, ff // 2, D_MODEL, tcon, "dw_down")
    dup, grads["conv_w"], grads["conv_b"] = _conv_gate_bwd(up, dact, cw, cb, "conv_gate_bwd")
    half = N_DEV // 2
    dh2 = _matmul_core(
        dup, w_up, NT, (s // tr, 1, N_DEV),
        pl.BlockSpec((None, tr, FF_SLAB), lambda i, j, t: (t // half, i, t % half)),
        pl.BlockSpec((None, D_MODEL, FF_SLAB), lambda i, j, t: (t, 0, 0)),
        pl.BlockSpec((tr, D_MODEL), lambda i, j, t: (i, 0)),
        jax.ShapeDtypeStruct((s, D_MODEL), F32), (tr, D_MODEL), "d_h2")
    grads["w_up"] = _matmul_core(
        h2, dup, TN, (1, N_DEV, s // tcon),
        pl.BlockSpec((tcon, D_MODEL), lambda i, j, t: (t, 0)),
        pl.BlockSpec((None, tcon, FF_SLAB), lambda i, j, t: (j // half, t, j % half)),
        pl.BlockSpec((None, D_MODEL, FF_SLAB), lambda i, j, t: (j, 0, 0)),
        jax.ShapeDtypeStruct((N_DEV, D_MODEL, FF_SLAB), F32), (D_MODEL, FF_SLAB), "dw_up")
    dx1, dy2, grads["norm_ffn_pre"], grads["norm_mix_post"] = _resnorm_bwd(
        dh2, x1, rep["norm_ffn_pre"], dx2, y2, rep["norm_mix_post"], "resnorm_bwd")
    dyn = _matmul(dy2, w_o, "nt", F32, tr, D_MODEL, D_MODEL, "d_yn")
    grads["w_o"] = _matmul(yn, dy2, "tn", F32, D_MODEL, D_MODEL, tcon, "dw_o")
    token = early_grads(grads)
    dya, delta_a, doaug, grads["out_norm_a"], grads["out_norm_b"], dya4, dya16, delta4, delta16 = _outnorm_bwd(
        dyn, ya, yb, rep["out_norm_a"] + token, rep["out_norm_b"], _head_ones(), "outnorm_bwd")
    stats = {1: (dya, lse_a, delta_a), 4: (dya4, lse4, delta4), 16: (dya16, lse16, delta16)}
    parts = []
    for g, (_, r) in enumerate(DIL_CONFIGS):
        qside = [q_of(r)] + [own(a) for a in stats[r]]
        kside = [k_of(r), v_of(r)]
        (dq,) = _band_call("dq", r, qside, kside, biases[g], f"band_dq_r{r}")
        dk, dv = _band_call("dkv", r, kside, qside, biases[g], f"band_dkv_r{r}")
        parts.append((dq, dk, dv))
    dproj_a = _sum_cast(parts, "band_grad_sum")
    dqc, dkc, dvp = _mla_bwd(qaug, kcat, kvb, doaug, "mla_bwd")
    dproj_b, grads["w_uq"], grads["w_ukv"], grads["q_lat_norm"], grads["kv_lat_norm"] = _mla_prep_bwd(
        dqc, dkc, dvp, proj, cqn, ckvn, rep["q_lat_norm"], rep["kv_lat_norm"], w_uq, w_ukv, tabs, "mla_prep_bwd")
    dproj = jnp.concatenate([dproj_a, dproj_b], axis=1)
    dh1 = _matmul(dproj, w_in, "nt", F32, tr, D_MODEL, N_PROJ // 2, "d_h1")
    grads["w_in"] = _matmul(h1, dproj, "tn", F32, D_MODEL, N_PROJ // 2, tcon, "dw_in")
    grad_x, grads["norm_mix_pre"] = _final_bwd(dh1, x, rep["norm_mix_pre"], dx1, "final_bwd")
    loss = 0.5 / D_MODEL * loss_acc[0, 0]
    return loss, grad_x, grads


MESH = pl.DeviceIdType.MESH
HBM_SPEC = pl.BlockSpec(memory_space=pltpu.HBM)
SMALL_ROWS = 96
BUF_SHAPES = {"w_up": (D_MODEL, FF_SLAB), "w_in": (D_MODEL, 384), "w_down": (D_FF // N_DEV, D_MODEL),
              "w_o": (D_MODEL // N_DEV, D_MODEL), "w_uq": (Q_LORA, 128), "w_ukv": (KV_LORA, 128), "conv_w": (8, FF_SLAB)}
BUF_ORDER = tuple(BUF_SHAPES)
MIXING = ("w_in", "w_o", "w_uq", "w_ukv")
MIXER = ("w_up", "w_down", "conv_w")
EARLY_GRADS = ("w_up", "w_down", "w_o", "conv_w")
LATE_GRADS = ("w_in", "w_uq", "w_ukv")


def _all_gather(bufs, name):
    nb = len(bufs)

    def body(*refs):
        x_refs, out_refs = refs[:nb], refs[nb:2 * nb]
        send_sems, recv_sems, local_sems = refs[2 * nb:]
        x, y, c = lax.axis_index("x"), lax.axis_index("y"), lax.axis_index("c")
        me, sibling = (x, y, c), (x, y, 1 - c)
        chips = [(1 - x, y), (x, 1 - y), (1 - x, 1 - y)]

        def copy(b, k, block, to, own=False):
            px, py, pc = block
            slot = out_refs[b].at[4 * px + 2 * py + pc]
            return pltpu.make_async_remote_copy(
                src_ref=x_refs[b] if own else slot, dst_ref=slot,
                send_sem=send_sems.at[7 * b + k], recv_sem=recv_sems.at[7 * b + k], device_id=to, device_id_type=MESH)

        mine = [pltpu.make_async_copy(x_refs[b], out_refs[b].at[4 * x + 2 * y + c], local_sems.at[b]) for b in range(nb)]
        sends = []
        for b in range(nb):
            mine[b].start()
            first = [copy(b, 0, me, sibling, own=True)]
            first += [copy(b, 1 + j, me, (*chip, c), own=True) for j, chip in enumerate(chips)]
            for cp in first:
                cp.start()
            sends += first
        for j, chip in enumerate(chips):
            for b in range(nb):
                copy(b, 1 + j, (*chip, c), me).wait_recv()
                passed = copy(b, 4 + j, (*chip, c), sibling)
                passed.start()
                sends.append(passed)
        for b in range(nb):
            copy(b, 0, sibling, me).wait_recv()
            for j, chip in enumerate(chips):
                copy(b, 4 + j, (*chip, 1 - c), me).wait_recv()
        for cp in sends:
            cp.wait_send()
        for cp in mine:
            cp.wait()

    return pl.pallas_call(
        body, name=name,
        out_shape=[jax.ShapeDtypeStruct((N_DEV,) + p.shape, p.dtype) for p in bufs],
        in_specs=[HBM_SPEC] * nb, out_specs=[HBM_SPEC] * nb,
        scratch_shapes=[pltpu.SemaphoreType.DMA((7 * nb,)), pltpu.SemaphoreType.DMA((7 * nb,)),
                        pltpu.SemaphoreType.DMA((nb,))],
    )(*bufs)


def _grad_exchange(bigs, small, name):
    flips = [(fx, fy, fc) for fx in (0, 1) for fy in (0, 1) for fc in (0, 1)][1:]
    nb = len(bigs)

    def body(*refs):
        big_refs, small_ref = refs[:nb], refs[nb]
        rbig_refs, rsmall_ref = refs[nb + 1:2 * nb + 1], refs[2 * nb + 1]
        send_sems, recv_sems, local_sems = refs[2 * nb + 2:]
        x, y, c = lax.axis_index("x"), lax.axis_index("y"), lax.axis_index("c")
        my = 4 * x + 2 * y + c
        own = [pltpu.make_async_copy(big_refs[b].at[my], rbig_refs[b].at[my], local_sems.at[b]) for b in range(nb)]
        own.append(pltpu.make_async_copy(small_ref, rsmall_ref.at[my], local_sems.at[nb]))
        for cp in own:
            cp.start()
        copies = []
        for b in range(nb + 1):
            for k, (fx, fy, fc) in enumerate(flips):
                px = 1 - x if fx else x
                py = 1 - y if fy else y
                pc = 1 - c if fc else c
                src = small_ref if b == nb else big_refs[b].at[4 * px + 2 * py + pc]
                dst = rsmall_ref.at[my] if b == nb else rbig_refs[b].at[my]
                copies.append(pltpu.make_async_remote_copy(
                    src_ref=src, dst_ref=dst, send_sem=send_sems.at[7 * b + k], recv_sem=recv_sems.at[7 * b + k],
                    device_id=(px, py, pc), device_id_type=MESH))
        for cp in copies:
            cp.start()
        for cp in copies:
            cp.wait()
        for cp in own:
            cp.wait()

    nsem = 7 * (nb + 1)
    return pl.pallas_call(
        body, name=name,
        out_shape=[jax.ShapeDtypeStruct(b.shape, b.dtype) for b in bigs]
        + [jax.ShapeDtypeStruct((N_DEV,) + small.shape, small.dtype)],
        in_specs=[HBM_SPEC] * (nb + 1), out_specs=[HBM_SPEC] * (nb + 1),
        scratch_shapes=[pltpu.SemaphoreType.DMA((nsem,)), pltpu.SemaphoreType.DMA((nsem,)),
                        pltpu.SemaphoreType.DMA((nb + 1,))],
    )(*bigs, small)


SEM_SPEC = pl.BlockSpec(memory_space=pltpu.SEMAPHORE)
ANY_SPEC = pl.BlockSpec(memory_space=pl.ANY)
FLIPS = tuple((fx, fy, fc) for fx in (0, 1) for fy in (0, 1) for fc in (0, 1))[1:]


def _split_copies(src_refs, land_refs, send_sems, recv_sems, scatter):
    x, y, c = lax.axis_index("x"), lax.axis_index("y"), lax.axis_index("c")
    my = 4 * x + 2 * y + c
    copies = []
    for b, (src, land) in enumerate(zip(src_refs, land_refs)):
        for k, (fx, fy, fc) in enumerate(FLIPS):
            px = 1 - x if fx else x
            py = 1 - y if fy else y
            pc = 1 - c if fc else c
            copies.append(pltpu.make_async_remote_copy(
                src_ref=src.at[4 * px + 2 * py + pc] if scatter else src, dst_ref=land.at[my],
                send_sem=send_sems.at[7 * b + k], recv_sem=recv_sems.at[7 * b + k],
                device_id=(px, py, pc), device_id_type=MESH))
    return copies


def _exchange_start(srcs, scatter, name):
    nb = len(srcs)
    lands = [lax.empty(s.shape if scatter else (N_DEV,) + s.shape, s.dtype) for s in srcs]

    def body(*refs):
        src_refs, land_refs = refs[:nb], refs[nb:2 * nb]
        send_sems, recv_sems = refs[2 * nb], refs[2 * nb + 1]
        token = refs[-1]
        for cp in _split_copies(src_refs, land_refs, send_sems, recv_sems, scatter):
            cp.start()
        token[...] = jnp.zeros_like(token)

    hbm = lambda a: pltpu.HBM(a.shape, a.dtype)
    outs = pl.pallas_call(
        body, name=name,
        out_shape=(pltpu.SemaphoreType.DMA((7 * nb,)), pltpu.SemaphoreType.DMA((7 * nb,)),
                   *[hbm(a) for a in srcs], *[hbm(a) for a in lands], jax.ShapeDtypeStruct((8, 128), F32)),
        in_specs=[HBM_SPEC] * (2 * nb),
        out_specs=(SEM_SPEC, SEM_SPEC, *[HBM_SPEC] * (2 * nb), pl.BlockSpec(memory_space=pltpu.VMEM)),
        input_output_aliases={i: 2 + i for i in range(2 * nb)},
        compiler_params=pltpu.CompilerParams(has_side_effects=pltpu.SideEffectType.DATAFLOW_SIDE_EFFECTING),
    )(*[pltpu.with_memory_space_constraint(a, pltpu.HBM) for a in srcs],
      *[pltpu.with_memory_space_constraint(a, pltpu.HBM) for a in lands])
    return outs[0], outs[1], list(outs[2:2 + nb]), list(outs[2 + nb:2 + 2 * nb]), outs[-1]


def _exchange_wait(started, scatter, after, name):
    send_sems, recv_sems, srcs, lands, _ = started
    nb = len(srcs)

    def body(*refs):
        src_refs, land_refs = refs[:nb], refs[nb:2 * nb]
        for cp in _split_copies(src_refs, land_refs, refs[2 * nb], refs[2 * nb + 1], scatter):
            cp.wait_send()
            cp.wait_recv()

    hbm = lambda a: pltpu.HBM(a.shape, a.dtype)
    outs = pl.pallas_call(
        body, name=name,
        out_shape=(*[hbm(a) for a in srcs], *[hbm(a) for a in lands]),
        in_specs=[HBM_SPEC] * (2 * nb) + [SEM_SPEC, SEM_SPEC, ANY_SPEC],
        out_specs=tuple([HBM_SPEC] * (2 * nb)),
        input_output_aliases={i: i for i in range(2 * nb)},
        compiler_params=pltpu.CompilerParams(has_side_effects=pltpu.SideEffectType.DATAFLOW_SIDE_EFFECTING),
    )(*srcs, *lands, send_sems, recv_sems, after)
    return list(outs[:nb]), list(outs[nb:])


def _own_slot(land, own):
    my = 4 * lax.axis_index("x") + 2 * lax.axis_index("y") + lax.axis_index("c")
    return lax.dynamic_update_slice(land, own[None], (my,) + (0,) * own.ndim)


def _adamw(parts, w, m, v, name):
    rows, n = w.shape
    tm = rows if rows <= 384 else 256
    assert rows % tm == 0

    def body(p_ref, w_ref, m_ref, v_ref, g_ref, d_ref, m2_ref, v2_ref):
        g = p_ref[0, :, 0:n].astype(F32)
        for s in range(1, N_DEV):
            g = g + p_ref[s, :, 0:n].astype(F32)
        g_ref[...] = g
        m2 = ADAM_B1 * m_ref[...] + (1.0 - ADAM_B1) * g
        v2 = ADAM_B2 * v_ref[...] + (1.0 - ADAM_B2) * jnp.square(g)
        m2_ref[...] = m2
        v2_ref[...] = v2
        m_hat = m2 / (1.0 - ADAM_B1 ** ADAM_STEP)
        v_hat = v2 / (1.0 - ADAM_B2 ** ADAM_STEP)
        d_ref[...] = -ADAM_LR * (m_hat / (jnp.sqrt(v_hat) + ADAM_EPS) + ADAM_WD * w_ref[...])

    row = pl.BlockSpec((tm, n), lambda i: (i, 0))
    return pl.pallas_call(
        body, name=name, grid=(rows // tm,),
        in_specs=[pl.BlockSpec((N_DEV, tm, parts.shape[2]), lambda i: (0, i, 0)), row, row, row],
        out_specs=[row] * 4,
        out_shape=[jax.ShapeDtypeStruct((rows, n), F32)] * 4,
        compiler_params=_params(("parallel",)),
    )(parts, w, m, v)


def _pack(flat_parts, rows):
    flat = jnp.concatenate(flat_parts, axis=-1)
    pad = rows * LANES - flat.shape[-1]
    flat = jnp.pad(flat, [(0, 0)] * (flat.ndim - 1) + [(0, pad)])
    return flat.reshape(flat.shape[:-1] + (rows, LANES))


def _unpack(packed, shapes):
    flat = packed.reshape(packed.shape[:-2] + (-1,))
    out, off = {}, 0
    for name, shape in shapes.items():
        n = int(np.prod(shape))
        out[name] = flat[..., off:off + n].reshape(flat.shape[:-1] + tuple(shape))
        off += n
    return out


def _pad_to(a, shape):
    return jnp.pad(a, [(0, t - d) for d, t in zip(a.shape, shape)])


def _pad_w_in(w):
    k = w.shape[0]
    z = lambda n: jnp.zeros((k, n), w.dtype)
    return jnp.concatenate([w[:, :COL_KR], z(64), w[:, COL_KR:], z(32)], axis=1)


def _unpad_w_in(w):
    return jnp.concatenate([w[:, :COL_KR], w[:, COL_KR + 64:COL_KR + 96]], axis=1)


def _assemble_weights(g, conv_b):
    half = N_DEV // 2
    cols = lambda a: a.transpose(1, 0, 2).reshape(a.shape[1], N_DEV * a.shape[2])
    make = {
        "w_in": lambda: _pad_w_in(cols(g["w_in"][:, :, :D_IN // N_DEV])),
        "w_uq": lambda: cols(g["w_uq"]),
        "w_ukv": lambda: cols(g["w_ukv"]),
        "w_o": lambda: g["w_o"].reshape(D_MODEL, D_MODEL),
        "w_up": lambda: g["w_up"],
        "w_down": lambda: _pad_to(g["w_down"].reshape(half, FF_SHARD, D_MODEL),
                                  (half, FF_SLAB, D_MODEL)).reshape(half * FF_SLAB, D_MODEL),
        "conv_w": lambda: g["conv_w"][:, :3].reshape(2, half, 3, FF_SLAB).transpose(0, 2, 1, 3).reshape(2, 3, half * FF_SLAB),
    }
    fw = {n: make[n]() for n in g}
    if conv_b is not None:
        fw["conv_b"] = _pad_to(conv_b.reshape(2, 1, half, FF_SHARD), (2, 1, half, FF_SLAB)).reshape(2, 1, half * FF_SLAB)
    return fw


def _grad_bufs(grads, names):
    half = N_DEV // 2
    slabs = lambda a: a.reshape(a.shape[0], N_DEV, a.shape[1] // N_DEV).transpose(1, 0, 2)
    make = {
        "w_in": lambda: _pad_to(slabs(_unpad_w_in(grads["w_in"])), (N_DEV,) + BUF_SHAPES["w_in"]),
        "w_uq": lambda: slabs(grads["w_uq"]),
        "w_ukv": lambda: slabs(grads["w_ukv"]),
        "w_o": lambda: grads["w_o"].reshape((N_DEV,) + BUF_SHAPES["w_o"]),
        "w_up": lambda: grads["w_up"],
        "w_down": lambda: grads["w_down"].reshape(half, FF_SLAB, D_MODEL)[:, :FF_SHARD].reshape((N_DEV,) + BUF_SHAPES["w_down"]),
        "conv_w": lambda: _pad_to(grads["conv_w"].reshape(2, 3, half, FF_SLAB).transpose(0, 2, 1, 3).reshape(N_DEV, 3, FF_SLAB),
                                  (N_DEV,) + BUF_SHAPES["conv_w"]),
    }
    return [make[n]() if n == "conv_w" else make[n]().astype(BF16) for n in names]


def kernel(x, norm_mix_pre, w_in, q_lat_norm, w_uq, kv_lat_norm, w_ukv, out_norm_a, out_norm_b, w_o, norm_mix_post, norm_ffn_pre, w_up, conv_w, conv_b, w_down, norm_ffn_post, loss_target, m_norm_mix_pre, m_w_in, m_q_lat_norm, m_w_uq, m_kv_lat_norm, m_w_ukv, m_out_norm_a, m_out_norm_b, m_w_o, m_norm_mix_post, m_norm_ffn_pre, m_w_up, m_conv_w, m_conv_b, m_w_down, m_norm_ffn_post, v_norm_mix_pre, v_w_in, v_q_lat_norm, v_w_uq, v_kv_lat_norm, v_w_ukv, v_out_norm_a, v_out_norm_b, v_w_o, v_norm_mix_post, v_norm_ffn_pre, v_w_up, v_conv_w, v_conv_b, v_w_down, v_norm_ffn_post):
    given = dict(locals())
    w = {n: given[n][0] for n in WEIGHTS}
    m = {n: given["m_" + n][0] for n in WEIGHTS}
    v = {n: given["v_" + n][0] for n in WEIGHTS}
    rep_shapes = {n: w[n].shape for n in REPLICATED}

    buf = lambda n: _pad_to(w[n] if n == "conv_w" else w[n].astype(BF16), BUF_SHAPES[n])
    first = dict(zip(MIXING, _all_gather([buf(n) for n in MIXING], "weight_all_gather")))
    fw = _assemble_weights(first, None)
    tie = first["w_o"][0, 0, 0].astype(F32) * 0.0
    late_bufs = [buf(n) + tie.astype(w[n].dtype if n == "conv_w" else BF16) for n in MIXER]
    mixer_started = _exchange_start(late_bufs, False, "mixer_weights_start")
    rep = {n: given[n] for n in REPLICATED}
    rep["norm_mix_pre"] = rep["norm_mix_pre"] + mixer_started[4][0, 0]

    def mixer_weights(after):
        srcs, lands = _exchange_wait(mixer_started, False, after, "mixer_weights_wait")
        got = {n: _own_slot(land, own) for n, land, own in zip(MIXER, lands, srcs)}
        return _assemble_weights(got, conv_b)

    early = {}

    def early_grads(grads):
        early["started"] = _exchange_start(_grad_bufs(grads, EARLY_GRADS), True, "early_grads_start")
        return early["started"][4][0, 0]

    loss_local, grad_x, grads = _local_step(x[0], loss_target[0], fw, rep, mixer_weights, early_grads)

    late = _grad_bufs(grads, LATE_GRADS)
    grads["conv_b"] = grads["conv_b"].reshape(N_DEV, FF_SLAB)[:, :FF_SHARD]
    small = _pack([grads[n].reshape(-1) for n in REPLICATED], SMALL_ROWS)
    received_late = _grad_exchange(late, small, "grad_exchange")
    srcs, lands = _exchange_wait(early["started"], True, grad_x, "early_grads_wait")
    my = 4 * lax.axis_index("x") + 2 * lax.axis_index("y") + lax.axis_index("c")
    received = {n: _own_slot(land, lax.dynamic_index_in_dim(src, my, 0, keepdims=False))
                for n, land, src in zip(EARLY_GRADS, lands, srcs)}
    received.update(zip(LATE_GRADS, received_late[:-1]))
    results = [{}, {}, {}, {}]
    for n in BUF_ORDER:
        parts = received[n]
        if n == "conv_w":
            args = [_pad_to(t[n], BUF_SHAPES[n]) for t in (w, m, v)]
        else:
            args = [w[n], m[n], v[n]]
        outs = _adamw(parts, *args, f"adamw_{n}")
        for t in range(4):
            results[t][n] = outs[t][:w[n].shape[0], :w[n].shape[1]] if n == "conv_w" else outs[t]
    pk = lambda d: _pack([d[n].reshape(-1) for n in REPLICATED], SMALL_ROWS)
    small_out = _adamw(received_late[-1], pk(w), pk(m), pk(v), "adamw_replicated")
    for t in range(4):
        results[t].update(_unpack(small_out[t], rep_shapes))

    loss = lax.psum(loss_local, ("x", "y", "c"))
    outs = [loss, grad_x[None]]
    for res in results:
        outs += [res[n][None] for n in WEIGHTS]
    return tuple(outs)
```

```python
import functools
import math

import numpy as np
import jax
import jax.numpy as jnp
from jax import lax
from jax.experimental import pallas as pl
from jax.experimental.pallas import tpu as pltpu

F32 = jnp.float32
BF16 = jnp.bfloat16

D_MODEL = 1024
N_DEV = 8
WIDTH_A = 512
N_HEADS = 8
Q_LORA = 384
KV_LORA = 256
QK_ROPE = 32
QK_NOPE = 64
D_FF = 2816
FF_SHARD = 2 * D_FF // N_DEV
FF_SLAB = 768
DIL_CONFIGS = ((128, 1), (512, 4), (2048, 16))
BAND_HALF = 64
ROPE_BASE = 10000.0
EPS = 1e-6
NEG = -1e30
MLA_SCALE = (QK_NOPE + QK_ROPE) ** -0.5
SCALE_A = 0.125
LOG2E = 1.0 / math.log(2.0)
LN2 = math.log(2.0)

COL_CQ = 3 * WIDTH_A
COL_CKV = COL_CQ + Q_LORA
COL_KR = COL_CKV + KV_LORA
N_PROJ = COL_KR + 128
N_LAT = N_PROJ - COL_CQ
D_IN = COL_KR + QK_ROPE

ADAM_LR = 0.001
ADAM_B1 = 0.9
ADAM_B2 = 0.999
ADAM_EPS = 1e-08
ADAM_WD = 0.01
ADAM_STEP = 10

LANES = 128
VMEM_LIMIT = 56 * 1024 * 1024

SHARDED = ("w_in", "w_uq", "w_ukv", "w_o", "w_up", "conv_w", "w_down")
REPLICATED = ("norm_mix_pre", "q_lat_norm", "kv_lat_norm", "out_norm_a", "out_norm_b", "norm_mix_post",
              "norm_ffn_pre", "conv_b", "norm_ffn_post")
WEIGHTS = ("norm_mix_pre", "w_in", "q_lat_norm", "w_uq", "kv_lat_norm", "w_ukv", "out_norm_a", "out_norm_b", "w_o",
           "norm_mix_post", "norm_ffn_pre", "w_up", "conv_w", "conv_b", "w_down", "norm_ffn_post")


def _params(sem=None):
    return pltpu.CompilerParams(dimension_semantics=sem, vmem_limit_bytes=VMEM_LIMIT)


def _dot(a, b, dims):
    return lax.dot_general(a, b, (dims, ((), ())), preferred_element_type=F32)


NN = ((1,), (0,))
NT = ((1,), (1,))
TN = ((0,), (0,))


def _rstd(x):
    return lax.rsqrt(jnp.mean(x * x, axis=-1, keepdims=True) + EPS)


def _rms_bwd(dy, x, g):
    r = _rstd(x)
    z = x * r
    gy = dy * g
    dx = r * (gy - z * jnp.mean(gy * z, axis=-1, keepdims=True))
    return dx, dy * z


def _split_hi_lo(v):
    hi = v.astype(BF16)
    lo = (v - hi.astype(F32)).astype(BF16)
    return hi, lo


def _matmul(a, b, mode, out_dtype, tm, tn, tk, name):
    if mode == "nn":
        (m, k), n = a.shape, b.shape[1]
        a_spec = pl.BlockSpec((tm, tk), lambda i, j, s: (i, s))
        b_spec = pl.BlockSpec((tk, tn), lambda i, j, s: (s, j))
        dims = NN
    elif mode == "nt":
        (m, k), n = a.shape, b.shape[0]
        a_spec = pl.BlockSpec((tm, tk), lambda i, j, s: (i, s))
        b_spec = pl.BlockSpec((tn, tk), lambda i, j, s: (j, s))
        dims = NT
    else:
        (k, m), n = a.shape, b.shape[1]
        a_spec = pl.BlockSpec((tk, tm), lambda i, j, s: (s, i))
        b_spec = pl.BlockSpec((tk, tn), lambda i, j, s: (s, j))
        dims = TN
    assert m % tm == 0 and n % tn == 0 and k % tk == 0, (name, m, n, k, tm, tn, tk)
    return _matmul_core(a, b, dims, (m // tm, n // tn, k // tk), a_spec, b_spec,
                        pl.BlockSpec((tm, tn), lambda i, j, s: (i, j)), jax.ShapeDtypeStruct((m, n), out_dtype),
                        (tm, tn), name)


def _matmul_core(a, b, dims, grid, a_spec, b_spec, o_spec, out_sds, acc_shape, name):
    nk = grid[2]

    def body(a_ref, b_ref, o_ref, acc_ref):
        s = pl.program_id(2)

        @pl.when(s == 0)
        def _():
            acc_ref[...] = jnp.zeros_like(acc_ref)

        acc_ref[...] += _dot(a_ref[...].astype(BF16), b_ref[...].astype(BF16), dims)

        @pl.when(s == nk - 1)
        def _():
            o_ref[...] = acc_ref[...].astype(out_sds.dtype)

    return pl.pallas_call(
        body, name=name, grid=grid, in_specs=[a_spec, b_spec], out_specs=o_spec, out_shape=out_sds,
        scratch_shapes=[pltpu.VMEM(acc_shape, F32)],
        compiler_params=_params(("parallel", "parallel", "arbitrary")),
    )(a, b)


def _norm_matmul(x, g, w, tn, name):
    s, k = x.shape
    tm = min(2048, s)
    if w.ndim == 3:
        nj, _, tn = w.shape
        half = nj // 2
        w_spec = pl.BlockSpec((None, k, tn), lambda i, j: (j, 0, 0))
        o_spec = pl.BlockSpec((None, tm, tn), lambda i, j: (j // half, i, j % half))
        o_sds = jax.ShapeDtypeStruct((2, s, half * tn), BF16)
    else:
        n = w.shape[1]
        assert n % tn == 0
        nj = n // tn
        w_spec = pl.BlockSpec((k, tn), lambda i, j: (0, j))
        o_spec = pl.BlockSpec((tm, tn), lambda i, j: (i, j))
        o_sds = jax.ShapeDtypeStruct((s, n), BF16)

    def body(x_ref, g_ref, w_ref, o_ref, h_ref):
        @pl.when(pl.program_id(1) == 0)
        def _():
            xv = x_ref[...]
            h_ref[...] = (xv * _rstd(xv) * g_ref[...]).astype(BF16)

        o_ref[...] = _dot(h_ref[...], w_ref[...], NN).astype(BF16)

    return pl.pallas_call(
        body, name=name, grid=(s // tm, nj),
        in_specs=[pl.BlockSpec((tm, k), lambda i, j: (i, 0)),
                  pl.BlockSpec((1, k), lambda i, j: (0, 0)),
                  w_spec],
        out_specs=[o_spec, pl.BlockSpec((tm, k), lambda i, j: (i, 0))],
        out_shape=[o_sds, jax.ShapeDtypeStruct((s, k), BF16)],
        compiler_params=_params(("parallel", "arbitrary")),
    )(x, g, w)


def _band_bias(r):
    off = np.arange(256)[None, :] - BAND_HALF - np.arange(128)[:, None]
    slopes = np.exp2(-8.0 * np.arange(1, N_HEADS + 1, dtype=np.float32) / N_HEADS).astype(np.float32)
    dist = (np.abs(off) * r).astype(np.float32)
    bias = -slopes[:, None, None] * dist[None]
    bias = np.where((np.abs(off) <= BAND_HALF)[None], bias, np.float32(NEG))
    return jnp.asarray(bias, F32)


def _band_call(mode, r, center, window, bias, name):
    seq = center[0][0].shape[0]
    tq = min(512, seq)
    nsub = tq // 128
    hb = tq // BAND_HALF
    nh = seq // BAND_HALF
    nc, nw = len(center), len(window)
    n_out = {"fwd": 2, "dq": 1, "dkv": 2}[mode]

    def specs(col):
        return (pl.BlockSpec((BAND_HALF, WIDTH_A), lambda c, i: (jnp.maximum(i * hb - 1, 0), col(c))),
                pl.BlockSpec((tq, WIDTH_A), lambda c, i: (i, col(c))),
                pl.BlockSpec((BAND_HALF, WIDTH_A), lambda c, i: (jnp.minimum((i + 1) * hb, nh - 1), col(c))))

    cspec = pl.BlockSpec((tq, WIDTH_A), lambda c, i: (i, c))
    in_specs = [specs(col)[1] for _, col in center]
    operands = [a for a, _ in center]
    for a, col in window:
        in_specs += list(specs(col))
        operands += [a, a, a]
    in_specs.append(pl.BlockSpec((N_HEADS, 128, 256), lambda c, i: (0, 0, 0)))
    operands.append(bias)
    window = [a for a, _ in window]

    def aug_stat(base, stat_sw, lane, act, e0):
        hi, lo = _split_hi_lo(stat_sw)
        return jnp.where(act, base, jnp.where(lane == e0, -hi, jnp.where(lane == e0 + 1, -lo, jnp.zeros_like(hi))))

    def aug_ones(base, lane, e0):
        return jnp.where((lane == e0) | (lane == e0 + 1), jnp.ones_like(base), base)

    def body(*refs):
        c_refs = refs[:nc]
        w_refs = refs[nc:nc + 3 * nw]
        bias_ref = refs[nc + 3 * nw]
        o_refs = refs[nc + 3 * nw + 1:nc + 3 * nw + 1 + n_out]
        wins = refs[nc + 3 * nw + 1 + n_out:]
        i = pl.program_id(1)
        for t in range(nw):
            wins[t][0:BAND_HALF, :] = w_refs[3 * t][...]
            wins[t][BAND_HALF:BAND_HALF + tq, :] = w_refs[3 * t + 1][...]
            wins[t][BAND_HALF + tq:BAND_HALF + tq + BAND_HALF, :] = w_refs[3 * t + 2][...]

        def sub(j, carry):
            r0 = pl.multiple_of(j * 128, 128)
            wpos = i * tq + j * 128 - BAND_HALF + lax.broadcasted_iota(jnp.int32, (128, 256), 1)
            valid = (wpos >= 0) & (wpos < seq)
            lane_c = lax.broadcasted_iota(jnp.int32, (128, 128), 1)
            lane_w = lax.broadcasted_iota(jnp.int32, (256, 128), 1)
            heads = [(p, a) for p in range(4) for a in range(2)]
            first, last_ops = [], []
            for p, a in heads:
                cols = slice(p * 128, (p + 1) * 128)
                cs = [c[pl.ds(r0, 128), cols] for c in c_refs]
                ws = [w[pl.ds(r0, 256), cols] for w in wins]
                e0 = 64 if a == 0 else 0
                act_c = (lane_c < 64) if a == 0 else (lane_c >= 64)
                act_w = (lane_w < 64) if a == 0 else (lane_w >= 64)
                bias_a = bias_ref[2 * p + a]
                if mode == "fwd":
                    qa = jnp.where(act_c, cs[0] * SCALE_A, jnp.zeros_like(cs[0]))
                    first.append((_dot(qa, ws[0], NT) + bias_a, None))
                    last_ops.append((ws[1],))
                elif mode == "dq":
                    q2, dy2, l2, d2 = cs
                    k2, v2 = ws
                    q_aug = aug_stat(q2 * SCALE_A, pltpu.roll(l2, 64, 1), lane_c, act_c, e0)
                    dy_aug = aug_stat(dy2, pltpu.roll(d2, 64, 1), lane_c, act_c, e0)
                    first.append((_dot(q_aug, aug_ones(k2, lane_w, e0), NT) + bias_a,
                                  _dot(dy_aug, aug_ones(v2, lane_w, e0), NT)))
                    last_ops.append((k2,))
                else:
                    k2, v2 = cs
                    q2, dy2, l2, d2 = ws
                    q_aug = aug_stat(q2 * SCALE_A, pltpu.roll(l2, 64, 1), lane_w, act_w, e0)
                    dy_aug = aug_stat(dy2, pltpu.roll(d2, 64, 1), lane_w, act_w, e0)
                    first.append((_dot(aug_ones(k2, lane_c, e0), q_aug, NT) + bias_a,
                                  _dot(aug_ones(v2, lane_c, e0), dy_aug, NT)))
                    last_ops.append((q_aug, dy_aug))
            mid = []
            for sc, dp in first:
                sc = jnp.where(valid, sc, NEG)
                if mode == "fwd":
                    m = jnp.max(sc, axis=-1, keepdims=True)
                    e = jnp.exp(sc - m)
                    l = jnp.sum(e, axis=-1, keepdims=True)
                    mid.append((e.astype(BF16), l, m + jnp.log(l)))
                else:
                    pr = jnp.exp(sc)
                    mid.append((pr.astype(BF16), (pr * dp).astype(BF16)))
            res = []
            for md, ops in zip(mid, last_ops):
                if mode == "fwd":
                    res.append((_dot(md[0], ops[0], NN) / md[1], jnp.broadcast_to(md[2], (128, 128))))
                elif mode == "dq":
                    res.append((_dot(md[1], ops[0], NN) * SCALE_A,))
                else:
                    res.append((_dot(md[1], ops[0], NN), _dot(md[0], ops[1], NN)))
            for t in range(n_out):
                pairs = [jnp.where(lane_c < 64, res[2 * p][t], res[2 * p + 1][t]) for p in range(4)]
                o_refs[t][pl.ds(r0, 128), :] = jnp.concatenate(pairs, axis=1)
            return carry

        lax.fori_loop(0, nsub, sub, 0)

    outs = pl.pallas_call(
        body, name=name, grid=(r, seq // tq),
        in_specs=in_specs,
        out_specs=[cspec] * n_out,
        out_shape=[jax.ShapeDtypeStruct((seq, r * WIDTH_A), F32)] * n_out,
        scratch_shapes=[pltpu.VMEM((tq + 2 * BAND_HALF, WIDTH_A), w.dtype) for w in window],
        compiler_params=_params(("parallel", "parallel")),
    )(*operands)
    return outs


def _slab_scratch(tm, w):
    return pltpu.VMEM((w // 128, tm, 128), F32)


def _put(scr, val):
    for j in range(scr.shape[0]):
        scr[j] = val[:, j * 128:(j + 1) * 128].astype(F32)


def _get(scr):
    return jnp.concatenate([scr[j] for j in range(scr.shape[0])], axis=1)


def _dilate_store(dst_ref, scr, r):
    nb, tm, _ = scr.shape
    w = nb * 128
    for c in range(r):
        for j in range(nb):
            dst_ref[:, c * w + j * 128:c * w + (j + 1) * 128] = scr[j, pl.ds(c, tm // r, stride=r), :].astype(dst_ref.dtype)


def _undilate(scr, src_ref, r):
    nb, tm, _ = scr.shape
    w = nb * 128
    for c in range(r):
        for j in range(nb):
            scr[j, pl.ds(c, tm // r, stride=r), :] = src_ref[:, c * w + j * 128:c * w + (j + 1) * 128].astype(F32)


def _dil_spec(tm, r, w):
    return pl.BlockSpec((tm // r, r * w), lambda i: (i, 0))


def _dil_shape(s, r, w, dtype):
    return jax.ShapeDtypeStruct((s // r, r * w), dtype)


def _in_proj(x, g, w, name):
    s, k = x.shape
    n = w.shape[1]
    tm = min(512, s)
    qkv = 3 * WIDTH_A

    def body(x_ref, g_ref, w_ref, o_ref, h_ref, d4_ref, d16_ref, scr):
        xv = x_ref[...]
        h = (xv * _rstd(xv) * g_ref[...]).astype(BF16)
        h_ref[...] = h
        acc = _dot(h, w_ref[...], NN)
        o_ref[...] = acc.astype(BF16)
        _put(scr, acc[:, 0:qkv])
        _dilate_store(d4_ref, scr, 4)
        _dilate_store(d16_ref, scr, 16)

    row = lambda c: pl.BlockSpec((tm, c), lambda i: (i, 0))
    return pl.pallas_call(
        body, name=name, grid=(s // tm,),
        in_specs=[row(k), pl.BlockSpec((1, k), lambda i: (0, 0)), pl.BlockSpec((k, n), lambda i: (0, 0))],
        out_specs=[row(n), row(k), _dil_spec(tm, 4, qkv), _dil_spec(tm, 16, qkv)],
        out_shape=[jax.ShapeDtypeStruct((s, n), BF16), jax.ShapeDtypeStruct((s, k), BF16),
                   _dil_shape(s, 4, qkv, BF16), _dil_shape(s, 16, qkv, BF16)],
        scratch_shapes=[_slab_scratch(tm, qkv)],
        compiler_params=_params(("parallel",)),
    )(x, g, w)


def _band_combine(os_, lses, name):
    s = os_[0].shape[0]
    tm = min(512, s)

    def body(o1, o4, o16, l1, l4, l16, ya_ref, lse_ref, lse4_ref, lse16_ref, so4, sl4, so16, sl16):
        _undilate(so4, o4, 4)
        _undilate(sl4, l4, 4)
        _undilate(so16, o16, 16)
        _undilate(sl16, l16, 16)
        a0, a1, a2 = l1[...], _get(sl4), _get(sl16)
        m = jnp.maximum(jnp.maximum(a0, a1), a2)
        e0, e1, e2 = jnp.exp(a0 - m), jnp.exp(a1 - m), jnp.exp(a2 - m)
        den = e0 + e1 + e2
        ya_ref[...] = (e0 * o1[...] + e1 * _get(so4) + e2 * _get(so16)) / den
        lse = m + jnp.log(den)
        lse_ref[...] = lse
        _put(sl4, lse)
        _dilate_store(lse4_ref, sl4, 4)
        _dilate_store(lse16_ref, sl4, 16)

    nat = pl.BlockSpec((tm, WIDTH_A), lambda i: (i, 0))
    d4, d16 = _dil_spec(tm, 4, WIDTH_A), _dil_spec(tm, 16, WIDTH_A)
    return pl.pallas_call(
        body, name=name, grid=(s // tm,), in_specs=[nat, d4, d16] * 2, out_specs=[nat, nat, d4, d16],
        out_shape=[jax.ShapeDtypeStruct((s, WIDTH_A), F32)] * 2
        + [_dil_shape(s, 4, WIDTH_A, F32), _dil_shape(s, 16, WIDTH_A, F32)],
        scratch_shapes=[_slab_scratch(tm, WIDTH_A)] * 4,
        compiler_params=_params(("parallel",)),
    )(*os_, *lses)


def _rope_tables(s):
    pos = jnp.arange(s, dtype=F32)
    inv_freq = jnp.exp(-math.log(ROPE_BASE) * jnp.arange(0, QK_ROPE, 2, dtype=F32) / QK_ROPE)
    ang = pos[:, None] * inv_freq[None, :]
    cos, sin = jnp.cos(ang), jnp.sin(ang)
    one = jnp.ones((s, 64), F32)
    zero16 = jnp.zeros((s, 16), F32)
    c = jnp.concatenate([one, cos, cos, jnp.ones((s, 32), F32)], axis=1)
    sa = jnp.concatenate([jnp.zeros((s, 64), F32), -sin, zero16, jnp.zeros((s, 32), F32)], axis=1)
    sb = jnp.concatenate([jnp.zeros((s, 64), F32), zero16, sin, jnp.zeros((s, 32), F32)], axis=1)
    return c, sa, sb


def _rope_fwd(x, c, sa, sb):
    return x * c + pltpu.roll(x, 112, 1) * sa + pltpu.roll(x, 16, 1) * sb


def _rope_bwd(dy, c, sa, sb):
    return dy * c + pltpu.roll(dy * sa, 16, 1) + pltpu.roll(dy * sb, 112, 1)


def _mla_prep(proj, g_q, g_kv, w_uq, w_ukv, tabs, name):
    s = proj.shape[0]
    tm = min(512, s)
    width = N_HEADS * 128

    def body(lat_ref, gq_ref, gkv_ref, wq_ref, wkv_ref, c_ref, sa_ref, sb_ref,
             q_ref, k_ref, kv_ref, cqn_ref, ckvn_ref):
        c, sa, sb = c_ref[...], sa_ref[...], sb_ref[...]
        cq = lat_ref[:, 0:Q_LORA].astype(F32)
        cqn = (cq * _rstd(cq) * gq_ref[...]).astype(BF16)
        cqn_ref[...] = cqn
        q = _dot(cqn, wq_ref[...], NN)
        ckv = lat_ref[:, Q_LORA:Q_LORA + KV_LORA].astype(F32)
        ckvn = (ckv * _rstd(ckv) * gkv_ref[...]).astype(BF16)
        ckvn_ref[...] = ckvn
        kv = _dot(ckvn, wkv_ref[...], NN)
        lane = lax.broadcasted_iota(jnp.int32, (tm, 128), 1)
        krr = _rope_fwd(lat_ref[:, Q_LORA + KV_LORA:].astype(F32), c, sa, sb)
        krr = jnp.where((lane == 96) | (lane == 97), 1.0, krr)
        ones01 = jnp.where(lane < 2, 1.0, 0.0)
        for h in range(N_HEADS):
            cols = slice(h * 128, (h + 1) * 128)
            q_ref[:, cols] = (_rope_fwd(q[:, cols], c, sa, sb) * (MLA_SCALE * LOG2E)).astype(BF16)
            k_ref[:, cols] = jnp.where(lane < 64, kv[:, cols], krr).astype(BF16)
            kv_ref[:, cols] = jnp.where(lane < 64, ones01, kv[:, cols]).astype(BF16)

    row = lambda n: pl.BlockSpec((tm, n), lambda i: (i, 0))
    full = lambda a: pl.BlockSpec(a.shape, lambda i: (0, 0))
    tab = pl.BlockSpec((tm, 128), lambda i: (i, 0))
    return pl.pallas_call(
        body, name=name, grid=(s // tm,),
        in_specs=[pl.BlockSpec((tm, N_LAT), lambda i: (i, COL_CQ // N_LAT)),
                  full(g_q), full(g_kv), full(w_uq), full(w_ukv), tab, tab, tab],
        out_specs=[row(width), row(width), row(width), row(Q_LORA), row(KV_LORA)],
        out_shape=[jax.ShapeDtypeStruct((s, width), BF16)] * 3
        + [jax.ShapeDtypeStruct((s, Q_LORA), BF16), jax.ShapeDtypeStruct((s, KV_LORA), BF16)],
        compiler_params=_params(("parallel",)),
    )(proj, g_q, g_kv, w_uq, w_ukv, *tabs)


def _mla_fwd(qcat, kcat, kvb, name):
    s = qcat.shape[0]
    tq = min(1024, s)
    tk = min(1024, s)
    nkc = s // tk

    def body(q_ref, k_ref, v_ref, yb_ref, qaug_ref, m_ref, acc_ref):
        lane = lax.broadcasted_iota(jnp.int32, (tq, 128), 1)
        m_ref[...] = jnp.full((2, tq, 128), NEG, F32)
        acc_ref[...] = jnp.zeros((2, tq, 128), F32)

        def chunk(cidx, carry):
            k0 = pl.multiple_of(cidx * tk, tk)
            cols = [slice(a * 128, (a + 1) * 128) for a in range(2)]
            scs = [_dot(q_ref[:, c], k_ref[pl.ds(k0, tk), c], NT) for c in cols]
            prs, alphas = [], []
            for a, sc in enumerate(scs):
                m_prev = m_ref[a]
                m_new = jnp.maximum(m_prev, jnp.max(sc, axis=-1, keepdims=True))
                alphas.append(jnp.exp2(m_prev - m_new))
                prs.append(jnp.exp2(sc - jnp.tile(m_new, (1, tk // 128))).astype(BF16))
                m_ref[a] = m_new
            for a, c in enumerate(cols):
                acc_ref[a] = alphas[a] * acc_ref[a] + _dot(prs[a], v_ref[pl.ds(k0, tk), c], NN)
            return carry

        lax.fori_loop(0, nkc, chunk, 0)
        outs = []
        for a in range(2):
            cols = slice(a * 128, (a + 1) * 128)
            acc = acc_ref[a]
            l = acc[:, 0:1]
            outs.append(acc / l)
            hi, lo = _split_hi_lo(m_ref[a] + jnp.log(l) * LOG2E)
            qaug_ref[:, cols] = jnp.where(lane == 96, -hi, jnp.where(lane == 97, -lo, q_ref[:, cols]))
        yb_ref[...] = jnp.where(lane < 64, pltpu.roll(outs[0], 64, 1), outs[1])

    return pl.pallas_call(
        body, name=name, grid=(4, s // tq),
        in_specs=[pl.BlockSpec((tq, 256), lambda p, i: (i, p)),
                  pl.BlockSpec((s, 256), lambda p, i: (0, p)),
                  pl.BlockSpec((s, 256), lambda p, i: (0, p))],
        out_specs=[pl.BlockSpec((tq, 128), lambda p, i: (i, p)),
                   pl.BlockSpec((tq, 256), lambda p, i: (i, p))],
        out_shape=[jax.ShapeDtypeStruct((s, WIDTH_A), F32), jax.ShapeDtypeStruct((s, N_HEADS * 128), BF16)],
        scratch_shapes=[pltpu.VMEM((2, tq, 128), F32)] * 2,
        compiler_params=_params(("parallel", "parallel")),
    )(qcat, kcat, kvb)


def _mla_bwd(qaug, kcat, kvb, doaug, name):
    s = qaug.shape[0]
    tq = min(512, s)
    tk = min(512, s)
    nqc = s // tq
    width = N_HEADS * 128

    def body(q_ref, do_ref, k_ref, v_ref, dq_ref, dk_ref, dv_ref, dk_acc, dv_acc):
        j = pl.program_id(1)

        @pl.when(j == 0)
        def _():
            dq_ref[...] = jnp.zeros_like(dq_ref)

        dk_acc[...] = jnp.zeros_like(dk_acc)
        dv_acc[...] = jnp.zeros_like(dv_acc)

        def chunk(cidx, carry):
            q0 = pl.multiple_of(cidx * tq, tq)
            cols = [slice(a * 128, (a + 1) * 128) for a in range(2)]
            qs = [q_ref[pl.ds(q0, tq), c] for c in cols]
            dos = [do_ref[pl.ds(q0, tq), c] for c in cols]
            kbs = [k_ref[:, c] for c in cols]
            sts = [_dot(kbs[a], qs[a], NT) for a in range(2)]
            dps = [_dot(v_ref[:, cols[a]], dos[a], NT) for a in range(2)]
            pts, dsts = [], []
            for a in range(2):
                pt = jnp.exp2(sts[a])
                pts.append(pt.astype(BF16))
                dsts.append((pt * dps[a]).astype(BF16))
            for a, c in enumerate(cols):
                dv_acc[:, c] += _dot(pts[a], dos[a], NN)
                dk_acc[:, c] += _dot(dsts[a], qs[a], NN)
                dq_ref[pl.ds(q0, tq), c] += _dot(dsts[a], kbs[a], TN)
            return carry

        lax.fori_loop(0, nqc, chunk, 0)
        dk_ref[...] = dk_acc[...] * LN2
        dv_ref[...] = dv_acc[...]

    return pl.pallas_call(
        body, name=name, grid=(N_HEADS // 2, s // tk),
        in_specs=[pl.BlockSpec((s, 256), lambda p, j: (0, p)),
                  pl.BlockSpec((s, 256), lambda p, j: (0, p)),
                  pl.BlockSpec((tk, 256), lambda p, j: (j, p)),
                  pl.BlockSpec((tk, 256), lambda p, j: (j, p))],
        out_specs=[pl.BlockSpec((s, 256), lambda p, j: (0, p)),
                   pl.BlockSpec((tk, 256), lambda p, j: (j, p)),
                   pl.BlockSpec((tk, 256), lambda p, j: (j, p))],
        out_shape=[jax.ShapeDtypeStruct((s, width), F32)] * 3,
        scratch_shapes=[pltpu.VMEM((tk, 256), F32)] * 2,
        compiler_params=_params(("parallel", "arbitrary")),
    )(qaug, doaug, kcat, kvb)


def _mla_prep_bwd(dqc, dkc, dvp, proj, cqn, ckvn, g_q, g_kv, w_uq, w_ukv, tabs, name):
    s = proj.shape[0]
    tm = min(256, s)
    width = N_HEADS * 128
    n_out_cols = N_LAT

    def body(dq_ref, dk_ref, dv_ref, lat_ref, cqn_ref, ckvn_ref, gq_ref, gkv_ref, wq_ref, wkv_ref,
             c_ref, sa_ref, sb_ref, dproj_ref, dwq_ref, dwkv_ref, dgq_ref, dgkv_ref):
        i = pl.program_id(0)

        @pl.when(i == 0)
        def _():
            dwq_ref[...] = jnp.zeros_like(dwq_ref)
            dwkv_ref[...] = jnp.zeros_like(dwkv_ref)
            dgq_ref[...] = jnp.zeros_like(dgq_ref)
            dgkv_ref[...] = jnp.zeros_like(dgkv_ref)

        c, sa, sb = c_ref[...], sa_ref[...], sb_ref[...]
        lane = lax.broadcasted_iota(jnp.int32, (tm, 128), 1)
        dkr = jnp.zeros((tm, 128), F32)
        dq_parts, dkv_parts = [], []
        for h in range(N_HEADS):
            cols = slice(h * 128, (h + 1) * 128)
            dq_parts.append(_rope_bwd(dq_ref[:, cols] * MLA_SCALE, c, sa, sb).astype(BF16))
            dkh = dk_ref[:, cols]
            dkr = dkr + dkh
            dkv_parts.append(jnp.where(lane < 64, dkh, dv_ref[:, cols]).astype(BF16))
        dq = jnp.concatenate(dq_parts, axis=1)
        dkv = jnp.concatenate(dkv_parts, axis=1)
        dkr = _rope_bwd(jnp.where((lane >= 64) & (lane < 96), dkr, 0.0), c, sa, sb)

        dcqn = _dot(dq, wq_ref[...], NT)
        dwq_ref[...] += _dot(cqn_ref[...], dq, TN)
        dcq, dgq = _rms_bwd(dcqn, lat_ref[:, 0:Q_LORA].astype(F32), gq_ref[...])
        dgq_ref[...] += jnp.sum(dgq, axis=0, keepdims=True)

        dckvn = _dot(dkv, wkv_ref[...], NT)
        dwkv_ref[...] += _dot(ckvn_ref[...], dkv, TN)
        dckv, dgkv = _rms_bwd(dckvn, lat_ref[:, Q_LORA:Q_LORA + KV_LORA].astype(F32), gkv_ref[...])
        dgkv_ref[...] += jnp.sum(dgkv, axis=0, keepdims=True)

        dproj_ref[:, 0:Q_LORA] = dcq.astype(BF16)
        dproj_ref[:, Q_LORA:Q_LORA + KV_LORA] = dckv.astype(BF16)
        dproj_ref[:, Q_LORA + KV_LORA:] = dkr.astype(BF16)

    row = lambda n: pl.BlockSpec((tm, n), lambda i: (i, 0))
    full = lambda a: pl.BlockSpec(a.shape, lambda i: (0, 0))
    tab = pl.BlockSpec((tm, 128), lambda i: (i, 0))
    return pl.pallas_call(
        body, name=name, grid=(s // tm,),
        in_specs=[row(width), row(width), row(width),
                  pl.BlockSpec((tm, N_LAT), lambda i: (i, COL_CQ // N_LAT)),
                  row(Q_LORA), row(KV_LORA), full(g_q), full(g_kv), full(w_uq), full(w_ukv), tab, tab, tab],
        out_specs=[row(n_out_cols), full(w_uq), full(w_ukv), full(g_q), full(g_kv)],
        out_shape=[jax.ShapeDtypeStruct((s, n_out_cols), BF16),
                   jax.ShapeDtypeStruct(w_uq.shape, F32), jax.ShapeDtypeStruct(w_ukv.shape, F32),
                   jax.ShapeDtypeStruct(g_q.shape, F32), jax.ShapeDtypeStruct(g_kv.shape, F32)],
        compiler_params=_params(("arbitrary",)),
    )(dqc, dkc, dvp, proj, cqn, ckvn, g_q, g_kv, w_uq, w_ukv, *tabs)


def _mix_out(ya, yb, na, nb, w_o, g_post, x, name):
    s = x.shape[0]
    tm = min(512, s)

    def body(ya_ref, yb_ref, na_ref, nb_ref, w_ref, g_ref, x_ref, yn_ref, y2_ref, x1_ref):
        a, b = ya_ref[...], yb_ref[...]
        yn = jnp.concatenate([a * _rstd(a) * na_ref[...], b * _rstd(b) * nb_ref[...]], axis=1).astype(BF16)
        yn_ref[...] = yn
        y2 = _dot(yn, w_ref[...], NN)
        y2_ref[...] = y2
        x1_ref[...] = x_ref[...] + y2 * _rstd(y2) * g_ref[...]

    row = lambda n: pl.BlockSpec((tm, n), lambda i: (i, 0))
    full = lambda a: pl.BlockSpec(a.shape, lambda i: (0, 0))
    return pl.pallas_call(
        body, name=name, grid=(s // tm,),
        in_specs=[row(WIDTH_A), row(WIDTH_A), full(na), full(nb), full(w_o), full(g_post), row(D_MODEL)],
        out_specs=[row(D_MODEL)] * 3,
        out_shape=[jax.ShapeDtypeStruct((s, D_MODEL), BF16), jax.ShapeDtypeStruct((s, D_MODEL), F32),
                   jax.ShapeDtypeStruct((s, D_MODEL), F32)],
        compiler_params=_params(("parallel",)),
    )(ya, yb, na, nb, w_o, g_post, x)


def _head_ones():
    blk = np.kron(np.eye(N_HEADS, dtype=np.float32), np.ones((64, 64), np.float32))
    return jnp.asarray(blk, F32)


def _outnorm_bwd(dyn, ya, yb, na, nb, ones, name):
    s = ya.shape[0]
    tm = min(256, s)

    def body(dyn_ref, ya_ref, yb_ref, na_ref, nb_ref, ones_ref, dya_ref, da_ref, do_ref, dna_ref, dnb_ref,
             dya4_ref, dya16_ref, da4_ref, da16_ref, scr):
        i = pl.program_id(0)

        @pl.when(i == 0)
        def _():
            dna_ref[...] = jnp.zeros_like(dna_ref)
            dnb_ref[...] = jnp.zeros_like(dnb_ref)

        a, b = ya_ref[...], yb_ref[...]
        dya, dna = _rms_bwd(dyn_ref[:, 0:WIDTH_A], a, na_ref[...])
        dyb, dnb = _rms_bwd(dyn_ref[:, WIDTH_A:], b, nb_ref[...])
        dna_ref[...] += jnp.sum(dna, axis=0, keepdims=True)
        dnb_ref[...] += jnp.sum(dnb, axis=0, keepdims=True)
        dya_b = dya.astype(BF16)
        dya_ref[...] = dya_b
        hp = lax.Precision.HIGHEST
        delta_a = jnp.dot(dya_b.astype(F32) * a, ones_ref[...], precision=hp, preferred_element_type=F32)
        da_ref[...] = delta_a
        _put(scr, dya_b)
        _dilate_store(dya4_ref, scr, 4)
        _dilate_store(dya16_ref, scr, 16)
        _put(scr, delta_a)
        _dilate_store(da4_ref, scr, 4)
        _dilate_store(da16_ref, scr, 16)
        dyb_b = dyb.astype(BF16)
        db = jnp.dot(dyb_b.astype(F32) * b, ones_ref[...], precision=hp, preferred_element_type=F32)
        lane = lax.broadcasted_iota(jnp.int32, (tm, 128), 1)
        zero = jnp.zeros((tm, 128), BF16)
        for p in range(4):
            cols = slice(p * 128, (p + 1) * 128)
            dyp = dyb_b[:, cols]
            dbp = db[:, cols]
            for a_ in range(2):
                src = pltpu.roll(dyp.astype(F32), 64, 1).astype(BF16) if a_ == 0 else dyp
                dlt = dbp if a_ == 0 else pltpu.roll(dbp, 64, 1)
                hi, lo = _split_hi_lo(dlt)
                blk = jnp.where(lane >= 64, src, jnp.where(lane == 0, -hi, jnp.where(lane == 1, -lo, zero)))
                h = 2 * p + a_
                do_ref[:, h * 128:(h + 1) * 128] = blk

    row = lambda n: pl.BlockSpec((tm, n), lambda i: (i, 0))
    full = lambda a: pl.BlockSpec(a.shape, lambda i: (0, 0))
    return pl.pallas_call(
        body, name=name, grid=(s // tm,),
        in_specs=[row(D_MODEL), row(WIDTH_A), row(WIDTH_A), full(na), full(nb), full(ones)],
        out_specs=[row(WIDTH_A), row(WIDTH_A), row(N_HEADS * 128), full(na), full(nb),
                   _dil_spec(tm, 4, WIDTH_A), _dil_spec(tm, 16, WIDTH_A), _dil_spec(tm, 4, WIDTH_A), _dil_spec(tm, 16, WIDTH_A)],
        out_shape=[jax.ShapeDtypeStruct((s, WIDTH_A), BF16), jax.ShapeDtypeStruct((s, WIDTH_A), F32),
                   jax.ShapeDtypeStruct((s, N_HEADS * 128), BF16),
                   jax.ShapeDtypeStruct(na.shape, F32), jax.ShapeDtypeStruct(nb.shape, F32),
                   _dil_shape(s, 4, WIDTH_A, BF16), _dil_shape(s, 16, WIDTH_A, BF16),
                   _dil_shape(s, 4, WIDTH_A, F32), _dil_shape(s, 16, WIDTH_A, F32)],
        scratch_shapes=[_slab_scratch(tm, WIDTH_A)],
        compiler_params=_params(("arbitrary",)),
    )(dyn, ya, yb, na, nb, ones)


def _sum_cast(parts, name):
    s = parts[0][0].shape[0]
    tm = min(512, s)

    def body(*refs):
        o_ref, s4, s16 = refs[9:]
        for t in range(3):
            _undilate(s4, refs[3 + t], 4)
            _undilate(s16, refs[6 + t], 16)
            acc = refs[t][...] + _get(s4) + _get(s16)
            o_ref[:, t * WIDTH_A:(t + 1) * WIDTH_A] = acc.astype(BF16)

    nat = pl.BlockSpec((tm, WIDTH_A), lambda i: (i, 0))
    flat = [parts[g][t] for g in range(3) for t in range(3)]
    return pl.pallas_call(
        body, name=name, grid=(s // tm,),
        in_specs=[nat] * 3 + [_dil_spec(tm, 4, WIDTH_A)] * 3 + [_dil_spec(tm, 16, WIDTH_A)] * 3,
        out_specs=pl.BlockSpec((tm, 3 * WIDTH_A), lambda i: (i, 0)),
        out_shape=jax.ShapeDtypeStruct((s, 3 * WIDTH_A), BF16),
        scratch_shapes=[_slab_scratch(tm, WIDTH_A)] * 2,
        compiler_params=_params(("parallel",)),
    )(*flat)


HALO = 16


def _gelu(x):
    k = math.sqrt(2.0 / math.pi)
    t = jnp.tanh(k * (x + 0.044715 * x * x * x))
    return 0.5 * x * (1.0 + t), t


def _gelu_grad(x, t):
    k = math.sqrt(2.0 / math.pi)
    return 0.5 * (1.0 + t) + 0.5 * x * (1.0 - t * t) * k * (1.0 + 3 * 0.044715 * x * x)


def _halo_specs(s, tm, tn, lead):
    nb = s // HALO
    hb = tm // HALO
    pre = (lead,) if lead else ()
    z = (0,) if lead else ()
    main = pl.BlockSpec(pre + (tm, tn), lambda j, i: z + (i, j))
    prev = pl.BlockSpec(pre + (HALO, tn), lambda j, i: z + (jnp.maximum(i * hb - 1, 0), j))
    nxt = pl.BlockSpec(pre + (HALO, tn), lambda j, i: z + (jnp.minimum((i + 1) * hb, nb - 1), j))
    return [prev, main, nxt]


def _fill_ext(ext, prev, main, nxt, i, tm, s):
    ext[0:HALO, :] = jnp.where(i > 0, prev.astype(F32), 0.0)
    ext[HALO:HALO + tm, :] = main.astype(F32)
    ext[HALO + tm:2 * HALO + tm, :] = jnp.where((i + 1) * tm < s, nxt.astype(F32), 0.0)


STRIP = 16


def _shifted(ref, row0):
    n = STRIP + 16
    win = ref[pl.ds(pl.multiple_of(row0 - 8, 8), n), :]
    return pltpu.roll(win, 1, 0)[8:8 + STRIP], win[8:8 + STRIP], pltpu.roll(win, n - 1, 0)[8:8 + STRIP]


def _conv3(e, row0, w_ref, b_ref, t):
    m1, c0, p1 = _shifted(e, row0)
    return w_ref[t, 0:1, :] * m1 + w_ref[t, 1:2, :] * c0 + w_ref[t, 2:3, :] * p1 + b_ref[t]


def _conv_gate(up, cw, cb, name):
    _, s, c = up.shape
    tm = min(512, s)
    tn = FF_SLAB

    def body(up_p, up_m, up_n, w_ref, b_ref, a_ref, eg, ev):
        i = pl.program_id(1)
        _fill_ext(eg, up_p[0], up_m[0], up_n[0], i, tm, s)
        _fill_ext(ev, up_p[1], up_m[1], up_n[1], i, tm, s)

        def strip(t, carry):
            r0 = pl.multiple_of(t * STRIP, STRIP)
            g, _ = _gelu(_conv3(eg, HALO + r0, w_ref, b_ref, 0))
            a_ref[pl.ds(r0, STRIP), :] = (g * _conv3(ev, HALO + r0, w_ref, b_ref, 1)).astype(BF16)
            return carry

        lax.fori_loop(0, tm // STRIP, strip, 0)

    return pl.pallas_call(
        body, name=name, grid=(c // tn, s // tm),
        in_specs=_halo_specs(s, tm, tn, 2)
        + [pl.BlockSpec((2, 3, tn), lambda j, i: (0, 0, j)), pl.BlockSpec((2, 1, tn), lambda j, i: (0, 0, j))],
        out_specs=pl.BlockSpec((tm, tn), lambda j, i: (i, j)),
        out_shape=jax.ShapeDtypeStruct((s, c), BF16),
        scratch_shapes=[pltpu.VMEM((tm + 2 * HALO, tn), F32)] * 2,
        compiler_params=_params(("parallel", "parallel")),
    )(up, up, up, cw, cb)


def _conv_gate_bwd(up, da, cw, cb, name):
    _, s, c = up.shape
    tm = min(256, s)
    tn = FF_SLAB
    te = tm + HALO

    def body(up_p, up_m, up_n, da_p, da_m, da_n, w_ref, b_ref, dup_ref, dw_ref, db_ref, eg, ev, ed, dug, duv):
        i = pl.program_id(1)

        @pl.when(i == 0)
        def _():
            dw_ref[...] = jnp.zeros_like(dw_ref)
            db_ref[...] = jnp.zeros_like(db_ref)

        _fill_ext(eg, up_p[0], up_m[0], up_n[0], i, tm, s)
        _fill_ext(ev, up_p[1], up_m[1], up_n[1], i, tm, s)
        _fill_ext(ed, da_p[...], da_m[...], da_n[...], i, tm, s)
        o = HALO // 2

        def du_strip(t, carry):
            r0 = pl.multiple_of(t * STRIP, STRIP)
            ug = _conv3(eg, o + r0, w_ref, b_ref, 0)
            uv = _conv3(ev, o + r0, w_ref, b_ref, 1)
            gl, th = _gelu(ug)
            dav = ed[pl.ds(pl.multiple_of(o + r0, 8), STRIP), :]
            dug[pl.ds(r0, STRIP), :] = dav * uv * _gelu_grad(ug, th)
            duv[pl.ds(r0, STRIP), :] = dav * gl
            return carry

        lax.fori_loop(0, te // STRIP, du_strip, 0)

        def back(du, e, t):
            def strip(k, acc):
                r0 = pl.multiple_of(k * STRIP, STRIP)
                dm1, c0, dp1 = _shifted(du, o + r0)
                dup_ref[t, pl.ds(r0, STRIP), :] = (w_ref[t, 0:1, :] * dp1 + w_ref[t, 1:2, :] * c0
                                                   + w_ref[t, 2:3, :] * dm1).astype(BF16)
                um1, u0, up1 = _shifted(e, HALO + r0)
                fold = lambda a: a[0:8] + a[8:16]
                return (acc[0] + fold(um1 * c0), acc[1] + fold(u0 * c0), acc[2] + fold(up1 * c0), acc[3] + fold(c0))

            zero = jnp.zeros((8, tn), F32)
            acc = lax.fori_loop(0, tm // STRIP, strip, (zero, zero, zero, zero))
            for k in range(3):
                dw_ref[t, k:k + 1, :] += jnp.sum(acc[k], axis=0, keepdims=True)
            db_ref[t] += jnp.sum(acc[3], axis=0, keepdims=True)

        back(dug, eg, 0)
        back(duv, ev, 1)

    wspec = pl.BlockSpec((2, 3, tn), lambda j, i: (0, 0, j))
    bspec = pl.BlockSpec((2, 1, tn), lambda j, i: (0, 0, j))
    return pl.pallas_call(
        body, name=name, grid=(c // tn, s // tm),
        in_specs=_halo_specs(s, tm, tn, 2) + _halo_specs(s, tm, tn, 0) + [wspec, bspec],
        out_specs=[pl.BlockSpec((2, tm, tn), lambda j, i: (0, i, j)), wspec, bspec],
        out_shape=[jax.ShapeDtypeStruct((2, s, c), BF16), jax.ShapeDtypeStruct((2, 3, c), F32),
                   jax.ShapeDtypeStruct((2, 1, c), F32)],
        scratch_shapes=[pltpu.VMEM((tm + 2 * HALO, tn), F32)] * 3 + [pltpu.VMEM((te, tn), F32)] * 2,
        compiler_params=_params(("parallel", "arbitrary")),
    )(up, up, up, da, da, da, cw, cb)


def _ffn_out(a, w_down, g_post, x1, target, name):
    s = x1.shape[0]
    tm = min(256, s)

    def body(a_ref, w_ref, g_ref, x1_ref, t_ref, dy3_ref, dx2_ref, loss_ref, dg_ref):
        i = pl.program_id(0)

        @pl.when(i == 0)
        def _():
            loss_ref[...] = jnp.zeros_like(loss_ref)
            dg_ref[...] = jnp.zeros_like(dg_ref)

        y3 = _dot(a_ref[...], w_ref[...], NN)
        g = g_ref[...]
        x2 = x1_ref[...] + y3 * _rstd(y3) * g
        diff = x2 - t_ref[...]
        loss_ref[...] += jnp.sum(jnp.sum(diff * diff, axis=1, keepdims=True), axis=0, keepdims=True)
        dx2 = diff * (1.0 / D_MODEL)
        dx2_ref[...] = dx2
        dy3, dg = _rms_bwd(dx2, y3, g)
        dy3_ref[...] = dy3.astype(BF16)
        dg_ref[...] += jnp.sum(dg, axis=0, keepdims=True)

    row = lambda n: pl.BlockSpec((tm, n), lambda i: (i, 0))
    full = lambda t: pl.BlockSpec(t.shape, lambda i: (0, 0))
    return pl.pallas_call(
        body, name=name, grid=(s // tm,),
        in_specs=[row(a.shape[1]), full(w_down), full(g_post), row(D_MODEL), row(D_MODEL)],
        out_specs=[row(D_MODEL), row(D_MODEL), pl.BlockSpec((8, 128), lambda i: (0, 0)), full(g_post)],
        out_shape=[jax.ShapeDtypeStruct((s, D_MODEL), BF16), jax.ShapeDtypeStruct((s, D_MODEL), F32),
                   jax.ShapeDtypeStruct((8, 128), F32), jax.ShapeDtypeStruct(g_post.shape, F32)],
        compiler_params=_params(("arbitrary",)),
    )(a, w_down, g_post, x1, target)


def _resnorm_bwd(dh2, x1, g_ffn_pre, dx2, y2, g_mix_post, name):
    s = x1.shape[0]
    tm = min(256, s)

    def body(dh_ref, x1_ref, gf_ref, dx2_ref, y2_ref, gp_ref, dx1_ref, dy2_ref, dgf_ref, dgp_ref):
        i = pl.program_id(0)

        @pl.when(i == 0)
        def _():
            dgf_ref[...] = jnp.zeros_like(dgf_ref)
            dgp_ref[...] = jnp.zeros_like(dgp_ref)

        dn, dgf = _rms_bwd(dh_ref[...], x1_ref[...], gf_ref[...])
        dx1 = dx2_ref[...] + dn
        dx1_ref[...] = dx1
        dgf_ref[...] += jnp.sum(dgf, axis=0, keepdims=True)
        dy2, dgp = _rms_bwd(dx1, y2_ref[...], gp_ref[...])
        dy2_ref[...] = dy2.astype(BF16)
        dgp_ref[...] += jnp.sum(dgp, axis=0, keepdims=True)

    row = pl.BlockSpec((tm, D_MODEL), lambda i: (i, 0))
    full = pl.BlockSpec((1, D_MODEL), lambda i: (0, 0))
    return pl.pallas_call(
        body, name=name, grid=(s // tm,),
        in_specs=[row, row, full, row, row, full],
        out_specs=[row, row, full, full],
        out_shape=[jax.ShapeDtypeStruct((s, D_MODEL), F32), jax.ShapeDtypeStruct((s, D_MODEL), BF16),
                   jax.ShapeDtypeStruct((1, D_MODEL), F32), jax.ShapeDtypeStruct((1, D_MODEL), F32)],
        compiler_params=_params(("arbitrary",)),
    )(dh2, x1, g_ffn_pre, dx2, y2, g_mix_post)


def _final_bwd(dh1, x, g_pre, dx1, name):
    s = x.shape[0]
    tm = min(256, s)

    def body(dh_ref, x_ref, g_ref, dx1_ref, dx_ref, dg_ref):
        @pl.when(pl.program_id(0) == 0)
        def _():
            dg_ref[...] = jnp.zeros_like(dg_ref)

        dn, dg = _rms_bwd(dh_ref[...], x_ref[...], g_ref[...])
        dx_ref[...] = dx1_ref[...] + dn
        dg_ref[...] += jnp.sum(dg, axis=0, keepdims=True)

    row = pl.BlockSpec((tm, D_MODEL), lambda i: (i, 0))
    full = pl.BlockSpec((1, D_MODEL), lambda i: (0, 0))
    return pl.pallas_call(
        body, name=name, grid=(s // tm,),
        in_specs=[row, row, full, row], out_specs=[row, full],
        out_shape=[jax.ShapeDtypeStruct((s, D_MODEL), F32), jax.ShapeDtypeStruct((1, D_MODEL), F32)],
        compiler_params=_params(("arbitrary",)),
    )(dh1, x, g_pre, dx1)


def _local_step(x, target, fw, rep, mixer_weights, early_grads):
    s = x.shape[0]
    tabs = _rope_tables(s)
    w_in, w_uq, w_ukv, w_o = (fw[n] for n in ("w_in", "w_uq", "w_ukv", "w_o"))
    tr = min(2048, s)
    tcon = min(2048, s)

    proj, h1, qkv4, qkv16 = _in_proj(x, rep["norm_mix_pre"], w_in, "in_proj")
    qkv = {1: proj, 4: qkv4, 16: qkv16}
    q_of = lambda r: (qkv[r], lambda c: 3 * c)
    k_of = lambda r: (qkv[r], lambda c: 3 * c + 1)
    v_of = lambda r: (qkv[r], lambda c: 3 * c + 2)
    own = lambda a: (a, lambda c: c)
    biases = [_band_bias(r) for _, r in DIL_CONFIGS]
    os_, lses = [], []
    for g, (_, r) in enumerate(DIL_CONFIGS):
        o, l = _band_call("fwd", r, [q_of(r)], [k_of(r), v_of(r)], biases[g], f"band_fwd_r{r}")
        os_.append(o)
        lses.append(l)
    ya, lse_a, lse4, lse16 = _band_combine(os_, lses, "band_combine")
    qcat, kcat, kvb, cqn, ckvn = _mla_prep(proj, rep["q_lat_norm"], rep["kv_lat_norm"], w_uq, w_ukv, tabs, "mla_prep")
    yb, qaug = _mla_fwd(qcat, kcat, kvb, "mla_fwd")
    yn, y2, x1 = _mix_out(ya, yb, rep["out_norm_a"], rep["out_norm_b"], w_o, rep["norm_mix_post"], x, "mix_out")
    mw = mixer_weights(x1)
    w_up, w_down, cw, cb = mw["w_up"], mw["w_down"], mw["conv_w"], mw["conv_b"]
    ff = w_down.shape[0]
    up, h2 = _norm_matmul(x1, rep["norm_ffn_pre"], w_up, None, "up_proj")
    act = _conv_gate(up, cw, cb, "conv_gate")
    dy3, dx2, loss_acc, dg_ffn_post = _ffn_out(act, w_down, rep["norm_ffn_post"], x1, target, "ffn_out")

    grads = {"norm_ffn_post": dg_ffn_post}
    dact = _matmul(dy3, w_down, "nt", BF16, tr, ff // 2, D_MODEL, "d_act")
    grads["w_down"] = _matmul(act, dy3, "tn", F32, ff // 2, D_MODEL, tcon, "dw_down")
    dup, grads["conv_w"], grads["conv_b"] = _conv_gate_bwd(up, dact, cw, cb, "conv_gate_bwd")
    half = N_DEV // 2
    dh2 = _matmul_core(
        dup, w_up, NT, (s // tr, 1, N_DEV),
        pl.BlockSpec((None, tr, FF_SLAB), lambda i, j, t: (t // half, i, t % half)),
        pl.BlockSpec((None, D_MODEL, FF_SLAB), lambda i, j, t: (t, 0, 0)),
        pl.BlockSpec((tr, D_MODEL), lambda i, j, t: (i, 0)),
        jax.ShapeDtypeStruct((s, D_MODEL), F32), (tr, D_MODEL), "d_h2")
    grads["w_up"] = _matmul_core(
        h2, dup, TN, (1, N_DEV, s // tcon),
        pl.BlockSpec((tcon, D_MODEL), lambda i, j, t: (t, 0)),
        pl.BlockSpec((None, tcon, FF_SLAB), lambda i, j, t: (j // half, t, j % half)),
        pl.BlockSpec((None, D_MODEL, FF_SLAB), lambda i, j, t: (j, 0, 0)),
        jax.ShapeDtypeStruct((N_DEV, D_MODEL, FF_SLAB), F32), (D_MODEL, FF_SLAB), "dw_up")
    dx1, dy2, grads["norm_ffn_pre"], grads["norm_mix_post"] = _resnorm_bwd(
        dh2, x1, rep["norm_ffn_pre"], dx2, y2, rep["norm_mix_post"], "resnorm_bwd")
    dyn = _matmul(dy2, w_o, "nt", F32, tr, D_MODEL, D_MODEL, "d_yn")
    grads["w_o"] = _matmul(yn, dy2, "tn", F32, D_MODEL, D_MODEL, tcon, "dw_o")
    token = early_grads(grads)
    dya, delta_a, doaug, grads["out_norm_a"], grads["out_norm_b"], dya4, dya16, delta4, delta16 = _outnorm_bwd(
        dyn, ya, yb, rep["out_norm_a"] + token, rep["out_norm_b"], _head_ones(), "outnorm_bwd")
    stats = {1: (dya, lse_a, delta_a), 4: (dya4, lse4, delta4), 16: (dya16, lse16, delta16)}
    parts = []
    for g, (_, r) in enumerate(DIL_CONFIGS):
        qside = [q_of(r)] + [own(a) for a in stats[r]]
        kside = [k_of(r), v_of(r)]
        (dq,) = _band_call("dq", r, qside, kside, biases[g], f"band_dq_r{r}")
        dk, dv = _band_call("dkv", r, kside, qside, biases[g], f"band_dkv_r{r}")
        parts.append((dq, dk, dv))
    dproj_a = _sum_cast(parts, "band_grad_sum")
    dqc, dkc, dvp = _mla_bwd(qaug, kcat, kvb, doaug, "mla_bwd")
    dproj_b, grads["w_uq"], grads["w_ukv"], grads["q_lat_norm"], grads["kv_lat_norm"] = _mla_prep_bwd(
        dqc, dkc, dvp, proj, cqn, ckvn, rep["q_lat_norm"], rep["kv_lat_norm"], w_uq, w_ukv, tabs, "mla_prep_bwd")
    dproj = jnp.concatenate([dproj_a, dproj_b], axis=1)
    dh1 = _matmul(dproj, w_in, "nt", F32, tr, D_MODEL, N_PROJ // 2, "d_h1")
    grads["w_in"] = _matmul(h1, dproj, "tn", F32, D_MODEL, N_PROJ // 2, tcon, "dw_in")
    grad_x, grads["norm_mix_pre"] = _final_bwd(dh1, x, rep["norm_mix_pre"], dx1, "final_bwd")
    loss = 0.5 / D_MODEL * loss_acc[0, 0]
    return loss, grad_x, grads


MESH = pl.DeviceIdType.MESH
HBM_SPEC = pl.BlockSpec(memory_space=pltpu.HBM)
SMALL_ROWS = 96
BUF_SHAPES = {"w_up": (D_MODEL, FF_SLAB), "w_in": (D_MODEL, 384), "w_down": (D_FF // N_DEV, D_MODEL),
              "w_o": (D_MODEL // N_DEV, D_MODEL), "w_uq": (Q_LORA, 128), "w_ukv": (KV_LORA, 128), "conv_w": (8, FF_SLAB)}
BUF_ORDER = tuple(BUF_SHAPES)
MIXING = ("w_in", "w_o", "w_uq", "w_ukv")
MIXER = ("w_up", "w_down", "conv_w")
EARLY_GRADS = ("w_up", "w_down", "w_o", "conv_w")
LATE_GRADS = ("w_in", "w_uq", "w_ukv")


def _all_gather(bufs, name):
    nb = len(bufs)

    def body(*refs):
        x_refs, out_refs = refs[:nb], refs[nb:2 * nb]
        send_sems, recv_sems, local_sems = refs[2 * nb:]
        x, y, c = lax.axis_index("x"), lax.axis_index("y"), lax.axis_index("c")
        me, sibling = (x, y, c), (x, y, 1 - c)
        chips = [(1 - x, y), (x, 1 - y), (1 - x, 1 - y)]

        def copy(b, k, block, to, own=False):
            px, py, pc = block
            slot = out_refs[b].at[4 * px + 2 * py + pc]
            return pltpu.make_async_remote_copy(
                src_ref=x_refs[b] if own else slot, dst_ref=slot,
                send_sem=send_sems.at[7 * b + k], recv_sem=recv_sems.at[7 * b + k], device_id=to, device_id_type=MESH)

        mine = [pltpu.make_async_copy(x_refs[b], out_refs[b].at[4 * x + 2 * y + c], local_sems.at[b]) for b in range(nb)]
        sends = []
        for b in range(nb):
            mine[b].start()
            first = [copy(b, 0, me, sibling, own=True)]
            first += [copy(b, 1 + j, me, (*chip, c), own=True) for j, chip in enumerate(chips)]
            for cp in first:
                cp.start()
            sends += first
        for j, chip in enumerate(chips):
            for b in range(nb):
                copy(b, 1 + j, (*chip, c), me).wait_recv()
                passed = copy(b, 4 + j, (*chip, c), sibling)
                passed.start()
                sends.append(passed)
        for b in range(nb):
            copy(b, 0, sibling, me).wait_recv()
            for j, chip in enumerate(chips):
                copy(b, 4 + j, (*chip, 1 - c), me).wait_recv()
        for cp in sends:
            cp.wait_send()
        for cp in mine:
            cp.wait()

    return pl.pallas_call(
        body, name=name,
        out_shape=[jax.ShapeDtypeStruct((N_DEV,) + p.shape, p.dtype) for p in bufs],
        in_specs=[HBM_SPEC] * nb, out_specs=[HBM_SPEC] * nb,
        scratch_shapes=[pltpu.SemaphoreType.DMA((7 * nb,)), pltpu.SemaphoreType.DMA((7 * nb,)),
                        pltpu.SemaphoreType.DMA((nb,))],
    )(*bufs)


def _grad_exchange(bigs, small, name):
    flips = [(fx, fy, fc) for fx in (0, 1) for fy in (0, 1) for fc in (0, 1)][1:]
    nb = len(bigs)

    def body(*refs):
        big_refs, small_ref = refs[:nb], refs[nb]
        rbig_refs, rsmall_ref = refs[nb + 1:2 * nb + 1], refs[2 * nb + 1]
        send_sems, recv_sems, local_sems = refs[2 * nb + 2:]
        x, y, c = lax.axis_index("x"), lax.axis_index("y"), lax.axis_index("c")
        my = 4 * x + 2 * y + c
        own = [pltpu.make_async_copy(big_refs[b].at[my], rbig_refs[b].at[my], local_sems.at[b]) for b in range(nb)]
        own.append(pltpu.make_async_copy(small_ref, rsmall_ref.at[my], local_sems.at[nb]))
        for cp in own:
            cp.start()
        copies = []
        for b in range(nb + 1):
            for k, (fx, fy, fc) in enumerate(flips):
                px = 1 - x if fx else x
                py = 1 - y if fy else y
                pc = 1 - c if fc else c
                src = small_ref if b == nb else big_refs[b].at[4 * px + 2 * py + pc]
                dst = rsmall_ref.at[my] if b == nb else rbig_refs[b].at[my]
                copies.append(pltpu.make_async_remote_copy(
                    src_ref=src, dst_ref=dst, send_sem=send_sems.at[7 * b + k], recv_sem=recv_sems.at[7 * b + k],
                    device_id=(px, py, pc), device_id_type=MESH))
        for cp in copies:
            cp.start()
        for cp in copies:
            cp.wait()
        for cp in own:
            cp.wait()

    nsem = 7 * (nb + 1)
    return pl.pallas_call(
        body, name=name,
        out_shape=[jax.ShapeDtypeStruct(b.shape, b.dtype) for b in bigs]
        + [jax.ShapeDtypeStruct((N_DEV,) + small.shape, small.dtype)],
        in_specs=[HBM_SPEC] * (nb + 1), out_specs=[HBM_SPEC] * (nb + 1),
        scratch_shapes=[pltpu.SemaphoreType.DMA((nsem,)), pltpu.SemaphoreType.DMA((nsem,)),
                        pltpu.SemaphoreType.DMA((nb + 1,))],
    )(*bigs, small)


SEM_SPEC = pl.BlockSpec(memory_space=pltpu.SEMAPHORE)
ANY_SPEC = pl.BlockSpec(memory_space=pl.ANY)
FLIPS = tuple((fx, fy, fc) for fx in (0, 1) for fy in (0, 1) for fc in (0, 1))[1:]


def _split_copies(src_refs, land_refs, send_sems, recv_sems, scatter):
    x, y, c = lax.axis_index("x"), lax.axis_index("y"), lax.axis_index("c")
    my = 4 * x + 2 * y + c
    copies = []
    for b, (src, land) in enumerate(zip(src_refs, land_refs)):
        for k, (fx, fy, fc) in enumerate(FLIPS):
            px = 1 - x if fx else x
            py = 1 - y if fy else y
            pc = 1 - c if fc else c
            copies.append(pltpu.make_async_remote_copy(
                src_ref=src.at[4 * px + 2 * py + pc] if scatter else src, dst_ref=land.at[my],
                send_sem=send_sems.at[7 * b + k], recv_sem=recv_sems.at[7 * b + k],
                device_id=(px, py, pc), device_id_type=MESH))
    return copies


def _exchange_start(srcs, scatter, name):
    nb = len(srcs)
    lands = [lax.empty(s.shape if scatter else (N_DEV,) + s.shape, s.dtype) for s in srcs]

    def body(*refs):
        src_refs, land_refs = refs[:nb], refs[nb:2 * nb]
        send_sems, recv_sems = refs[2 * nb], refs[2 * nb + 1]
        token = refs[-1]
        for cp in _split_copies(src_refs, land_refs, send_sems, recv_sems, scatter):
            cp.start()
        token[...] = jnp.zeros_like(token)

    hbm = lambda a: pltpu.HBM(a.shape, a.dtype)
    outs = pl.pallas_call(
        body, name=name,
        out_shape=(pltpu.SemaphoreType.DMA((7 * nb,)), pltpu.SemaphoreType.DMA((7 * nb,)),
                   *[hbm(a) for a in srcs], *[hbm(a) for a in lands], jax.ShapeDtypeStruct((8, 128), F32)),
        in_specs=[HBM_SPEC] * (2 * nb),
        out_specs=(SEM_SPEC, SEM_SPEC, *[HBM_SPEC] * (2 * nb), pl.BlockSpec(memory_space=pltpu.VMEM)),
        input_output_aliases={i: 2 + i for i in range(2 * nb)},
        compiler_params=pltpu.CompilerParams(has_side_effects=pltpu.SideEffectType.DATAFLOW_SIDE_EFFECTING),
    )(*[pltpu.with_memory_space_constraint(a, pltpu.HBM) for a in srcs],
      *[pltpu.with_memory_space_constraint(a, pltpu.HBM) for a in lands])
    return outs[0], outs[1], list(outs[2:2 + nb]), list(outs[2 + nb:2 + 2 * nb]), outs[-1]


def _exchange_wait(started, scatter, after, name):
    send_sems, recv_sems, srcs, lands, _ = started
    nb = len(srcs)

    def body(*refs):
        src_refs, land_refs = refs[:nb], refs[nb:2 * nb]
        for cp in _split_copies(src_refs, land_refs, refs[2 * nb], refs[2 * nb + 1], scatter):
            cp.wait_send()
            cp.wait_recv()

    hbm = lambda a: pltpu.HBM(a.shape, a.dtype)
    outs = pl.pallas_call(
        body, name=name,
        out_shape=(*[hbm(a) for a in srcs], *[hbm(a) for a in lands]),
        in_specs=[HBM_SPEC] * (2 * nb) + [SEM_SPEC, SEM_SPEC, ANY_SPEC],
        out_specs=tuple([HBM_SPEC] * (2 * nb)),
        input_output_aliases={i: i for i in range(2 * nb)},
        compiler_params=pltpu.CompilerParams(has_side_effects=pltpu.SideEffectType.DATAFLOW_SIDE_EFFECTING),
    )(*srcs, *lands, send_sems, recv_sems, after)
    return list(outs[:nb]), list(outs[nb:])


def _own_slot(land, own):
    my = 4 * lax.axis_index("x") + 2 * lax.axis_index("y") + lax.axis_index("c")
    return lax.dynamic_update_slice(land, own[None], (my,) + (0,) * own.ndim)


def _adamw(parts, w, m, v, name):
    rows, n = w.shape
    tm = rows if rows <= 384 else 256
    assert rows % tm == 0

    def body(p_ref, w_ref, m_ref, v_ref, g_ref, d_ref, m2_ref, v2_ref):
        g = p_ref[0, :, 0:n].astype(F32)
        for s in range(1, N_DEV):
            g = g + p_ref[s, :, 0:n].astype(F32)
        g_ref[...] = g
        m2 = ADAM_B1 * m_ref[...] + (1.0 - ADAM_B1) * g
        v2 = ADAM_B2 * v_ref[...] + (1.0 - ADAM_B2) * jnp.square(g)
        m2_ref[...] = m2
        v2_ref[...] = v2
        m_hat = m2 / (1.0 - ADAM_B1 ** ADAM_STEP)
        v_hat = v2 / (1.0 - ADAM_B2 ** ADAM_STEP)
        d_ref[...] = -ADAM_LR * (m_hat / (jnp.sqrt(v_hat) + ADAM_EPS) + ADAM_WD * w_ref[...])

    row = pl.BlockSpec((tm, n), lambda i: (i, 0))
    return pl.pallas_call(
        body, name=name, grid=(rows // tm,),
        in_specs=[pl.BlockSpec((N_DEV, tm, parts.shape[2]), lambda i: (0, i, 0)), row, row, row],
        out_specs=[row] * 4,
        out_shape=[jax.ShapeDtypeStruct((rows, n), F32)] * 4,
        compiler_params=_params(("parallel",)),
    )(parts, w, m, v)


def _pack(flat_parts, rows):
    flat = jnp.concatenate(flat_parts, axis=-1)
    pad = rows * LANES - flat.shape[-1]
    flat = jnp.pad(flat, [(0, 0)] * (flat.ndim - 1) + [(0, pad)])
    return flat.reshape(flat.shape[:-1] + (rows, LANES))


def _unpack(packed, shapes):
    flat = packed.reshape(packed.shape[:-2] + (-1,))
    out, off = {}, 0
    for name, shape in shapes.items():
        n = int(np.prod(shape))
        out[name] = flat[..., off:off + n].reshape(flat.shape[:-1] + tuple(shape))
        off += n
    return out


def _pad_to(a, shape):
    return jnp.pad(a, [(0, t - d) for d, t in zip(a.shape, shape)])


def _pad_w_in(w):
    k = w.shape[0]
    z = lambda n: jnp.zeros((k, n), w.dtype)
    return jnp.concatenate([w[:, :COL_KR], z(64), w[:, COL_KR:], z(32)], axis=1)


def _unpad_w_in(w):
    return jnp.concatenate([w[:, :COL_KR], w[:, COL_KR + 64:COL_KR + 96]], axis=1)


def _assemble_weights(g, conv_b):
    half = N_DEV // 2
    cols = lambda a: a.transpose(1, 0, 2).reshape(a.shape[1], N_DEV * a.shape[2])
    make = {
        "w_in": lambda: _pad_w_in(cols(g["w_in"][:, :, :D_IN // N_DEV])),
        "w_uq": lambda: cols(g["w_uq"]),
        "w_ukv": lambda: cols(g["w_ukv"]),
        "w_o": lambda: g["w_o"].reshape(D_MODEL, D_MODEL),
        "w_up": lambda: g["w_up"],
        "w_down": lambda: _pad_to(g["w_down"].reshape(half, FF_SHARD, D_MODEL),
                                  (half, FF_SLAB, D_MODEL)).reshape(half * FF_SLAB, D_MODEL),
        "conv_w": lambda: g["conv_w"][:, :3].reshape(2, half, 3, FF_SLAB).transpose(0, 2, 1, 3).reshape(2, 3, half * FF_SLAB),
    }
    fw = {n: make[n]() for n in g}
    if conv_b is not None:
        fw["conv_b"] = _pad_to(conv_b.reshape(2, 1, half, FF_SHARD), (2, 1, half, FF_SLAB)).reshape(2, 1, half * FF_SLAB)
    return fw


def _grad_bufs(grads, names):
    half = N_DEV // 2
    slabs = lambda a: a.reshape(a.shape[0], N_DEV, a.shape[1] // N_DEV).transpose(1, 0, 2)
    make = {
        "w_in": lambda: _pad_to(slabs(_unpad_w_in(grads["w_in"])), (N_DEV,) + BUF_SHAPES["w_in"]),
        "w_uq": lambda: slabs(grads["w_uq"]),
        "w_ukv": lambda: slabs(grads["w_ukv"]),
        "w_o": lambda: grads["w_o"].reshape((N_DEV,) + BUF_SHAPES["w_o"]),
        "w_up": lambda: grads["w_up"],
        "w_down": lambda: grads["w_down"].reshape(half, FF_SLAB, D_MODEL)[:, :FF_SHARD].reshape((N_DEV,) + BUF_SHAPES["w_down"]),
        "conv_w": lambda: _pad_to(grads["conv_w"].reshape(2, 3, half, FF_SLAB).transpose(0, 2, 1, 3).reshape(N_DEV, 3, FF_SLAB),
                                  (N_DEV,) + BUF_SHAPES["conv_w"]),
    }
    return [make[n]() if n == "conv_w" else make[n]().astype(BF16) for n in names]


def kernel(x, norm_mix_pre, w_in, q_lat_norm, w_uq, kv_lat_norm, w_ukv, out_norm_a, out_norm_b, w_o, norm_mix_post, norm_ffn_pre, w_up, conv_w, conv_b, w_down, norm_ffn_post, loss_target, m_norm_mix_pre, m_w_in, m_q_lat_norm, m_w_uq, m_kv_lat_norm, m_w_ukv, m_out_norm_a, m_out_norm_b, m_w_o, m_norm_mix_post, m_norm_ffn_pre, m_w_up, m_conv_w, m_conv_b, m_w_down, m_norm_ffn_post, v_norm_mix_pre, v_w_in, v_q_lat_norm, v_w_uq, v_kv_lat_norm, v_w_ukv, v_out_norm_a, v_out_norm_b, v_w_o, v_norm_mix_post, v_norm_ffn_pre, v_w_up, v_conv_w, v_conv_b, v_w_down, v_norm_ffn_post):
    given = dict(locals())
    w = {n: given[n][0] for n in WEIGHTS}
    m = {n: given["m_" + n][0] for n in WEIGHTS}
    v = {n: given["v_" + n][0] for n in WEIGHTS}
    rep_shapes = {n: w[n].shape for n in REPLICATED}

    buf = lambda n: _pad_to(w[n] if n == "conv_w" else w[n].astype(BF16), BUF_SHAPES[n])
    first = dict(zip(MIXING, _all_gather([buf(n) for n in MIXING], "weight_all_gather")))
    fw = _assemble_weights(first, None)
    tie = first["w_o"][0, 0, 0].astype(F32) * 0.0
    late_bufs = [buf(n) + tie.astype(w[n].dtype if n == "conv_w" else BF16) for n in MIXER]
    mixer_started = _exchange_start(late_bufs, False, "mixer_weights_start")
    rep = {n: given[n] for n in REPLICATED}
    rep["norm_mix_pre"] = rep["norm_mix_pre"] + mixer_started[4][0, 0]

    def mixer_weights(after):
        srcs, lands = _exchange_wait(mixer_started, False, after, "mixer_weights_wait")
        got = {n: _own_slot(land, own) for n, land, own in zip(MIXER, lands, srcs)}
        return _assemble_weights(got, conv_b)

    early = {}

    def early_grads(grads):
        early["started"] = _exchange_start(_grad_bufs(grads, EARLY_GRADS), True, "early_grads_start")
        return early["started"][4][0, 0]

    loss_local, grad_x, grads = _local_step(x[0], loss_target[0], fw, rep, mixer_weights, early_grads)

    late = _grad_bufs(grads, LATE_GRADS)
    grads["conv_b"] = grads["conv_b"].reshape(N_DEV, FF_SLAB)[:, :FF_SHARD]
    small = _pack([grads[n].reshape(-1) for n in REPLICATED] + [loss_local.reshape(1)], SMALL_ROWS)
    received_late = _grad_exchange(late, small, "grad_exchange")
    srcs, lands = _exchange_wait(early["started"], True, grad_x, "early_grads_wait")
    my = 4 * lax.axis_index("x") + 2 * lax.axis_index("y") + lax.axis_index("c")
    received = {n: _own_slot(land, lax.dynamic_index_in_dim(src, my, 0, keepdims=False))
                for n, land, src in zip(EARLY_GRADS, lands, srcs)}
    received.update(zip(LATE_GRADS, received_late[:-1]))
    results = [{}, {}, {}, {}]
    for n in BUF_ORDER:
        parts = received[n]
        if n == "conv_w":
            args = [_pad_to(t[n], BUF_SHAPES[n]) for t in (w, m, v)]
        else:
            args = [w[n], m[n], v[n]]
        outs = _adamw(parts, *args, f"adamw_{n}")
        for t in range(4):
            results[t][n] = outs[t][:w[n].shape[0], :w[n].shape[1]] if n == "conv_w" else outs[t]
    pk = lambda d: _pack([d[n].reshape(-1) for n in REPLICATED] + [jnp.zeros((1,), F32)], SMALL_ROWS)
    small_out = _adamw(received_late[-1], pk(w), pk(m), pk(v), "adamw_replicated")
    rep_shapes["loss"] = (1,)
    for t in range(4):
        results[t].update(_unpack(small_out[t], rep_shapes))

    loss = results[0]["loss"][0]
    outs = [loss, grad_x[None]]
    for res in results:
        outs += [res[n][None] for n in WEIGHTS]
    return tuple(outs)
```

```python
import functools
import math

import numpy as np
import jax
import jax.numpy as jnp
from jax import lax
from jax.experimental import pallas as pl
from jax.experimental.pallas import tpu as pltpu

F32 = jnp.float32
BF16 = jnp.bfloat16

D_MODEL = 1024
N_DEV = 8
WIDTH_A = 512
N_HEADS = 8
Q_LORA = 384
KV_LORA = 256
QK_ROPE = 32
QK_NOPE = 64
D_FF = 2816
FF_SHARD = 2 * D_FF // N_DEV
FF_SLAB = 768
DIL_CONFIGS = ((128, 1), (512, 4), (2048, 16))
BAND_HALF = 64
ROPE_BASE = 10000.0
EPS = 1e-6
NEG = -1e30
MLA_SCALE = (QK_NOPE + QK_ROPE) ** -0.5
SCALE_A = 0.125
LOG2E = 1.0 / math.log(2.0)
LN2 = math.log(2.0)
QSCALE_A = SCALE_A * LOG2E

COL_CQ = 3 * WIDTH_A
COL_CKV = COL_CQ + Q_LORA
COL_KR = COL_CKV + KV_LORA
N_PROJ = COL_KR + 128
N_LAT = N_PROJ - COL_CQ
D_IN = COL_KR + QK_ROPE

ADAM_LR = 0.001
ADAM_B1 = 0.9
ADAM_B2 = 0.999
ADAM_EPS = 1e-08
ADAM_WD = 0.01
ADAM_STEP = 10

LANES = 128
VMEM_LIMIT = 56 * 1024 * 1024

SHARDED = ("w_in", "w_uq", "w_ukv", "w_o", "w_up", "conv_w", "w_down")
REPLICATED = ("norm_mix_pre", "q_lat_norm", "kv_lat_norm", "out_norm_a", "out_norm_b", "norm_mix_post",
              "norm_ffn_pre", "conv_b", "norm_ffn_post")
WEIGHTS = ("norm_mix_pre", "w_in", "q_lat_norm", "w_uq", "kv_lat_norm", "w_ukv", "out_norm_a", "out_norm_b", "w_o",
           "norm_mix_post", "norm_ffn_pre", "w_up", "conv_w", "conv_b", "w_down", "norm_ffn_post")


def _params(sem=None):
    return pltpu.CompilerParams(dimension_semantics=sem, vmem_limit_bytes=VMEM_LIMIT)


def _dot(a, b, dims):
    return lax.dot_general(a, b, (dims, ((), ())), preferred_element_type=F32)


NN = ((1,), (0,))
NT = ((1,), (1,))
TN = ((0,), (0,))


def _rstd(x):
    return lax.rsqrt(jnp.mean(x * x, axis=-1, keepdims=True) + EPS)


def _rms_bwd(dy, x, g):
    r = _rstd(x)
    z = x * r
    gy = dy * g
    dx = r * (gy - z * jnp.mean(gy * z, axis=-1, keepdims=True))
    return dx, dy * z


def _split_hi_lo(v):
    hi = v.astype(BF16)
    lo = (v - hi.astype(F32)).astype(BF16)
    return hi, lo


def _matmul(a, b, mode, out_dtype, tm, tn, tk, name):
    if mode == "nn":
        (m, k), n = a.shape, b.shape[1]
        a_spec = pl.BlockSpec((tm, tk), lambda i, j, s: (i, s))
        b_spec = pl.BlockSpec((tk, tn), lambda i, j, s: (s, j))
        dims = NN
    elif mode == "nt":
        (m, k), n = a.shape, b.shape[0]
        a_spec = pl.BlockSpec((tm, tk), lambda i, j, s: (i, s))
        b_spec = pl.BlockSpec((tn, tk), lambda i, j, s: (j, s))
        dims = NT
    else:
        (k, m), n = a.shape, b.shape[1]
        a_spec = pl.BlockSpec((tk, tm), lambda i, j, s: (s, i))
        b_spec = pl.BlockSpec((tk, tn), lambda i, j, s: (s, j))
        dims = TN
    assert m % tm == 0 and n % tn == 0 and k % tk == 0, (name, m, n, k, tm, tn, tk)
    return _matmul_core(a, b, dims, (m // tm, n // tn, k // tk), a_spec, b_spec,
                        pl.BlockSpec((tm, tn), lambda i, j, s: (i, j)), jax.ShapeDtypeStruct((m, n), out_dtype),
                        (tm, tn), name)


def _matmul_core(a, b, dims, grid, a_spec, b_spec, o_spec, out_sds, acc_shape, name):
    nk = grid[2]

    def body(a_ref, b_ref, o_ref, acc_ref):
        s = pl.program_id(2)

        @pl.when(s == 0)
        def _():
            acc_ref[...] = jnp.zeros_like(acc_ref)

        acc_ref[...] += _dot(a_ref[...].astype(BF16), b_ref[...].astype(BF16), dims)

        @pl.when(s == nk - 1)
        def _():
            o_ref[...] = acc_ref[...].astype(out_sds.dtype)

    return pl.pallas_call(
        body, name=name, grid=grid, in_specs=[a_spec, b_spec], out_specs=o_spec, out_shape=out_sds,
        scratch_shapes=[pltpu.VMEM(acc_shape, F32)],
        compiler_params=_params(("parallel", "parallel", "arbitrary")),
    )(a, b)


def _norm_matmul(x, g, w, tn, name):
    s, k = x.shape
    tm = min(2048, s)
    if w.ndim == 3:
        nj, _, tn = w.shape
        half = nj // 2
        w_spec = pl.BlockSpec((None, k, tn), lambda i, j: (j, 0, 0))
        o_spec = pl.BlockSpec((None, tm, tn), lambda i, j: (j // half, i, j % half))
        o_sds = jax.ShapeDtypeStruct((2, s, half * tn), BF16)
    else:
        n = w.shape[1]
        assert n % tn == 0
        nj = n // tn
        w_spec = pl.BlockSpec((k, tn), lambda i, j: (0, j))
        o_spec = pl.BlockSpec((tm, tn), lambda i, j: (i, j))
        o_sds = jax.ShapeDtypeStruct((s, n), BF16)

    def body(x_ref, g_ref, w_ref, o_ref, h_ref):
        @pl.when(pl.program_id(1) == 0)
        def _():
            xv = x_ref[...]
            h_ref[...] = (xv * _rstd(xv) * g_ref[...]).astype(BF16)

        o_ref[...] = _dot(h_ref[...], w_ref[...], NN).astype(BF16)

    return pl.pallas_call(
        body, name=name, grid=(s // tm, nj),
        in_specs=[pl.BlockSpec((tm, k), lambda i, j: (i, 0)),
                  pl.BlockSpec((1, k), lambda i, j: (0, 0)),
                  w_spec],
        out_specs=[o_spec, pl.BlockSpec((tm, k), lambda i, j: (i, 0))],
        out_shape=[o_sds, jax.ShapeDtypeStruct((s, k), BF16)],
        compiler_params=_params(("parallel", "arbitrary")),
    )(x, g, w)


def _band_bias(r):
    off = np.arange(256)[None, :] - BAND_HALF - np.arange(128)[:, None]
    slopes = np.exp2(-8.0 * np.arange(1, N_HEADS + 1, dtype=np.float32) / N_HEADS).astype(np.float32)
    dist = (np.abs(off) * r).astype(np.float32)
    bias = -slopes[:, None, None] * dist[None]
    bias = np.where((np.abs(off) <= BAND_HALF)[None], bias * np.float32(LOG2E), np.float32(NEG))
    return jnp.asarray(bias, F32)


def _band_call(mode, r, center, window, bias, name):
    seq = center[0][0].shape[0]
    tq = min(512, seq)
    nsub = tq // 128
    hb = tq // BAND_HALF
    nh = seq // BAND_HALF
    nc, nw = len(center), len(window)
    n_out = {"fwd": 2, "dq": 1, "dkv": 2}[mode]

    def specs(col):
        return (pl.BlockSpec((BAND_HALF, WIDTH_A), lambda c, i: (jnp.maximum(i * hb - 1, 0), col(c))),
                pl.BlockSpec((tq, WIDTH_A), lambda c, i: (i, col(c))),
                pl.BlockSpec((BAND_HALF, WIDTH_A), lambda c, i: (jnp.minimum((i + 1) * hb, nh - 1), col(c))))

    cspec = pl.BlockSpec((tq, WIDTH_A), lambda c, i: (i, c))
    in_specs = [specs(col)[1] for _, col in center]
    operands = [a for a, _ in center]
    for a, col in window:
        in_specs += list(specs(col))
        operands += [a, a, a]
    in_specs.append(pl.BlockSpec((N_HEADS, 128, 256), lambda c, i: (0, 0, 0)))
    operands.append(bias)
    window = [a for a, _ in window]

    def aug_stat(base, stat_sw, lane, act, e0):
        hi, lo = _split_hi_lo(stat_sw)
        return jnp.where(act, base, jnp.where(lane == e0, -hi, jnp.where(lane == e0 + 1, -lo, jnp.zeros_like(hi))))

    def aug_ones(base, lane, e0):
        return jnp.where((lane == e0) | (lane == e0 + 1), jnp.ones_like(base), base)

    def body(*refs):
        c_refs = refs[:nc]
        w_refs = refs[nc:nc + 3 * nw]
        bias_ref = refs[nc + 3 * nw]
        o_refs = refs[nc + 3 * nw + 1:nc + 3 * nw + 1 + n_out]
        wins = refs[nc + 3 * nw + 1 + n_out:]
        i = pl.program_id(1)
        for t in range(nw):
            wins[t][0:BAND_HALF, :] = w_refs[3 * t][...]
            wins[t][BAND_HALF:BAND_HALF + tq, :] = w_refs[3 * t + 1][...]
            wins[t][BAND_HALF + tq:BAND_HALF + tq + BAND_HALF, :] = w_refs[3 * t + 2][...]

        def sub(j, carry):
            r0 = pl.multiple_of(j * 128, 128)
            wpos = i * tq + j * 128 - BAND_HALF + lax.broadcasted_iota(jnp.int32, (128, 256), 1)
            valid = (wpos >= 0) & (wpos < seq)
            lane_c = lax.broadcasted_iota(jnp.int32, (128, 128), 1)
            lane_w = lax.broadcasted_iota(jnp.int32, (256, 128), 1)
            heads = [(p, a) for p in range(4) for a in range(2)]
            first, last_ops = [], []
            for p, a in heads:
                cols = slice(p * 128, (p + 1) * 128)
                cs = [c[pl.ds(r0, 128), cols] for c in c_refs]
                ws = [w[pl.ds(r0, 256), cols] for w in wins]
                e0 = 64 if a == 0 else 0
                act_c = (lane_c < 64) if a == 0 else (lane_c >= 64)
                act_w = (lane_w < 64) if a == 0 else (lane_w >= 64)
                bias_a = bias_ref[2 * p + a]
                if mode == "fwd":
                    qa = jnp.where(act_c, cs[0] * QSCALE_A, jnp.zeros_like(cs[0]))
                    first.append((_dot(qa, ws[0], NT) + bias_a, None))
                    last_ops.append((ws[1],))
                elif mode == "dq":
                    q2, dy2, l2, d2 = cs
                    k2, v2 = ws
                    q_aug = aug_stat(q2 * QSCALE_A, pltpu.roll(l2, 64, 1), lane_c, act_c, e0)
                    dy_aug = aug_stat(dy2, pltpu.roll(d2, 64, 1), lane_c, act_c, e0)
                    first.append((_dot(q_aug, aug_ones(k2, lane_w, e0), NT) + bias_a,
                                  _dot(dy_aug, aug_ones(v2, lane_w, e0), NT)))
                    last_ops.append((k2,))
                else:
                    k2, v2 = cs
                    q2, dy2, l2, d2 = ws
                    q_aug = aug_stat(q2 * QSCALE_A, pltpu.roll(l2, 64, 1), lane_w, act_w, e0)
                    dy_aug = aug_stat(dy2, pltpu.roll(d2, 64, 1), lane_w, act_w, e0)
                    first.append((_dot(aug_ones(k2, lane_c, e0), q_aug, NT) + bias_a,
                                  _dot(aug_ones(v2, lane_c, e0), dy_aug, NT)))
                    last_ops.append((q_aug, dy_aug))
            mid = []
            for sc, dp in first:
                sc = jnp.where(valid, sc, NEG)
                if mode == "fwd":
                    m = jnp.max(sc, axis=-1, keepdims=True)
                    e = jnp.exp2(sc - m)
                    l = jnp.sum(e, axis=-1, keepdims=True)
                    mid.append((e.astype(BF16), l, m + jnp.log(l) * LOG2E))
                else:
                    pr = jnp.exp2(sc)
                    mid.append((pr.astype(BF16), (pr * dp).astype(BF16)))
            res = []
            for md, ops in zip(mid, last_ops):
                if mode == "fwd":
                    res.append((_dot(md[0], ops[0], NN) / md[1], jnp.broadcast_to(md[2], (128, 128))))
                elif mode == "dq":
                    res.append((_dot(md[1], ops[0], NN) * SCALE_A,))
                else:
                    res.append((_dot(md[1], ops[0], NN) * LN2, _dot(md[0], ops[1], NN)))
            for t in range(n_out):
                pairs = [jnp.where(lane_c < 64, res[2 * p][t], res[2 * p + 1][t]) for p in range(4)]
                o_refs[t][pl.ds(r0, 128), :] = jnp.concatenate(pairs, axis=1)
            return carry

        lax.fori_loop(0, nsub, sub, 0)

    outs = pl.pallas_call(
        body, name=name, grid=(r, seq // tq),
        in_specs=in_specs,
        out_specs=[cspec] * n_out,
        out_shape=[jax.ShapeDtypeStruct((seq, r * WIDTH_A), F32)] * n_out,
        scratch_shapes=[pltpu.VMEM((tq + 2 * BAND_HALF, WIDTH_A), w.dtype) for w in window],
        compiler_params=_params(("parallel", "parallel")),
    )(*operands)
    return outs


def _slab_scratch(tm, w):
    return pltpu.VMEM((w // 128, tm, 128), F32)


def _put(scr, val):
    for j in range(scr.shape[0]):
        scr[j] = val[:, j * 128:(j + 1) * 128].astype(F32)


def _get(scr):
    return jnp.concatenate([scr[j] for j in range(scr.shape[0])], axis=1)


def _dilate_store(dst_ref, scr, r):
    nb, tm, _ = scr.shape
    w = nb * 128
    for c in range(r):
        for j in range(nb):
            dst_ref[:, c * w + j * 128:c * w + (j + 1) * 128] = scr[j, pl.ds(c, tm // r, stride=r), :].astype(dst_ref.dtype)


def _undilate(scr, src_ref, r):
    nb, tm, _ = scr.shape
    w = nb * 128
    for c in range(r):
        for j in range(nb):
            scr[j, pl.ds(c, tm // r, stride=r), :] = src_ref[:, c * w + j * 128:c * w + (j + 1) * 128].astype(F32)


def _dil_spec(tm, r, w):
    return pl.BlockSpec((tm // r, r * w), lambda i: (i, 0))


def _dil_shape(s, r, w, dtype):
    return jax.ShapeDtypeStruct((s // r, r * w), dtype)


def _in_proj(x, g, w, name):
    s, k = x.shape
    n = w.shape[1]
    tm = min(512, s)
    qkv = 3 * WIDTH_A

    def body(x_ref, g_ref, w_ref, o_ref, h_ref, d4_ref, d16_ref, scr):
        xv = x_ref[...]
        h = (xv * _rstd(xv) * g_ref[...]).astype(BF16)
        h_ref[...] = h
        acc = _dot(h, w_ref[...], NN)
        o_ref[...] = acc.astype(BF16)
        _put(scr, acc[:, 0:qkv])
        _dilate_store(d4_ref, scr, 4)
        _dilate_store(d16_ref, scr, 16)

    row = lambda c: pl.BlockSpec((tm, c), lambda i: (i, 0))
    return pl.pallas_call(
        body, name=name, grid=(s // tm,),
        in_specs=[row(k), pl.BlockSpec((1, k), lambda i: (0, 0)), pl.BlockSpec((k, n), lambda i: (0, 0))],
        out_specs=[row(n), row(k), _dil_spec(tm, 4, qkv), _dil_spec(tm, 16, qkv)],
        out_shape=[jax.ShapeDtypeStruct((s, n), BF16), jax.ShapeDtypeStruct((s, k), BF16),
                   _dil_shape(s, 4, qkv, BF16), _dil_shape(s, 16, qkv, BF16)],
        scratch_shapes=[_slab_scratch(tm, qkv)],
        compiler_params=_params(("parallel",)),
    )(x, g, w)


def _band_combine(os_, lses, name):
    s = os_[0].shape[0]
    tm = min(512, s)

    def body(o1, o4, o16, l1, l4, l16, ya_ref, lse_ref, lse4_ref, lse16_ref, so4, sl4, so16, sl16):
        _undilate(so4, o4, 4)
        _undilate(sl4, l4, 4)
        _undilate(so16, o16, 16)
        _undilate(sl16, l16, 16)
        a0, a1, a2 = l1[...], _get(sl4), _get(sl16)
        m = jnp.maximum(jnp.maximum(a0, a1), a2)
        e0, e1, e2 = jnp.exp2(a0 - m), jnp.exp2(a1 - m), jnp.exp2(a2 - m)
        den = e0 + e1 + e2
        ya_ref[...] = (e0 * o1[...] + e1 * _get(so4) + e2 * _get(so16)) / den
        lse = m + jnp.log(den) * LOG2E
        lse_ref[...] = lse
        _put(sl4, lse)
        _dilate_store(lse4_ref, sl4, 4)
        _dilate_store(lse16_ref, sl4, 16)

    nat = pl.BlockSpec((tm, WIDTH_A), lambda i: (i, 0))
    d4, d16 = _dil_spec(tm, 4, WIDTH_A), _dil_spec(tm, 16, WIDTH_A)
    return pl.pallas_call(
        body, name=name, grid=(s // tm,), in_specs=[nat, d4, d16] * 2, out_specs=[nat, nat, d4, d16],
        out_shape=[jax.ShapeDtypeStruct((s, WIDTH_A), F32)] * 2
        + [_dil_shape(s, 4, WIDTH_A, F32), _dil_shape(s, 16, WIDTH_A, F32)],
        scratch_shapes=[_slab_scratch(tm, WIDTH_A)] * 4,
        compiler_params=_params(("parallel",)),
    )(*os_, *lses)


def _rope_tables(s):
    pos = jnp.arange(s, dtype=F32)
    inv_freq = jnp.exp(-math.log(ROPE_BASE) * jnp.arange(0, QK_ROPE, 2, dtype=F32) / QK_ROPE)
    ang = pos[:, None] * inv_freq[None, :]
    cos, sin = jnp.cos(ang), jnp.sin(ang)
    one = jnp.ones((s, 64), F32)
    zero16 = jnp.zeros((s, 16), F32)
    c = jnp.concatenate([one, cos, cos, jnp.ones((s, 32), F32)], axis=1)
    sa = jnp.concatenate([jnp.zeros((s, 64), F32), -sin, zero16, jnp.zeros((s, 32), F32)], axis=1)
    sb = jnp.concatenate([jnp.zeros((s, 64), F32), zero16, sin, jnp.zeros((s, 32), F32)], axis=1)
    return c, sa, sb


def _rope_fwd(x, c, sa, sb):
    return x * c + pltpu.roll(x, 112, 1) * sa + pltpu.roll(x, 16, 1) * sb


def _rope_bwd(dy, c, sa, sb):
    return dy * c + pltpu.roll(dy * sa, 16, 1) + pltpu.roll(dy * sb, 112, 1)


def _mla_prep(proj, g_q, g_kv, w_uq, w_ukv, tabs, name):
    s = proj.shape[0]
    tm = min(512, s)
    width = N_HEADS * 128

    def body(lat_ref, gq_ref, gkv_ref, wq_ref, wkv_ref, c_ref, sa_ref, sb_ref,
             q_ref, k_ref, kv_ref, cqn_ref, ckvn_ref):
        c, sa, sb = c_ref[...], sa_ref[...], sb_ref[...]
        cq = lat_ref[:, 0:Q_LORA].astype(F32)
        cqn = (cq * _rstd(cq) * gq_ref[...]).astype(BF16)
        cqn_ref[...] = cqn
        q = _dot(cqn, wq_ref[...], NN)
        ckv = lat_ref[:, Q_LORA:Q_LORA + KV_LORA].astype(F32)
        ckvn = (ckv * _rstd(ckv) * gkv_ref[...]).astype(BF16)
        ckvn_ref[...] = ckvn
        kv = _dot(ckvn, wkv_ref[...], NN)
        lane = lax.broadcasted_iota(jnp.int32, (tm, 128), 1)
        krr = _rope_fwd(lat_ref[:, Q_LORA + KV_LORA:].astype(F32), c, sa, sb)
        krr = jnp.where((lane == 96) | (lane == 97), 1.0, krr)
        ones01 = jnp.where(lane < 2, 1.0, 0.0)
        for h in range(N_HEADS):
            cols = slice(h * 128, (h + 1) * 128)
            q_ref[:, cols] = (_rope_fwd(q[:, cols], c, sa, sb) * (MLA_SCALE * LOG2E)).astype(BF16)
            k_ref[:, cols] = jnp.where(lane < 64, kv[:, cols], krr).astype(BF16)
            kv_ref[:, cols] = jnp.where(lane < 64, ones01, kv[:, cols]).astype(BF16)

    row = lambda n: pl.BlockSpec((tm, n), lambda i: (i, 0))
    full = lambda a: pl.BlockSpec(a.shape, lambda i: (0, 0))
    tab = pl.BlockSpec((tm, 128), lambda i: (i, 0))
    return pl.pallas_call(
        body, name=name, grid=(s // tm,),
        in_specs=[pl.BlockSpec((tm, N_LAT), lambda i: (i, COL_CQ // N_LAT)),
                  full(g_q), full(g_kv), full(w_uq), full(w_ukv), tab, tab, tab],
        out_specs=[row(width), row(width), row(width), row(Q_LORA), row(KV_LORA)],
        out_shape=[jax.ShapeDtypeStruct((s, width), BF16)] * 3
        + [jax.ShapeDtypeStruct((s, Q_LORA), BF16), jax.ShapeDtypeStruct((s, KV_LORA), BF16)],
        compiler_params=_params(("parallel",)),
    )(proj, g_q, g_kv, w_uq, w_ukv, *tabs)


def _mla_fwd(qcat, kcat, kvb, name):
    s = qcat.shape[0]
    tq = min(1024, s)
    tk = min(1024, s)
    nkc = s // tk

    def body(q_ref, k_ref, v_ref, yb_ref, qaug_ref, m_ref, acc_ref):
        lane = lax.broadcasted_iota(jnp.int32, (tq, 128), 1)
        m_ref[...] = jnp.full((2, tq, 128), NEG, F32)
        acc_ref[...] = jnp.zeros((2, tq, 128), F32)

        def chunk(cidx, carry):
            k0 = pl.multiple_of(cidx * tk, tk)
            cols = [slice(a * 128, (a + 1) * 128) for a in range(2)]
            scs = [_dot(q_ref[:, c], k_ref[pl.ds(k0, tk), c], NT) for c in cols]
            prs, alphas = [], []
            for a, sc in enumerate(scs):
                m_prev = m_ref[a]
                m_new = jnp.maximum(m_prev, jnp.max(sc, axis=-1, keepdims=True))
                alphas.append(jnp.exp2(m_prev - m_new))
                prs.append(jnp.exp2(sc - jnp.tile(m_new, (1, tk // 128))).astype(BF16))
                m_ref[a] = m_new
            for a, c in enumerate(cols):
                acc_ref[a] = alphas[a] * acc_ref[a] + _dot(prs[a], v_ref[pl.ds(k0, tk), c], NN)
            return carry

        lax.fori_loop(0, nkc, chunk, 0)
        outs = []
        for a in range(2):
            cols = slice(a * 128, (a + 1) * 128)
            acc = acc_ref[a]
            l = acc[:, 0:1]
            outs.append(acc / l)
            hi, lo = _split_hi_lo(m_ref[a] + jnp.log(l) * LOG2E)
            qaug_ref[:, cols] = jnp.where(lane == 96, -hi, jnp.where(lane == 97, -lo, q_ref[:, cols]))
        yb_ref[...] = jnp.where(lane < 64, pltpu.roll(outs[0], 64, 1), outs[1])

    return pl.pallas_call(
        body, name=name, grid=(4, s // tq),
        in_specs=[pl.BlockSpec((tq, 256), lambda p, i: (i, p)),
                  pl.BlockSpec((s, 256), lambda p, i: (0, p)),
                  pl.BlockSpec((s, 256), lambda p, i: (0, p))],
        out_specs=[pl.BlockSpec((tq, 128), lambda p, i: (i, p)),
                   pl.BlockSpec((tq, 256), lambda p, i: (i, p))],
        out_shape=[jax.ShapeDtypeStruct((s, WIDTH_A), F32), jax.ShapeDtypeStruct((s, N_HEADS * 128), BF16)],
        scratch_shapes=[pltpu.VMEM((2, tq, 128), F32)] * 2,
        compiler_params=_params(("parallel", "parallel")),
    )(qcat, kcat, kvb)


def _mla_bwd(qaug, kcat, kvb, doaug, name):
    s = qaug.shape[0]
    tq = min(1024, s)
    tk = min(512, s)
    nqc = s // tq
    width = N_HEADS * 128

    def body(q_ref, do_ref, k_ref, v_ref, dq_ref, dk_acc, dv_acc):
        j = pl.program_id(1)

        @pl.when(j == 0)
        def _():
            dq_ref[...] = jnp.zeros_like(dq_ref)

        dk_acc[...] = jnp.zeros_like(dk_acc)
        dv_acc[...] = jnp.zeros_like(dv_acc)

        def chunk(cidx, carry):
            q0 = pl.multiple_of(cidx * tq, tq)
            cols = [slice(a * 128, (a + 1) * 128) for a in range(2)]
            qs = [q_ref[pl.ds(q0, tq), c] for c in cols]
            dos = [do_ref[pl.ds(q0, tq), c] for c in cols]
            kbs = [k_ref[:, c] for c in cols]
            sts = [_dot(kbs[a], qs[a], NT) for a in range(2)]
            dps = [_dot(v_ref[:, cols[a]], dos[a], NT) for a in range(2)]
            pts, dsts = [], []
            for a in range(2):
                pt = jnp.exp2(sts[a])
                pts.append(pt.astype(BF16))
                dsts.append((pt * dps[a]).astype(BF16))
            for a, c in enumerate(cols):
                dv_acc[:, c] += _dot(pts[a], dos[a], NN)
                dk_acc[:, c] += _dot(dsts[a], qs[a], NN)
                dq_ref[pl.ds(q0, tq), c] += _dot(dsts[a], kbs[a], TN)
            return carry

        lax.fori_loop(0, nqc, chunk, 0)

    return pl.pallas_call(
        body, name=name, grid=(N_HEADS // 2, s // tk),
        in_specs=[pl.BlockSpec((s, 256), lambda p, j: (0, p)),
                  pl.BlockSpec((s, 256), lambda p, j: (0, p)),
                  pl.BlockSpec((tk, 256), lambda p, j: (j, p)),
                  pl.BlockSpec((tk, 256), lambda p, j: (j, p))],
        out_specs=[pl.BlockSpec((s, 256), lambda p, j: (0, p)),
                   pl.BlockSpec((tk, 256), lambda p, j: (j, p)),
                   pl.BlockSpec((tk, 256), lambda p, j: (j, p))],
        out_shape=[jax.ShapeDtypeStruct((s, width), F32)] * 3,
        compiler_params=_params(("parallel", "arbitrary")),
    )(qaug, doaug, kcat, kvb)


def _mla_prep_bwd(dqc, dkc, dvp, proj, cqn, ckvn, g_q, g_kv, w_uq, w_ukv, tabs, name):
    s = proj.shape[0]
    tm = min(256, s)
    width = N_HEADS * 128
    n_out_cols = N_LAT

    def body(dq_ref, dk_ref, dv_ref, lat_ref, cqn_ref, ckvn_ref, gq_ref, gkv_ref, wq_ref, wkv_ref,
             c_ref, sa_ref, sb_ref, dproj_ref, dwq_ref, dwkv_ref, dgq_ref, dgkv_ref):
        i = pl.program_id(0)

        @pl.when(i == 0)
        def _():
            dwq_ref[...] = jnp.zeros_like(dwq_ref)
            dwkv_ref[...] = jnp.zeros_like(dwkv_ref)
            dgq_ref[...] = jnp.zeros_like(dgq_ref)
            dgkv_ref[...] = jnp.zeros_like(dgkv_ref)

        c, sa, sb = c_ref[...], sa_ref[...], sb_ref[...]
        lane = lax.broadcasted_iota(jnp.int32, (tm, 128), 1)
        dkr = jnp.zeros((tm, 128), F32)
        dq_parts, dkv_parts = [], []
        for h in range(N_HEADS):
            cols = slice(h * 128, (h + 1) * 128)
            dq_parts.append(_rope_bwd(dq_ref[:, cols] * MLA_SCALE, c, sa, sb).astype(BF16))
            dkh = dk_ref[:, cols] * LN2
            dkr = dkr + dkh
            dkv_parts.append(jnp.where(lane < 64, dkh, dv_ref[:, cols]).astype(BF16))
        dq = jnp.concatenate(dq_parts, axis=1)
        dkv = jnp.concatenate(dkv_parts, axis=1)
        dkr = _rope_bwd(jnp.where((lane >= 64) & (lane < 96), dkr, 0.0), c, sa, sb)

        dcqn = _dot(dq, wq_ref[...], NT)
        dwq_ref[...] += _dot(cqn_ref[...], dq, TN)
        dcq, dgq = _rms_bwd(dcqn, lat_ref[:, 0:Q_LORA].astype(F32), gq_ref[...])
        dgq_ref[...] += jnp.sum(dgq, axis=0, keepdims=True)

        dckvn = _dot(dkv, wkv_ref[...], NT)
        dwkv_ref[...] += _dot(ckvn_ref[...], dkv, TN)
        dckv, dgkv = _rms_bwd(dckvn, lat_ref[:, Q_LORA:Q_LORA + KV_LORA].astype(F32), gkv_ref[...])
        dgkv_ref[...] += jnp.sum(dgkv, axis=0, keepdims=True)

        dproj_ref[:, 0:Q_LORA] = dcq.astype(BF16)
        dproj_ref[:, Q_LORA:Q_LORA + KV_LORA] = dckv.astype(BF16)
        dproj_ref[:, Q_LORA + KV_LORA:] = dkr.astype(BF16)

    row = lambda n: pl.BlockSpec((tm, n), lambda i: (i, 0))
    full = lambda a: pl.BlockSpec(a.shape, lambda i: (0, 0))
    tab = pl.BlockSpec((tm, 128), lambda i: (i, 0))
    return pl.pallas_call(
        body, name=name, grid=(s // tm,),
        in_specs=[row(width), row(width), row(width),
                  pl.BlockSpec((tm, N_LAT), lambda i: (i, COL_CQ // N_LAT)),
                  row(Q_LORA), row(KV_LORA), full(g_q), full(g_kv), full(w_uq), full(w_ukv), tab, tab, tab],
        out_specs=[row(n_out_cols), full(w_uq), full(w_ukv), full(g_q), full(g_kv)],
        out_shape=[jax.ShapeDtypeStruct((s, n_out_cols), BF16),
                   jax.ShapeDtypeStruct(w_uq.shape, F32), jax.ShapeDtypeStruct(w_ukv.shape, F32),
                   jax.ShapeDtypeStruct(g_q.shape, F32), jax.ShapeDtypeStruct(g_kv.shape, F32)],
        compiler_params=_params(("arbitrary",)),
    )(dqc, dkc, dvp, proj, cqn, ckvn, g_q, g_kv, w_uq, w_ukv, *tabs)


def _mix_out(ya, yb, na, nb, w_o, g_post, x, name):
    s = x.shape[0]
    tm = min(512, s)

    def body(ya_ref, yb_ref, na_ref, nb_ref, w_ref, g_ref, x_ref, yn_ref, y2_ref, x1_ref):
        a, b = ya_ref[...], yb_ref[...]
        yn = jnp.concatenate([a * _rstd(a) * na_ref[...], b * _rstd(b) * nb_ref[...]], axis=1).astype(BF16)
        yn_ref[...] = yn
        y2 = _dot(yn, w_ref[...], NN)
        y2_ref[...] = y2
        x1_ref[...] = x_ref[...] + y2 * _rstd(y2) * g_ref[...]

    row = lambda n: pl.BlockSpec((tm, n), lambda i: (i, 0))
    full = lambda a: pl.BlockSpec(a.shape, lambda i: (0, 0))
    return pl.pallas_call(
        body, name=name, grid=(s // tm,),
        in_specs=[row(WIDTH_A), row(WIDTH_A), full(na), full(nb), full(w_o), full(g_post), row(D_MODEL)],
        out_specs=[row(D_MODEL)] * 3,
        out_shape=[jax.ShapeDtypeStruct((s, D_MODEL), BF16), jax.ShapeDtypeStruct((s, D_MODEL), F32),
                   jax.ShapeDtypeStruct((s, D_MODEL), F32)],
        compiler_params=_params(("parallel",)),
    )(ya, yb, na, nb, w_o, g_post, x)


def _head_ones():
    blk = np.kron(np.eye(N_HEADS, dtype=np.float32), np.ones((64, 64), np.float32))
    return jnp.asarray(blk, F32)


def _outnorm_bwd(dyn, ya, yb, na, nb, ones, name):
    s = ya.shape[0]
    tm = min(256, s)

    def body(dyn_ref, ya_ref, yb_ref, na_ref, nb_ref, ones_ref, dya_ref, da_ref, do_ref, dna_ref, dnb_ref,
             dya4_ref, dya16_ref, da4_ref, da16_ref, scr):
        i = pl.program_id(0)

        @pl.when(i == 0)
        def _():
            dna_ref[...] = jnp.zeros_like(dna_ref)
            dnb_ref[...] = jnp.zeros_like(dnb_ref)

        a, b = ya_ref[...], yb_ref[...]
        dya, dna = _rms_bwd(dyn_ref[:, 0:WIDTH_A], a, na_ref[...])
        dyb, dnb = _rms_bwd(dyn_ref[:, WIDTH_A:], b, nb_ref[...])
        dna_ref[...] += jnp.sum(dna, axis=0, keepdims=True)
        dnb_ref[...] += jnp.sum(dnb, axis=0, keepdims=True)
        dya_b = dya.astype(BF16)
        dya_ref[...] = dya_b
        hp = lax.Precision.HIGHEST
        delta_a = jnp.dot(dya_b.astype(F32) * a, ones_ref[...], precision=hp, preferred_element_type=F32)
        da_ref[...] = delta_a
        _put(scr, dya_b)
        _dilate_store(dya4_ref, scr, 4)
        _dilate_store(dya16_ref, scr, 16)
        _put(scr, delta_a)
        _dilate_store(da4_ref, scr, 4)
        _dilate_store(da16_ref, scr, 16)
        dyb_b = dyb.astype(BF16)
        db = jnp.dot(dyb_b.astype(F32) * b, ones_ref[...], precision=hp, preferred_element_type=F32)
        lane = lax.broadcasted_iota(jnp.int32, (tm, 128), 1)
        zero = jnp.zeros((tm, 128), BF16)
        for p in range(4):
            cols = slice(p * 128, (p + 1) * 128)
            dyp = dyb_b[:, cols]
            dbp = db[:, cols]
            for a_ in range(2):
                src = pltpu.roll(dyp.astype(F32), 64, 1).astype(BF16) if a_ == 0 else dyp
                dlt = dbp if a_ == 0 else pltpu.roll(dbp, 64, 1)
                hi, lo = _split_hi_lo(dlt)
                blk = jnp.where(lane >= 64, src, jnp.where(lane == 0, -hi, jnp.where(lane == 1, -lo, zero)))
                h = 2 * p + a_
                do_ref[:, h * 128:(h + 1) * 128] = blk

    row = lambda n: pl.BlockSpec((tm, n), lambda i: (i, 0))
    full = lambda a: pl.BlockSpec(a.shape, lambda i: (0, 0))
    return pl.pallas_call(
        body, name=name, grid=(s // tm,),
        in_specs=[row(D_MODEL), row(WIDTH_A), row(WIDTH_A), full(na), full(nb), full(ones)],
        out_specs=[row(WIDTH_A), row(WIDTH_A), row(N_HEADS * 128), full(na), full(nb),
                   _dil_spec(tm, 4, WIDTH_A), _dil_spec(tm, 16, WIDTH_A), _dil_spec(tm, 4, WIDTH_A), _dil_spec(tm, 16, WIDTH_A)],
        out_shape=[jax.ShapeDtypeStruct((s, WIDTH_A), BF16), jax.ShapeDtypeStruct((s, WIDTH_A), F32),
                   jax.ShapeDtypeStruct((s, N_HEADS * 128), BF16),
                   jax.ShapeDtypeStruct(na.shape, F32), jax.ShapeDtypeStruct(nb.shape, F32),
                   _dil_shape(s, 4, WIDTH_A, BF16), _dil_shape(s, 16, WIDTH_A, BF16),
                   _dil_shape(s, 4, WIDTH_A, F32), _dil_shape(s, 16, WIDTH_A, F32)],
        scratch_shapes=[_slab_scratch(tm, WIDTH_A)],
        compiler_params=_params(("arbitrary",)),
    )(dyn, ya, yb, na, nb, ones)


def _sum_cast(parts, name):
    s = parts[0][0].shape[0]
    tm = min(512, s)

    def body(*refs):
        o_ref, s4, s16 = refs[9:]
        for t in range(3):
            _undilate(s4, refs[3 + t], 4)
            _undilate(s16, refs[6 + t], 16)
            acc = refs[t][...] + _get(s4) + _get(s16)
            o_ref[:, t * WIDTH_A:(t + 1) * WIDTH_A] = acc.astype(BF16)

    nat = pl.BlockSpec((tm, WIDTH_A), lambda i: (i, 0))
    flat = [parts[g][t] for g in range(3) for t in range(3)]
    return pl.pallas_call(
        body, name=name, grid=(s // tm,),
        in_specs=[nat] * 3 + [_dil_spec(tm, 4, WIDTH_A)] * 3 + [_dil_spec(tm, 16, WIDTH_A)] * 3,
        out_specs=pl.BlockSpec((tm, 3 * WIDTH_A), lambda i: (i, 0)),
        out_shape=jax.ShapeDtypeStruct((s, 3 * WIDTH_A), BF16),
        scratch_shapes=[_slab_scratch(tm, WIDTH_A)] * 2,
        compiler_params=_params(("parallel",)),
    )(*flat)


HALO = 16


def _gelu(x):
    k = math.sqrt(2.0 / math.pi)
    t = jnp.tanh(k * (x + 0.044715 * x * x * x))
    return 0.5 * x * (1.0 + t), t


def _gelu_grad(x, t):
    k = math.sqrt(2.0 / math.pi)
    return 0.5 * (1.0 + t) + 0.5 * x * (1.0 - t * t) * k * (1.0 + 3 * 0.044715 * x * x)


def _halo_specs(s, tm, tn, lead):
    nb = s // HALO
    hb = tm // HALO
    pre = (lead,) if lead else ()
    z = (0,) if lead else ()
    main = pl.BlockSpec(pre + (tm, tn), lambda j, i: z + (i, j))
    prev = pl.BlockSpec(pre + (HALO, tn), lambda j, i: z + (jnp.maximum(i * hb - 1, 0), j))
    nxt = pl.BlockSpec(pre + (HALO, tn), lambda j, i: z + (jnp.minimum((i + 1) * hb, nb - 1), j))
    return [prev, main, nxt]


def _fill_ext(ext, prev, main, nxt, i, tm, s):
    ext[0:HALO, :] = jnp.where(i > 0, prev.astype(F32), 0.0)
    ext[HALO:HALO + tm, :] = main.astype(F32)
    ext[HALO + tm:2 * HALO + tm, :] = jnp.where((i + 1) * tm < s, nxt.astype(F32), 0.0)


STRIP = 16


def _shifted(ref, row0):
    n = STRIP + 16
    win = ref[pl.ds(pl.multiple_of(row0 - 8, 8), n), :]
    return pltpu.roll(win, 1, 0)[8:8 + STRIP], win[8:8 + STRIP], pltpu.roll(win, n - 1, 0)[8:8 + STRIP]


def _conv3(e, row0, w_ref, b_ref, t):
    m1, c0, p1 = _shifted(e, row0)
    return w_ref[t, 0:1, :] * m1 + w_ref[t, 1:2, :] * c0 + w_ref[t, 2:3, :] * p1 + b_ref[t]


def _conv_gate(up, cw, cb, name):
    _, s, c = up.shape
    tm = min(512, s)
    tn = FF_SLAB

    def body(up_p, up_m, up_n, w_ref, b_ref, a_ref, eg, ev):
        i = pl.program_id(1)
        _fill_ext(eg, up_p[0], up_m[0], up_n[0], i, tm, s)
        _fill_ext(ev, up_p[1], up_m[1], up_n[1], i, tm, s)

        def strip(t, carry):
            r0 = pl.multiple_of(t * STRIP, STRIP)
            g, _ = _gelu(_conv3(eg, HALO + r0, w_ref, b_ref, 0))
            a_ref[pl.ds(r0, STRIP), :] = (g * _conv3(ev, HALO + r0, w_ref, b_ref, 1)).astype(BF16)
            return carry

        lax.fori_loop(0, tm // STRIP, strip, 0)

    return pl.pallas_call(
        body, name=name, grid=(c // tn, s // tm),
        in_specs=_halo_specs(s, tm, tn, 2)
        + [pl.BlockSpec((2, 3, tn), lambda j, i: (0, 0, j)), pl.BlockSpec((2, 1, tn), lambda j, i: (0, 0, j))],
        out_specs=pl.BlockSpec((tm, tn), lambda j, i: (i, j)),
        out_shape=jax.ShapeDtypeStruct((s, c), BF16),
        scratch_shapes=[pltpu.VMEM((tm + 2 * HALO, tn), F32)] * 2,
        compiler_params=_params(("parallel", "parallel")),
    )(up, up, up, cw, cb)


def _conv_gate_bwd(up, da, cw, cb, name):
    _, s, c = up.shape
    tm = min(256, s)
    tn = FF_SLAB
    te = tm + HALO

    def body(up_p, up_m, up_n, da_p, da_m, da_n, w_ref, b_ref, dup_ref, dw_ref, db_ref, eg, ev, ed, dug, duv):
        i = pl.program_id(1)

        @pl.when(i == 0)
        def _():
            dw_ref[...] = jnp.zeros_like(dw_ref)
            db_ref[...] = jnp.zeros_like(db_ref)

        _fill_ext(eg, up_p[0], up_m[0], up_n[0], i, tm, s)
        _fill_ext(ev, up_p[1], up_m[1], up_n[1], i, tm, s)
        _fill_ext(ed, da_p[...], da_m[...], da_n[...], i, tm, s)
        o = HALO // 2

        def du_strip(t, carry):
            r0 = pl.multiple_of(t * STRIP, STRIP)
            ug = _conv3(eg, o + r0, w_ref, b_ref, 0)
            uv = _conv3(ev, o + r0, w_ref, b_ref, 1)
            gl, th = _gelu(ug)
            dav = ed[pl.ds(pl.multiple_of(o + r0, 8), STRIP), :]
            dug[pl.ds(r0, STRIP), :] = dav * uv * _gelu_grad(ug, th)
            duv[pl.ds(r0, STRIP), :] = dav * gl
            return carry

        lax.fori_loop(0, te // STRIP, du_strip, 0)

        def back(du, e, t):
            def strip(k, acc):
                r0 = pl.multiple_of(k * STRIP, STRIP)
                dm1, c0, dp1 = _shifted(du, o + r0)
                dup_ref[t, pl.ds(r0, STRIP), :] = (w_ref[t, 0:1, :] * dp1 + w_ref[t, 1:2, :] * c0
                                                   + w_ref[t, 2:3, :] * dm1).astype(BF16)
                um1, u0, up1 = _shifted(e, HALO + r0)
                fold = lambda a: a[0:8] + a[8:16]
                return (acc[0] + fold(um1 * c0), acc[1] + fold(u0 * c0), acc[2] + fold(up1 * c0), acc[3] + fold(c0))

            zero = jnp.zeros((8, tn), F32)
            acc = lax.fori_loop(0, tm // STRIP, strip, (zero, zero, zero, zero))
            for k in range(3):
                dw_ref[t, k:k + 1, :] += jnp.sum(acc[k], axis=0, keepdims=True)
            db_ref[t] += jnp.sum(acc[3], axis=0, keepdims=True)

        back(dug, eg, 0)
        back(duv, ev, 1)

    wspec = pl.BlockSpec((2, 3, tn), lambda j, i: (0, 0, j))
    bspec = pl.BlockSpec((2, 1, tn), lambda j, i: (0, 0, j))
    return pl.pallas_call(
        body, name=name, grid=(c // tn, s // tm),
        in_specs=_halo_specs(s, tm, tn, 2) + _halo_specs(s, tm, tn, 0) + [wspec, bspec],
        out_specs=[pl.BlockSpec((2, tm, tn), lambda j, i: (0, i, j)), wspec, bspec],
        out_shape=[jax.ShapeDtypeStruct((2, s, c), BF16), jax.ShapeDtypeStruct((2, 3, c), F32),
                   jax.ShapeDtypeStruct((2, 1, c), F32)],
        scratch_shapes=[pltpu.VMEM((tm + 2 * HALO, tn), F32)] * 3 + [pltpu.VMEM((te, tn), F32)] * 2,
        compiler_params=_params(("parallel", "arbitrary")),
    )(up, up, up, da, da, da, cw, cb)


def _ffn_out(a, w_down, g_post, x1, target, name):
    s = x1.shape[0]
    tm = min(256, s)

    def body(a_ref, w_ref, g_ref, x1_ref, t_ref, dy3_ref, dx2_ref, loss_ref, dg_ref):
        i = pl.program_id(0)

        @pl.when(i == 0)
        def _():
            loss_ref[...] = jnp.zeros_like(loss_ref)
            dg_ref[...] = jnp.zeros_like(dg_ref)

        y3 = _dot(a_ref[...], w_ref[...], NN)
        g = g_ref[...]
        x2 = x1_ref[...] + y3 * _rstd(y3) * g
        diff = x2 - t_ref[...]
        loss_ref[...] += jnp.sum(jnp.sum(diff * diff, axis=1, keepdims=True), axis=0, keepdims=True)
        dx2 = diff * (1.0 / D_MODEL)
        dx2_ref[...] = dx2
        dy3, dg = _rms_bwd(dx2, y3, g)
        dy3_ref[...] = dy3.astype(BF16)
        dg_ref[...] += jnp.sum(dg, axis=0, keepdims=True)

    row = lambda n: pl.BlockSpec((tm, n), lambda i: (i, 0))
    full = lambda t: pl.BlockSpec(t.shape, lambda i: (0, 0))
    return pl.pallas_call(
        body, name=name, grid=(s // tm,),
        in_specs=[row(a.shape[1]), full(w_down), full(g_post), row(D_MODEL), row(D_MODEL)],
        out_specs=[row(D_MODEL), row(D_MODEL), pl.BlockSpec((8, 128), lambda i: (0, 0)), full(g_post)],
        out_shape=[jax.ShapeDtypeStruct((s, D_MODEL), BF16), jax.ShapeDtypeStruct((s, D_MODEL), F32),
                   jax.ShapeDtypeStruct((8, 128), F32), jax.ShapeDtypeStruct(g_post.shape, F32)],
        compiler_params=_params(("arbitrary",)),
    )(a, w_down, g_post, x1, target)


def _resnorm_bwd(dh2, x1, g_ffn_pre, dx2, y2, g_mix_post, name):
    s = x1.shape[0]
    tm = min(256, s)

    def body(dh_ref, x1_ref, gf_ref, dx2_ref, y2_ref, gp_ref, dx1_ref, dy2_ref, dgf_ref, dgp_ref):
        i = pl.program_id(0)

        @pl.when(i == 0)
        def _():
            dgf_ref[...] = jnp.zeros_like(dgf_ref)
            dgp_ref[...] = jnp.zeros_like(dgp_ref)

        dn, dgf = _rms_bwd(dh_ref[...], x1_ref[...], gf_ref[...])
        dx1 = dx2_ref[...] + dn
        dx1_ref[...] = dx1
        dgf_ref[...] += jnp.sum(dgf, axis=0, keepdims=True)
        dy2, dgp = _rms_bwd(dx1, y2_ref[...], gp_ref[...])
        dy2_ref[...] = dy2.astype(BF16)
        dgp_ref[...] += jnp.sum(dgp, axis=0, keepdims=True)

    row = pl.BlockSpec((tm, D_MODEL), lambda i: (i, 0))
    full = pl.BlockSpec((1, D_MODEL), lambda i: (0, 0))
    return pl.pallas_call(
        body, name=name, grid=(s // tm,),
        in_specs=[row, row, full, row, row, full],
        out_specs=[row, row, full, full],
        out_shape=[jax.ShapeDtypeStruct((s, D_MODEL), F32), jax.ShapeDtypeStruct((s, D_MODEL), BF16),
                   jax.ShapeDtypeStruct((1, D_MODEL), F32), jax.ShapeDtypeStruct((1, D_MODEL), F32)],
        compiler_params=_params(("arbitrary",)),
    )(dh2, x1, g_ffn_pre, dx2, y2, g_mix_post)


def _final_bwd(dh1, x, g_pre, dx1, name):
    s = x.shape[0]
    tm = min(256, s)

    def body(dh_ref, x_ref, g_ref, dx1_ref, dx_ref, dg_ref):
        @pl.when(pl.program_id(0) == 0)
        def _():
            dg_ref[...] = jnp.zeros_like(dg_ref)

        dn, dg = _rms_bwd(dh_ref[...], x_ref[...], g_ref[...])
        dx_ref[...] = dx1_ref[...] + dn
        dg_ref[...] += jnp.sum(dg, axis=0, keepdims=True)

    row = pl.BlockSpec((tm, D_MODEL), lambda i: (i, 0))
    full = pl.BlockSpec((1, D_MODEL), lambda i: (0, 0))
    return pl.pallas_call(
        body, name=name, grid=(s // tm,),
        in_specs=[row, row, full, row], out_specs=[row, full],
        out_shape=[jax.ShapeDtypeStruct((s, D_MODEL), F32), jax.ShapeDtypeStruct((1, D_MODEL), F32)],
        compiler_params=_params(("arbitrary",)),
    )(dh1, x, g_pre, dx1)


def _local_step(x, target, fw, rep, mixer_weights, early_grads, late_grads):
    s = x.shape[0]
    tabs = _rope_tables(s)
    w_in, w_uq, w_ukv, w_o = (fw[n] for n in ("w_in", "w_uq", "w_ukv", "w_o"))
    tr = min(2048, s)
    tcon = min(2048, s)

    proj, h1, qkv4, qkv16 = _in_proj(x, rep["norm_mix_pre"], w_in, "in_proj")
    qkv = {1: proj, 4: qkv4, 16: qkv16}
    q_of = lambda r: (qkv[r], lambda c: 3 * c)
    k_of = lambda r: (qkv[r], lambda c: 3 * c + 1)
    v_of = lambda r: (qkv[r], lambda c: 3 * c + 2)
    own = lambda a: (a, lambda c: c)
    biases = [_band_bias(r) for _, r in DIL_CONFIGS]
    os_, lses = [], []
    for g, (_, r) in enumerate(DIL_CONFIGS):
        o, l = _band_call("fwd", r, [q_of(r)], [k_of(r), v_of(r)], biases[g], f"band_fwd_r{r}")
        os_.append(o)
        lses.append(l)
    ya, lse_a, lse4, lse16 = _band_combine(os_, lses, "band_combine")
    qcat, kcat, kvb, cqn, ckvn = _mla_prep(proj, rep["q_lat_norm"], rep["kv_lat_norm"], w_uq, w_ukv, tabs, "mla_prep")
    yb, qaug = _mla_fwd(qcat, kcat, kvb, "mla_fwd")
    yn, y2, x1 = _mix_out(ya, yb, rep["out_norm_a"], rep["out_norm_b"], w_o, rep["norm_mix_post"], x, "mix_out")
    mw = mixer_weights(x1)
    w_up, w_down, cw, cb = mw["w_up"], mw["w_down"], mw["conv_w"], mw["conv_b"]
    ff = w_down.shape[0]
    up, h2 = _norm_matmul(x1, rep["norm_ffn_pre"], w_up, None, "up_proj")
    act = _conv_gate(up, cw, cb, "conv_gate")
    dy3, dx2, loss_acc, dg_ffn_post = _ffn_out(act, w_down, rep["norm_ffn_post"], x1, target, "ffn_out")

    grads = {"norm_ffn_post": dg_ffn_post}
    dact = _matmul(dy3, w_down, "nt", BF16, tr, ff // 2, D_MODEL, "d_act")
    grads["w_down"] = _matmul(act, dy3, "tn", F32, ff // 2, D_MODEL, tcon, "dw_down")
    dup, grads["conv_w"], grads["conv_b"] = _conv_gate_bwd(up, dact, cw, cb, "conv_gate_bwd")
    half = N_DEV // 2
    dh2 = _matmul_core(
        dup, w_up, NT, (s // tr, 1, N_DEV),
        pl.BlockSpec((None, tr, FF_SLAB), lambda i, j, t: (t // half, i, t % half)),
        pl.BlockSpec((None, D_MODEL, FF_SLAB), lambda i, j, t: (t, 0, 0)),
        pl.BlockSpec((tr, D_MODEL), lambda i, j, t: (i, 0)),
        jax.ShapeDtypeStruct((s, D_MODEL), F32), (tr, D_MODEL), "d_h2")
    grads["w_up"] = _matmul_core(
        h2, dup, TN, (1, N_DEV, s // tcon),
        pl.BlockSpec((tcon, D_MODEL), lambda i, j, t: (t, 0)),
        pl.BlockSpec((None, tcon, FF_SLAB), lambda i, j, t: (j // half, t, j % half)),
        pl.BlockSpec((None, D_MODEL, FF_SLAB), lambda i, j, t: (j, 0, 0)),
        jax.ShapeDtypeStruct((N_DEV, D_MODEL, FF_SLAB), F32), (D_MODEL, FF_SLAB), "dw_up")
    dx1, dy2, grads["norm_ffn_pre"], grads["norm_mix_post"] = _resnorm_bwd(
        dh2, x1, rep["norm_ffn_pre"], dx2, y2, rep["norm_mix_post"], "resnorm_bwd")
    dyn = _matmul(dy2, w_o, "nt", F32, tr, D_MODEL, D_MODEL, "d_yn")
    grads["w_o"] = _matmul(yn, dy2, "tn", F32, D_MODEL, D_MODEL, tcon, "dw_o")
    token = early_grads(grads)
    dya, delta_a, doaug, grads["out_norm_a"], grads["out_norm_b"], dya4, dya16, delta4, delta16 = _outnorm_bwd(
        dyn, ya, yb, rep["out_norm_a"] + token, rep["out_norm_b"], _head_ones(), "outnorm_bwd")
    stats = {1: (dya, lse_a, delta_a), 4: (dya4, lse4, delta4), 16: (dya16, lse16, delta16)}
    parts = []
    for g, (_, r) in enumerate(DIL_CONFIGS):
        qside = [q_of(r)] + [own(a) for a in stats[r]]
        kside = [k_of(r), v_of(r)]
        (dq,) = _band_call("dq", r, qside, kside, biases[g], f"band_dq_r{r}")
        dk, dv = _band_call("dkv", r, kside, qside, biases[g], f"band_dkv_r{r}")
        parts.append((dq, dk, dv))
    dproj_a = _sum_cast(parts, "band_grad_sum")
    dqc, dkc, dvp = _mla_bwd(qaug, kcat, kvb, doaug, "mla_bwd")
    dproj_b, grads["w_uq"], grads["w_ukv"], grads["q_lat_norm"], grads["kv_lat_norm"] = _mla_prep_bwd(
        dqc, dkc, dvp, proj, cqn, ckvn, rep["q_lat_norm"], rep["kv_lat_norm"], w_uq, w_ukv, tabs, "mla_prep_bwd")
    dproj = jnp.concatenate([dproj_a, dproj_b], axis=1)
    grads["w_in"] = _matmul(h1, dproj, "tn", F32, D_MODEL, N_PROJ // 2, tcon, "dw_in")
    token = late_grads(grads)
    dh1 = _matmul(dproj, w_in, "nt", F32, tr, D_MODEL, N_PROJ // 2, "d_h1")
    grad_x, grads["norm_mix_pre"] = _final_bwd(dh1, x, rep["norm_mix_pre"] + token, dx1, "final_bwd")
    loss = 0.5 / D_MODEL * loss_acc[0, 0]
    return loss, grad_x, grads


MESH = pl.DeviceIdType.MESH
HBM_SPEC = pl.BlockSpec(memory_space=pltpu.HBM)
SMALL_ROWS = 96
BUF_SHAPES = {"w_up": (D_MODEL, FF_SLAB), "w_in": (D_MODEL, 384), "w_down": (D_FF // N_DEV, D_MODEL),
              "w_o": (D_MODEL // N_DEV, D_MODEL), "w_uq": (Q_LORA, 128), "w_ukv": (KV_LORA, 128), "conv_w": (8, FF_SLAB)}
BUF_ORDER = tuple(BUF_SHAPES)
MIXING = ("w_in", "w_o", "w_uq", "w_ukv")
MIXER = ("w_up", "w_down", "conv_w")
EARLY_GRADS = ("w_up", "w_down", "w_o", "conv_w")
LATE_GRADS = ("w_in", "w_uq", "w_ukv")


def _all_gather(bufs, name):
    nb = len(bufs)

    def body(*refs):
        x_refs, out_refs = refs[:nb], refs[nb:2 * nb]
        send_sems, recv_sems, local_sems = refs[2 * nb:]
        x, y, c = lax.axis_index("x"), lax.axis_index("y"), lax.axis_index("c")
        me, sibling = (x, y, c), (x, y, 1 - c)
        chips = [(1 - x, y), (x, 1 - y), (1 - x, 1 - y)]

        def copy(b, k, block, to, own=False):
            px, py, pc = block
            slot = out_refs[b].at[4 * px + 2 * py + pc]
            return pltpu.make_async_remote_copy(
                src_ref=x_refs[b] if own else slot, dst_ref=slot,
                send_sem=send_sems.at[7 * b + k], recv_sem=recv_sems.at[7 * b + k], device_id=to, device_id_type=MESH)

        mine = [pltpu.make_async_copy(x_refs[b], out_refs[b].at[4 * x + 2 * y + c], local_sems.at[b]) for b in range(nb)]
        sends = []
        for b in range(nb):
            mine[b].start()
            first = [copy(b, 0, me, sibling, own=True)]
            first += [copy(b, 1 + j, me, (*chip, c), own=True) for j, chip in enumerate(chips)]
            for cp in first:
                cp.start()
            sends += first
        for j, chip in enumerate(chips):
            for b in range(nb):
                copy(b, 1 + j, (*chip, c), me).wait_recv()
                passed = copy(b, 4 + j, (*chip, c), sibling)
                passed.start()
                sends.append(passed)
        for b in range(nb):
            copy(b, 0, sibling, me).wait_recv()
            for j, chip in enumerate(chips):
                copy(b, 4 + j, (*chip, 1 - c), me).wait_recv()
        for cp in sends:
            cp.wait_send()
        for cp in mine:
            cp.wait()

    return pl.pallas_call(
        body, name=name,
        out_shape=[jax.ShapeDtypeStruct((N_DEV,) + p.shape, p.dtype) for p in bufs],
        in_specs=[HBM_SPEC] * nb, out_specs=[HBM_SPEC] * nb,
        scratch_shapes=[pltpu.SemaphoreType.DMA((7 * nb,)), pltpu.SemaphoreType.DMA((7 * nb,)),
                        pltpu.SemaphoreType.DMA((nb,))],
    )(*bufs)


def _grad_exchange(bigs, small, name):
    flips = [(fx, fy, fc) for fx in (0, 1) for fy in (0, 1) for fc in (0, 1)][1:]
    nb = len(bigs)

    def body(*refs):
        big_refs, small_ref = refs[:nb], refs[nb]
        rbig_refs, rsmall_ref = refs[nb + 1:2 * nb + 1], refs[2 * nb + 1]
        send_sems, recv_sems, local_sems = refs[2 * nb + 2:]
        x, y, c = lax.axis_index("x"), lax.axis_index("y"), lax.axis_index("c")
        my = 4 * x + 2 * y + c
        own = [pltpu.make_async_copy(big_refs[b].at[my], rbig_refs[b].at[my], local_sems.at[b]) for b in range(nb)]
        own.append(pltpu.make_async_copy(small_ref, rsmall_ref.at[my], local_sems.at[nb]))
        for cp in own:
            cp.start()
        copies = []
        for b in range(nb + 1):
            for k, (fx, fy, fc) in enumerate(flips):
                px = 1 - x if fx else x
                py = 1 - y if fy else y
                pc = 1 - c if fc else c
                src = small_ref if b == nb else big_refs[b].at[4 * px + 2 * py + pc]
                dst = rsmall_ref.at[my] if b == nb else rbig_refs[b].at[my]
                copies.append(pltpu.make_async_remote_copy(
                    src_ref=src, dst_ref=dst, send_sem=send_sems.at[7 * b + k], recv_sem=recv_sems.at[7 * b + k],
                    device_id=(px, py, pc), device_id_type=MESH))
        for cp in copies:
            cp.start()
        for cp in copies:
            cp.wait()
        for cp in own:
            cp.wait()

    nsem = 7 * (nb + 1)
    return pl.pallas_call(
        body, name=name,
        out_shape=[jax.ShapeDtypeStruct(b.shape, b.dtype) for b in bigs]
        + [jax.ShapeDtypeStruct((N_DEV,) + small.shape, small.dtype)],
        in_specs=[HBM_SPEC] * (nb + 1), out_specs=[HBM_SPEC] * (nb + 1),
        scratch_shapes=[pltpu.SemaphoreType.DMA((nsem,)), pltpu.SemaphoreType.DMA((nsem,)),
                        pltpu.SemaphoreType.DMA((nb + 1,))],
    )(*bigs, small)


SEM_SPEC = pl.BlockSpec(memory_space=pltpu.SEMAPHORE)
ANY_SPEC = pl.BlockSpec(memory_space=pl.ANY)
FLIPS = tuple((fx, fy, fc) for fx in (0, 1) for fy in (0, 1) for fc in (0, 1))[1:]


def _split_copies(src_refs, land_refs, send_sems, recv_sems, scatter):
    x, y, c = lax.axis_index("x"), lax.axis_index("y"), lax.axis_index("c")
    my = 4 * x + 2 * y + c
    copies = []
    for b, (src, land) in enumerate(zip(src_refs, land_refs)):
        for k, (fx, fy, fc) in enumerate(FLIPS):
            px = 1 - x if fx else x
            py = 1 - y if fy else y
            pc = 1 - c if fc else c
            copies.append(pltpu.make_async_remote_copy(
                src_ref=src.at[4 * px + 2 * py + pc] if scatter else src, dst_ref=land.at[my],
                send_sem=send_sems.at[7 * b + k], recv_sem=recv_sems.at[7 * b + k],
                device_id=(px, py, pc), device_id_type=MESH))
    return copies


def _exchange_start(srcs, scatter, name):
    nb = len(srcs)
    lands = [lax.empty(s.shape if scatter else (N_DEV,) + s.shape, s.dtype) for s in srcs]

    def body(*refs):
        src_refs, land_refs = refs[:nb], refs[nb:2 * nb]
        send_sems, recv_sems = refs[2 * nb], refs[2 * nb + 1]
        token = refs[-1]
        for cp in _split_copies(src_refs, land_refs, send_sems, recv_sems, scatter):
            cp.start()
        token[...] = jnp.zeros_like(token)

    hbm = lambda a: pltpu.HBM(a.shape, a.dtype)
    outs = pl.pallas_call(
        body, name=name,
        out_shape=(pltpu.SemaphoreType.DMA((7 * nb,)), pltpu.SemaphoreType.DMA((7 * nb,)),
                   *[hbm(a) for a in srcs], *[hbm(a) for a in lands], jax.ShapeDtypeStruct((8, 128), F32)),
        in_specs=[HBM_SPEC] * (2 * nb),
        out_specs=(SEM_SPEC, SEM_SPEC, *[HBM_SPEC] * (2 * nb), pl.BlockSpec(memory_space=pltpu.VMEM)),
        input_output_aliases={i: 2 + i for i in range(2 * nb)},
        compiler_params=pltpu.CompilerParams(has_side_effects=pltpu.SideEffectType.DATAFLOW_SIDE_EFFECTING),
    )(*[pltpu.with_memory_space_constraint(a, pltpu.HBM) for a in srcs],
      *[pltpu.with_memory_space_constraint(a, pltpu.HBM) for a in lands])
    return outs[0], outs[1], list(outs[2:2 + nb]), list(outs[2 + nb:2 + 2 * nb]), outs[-1]


def _exchange_wait(started, scatter, after, name):
    send_sems, recv_sems, srcs, lands, _ = started
    nb = len(srcs)

    def body(*refs):
        src_refs, land_refs = refs[:nb], refs[nb:2 * nb]
        for cp in _split_copies(src_refs, land_refs, refs[2 * nb], refs[2 * nb + 1], scatter):
            cp.wait_send()
            cp.wait_recv()

    hbm = lambda a: pltpu.HBM(a.shape, a.dtype)
    outs = pl.pallas_call(
        body, name=name,
        out_shape=(*[hbm(a) for a in srcs], *[hbm(a) for a in lands]),
        in_specs=[HBM_SPEC] * (2 * nb) + [SEM_SPEC, SEM_SPEC, ANY_SPEC],
        out_specs=tuple([HBM_SPEC] * (2 * nb)),
        input_output_aliases={i: i for i in range(2 * nb)},
        compiler_params=pltpu.CompilerParams(has_side_effects=pltpu.SideEffectType.DATAFLOW_SIDE_EFFECTING),
    )(*srcs, *lands, send_sems, recv_sems, after)
    return list(outs[:nb]), list(outs[nb:])


def _own_slot(land, own):
    my = 4 * lax.axis_index("x") + 2 * lax.axis_index("y") + lax.axis_index("c")
    return lax.dynamic_update_slice(land, own[None], (my,) + (0,) * own.ndim)


def _adamw(parts, w, m, v, name):
    rows, n = w.shape
    tm = rows if rows <= 384 else 256
    assert rows % tm == 0

    def body(p_ref, w_ref, m_ref, v_ref, g_ref, d_ref, m2_ref, v2_ref):
        g = p_ref[0, :, 0:n].astype(F32)
        for s in range(1, N_DEV):
            g = g + p_ref[s, :, 0:n].astype(F32)
        g_ref[...] = g
        m2 = ADAM_B1 * m_ref[...] + (1.0 - ADAM_B1) * g
        v2 = ADAM_B2 * v_ref[...] + (1.0 - ADAM_B2) * jnp.square(g)
        m2_ref[...] = m2
        v2_ref[...] = v2
        m_hat = m2 / (1.0 - ADAM_B1 ** ADAM_STEP)
        v_hat = v2 / (1.0 - ADAM_B2 ** ADAM_STEP)
        d_ref[...] = -ADAM_LR * (m_hat / (jnp.sqrt(v_hat) + ADAM_EPS) + ADAM_WD * w_ref[...])

    row = pl.BlockSpec((tm, n), lambda i: (i, 0))
    return pl.pallas_call(
        body, name=name, grid=(rows // tm,),
        in_specs=[pl.BlockSpec((N_DEV, tm, parts.shape[2]), lambda i: (0, i, 0)), row, row, row],
        out_specs=[row] * 4,
        out_shape=[jax.ShapeDtypeStruct((rows, n), F32)] * 4,
        compiler_params=_params(("parallel",)),
    )(parts, w, m, v)


def _pack(flat_parts, rows):
    flat = jnp.concatenate(flat_parts, axis=-1)
    pad = rows * LANES - flat.shape[-1]
    flat = jnp.pad(flat, [(0, 0)] * (flat.ndim - 1) + [(0, pad)])
    return flat.reshape(flat.shape[:-1] + (rows, LANES))


def _unpack(packed, shapes):
    flat = packed.reshape(packed.shape[:-2] + (-1,))
    out, off = {}, 0
    for name, shape in shapes.items():
        n = int(np.prod(shape))
        out[name] = flat[..., off:off + n].reshape(flat.shape[:-1] + tuple(shape))
        off += n
    return out


def _pad_to(a, shape):
    return jnp.pad(a, [(0, t - d) for d, t in zip(a.shape, shape)])


def _pad_w_in(w):
    k = w.shape[0]
    z = lambda n: jnp.zeros((k, n), w.dtype)
    return jnp.concatenate([w[:, :COL_KR], z(64), w[:, COL_KR:], z(32)], axis=1)


def _unpad_w_in(w):
    return jnp.concatenate([w[:, :COL_KR], w[:, COL_KR + 64:COL_KR + 96]], axis=1)


def _assemble_weights(g, conv_b):
    half = N_DEV // 2
    cols = lambda a: a.transpose(1, 0, 2).reshape(a.shape[1], N_DEV * a.shape[2])
    make = {
        "w_in": lambda: _pad_w_in(cols(g["w_in"][:, :, :D_IN // N_DEV])),
        "w_uq": lambda: cols(g["w_uq"]),
        "w_ukv": lambda: cols(g["w_ukv"]),
        "w_o": lambda: g["w_o"].reshape(D_MODEL, D_MODEL),
        "w_up": lambda: g["w_up"],
        "w_down": lambda: _pad_to(g["w_down"].reshape(half, FF_SHARD, D_MODEL),
                                  (half, FF_SLAB, D_MODEL)).reshape(half * FF_SLAB, D_MODEL),
        "conv_w": lambda: g["conv_w"][:, :3].reshape(2, half, 3, FF_SLAB).transpose(0, 2, 1, 3).reshape(2, 3, half * FF_SLAB),
    }
    fw = {n: make[n]() for n in g}
    if conv_b is not None:
        fw["conv_b"] = _pad_to(conv_b.reshape(2, 1, half, FF_SHARD), (2, 1, half, FF_SLAB)).reshape(2, 1, half * FF_SLAB)
    return fw


def _grad_bufs(grads, names):
    half = N_DEV // 2
    slabs = lambda a: a.reshape(a.shape[0], N_DEV, a.shape[1] // N_DEV).transpose(1, 0, 2)
    make = {
        "w_in": lambda: _pad_to(slabs(_unpad_w_in(grads["w_in"])), (N_DEV,) + BUF_SHAPES["w_in"]),
        "w_uq": lambda: slabs(grads["w_uq"]),
        "w_ukv": lambda: slabs(grads["w_ukv"]),
        "w_o": lambda: grads["w_o"].reshape((N_DEV,) + BUF_SHAPES["w_o"]),
        "w_up": lambda: grads["w_up"],
        "w_down": lambda: grads["w_down"].reshape(half, FF_SLAB, D_MODEL)[:, :FF_SHARD].reshape((N_DEV,) + BUF_SHAPES["w_down"]),
        "conv_w": lambda: _pad_to(grads["conv_w"].reshape(2, 3, half, FF_SLAB).transpose(0, 2, 1, 3).reshape(N_DEV, 3, FF_SLAB),
                                  (N_DEV,) + BUF_SHAPES["conv_w"]),
    }
    return [make[n]() if n == "conv_w" else make[n]().astype(BF16) for n in names]


def kernel(x, norm_mix_pre, w_in, q_lat_norm, w_uq, kv_lat_norm, w_ukv, out_norm_a, out_norm_b, w_o, norm_mix_post, norm_ffn_pre, w_up, conv_w, conv_b, w_down, norm_ffn_post, loss_target, m_norm_mix_pre, m_w_in, m_q_lat_norm, m_w_uq, m_kv_lat_norm, m_w_ukv, m_out_norm_a, m_out_norm_b, m_w_o, m_norm_mix_post, m_norm_ffn_pre, m_w_up, m_conv_w, m_conv_b, m_w_down, m_norm_ffn_post, v_norm_mix_pre, v_w_in, v_q_lat_norm, v_w_uq, v_kv_lat_norm, v_w_ukv, v_out_norm_a, v_out_norm_b, v_w_o, v_norm_mix_post, v_norm_ffn_pre, v_w_up, v_conv_w, v_conv_b, v_w_down, v_norm_ffn_post):
    given = dict(locals())
    w = {n: given[n][0] for n in WEIGHTS}
    m = {n: given["m_" + n][0] for n in WEIGHTS}
    v = {n: given["v_" + n][0] for n in WEIGHTS}
    rep_shapes = {n: w[n].shape for n in REPLICATED}

    buf = lambda n: _pad_to(w[n] if n == "conv_w" else w[n].astype(BF16), BUF_SHAPES[n])
    first = dict(zip(MIXING, _all_gather([buf(n) for n in MIXING], "weight_all_gather")))
    fw = _assemble_weights(first, None)
    tie = first["w_o"][0, 0, 0].astype(F32) * 0.0
    late_bufs = [buf(n) + tie.astype(w[n].dtype if n == "conv_w" else BF16) for n in MIXER]
    mixer_started = _exchange_start(late_bufs, False, "mixer_weights_start")
    rep = {n: given[n] for n in REPLICATED}
    rep["norm_mix_pre"] = rep["norm_mix_pre"] + mixer_started[4][0, 0]

    def mixer_weights(after):
        srcs, lands = _exchange_wait(mixer_started, False, after, "mixer_weights_wait")
        got = {n: _own_slot(land, own) for n, land, own in zip(MIXER, lands, srcs)}
        return _assemble_weights(got, conv_b)

    early = {}

    def early_grads(grads):
        early["started"] = _exchange_start(_grad_bufs(grads, EARLY_GRADS), True, "early_grads_start")
        return early["started"][4][0, 0]

    def late_grads(grads):
        early["late"] = _exchange_start(_grad_bufs(grads, LATE_GRADS), True, "late_grads_start")
        return early["late"][4][0, 0]

    loss_local, grad_x, grads = _local_step(x[0], loss_target[0], fw, rep, mixer_weights, early_grads, late_grads)

    grads["conv_b"] = grads["conv_b"].reshape(N_DEV, FF_SLAB)[:, :FF_SHARD]
    small = _pack([grads[n].reshape(-1) for n in REPLICATED] + [loss_local.reshape(1)], SMALL_ROWS)
    received_small = _grad_exchange([], small, "grad_exchange")[0]
    my = 4 * lax.axis_index("x") + 2 * lax.axis_index("y") + lax.axis_index("c")
    received = {}
    for names, key, tag in ((EARLY_GRADS, "started", "early_grads_wait"), (LATE_GRADS, "late", "late_grads_wait")):
        srcs, lands = _exchange_wait(early[key], True, received_small, tag)
        for n, land, src in zip(names, lands, srcs):
            received[n] = _own_slot(land, lax.dynamic_index_in_dim(src, my, 0, keepdims=False))
    results = [{}, {}, {}, {}]
    for n in BUF_ORDER:
        parts = received[n]
        if n == "conv_w":
            args = [_pad_to(t[n], BUF_SHAPES[n]) for t in (w, m, v)]
        else:
            args = [w[n], m[n], v[n]]
        outs = _adamw(parts, *args, f"adamw_{n}")
        for t in range(4):
            results[t][n] = outs[t][:w[n].shape[0], :w[n].shape[1]] if n == "conv_w" else outs[t]
    pk = lambda d: _pack([d[n].reshape(-1) for n in REPLICATED] + [jnp.zeros((1,), F32)], SMALL_ROWS)
    small_out = _adamw(received_small, pk(w), pk(m), pk(v), "adamw_replicated")
    rep_shapes["loss"] = (1,)
    for t in range(4):
        results[t].update(_unpack(small_out[t], rep_shapes))

    loss = results[0]["loss"][0]
    outs = [loss, grad_x[None]]
    for res in results:
        outs += [res[n][None] for n in WEIGHTS]
    return tuple(outs)
```

```python
import functools
import math

import numpy as np
import jax
import jax.numpy as jnp
from jax import lax
from jax.experimental import pallas as pl
from jax.experimental.pallas import tpu as pltpu

F32 = jnp.float32
BF16 = jnp.bfloat16

D_MODEL = 1024
N_DEV = 8
WIDTH_A = 512
N_HEADS = 8
Q_LORA = 384
KV_LORA = 256
QK_ROPE = 32
QK_NOPE = 64
D_FF = 2816
FF_SHARD = 2 * D_FF // N_DEV
FF_SLAB = 768
DIL_CONFIGS = ((128, 1), (512, 4), (2048, 16))
BAND_HALF = 64
ROPE_BASE = 10000.0
EPS = 1e-6
NEG = -1e30
MLA_SCALE = (QK_NOPE + QK_ROPE) ** -0.5
SCALE_A = 0.125
LOG2E = 1.0 / math.log(2.0)
LN2 = math.log(2.0)
QSCALE_A = SCALE_A * LOG2E

COL_CQ = 3 * WIDTH_A
COL_CKV = COL_CQ + Q_LORA
COL_KR = COL_CKV + KV_LORA
N_PROJ = COL_KR + 128
N_LAT = N_PROJ - COL_CQ
D_IN = COL_KR + QK_ROPE

ADAM_LR = 0.001
ADAM_B1 = 0.9
ADAM_B2 = 0.999
ADAM_EPS = 1e-08
ADAM_WD = 0.01
ADAM_STEP = 10

LANES = 128
VMEM_LIMIT = 56 * 1024 * 1024

SHARDED = ("w_in", "w_uq", "w_ukv", "w_o", "w_up", "conv_w", "w_down")
REPLICATED = ("norm_mix_pre", "q_lat_norm", "kv_lat_norm", "out_norm_a", "out_norm_b", "norm_mix_post",
              "norm_ffn_pre", "conv_b", "norm_ffn_post")
WEIGHTS = ("norm_mix_pre", "w_in", "q_lat_norm", "w_uq", "kv_lat_norm", "w_ukv", "out_norm_a", "out_norm_b", "w_o",
           "norm_mix_post", "norm_ffn_pre", "w_up", "conv_w", "conv_b", "w_down", "norm_ffn_post")


def _params(sem=None):
    return pltpu.CompilerParams(dimension_semantics=sem, vmem_limit_bytes=VMEM_LIMIT)


def _dot(a, b, dims):
    return lax.dot_general(a, b, (dims, ((), ())), preferred_element_type=F32)


NN = ((1,), (0,))
NT = ((1,), (1,))
TN = ((0,), (0,))


def _rstd(x):
    return lax.rsqrt(jnp.mean(x * x, axis=-1, keepdims=True) + EPS)


def _rms_bwd(dy, x, g):
    r = _rstd(x)
    z = x * r
    gy = dy * g
    dx = r * (gy - z * jnp.mean(gy * z, axis=-1, keepdims=True))
    return dx, dy * z


def _split_hi_lo(v):
    hi = v.astype(BF16)
    lo = (v - hi.astype(F32)).astype(BF16)
    return hi, lo


def _matmul(a, b, mode, out_dtype, tm, tn, tk, name):
    if mode == "nn":
        (m, k), n = a.shape, b.shape[1]
        a_spec = pl.BlockSpec((tm, tk), lambda i, j, s: (i, s))
        b_spec = pl.BlockSpec((tk, tn), lambda i, j, s: (s, j))
        dims = NN
    elif mode == "nt":
        (m, k), n = a.shape, b.shape[0]
        a_spec = pl.BlockSpec((tm, tk), lambda i, j, s: (i, s))
        b_spec = pl.BlockSpec((tn, tk), lambda i, j, s: (j, s))
        dims = NT
    else:
        (k, m), n = a.shape, b.shape[1]
        a_spec = pl.BlockSpec((tk, tm), lambda i, j, s: (s, i))
        b_spec = pl.BlockSpec((tk, tn), lambda i, j, s: (s, j))
        dims = TN
    assert m % tm == 0 and n % tn == 0 and k % tk == 0, (name, m, n, k, tm, tn, tk)
    return _matmul_core(a, b, dims, (m // tm, n // tn, k // tk), a_spec, b_spec,
                        pl.BlockSpec((tm, tn), lambda i, j, s: (i, j)), jax.ShapeDtypeStruct((m, n), out_dtype),
                        (tm, tn), name)


def _matmul_core(a, b, dims, grid, a_spec, b_spec, o_spec, out_sds, acc_shape, name):
    nk = grid[2]

    def body(a_ref, b_ref, o_ref, acc_ref):
        s = pl.program_id(2)

        @pl.when(s == 0)
        def _():
            acc_ref[...] = jnp.zeros_like(acc_ref)

        acc_ref[...] += _dot(a_ref[...].astype(BF16), b_ref[...].astype(BF16), dims)

        @pl.when(s == nk - 1)
        def _():
            o_ref[...] = acc_ref[...].astype(out_sds.dtype)

    return pl.pallas_call(
        body, name=name, grid=grid, in_specs=[a_spec, b_spec], out_specs=o_spec, out_shape=out_sds,
        scratch_shapes=[pltpu.VMEM(acc_shape, F32)],
        compiler_params=_params(("parallel", "parallel", "arbitrary")),
    )(a, b)


def _norm_matmul(x, g, w, name):
    s, k = x.shape
    tm = min(2048, s)
    nj, tn, _ = w.shape
    half = nj // 2
    w_spec = pl.BlockSpec((None, tn, k), lambda i, j: (j, 0, 0))
    o_spec = pl.BlockSpec((None, tm, tn), lambda i, j: (j // half, i, j % half))
    o_sds = jax.ShapeDtypeStruct((2, s, half * tn), BF16)

    def body(x_ref, g_ref, w_ref, o_ref, h_ref):
        @pl.when(pl.program_id(1) == 0)
        def _():
            xv = x_ref[...]
            h_ref[...] = (xv * _rstd(xv) * g_ref[...]).astype(BF16)

        o_ref[...] = _dot(h_ref[...], w_ref[...], NT).astype(BF16)

    return pl.pallas_call(
        body, name=name, grid=(s // tm, nj),
        in_specs=[pl.BlockSpec((tm, k), lambda i, j: (i, 0)),
                  pl.BlockSpec((1, k), lambda i, j: (0, 0)),
                  w_spec],
        out_specs=[o_spec, pl.BlockSpec((tm, k), lambda i, j: (i, 0))],
        out_shape=[o_sds, jax.ShapeDtypeStruct((s, k), BF16)],
        compiler_params=_params(("parallel", "arbitrary")),
    )(x, g, w)


def _band_bias(r):
    off = np.arange(256)[None, :] - BAND_HALF - np.arange(128)[:, None]
    slopes = np.exp2(-8.0 * np.arange(1, N_HEADS + 1, dtype=np.float32) / N_HEADS).astype(np.float32)
    dist = (np.abs(off) * r).astype(np.float32)
    bias = -slopes[:, None, None] * dist[None]
    bias = np.where((np.abs(off) <= BAND_HALF)[None], bias * np.float32(LOG2E), np.float32(NEG))
    return jnp.asarray(bias, F32)


def _band_call(mode, r, center, window, bias, name):
    seq = center[0][0].shape[0]
    tq = min(512, seq)
    nsub = tq // 128
    hb = tq // BAND_HALF
    nh = seq // BAND_HALF
    nc, nw = len(center), len(window)
    out_dtypes = {"fwd": (BF16, F32), "dq": (BF16,), "dkv": (BF16, BF16)}[mode]
    n_out = len(out_dtypes)

    def specs(col):
        return (pl.BlockSpec((BAND_HALF, WIDTH_A), lambda c, i: (jnp.maximum(i * hb - 1, 0), col(c))),
                pl.BlockSpec((tq, WIDTH_A), lambda c, i: (i, col(c))),
                pl.BlockSpec((BAND_HALF, WIDTH_A), lambda c, i: (jnp.minimum((i + 1) * hb, nh - 1), col(c))))

    cspec = pl.BlockSpec((tq, WIDTH_A), lambda c, i: (i, c))
    in_specs = [specs(col)[1] for _, col in center]
    operands = [a for a, _ in center]
    for a, col in window:
        in_specs += list(specs(col))
        operands += [a, a, a]
    in_specs.append(pl.BlockSpec((N_HEADS, 128, 256), lambda c, i: (0, 0, 0)))
    operands.append(bias)
    window = [a for a, _ in window]

    def aug_stat(base, stat_sw, lane, act, e0):
        hi, lo = _split_hi_lo(stat_sw)
        return jnp.where(act, base, jnp.where(lane == e0, -hi, jnp.where(lane == e0 + 1, -lo, jnp.zeros_like(hi))))

    def aug_ones(base, lane, e0):
        return jnp.where((lane == e0) | (lane == e0 + 1), jnp.ones_like(base), base)

    def body(*refs):
        c_refs = refs[:nc]
        w_refs = refs[nc:nc + 3 * nw]
        bias_ref = refs[nc + 3 * nw]
        o_refs = refs[nc + 3 * nw + 1:nc + 3 * nw + 1 + n_out]
        wins = refs[nc + 3 * nw + 1 + n_out:]
        i = pl.program_id(1)
        for t in range(nw):
            wins[t][0:BAND_HALF, :] = w_refs[3 * t][...]
            wins[t][BAND_HALF:BAND_HALF + tq, :] = w_refs[3 * t + 1][...]
            wins[t][BAND_HALF + tq:BAND_HALF + tq + BAND_HALF, :] = w_refs[3 * t + 2][...]

        def sub(j, carry):
            r0 = pl.multiple_of(j * 128, 128)
            wpos = i * tq + j * 128 - BAND_HALF + lax.broadcasted_iota(jnp.int32, (128, 256), 1)
            valid = (wpos >= 0) & (wpos < seq)
            lane_c = lax.broadcasted_iota(jnp.int32, (128, 128), 1)
            lane_w = lax.broadcasted_iota(jnp.int32, (256, 128), 1)
            heads = [(p, a) for p in range(4) for a in range(2)]
            first, last_ops = [], []
            for p, a in heads:
                cols = slice(p * 128, (p + 1) * 128)
                cs = [c[pl.ds(r0, 128), cols] for c in c_refs]
                ws = [w[pl.ds(r0, 256), cols] for w in wins]
                e0 = 64 if a == 0 else 0
                act_c = (lane_c < 64) if a == 0 else (lane_c >= 64)
                act_w = (lane_w < 64) if a == 0 else (lane_w >= 64)
                bias_a = bias_ref[2 * p + a]
                if mode == "fwd":
                    qa = jnp.where(act_c, cs[0] * QSCALE_A, jnp.zeros_like(cs[0]))
                    first.append((_dot(qa, ws[0], NT) + bias_a, None))
                    last_ops.append((ws[1],))
                elif mode == "dq":
                    q2, dy2, l2, d2 = cs
                    k2, v2 = ws
                    q_aug = aug_stat(q2 * QSCALE_A, pltpu.roll(l2, 64, 1), lane_c, act_c, e0)
                    dy_aug = aug_stat(dy2, pltpu.roll(d2, 64, 1), lane_c, act_c, e0)
                    first.append((_dot(q_aug, aug_ones(k2, lane_w, e0), NT) + bias_a,
                                  _dot(dy_aug, aug_ones(v2, lane_w, e0), NT)))
                    last_ops.append((k2,))
                else:
                    k2, v2 = cs
                    q2, dy2, l2, d2 = ws
                    q_aug = aug_stat(q2 * QSCALE_A, pltpu.roll(l2, 64, 1), lane_w, act_w, e0)
                    dy_aug = aug_stat(dy2, pltpu.roll(d2, 64, 1), lane_w, act_w, e0)
                    first.append((_dot(aug_ones(k2, lane_c, e0), q_aug, NT) + bias_a,
                                  _dot(aug_ones(v2, lane_c, e0), dy_aug, NT)))
                    last_ops.append((q_aug, dy_aug))
            mid = []
            for sc, dp in first:
                sc = jnp.where(valid, sc, NEG)
                if mode == "fwd":
                    m = jnp.max(sc, axis=-1, keepdims=True)
                    e = jnp.exp2(sc - m)
                    l = jnp.sum(e, axis=-1, keepdims=True)
                    mid.append((e.astype(BF16), l, m + jnp.log(l) * LOG2E))
                else:
                    pr = jnp.exp2(sc)
                    mid.append((pr.astype(BF16), (pr * dp).astype(BF16)))
            res = []
            for md, ops in zip(mid, last_ops):
                if mode == "fwd":
                    res.append((_dot(md[0], ops[0], NN) / md[1], jnp.broadcast_to(md[2], (128, 128))))
                elif mode == "dq":
                    res.append((_dot(md[1], ops[0], NN) * SCALE_A,))
                else:
                    res.append((_dot(md[1], ops[0], NN) * LN2, _dot(md[0], ops[1], NN)))
            for t in range(n_out):
                pairs = [jnp.where(lane_c < 64, res[2 * p][t], res[2 * p + 1][t]) for p in range(4)]
                o_refs[t][pl.ds(r0, 128), :] = jnp.concatenate(pairs, axis=1).astype(out_dtypes[t])
            return carry

        lax.fori_loop(0, nsub, sub, 0)

    outs = pl.pallas_call(
        body, name=name, grid=(r, seq // tq),
        in_specs=in_specs,
        out_specs=[cspec] * n_out,
        out_shape=[jax.ShapeDtypeStruct((seq, r * WIDTH_A), dt) for dt in out_dtypes],
        scratch_shapes=[pltpu.VMEM((tq + 2 * BAND_HALF, WIDTH_A), w.dtype) for w in window],
        compiler_params=_params(("parallel", "parallel")),
    )(*operands)
    return outs


def _slab_scratch(tm, w):
    return pltpu.VMEM((w // 128, tm, 128), F32)


def _put(scr, val):
    for j in range(scr.shape[0]):
        scr[j] = val[:, j * 128:(j + 1) * 128].astype(F32)


def _get(scr):
    return jnp.concatenate([scr[j] for j in range(scr.shape[0])], axis=1)


def _dilate_store(dst_ref, scr, r):
    nb, tm, _ = scr.shape
    w = nb * 128
    for c in range(r):
        for j in range(nb):
            dst_ref[:, c * w + j * 128:c * w + (j + 1) * 128] = scr[j, pl.ds(c, tm // r, stride=r), :].astype(dst_ref.dtype)


def _undilate(scr, src_ref, r):
    nb, tm, _ = scr.shape
    w = nb * 128
    for c in range(r):
        for j in range(nb):
            scr[j, pl.ds(c, tm // r, stride=r), :] = src_ref[:, c * w + j * 128:c * w + (j + 1) * 128].astype(F32)


def _dil_spec(tm, r, w):
    return pl.BlockSpec((tm // r, r * w), lambda i: (i, 0))


def _dil_shape(s, r, w, dtype):
    return jax.ShapeDtypeStruct((s // r, r * w), dtype)


def _in_proj(x, g, w, name):
    s, k = x.shape
    n = w.shape[0]
    tm = min(512, s)
    qkv = 3 * WIDTH_A

    def body(x_ref, g_ref, w_ref, o_ref, h_ref, d4_ref, d16_ref, scr):
        xv = x_ref[...]
        h = (xv * _rstd(xv) * g_ref[...]).astype(BF16)
        h_ref[...] = h
        acc = _dot(h, w_ref[...], NT)
        o_ref[...] = acc.astype(BF16)
        _put(scr, acc[:, 0:qkv])
        _dilate_store(d4_ref, scr, 4)
        _dilate_store(d16_ref, scr, 16)

    row = lambda c: pl.BlockSpec((tm, c), lambda i: (i, 0))
    return pl.pallas_call(
        body, name=name, grid=(s // tm,),
        in_specs=[row(k), pl.BlockSpec((1, k), lambda i: (0, 0)), pl.BlockSpec((n, k), lambda i: (0, 0))],
        out_specs=[row(n), row(k), _dil_spec(tm, 4, qkv), _dil_spec(tm, 16, qkv)],
        out_shape=[jax.ShapeDtypeStruct((s, n), BF16), jax.ShapeDtypeStruct((s, k), BF16),
                   _dil_shape(s, 4, qkv, BF16), _dil_shape(s, 16, qkv, BF16)],
        scratch_shapes=[_slab_scratch(tm, qkv)],
        compiler_params=_params(("parallel",)),
    )(x, g, w)


def _band_combine(os_, lses, name):
    s = os_[0].shape[0]
    tm = min(512, s)

    def body(o1, o4, o16, l1, l4, l16, ya_ref, lse_ref, lse4_ref, lse16_ref, so4, sl4, so16, sl16):
        _undilate(so4, o4, 4)
        _undilate(sl4, l4, 4)
        _undilate(so16, o16, 16)
        _undilate(sl16, l16, 16)
        a0, a1, a2 = l1[...], _get(sl4), _get(sl16)
        m = jnp.maximum(jnp.maximum(a0, a1), a2)
        e0, e1, e2 = jnp.exp2(a0 - m), jnp.exp2(a1 - m), jnp.exp2(a2 - m)
        den = e0 + e1 + e2
        ya_ref[...] = (e0 * o1[...] + e1 * _get(so4) + e2 * _get(so16)) / den
        lse = m + jnp.log(den) * LOG2E
        lse_ref[...] = lse
        _put(sl4, lse)
        _dilate_store(lse4_ref, sl4, 4)
        _dilate_store(lse16_ref, sl4, 16)

    nat = pl.BlockSpec((tm, WIDTH_A), lambda i: (i, 0))
    d4, d16 = _dil_spec(tm, 4, WIDTH_A), _dil_spec(tm, 16, WIDTH_A)
    return pl.pallas_call(
        body, name=name, grid=(s // tm,), in_specs=[nat, d4, d16] * 2, out_specs=[nat, nat, d4, d16],
        out_shape=[jax.ShapeDtypeStruct((s, WIDTH_A), F32)] * 2
        + [_dil_shape(s, 4, WIDTH_A, F32), _dil_shape(s, 16, WIDTH_A, F32)],
        scratch_shapes=[_slab_scratch(tm, WIDTH_A)] * 4,
        compiler_params=_params(("parallel",)),
    )(*os_, *lses)


def _rope_tables(s):
    pos = jnp.arange(s, dtype=F32)
    inv_freq = jnp.exp(-math.log(ROPE_BASE) * jnp.arange(0, QK_ROPE, 2, dtype=F32) / QK_ROPE)
    ang = pos[:, None] * inv_freq[None, :]
    cos, sin = jnp.cos(ang), jnp.sin(ang)
    one = jnp.ones((s, 64), F32)
    zero16 = jnp.zeros((s, 16), F32)
    c = jnp.concatenate([one, cos, cos, jnp.ones((s, 32), F32)], axis=1)
    sa = jnp.concatenate([jnp.zeros((s, 64), F32), -sin, zero16, jnp.zeros((s, 32), F32)], axis=1)
    sb = jnp.concatenate([jnp.zeros((s, 64), F32), zero16, sin, jnp.zeros((s, 32), F32)], axis=1)
    return c, sa, sb


def _rope_fwd(x, c, sa, sb):
    return x * c + pltpu.roll(x, 112, 1) * sa + pltpu.roll(x, 16, 1) * sb


def _rope_bwd(dy, c, sa, sb):
    return dy * c + pltpu.roll(dy * sa, 16, 1) + pltpu.roll(dy * sb, 112, 1)


def _mla_prep(proj, g_q, g_kv, w_uq, w_ukv, tabs, name):
    s = proj.shape[0]
    tm = min(512, s)
    width = N_HEADS * 128

    def body(lat_ref, gq_ref, gkv_ref, wq_ref, wkv_ref, c_ref, sa_ref, sb_ref,
             q_ref, k_ref, kv_ref, cqn_ref, ckvn_ref):
        c, sa, sb = c_ref[...], sa_ref[...], sb_ref[...]
        cq = lat_ref[:, 0:Q_LORA].astype(F32)
        cqn = (cq * _rstd(cq) * gq_ref[...]).astype(BF16)
        cqn_ref[...] = cqn
        q = _dot(cqn, wq_ref[...], NT)
        ckv = lat_ref[:, Q_LORA:Q_LORA + KV_LORA].astype(F32)
        ckvn = (ckv * _rstd(ckv) * gkv_ref[...]).astype(BF16)
        ckvn_ref[...] = ckvn
        kv = _dot(ckvn, wkv_ref[...], NN)
        lane = lax.broadcasted_iota(jnp.int32, (tm, 128), 1)
        krr = _rope_fwd(lat_ref[:, Q_LORA + KV_LORA:].astype(F32), c, sa, sb)
        krr = jnp.where((lane == 96) | (lane == 97), 1.0, krr)
        ones01 = jnp.where(lane < 2, 1.0, 0.0)
        for h in range(N_HEADS):
            cols = slice(h * 128, (h + 1) * 128)
            q_ref[:, cols] = (_rope_fwd(q[:, cols], c, sa, sb) * (MLA_SCALE * LOG2E)).astype(BF16)
            k_ref[:, cols] = jnp.where(lane < 64, kv[:, cols], krr).astype(BF16)
            kv_ref[:, cols] = jnp.where(lane < 64, ones01, kv[:, cols]).astype(BF16)

    row = lambda n: pl.BlockSpec((tm, n), lambda i: (i, 0))
    full = lambda a: pl.BlockSpec(a.shape, lambda i: (0, 0))
    tab = pl.BlockSpec((tm, 128), lambda i: (i, 0))
    return pl.pallas_call(
        body, name=name, grid=(s // tm,),
        in_specs=[pl.BlockSpec((tm, N_LAT), lambda i: (i, COL_CQ // N_LAT)),
                  full(g_q), full(g_kv), full(w_uq), full(w_ukv), tab, tab, tab],
        out_specs=[row(width), row(width), row(width), row(Q_LORA), row(KV_LORA)],
        out_shape=[jax.ShapeDtypeStruct((s, width), BF16)] * 3
        + [jax.ShapeDtypeStruct((s, Q_LORA), BF16), jax.ShapeDtypeStruct((s, KV_LORA), BF16)],
        compiler_params=_params(("parallel",)),
    )(proj, g_q, g_kv, w_uq, w_ukv, *tabs)


def _mla_fwd(qcat, kcat, kvb, name):
    s = qcat.shape[0]
    tq = min(1024, s)
    tk = min(1024, s)
    nkc = s // tk

    def body(q_ref, k_ref, v_ref, yb_ref, qaug_ref, m_ref, acc_ref):
        lane = lax.broadcasted_iota(jnp.int32, (tq, 128), 1)
        m_ref[...] = jnp.full((2, tq, 128), NEG, F32)
        acc_ref[...] = jnp.zeros((2, tq, 128), F32)

        def chunk(cidx, carry):
            k0 = pl.multiple_of(cidx * tk, tk)
            cols = [slice(a * 128, (a + 1) * 128) for a in range(2)]
            scs = [_dot(q_ref[:, c], k_ref[pl.ds(k0, tk), c], NT) for c in cols]
            prs, alphas = [], []
            for a, sc in enumerate(scs):
                m_prev = m_ref[a]
                m_new = jnp.maximum(m_prev, jnp.max(sc, axis=-1, keepdims=True))
                alphas.append(jnp.exp2(m_prev - m_new))
                prs.append(jnp.exp2(sc - jnp.tile(m_new, (1, tk // 128))).astype(BF16))
                m_ref[a] = m_new
            for a, c in enumerate(cols):
                acc_ref[a] = alphas[a] * acc_ref[a] + _dot(prs[a], v_ref[pl.ds(k0, tk), c], NN)
            return carry

        lax.fori_loop(0, nkc, chunk, 0)
        outs = []
        for a in range(2):
            cols = slice(a * 128, (a + 1) * 128)
            acc = acc_ref[a]
            l = acc[:, 0:1]
            outs.append(acc / l)
            hi, lo = _split_hi_lo(m_ref[a] + jnp.log(l) * LOG2E)
            qaug_ref[:, cols] = jnp.where(lane == 96, -hi, jnp.where(lane == 97, -lo, q_ref[:, cols]))
        yb_ref[...] = jnp.where(lane < 64, pltpu.roll(outs[0], 64, 1), outs[1])

    return pl.pallas_call(
        body, name=name, grid=(4, s // tq),
        in_specs=[pl.BlockSpec((tq, 256), lambda p, i: (i, p)),
                  pl.BlockSpec((s, 256), lambda p, i: (0, p)),
                  pl.BlockSpec((s, 256), lambda p, i: (0, p))],
        out_specs=[pl.BlockSpec((tq, 128), lambda p, i: (i, p)),
                   pl.BlockSpec((tq, 256), lambda p, i: (i, p))],
        out_shape=[jax.ShapeDtypeStruct((s, WIDTH_A), F32), jax.ShapeDtypeStruct((s, N_HEADS * 128), BF16)],
        scratch_shapes=[pltpu.VMEM((2, tq, 128), F32)] * 2,
        compiler_params=_params(("parallel", "parallel")),
    )(qcat, kcat, kvb)


def _mla_bwd(qaug, kcat, kvb, doaug, name):
    s = qaug.shape[0]
    tq = min(1024, s)
    tk = min(512, s)
    nqc = s // tq
    width = N_HEADS * 128

    def body(q_ref, do_ref, k_ref, v_ref, dq_ref, dk_acc, dv_acc):
        j = pl.program_id(1)

        @pl.when(j == 0)
        def _():
            dq_ref[...] = jnp.zeros_like(dq_ref)

        dk_acc[...] = jnp.zeros_like(dk_acc)
        dv_acc[...] = jnp.zeros_like(dv_acc)

        def chunk(cidx, carry):
            q0 = pl.multiple_of(cidx * tq, tq)
            cols = [slice(a * 128, (a + 1) * 128) for a in range(2)]
            qs = [q_ref[pl.ds(q0, tq), c] for c in cols]
            dos = [do_ref[pl.ds(q0, tq), c] for c in cols]
            kbs = [k_ref[:, c] for c in cols]
            sts = [_dot(kbs[a], qs[a], NT) for a in range(2)]
            dps = [_dot(v_ref[:, cols[a]], dos[a], NT) for a in range(2)]
            pts, dsts = [], []
            for a in range(2):
                pt = jnp.exp2(sts[a])
                pts.append(pt.astype(BF16))
                dsts.append((pt * dps[a]).astype(BF16))
            for a, c in enumerate(cols):
                dv_acc[:, c] += _dot(pts[a], dos[a], NN)
                dk_acc[:, c] += _dot(dsts[a], qs[a], NN)
                dq_ref[pl.ds(q0, tq), c] += _dot(dsts[a], kbs[a], TN)
            return carry

        lax.fori_loop(0, nqc, chunk, 0)

    return pl.pallas_call(
        body, name=name, grid=(N_HEADS // 2, s // tk),
        in_specs=[pl.BlockSpec((s, 256), lambda p, j: (0, p)),
                  pl.BlockSpec((s, 256), lambda p, j: (0, p)),
                  pl.BlockSpec((tk, 256), lambda p, j: (j, p)),
                  pl.BlockSpec((tk, 256), lambda p, j: (j, p))],
        out_specs=[pl.BlockSpec((s, 256), lambda p, j: (0, p)),
                   pl.BlockSpec((tk, 256), lambda p, j: (j, p)),
                   pl.BlockSpec((tk, 256), lambda p, j: (j, p))],
        out_shape=[jax.ShapeDtypeStruct((s, width), F32)] * 3,
        compiler_params=_params(("parallel", "arbitrary")),
    )(qaug, doaug, kcat, kvb)


def _mla_prep_bwd(dqc, dkc, dvp, proj, cqn, ckvn, g_q, g_kv, w_uq, w_ukv, tabs, name):
    s = proj.shape[0]
    tm = min(256, s)
    width = N_HEADS * 128
    n_out_cols = N_LAT

    def body(dq_ref, dk_ref, dv_ref, lat_ref, cqn_ref, ckvn_ref, gq_ref, gkv_ref, wq_ref, wkv_ref,
             c_ref, sa_ref, sb_ref, dproj_ref, dwq_ref, dwkv_ref, dgq_ref, dgkv_ref):
        i = pl.program_id(0)

        @pl.when(i == 0)
        def _():
            dwq_ref[...] = jnp.zeros_like(dwq_ref)
            dwkv_ref[...] = jnp.zeros_like(dwkv_ref)
            dgq_ref[...] = jnp.zeros_like(dgq_ref)
            dgkv_ref[...] = jnp.zeros_like(dgkv_ref)

        c, sa, sb = c_ref[...], sa_ref[...], sb_ref[...]
        lane = lax.broadcasted_iota(jnp.int32, (tm, 128), 1)
        dkr = jnp.zeros((tm, 128), F32)
        dq_parts, dkv_parts = [], []
        for h in range(N_HEADS):
            cols = slice(h * 128, (h + 1) * 128)
            dq_parts.append(_rope_bwd(dq_ref[:, cols] * MLA_SCALE, c, sa, sb).astype(BF16))
            dkh = dk_ref[:, cols] * LN2
            dkr = dkr + dkh
            dkv_parts.append(jnp.where(lane < 64, dkh, dv_ref[:, cols]).astype(BF16))
        dq = jnp.concatenate(dq_parts, axis=1)
        dkv = jnp.concatenate(dkv_parts, axis=1)
        dkr = _rope_bwd(jnp.where((lane >= 64) & (lane < 96), dkr, 0.0), c, sa, sb)

        dcqn = _dot(dq, wq_ref[...], NN)
        dwq_ref[...] += _dot(dq, cqn_ref[...], TN)
        dcq, dgq = _rms_bwd(dcqn, lat_ref[:, 0:Q_LORA].astype(F32), gq_ref[...])
        dgq_ref[...] += jnp.sum(dgq, axis=0, keepdims=True)

        dckvn = _dot(dkv, wkv_ref[...], NT)
        dwkv_ref[...] += _dot(ckvn_ref[...], dkv, TN)
        dckv, dgkv = _rms_bwd(dckvn, lat_ref[:, Q_LORA:Q_LORA + KV_LORA].astype(F32), gkv_ref[...])
        dgkv_ref[...] += jnp.sum(dgkv, axis=0, keepdims=True)

        dproj_ref[:, 0:Q_LORA] = dcq.astype(BF16)
        dproj_ref[:, Q_LORA:Q_LORA + KV_LORA] = dckv.astype(BF16)
        dproj_ref[:, Q_LORA + KV_LORA:] = dkr.astype(BF16)

    row = lambda n: pl.BlockSpec((tm, n), lambda i: (i, 0))
    full = lambda a: pl.BlockSpec(a.shape, lambda i: (0, 0))
    tab = pl.BlockSpec((tm, 128), lambda i: (i, 0))
    return pl.pallas_call(
        body, name=name, grid=(s // tm,),
        in_specs=[row(width), row(width), row(width),
                  pl.BlockSpec((tm, N_LAT), lambda i: (i, COL_CQ // N_LAT)),
                  row(Q_LORA), row(KV_LORA), full(g_q), full(g_kv), full(w_uq), full(w_ukv), tab, tab, tab],
        out_specs=[row(n_out_cols), full(w_uq), full(w_ukv), full(g_q), full(g_kv)],
        out_shape=[jax.ShapeDtypeStruct((s, n_out_cols), BF16),
                   jax.ShapeDtypeStruct(w_uq.shape, F32), jax.ShapeDtypeStruct(w_ukv.shape, F32),
                   jax.ShapeDtypeStruct(g_q.shape, F32), jax.ShapeDtypeStruct(g_kv.shape, F32)],
        compiler_params=_params(("arbitrary",)),
    )(dqc, dkc, dvp, proj, cqn, ckvn, g_q, g_kv, w_uq, w_ukv, *tabs)


def _mix_out(ya, yb, na, nb, w_o, g_post, x, name):
    s = x.shape[0]
    tm = min(512, s)

    def body(ya_ref, yb_ref, na_ref, nb_ref, w_ref, g_ref, x_ref, yn_ref, y2_ref, x1_ref):
        a, b = ya_ref[...], yb_ref[...]
        yn = jnp.concatenate([a * _rstd(a) * na_ref[...], b * _rstd(b) * nb_ref[...]], axis=1).astype(BF16)
        yn_ref[...] = yn
        y2 = _dot(yn, w_ref[...], NN)
        y2_ref[...] = y2
        x1_ref[...] = x_ref[...] + y2 * _rstd(y2) * g_ref[...]

    row = lambda n: pl.BlockSpec((tm, n), lambda i: (i, 0))
    full = lambda a: pl.BlockSpec(a.shape, lambda i: (0, 0))
    return pl.pallas_call(
        body, name=name, grid=(s // tm,),
        in_specs=[row(WIDTH_A), row(WIDTH_A), full(na), full(nb), full(w_o), full(g_post), row(D_MODEL)],
        out_specs=[row(D_MODEL)] * 3,
        out_shape=[jax.ShapeDtypeStruct((s, D_MODEL), BF16), jax.ShapeDtypeStruct((s, D_MODEL), F32),
                   jax.ShapeDtypeStruct((s, D_MODEL), F32)],
        compiler_params=_params(("parallel",)),
    )(ya, yb, na, nb, w_o, g_post, x)


def _head_ones():
    blk = np.kron(np.eye(N_HEADS, dtype=np.float32), np.ones((64, 64), np.float32))
    return jnp.asarray(blk, F32)


def _outnorm_bwd(dyn, ya, yb, na, nb, ones, name):
    s = ya.shape[0]
    tm = min(256, s)

    def body(dyn_ref, ya_ref, yb_ref, na_ref, nb_ref, ones_ref, dya_ref, da_ref, do_ref, dna_ref, dnb_ref,
             dya4_ref, dya16_ref, da4_ref, da16_ref, scr):
        i = pl.program_id(0)

        @pl.when(i == 0)
        def _():
            dna_ref[...] = jnp.zeros_like(dna_ref)
            dnb_ref[...] = jnp.zeros_like(dnb_ref)

        a, b = ya_ref[...], yb_ref[...]
        dya, dna = _rms_bwd(dyn_ref[:, 0:WIDTH_A], a, na_ref[...])
        dyb, dnb = _rms_bwd(dyn_ref[:, WIDTH_A:], b, nb_ref[...])
        dna_ref[...] += jnp.sum(dna, axis=0, keepdims=True)
        dnb_ref[...] += jnp.sum(dnb, axis=0, keepdims=True)
        dya_b = dya.astype(BF16)
        dya_ref[...] = dya_b
        hp = lax.Precision.HIGHEST
        delta_a = jnp.dot(dya_b.astype(F32) * a, ones_ref[...], precision=hp, preferred_element_type=F32)
        da_ref[...] = delta_a
        _put(scr, dya_b)
        _dilate_store(dya4_ref, scr, 4)
        _dilate_store(dya16_ref, scr, 16)
        _put(scr, delta_a)
        _dilate_store(da4_ref, scr, 4)
        _dilate_store(da16_ref, scr, 16)
        dyb_b = dyb.astype(BF16)
        db = jnp.dot(dyb_b.astype(F32) * b, ones_ref[...], precision=hp, preferred_element_type=F32)
        lane = lax.broadcasted_iota(jnp.int32, (tm, 128), 1)
        zero = jnp.zeros((tm, 128), BF16)
        for p in range(4):
            cols = slice(p * 128, (p + 1) * 128)
            dyp = dyb_b[:, cols]
            dbp = db[:, cols]
            for a_ in range(2):
                src = pltpu.roll(dyp.astype(F32), 64, 1).astype(BF16) if a_ == 0 else dyp
                dlt = dbp if a_ == 0 else pltpu.roll(dbp, 64, 1)
                hi, lo = _split_hi_lo(dlt)
                blk = jnp.where(lane >= 64, src, jnp.where(lane == 0, -hi, jnp.where(lane == 1, -lo, zero)))
                h = 2 * p + a_
                do_ref[:, h * 128:(h + 1) * 128] = blk

    row = lambda n: pl.BlockSpec((tm, n), lambda i: (i, 0))
    full = lambda a: pl.BlockSpec(a.shape, lambda i: (0, 0))
    return pl.pallas_call(
        body, name=name, grid=(s // tm,),
        in_specs=[row(D_MODEL), row(WIDTH_A), row(WIDTH_A), full(na), full(nb), full(ones)],
        out_specs=[row(WIDTH_A), row(WIDTH_A), row(N_HEADS * 128), full(na), full(nb),
                   _dil_spec(tm, 4, WIDTH_A), _dil_spec(tm, 16, WIDTH_A), _dil_spec(tm, 4, WIDTH_A), _dil_spec(tm, 16, WIDTH_A)],
        out_shape=[jax.ShapeDtypeStruct((s, WIDTH_A), BF16), jax.ShapeDtypeStruct((s, WIDTH_A), F32),
                   jax.ShapeDtypeStruct((s, N_HEADS * 128), BF16),
                   jax.ShapeDtypeStruct(na.shape, F32), jax.ShapeDtypeStruct(nb.shape, F32),
                   _dil_shape(s, 4, WIDTH_A, BF16), _dil_shape(s, 16, WIDTH_A, BF16),
                   _dil_shape(s, 4, WIDTH_A, F32), _dil_shape(s, 16, WIDTH_A, F32)],
        scratch_shapes=[_slab_scratch(tm, WIDTH_A)],
        compiler_params=_params(("arbitrary",)),
    )(dyn, ya, yb, na, nb, ones)


def _sum_cast(parts, name):
    s = parts[0][0].shape[0]
    tm = min(512, s)

    def body(*refs):
        o_ref, s4, s16 = refs[9:]
        for t in range(3):
            _undilate(s4, refs[3 + t], 4)
            _undilate(s16, refs[6 + t], 16)
            acc = refs[t][...] + _get(s4) + _get(s16)
            o_ref[:, t * WIDTH_A:(t + 1) * WIDTH_A] = acc.astype(BF16)

    nat = pl.BlockSpec((tm, WIDTH_A), lambda i: (i, 0))
    flat = [parts[g][t] for g in range(3) for t in range(3)]
    return pl.pallas_call(
        body, name=name, grid=(s // tm,),
        in_specs=[nat] * 3 + [_dil_spec(tm, 4, WIDTH_A)] * 3 + [_dil_spec(tm, 16, WIDTH_A)] * 3,
        out_specs=pl.BlockSpec((tm, 3 * WIDTH_A), lambda i: (i, 0)),
        out_shape=jax.ShapeDtypeStruct((s, 3 * WIDTH_A), BF16),
        scratch_shapes=[_slab_scratch(tm, WIDTH_A)] * 2,
        compiler_params=_params(("parallel",)),
    )(*flat)


HALO = 16


def _gelu(x):
    k = math.sqrt(2.0 / math.pi)
    t = jnp.tanh(k * (x + 0.044715 * x * x * x))
    return 0.5 * x * (1.0 + t), t


def _gelu_grad(x, t):
    k = math.sqrt(2.0 / math.pi)
    return 0.5 * (1.0 + t) + 0.5 * x * (1.0 - t * t) * k * (1.0 + 3 * 0.044715 * x * x)


def _halo_specs(s, tm, tn, lead):
    nb = s // HALO
    hb = tm // HALO
    pre = (lead,) if lead else ()
    z = (0,) if lead else ()
    main = pl.BlockSpec(pre + (tm, tn), lambda j, i: z + (i, j))
    prev = pl.BlockSpec(pre + (HALO, tn), lambda j, i: z + (jnp.maximum(i * hb - 1, 0), j))
    nxt = pl.BlockSpec(pre + (HALO, tn), lambda j, i: z + (jnp.minimum((i + 1) * hb, nb - 1), j))
    return [prev, main, nxt]


def _fill_ext(ext, prev, main, nxt, i, tm, s):
    ext[0:HALO, :] = jnp.where(i > 0, prev.astype(F32), 0.0)
    ext[HALO:HALO + tm, :] = main.astype(F32)
    ext[HALO + tm:2 * HALO + tm, :] = jnp.where((i + 1) * tm < s, nxt.astype(F32), 0.0)


STRIP = 16


def _shifted(ref, row0):
    n = STRIP + 16
    win = ref[pl.ds(pl.multiple_of(row0 - 8, 8), n), :]
    return pltpu.roll(win, 1, 0)[8:8 + STRIP], win[8:8 + STRIP], pltpu.roll(win, n - 1, 0)[8:8 + STRIP]


def _conv3(e, row0, w_ref, b_ref, t):
    m1, c0, p1 = _shifted(e, row0)
    return w_ref[t, 0:1, :] * m1 + w_ref[t, 1:2, :] * c0 + w_ref[t, 2:3, :] * p1 + b_ref[t]


def _conv_gate(up, cw, cb, name):
    _, s, c = up.shape
    tm = min(512, s)
    tn = FF_SLAB

    def body(up_p, up_m, up_n, w_ref, b_ref, a_ref, eg, ev):
        i = pl.program_id(1)
        _fill_ext(eg, up_p[0], up_m[0], up_n[0], i, tm, s)
        _fill_ext(ev, up_p[1], up_m[1], up_n[1], i, tm, s)

        def strip(t, carry):
            r0 = pl.multiple_of(t * STRIP, STRIP)
            g, _ = _gelu(_conv3(eg, HALO + r0, w_ref, b_ref, 0))
            a_ref[pl.ds(r0, STRIP), :] = (g * _conv3(ev, HALO + r0, w_ref, b_ref, 1)).astype(BF16)
            return carry

        lax.fori_loop(0, tm // STRIP, strip, 0)

    return pl.pallas_call(
        body, name=name, grid=(c // tn, s // tm),
        in_specs=_halo_specs(s, tm, tn, 2)
        + [pl.BlockSpec((2, 3, tn), lambda j, i: (0, 0, j)), pl.BlockSpec((2, 1, tn), lambda j, i: (0, 0, j))],
        out_specs=pl.BlockSpec((tm, tn), lambda j, i: (i, j)),
        out_shape=jax.ShapeDtypeStruct((s, c), BF16),
        scratch_shapes=[pltpu.VMEM((tm + 2 * HALO, tn), F32)] * 2,
        compiler_params=_params(("parallel", "parallel")),
    )(up, up, up, cw, cb)


def _conv_gate_bwd(up, da, cw, cb, name):
    _, s, c = up.shape
    tm = min(256, s)
    tn = FF_SLAB
    te = tm + HALO

    def body(up_p, up_m, up_n, da_p, da_m, da_n, w_ref, b_ref, dup_ref, dw_ref, db_ref, eg, ev, ed, dug, duv):
        i = pl.program_id(1)

        @pl.when(i == 0)
        def _():
            dw_ref[...] = jnp.zeros_like(dw_ref)
            db_ref[...] = jnp.zeros_like(db_ref)

        _fill_ext(eg, up_p[0], up_m[0], up_n[0], i, tm, s)
        _fill_ext(ev, up_p[1], up_m[1], up_n[1], i, tm, s)
        _fill_ext(ed, da_p[...], da_m[...], da_n[...], i, tm, s)
        o = HALO // 2

        def du_strip(t, carry):
            r0 = pl.multiple_of(t * STRIP, STRIP)
            ug = _conv3(eg, o + r0, w_ref, b_ref, 0)
            uv = _conv3(ev, o + r0, w_ref, b_ref, 1)
            gl, th = _gelu(ug)
            dav = ed[pl.ds(pl.multiple_of(o + r0, 8), STRIP), :]
            dug[pl.ds(r0, STRIP), :] = dav * uv * _gelu_grad(ug, th)
            duv[pl.ds(r0, STRIP), :] = dav * gl
            return carry

        lax.fori_loop(0, te // STRIP, du_strip, 0)

        def back(du, e, t):
            def strip(k, acc):
                r0 = pl.multiple_of(k * STRIP, STRIP)
                dm1, c0, dp1 = _shifted(du, o + r0)
                dup_ref[t, pl.ds(r0, STRIP), :] = (w_ref[t, 0:1, :] * dp1 + w_ref[t, 1:2, :] * c0
                                                   + w_ref[t, 2:3, :] * dm1).astype(BF16)
                um1, u0, up1 = _shifted(e, HALO + r0)
                fold = lambda a: a[0:8] + a[8:16]
                return (acc[0] + fold(um1 * c0), acc[1] + fold(u0 * c0), acc[2] + fold(up1 * c0), acc[3] + fold(c0))

            zero = jnp.zeros((8, tn), F32)
            acc = lax.fori_loop(0, tm // STRIP, strip, (zero, zero, zero, zero))
            for k in range(3):
                dw_ref[t, k:k + 1, :] += jnp.sum(acc[k], axis=0, keepdims=True)
            db_ref[t] += jnp.sum(acc[3], axis=0, keepdims=True)

        back(dug, eg, 0)
        back(duv, ev, 1)

    wspec = pl.BlockSpec((2, 3, tn), lambda j, i: (0, 0, j))
    bspec = pl.BlockSpec((2, 1, tn), lambda j, i: (0, 0, j))
    return pl.pallas_call(
        body, name=name, grid=(c // tn, s // tm),
        in_specs=_halo_specs(s, tm, tn, 2) + _halo_specs(s, tm, tn, 0) + [wspec, bspec],
        out_specs=[pl.BlockSpec((2, tm, tn), lambda j, i: (0, i, j)), wspec, bspec],
        out_shape=[jax.ShapeDtypeStruct((2, s, c), BF16), jax.ShapeDtypeStruct((2, 3, c), F32),
                   jax.ShapeDtypeStruct((2, 1, c), F32)],
        scratch_shapes=[pltpu.VMEM((tm + 2 * HALO, tn), F32)] * 3 + [pltpu.VMEM((te, tn), F32)] * 2,
        compiler_params=_params(("parallel", "arbitrary")),
    )(up, up, up, da, da, da, cw, cb)


def _ffn_out(a, w_down, g_post, x1, target, name):
    s = x1.shape[0]
    tm = min(256, s)

    def body(a_ref, w_ref, g_ref, x1_ref, t_ref, dy3_ref, dx2_ref, loss_ref, dg_ref):
        i = pl.program_id(0)

        @pl.when(i == 0)
        def _():
            loss_ref[...] = jnp.zeros_like(loss_ref)
            dg_ref[...] = jnp.zeros_like(dg_ref)

        y3 = _dot(a_ref[...], w_ref[...], NN)
        g = g_ref[...]
        x2 = x1_ref[...] + y3 * _rstd(y3) * g
        diff = x2 - t_ref[...]
        loss_ref[...] += jnp.sum(jnp.sum(diff * diff, axis=1, keepdims=True), axis=0, keepdims=True)
        dx2 = diff * (1.0 / D_MODEL)
        dx2_ref[...] = dx2
        dy3, dg = _rms_bwd(dx2, y3, g)
        dy3_ref[...] = dy3.astype(BF16)
        dg_ref[...] += jnp.sum(dg, axis=0, keepdims=True)

    row = lambda n: pl.BlockSpec((tm, n), lambda i: (i, 0))
    full = lambda t: pl.BlockSpec(t.shape, lambda i: (0, 0))
    return pl.pallas_call(
        body, name=name, grid=(s // tm,),
        in_specs=[row(a.shape[1]), full(w_down), full(g_post), row(D_MODEL), row(D_MODEL)],
        out_specs=[row(D_MODEL), row(D_MODEL), pl.BlockSpec((8, 128), lambda i: (0, 0)), full(g_post)],
        out_shape=[jax.ShapeDtypeStruct((s, D_MODEL), BF16), jax.ShapeDtypeStruct((s, D_MODEL), F32),
                   jax.ShapeDtypeStruct((8, 128), F32), jax.ShapeDtypeStruct(g_post.shape, F32)],
        compiler_params=_params(("arbitrary",)),
    )(a, w_down, g_post, x1, target)


def _resnorm_bwd(dh2, x1, g_ffn_pre, dx2, y2, g_mix_post, name):
    s = x1.shape[0]
    tm = min(256, s)

    def body(dh_ref, x1_ref, gf_ref, dx2_ref, y2_ref, gp_ref, dx1_ref, dy2_ref, dgf_ref, dgp_ref):
        i = pl.program_id(0)

        @pl.when(i == 0)
        def _():
            dgf_ref[...] = jnp.zeros_like(dgf_ref)
            dgp_ref[...] = jnp.zeros_like(dgp_ref)

        dn, dgf = _rms_bwd(dh_ref[...], x1_ref[...], gf_ref[...])
        dx1 = dx2_ref[...] + dn
        dx1_ref[...] = dx1
        dgf_ref[...] += jnp.sum(dgf, axis=0, keepdims=True)
        dy2, dgp = _rms_bwd(dx1, y2_ref[...], gp_ref[...])
        dy2_ref[...] = dy2.astype(BF16)
        dgp_ref[...] += jnp.sum(dgp, axis=0, keepdims=True)

    row = pl.BlockSpec((tm, D_MODEL), lambda i: (i, 0))
    full = pl.BlockSpec((1, D_MODEL), lambda i: (0, 0))
    return pl.pallas_call(
        body, name=name, grid=(s // tm,),
        in_specs=[row, row, full, row, row, full],
        out_specs=[row, row, full, full],
        out_shape=[jax.ShapeDtypeStruct((s, D_MODEL), F32), jax.ShapeDtypeStruct((s, D_MODEL), BF16),
                   jax.ShapeDtypeStruct((1, D_MODEL), F32), jax.ShapeDtypeStruct((1, D_MODEL), F32)],
        compiler_params=_params(("arbitrary",)),
    )(dh2, x1, g_ffn_pre, dx2, y2, g_mix_post)


def _final_bwd(dh1, x, g_pre, dx1, name):
    s = x.shape[0]
    tm = min(256, s)

    def body(dh_ref, x_ref, g_ref, dx1_ref, dx_ref, dg_ref):
        @pl.when(pl.program_id(0) == 0)
        def _():
            dg_ref[...] = jnp.zeros_like(dg_ref)

        dn, dg = _rms_bwd(dh_ref[...], x_ref[...], g_ref[...])
        dx_ref[...] = dx1_ref[...] + dn
        dg_ref[...] += jnp.sum(dg, axis=0, keepdims=True)

    row = pl.BlockSpec((tm, D_MODEL), lambda i: (i, 0))
    full = pl.BlockSpec((1, D_MODEL), lambda i: (0, 0))
    return pl.pallas_call(
        body, name=name, grid=(s // tm,),
        in_specs=[row, row, full, row], out_specs=[row, full],
        out_shape=[jax.ShapeDtypeStruct((s, D_MODEL), F32), jax.ShapeDtypeStruct((1, D_MODEL), F32)],
        compiler_params=_params(("arbitrary",)),
    )(dh1, x, g_pre, dx1)


def _local_step(x, target, fw, rep, mixer_weights, early_grads, late_grads):
    s = x.shape[0]
    tabs = _rope_tables(s)
    w_in, w_uq, w_ukv, w_o = (fw[n] for n in ("w_in", "w_uq", "w_ukv", "w_o"))
    tr = min(2048, s)
    tcon = min(2048, s)

    proj, h1, qkv4, qkv16 = _in_proj(x, rep["norm_mix_pre"], w_in, "in_proj")
    qkv = {1: proj, 4: qkv4, 16: qkv16}
    q_of = lambda r: (qkv[r], lambda c: 3 * c)
    k_of = lambda r: (qkv[r], lambda c: 3 * c + 1)
    v_of = lambda r: (qkv[r], lambda c: 3 * c + 2)
    own = lambda a: (a, lambda c: c)
    biases = [_band_bias(r) for _, r in DIL_CONFIGS]
    os_, lses = [], []
    for g, (_, r) in enumerate(DIL_CONFIGS):
        o, l = _band_call("fwd", r, [q_of(r)], [k_of(r), v_of(r)], biases[g], f"band_fwd_r{r}")
        os_.append(o)
        lses.append(l)
    ya, lse_a, lse4, lse16 = _band_combine(os_, lses, "band_combine")
    qcat, kcat, kvb, cqn, ckvn = _mla_prep(proj, rep["q_lat_norm"], rep["kv_lat_norm"], w_uq, w_ukv, tabs, "mla_prep")
    yb, qaug = _mla_fwd(qcat, kcat, kvb, "mla_fwd")
    yn, y2, x1 = _mix_out(ya, yb, rep["out_norm_a"], rep["out_norm_b"], w_o, rep["norm_mix_post"], x, "mix_out")
    mw = mixer_weights(x1)
    w_up, w_down, cw, cb = mw["w_up"], mw["w_down"], mw["conv_w"], mw["conv_b"]
    ff = w_down.shape[0]
    up, h2 = _norm_matmul(x1, rep["norm_ffn_pre"], w_up, "up_proj")
    act = _conv_gate(up, cw, cb, "conv_gate")
    dy3, dx2, loss_acc, dg_ffn_post = _ffn_out(act, w_down, rep["norm_ffn_post"], x1, target, "ffn_out")

    grads = {"norm_ffn_post": dg_ffn_post}
    dact = _matmul(dy3, w_down, "nt", BF16, tr, ff // 2, D_MODEL, "d_act")
    grads["w_down"] = _matmul(act, dy3, "tn", BF16, ff // 2, D_MODEL, tcon, "dw_down")
    dup, grads["conv_w"], grads["conv_b"] = _conv_gate_bwd(up, dact, cw, cb, "conv_gate_bwd")
    half = N_DEV // 2
    dh2 = _matmul_core(
        dup, w_up, NN, (s // tr, 1, N_DEV),
        pl.BlockSpec((None, tr, FF_SLAB), lambda i, j, t: (t // half, i, t % half)),
        pl.BlockSpec((None, FF_SLAB, D_MODEL), lambda i, j, t: (t, 0, 0)),
        pl.BlockSpec((tr, D_MODEL), lambda i, j, t: (i, 0)),
        jax.ShapeDtypeStruct((s, D_MODEL), F32), (tr, D_MODEL), "d_h2")
    grads["w_up"] = _matmul_core(
        dup, h2, TN, (1, N_DEV, s // tcon),
        pl.BlockSpec((None, tcon, FF_SLAB), lambda i, j, t: (j // half, t, j % half)),
        pl.BlockSpec((tcon, D_MODEL), lambda i, j, t: (t, 0)),
        pl.BlockSpec((None, FF_SLAB, D_MODEL), lambda i, j, t: (j, 0, 0)),
        jax.ShapeDtypeStruct((N_DEV, FF_SLAB, D_MODEL), BF16), (FF_SLAB, D_MODEL), "dw_up")
    dx1, dy2, grads["norm_ffn_pre"], grads["norm_mix_post"] = _resnorm_bwd(
        dh2, x1, rep["norm_ffn_pre"], dx2, y2, rep["norm_mix_post"], "resnorm_bwd")
    dyn = _matmul(dy2, w_o, "nt", F32, tr, D_MODEL, D_MODEL, "d_yn")
    grads["w_o"] = _matmul(yn, dy2, "tn", BF16, D_MODEL, D_MODEL, tcon, "dw_o")
    token = early_grads(grads)
    dya, delta_a, doaug, grads["out_norm_a"], grads["out_norm_b"], dya4, dya16, delta4, delta16 = _outnorm_bwd(
        dyn, ya, yb, rep["out_norm_a"] + token, rep["out_norm_b"], _head_ones(), "outnorm_bwd")
    stats = {1: (dya, lse_a, delta_a), 4: (dya4, lse4, delta4), 16: (dya16, lse16, delta16)}
    parts = []
    for g, (_, r) in enumerate(DIL_CONFIGS):
        qside = [q_of(r)] + [own(a) for a in stats[r]]
        kside = [k_of(r), v_of(r)]
        (dq,) = _band_call("dq", r, qside, kside, biases[g], f"band_dq_r{r}")
        dk, dv = _band_call("dkv", r, kside, qside, biases[g], f"band_dkv_r{r}")
        parts.append((dq, dk, dv))
    dproj_a = _sum_cast(parts, "band_grad_sum")
    dqc, dkc, dvp = _mla_bwd(qaug, kcat, kvb, doaug, "mla_bwd")
    dproj_b, grads["w_uq"], grads["w_ukv"], grads["q_lat_norm"], grads["kv_lat_norm"] = _mla_prep_bwd(
        dqc, dkc, dvp, proj, cqn, ckvn, rep["q_lat_norm"], rep["kv_lat_norm"], w_uq, w_ukv, tabs, "mla_prep_bwd")
    dproj = jnp.concatenate([dproj_a, dproj_b], axis=1)
    grads["w_in"] = _matmul(dproj, h1, "tn", BF16, N_PROJ // 2, D_MODEL, tcon, "dw_in")
    token = late_grads(grads)
    dh1 = _matmul(dproj, w_in, "nn", F32, tr, D_MODEL, N_PROJ // 2, "d_h1")
    grad_x, grads["norm_mix_pre"] = _final_bwd(dh1, x, rep["norm_mix_pre"] + token, dx1, "final_bwd")
    loss = 0.5 / D_MODEL * loss_acc[0, 0]
    return loss, grad_x, grads


MESH = pl.DeviceIdType.MESH
HBM_SPEC = pl.BlockSpec(memory_space=pltpu.HBM)
SMALL_ROWS = 96
TRANSPOSED = ("w_in", "w_up", "w_uq")
BUF_SHAPES = {"w_up": (FF_SLAB, D_MODEL), "w_in": (D_IN // N_DEV, D_MODEL), "w_down": (D_FF // N_DEV, D_MODEL),
              "w_o": (D_MODEL // N_DEV, D_MODEL), "w_uq": (QK_NOPE + QK_ROPE, Q_LORA), "w_ukv": (KV_LORA, 128),
              "conv_w": (8, FF_SLAB)}
BUF_ORDER = tuple(BUF_SHAPES)
MIXING = ("w_in", "w_o", "w_uq", "w_ukv")
MIXER = ("w_up", "w_down", "conv_w")
EARLY_GRADS = ("w_up", "w_down", "w_o", "conv_w")
LATE_GRADS = ("w_in", "w_uq", "w_ukv")


def _all_gather(bufs, name):
    nb = len(bufs)

    def body(*refs):
        x_refs, out_refs = refs[:nb], refs[nb:2 * nb]
        send_sems, recv_sems, local_sems = refs[2 * nb:]
        x, y, c = lax.axis_index("x"), lax.axis_index("y"), lax.axis_index("c")
        me, sibling = (x, y, c), (x, y, 1 - c)
        chips = [(1 - x, y), (x, 1 - y), (1 - x, 1 - y)]

        def copy(b, k, block, to, own=False):
            px, py, pc = block
            slot = out_refs[b].at[4 * px + 2 * py + pc]
            return pltpu.make_async_remote_copy(
                src_ref=x_refs[b] if own else slot, dst_ref=slot,
                send_sem=send_sems.at[7 * b + k], recv_sem=recv_sems.at[7 * b + k], device_id=to, device_id_type=MESH)

        mine = [pltpu.make_async_copy(x_refs[b], out_refs[b].at[4 * x + 2 * y + c], local_sems.at[b]) for b in range(nb)]
        sends = []
        for b in range(nb):
            mine[b].start()
            first = [copy(b, 0, me, sibling, own=True)]
            first += [copy(b, 1 + j, me, (*chip, c), own=True) for j, chip in enumerate(chips)]
            for cp in first:
                cp.start()
            sends += first
        for j, chip in enumerate(chips):
            for b in range(nb):
                copy(b, 1 + j, (*chip, c), me).wait_recv()
                passed = copy(b, 4 + j, (*chip, c), sibling)
                passed.start()
                sends.append(passed)
        for b in range(nb):
            copy(b, 0, sibling, me).wait_recv()
            for j, chip in enumerate(chips):
                copy(b, 4 + j, (*chip, 1 - c), me).wait_recv()
        for cp in sends:
            cp.wait_send()
        for cp in mine:
            cp.wait()

    return pl.pallas_call(
        body, name=name,
        out_shape=[jax.ShapeDtypeStruct((N_DEV,) + p.shape, p.dtype) for p in bufs],
        in_specs=[HBM_SPEC] * nb, out_specs=[HBM_SPEC] * nb,
        scratch_shapes=[pltpu.SemaphoreType.DMA((7 * nb,)), pltpu.SemaphoreType.DMA((7 * nb,)),
                        pltpu.SemaphoreType.DMA((nb,))],
    )(*bufs)


def _grad_exchange(bigs, small, name):
    flips = [(fx, fy, fc) for fx in (0, 1) for fy in (0, 1) for fc in (0, 1)][1:]
    nb = len(bigs)

    def body(*refs):
        big_refs, small_ref = refs[:nb], refs[nb]
        rbig_refs, rsmall_ref = refs[nb + 1:2 * nb + 1], refs[2 * nb + 1]
        send_sems, recv_sems, local_sems = refs[2 * nb + 2:]
        x, y, c = lax.axis_index("x"), lax.axis_index("y"), lax.axis_index("c")
        my = 4 * x + 2 * y + c
        own = [pltpu.make_async_copy(big_refs[b].at[my], rbig_refs[b].at[my], local_sems.at[b]) for b in range(nb)]
        own.append(pltpu.make_async_copy(small_ref, rsmall_ref.at[my], local_sems.at[nb]))
        for cp in own:
            cp.start()
        copies = []
        for b in range(nb + 1):
            for k, (fx, fy, fc) in enumerate(flips):
                px = 1 - x if fx else x
                py = 1 - y if fy else y
                pc = 1 - c if fc else c
                src = small_ref if b == nb else big_refs[b].at[4 * px + 2 * py + pc]
                dst = rsmall_ref.at[my] if b == nb else rbig_refs[b].at[my]
                copies.append(pltpu.make_async_remote_copy(
                    src_ref=src, dst_ref=dst, send_sem=send_sems.at[7 * b + k], recv_sem=recv_sems.at[7 * b + k],
                    device_id=(px, py, pc), device_id_type=MESH))
        for cp in copies:
            cp.start()
        for cp in copies:
            cp.wait()
        for cp in own:
            cp.wait()

    nsem = 7 * (nb + 1)
    return pl.pallas_call(
        body, name=name,
        out_shape=[jax.ShapeDtypeStruct(b.shape, b.dtype) for b in bigs]
        + [jax.ShapeDtypeStruct((N_DEV,) + small.shape, small.dtype)],
        in_specs=[HBM_SPEC] * (nb + 1), out_specs=[HBM_SPEC] * (nb + 1),
        scratch_shapes=[pltpu.SemaphoreType.DMA((nsem,)), pltpu.SemaphoreType.DMA((nsem,)),
                        pltpu.SemaphoreType.DMA((nb + 1,))],
    )(*bigs, small)


SEM_SPEC = pl.BlockSpec(memory_space=pltpu.SEMAPHORE)
ANY_SPEC = pl.BlockSpec(memory_space=pl.ANY)
FLIPS = tuple((fx, fy, fc) for fx in (0, 1) for fy in (0, 1) for fc in (0, 1))[1:]


def _split_copies(src_refs, land_refs, send_sems, recv_sems, scatter):
    x, y, c = lax.axis_index("x"), lax.axis_index("y"), lax.axis_index("c")
    my = 4 * x + 2 * y + c
    copies = []
    for b, (src, land) in enumerate(zip(src_refs, land_refs)):
        for k, (fx, fy, fc) in enumerate(FLIPS):
            px = 1 - x if fx else x
            py = 1 - y if fy else y
            pc = 1 - c if fc else c
            copies.append(pltpu.make_async_remote_copy(
                src_ref=src.at[4 * px + 2 * py + pc] if scatter else src, dst_ref=land.at[my],
                send_sem=send_sems.at[7 * b + k], recv_sem=recv_sems.at[7 * b + k],
                device_id=(px, py, pc), device_id_type=MESH))
    return copies


def _exchange_start(srcs, scatter, name):
    nb = len(srcs)
    lands = [lax.empty(s.shape if scatter else (N_DEV,) + s.shape, s.dtype) for s in srcs]

    def body(*refs):
        src_refs, land_refs = refs[:nb], refs[nb:2 * nb]
        send_sems, recv_sems = refs[2 * nb], refs[2 * nb + 1]
        token = refs[-1]
        for cp in _split_copies(src_refs, land_refs, send_sems, recv_sems, scatter):
            cp.start()
        token[...] = jnp.zeros_like(token)

    hbm = lambda a: pltpu.HBM(a.shape, a.dtype)
    outs = pl.pallas_call(
        body, name=name,
        out_shape=(pltpu.SemaphoreType.DMA((7 * nb,)), pltpu.SemaphoreType.DMA((7 * nb,)),
                   *[hbm(a) for a in srcs], *[hbm(a) for a in lands], jax.ShapeDtypeStruct((8, 128), F32)),
        in_specs=[HBM_SPEC] * (2 * nb),
        out_specs=(SEM_SPEC, SEM_SPEC, *[HBM_SPEC] * (2 * nb), pl.BlockSpec(memory_space=pltpu.VMEM)),
        input_output_aliases={i: 2 + i for i in range(2 * nb)},
        compiler_params=pltpu.CompilerParams(has_side_effects=pltpu.SideEffectType.DATAFLOW_SIDE_EFFECTING),
    )(*[pltpu.with_memory_space_constraint(a, pltpu.HBM) for a in srcs],
      *[pltpu.with_memory_space_constraint(a, pltpu.HBM) for a in lands])
    return outs[0], outs[1], list(outs[2:2 + nb]), list(outs[2 + nb:2 + 2 * nb]), outs[-1]


def _exchange_wait(started, scatter, after, name):
    send_sems, recv_sems, srcs, lands, _ = started
    nb = len(srcs)

    def body(*refs):
        src_refs, land_refs = refs[:nb], refs[nb:2 * nb]
        for cp in _split_copies(src_refs, land_refs, refs[2 * nb], refs[2 * nb + 1], scatter):
            cp.wait_send()
            cp.wait_recv()

    hbm = lambda a: pltpu.HBM(a.shape, a.dtype)
    outs = pl.pallas_call(
        body, name=name,
        out_shape=(*[hbm(a) for a in srcs], *[hbm(a) for a in lands]),
        in_specs=[HBM_SPEC] * (2 * nb) + [SEM_SPEC, SEM_SPEC, ANY_SPEC],
        out_specs=tuple([HBM_SPEC] * (2 * nb)),
        input_output_aliases={i: i for i in range(2 * nb)},
        compiler_params=pltpu.CompilerParams(has_side_effects=pltpu.SideEffectType.DATAFLOW_SIDE_EFFECTING),
    )(*srcs, *lands, send_sems, recv_sems, after)
    return list(outs[:nb]), list(outs[nb:])


def _own_slot(land, own):
    my = 4 * lax.axis_index("x") + 2 * lax.axis_index("y") + lax.axis_index("c")
    return lax.dynamic_update_slice(land, own[None], (my,) + (0,) * own.ndim)


def _adamw(parts, w, m, v, name):
    rows, n = w.shape
    tm = rows if rows <= 384 else next(t for t in (256, 176) if rows % t == 0)
    assert rows % tm == 0

    def body(p_ref, w_ref, m_ref, v_ref, g_ref, d_ref, m2_ref, v2_ref):
        g = p_ref[0, :, 0:n].astype(F32)
        for s in range(1, N_DEV):
            g = g + p_ref[s, :, 0:n].astype(F32)
        g_ref[...] = g
        m2 = ADAM_B1 * m_ref[...] + (1.0 - ADAM_B1) * g
        v2 = ADAM_B2 * v_ref[...] + (1.0 - ADAM_B2) * jnp.square(g)
        m2_ref[...] = m2
        v2_ref[...] = v2
        m_hat = m2 / (1.0 - ADAM_B1 ** ADAM_STEP)
        v_hat = v2 / (1.0 - ADAM_B2 ** ADAM_STEP)
        d_ref[...] = -ADAM_LR * (m_hat / (jnp.sqrt(v_hat) + ADAM_EPS) + ADAM_WD * w_ref[...])

    row = pl.BlockSpec((tm, n), lambda i: (i, 0))
    return pl.pallas_call(
        body, name=name, grid=(rows // tm,),
        in_specs=[pl.BlockSpec((N_DEV, tm, parts.shape[2]), lambda i: (0, i, 0)), row, row, row],
        out_specs=[row] * 4,
        out_shape=[jax.ShapeDtypeStruct((rows, n), F32)] * 4,
        compiler_params=_params(("parallel",)),
    )(parts, w, m, v)


def _pack(flat_parts, rows):
    flat = jnp.concatenate(flat_parts, axis=-1)
    pad = rows * LANES - flat.shape[-1]
    flat = jnp.pad(flat, [(0, 0)] * (flat.ndim - 1) + [(0, pad)])
    return flat.reshape(flat.shape[:-1] + (rows, LANES))


def _unpack(packed, shapes):
    flat = packed.reshape(packed.shape[:-2] + (-1,))
    out, off = {}, 0
    for name, shape in shapes.items():
        n = int(np.prod(shape))
        out[name] = flat[..., off:off + n].reshape(flat.shape[:-1] + tuple(shape))
        off += n
    return out


def _pad_to(a, shape):
    return jnp.pad(a, [(0, t - d) for d, t in zip(a.shape, shape)])


def _pad_w_in(w):
    k = w.shape[1]
    z = lambda n: jnp.zeros((n, k), w.dtype)
    return jnp.concatenate([w[:COL_KR], z(64), w[COL_KR:], z(32)], axis=0)


def _unpad_w_in(w):
    return jnp.concatenate([w[:COL_KR], w[COL_KR + 64:COL_KR + 96]], axis=0)


def _assemble_weights(g, conv_b):
    half = N_DEV // 2
    cols = lambda a: a.transpose(1, 0, 2).reshape(a.shape[1], N_DEV * a.shape[2])
    make = {
        "w_in": lambda: _pad_w_in(g["w_in"].reshape(D_IN, D_MODEL)),
        "w_uq": lambda: _pad_to(g["w_uq"], (N_DEV, 128, Q_LORA)).reshape(N_DEV * 128, Q_LORA),
        "w_ukv": lambda: cols(g["w_ukv"]),
        "w_o": lambda: g["w_o"].reshape(D_MODEL, D_MODEL),
        "w_up": lambda: g["w_up"],
        "w_down": lambda: _pad_to(g["w_down"].reshape(half, FF_SHARD, D_MODEL),
                                  (half, FF_SLAB, D_MODEL)).reshape(half * FF_SLAB, D_MODEL),
        "conv_w": lambda: g["conv_w"][:, :3].reshape(2, half, 3, FF_SLAB).transpose(0, 2, 1, 3).reshape(2, 3, half * FF_SLAB),
    }
    fw = {n: make[n]() for n in g}
    if conv_b is not None:
        fw["conv_b"] = _pad_to(conv_b.reshape(2, 1, half, FF_SHARD), (2, 1, half, FF_SLAB)).reshape(2, 1, half * FF_SLAB)
    return fw


def _grad_bufs(grads, names):
    half = N_DEV // 2
    slabs = lambda a: a.reshape(a.shape[0], N_DEV, a.shape[1] // N_DEV).transpose(1, 0, 2)
    make = {
        "w_in": lambda: _unpad_w_in(grads["w_in"]).reshape((N_DEV,) + BUF_SHAPES["w_in"]),
        "w_uq": lambda: grads["w_uq"].reshape(N_DEV, 128, Q_LORA)[:, :QK_NOPE + QK_ROPE],
        "w_ukv": lambda: slabs(grads["w_ukv"]),
        "w_o": lambda: grads["w_o"].reshape((N_DEV,) + BUF_SHAPES["w_o"]),
        "w_up": lambda: grads["w_up"],
        "w_down": lambda: grads["w_down"].reshape(half, FF_SLAB, D_MODEL)[:, :FF_SHARD].reshape((N_DEV,) + BUF_SHAPES["w_down"]),
        "conv_w": lambda: _pad_to(grads["conv_w"].reshape(2, 3, half, FF_SLAB).transpose(0, 2, 1, 3).reshape(N_DEV, 3, FF_SLAB),
                                  (N_DEV,) + BUF_SHAPES["conv_w"]),
    }
    return [make[n]() if n == "conv_w" else make[n]().astype(BF16) for n in names]


def kernel(x, norm_mix_pre, w_in, q_lat_norm, w_uq, kv_lat_norm, w_ukv, out_norm_a, out_norm_b, w_o, norm_mix_post, norm_ffn_pre, w_up, conv_w, conv_b, w_down, norm_ffn_post, loss_target, m_norm_mix_pre, m_w_in, m_q_lat_norm, m_w_uq, m_kv_lat_norm, m_w_ukv, m_out_norm_a, m_out_norm_b, m_w_o, m_norm_mix_post, m_norm_ffn_pre, m_w_up, m_conv_w, m_conv_b, m_w_down, m_norm_ffn_post, v_norm_mix_pre, v_w_in, v_q_lat_norm, v_w_uq, v_kv_lat_norm, v_w_ukv, v_out_norm_a, v_out_norm_b, v_w_o, v_norm_mix_post, v_norm_ffn_pre, v_w_up, v_conv_w, v_conv_b, v_w_down, v_norm_ffn_post):
    given = dict(locals())
    shard = lambda a, n: a[0].T if n in TRANSPOSED else a[0]
    w = {n: shard(given[n], n) for n in WEIGHTS}
    m = {n: shard(given["m_" + n], n) for n in WEIGHTS}
    v = {n: shard(given["v_" + n], n) for n in WEIGHTS}
    rep_shapes = {n: w[n].shape for n in REPLICATED}

    buf = lambda n: _pad_to(w[n] if n == "conv_w" else w[n].astype(BF16), BUF_SHAPES[n])
    first = dict(zip(MIXING, _all_gather([buf(n) for n in MIXING], "weight_all_gather")))
    fw = _assemble_weights(first, None)
    tie = first["w_o"][0, 0, 0].astype(F32) * 0.0
    late_bufs = [buf(n) + tie.astype(w[n].dtype if n == "conv_w" else BF16) for n in MIXER]
    mixer_started = _exchange_start(late_bufs, False, "mixer_weights_start")
    rep = {n: given[n] for n in REPLICATED}
    rep["norm_mix_pre"] = rep["norm_mix_pre"] + mixer_started[4][0, 0]

    def mixer_weights(after):
        srcs, lands = _exchange_wait(mixer_started, False, after, "mixer_weights_wait")
        got = {n: _own_slot(land, own) for n, land, own in zip(MIXER, lands, srcs)}
        return _assemble_weights(got, conv_b)

    early = {}

    def early_grads(grads):
        early["started"] = _exchange_start(_grad_bufs(grads, EARLY_GRADS), True, "early_grads_start")
        return early["started"][4][0, 0]

    def late_grads(grads):
        early["late"] = _exchange_start(_grad_bufs(grads, LATE_GRADS), True, "late_grads_start")
        return early["late"][4][0, 0]

    loss_local, grad_x, grads = _local_step(x[0], loss_target[0], fw, rep, mixer_weights, early_grads, late_grads)

    grads["conv_b"] = grads["conv_b"].reshape(N_DEV, FF_SLAB)[:, :FF_SHARD]
    small = _pack([grads[n].reshape(-1) for n in REPLICATED] + [loss_local.reshape(1)], SMALL_ROWS)
    received_small = _grad_exchange([], small, "grad_exchange")[0]
    my = 4 * lax.axis_index("x") + 2 * lax.axis_index("y") + lax.axis_index("c")
    received = {}
    for names, key, tag in ((EARLY_GRADS, "started", "early_grads_wait"), (LATE_GRADS, "late", "late_grads_wait")):
        srcs, lands = _exchange_wait(early[key], True, received_small, tag)
        for n, land, src in zip(names, lands, srcs):
            received[n] = _own_slot(land, lax.dynamic_index_in_dim(src, my, 0, keepdims=False))
    results = [{}, {}, {}, {}]
    for n in BUF_ORDER:
        parts = received[n]
        if n == "conv_w":
            args = [_pad_to(t[n], BUF_SHAPES[n]) for t in (w, m, v)]
        else:
            args = [w[n], m[n], v[n]]
        outs = _adamw(parts, *args, f"adamw_{n}")
        for t in range(4):
            results[t][n] = outs[t][:w[n].shape[0], :w[n].shape[1]] if n == "conv_w" else outs[t]
    pk = lambda d: _pack([d[n].reshape(-1) for n in REPLICATED] + [jnp.zeros((1,), F32)], SMALL_ROWS)
    small_out = _adamw(received_small, pk(w), pk(m), pk(v), "adamw_replicated")
    rep_shapes["loss"] = (1,)
    for t in range(4):
        results[t].update(_unpack(small_out[t], rep_shapes))

    loss = results[0]["loss"][0]
    outs = [loss, grad_x[None]]
    for res in results:
        outs += [(res[n].T if n in TRANSPOSED else res[n])[None] for n in WEIGHTS]
    return tuple(outs)
```

```python
import functools
import math

import numpy as np
import jax
import jax.numpy as jnp
from jax import lax
from jax.experimental import pallas as pl
from jax.experimental.pallas import tpu as pltpu

F32 = jnp.float32
BF16 = jnp.bfloat16

D_MODEL = 1024
N_DEV = 8
WIDTH_A = 512
N_HEADS = 8
Q_LORA = 384
KV_LORA = 256
QK_ROPE = 32
QK_NOPE = 64
D_FF = 2816
FF_SHARD = 2 * D_FF // N_DEV
FF_SLAB = 768
DIL_CONFIGS = ((128, 1), (512, 4), (2048, 16))
BAND_HALF = 64
ROPE_BASE = 10000.0
EPS = 1e-6
NEG = -1e30
MLA_SCALE = (QK_NOPE + QK_ROPE) ** -0.5
SCALE_A = 0.125
LOG2E = 1.0 / math.log(2.0)
LN2 = math.log(2.0)
QSCALE_A = SCALE_A * LOG2E

COL_CQ = 3 * WIDTH_A
COL_CKV = COL_CQ + Q_LORA
COL_KR = COL_CKV + KV_LORA
N_PROJ = COL_KR + 128
N_LAT = N_PROJ - COL_CQ
D_IN = COL_KR + QK_ROPE

ADAM_LR = 0.001
ADAM_B1 = 0.9
ADAM_B2 = 0.999
ADAM_EPS = 1e-08
ADAM_WD = 0.01
ADAM_STEP = 10

LANES = 128
VMEM_LIMIT = 56 * 1024 * 1024

SHARDED = ("w_in", "w_uq", "w_ukv", "w_o", "w_up", "conv_w", "w_down")
REPLICATED = ("norm_mix_pre", "q_lat_norm", "kv_lat_norm", "out_norm_a", "out_norm_b", "norm_mix_post",
              "norm_ffn_pre", "conv_b", "norm_ffn_post")
WEIGHTS = ("norm_mix_pre", "w_in", "q_lat_norm", "w_uq", "kv_lat_norm", "w_ukv", "out_norm_a", "out_norm_b", "w_o",
           "norm_mix_post", "norm_ffn_pre", "w_up", "conv_w", "conv_b", "w_down", "norm_ffn_post")


def _params(sem=None):
    return pltpu.CompilerParams(dimension_semantics=sem, vmem_limit_bytes=VMEM_LIMIT)


def _dot(a, b, dims):
    return lax.dot_general(a, b, (dims, ((), ())), preferred_element_type=F32)


NN = ((1,), (0,))
NT = ((1,), (1,))
TN = ((0,), (0,))


def _rstd(x):
    return lax.rsqrt(jnp.mean(x * x, axis=-1, keepdims=True) + EPS)


def _rms_bwd(dy, x, g):
    r = _rstd(x)
    z = x * r
    gy = dy * g
    dx = r * (gy - z * jnp.mean(gy * z, axis=-1, keepdims=True))
    return dx, dy * z


def _split_hi_lo(v):
    hi = v.astype(BF16)
    lo = (v - hi.astype(F32)).astype(BF16)
    return hi, lo


def _matmul(a, b, mode, out_dtype, tm, tn, tk, name):
    if mode == "nn":
        (m, k), n = a.shape, b.shape[1]
        a_spec = pl.BlockSpec((tm, tk), lambda i, j, s: (i, s))
        b_spec = pl.BlockSpec((tk, tn), lambda i, j, s: (s, j))
        dims = NN
    elif mode == "nt":
        (m, k), n = a.shape, b.shape[0]
        a_spec = pl.BlockSpec((tm, tk), lambda i, j, s: (i, s))
        b_spec = pl.BlockSpec((tn, tk), lambda i, j, s: (j, s))
        dims = NT
    else:
        (k, m), n = a.shape, b.shape[1]
        a_spec = pl.BlockSpec((tk, tm), lambda i, j, s: (s, i))
        b_spec = pl.BlockSpec((tk, tn), lambda i, j, s: (s, j))
        dims = TN
    assert m % tm == 0 and n % tn == 0 and k % tk == 0, (name, m, n, k, tm, tn, tk)
    return _matmul_core(a, b, dims, (m // tm, n // tn, k // tk), a_spec, b_spec,
                        pl.BlockSpec((tm, tn), lambda i, j, s: (i, j)), jax.ShapeDtypeStruct((m, n), out_dtype),
                        (tm, tn), name)


def _matmul_core(a, b, dims, grid, a_spec, b_spec, o_spec, out_sds, acc_shape, name):
    nk = grid[2]

    def body(a_ref, b_ref, o_ref, acc_ref):
        s = pl.program_id(2)

        @pl.when(s == 0)
        def _():
            acc_ref[...] = jnp.zeros_like(acc_ref)

        acc_ref[...] += _dot(a_ref[...].astype(BF16), b_ref[...].astype(BF16), dims)

        @pl.when(s == nk - 1)
        def _():
            o_ref[...] = acc_ref[...].astype(out_sds.dtype)

    return pl.pallas_call(
        body, name=name, grid=grid, in_specs=[a_spec, b_spec], out_specs=o_spec, out_shape=out_sds,
        scratch_shapes=[pltpu.VMEM(acc_shape, F32)],
        compiler_params=_params(("parallel", "parallel", "arbitrary")),
    )(a, b)


def _norm_matmul(x, g, w, name):
    s, k = x.shape
    tm = min(2048, s)
    nj, tn, _ = w.shape
    half = nj // 2
    w_spec = pl.BlockSpec((None, tn, k), lambda i, j: (j, 0, 0))
    o_spec = pl.BlockSpec((None, tm, tn), lambda i, j: (j // half, i, j % half))
    o_sds = jax.ShapeDtypeStruct((2, s, half * tn), BF16)

    def body(x_ref, g_ref, w_ref, o_ref, h_ref):
        @pl.when(pl.program_id(1) == 0)
        def _():
            xv = x_ref[...]
            h_ref[...] = (xv * _rstd(xv) * g_ref[...]).astype(BF16)

        o_ref[...] = _dot(h_ref[...], w_ref[...], NT).astype(BF16)

    return pl.pallas_call(
        body, name=name, grid=(s // tm, nj),
        in_specs=[pl.BlockSpec((tm, k), lambda i, j: (i, 0)),
                  pl.BlockSpec((1, k), lambda i, j: (0, 0)),
                  w_spec],
        out_specs=[o_spec, pl.BlockSpec((tm, k), lambda i, j: (i, 0))],
        out_shape=[o_sds, jax.ShapeDtypeStruct((s, k), BF16)],
        compiler_params=_params(("parallel", "arbitrary")),
    )(x, g, w)


def _band_bias(r):
    off = np.arange(256)[None, :] - BAND_HALF - np.arange(128)[:, None]
    slopes = np.exp2(-8.0 * np.arange(1, N_HEADS + 1, dtype=np.float32) / N_HEADS).astype(np.float32)
    dist = (np.abs(off) * r).astype(np.float32)
    bias = -slopes[:, None, None] * dist[None]
    bias = np.where((np.abs(off) <= BAND_HALF)[None], bias * np.float32(LOG2E), np.float32(NEG))
    return jnp.asarray(bias, F32)


def _band_call(mode, r, center, window, bias, name):
    seq = center[0][0].shape[0]
    tq = min(512, seq)
    nsub = tq // 128
    hb = tq // BAND_HALF
    nh = seq // BAND_HALF
    nc, nw = len(center), len(window)
    out_dtypes = {"fwd": (BF16, F32), "dq": (BF16,), "dkv": (BF16, BF16)}[mode]
    n_out = len(out_dtypes)

    def specs(col):
        return (pl.BlockSpec((BAND_HALF, WIDTH_A), lambda c, i: (jnp.maximum(i * hb - 1, 0), col(c))),
                pl.BlockSpec((tq, WIDTH_A), lambda c, i: (i, col(c))),
                pl.BlockSpec((BAND_HALF, WIDTH_A), lambda c, i: (jnp.minimum((i + 1) * hb, nh - 1), col(c))))

    cspec = pl.BlockSpec((tq, WIDTH_A), lambda c, i: (i, c))
    in_specs = [specs(col)[1] for _, col in center]
    operands = [a for a, _ in center]
    for a, col in window:
        in_specs += list(specs(col))
        operands += [a, a, a]
    in_specs.append(pl.BlockSpec((N_HEADS, 128, 256), lambda c, i: (0, 0, 0)))
    operands.append(bias)
    window = [a for a, _ in window]

    def aug_stat(base, stat_sw, lane, act, e0):
        hi, lo = _split_hi_lo(stat_sw)
        return jnp.where(act, base, jnp.where(lane == e0, -hi, jnp.where(lane == e0 + 1, -lo, jnp.zeros_like(hi))))

    def aug_ones(base, lane, e0):
        return jnp.where((lane == e0) | (lane == e0 + 1), jnp.ones_like(base), base)

    def body(*refs):
        c_refs = refs[:nc]
        w_refs = refs[nc:nc + 3 * nw]
        bias_ref = refs[nc + 3 * nw]
        o_refs = refs[nc + 3 * nw + 1:nc + 3 * nw + 1 + n_out]
        wins = refs[nc + 3 * nw + 1 + n_out:]
        i = pl.program_id(1)
        for t in range(nw):
            wins[t][0:BAND_HALF, :] = w_refs[3 * t][...]
            wins[t][BAND_HALF:BAND_HALF + tq, :] = w_refs[3 * t + 1][...]
            wins[t][BAND_HALF + tq:BAND_HALF + tq + BAND_HALF, :] = w_refs[3 * t + 2][...]

        def sub(j, carry):
            r0 = pl.multiple_of(j * 128, 128)
            wpos = i * tq + j * 128 - BAND_HALF + lax.broadcasted_iota(jnp.int32, (128, 256), 1)
            valid = (wpos >= 0) & (wpos < seq)
            lane_c = lax.broadcasted_iota(jnp.int32, (128, 128), 1)
            lane_w = lax.broadcasted_iota(jnp.int32, (256, 128), 1)
            heads = [(p, a) for p in range(4) for a in range(2)]
            first, last_ops = [], []
            for p, a in heads:
                cols = slice(p * 128, (p + 1) * 128)
                cs = [c[pl.ds(r0, 128), cols] for c in c_refs]
                ws = [w[pl.ds(r0, 256), cols] for w in wins]
                e0 = 64 if a == 0 else 0
                act_c = (lane_c < 64) if a == 0 else (lane_c >= 64)
                act_w = (lane_w < 64) if a == 0 else (lane_w >= 64)
                bias_a = bias_ref[2 * p + a]
                if mode == "fwd":
                    qa = jnp.where(act_c, cs[0] * QSCALE_A, jnp.zeros_like(cs[0]))
                    first.append((_dot(qa, ws[0], NT) + bias_a, None))
                    last_ops.append((ws[1],))
                elif mode == "dq":
                    q2, dy2, l2, d2 = cs
                    k2, v2 = ws
                    q_aug = aug_stat(q2 * QSCALE_A, pltpu.roll(l2, 64, 1), lane_c, act_c, e0)
                    dy_aug = aug_stat(dy2, pltpu.roll(d2, 64, 1), lane_c, act_c, e0)
                    first.append((_dot(q_aug, aug_ones(k2, lane_w, e0), NT) + bias_a,
                                  _dot(dy_aug, aug_ones(v2, lane_w, e0), NT)))
                    last_ops.append((k2,))
                else:
                    k2, v2 = cs
                    q2, dy2, l2, d2 = ws
                    q_aug = aug_stat(q2 * QSCALE_A, pltpu.roll(l2, 64, 1), lane_w, act_w, e0)
                    dy_aug = aug_stat(dy2, pltpu.roll(d2, 64, 1), lane_w, act_w, e0)
                    first.append((_dot(aug_ones(k2, lane_c, e0), q_aug, NT) + bias_a,
                                  _dot(aug_ones(v2, lane_c, e0), dy_aug, NT)))
                    last_ops.append((q_aug, dy_aug))
            mid = []
            for sc, dp in first:
                sc = jnp.where(valid, sc, NEG)
                if mode == "fwd":
                    m = jnp.max(sc, axis=-1, keepdims=True)
                    e = jnp.exp2(sc - m)
                    l = jnp.sum(e, axis=-1, keepdims=True)
                    mid.append((e.astype(BF16), l, m + jnp.log(l) * LOG2E))
                else:
                    pr = jnp.exp2(sc)
                    mid.append((pr.astype(BF16), (pr * dp).astype(BF16)))
            res = []
            for md, ops in zip(mid, last_ops):
                if mode == "fwd":
                    res.append((_dot(md[0], ops[0], NN) / md[1], jnp.broadcast_to(md[2], (128, 128))))
                elif mode == "dq":
                    res.append((_dot(md[1], ops[0], NN) * SCALE_A,))
                else:
                    res.append((_dot(md[1], ops[0], NN) * LN2, _dot(md[0], ops[1], NN)))
            for t in range(n_out):
                pairs = [jnp.where(lane_c < 64, res[2 * p][t], res[2 * p + 1][t]) for p in range(4)]
                o_refs[t][pl.ds(r0, 128), :] = jnp.concatenate(pairs, axis=1).astype(out_dtypes[t])
            return carry

        lax.fori_loop(0, nsub, sub, 0)

    outs = pl.pallas_call(
        body, name=name, grid=(r, seq // tq),
        in_specs=in_specs,
        out_specs=[cspec] * n_out,
        out_shape=[jax.ShapeDtypeStruct((seq, r * WIDTH_A), dt) for dt in out_dtypes],
        scratch_shapes=[pltpu.VMEM((tq + 2 * BAND_HALF, WIDTH_A), w.dtype) for w in window],
        compiler_params=_params(("parallel", "parallel")),
    )(*operands)
    return outs


def _slab_scratch(tm, w):
    return pltpu.VMEM((w // 128, tm, 128), F32)


def _put(scr, val):
    for j in range(scr.shape[0]):
        scr[j] = val[:, j * 128:(j + 1) * 128].astype(F32)


def _get(scr):
    return jnp.concatenate([scr[j] for j in range(scr.shape[0])], axis=1)


def _dilate_store(dst_ref, scr, r):
    nb, tm, _ = scr.shape
    w = nb * 128
    for c in range(r):
        for j in range(nb):
            dst_ref[:, c * w + j * 128:c * w + (j + 1) * 128] = scr[j, pl.ds(c, tm // r, stride=r), :].astype(dst_ref.dtype)


def _undilate(scr, src_ref, r):
    nb, tm, _ = scr.shape
    w = nb * 128
    for c in range(r):
        for j in range(nb):
            scr[j, pl.ds(c, tm // r, stride=r), :] = src_ref[:, c * w + j * 128:c * w + (j + 1) * 128].astype(F32)


def _dil_spec(tm, r, w):
    return pl.BlockSpec((tm // r, r * w), lambda i: (i, 0))


def _dil_shape(s, r, w, dtype):
    return jax.ShapeDtypeStruct((s // r, r * w), dtype)


def _in_proj(x, g, w, name):
    s, k = x.shape
    n = w.shape[0]
    tm = min(512, s)
    qkv = 3 * WIDTH_A

    def body(x_ref, g_ref, w_ref, o_ref, h_ref, d4_ref, d16_ref, scr):
        xv = x_ref[...]
        h = (xv * _rstd(xv) * g_ref[...]).astype(BF16)
        h_ref[...] = h
        acc = _dot(h, w_ref[...], NT)
        o_ref[...] = acc.astype(BF16)
        _put(scr, acc[:, 0:qkv])
        _dilate_store(d4_ref, scr, 4)
        _dilate_store(d16_ref, scr, 16)

    row = lambda c: pl.BlockSpec((tm, c), lambda i: (i, 0))
    return pl.pallas_call(
        body, name=name, grid=(s // tm,),
        in_specs=[row(k), pl.BlockSpec((1, k), lambda i: (0, 0)), pl.BlockSpec((n, k), lambda i: (0, 0))],
        out_specs=[row(n), row(k), _dil_spec(tm, 4, qkv), _dil_spec(tm, 16, qkv)],
        out_shape=[jax.ShapeDtypeStruct((s, n), BF16), jax.ShapeDtypeStruct((s, k), BF16),
                   _dil_shape(s, 4, qkv, BF16), _dil_shape(s, 16, qkv, BF16)],
        scratch_shapes=[_slab_scratch(tm, qkv)],
        compiler_params=_params(("parallel",)),
    )(x, g, w)


def _band_combine(os_, lses, name):
    s = os_[0].shape[0]
    tm = min(512, s)

    def body(o1, o4, o16, l1, l4, l16, ya_ref, lse_ref, lse4_ref, lse16_ref, so4, sl4, so16, sl16):
        _undilate(so4, o4, 4)
        _undilate(sl4, l4, 4)
        _undilate(so16, o16, 16)
        _undilate(sl16, l16, 16)
        a0, a1, a2 = l1[...], _get(sl4), _get(sl16)
        m = jnp.maximum(jnp.maximum(a0, a1), a2)
        e0, e1, e2 = jnp.exp2(a0 - m), jnp.exp2(a1 - m), jnp.exp2(a2 - m)
        den = e0 + e1 + e2
        ya_ref[...] = (e0 * o1[...] + e1 * _get(so4) + e2 * _get(so16)) / den
        lse = m + jnp.log(den) * LOG2E
        lse_ref[...] = lse
        _put(sl4, lse)
        _dilate_store(lse4_ref, sl4, 4)
        _dilate_store(lse16_ref, sl4, 16)

    nat = pl.BlockSpec((tm, WIDTH_A), lambda i: (i, 0))
    d4, d16 = _dil_spec(tm, 4, WIDTH_A), _dil_spec(tm, 16, WIDTH_A)
    return pl.pallas_call(
        body, name=name, grid=(s // tm,), in_specs=[nat, d4, d16] * 2, out_specs=[nat, nat, d4, d16],
        out_shape=[jax.ShapeDtypeStruct((s, WIDTH_A), F32)] * 2
        + [_dil_shape(s, 4, WIDTH_A, F32), _dil_shape(s, 16, WIDTH_A, F32)],
        scratch_shapes=[_slab_scratch(tm, WIDTH_A)] * 4,
        compiler_params=_params(("parallel",)),
    )(*os_, *lses)


def _rope_tables(s):
    pos = jnp.arange(s, dtype=F32)
    inv_freq = jnp.exp(-math.log(ROPE_BASE) * jnp.arange(0, QK_ROPE, 2, dtype=F32) / QK_ROPE)
    ang = pos[:, None] * inv_freq[None, :]
    cos, sin = jnp.cos(ang), jnp.sin(ang)
    one = jnp.ones((s, 64), F32)
    zero16 = jnp.zeros((s, 16), F32)
    c = jnp.concatenate([one, cos, cos, jnp.ones((s, 32), F32)], axis=1)
    sa = jnp.concatenate([jnp.zeros((s, 64), F32), -sin, zero16, jnp.zeros((s, 32), F32)], axis=1)
    sb = jnp.concatenate([jnp.zeros((s, 64), F32), zero16, sin, jnp.zeros((s, 32), F32)], axis=1)
    return c, sa, sb


def _rope_fwd(x, c, sa, sb):
    return x * c + pltpu.roll(x, 112, 1) * sa + pltpu.roll(x, 16, 1) * sb


def _rope_bwd(dy, c, sa, sb):
    return dy * c + pltpu.roll(dy * sa, 16, 1) + pltpu.roll(dy * sb, 112, 1)


def _mla_prep(proj, g_q, g_kv, w_uq, w_ukv, tabs, name):
    s = proj.shape[0]
    tm = min(512, s)
    width = N_HEADS * 128

    def body(lat_ref, gq_ref, gkv_ref, wq_ref, wkv_ref, c_ref, sa_ref, sb_ref,
             q_ref, k_ref, kv_ref, cqn_ref, ckvn_ref):
        c, sa, sb = c_ref[...], sa_ref[...], sb_ref[...]
        cq = lat_ref[:, 0:Q_LORA].astype(F32)
        cqn = (cq * _rstd(cq) * gq_ref[...]).astype(BF16)
        cqn_ref[...] = cqn
        q = _dot(cqn, wq_ref[...], NT)
        ckv = lat_ref[:, Q_LORA:Q_LORA + KV_LORA].astype(F32)
        ckvn = (ckv * _rstd(ckv) * gkv_ref[...]).astype(BF16)
        ckvn_ref[...] = ckvn
        kv = _dot(ckvn, wkv_ref[...], NN)
        lane = lax.broadcasted_iota(jnp.int32, (tm, 128), 1)
        krr = _rope_fwd(lat_ref[:, Q_LORA + KV_LORA:].astype(F32), c, sa, sb)
        krr = jnp.where((lane == 96) | (lane == 97), 1.0, krr)
        ones01 = jnp.where(lane < 2, 1.0, 0.0)
        for h in range(N_HEADS):
            cols = slice(h * 128, (h + 1) * 128)
            q_ref[:, cols] = (_rope_fwd(q[:, cols], c, sa, sb) * (MLA_SCALE * LOG2E)).astype(BF16)
            k_ref[:, cols] = jnp.where(lane < 64, kv[:, cols], krr).astype(BF16)
            kv_ref[:, cols] = jnp.where(lane < 64, ones01, kv[:, cols]).astype(BF16)

    row = lambda n: pl.BlockSpec((tm, n), lambda i: (i, 0))
    full = lambda a: pl.BlockSpec(a.shape, lambda i: (0, 0))
    tab = pl.BlockSpec((tm, 128), lambda i: (i, 0))
    return pl.pallas_call(
        body, name=name, grid=(s // tm,),
        in_specs=[pl.BlockSpec((tm, N_LAT), lambda i: (i, COL_CQ // N_LAT)),
                  full(g_q), full(g_kv), full(w_uq), full(w_ukv), tab, tab, tab],
        out_specs=[row(width), row(width), row(width), row(Q_LORA), row(KV_LORA)],
        out_shape=[jax.ShapeDtypeStruct((s, width), BF16)] * 3
        + [jax.ShapeDtypeStruct((s, Q_LORA), BF16), jax.ShapeDtypeStruct((s, KV_LORA), BF16)],
        compiler_params=_params(("parallel",)),
    )(proj, g_q, g_kv, w_uq, w_ukv, *tabs)


def _mla_fwd(qcat, kcat, kvb, name):
    s = qcat.shape[0]
    tq = min(1024, s)
    tk = min(1024, s)
    nkc = s // tk

    def body(q_ref, k_ref, v_ref, yb_ref, qaug_ref, m_ref, acc_ref):
        lane = lax.broadcasted_iota(jnp.int32, (tq, 128), 1)
        m_ref[...] = jnp.full((2, tq, 128), NEG, F32)
        acc_ref[...] = jnp.zeros((2, tq, 128), F32)

        def chunk(cidx, carry):
            k0 = pl.multiple_of(cidx * tk, tk)
            cols = [slice(a * 128, (a + 1) * 128) for a in range(2)]
            scs = [_dot(q_ref[:, c], k_ref[pl.ds(k0, tk), c], NT) for c in cols]
            prs, alphas = [], []
            for a, sc in enumerate(scs):
                m_prev = m_ref[a]
                m_new = jnp.maximum(m_prev, jnp.max(sc, axis=-1, keepdims=True))
                alphas.append(jnp.exp2(m_prev - m_new))
                prs.append(jnp.exp2(sc - jnp.tile(m_new, (1, tk // 128))).astype(BF16))
                m_ref[a] = m_new
            for a, c in enumerate(cols):
                acc_ref[a] = alphas[a] * acc_ref[a] + _dot(prs[a], v_ref[pl.ds(k0, tk), c], NN)
            return carry

        lax.fori_loop(0, nkc, chunk, 0)
        outs = []
        for a in range(2):
            cols = slice(a * 128, (a + 1) * 128)
            acc = acc_ref[a]
            l = acc[:, 0:1]
            outs.append(acc / l)
            hi, lo = _split_hi_lo(m_ref[a] + jnp.log(l) * LOG2E)
            qaug_ref[:, cols] = jnp.where(lane == 96, -hi, jnp.where(lane == 97, -lo, q_ref[:, cols]))
        yb_ref[...] = jnp.where(lane < 64, pltpu.roll(outs[0], 64, 1), outs[1])

    return pl.pallas_call(
        body, name=name, grid=(4, s // tq),
        in_specs=[pl.BlockSpec((tq, 256), lambda p, i: (i, p)),
                  pl.BlockSpec((s, 256), lambda p, i: (0, p)),
                  pl.BlockSpec((s, 256), lambda p, i: (0, p))],
        out_specs=[pl.BlockSpec((tq, 128), lambda p, i: (i, p)),
                   pl.BlockSpec((tq, 256), lambda p, i: (i, p))],
        out_shape=[jax.ShapeDtypeStruct((s, WIDTH_A), F32), jax.ShapeDtypeStruct((s, N_HEADS * 128), BF16)],
        scratch_shapes=[pltpu.VMEM((2, tq, 128), F32)] * 2,
        compiler_params=_params(("parallel", "parallel")),
    )(qcat, kcat, kvb)


def _mla_bwd(qaug, kcat, kvb, doaug, name):
    s = qaug.shape[0]
    tq = min(1024, s)
    tk = min(512, s)
    nqc = s // tq
    width = N_HEADS * 128

    def body(q_ref, do_ref, k_ref, v_ref, dq_ref, dk_acc, dv_acc):
        j = pl.program_id(1)

        @pl.when(j == 0)
        def _():
            dq_ref[...] = jnp.zeros_like(dq_ref)

        dk_acc[...] = jnp.zeros_like(dk_acc)
        dv_acc[...] = jnp.zeros_like(dv_acc)

        def chunk(cidx, carry):
            q0 = pl.multiple_of(cidx * tq, tq)
            cols = [slice(a * 128, (a + 1) * 128) for a in range(2)]
            qs = [q_ref[pl.ds(q0, tq), c] for c in cols]
            dos = [do_ref[pl.ds(q0, tq), c] for c in cols]
            kbs = [k_ref[:, c] for c in cols]
            sts = [_dot(kbs[a], qs[a], NT) for a in range(2)]
            dps = [_dot(v_ref[:, cols[a]], dos[a], NT) for a in range(2)]
            pts, dsts = [], []
            for a in range(2):
                pt = jnp.exp2(sts[a])
                pts.append(pt.astype(BF16))
                dsts.append((pt * dps[a]).astype(BF16))
            for a, c in enumerate(cols):
                dv_acc[:, c] += _dot(pts[a], dos[a], NN)
                dk_acc[:, c] += _dot(dsts[a], qs[a], NN)
                dq_ref[pl.ds(q0, tq), c] += _dot(dsts[a], kbs[a], TN)
            return carry

        lax.fori_loop(0, nqc, chunk, 0)

    return pl.pallas_call(
        body, name=name, grid=(N_HEADS // 2, s // tk),
        in_specs=[pl.BlockSpec((s, 256), lambda p, j: (0, p)),
                  pl.BlockSpec((s, 256), lambda p, j: (0, p)),
                  pl.BlockSpec((tk, 256), lambda p, j: (j, p)),
                  pl.BlockSpec((tk, 256), lambda p, j: (j, p))],
        out_specs=[pl.BlockSpec((s, 256), lambda p, j: (0, p)),
                   pl.BlockSpec((tk, 256), lambda p, j: (j, p)),
                   pl.BlockSpec((tk, 256), lambda p, j: (j, p))],
        out_shape=[jax.ShapeDtypeStruct((s, width), F32)] * 3,
        compiler_params=_params(("parallel", "arbitrary")),
    )(qaug, doaug, kcat, kvb)


def _mla_prep_bwd(dqc, dkc, dvp, proj, cqn, ckvn, g_q, g_kv, w_uq, w_ukv, tabs, name):
    s = proj.shape[0]
    tm = min(512, s)
    width = N_HEADS * 128
    n_out_cols = N_LAT

    def body(dq_ref, dk_ref, dv_ref, lat_ref, cqn_ref, ckvn_ref, gq_ref, gkv_ref, wq_ref, wkv_ref,
             c_ref, sa_ref, sb_ref, dproj_ref, dwq_ref, dwkv_ref, dgq_ref, dgkv_ref):
        i = pl.program_id(0)

        @pl.when(i == 0)
        def _():
            dwq_ref[...] = jnp.zeros_like(dwq_ref)
            dwkv_ref[...] = jnp.zeros_like(dwkv_ref)
            dgq_ref[...] = jnp.zeros_like(dgq_ref)
            dgkv_ref[...] = jnp.zeros_like(dgkv_ref)

        c, sa, sb = c_ref[...], sa_ref[...], sb_ref[...]
        lane = lax.broadcasted_iota(jnp.int32, (tm, 128), 1)
        dkr = jnp.zeros((tm, 128), F32)
        dq_parts, dkv_parts = [], []
        for h in range(N_HEADS):
            cols = slice(h * 128, (h + 1) * 128)
            dq_parts.append(_rope_bwd(dq_ref[:, cols] * MLA_SCALE, c, sa, sb).astype(BF16))
            dkh = dk_ref[:, cols] * LN2
            dkr = dkr + dkh
            dkv_parts.append(jnp.where(lane < 64, dkh, dv_ref[:, cols]).astype(BF16))
        dq = jnp.concatenate(dq_parts, axis=1)
        dkv = jnp.concatenate(dkv_parts, axis=1)
        dkr = _rope_bwd(jnp.where((lane >= 64) & (lane < 96), dkr, 0.0), c, sa, sb)

        dcqn = _dot(dq, wq_ref[...], NN)
        dwq_ref[...] += _dot(dq, cqn_ref[...], TN)
        dcq, dgq = _rms_bwd(dcqn, lat_ref[:, 0:Q_LORA].astype(F32), gq_ref[...])
        dgq_ref[...] += jnp.sum(dgq, axis=0, keepdims=True)

        dckvn = _dot(dkv, wkv_ref[...], NT)
        dwkv_ref[...] += _dot(ckvn_ref[...], dkv, TN)
        dckv, dgkv = _rms_bwd(dckvn, lat_ref[:, Q_LORA:Q_LORA + KV_LORA].astype(F32), gkv_ref[...])
        dgkv_ref[...] += jnp.sum(dgkv, axis=0, keepdims=True)

        dproj_ref[:, 0:Q_LORA] = dcq.astype(BF16)
        dproj_ref[:, Q_LORA:Q_LORA + KV_LORA] = dckv.astype(BF16)
        dproj_ref[:, Q_LORA + KV_LORA:] = dkr.astype(BF16)

    row = lambda n: pl.BlockSpec((tm, n), lambda i: (i, 0))
    full = lambda a: pl.BlockSpec(a.shape, lambda i: (0, 0))
    tab = pl.BlockSpec((tm, 128), lambda i: (i, 0))
    return pl.pallas_call(
        body, name=name, grid=(s // tm,),
        in_specs=[row(width), row(width), row(width),
                  pl.BlockSpec((tm, N_LAT), lambda i: (i, COL_CQ // N_LAT)),
                  row(Q_LORA), row(KV_LORA), full(g_q), full(g_kv), full(w_uq), full(w_ukv), tab, tab, tab],
        out_specs=[row(n_out_cols), full(w_uq), full(w_ukv), full(g_q), full(g_kv)],
        out_shape=[jax.ShapeDtypeStruct((s, n_out_cols), BF16),
                   jax.ShapeDtypeStruct(w_uq.shape, F32), jax.ShapeDtypeStruct(w_ukv.shape, F32),
                   jax.ShapeDtypeStruct(g_q.shape, F32), jax.ShapeDtypeStruct(g_kv.shape, F32)],
        compiler_params=_params(("arbitrary",)),
    )(dqc, dkc, dvp, proj, cqn, ckvn, g_q, g_kv, w_uq, w_ukv, *tabs)


def _mix_out(ya, yb, na, nb, w_o, g_post, x, name):
    s = x.shape[0]
    tm = min(512, s)

    def body(ya_ref, yb_ref, na_ref, nb_ref, w_ref, g_ref, x_ref, yn_ref, y2_ref, x1_ref):
        a, b = ya_ref[...], yb_ref[...]
        yn = jnp.concatenate([a * _rstd(a) * na_ref[...], b * _rstd(b) * nb_ref[...]], axis=1).astype(BF16)
        yn_ref[...] = yn
        y2 = _dot(yn, w_ref[...], NN)
        y2_ref[...] = y2
        x1_ref[...] = x_ref[...] + y2 * _rstd(y2) * g_ref[...]

    row = lambda n: pl.BlockSpec((tm, n), lambda i: (i, 0))
    full = lambda a: pl.BlockSpec(a.shape, lambda i: (0, 0))
    return pl.pallas_call(
        body, name=name, grid=(s // tm,),
        in_specs=[row(WIDTH_A), row(WIDTH_A), full(na), full(nb), full(w_o), full(g_post), row(D_MODEL)],
        out_specs=[row(D_MODEL)] * 3,
        out_shape=[jax.ShapeDtypeStruct((s, D_MODEL), BF16), jax.ShapeDtypeStruct((s, D_MODEL), F32),
                   jax.ShapeDtypeStruct((s, D_MODEL), F32)],
        compiler_params=_params(("parallel",)),
    )(ya, yb, na, nb, w_o, g_post, x)


def _head_ones():
    blk = np.kron(np.eye(N_HEADS, dtype=np.float32), np.ones((64, 64), np.float32))
    return jnp.asarray(blk, F32)


def _outnorm_bwd(dyn, ya, yb, na, nb, ones, name):
    s = ya.shape[0]
    tm = min(512, s)

    def body(dyn_ref, ya_ref, yb_ref, na_ref, nb_ref, ones_ref, dya_ref, da_ref, do_ref, dna_ref, dnb_ref,
             dya4_ref, dya16_ref, da4_ref, da16_ref, scr):
        i = pl.program_id(0)

        @pl.when(i == 0)
        def _():
            dna_ref[...] = jnp.zeros_like(dna_ref)
            dnb_ref[...] = jnp.zeros_like(dnb_ref)

        a, b = ya_ref[...], yb_ref[...]
        dya, dna = _rms_bwd(dyn_ref[:, 0:WIDTH_A], a, na_ref[...])
        dyb, dnb = _rms_bwd(dyn_ref[:, WIDTH_A:], b, nb_ref[...])
        dna_ref[...] += jnp.sum(dna, axis=0, keepdims=True)
        dnb_ref[...] += jnp.sum(dnb, axis=0, keepdims=True)
        dya_b = dya.astype(BF16)
        dya_ref[...] = dya_b
        hp = lax.Precision.HIGHEST
        delta_a = jnp.dot(dya_b.astype(F32) * a, ones_ref[...], precision=hp, preferred_element_type=F32)
        da_ref[...] = delta_a
        _put(scr, dya_b)
        _dilate_store(dya4_ref, scr, 4)
        _dilate_store(dya16_ref, scr, 16)
        _put(scr, delta_a)
        _dilate_store(da4_ref, scr, 4)
        _dilate_store(da16_ref, scr, 16)
        dyb_b = dyb.astype(BF16)
        db = jnp.dot(dyb_b.astype(F32) * b, ones_ref[...], precision=hp, preferred_element_type=F32)
        lane = lax.broadcasted_iota(jnp.int32, (tm, 128), 1)
        zero = jnp.zeros((tm, 128), BF16)
        for p in range(4):
            cols = slice(p * 128, (p + 1) * 128)
            dyp = dyb_b[:, cols]
            dbp = db[:, cols]
            for a_ in range(2):
                src = pltpu.roll(dyp.astype(F32), 64, 1).astype(BF16) if a_ == 0 else dyp
                dlt = dbp if a_ == 0 else pltpu.roll(dbp, 64, 1)
                hi, lo = _split_hi_lo(dlt)
                blk = jnp.where(lane >= 64, src, jnp.where(lane == 0, -hi, jnp.where(lane == 1, -lo, zero)))
                h = 2 * p + a_
                do_ref[:, h * 128:(h + 1) * 128] = blk

    row = lambda n: pl.BlockSpec((tm, n), lambda i: (i, 0))
    full = lambda a: pl.BlockSpec(a.shape, lambda i: (0, 0))
    return pl.pallas_call(
        body, name=name, grid=(s // tm,),
        in_specs=[row(D_MODEL), row(WIDTH_A), row(WIDTH_A), full(na), full(nb), full(ones)],
        out_specs=[row(WIDTH_A), row(WIDTH_A), row(N_HEADS * 128), full(na), full(nb),
                   _dil_spec(tm, 4, WIDTH_A), _dil_spec(tm, 16, WIDTH_A), _dil_spec(tm, 4, WIDTH_A), _dil_spec(tm, 16, WIDTH_A)],
        out_shape=[jax.ShapeDtypeStruct((s, WIDTH_A), BF16), jax.ShapeDtypeStruct((s, WIDTH_A), F32),
                   jax.ShapeDtypeStruct((s, N_HEADS * 128), BF16),
                   jax.ShapeDtypeStruct(na.shape, F32), jax.ShapeDtypeStruct(nb.shape, F32),
                   _dil_shape(s, 4, WIDTH_A, BF16), _dil_shape(s, 16, WIDTH_A, BF16),
                   _dil_shape(s, 4, WIDTH_A, F32), _dil_shape(s, 16, WIDTH_A, F32)],
        scratch_shapes=[_slab_scratch(tm, WIDTH_A)],
        compiler_params=_params(("arbitrary",)),
    )(dyn, ya, yb, na, nb, ones)


def _sum_cast(parts, name):
    s = parts[0][0].shape[0]
    tm = min(512, s)

    def body(*refs):
        o_ref, s4, s16 = refs[9:]
        for t in range(3):
            _undilate(s4, refs[3 + t], 4)
            _undilate(s16, refs[6 + t], 16)
            acc = refs[t][...] + _get(s4) + _get(s16)
            o_ref[:, t * WIDTH_A:(t + 1) * WIDTH_A] = acc.astype(BF16)

    nat = pl.BlockSpec((tm, WIDTH_A), lambda i: (i, 0))
    flat = [parts[g][t] for g in range(3) for t in range(3)]
    return pl.pallas_call(
        body, name=name, grid=(s // tm,),
        in_specs=[nat] * 3 + [_dil_spec(tm, 4, WIDTH_A)] * 3 + [_dil_spec(tm, 16, WIDTH_A)] * 3,
        out_specs=pl.BlockSpec((tm, 3 * WIDTH_A), lambda i: (i, 0)),
        out_shape=jax.ShapeDtypeStruct((s, 3 * WIDTH_A), BF16),
        scratch_shapes=[_slab_scratch(tm, WIDTH_A)] * 2,
        compiler_params=_params(("parallel",)),
    )(*flat)


HALO = 16


def _gelu(x):
    k = math.sqrt(2.0 / math.pi)
    t = jnp.tanh(k * (x + 0.044715 * x * x * x))
    return 0.5 * x * (1.0 + t), t


def _gelu_grad(x, t):
    k = math.sqrt(2.0 / math.pi)
    return 0.5 * (1.0 + t) + 0.5 * x * (1.0 - t * t) * k * (1.0 + 3 * 0.044715 * x * x)


def _halo_specs(s, tm, tn, lead):
    nb = s // HALO
    hb = tm // HALO
    pre = (lead,) if lead else ()
    z = (0,) if lead else ()
    main = pl.BlockSpec(pre + (tm, tn), lambda j, i: z + (i, j))
    prev = pl.BlockSpec(pre + (HALO, tn), lambda j, i: z + (jnp.maximum(i * hb - 1, 0), j))
    nxt = pl.BlockSpec(pre + (HALO, tn), lambda j, i: z + (jnp.minimum((i + 1) * hb, nb - 1), j))
    return [prev, main, nxt]


def _fill_ext(ext, prev, main, nxt, i, tm, s):
    ext[0:HALO, :] = jnp.where(i > 0, prev.astype(F32), 0.0)
    ext[HALO:HALO + tm, :] = main.astype(F32)
    ext[HALO + tm:2 * HALO + tm, :] = jnp.where((i + 1) * tm < s, nxt.astype(F32), 0.0)


STRIP = 16


def _shifted(ref, row0):
    n = STRIP + 16
    win = ref[pl.ds(pl.multiple_of(row0 - 8, 8), n), :]
    return pltpu.roll(win, 1, 0)[8:8 + STRIP], win[8:8 + STRIP], pltpu.roll(win, n - 1, 0)[8:8 + STRIP]


def _conv3(e, row0, w_ref, b_ref, t):
    m1, c0, p1 = _shifted(e, row0)
    return w_ref[t, 0:1, :] * m1 + w_ref[t, 1:2, :] * c0 + w_ref[t, 2:3, :] * p1 + b_ref[t]


def _conv_gate(up, cw, cb, name):
    _, s, c = up.shape
    tm = min(1024, s)
    tn = FF_SLAB

    def body(up_p, up_m, up_n, w_ref, b_ref, a_ref, eg, ev):
        i = pl.program_id(1)
        _fill_ext(eg, up_p[0], up_m[0], up_n[0], i, tm, s)
        _fill_ext(ev, up_p[1], up_m[1], up_n[1], i, tm, s)

        def strip(t, carry):
            r0 = pl.multiple_of(t * STRIP, STRIP)
            g, _ = _gelu(_conv3(eg, HALO + r0, w_ref, b_ref, 0))
            a_ref[pl.ds(r0, STRIP), :] = (g * _conv3(ev, HALO + r0, w_ref, b_ref, 1)).astype(BF16)
            return carry

        lax.fori_loop(0, tm // STRIP, strip, 0)

    return pl.pallas_call(
        body, name=name, grid=(c // tn, s // tm),
        in_specs=_halo_specs(s, tm, tn, 2)
        + [pl.BlockSpec((2, 3, tn), lambda j, i: (0, 0, j)), pl.BlockSpec((2, 1, tn), lambda j, i: (0, 0, j))],
        out_specs=pl.BlockSpec((tm, tn), lambda j, i: (i, j)),
        out_shape=jax.ShapeDtypeStruct((s, c), BF16),
        scratch_shapes=[pltpu.VMEM((tm + 2 * HALO, tn), F32)] * 2,
        compiler_params=_params(("parallel", "parallel")),
    )(up, up, up, cw, cb)


def _conv_gate_bwd(up, da, cw, cb, name):
    _, s, c = up.shape
    tm = min(512, s)
    tn = FF_SLAB
    te = tm + HALO

    def body(up_p, up_m, up_n, da_p, da_m, da_n, w_ref, b_ref, dup_ref, dw_ref, db_ref, eg, ev, ed, dug, duv):
        i = pl.program_id(1)

        @pl.when(i == 0)
        def _():
            dw_ref[...] = jnp.zeros_like(dw_ref)
            db_ref[...] = jnp.zeros_like(db_ref)

        _fill_ext(eg, up_p[0], up_m[0], up_n[0], i, tm, s)
        _fill_ext(ev, up_p[1], up_m[1], up_n[1], i, tm, s)
        _fill_ext(ed, da_p[...], da_m[...], da_n[...], i, tm, s)
        o = HALO // 2

        def du_strip(t, carry):
            r0 = pl.multiple_of(t * STRIP, STRIP)
            ug = _conv3(eg, o + r0, w_ref, b_ref, 0)
            uv = _conv3(ev, o + r0, w_ref, b_ref, 1)
            gl, th = _gelu(ug)
            dav = ed[pl.ds(pl.multiple_of(o + r0, 8), STRIP), :]
            dug[pl.ds(r0, STRIP), :] = dav * uv * _gelu_grad(ug, th)
            duv[pl.ds(r0, STRIP), :] = dav * gl
            return carry

        lax.fori_loop(0, te // STRIP, du_strip, 0)

        def back(du, e, t):
            def strip(k, acc):
                r0 = pl.multiple_of(k * STRIP, STRIP)
                dm1, c0, dp1 = _shifted(du, o + r0)
                dup_ref[t, pl.ds(r0, STRIP), :] = (w_ref[t, 0:1, :] * dp1 + w_ref[t, 1:2, :] * c0
                                                   + w_ref[t, 2:3, :] * dm1).astype(BF16)
                um1, u0, up1 = _shifted(e, HALO + r0)
                fold = lambda a: a[0:8] + a[8:16]
                return (acc[0] + fold(um1 * c0), acc[1] + fold(u0 * c0), acc[2] + fold(up1 * c0), acc[3] + fold(c0))

            zero = jnp.zeros((8, tn), F32)
            acc = lax.fori_loop(0, tm // STRIP, strip, (zero, zero, zero, zero))
            for k in range(3):
                dw_ref[t, k:k + 1, :] += jnp.sum(acc[k], axis=0, keepdims=True)
            db_ref[t] += jnp.sum(acc[3], axis=0, keepdims=True)

        back(dug, eg, 0)
        back(duv, ev, 1)

    wspec = pl.BlockSpec((2, 3, tn), lambda j, i: (0, 0, j))
    bspec = pl.BlockSpec((2, 1, tn), lambda j, i: (0, 0, j))
    return pl.pallas_call(
        body, name=name, grid=(c // tn, s // tm),
        in_specs=_halo_specs(s, tm, tn, 2) + _halo_specs(s, tm, tn, 0) + [wspec, bspec],
        out_specs=[pl.BlockSpec((2, tm, tn), lambda j, i: (0, i, j)), wspec, bspec],
        out_shape=[jax.ShapeDtypeStruct((2, s, c), BF16), jax.ShapeDtypeStruct((2, 3, c), F32),
                   jax.ShapeDtypeStruct((2, 1, c), F32)],
        scratch_shapes=[pltpu.VMEM((tm + 2 * HALO, tn), F32)] * 3 + [pltpu.VMEM((te, tn), F32)] * 2,
        compiler_params=_params(("parallel", "arbitrary")),
    )(up, up, up, da, da, da, cw, cb)


def _ffn_out(a, w_down, g_post, x1, target, name):
    s = x1.shape[0]
    tm = min(512, s)

    def body(a_ref, w_ref, g_ref, x1_ref, t_ref, dy3_ref, dx2_ref, loss_ref, dg_ref):
        i = pl.program_id(0)

        @pl.when(i == 0)
        def _():
            loss_ref[...] = jnp.zeros_like(loss_ref)
            dg_ref[...] = jnp.zeros_like(dg_ref)

        y3 = _dot(a_ref[...], w_ref[...], NN)
        g = g_ref[...]
        x2 = x1_ref[...] + y3 * _rstd(y3) * g
        diff = x2 - t_ref[...]
        loss_ref[...] += jnp.sum(jnp.sum(diff * diff, axis=1, keepdims=True), axis=0, keepdims=True)
        dx2 = diff * (1.0 / D_MODEL)
        dx2_ref[...] = dx2
        dy3, dg = _rms_bwd(dx2, y3, g)
        dy3_ref[...] = dy3.astype(BF16)
        dg_ref[...] += jnp.sum(dg, axis=0, keepdims=True)

    row = lambda n: pl.BlockSpec((tm, n), lambda i: (i, 0))
    full = lambda t: pl.BlockSpec(t.shape, lambda i: (0, 0))
    return pl.pallas_call(
        body, name=name, grid=(s // tm,),
        in_specs=[row(a.shape[1]), full(w_down), full(g_post), row(D_MODEL), row(D_MODEL)],
        out_specs=[row(D_MODEL), row(D_MODEL), pl.BlockSpec((8, 128), lambda i: (0, 0)), full(g_post)],
        out_shape=[jax.ShapeDtypeStruct((s, D_MODEL), BF16), jax.ShapeDtypeStruct((s, D_MODEL), F32),
                   jax.ShapeDtypeStruct((8, 128), F32), jax.ShapeDtypeStruct(g_post.shape, F32)],
        compiler_params=_params(("arbitrary",)),
    )(a, w_down, g_post, x1, target)


def _resnorm_bwd(dh2, x1, g_ffn_pre, dx2, y2, g_mix_post, name):
    s = x1.shape[0]
    tm = min(512, s)

    def body(dh_ref, x1_ref, gf_ref, dx2_ref, y2_ref, gp_ref, dx1_ref, dy2_ref, dgf_ref, dgp_ref):
        i = pl.program_id(0)

        @pl.when(i == 0)
        def _():
            dgf_ref[...] = jnp.zeros_like(dgf_ref)
            dgp_ref[...] = jnp.zeros_like(dgp_ref)

        dn, dgf = _rms_bwd(dh_ref[...], x1_ref[...], gf_ref[...])
        dx1 = dx2_ref[...] + dn
        dx1_ref[...] = dx1
        dgf_ref[...] += jnp.sum(dgf, axis=0, keepdims=True)
        dy2, dgp = _rms_bwd(dx1, y2_ref[...], gp_ref[...])
        dy2_ref[...] = dy2.astype(BF16)
        dgp_ref[...] += jnp.sum(dgp, axis=0, keepdims=True)

    row = pl.BlockSpec((tm, D_MODEL), lambda i: (i, 0))
    full = pl.BlockSpec((1, D_MODEL), lambda i: (0, 0))
    return pl.pallas_call(
        body, name=name, grid=(s // tm,),
        in_specs=[row, row, full, row, row, full],
        out_specs=[row, row, full, full],
        out_shape=[jax.ShapeDtypeStruct((s, D_MODEL), F32), jax.ShapeDtypeStruct((s, D_MODEL), BF16),
                   jax.ShapeDtypeStruct((1, D_MODEL), F32), jax.ShapeDtypeStruct((1, D_MODEL), F32)],
        compiler_params=_params(("arbitrary",)),
    )(dh2, x1, g_ffn_pre, dx2, y2, g_mix_post)


def _final_bwd(dh1, x, g_pre, dx1, name):
    s = x.shape[0]
    tm = min(512, s)

    def body(dh_ref, x_ref, g_ref, dx1_ref, dx_ref, dg_ref):
        @pl.when(pl.program_id(0) == 0)
        def _():
            dg_ref[...] = jnp.zeros_like(dg_ref)

        dn, dg = _rms_bwd(dh_ref[...], x_ref[...], g_ref[...])
        dx_ref[...] = dx1_ref[...] + dn
        dg_ref[...] += jnp.sum(dg, axis=0, keepdims=True)

    row = pl.BlockSpec((tm, D_MODEL), lambda i: (i, 0))
    full = pl.BlockSpec((1, D_MODEL), lambda i: (0, 0))
    return pl.pallas_call(
        body, name=name, grid=(s // tm,),
        in_specs=[row, row, full, row], out_specs=[row, full],
        out_shape=[jax.ShapeDtypeStruct((s, D_MODEL), F32), jax.ShapeDtypeStruct((1, D_MODEL), F32)],
        compiler_params=_params(("arbitrary",)),
    )(dh1, x, g_pre, dx1)


def _local_step(x, target, fw, rep, mixer_weights, early_grads, late_grads):
    s = x.shape[0]
    tabs = _rope_tables(s)
    w_in, w_uq, w_ukv, w_o = (fw[n] for n in ("w_in", "w_uq", "w_ukv", "w_o"))
    tr = min(2048, s)
    tcon = min(2048, s)

    proj, h1, qkv4, qkv16 = _in_proj(x, rep["norm_mix_pre"], w_in, "in_proj")
    qkv = {1: proj, 4: qkv4, 16: qkv16}
    q_of = lambda r: (qkv[r], lambda c: 3 * c)
    k_of = lambda r: (qkv[r], lambda c: 3 * c + 1)
    v_of = lambda r: (qkv[r], lambda c: 3 * c + 2)
    own = lambda a: (a, lambda c: c)
    biases = [_band_bias(r) for _, r in DIL_CONFIGS]
    os_, lses = [], []
    for g, (_, r) in enumerate(DIL_CONFIGS):
        o, l = _band_call("fwd", r, [q_of(r)], [k_of(r), v_of(r)], biases[g], f"band_fwd_r{r}")
        os_.append(o)
        lses.append(l)
    ya, lse_a, lse4, lse16 = _band_combine(os_, lses, "band_combine")
    qcat, kcat, kvb, cqn, ckvn = _mla_prep(proj, rep["q_lat_norm"], rep["kv_lat_norm"], w_uq, w_ukv, tabs, "mla_prep")
    yb, qaug = _mla_fwd(qcat, kcat, kvb, "mla_fwd")
    yn, y2, x1 = _mix_out(ya, yb, rep["out_norm_a"], rep["out_norm_b"], w_o, rep["norm_mix_post"], x, "mix_out")
    mw = mixer_weights(x1)
    w_up, w_down, cw, cb = mw["w_up"], mw["w_down"], mw["conv_w"], mw["conv_b"]
    ff = w_down.shape[0]
    up, h2 = _norm_matmul(x1, rep["norm_ffn_pre"], w_up, "up_proj")
    act = _conv_gate(up, cw, cb, "conv_gate")
    dy3, dx2, loss_acc, dg_ffn_post = _ffn_out(act, w_down, rep["norm_ffn_post"], x1, target, "ffn_out")

    grads = {"norm_ffn_post": dg_ffn_post}
    dact = _matmul(dy3, w_down, "nt", BF16, tr, ff // 2, D_MODEL, "d_act")
    grads["w_down"] = _matmul(act, dy3, "tn", BF16, ff // 2, D_MODEL, tcon, "dw_down")
    dup, grads["conv_w"], grads["conv_b"] = _conv_gate_bwd(up, dact, cw, cb, "conv_gate_bwd")
    half = N_DEV // 2
    dh2 = _matmul_core(
        dup, w_up, NN, (s // tr, 1, N_DEV),
        pl.BlockSpec((None, tr, FF_SLAB), lambda i, j, t: (t // half, i, t % half)),
        pl.BlockSpec((None, FF_SLAB, D_MODEL), lambda i, j, t: (t, 0, 0)),
        pl.BlockSpec((tr, D_MODEL), lambda i, j, t: (i, 0)),
        jax.ShapeDtypeStruct((s, D_MODEL), F32), (tr, D_MODEL), "d_h2")
    grads["w_up"] = _matmul_core(
        dup, h2, TN, (1, N_DEV, s // tcon),
        pl.BlockSpec((None, tcon, FF_SLAB), lambda i, j, t: (j // half, t, j % half)),
        pl.BlockSpec((tcon, D_MODEL), lambda i, j, t: (t, 0)),
        pl.BlockSpec((None, FF_SLAB, D_MODEL), lambda i, j, t: (j, 0, 0)),
        jax.ShapeDtypeStruct((N_DEV, FF_SLAB, D_MODEL), BF16), (FF_SLAB, D_MODEL), "dw_up")
    dx1, dy2, grads["norm_ffn_pre"], grads["norm_mix_post"] = _resnorm_bwd(
        dh2, x1, rep["norm_ffn_pre"], dx2, y2, rep["norm_mix_post"], "resnorm_bwd")
    dyn = _matmul(dy2, w_o, "nt", F32, tr, D_MODEL, D_MODEL, "d_yn")
    grads["w_o"] = _matmul(yn, dy2, "tn", BF16, D_MODEL, D_MODEL, tcon, "dw_o")
    token = early_grads(grads)
    dya, delta_a, doaug, grads["out_norm_a"], grads["out_norm_b"], dya4, dya16, delta4, delta16 = _outnorm_bwd(
        dyn, ya, yb, rep["out_norm_a"] + token, rep["out_norm_b"], _head_ones(), "outnorm_bwd")
    stats = {1: (dya, lse_a, delta_a), 4: (dya4, lse4, delta4), 16: (dya16, lse16, delta16)}
    parts = []
    for g, (_, r) in enumerate(DIL_CONFIGS):
        qside = [q_of(r)] + [own(a) for a in stats[r]]
        kside = [k_of(r), v_of(r)]
        (dq,) = _band_call("dq", r, qside, kside, biases[g], f"band_dq_r{r}")
        dk, dv = _band_call("dkv", r, kside, qside, biases[g], f"band_dkv_r{r}")
        parts.append((dq, dk, dv))
    dproj_a = _sum_cast(parts, "band_grad_sum")
    dqc, dkc, dvp = _mla_bwd(qaug, kcat, kvb, doaug, "mla_bwd")
    dproj_b, grads["w_uq"], grads["w_ukv"], grads["q_lat_norm"], grads["kv_lat_norm"] = _mla_prep_bwd(
        dqc, dkc, dvp, proj, cqn, ckvn, rep["q_lat_norm"], rep["kv_lat_norm"], w_uq, w_ukv, tabs, "mla_prep_bwd")
    dproj = jnp.concatenate([dproj_a, dproj_b], axis=1)
    grads["w_in"] = _matmul(dproj, h1, "tn", BF16, N_PROJ // 2, D_MODEL, tcon, "dw_in")
    token = late_grads(grads)
    dh1 = _matmul(dproj, w_in, "nn", F32, tr, D_MODEL, N_PROJ // 2, "d_h1")
    grad_x, grads["norm_mix_pre"] = _final_bwd(dh1, x, rep["norm_mix_pre"] + token, dx1, "final_bwd")
    loss = 0.5 / D_MODEL * loss_acc[0, 0]
    return loss, grad_x, grads


MESH = pl.DeviceIdType.MESH
HBM_SPEC = pl.BlockSpec(memory_space=pltpu.HBM)
SMALL_ROWS = 96
TRANSPOSED = ("w_in", "w_up", "w_uq")
BUF_SHAPES = {"w_up": (FF_SLAB, D_MODEL), "w_in": (D_IN // N_DEV, D_MODEL), "w_down": (D_FF // N_DEV, D_MODEL),
              "w_o": (D_MODEL // N_DEV, D_MODEL), "w_uq": (QK_NOPE + QK_ROPE, Q_LORA), "w_ukv": (KV_LORA, 128),
              "conv_w": (8, FF_SLAB)}
BUF_ORDER = tuple(BUF_SHAPES)
MIXING = ("w_in", "w_o", "w_uq", "w_ukv")
MIXER = ("w_up", "w_down", "conv_w")
EARLY_GRADS = ("w_up", "w_down", "w_o", "conv_w")
LATE_GRADS = ("w_in", "w_uq", "w_ukv")


def _all_gather(bufs, name):
    nb = len(bufs)

    def body(*refs):
        x_refs, out_refs = refs[:nb], refs[nb:2 * nb]
        send_sems, recv_sems, local_sems = refs[2 * nb:]
        x, y, c = lax.axis_index("x"), lax.axis_index("y"), lax.axis_index("c")
        me, sibling = (x, y, c), (x, y, 1 - c)
        chips = [(1 - x, y), (x, 1 - y), (1 - x, 1 - y)]

        def copy(b, k, block, to, own=False):
            px, py, pc = block
            slot = out_refs[b].at[4 * px + 2 * py + pc]
            return pltpu.make_async_remote_copy(
                src_ref=x_refs[b] if own else slot, dst_ref=slot,
                send_sem=send_sems.at[7 * b + k], recv_sem=recv_sems.at[7 * b + k], device_id=to, device_id_type=MESH)

        mine = [pltpu.make_async_copy(x_refs[b], out_refs[b].at[4 * x + 2 * y + c], local_sems.at[b]) for b in range(nb)]
        sends = []
        for b in range(nb):
            mine[b].start()
            first = [copy(b, 0, me, sibling, own=True)]
            first += [copy(b, 1 + j, me, (*chip, c), own=True) for j, chip in enumerate(chips)]
            for cp in first:
                cp.start()
            sends += first
        for j, chip in enumerate(chips):
            for b in range(nb):
                copy(b, 1 + j, (*chip, c), me).wait_recv()
                passed = copy(b, 4 + j, (*chip, c), sibling)
                passed.start()
                sends.append(passed)
        for b in range(nb):
            copy(b, 0, sibling, me).wait_recv()
            for j, chip in enumerate(chips):
                copy(b, 4 + j, (*chip, 1 - c), me).wait_recv()
        for cp in sends:
            cp.wait_send()
        for cp in mine:
            cp.wait()

    return pl.pallas_call(
        body, name=name,
        out_shape=[jax.ShapeDtypeStruct((N_DEV,) + p.shape, p.dtype) for p in bufs],
        in_specs=[HBM_SPEC] * nb, out_specs=[HBM_SPEC] * nb,
        scratch_shapes=[pltpu.SemaphoreType.DMA((7 * nb,)), pltpu.SemaphoreType.DMA((7 * nb,)),
                        pltpu.SemaphoreType.DMA((nb,))],
    )(*bufs)


def _grad_exchange(bigs, small, name):
    flips = [(fx, fy, fc) for fx in (0, 1) for fy in (0, 1) for fc in (0, 1)][1:]
    nb = len(bigs)

    def body(*refs):
        big_refs, small_ref = refs[:nb], refs[nb]
        rbig_refs, rsmall_ref = refs[nb + 1:2 * nb + 1], refs[2 * nb + 1]
        send_sems, recv_sems, local_sems = refs[2 * nb + 2:]
        x, y, c = lax.axis_index("x"), lax.axis_index("y"), lax.axis_index("c")
        my = 4 * x + 2 * y + c
        own = [pltpu.make_async_copy(big_refs[b].at[my], rbig_refs[b].at[my], local_sems.at[b]) for b in range(nb)]
        own.append(pltpu.make_async_copy(small_ref, rsmall_ref.at[my], local_sems.at[nb]))
        for cp in own:
            cp.start()
        copies = []
        for b in range(nb + 1):
            for k, (fx, fy, fc) in enumerate(flips):
                px = 1 - x if fx else x
                py = 1 - y if fy else y
                pc = 1 - c if fc else c
                src = small_ref if b == nb else big_refs[b].at[4 * px + 2 * py + pc]
                dst = rsmall_ref.at[my] if b == nb else rbig_refs[b].at[my]
                copies.append(pltpu.make_async_remote_copy(
                    src_ref=src, dst_ref=dst, send_sem=send_sems.at[7 * b + k], recv_sem=recv_sems.at[7 * b + k],
                    device_id=(px, py, pc), device_id_type=MESH))
        for cp in copies:
            cp.start()
        for cp in copies:
            cp.wait()
        for cp in own:
            cp.wait()

    nsem = 7 * (nb + 1)
    return pl.pallas_call(
        body, name=name,
        out_shape=[jax.ShapeDtypeStruct(b.shape, b.dtype) for b in bigs]
        + [jax.ShapeDtypeStruct((N_DEV,) + small.shape, small.dtype)],
        in_specs=[HBM_SPEC] * (nb + 1), out_specs=[HBM_SPEC] * (nb + 1),
        scratch_shapes=[pltpu.SemaphoreType.DMA((nsem,)), pltpu.SemaphoreType.DMA((nsem,)),
                        pltpu.SemaphoreType.DMA((nb + 1,))],
    )(*bigs, small)


SEM_SPEC = pl.BlockSpec(memory_space=pltpu.SEMAPHORE)
ANY_SPEC = pl.BlockSpec(memory_space=pl.ANY)
FLIPS = tuple((fx, fy, fc) for fx in (0, 1) for fy in (0, 1) for fc in (0, 1))[1:]


def _split_copies(src_refs, land_refs, send_sems, recv_sems, scatter):
    x, y, c = lax.axis_index("x"), lax.axis_index("y"), lax.axis_index("c")
    my = 4 * x + 2 * y + c
    copies = []
    for b, (src, land) in enumerate(zip(src_refs, land_refs)):
        for k, (fx, fy, fc) in enumerate(FLIPS):
            px = 1 - x if fx else x
            py = 1 - y if fy else y
            pc = 1 - c if fc else c
            copies.append(pltpu.make_async_remote_copy(
                src_ref=src.at[4 * px + 2 * py + pc] if scatter else src, dst_ref=land.at[my],
                send_sem=send_sems.at[7 * b + k], recv_sem=recv_sems.at[7 * b + k],
                device_id=(px, py, pc), device_id_type=MESH))
    return copies


def _exchange_start(srcs, scatter, name):
    nb = len(srcs)
    lands = [lax.empty(s.shape if scatter else (N_DEV,) + s.shape, s.dtype) for s in srcs]

    def body(*refs):
        src_refs, land_refs = refs[:nb], refs[nb:2 * nb]
        send_sems, recv_sems = refs[2 * nb], refs[2 * nb + 1]
        token = refs[-1]
        for cp in _split_copies(src_refs, land_refs, send_sems, recv_sems, scatter):
            cp.start()
        token[...] = jnp.zeros_like(token)

    hbm = lambda a: pltpu.HBM(a.shape, a.dtype)
    outs = pl.pallas_call(
        body, name=name,
        out_shape=(pltpu.SemaphoreType.DMA((7 * nb,)), pltpu.SemaphoreType.DMA((7 * nb,)),
                   *[hbm(a) for a in srcs], *[hbm(a) for a in lands], jax.ShapeDtypeStruct((8, 128), F32)),
        in_specs=[HBM_SPEC] * (2 * nb),
        out_specs=(SEM_SPEC, SEM_SPEC, *[HBM_SPEC] * (2 * nb), pl.BlockSpec(memory_space=pltpu.VMEM)),
        input_output_aliases={i: 2 + i for i in range(2 * nb)},
        compiler_params=pltpu.CompilerParams(has_side_effects=pltpu.SideEffectType.DATAFLOW_SIDE_EFFECTING),
    )(*[pltpu.with_memory_space_constraint(a, pltpu.HBM) for a in srcs],
      *[pltpu.with_memory_space_constraint(a, pltpu.HBM) for a in lands])
    return outs[0], outs[1], list(outs[2:2 + nb]), list(outs[2 + nb:2 + 2 * nb]), outs[-1]


def _exchange_wait(started, scatter, after, name):
    send_sems, recv_sems, srcs, lands, _ = started
    nb = len(srcs)

    def body(*refs):
        src_refs, land_refs = refs[:nb], refs[nb:2 * nb]
        for cp in _split_copies(src_refs, land_refs, refs[2 * nb], refs[2 * nb + 1], scatter):
            cp.wait_send()
            cp.wait_recv()

    hbm = lambda a: pltpu.HBM(a.shape, a.dtype)
    outs = pl.pallas_call(
        body, name=name,
        out_shape=(*[hbm(a) for a in srcs], *[hbm(a) for a in lands]),
        in_specs=[HBM_SPEC] * (2 * nb) + [SEM_SPEC, SEM_SPEC, ANY_SPEC],
        out_specs=tuple([HBM_SPEC] * (2 * nb)),
        input_output_aliases={i: i for i in range(2 * nb)},
        compiler_params=pltpu.CompilerParams(has_side_effects=pltpu.SideEffectType.DATAFLOW_SIDE_EFFECTING),
    )(*srcs, *lands, send_sems, recv_sems, after)
    return list(outs[:nb]), list(outs[nb:])


def _own_slot(land, own):
    my = 4 * lax.axis_index("x") + 2 * lax.axis_index("y") + lax.axis_index("c")
    return lax.dynamic_update_slice(land, own[None], (my,) + (0,) * own.ndim)


def _adamw(parts, w, m, v, name):
    rows, n = w.shape
    tm = rows if rows <= 384 else next(t for t in (256, 176) if rows % t == 0)
    assert rows % tm == 0

    def body(p_ref, w_ref, m_ref, v_ref, g_ref, d_ref, m2_ref, v2_ref):
        g = p_ref[0, :, 0:n].astype(F32)
        for s in range(1, N_DEV):
            g = g + p_ref[s, :, 0:n].astype(F32)
        g_ref[...] = g
        m2 = ADAM_B1 * m_ref[...] + (1.0 - ADAM_B1) * g
        v2 = ADAM_B2 * v_ref[...] + (1.0 - ADAM_B2) * jnp.square(g)
        m2_ref[...] = m2
        v2_ref[...] = v2
        m_hat = m2 / (1.0 - ADAM_B1 ** ADAM_STEP)
        v_hat = v2 / (1.0 - ADAM_B2 ** ADAM_STEP)
        d_ref[...] = -ADAM_LR * (m_hat / (jnp.sqrt(v_hat) + ADAM_EPS) + ADAM_WD * w_ref[...])

    row = pl.BlockSpec((tm, n), lambda i: (i, 0))
    return pl.pallas_call(
        body, name=name, grid=(rows // tm,),
        in_specs=[pl.BlockSpec((N_DEV, tm, parts.shape[2]), lambda i: (0, i, 0)), row, row, row],
        out_specs=[row] * 4,
        out_shape=[jax.ShapeDtypeStruct((rows, n), F32)] * 4,
        compiler_params=_params(("parallel",)),
    )(parts, w, m, v)


def _pack(flat_parts, rows):
    flat = jnp.concatenate(flat_parts, axis=-1)
    pad = rows * LANES - flat.shape[-1]
    flat = jnp.pad(flat, [(0, 0)] * (flat.ndim - 1) + [(0, pad)])
    return flat.reshape(flat.shape[:-1] + (rows, LANES))


def _unpack(packed, shapes):
    flat = packed.reshape(packed.shape[:-2] + (-1,))
    out, off = {}, 0
    for name, shape in shapes.items():
        n = int(np.prod(shape))
        out[name] = flat[..., off:off + n].reshape(flat.shape[:-1] + tuple(shape))
        off += n
    return out


def _pad_to(a, shape):
    return jnp.pad(a, [(0, t - d) for d, t in zip(a.shape, shape)])


def _pad_w_in(w):
    k = w.shape[1]
    z = lambda n: jnp.zeros((n, k), w.dtype)
    return jnp.concatenate([w[:COL_KR], z(64), w[COL_KR:], z(32)], axis=0)


def _unpad_w_in(w):
    return jnp.concatenate([w[:COL_KR], w[COL_KR + 64:COL_KR + 96]], axis=0)


def _assemble_weights(g, conv_b):
    half = N_DEV // 2
    cols = lambda a: a.transpose(1, 0, 2).reshape(a.shape[1], N_DEV * a.shape[2])
    make = {
        "w_in": lambda: _pad_w_in(g["w_in"].reshape(D_IN, D_MODEL)),
        "w_uq": lambda: _pad_to(g["w_uq"], (N_DEV, 128, Q_LORA)).reshape(N_DEV * 128, Q_LORA),
        "w_ukv": lambda: cols(g["w_ukv"]),
        "w_o": lambda: g["w_o"].reshape(D_MODEL, D_MODEL),
        "w_up": lambda: g["w_up"],
        "w_down": lambda: _pad_to(g["w_down"].reshape(half, FF_SHARD, D_MODEL),
                                  (half, FF_SLAB, D_MODEL)).reshape(half * FF_SLAB, D_MODEL),
        "conv_w": lambda: g["conv_w"][:, :3].reshape(2, half, 3, FF_SLAB).transpose(0, 2, 1, 3).reshape(2, 3, half * FF_SLAB),
    }
    fw = {n: make[n]() for n in g}
    if conv_b is not None:
        fw["conv_b"] = _pad_to(conv_b.reshape(2, 1, half, FF_SHARD), (2, 1, half, FF_SLAB)).reshape(2, 1, half * FF_SLAB)
    return fw


def _grad_bufs(grads, names):
    half = N_DEV // 2
    slabs = lambda a: a.reshape(a.shape[0], N_DEV, a.shape[1] // N_DEV).transpose(1, 0, 2)
    make = {
        "w_in": lambda: _unpad_w_in(grads["w_in"]).reshape((N_DEV,) + BUF_SHAPES["w_in"]),
        "w_uq": lambda: grads["w_uq"].reshape(N_DEV, 128, Q_LORA)[:, :QK_NOPE + QK_ROPE],
        "w_ukv": lambda: slabs(grads["w_ukv"]),
        "w_o": lambda: grads["w_o"].reshape((N_DEV,) + BUF_SHAPES["w_o"]),
        "w_up": lambda: grads["w_up"],
        "w_down": lambda: grads["w_down"].reshape(half, FF_SLAB, D_MODEL)[:, :FF_SHARD].reshape((N_DEV,) + BUF_SHAPES["w_down"]),
        "conv_w": lambda: _pad_to(grads["conv_w"].reshape(2, 3, half, FF_SLAB).transpose(0, 2, 1, 3).reshape(N_DEV, 3, FF_SLAB),
                                  (N_DEV,) + BUF_SHAPES["conv_w"]),
    }
    return [make[n]() if n == "conv_w" else make[n]().astype(BF16) for n in names]


def kernel(x, norm_mix_pre, w_in, q_lat_norm, w_uq, kv_lat_norm, w_ukv, out_norm_a, out_norm_b, w_o, norm_mix_post, norm_ffn_pre, w_up, conv_w, conv_b, w_down, norm_ffn_post, loss_target, m_norm_mix_pre, m_w_in, m_q_lat_norm, m_w_uq, m_kv_lat_norm, m_w_ukv, m_out_norm_a, m_out_norm_b, m_w_o, m_norm_mix_post, m_norm_ffn_pre, m_w_up, m_conv_w, m_conv_b, m_w_down, m_norm_ffn_post, v_norm_mix_pre, v_w_in, v_q_lat_norm, v_w_uq, v_kv_lat_norm, v_w_ukv, v_out_norm_a, v_out_norm_b, v_w_o, v_norm_mix_post, v_norm_ffn_pre, v_w_up, v_conv_w, v_conv_b, v_w_down, v_norm_ffn_post):
    given = dict(locals())
    shard = lambda a, n: a[0].T if n in TRANSPOSED else a[0]
    w = {n: shard(given[n], n) for n in WEIGHTS}
    m = {n: shard(given["m_" + n], n) for n in WEIGHTS}
    v = {n: shard(given["v_" + n], n) for n in WEIGHTS}
    rep_shapes = {n: w[n].shape for n in REPLICATED}

    buf = lambda n: _pad_to(w[n] if n == "conv_w" else w[n].astype(BF16), BUF_SHAPES[n])
    first = dict(zip(MIXING, _all_gather([buf(n) for n in MIXING], "weight_all_gather")))
    fw = _assemble_weights(first, None)
    tie = first["w_o"][0, 0, 0].astype(F32) * 0.0
    late_bufs = [buf(n) + tie.astype(w[n].dtype if n == "conv_w" else BF16) for n in MIXER]
    mixer_started = _exchange_start(late_bufs, False, "mixer_weights_start")
    rep = {n: given[n] for n in REPLICATED}
    rep["norm_mix_pre"] = rep["norm_mix_pre"] + mixer_started[4][0, 0]

    def mixer_weights(after):
        srcs, lands = _exchange_wait(mixer_started, False, after, "mixer_weights_wait")
        got = {n: _own_slot(land, own) for n, land, own in zip(MIXER, lands, srcs)}
        return _assemble_weights(got, conv_b)

    early = {}

    def early_grads(grads):
        early["started"] = _exchange_start(_grad_bufs(grads, EARLY_GRADS), True, "early_grads_start")
        return early["started"][4][0, 0]

    def late_grads(grads):
        early["late"] = _exchange_start(_grad_bufs(grads, LATE_GRADS), True, "late_grads_start")
        return early["late"][4][0, 0]

    loss_local, grad_x, grads = _local_step(x[0], loss_target[0], fw, rep, mixer_weights, early_grads, late_grads)

    grads["conv_b"] = grads["conv_b"].reshape(N_DEV, FF_SLAB)[:, :FF_SHARD]
    small = _pack([grads[n].reshape(-1) for n in REPLICATED] + [loss_local.reshape(1)], SMALL_ROWS)
    received_small = _grad_exchange([], small, "grad_exchange")[0]
    my = 4 * lax.axis_index("x") + 2 * lax.axis_index("y") + lax.axis_index("c")
    received = {}
    for names, key, tag in ((EARLY_GRADS, "started", "early_grads_wait"), (LATE_GRADS, "late", "late_grads_wait")):
        srcs, lands = _exchange_wait(early[key], True, received_small, tag)
        for n, land, src in zip(names, lands, srcs):
            received[n] = _own_slot(land, lax.dynamic_index_in_dim(src, my, 0, keepdims=False))
    results = [{}, {}, {}, {}]
    for n in BUF_ORDER:
        parts = received[n]
        if n == "conv_w":
            args = [_pad_to(t[n], BUF_SHAPES[n]) for t in (w, m, v)]
        else:
            args = [w[n], m[n], v[n]]
        outs = _adamw(parts, *args, f"adamw_{n}")
        for t in range(4):
            results[t][n] = outs[t][:w[n].shape[0], :w[n].shape[1]] if n == "conv_w" else outs[t]
    pk = lambda d: _pack([d[n].reshape(-1) for n in REPLICATED] + [jnp.zeros((1,), F32)], SMALL_ROWS)
    small_out = _adamw(received_small, pk(w), pk(m), pk(v), "adamw_replicated")
    rep_shapes["loss"] = (1,)
    for t in range(4):
        results[t].update(_unpack(small_out[t], rep_shapes))

    loss = results[0]["loss"][0]
    outs = [loss, grad_x[None]]
    for res in results:
        outs += [(res[n].T if n in TRANSPOSED else res[n])[None] for n in WEIGHTS]
    return tuple(outs)
```

```python
import math

import numpy as np
import jax
import jax.numpy as jnp
from jax import lax
from jax.experimental import pallas as pl
from jax.experimental.pallas import tpu as pltpu

F32 = jnp.float32
BF16 = jnp.bfloat16

D_MODEL = 1024
N_DEV = 8
WIDTH_A = 512
N_HEADS = 8
Q_LORA = 384
KV_LORA = 256
QK_ROPE = 32
QK_NOPE = 64
D_FF = 2816
FF_SHARD = 2 * D_FF // N_DEV
FF_SLAB = 768
DIL_CONFIGS = ((128, 1), (512, 4), (2048, 16))
BAND_HALF = 64
ROPE_BASE = 10000.0
EPS = 1e-6
NEG = -1e30
MLA_SCALE = (QK_NOPE + QK_ROPE) ** -0.5
SCALE_A = 0.125
LOG2E = 1.0 / math.log(2.0)
LN2 = math.log(2.0)
QSCALE_A = SCALE_A * LOG2E

COL_CQ = 3 * WIDTH_A
COL_CKV = COL_CQ + Q_LORA
COL_KR = COL_CKV + KV_LORA
N_PROJ = COL_KR + 128
N_LAT = N_PROJ - COL_CQ
D_IN = COL_KR + QK_ROPE

ADAM_LR = 0.001
ADAM_B1 = 0.9
ADAM_B2 = 0.999
ADAM_EPS = 1e-08
ADAM_WD = 0.01
ADAM_STEP = 10

LANES = 128
VMEM_LIMIT = 56 * 1024 * 1024

REPLICATED = ("norm_mix_pre", "q_lat_norm", "kv_lat_norm", "out_norm_a", "out_norm_b", "norm_mix_post",
              "norm_ffn_pre", "conv_b", "norm_ffn_post")
WEIGHTS = ("norm_mix_pre", "w_in", "q_lat_norm", "w_uq", "kv_lat_norm", "w_ukv", "out_norm_a", "out_norm_b", "w_o",
           "norm_mix_post", "norm_ffn_pre", "w_up", "conv_w", "conv_b", "w_down", "norm_ffn_post")


def _params(sem=None):
    return pltpu.CompilerParams(dimension_semantics=sem, vmem_limit_bytes=VMEM_LIMIT)


def _dot(a, b, dims):
    return lax.dot_general(a, b, (dims, ((), ())), preferred_element_type=F32)


NN = ((1,), (0,))
NT = ((1,), (1,))
TN = ((0,), (0,))


def _rstd(x):
    return lax.rsqrt(jnp.mean(x * x, axis=-1, keepdims=True) + EPS)


def _rms_bwd(dy, x, g):
    r = _rstd(x)
    z = x * r
    gy = dy * g
    dx = r * (gy - z * jnp.mean(gy * z, axis=-1, keepdims=True))
    return dx, dy * z


def _split_hi_lo(v):
    hi = v.astype(BF16)
    lo = (v - hi.astype(F32)).astype(BF16)
    return hi, lo


def _matmul(a, b, mode, out_dtype, tm, tn, tk, name):
    if mode == "nn":
        (m, k), n = a.shape, b.shape[1]
        a_spec = pl.BlockSpec((tm, tk), lambda i, j, s: (i, s))
        b_spec = pl.BlockSpec((tk, tn), lambda i, j, s: (s, j))
        dims = NN
    elif mode == "nt":
        (m, k), n = a.shape, b.shape[0]
        a_spec = pl.BlockSpec((tm, tk), lambda i, j, s: (i, s))
        b_spec = pl.BlockSpec((tn, tk), lambda i, j, s: (j, s))
        dims = NT
    else:
        (k, m), n = a.shape, b.shape[1]
        a_spec = pl.BlockSpec((tk, tm), lambda i, j, s: (s, i))
        b_spec = pl.BlockSpec((tk, tn), lambda i, j, s: (s, j))
        dims = TN
    assert m % tm == 0 and n % tn == 0 and k % tk == 0, (name, m, n, k, tm, tn, tk)
    return _matmul_core(a, b, dims, (m // tm, n // tn, k // tk), a_spec, b_spec,
                        pl.BlockSpec((tm, tn), lambda i, j, s: (i, j)), jax.ShapeDtypeStruct((m, n), out_dtype),
                        (tm, tn), name)


def _matmul_core(a, b, dims, grid, a_spec, b_spec, o_spec, out_sds, acc_shape, name):
    nk = grid[2]

    def body(a_ref, b_ref, o_ref, acc_ref):
        s = pl.program_id(2)

        @pl.when(s == 0)
        def _():
            acc_ref[...] = jnp.zeros_like(acc_ref)

        acc_ref[...] += _dot(a_ref[...].astype(BF16), b_ref[...].astype(BF16), dims)

        @pl.when(s == nk - 1)
        def _():
            o_ref[...] = acc_ref[...].astype(out_sds.dtype)

    return pl.pallas_call(
        body, name=name, grid=grid, in_specs=[a_spec, b_spec], out_specs=o_spec, out_shape=out_sds,
        scratch_shapes=[pltpu.VMEM(acc_shape, F32)],
        compiler_params=_params(("parallel", "parallel", "arbitrary")),
    )(a, b)


def _norm_matmul(x, g, w, name):
    s, k = x.shape
    tm = min(2048, s)
    nj, tn, _ = w.shape
    half = nj // 2
    w_spec = pl.BlockSpec((None, tn, k), lambda i, j: (j, 0, 0))
    o_spec = pl.BlockSpec((None, tm, tn), lambda i, j: (j // half, i, j % half))
    o_sds = jax.ShapeDtypeStruct((2, s, half * tn), BF16)

    def body(x_ref, g_ref, w_ref, o_ref, h_ref):
        @pl.when(pl.program_id(1) == 0)
        def _():
            xv = x_ref[...]
            h_ref[...] = (xv * _rstd(xv) * g_ref[...]).astype(BF16)

        o_ref[...] = _dot(h_ref[...], w_ref[...], NT).astype(BF16)

    return pl.pallas_call(
        body, name=name, grid=(s // tm, nj),
        in_specs=[pl.BlockSpec((tm, k), lambda i, j: (i, 0)),
                  pl.BlockSpec((1, k), lambda i, j: (0, 0)),
                  w_spec],
        out_specs=[o_spec, pl.BlockSpec((tm, k), lambda i, j: (i, 0))],
        out_shape=[o_sds, jax.ShapeDtypeStruct((s, k), BF16)],
        compiler_params=_params(("parallel", "arbitrary")),
    )(x, g, w)


def _band_bias(r):
    off = np.arange(256)[None, :] - BAND_HALF - np.arange(128)[:, None]
    slopes = np.exp2(-8.0 * np.arange(1, N_HEADS + 1, dtype=np.float32) / N_HEADS).astype(np.float32)
    dist = (np.abs(off) * r).astype(np.float32)
    bias = -slopes[:, None, None] * dist[None]
    bias = np.where((np.abs(off) <= BAND_HALF)[None], bias * np.float32(LOG2E), np.float32(NEG))
    return jnp.asarray(bias, F32)


def _band_call(mode, r, center, window, bias, name):
    seq = center[0][0].shape[0]
    tq = min(512, seq)
    nsub = tq // 128
    hb = tq // BAND_HALF
    nh = seq // BAND_HALF
    nc, nw = len(center), len(window)
    out_dtypes = {"fwd": (BF16, F32), "dq": (BF16,), "dkv": (BF16, BF16)}[mode]
    n_out = len(out_dtypes)

    def specs(col):
        return (pl.BlockSpec((BAND_HALF, WIDTH_A), lambda c, i: (jnp.maximum(i * hb - 1, 0), col(c))),
                pl.BlockSpec((tq, WIDTH_A), lambda c, i: (i, col(c))),
                pl.BlockSpec((BAND_HALF, WIDTH_A), lambda c, i: (jnp.minimum((i + 1) * hb, nh - 1), col(c))))

    cspec = pl.BlockSpec((tq, WIDTH_A), lambda c, i: (i, c))
    in_specs = [specs(col)[1] for _, col in center]
    operands = [a for a, _ in center]
    for a, col in window:
        in_specs += list(specs(col))
        operands += [a, a, a]
    in_specs.append(pl.BlockSpec((N_HEADS, 128, 256), lambda c, i: (0, 0, 0)))
    operands.append(bias)
    window = [a for a, _ in window]

    def aug_stat(base, stat_sw, lane, act, e0):
        hi, lo = _split_hi_lo(stat_sw)
        return jnp.where(act, base, jnp.where(lane == e0, -hi, jnp.where(lane == e0 + 1, -lo, jnp.zeros_like(hi))))

    def aug_ones(base, lane, e0):
        return jnp.where((lane == e0) | (lane == e0 + 1), jnp.ones_like(base), base)

    def body(*refs):
        c_refs = refs[:nc]
        w_refs = refs[nc:nc + 3 * nw]
        bias_ref = refs[nc + 3 * nw]
        o_refs = refs[nc + 3 * nw + 1:nc + 3 * nw + 1 + n_out]
        wins = refs[nc + 3 * nw + 1 + n_out:]
        i = pl.program_id(1)
        for t in range(nw):
            wins[t][0:BAND_HALF, :] = w_refs[3 * t][...]
            wins[t][BAND_HALF:BAND_HALF + tq, :] = w_refs[3 * t + 1][...]
            wins[t][BAND_HALF + tq:BAND_HALF + tq + BAND_HALF, :] = w_refs[3 * t + 2][...]

        def sub(j, carry):
            r0 = pl.multiple_of(j * 128, 128)
            wpos = i * tq + j * 128 - BAND_HALF + lax.broadcasted_iota(jnp.int32, (128, 256), 1)
            valid = (wpos >= 0) & (wpos < seq)
            lane_c = lax.broadcasted_iota(jnp.int32, (128, 128), 1)
            lane_w = lax.broadcasted_iota(jnp.int32, (256, 128), 1)
            heads = [(p, a) for p in range(4) for a in range(2)]
            first, last_ops = [], []
            for p, a in heads:
                cols = slice(p * 128, (p + 1) * 128)
                cs = [c[pl.ds(r0, 128), cols] for c in c_refs]
                ws = [w[pl.ds(r0, 256), cols] for w in wins]
                e0 = 64 if a == 0 else 0
                act_c = (lane_c < 64) if a == 0 else (lane_c >= 64)
                act_w = (lane_w < 64) if a == 0 else (lane_w >= 64)
                bias_a = bias_ref[2 * p + a]
                if mode == "fwd":
                    qa = jnp.where(act_c, cs[0] * QSCALE_A, jnp.zeros_like(cs[0]))
                    first.append((_dot(qa, ws[0], NT) + bias_a, None))
                    last_ops.append((ws[1],))
                elif mode == "dq":
                    q2, dy2, l2, d2 = cs
                    k2, v2 = ws
                    q_aug = aug_stat(q2 * QSCALE_A, pltpu.roll(l2, 64, 1), lane_c, act_c, e0)
                    dy_aug = aug_stat(dy2, pltpu.roll(d2, 64, 1), lane_c, act_c, e0)
                    first.append((_dot(q_aug, aug_ones(k2, lane_w, e0), NT) + bias_a,
                                  _dot(dy_aug, aug_ones(v2, lane_w, e0), NT)))
                    last_ops.append((k2,))
                else:
                    k2, v2 = cs
                    q2, dy2, l2, d2 = ws
                    q_aug = aug_stat(q2 * QSCALE_A, pltpu.roll(l2, 64, 1), lane_w, act_w, e0)
                    dy_aug = aug_stat(dy2, pltpu.roll(d2, 64, 1), lane_w, act_w, e0)
                    first.append((_dot(aug_ones(k2, lane_c, e0), q_aug, NT) + bias_a,
                                  _dot(aug_ones(v2, lane_c, e0), dy_aug, NT)))
                    last_ops.append((q_aug, dy_aug))
            mid = []
            for sc, dp in first:
                sc = jnp.where(valid, sc, NEG)
                if mode == "fwd":
                    m = jnp.max(sc, axis=-1, keepdims=True)
                    e = jnp.exp2(sc - m)
                    l = jnp.sum(e, axis=-1, keepdims=True)
                    mid.append((e.astype(BF16), l, m + jnp.log(l) * LOG2E))
                else:
                    pr = jnp.exp2(sc)
                    mid.append((pr.astype(BF16), (pr * dp).astype(BF16)))
            res = []
            for md, ops in zip(mid, last_ops):
                if mode == "fwd":
                    res.append((_dot(md[0], ops[0], NN) / md[1], jnp.broadcast_to(md[2], (128, 128))))
                elif mode == "dq":
                    res.append((_dot(md[1], ops[0], NN) * SCALE_A,))
                else:
                    res.append((_dot(md[1], ops[0], NN) * LN2, _dot(md[0], ops[1], NN)))
            for t in range(n_out):
                pairs = [jnp.where(lane_c < 64, res[2 * p][t], res[2 * p + 1][t]) for p in range(4)]
                o_refs[t][pl.ds(r0, 128), :] = jnp.concatenate(pairs, axis=1).astype(out_dtypes[t])
            return carry

        lax.fori_loop(0, nsub, sub, 0, unroll=True)

    outs = pl.pallas_call(
        body, name=name, grid=(r, seq // tq),
        in_specs=in_specs,
        out_specs=[cspec] * n_out,
        out_shape=[jax.ShapeDtypeStruct((seq, r * WIDTH_A), dt) for dt in out_dtypes],
        scratch_shapes=[pltpu.VMEM((tq + 2 * BAND_HALF, WIDTH_A), w.dtype) for w in window],
        compiler_params=_params(("parallel", "parallel")),
    )(*operands)
    return outs


def _slab_scratch(tm, w):
    return pltpu.VMEM((w // 128, tm, 128), F32)


def _put(scr, val):
    for j in range(scr.shape[0]):
        scr[j] = val[:, j * 128:(j + 1) * 128].astype(F32)


def _get(scr):
    return jnp.concatenate([scr[j] for j in range(scr.shape[0])], axis=1)


def _dilate_store(dst_ref, scr, r):
    nb, tm, _ = scr.shape
    w = nb * 128
    for c in range(r):
        for j in range(nb):
            dst_ref[:, c * w + j * 128:c * w + (j + 1) * 128] = scr[j, pl.ds(c, tm // r, stride=r), :].astype(dst_ref.dtype)


def _undilate(scr, src_ref, r):
    nb, tm, _ = scr.shape
    w = nb * 128
    for c in range(r):
        for j in range(nb):
            scr[j, pl.ds(c, tm // r, stride=r), :] = src_ref[:, c * w + j * 128:c * w + (j + 1) * 128].astype(F32)


def _dil_spec(tm, r, w):
    return pl.BlockSpec((tm // r, r * w), lambda i: (i, 0))


def _dil_shape(s, r, w, dtype):
    return jax.ShapeDtypeStruct((s // r, r * w), dtype)


def _in_proj(x, g, w, name):
    s, k = x.shape
    n = w.shape[0]
    tm = min(512, s)
    qkv = 3 * WIDTH_A

    def body(x_ref, g_ref, w_ref, o_ref, h_ref, d4_ref, d16_ref, scr):
        xv = x_ref[...]
        h = (xv * _rstd(xv) * g_ref[...]).astype(BF16)
        h_ref[...] = h
        acc = _dot(h, w_ref[...], NT)
        o_ref[...] = acc.astype(BF16)
        _put(scr, acc[:, 0:qkv])
        _dilate_store(d4_ref, scr, 4)
        _dilate_store(d16_ref, scr, 16)

    row = lambda c: pl.BlockSpec((tm, c), lambda i: (i, 0))
    return pl.pallas_call(
        body, name=name, grid=(s // tm,),
        in_specs=[row(k), pl.BlockSpec((1, k), lambda i: (0, 0)), pl.BlockSpec((n, k), lambda i: (0, 0))],
        out_specs=[row(n), row(k), _dil_spec(tm, 4, qkv), _dil_spec(tm, 16, qkv)],
        out_shape=[jax.ShapeDtypeStruct((s, n), BF16), jax.ShapeDtypeStruct((s, k), BF16),
                   _dil_shape(s, 4, qkv, BF16), _dil_shape(s, 16, qkv, BF16)],
        scratch_shapes=[_slab_scratch(tm, qkv)],
        compiler_params=_params(("parallel",)),
    )(x, g, w)


def _band_combine(os_, lses, name):
    s = os_[0].shape[0]
    tm = min(512, s)

    def body(o1, o4, o16, l1, l4, l16, ya_ref, lse_ref, lse4_ref, lse16_ref, so4, sl4, so16, sl16):
        _undilate(so4, o4, 4)
        _undilate(sl4, l4, 4)
        _undilate(so16, o16, 16)
        _undilate(sl16, l16, 16)
        a0, a1, a2 = l1[...], _get(sl4), _get(sl16)
        m = jnp.maximum(jnp.maximum(a0, a1), a2)
        e0, e1, e2 = jnp.exp2(a0 - m), jnp.exp2(a1 - m), jnp.exp2(a2 - m)
        den = e0 + e1 + e2
        ya_ref[...] = (e0 * o1[...] + e1 * _get(so4) + e2 * _get(so16)) / den
        lse = m + jnp.log(den) * LOG2E
        lse_ref[...] = lse
        _put(sl4, lse)
        _dilate_store(lse4_ref, sl4, 4)
        _dilate_store(lse16_ref, sl4, 16)

    nat = pl.BlockSpec((tm, WIDTH_A), lambda i: (i, 0))
    d4, d16 = _dil_spec(tm, 4, WIDTH_A), _dil_spec(tm, 16, WIDTH_A)
    return pl.pallas_call(
        body, name=name, grid=(s // tm,), in_specs=[nat, d4, d16] * 2, out_specs=[nat, nat, d4, d16],
        out_shape=[jax.ShapeDtypeStruct((s, WIDTH_A), F32)] * 2
        + [_dil_shape(s, 4, WIDTH_A, F32), _dil_shape(s, 16, WIDTH_A, F32)],
        scratch_shapes=[_slab_scratch(tm, WIDTH_A)] * 4,
        compiler_params=_params(("parallel",)),
    )(*os_, *lses)


def _rope_tables(s):
    pos = jnp.arange(s, dtype=F32)
    inv_freq = jnp.exp(-math.log(ROPE_BASE) * jnp.arange(0, QK_ROPE, 2, dtype=F32) / QK_ROPE)
    ang = pos[:, None] * inv_freq[None, :]
    cos, sin = jnp.cos(ang), jnp.sin(ang)
    one = jnp.ones((s, 64), F32)
    zero16 = jnp.zeros((s, 16), F32)
    c = jnp.concatenate([one, cos, cos, jnp.ones((s, 32), F32)], axis=1)
    sa = jnp.concatenate([jnp.zeros((s, 64), F32), -sin, zero16, jnp.zeros((s, 32), F32)], axis=1)
    sb = jnp.concatenate([jnp.zeros((s, 64), F32), zero16, sin, jnp.zeros((s, 32), F32)], axis=1)
    return c, sa, sb


def _rope_fwd(x, c, sa, sb):
    return x * c + pltpu.roll(x, 112, 1) * sa + pltpu.roll(x, 16, 1) * sb


def _rope_bwd(dy, c, sa, sb):
    return dy * c + pltpu.roll(dy * sa, 16, 1) + pltpu.roll(dy * sb, 112, 1)


def _mla_prep(proj, g_q, g_kv, w_uq, w_ukv, tabs, name):
    s = proj.shape[0]
    tm = min(512, s)
    width = N_HEADS * 128

    def body(lat_ref, gq_ref, gkv_ref, wq_ref, wkv_ref, c_ref, sa_ref, sb_ref,
             q_ref, k_ref, kv_ref, cqn_ref, ckvn_ref):
        c, sa, sb = c_ref[...], sa_ref[...], sb_ref[...]
        cq = lat_ref[:, 0:Q_LORA].astype(F32)
        cqn = (cq * _rstd(cq) * gq_ref[...]).astype(BF16)
        cqn_ref[...] = cqn
        q = _dot(cqn, wq_ref[...], NT)
        ckv = lat_ref[:, Q_LORA:Q_LORA + KV_LORA].astype(F32)
        ckvn = (ckv * _rstd(ckv) * gkv_ref[...]).astype(BF16)
        ckvn_ref[...] = ckvn
        kv = _dot(ckvn, wkv_ref[...], NN)
        lane = lax.broadcasted_iota(jnp.int32, (tm, 128), 1)
        krr = _rope_fwd(lat_ref[:, Q_LORA + KV_LORA:].astype(F32), c, sa, sb)
        krr = jnp.where((lane == 96) | (lane == 97), 1.0, krr)
        ones01 = jnp.where(lane < 2, 1.0, 0.0)
        for h in range(N_HEADS):
            cols = slice(h * 128, (h + 1) * 128)
            q_ref[:, cols] = (_rope_fwd(q[:, cols], c, sa, sb) * (MLA_SCALE * LOG2E)).astype(BF16)
            k_ref[:, cols] = jnp.where(lane < 64, kv[:, cols], krr).astype(BF16)
            kv_ref[:, cols] = jnp.where(lane < 64, ones01, kv[:, cols]).astype(BF16)

    row = lambda n: pl.BlockSpec((tm, n), lambda i: (i, 0))
    full = lambda a: pl.BlockSpec(a.shape, lambda i: (0, 0))
    tab = pl.BlockSpec((tm, 128), lambda i: (i, 0))
    return pl.pallas_call(
        body, name=name, grid=(s // tm,),
        in_specs=[pl.BlockSpec((tm, N_LAT), lambda i: (i, COL_CQ // N_LAT)),
                  full(g_q), full(g_kv), full(w_uq), full(w_ukv), tab, tab, tab],
        out_specs=[row(width), row(width), row(width), row(Q_LORA), row(KV_LORA)],
        out_shape=[jax.ShapeDtypeStruct((s, width), BF16)] * 3
        + [jax.ShapeDtypeStruct((s, Q_LORA), BF16), jax.ShapeDtypeStruct((s, KV_LORA), BF16)],
        compiler_params=_params(("parallel",)),
    )(proj, g_q, g_kv, w_uq, w_ukv, *tabs)


def _mla_fwd(qcat, kcat, kvb, name):
    s = qcat.shape[0]
    tq = min(1024, s)
    tk = min(1024, s)
    nkc = s // tk

    def body(q_ref, k_ref, v_ref, yb_ref, qaug_ref, m_ref, acc_ref):
        lane = lax.broadcasted_iota(jnp.int32, (tq, 128), 1)
        m_ref[...] = jnp.full((2, tq, 128), NEG, F32)
        acc_ref[...] = jnp.zeros((2, tq, 128), F32)

        def chunk(cidx, carry):
            k0 = pl.multiple_of(cidx * tk, tk)
            cols = [slice(a * 128, (a + 1) * 128) for a in range(2)]
            scs = [_dot(q_ref[:, c], k_ref[pl.ds(k0, tk), c], NT) for c in cols]
            prs, alphas = [], []
            for a, sc in enumerate(scs):
                m_prev = m_ref[a]
                m_new = jnp.maximum(m_prev, jnp.max(sc, axis=-1, keepdims=True))
                alphas.append(jnp.exp2(m_prev - m_new))
                prs.append(jnp.exp2(sc - jnp.tile(m_new, (1, tk // 128))).astype(BF16))
                m_ref[a] = m_new
            for a, c in enumerate(cols):
                acc_ref[a] = alphas[a] * acc_ref[a] + _dot(prs[a], v_ref[pl.ds(k0, tk), c], NN)
            return carry

        lax.fori_loop(0, nkc, chunk, 0)
        outs = []
        for a in range(2):
            cols = slice(a * 128, (a + 1) * 128)
            acc = acc_ref[a]
            l = acc[:, 0:1]
            outs.append(acc / l)
            hi, lo = _split_hi_lo(m_ref[a] + jnp.log(l) * LOG2E)
            qaug_ref[:, cols] = jnp.where(lane == 96, -hi, jnp.where(lane == 97, -lo, q_ref[:, cols]))
        yb_ref[...] = jnp.where(lane < 64, pltpu.roll(outs[0], 64, 1), outs[1])

    return pl.pallas_call(
        body, name=name, grid=(4, s // tq),
        in_specs=[pl.BlockSpec((tq, 256), lambda p, i: (i, p)),
                  pl.BlockSpec((s, 256), lambda p, i: (0, p)),
                  pl.BlockSpec((s, 256), lambda p, i: (0, p))],
        out_specs=[pl.BlockSpec((tq, 128), lambda p, i: (i, p)),
                   pl.BlockSpec((tq, 256), lambda p, i: (i, p))],
        out_shape=[jax.ShapeDtypeStruct((s, WIDTH_A), F32), jax.ShapeDtypeStruct((s, N_HEADS * 128), BF16)],
        scratch_shapes=[pltpu.VMEM((2, tq, 128), F32)] * 2,
        compiler_params=_params(("parallel", "parallel")),
    )(qcat, kcat, kvb)


def _mla_bwd(qaug, kcat, kvb, doaug, name):
    s = qaug.shape[0]
    tq = min(1024, s)
    tk = min(512, s)
    nqc = s // tq
    width = N_HEADS * 128

    def body(q_ref, do_ref, k_ref, v_ref, dq_ref, dk_acc, dv_acc):
        j = pl.program_id(1)

        @pl.when(j == 0)
        def _():
            dq_ref[...] = jnp.zeros_like(dq_ref)

        dk_acc[...] = jnp.zeros_like(dk_acc)
        dv_acc[...] = jnp.zeros_like(dv_acc)

        def chunk(cidx, carry):
            q0 = pl.multiple_of(cidx * tq, tq)
            cols = [slice(a * 128, (a + 1) * 128) for a in range(2)]
            qs = [q_ref[pl.ds(q0, tq), c] for c in cols]
            dos = [do_ref[pl.ds(q0, tq), c] for c in cols]
            kbs = [k_ref[:, c] for c in cols]
            sts = [_dot(kbs[a], qs[a], NT) for a in range(2)]
            dps = [_dot(v_ref[:, cols[a]], dos[a], NT) for a in range(2)]
            pts, dsts = [], []
            for a in range(2):
                pt = jnp.exp2(sts[a])
                pts.append(pt.astype(BF16))
                dsts.append((pt * dps[a]).astype(BF16))
            for a, c in enumerate(cols):
                dv_acc[:, c] += _dot(pts[a], dos[a], NN)
                dk_acc[:, c] += _dot(dsts[a], qs[a], NN)
                dq_ref[pl.ds(q0, tq), c] += _dot(dsts[a], kbs[a], TN)
            return carry

        lax.fori_loop(0, nqc, chunk, 0)

    return pl.pallas_call(
        body, name=name, grid=(N_HEADS // 2, s // tk),
        in_specs=[pl.BlockSpec((s, 256), lambda p, j: (0, p)),
                  pl.BlockSpec((s, 256), lambda p, j: (0, p)),
                  pl.BlockSpec((tk, 256), lambda p, j: (j, p)),
                  pl.BlockSpec((tk, 256), lambda p, j: (j, p))],
        out_specs=[pl.BlockSpec((s, 256), lambda p, j: (0, p)),
                   pl.BlockSpec((tk, 256), lambda p, j: (j, p)),
                   pl.BlockSpec((tk, 256), lambda p, j: (j, p))],
        out_shape=[jax.ShapeDtypeStruct((s, width), F32)] * 3,
        compiler_params=_params(("parallel", "arbitrary")),
    )(qaug, doaug, kcat, kvb)


def _mla_prep_bwd(dqc, dkc, dvp, proj, cqn, ckvn, g_q, g_kv, w_uq, w_ukv, tabs, name):
    s = proj.shape[0]
    tm = min(512, s)
    width = N_HEADS * 128
    n_out_cols = N_LAT

    def body(dq_ref, dk_ref, dv_ref, lat_ref, cqn_ref, ckvn_ref, gq_ref, gkv_ref, wq_ref, wkv_ref,
             c_ref, sa_ref, sb_ref, dproj_ref, dwq_ref, dwkv_ref, dgq_ref, dgkv_ref):
        i = pl.program_id(0)

        @pl.when(i == 0)
        def _():
            dwq_ref[...] = jnp.zeros_like(dwq_ref)
            dwkv_ref[...] = jnp.zeros_like(dwkv_ref)
            dgq_ref[...] = jnp.zeros_like(dgq_ref)
            dgkv_ref[...] = jnp.zeros_like(dgkv_ref)

        c, sa, sb = c_ref[...], sa_ref[...], sb_ref[...]
        lane = lax.broadcasted_iota(jnp.int32, (tm, 128), 1)
        dkr = jnp.zeros((tm, 128), F32)
        dq_parts, dkv_parts = [], []
        for h in range(N_HEADS):
            cols = slice(h * 128, (h + 1) * 128)
            dq_parts.append(_rope_bwd(dq_ref[:, cols] * MLA_SCALE, c, sa, sb).astype(BF16))
            dkh = dk_ref[:, cols] * LN2
            dkr = dkr + dkh
            dkv_parts.append(jnp.where(lane < 64, dkh, dv_ref[:, cols]).astype(BF16))
        dq = jnp.concatenate(dq_parts, axis=1)
        dkv = jnp.concatenate(dkv_parts, axis=1)
        dkr = _rope_bwd(jnp.where((lane >= 64) & (lane < 96), dkr, 0.0), c, sa, sb)

        dcqn = _dot(dq, wq_ref[...], NN)
        dwq_ref[...] += _dot(dq, cqn_ref[...], TN)
        dcq, dgq = _rms_bwd(dcqn, lat_ref[:, 0:Q_LORA].astype(F32), gq_ref[...])
        dgq_ref[...] += jnp.sum(dgq, axis=0, keepdims=True)

        dckvn = _dot(dkv, wkv_ref[...], NT)
        dwkv_ref[...] += _dot(ckvn_ref[...], dkv, TN)
        dckv, dgkv = _rms_bwd(dckvn, lat_ref[:, Q_LORA:Q_LORA + KV_LORA].astype(F32), gkv_ref[...])
        dgkv_ref[...] += jnp.sum(dgkv, axis=0, keepdims=True)

        dproj_ref[:, 0:Q_LORA] = dcq.astype(BF16)
        dproj_ref[:, Q_LORA:Q_LORA + KV_LORA] = dckv.astype(BF16)
        dproj_ref[:, Q_LORA + KV_LORA:] = dkr.astype(BF16)

    row = lambda n: pl.BlockSpec((tm, n), lambda i: (i, 0))
    full = lambda a: pl.BlockSpec(a.shape, lambda i: (0, 0))
    tab = pl.BlockSpec((tm, 128), lambda i: (i, 0))
    return pl.pallas_call(
        body, name=name, grid=(s // tm,),
        in_specs=[row(width), row(width), row(width),
                  pl.BlockSpec((tm, N_LAT), lambda i: (i, COL_CQ // N_LAT)),
                  row(Q_LORA), row(KV_LORA), full(g_q), full(g_kv), full(w_uq), full(w_ukv), tab, tab, tab],
        out_specs=[row(n_out_cols), full(w_uq), full(w_ukv), full(g_q), full(g_kv)],
        out_shape=[jax.ShapeDtypeStruct((s, n_out_cols), BF16),
                   jax.ShapeDtypeStruct(w_uq.shape, F32), jax.ShapeDtypeStruct(w_ukv.shape, F32),
                   jax.ShapeDtypeStruct(g_q.shape, F32), jax.ShapeDtypeStruct(g_kv.shape, F32)],
        compiler_params=_params(("arbitrary",)),
    )(dqc, dkc, dvp, proj, cqn, ckvn, g_q, g_kv, w_uq, w_ukv, *tabs)


def _mix_out(ya, yb, na, nb, w_o, g_post, x, name):
    s = x.shape[0]
    tm = min(512, s)

    def body(ya_ref, yb_ref, na_ref, nb_ref, w_ref, g_ref, x_ref, yn_ref, y2_ref, x1_ref):
        a, b = ya_ref[...], yb_ref[...]
        yn = jnp.concatenate([a * _rstd(a) * na_ref[...], b * _rstd(b) * nb_ref[...]], axis=1).astype(BF16)
        yn_ref[...] = yn
        y2 = _dot(yn, w_ref[...], NN)
        y2_ref[...] = y2
        x1_ref[...] = x_ref[...] + y2 * _rstd(y2) * g_ref[...]

    row = lambda n: pl.BlockSpec((tm, n), lambda i: (i, 0))
    full = lambda a: pl.BlockSpec(a.shape, lambda i: (0, 0))
    return pl.pallas_call(
        body, name=name, grid=(s // tm,),
        in_specs=[row(WIDTH_A), row(WIDTH_A), full(na), full(nb), full(w_o), full(g_post), row(D_MODEL)],
        out_specs=[row(D_MODEL)] * 3,
        out_shape=[jax.ShapeDtypeStruct((s, D_MODEL), BF16), jax.ShapeDtypeStruct((s, D_MODEL), F32),
                   jax.ShapeDtypeStruct((s, D_MODEL), F32)],
        compiler_params=_params(("parallel",)),
    )(ya, yb, na, nb, w_o, g_post, x)


def _head_ones():
    blk = np.kron(np.eye(N_HEADS, dtype=np.float32), np.ones((64, 64), np.float32))
    return jnp.asarray(blk, F32)


def _outnorm_bwd(dyn, ya, yb, na, nb, ones, name):
    s = ya.shape[0]
    tm = min(512, s)

    def body(dyn_ref, ya_ref, yb_ref, na_ref, nb_ref, ones_ref, dya_ref, da_ref, do_ref, dna_ref, dnb_ref,
             dya4_ref, dya16_ref, da4_ref, da16_ref, scr):
        i = pl.program_id(0)

        @pl.when(i == 0)
        def _():
            dna_ref[...] = jnp.zeros_like(dna_ref)
            dnb_ref[...] = jnp.zeros_like(dnb_ref)

        a, b = ya_ref[...], yb_ref[...]
        dya, dna = _rms_bwd(dyn_ref[:, 0:WIDTH_A], a, na_ref[...])
        dyb, dnb = _rms_bwd(dyn_ref[:, WIDTH_A:], b, nb_ref[...])
        dna_ref[...] += jnp.sum(dna, axis=0, keepdims=True)
        dnb_ref[...] += jnp.sum(dnb, axis=0, keepdims=True)
        dya_b = dya.astype(BF16)
        dya_ref[...] = dya_b
        hp = lax.Precision.HIGHEST
        delta_a = jnp.dot(dya_b.astype(F32) * a, ones_ref[...], precision=hp, preferred_element_type=F32)
        da_ref[...] = delta_a
        _put(scr, dya_b)
        _dilate_store(dya4_ref, scr, 4)
        _dilate_store(dya16_ref, scr, 16)
        _put(scr, delta_a)
        _dilate_store(da4_ref, scr, 4)
        _dilate_store(da16_ref, scr, 16)
        dyb_b = dyb.astype(BF16)
        db = jnp.dot(dyb_b.astype(F32) * b, ones_ref[...], precision=hp, preferred_element_type=F32)
        lane = lax.broadcasted_iota(jnp.int32, (tm, 128), 1)
        zero = jnp.zeros((tm, 128), BF16)
        for p in range(4):
            cols = slice(p * 128, (p + 1) * 128)
            dyp = dyb_b[:, cols]
            dbp = db[:, cols]
            for a_ in range(2):
                src = pltpu.roll(dyp.astype(F32), 64, 1).astype(BF16) if a_ == 0 else dyp
                dlt = dbp if a_ == 0 else pltpu.roll(dbp, 64, 1)
                hi, lo = _split_hi_lo(dlt)
                blk = jnp.where(lane >= 64, src, jnp.where(lane == 0, -hi, jnp.where(lane == 1, -lo, zero)))
                h = 2 * p + a_
                do_ref[:, h * 128:(h + 1) * 128] = blk

    row = lambda n: pl.BlockSpec((tm, n), lambda i: (i, 0))
    full = lambda a: pl.BlockSpec(a.shape, lambda i: (0, 0))
    return pl.pallas_call(
        body, name=name, grid=(s // tm,),
        in_specs=[row(D_MODEL), row(WIDTH_A), row(WIDTH_A), full(na), full(nb), full(ones)],
        out_specs=[row(WIDTH_A), row(WIDTH_A), row(N_HEADS * 128), full(na), full(nb),
                   _dil_spec(tm, 4, WIDTH_A), _dil_spec(tm, 16, WIDTH_A), _dil_spec(tm, 4, WIDTH_A), _dil_spec(tm, 16, WIDTH_A)],
        out_shape=[jax.ShapeDtypeStruct((s, WIDTH_A), BF16), jax.ShapeDtypeStruct((s, WIDTH_A), F32),
                   jax.ShapeDtypeStruct((s, N_HEADS * 128), BF16),
                   jax.ShapeDtypeStruct(na.shape, F32), jax.ShapeDtypeStruct(nb.shape, F32),
                   _dil_shape(s, 4, WIDTH_A, BF16), _dil_shape(s, 16, WIDTH_A, BF16),
                   _dil_shape(s, 4, WIDTH_A, F32), _dil_shape(s, 16, WIDTH_A, F32)],
        scratch_shapes=[_slab_scratch(tm, WIDTH_A)],
        compiler_params=_params(("arbitrary",)),
    )(dyn, ya, yb, na, nb, ones)


def _sum_cast(parts, name):
    s = parts[0][0].shape[0]
    tm = min(512, s)

    def body(*refs):
        o_ref, s4, s16 = refs[9:]
        for t in range(3):
            _undilate(s4, refs[3 + t], 4)
            _undilate(s16, refs[6 + t], 16)
            acc = refs[t][...] + _get(s4) + _get(s16)
            o_ref[:, t * WIDTH_A:(t + 1) * WIDTH_A] = acc.astype(BF16)

    nat = pl.BlockSpec((tm, WIDTH_A), lambda i: (i, 0))
    flat = [parts[g][t] for g in range(3) for t in range(3)]
    return pl.pallas_call(
        body, name=name, grid=(s // tm,),
        in_specs=[nat] * 3 + [_dil_spec(tm, 4, WIDTH_A)] * 3 + [_dil_spec(tm, 16, WIDTH_A)] * 3,
        out_specs=pl.BlockSpec((tm, 3 * WIDTH_A), lambda i: (i, 0)),
        out_shape=jax.ShapeDtypeStruct((s, 3 * WIDTH_A), BF16),
        scratch_shapes=[_slab_scratch(tm, WIDTH_A)] * 2,
        compiler_params=_params(("parallel",)),
    )(*flat)


HALO = 16


def _gelu(x):
    k = math.sqrt(2.0 / math.pi)
    t = jnp.tanh(k * (x + 0.044715 * x * x * x))
    return 0.5 * x * (1.0 + t), t


def _gelu_grad(x, t):
    k = math.sqrt(2.0 / math.pi)
    return 0.5 * (1.0 + t) + 0.5 * x * (1.0 - t * t) * k * (1.0 + 3 * 0.044715 * x * x)


def _halo_specs(s, tm, tn, lead):
    nb = s // HALO
    hb = tm // HALO
    pre = (lead,) if lead else ()
    z = (0,) if lead else ()
    main = pl.BlockSpec(pre + (tm, tn), lambda j, i: z + (i, j))
    prev = pl.BlockSpec(pre + (HALO, tn), lambda j, i: z + (jnp.maximum(i * hb - 1, 0), j))
    nxt = pl.BlockSpec(pre + (HALO, tn), lambda j, i: z + (jnp.minimum((i + 1) * hb, nb - 1), j))
    return [prev, main, nxt]


def _fill_ext(ext, prev, main, nxt, i, tm, s):
    ext[0:HALO, :] = jnp.where(i > 0, prev.astype(F32), 0.0)
    ext[HALO:HALO + tm, :] = main.astype(F32)
    ext[HALO + tm:2 * HALO + tm, :] = jnp.where((i + 1) * tm < s, nxt.astype(F32), 0.0)


STRIP = 16


def _shifted(ref, row0):
    n = STRIP + 16
    win = ref[pl.ds(pl.multiple_of(row0 - 8, 8), n), :]
    return pltpu.roll(win, 1, 0)[8:8 + STRIP], win[8:8 + STRIP], pltpu.roll(win, n - 1, 0)[8:8 + STRIP]


def _conv3(e, row0, w_ref, b_ref, t):
    m1, c0, p1 = _shifted(e, row0)
    return w_ref[t, 0:1, :] * m1 + w_ref[t, 1:2, :] * c0 + w_ref[t, 2:3, :] * p1 + b_ref[t]


def _conv_gate(up, cw, cb, name):
    _, s, c = up.shape
    tm = min(1024, s)
    tn = FF_SLAB

    def body(up_p, up_m, up_n, w_ref, b_ref, a_ref, eg, ev):
        i = pl.program_id(1)
        _fill_ext(eg, up_p[0], up_m[0], up_n[0], i, tm, s)
        _fill_ext(ev, up_p[1], up_m[1], up_n[1], i, tm, s)

        def strip(t, carry):
            r0 = pl.multiple_of(t * STRIP, STRIP)
            g, _ = _gelu(_conv3(eg, HALO + r0, w_ref, b_ref, 0))
            a_ref[pl.ds(r0, STRIP), :] = (g * _conv3(ev, HALO + r0, w_ref, b_ref, 1)).astype(BF16)
            return carry

        lax.fori_loop(0, tm // STRIP, strip, 0)

    return pl.pallas_call(
        body, name=name, grid=(c // tn, s // tm),
        in_specs=_halo_specs(s, tm, tn, 2)
        + [pl.BlockSpec((2, 3, tn), lambda j, i: (0, 0, j)), pl.BlockSpec((2, 1, tn), lambda j, i: (0, 0, j))],
        out_specs=pl.BlockSpec((tm, tn), lambda j, i: (i, j)),
        out_shape=jax.ShapeDtypeStruct((s, c), BF16),
        scratch_shapes=[pltpu.VMEM((tm + 2 * HALO, tn), F32)] * 2,
        compiler_params=_params(("parallel", "parallel")),
    )(up, up, up, cw, cb)


def _conv_gate_bwd(up, da, cw, cb, name):
    _, s, c = up.shape
    tm = min(512, s)
    tn = FF_SLAB
    te = tm + HALO

    def body(up_p, up_m, up_n, da_p, da_m, da_n, w_ref, b_ref, dup_ref, dw_ref, db_ref, eg, ev, ed, dug, duv):
        i = pl.program_id(1)

        @pl.when(i == 0)
        def _():
            dw_ref[...] = jnp.zeros_like(dw_ref)
            db_ref[...] = jnp.zeros_like(db_ref)

        _fill_ext(eg, up_p[0], up_m[0], up_n[0], i, tm, s)
        _fill_ext(ev, up_p[1], up_m[1], up_n[1], i, tm, s)
        _fill_ext(ed, da_p[...], da_m[...], da_n[...], i, tm, s)
        o = HALO // 2

        def du_strip(t, carry):
            r0 = pl.multiple_of(t * STRIP, STRIP)
            ug = _conv3(eg, o + r0, w_ref, b_ref, 0)
            uv = _conv3(ev, o + r0, w_ref, b_ref, 1)
            gl, th = _gelu(ug)
            dav = ed[pl.ds(pl.multiple_of(o + r0, 8), STRIP), :]
            dug[pl.ds(r0, STRIP), :] = dav * uv * _gelu_grad(ug, th)
            duv[pl.ds(r0, STRIP), :] = dav * gl
            return carry

        lax.fori_loop(0, te // STRIP, du_strip, 0)

        def back(du, e, t):
            def strip(k, acc):
                r0 = pl.multiple_of(k * STRIP, STRIP)
                dm1, c0, dp1 = _shifted(du, o + r0)
                dup_ref[t, pl.ds(r0, STRIP), :] = (w_ref[t, 0:1, :] * dp1 + w_ref[t, 1:2, :] * c0
                                                   + w_ref[t, 2:3, :] * dm1).astype(BF16)
                um1, u0, up1 = _shifted(e, HALO + r0)
                fold = lambda a: a[0:8] + a[8:16]
                return (acc[0] + fold(um1 * c0), acc[1] + fold(u0 * c0), acc[2] + fold(up1 * c0), acc[3] + fold(c0))

            zero = jnp.zeros((8, tn), F32)
            acc = lax.fori_loop(0, tm // STRIP, strip, (zero, zero, zero, zero))
            for k in range(3):
                dw_ref[t, k:k + 1, :] += jnp.sum(acc[k], axis=0, keepdims=True)
            db_ref[t] += jnp.sum(acc[3], axis=0, keepdims=True)

        back(dug, eg, 0)
        back(duv, ev, 1)

    wspec = pl.BlockSpec((2, 3, tn), lambda j, i: (0, 0, j))
    bspec = pl.BlockSpec((2, 1, tn), lambda j, i: (0, 0, j))
    return pl.pallas_call(
        body, name=name, grid=(c // tn, s // tm),
        in_specs=_halo_specs(s, tm, tn, 2) + _halo_specs(s, tm, tn, 0) + [wspec, bspec],
        out_specs=[pl.BlockSpec((2, tm, tn), lambda j, i: (0, i, j)), wspec, bspec],
        out_shape=[jax.ShapeDtypeStruct((2, s, c), BF16), jax.ShapeDtypeStruct((2, 3, c), F32),
                   jax.ShapeDtypeStruct((2, 1, c), F32)],
        scratch_shapes=[pltpu.VMEM((tm + 2 * HALO, tn), F32)] * 3 + [pltpu.VMEM((te, tn), F32)] * 2,
        compiler_params=_params(("parallel", "arbitrary")),
    )(up, up, up, da, da, da, cw, cb)


def _ffn_out(a, w_down, g_post, x1, target, name):
    s = x1.shape[0]
    tm = min(512, s)

    def body(a_ref, w_ref, g_ref, x1_ref, t_ref, dy3_ref, dx2_ref, loss_ref, dg_ref):
        i = pl.program_id(0)

        @pl.when(i == 0)
        def _():
            loss_ref[...] = jnp.zeros_like(loss_ref)
            dg_ref[...] = jnp.zeros_like(dg_ref)

        y3 = _dot(a_ref[...], w_ref[...], NN)
        g = g_ref[...]
        x2 = x1_ref[...] + y3 * _rstd(y3) * g
        diff = x2 - t_ref[...]
        loss_ref[...] += jnp.sum(jnp.sum(diff * diff, axis=1, keepdims=True), axis=0, keepdims=True)
        dx2 = diff * (1.0 / D_MODEL)
        dx2_ref[...] = dx2
        dy3, dg = _rms_bwd(dx2, y3, g)
        dy3_ref[...] = dy3.astype(BF16)
        dg_ref[...] += jnp.sum(dg, axis=0, keepdims=True)

    row = lambda n: pl.BlockSpec((tm, n), lambda i: (i, 0))
    full = lambda t: pl.BlockSpec(t.shape, lambda i: (0, 0))
    return pl.pallas_call(
        body, name=name, grid=(s // tm,),
        in_specs=[row(a.shape[1]), full(w_down), full(g_post), row(D_MODEL), row(D_MODEL)],
        out_specs=[row(D_MODEL), row(D_MODEL), pl.BlockSpec((8, 128), lambda i: (0, 0)), full(g_post)],
        out_shape=[jax.ShapeDtypeStruct((s, D_MODEL), BF16), jax.ShapeDtypeStruct((s, D_MODEL), F32),
                   jax.ShapeDtypeStruct((8, 128), F32), jax.ShapeDtypeStruct(g_post.shape, F32)],
        compiler_params=_params(("arbitrary",)),
    )(a, w_down, g_post, x1, target)


def _resnorm_bwd(dh2, x1, g_ffn_pre, dx2, y2, g_mix_post, name):
    s = x1.shape[0]
    tm = min(512, s)

    def body(dh_ref, x1_ref, gf_ref, dx2_ref, y2_ref, gp_ref, dx1_ref, dy2_ref, dgf_ref, dgp_ref):
        i = pl.program_id(0)

        @pl.when(i == 0)
        def _():
            dgf_ref[...] = jnp.zeros_like(dgf_ref)
            dgp_ref[...] = jnp.zeros_like(dgp_ref)

        dn, dgf = _rms_bwd(dh_ref[...], x1_ref[...], gf_ref[...])
        dx1 = dx2_ref[...] + dn
        dx1_ref[...] = dx1
        dgf_ref[...] += jnp.sum(dgf, axis=0, keepdims=True)
        dy2, dgp = _rms_bwd(dx1, y2_ref[...], gp_ref[...])
        dy2_ref[...] = dy2.astype(BF16)
        dgp_ref[...] += jnp.sum(dgp, axis=0, keepdims=True)

    row = pl.BlockSpec((tm, D_MODEL), lambda i: (i, 0))
    full = pl.BlockSpec((1, D_MODEL), lambda i: (0, 0))
    return pl.pallas_call(
        body, name=name, grid=(s // tm,),
        in_specs=[row, row, full, row, row, full],
        out_specs=[row, row, full, full],
        out_shape=[jax.ShapeDtypeStruct((s, D_MODEL), F32), jax.ShapeDtypeStruct((s, D_MODEL), BF16),
                   jax.ShapeDtypeStruct((1, D_MODEL), F32), jax.ShapeDtypeStruct((1, D_MODEL), F32)],
        compiler_params=_params(("arbitrary",)),
    )(dh2, x1, g_ffn_pre, dx2, y2, g_mix_post)


def _final_bwd(dh1, x, g_pre, dx1, name):
    s = x.shape[0]
    tm = min(512, s)

    def body(dh_ref, x_ref, g_ref, dx1_ref, dx_ref, dg_ref):
        @pl.when(pl.program_id(0) == 0)
        def _():
            dg_ref[...] = jnp.zeros_like(dg_ref)

        dn, dg = _rms_bwd(dh_ref[...], x_ref[...], g_ref[...])
        dx_ref[...] = dx1_ref[...] + dn
        dg_ref[...] += jnp.sum(dg, axis=0, keepdims=True)

    row = pl.BlockSpec((tm, D_MODEL), lambda i: (i, 0))
    full = pl.BlockSpec((1, D_MODEL), lambda i: (0, 0))
    return pl.pallas_call(
        body, name=name, grid=(s // tm,),
        in_specs=[row, row, full, row], out_specs=[row, full],
        out_shape=[jax.ShapeDtypeStruct((s, D_MODEL), F32), jax.ShapeDtypeStruct((1, D_MODEL), F32)],
        compiler_params=_params(("arbitrary",)),
    )(dh1, x, g_pre, dx1)


def _local_step(x, target, fw, rep, mixer_weights, early_grads, late_grads):
    s = x.shape[0]
    tabs = _rope_tables(s)
    w_in, w_uq, w_ukv, w_o = (fw[n] for n in ("w_in", "w_uq", "w_ukv", "w_o"))
    tr = min(2048, s)
    tcon = min(2048, s)

    proj, h1, qkv4, qkv16 = _in_proj(x, rep["norm_mix_pre"], w_in, "in_proj")
    qkv = {1: proj, 4: qkv4, 16: qkv16}
    q_of = lambda r: (qkv[r], lambda c: 3 * c)
    k_of = lambda r: (qkv[r], lambda c: 3 * c + 1)
    v_of = lambda r: (qkv[r], lambda c: 3 * c + 2)
    own = lambda a: (a, lambda c: c)
    biases = [_band_bias(r) for _, r in DIL_CONFIGS]
    os_, lses = [], []
    for g, (_, r) in enumerate(DIL_CONFIGS):
        o, l = _band_call("fwd", r, [q_of(r)], [k_of(r), v_of(r)], biases[g], f"band_fwd_r{r}")
        os_.append(o)
        lses.append(l)
    ya, lse_a, lse4, lse16 = _band_combine(os_, lses, "band_combine")
    qcat, kcat, kvb, cqn, ckvn = _mla_prep(proj, rep["q_lat_norm"], rep["kv_lat_norm"], w_uq, w_ukv, tabs, "mla_prep")
    yb, qaug = _mla_fwd(qcat, kcat, kvb, "mla_fwd")
    yn, y2, x1 = _mix_out(ya, yb, rep["out_norm_a"], rep["out_norm_b"], w_o, rep["norm_mix_post"], x, "mix_out")
    mw = mixer_weights(x1)
    w_up, w_down, cw, cb = mw["w_up"], mw["w_down"], mw["conv_w"], mw["conv_b"]
    ff = w_down.shape[0]
    up, h2 = _norm_matmul(x1, rep["norm_ffn_pre"], w_up, "up_proj")
    act = _conv_gate(up, cw, cb, "conv_gate")
    dy3, dx2, loss_acc, dg_ffn_post = _ffn_out(act, w_down, rep["norm_ffn_post"], x1, target, "ffn_out")

    grads = {"norm_ffn_post": dg_ffn_post}
    dact = _matmul(dy3, w_down, "nt", BF16, tr, ff // 2, D_MODEL, "d_act")
    grads["w_down"] = _matmul(act, dy3, "tn", BF16, ff // 2, D_MODEL, tcon, "dw_down")
    dup, grads["conv_w"], grads["conv_b"] = _conv_gate_bwd(up, dact, cw, cb, "conv_gate_bwd")
    half = N_DEV // 2
    dh2 = _matmul_core(
        dup, w_up, NN, (s // tr, 1, N_DEV),
        pl.BlockSpec((None, tr, FF_SLAB), lambda i, j, t: (t // half, i, t % half)),
        pl.BlockSpec((None, FF_SLAB, D_MODEL), lambda i, j, t: (t, 0, 0)),
        pl.BlockSpec((tr, D_MODEL), lambda i, j, t: (i, 0)),
        jax.ShapeDtypeStruct((s, D_MODEL), F32), (tr, D_MODEL), "d_h2")
    grads["w_up"] = _matmul_core(
        dup, h2, TN, (1, N_DEV, s // tcon),
        pl.BlockSpec((None, tcon, FF_SLAB), lambda i, j, t: (j // half, t, j % half)),
        pl.BlockSpec((tcon, D_MODEL), lambda i, j, t: (t, 0)),
        pl.BlockSpec((None, FF_SLAB, D_MODEL), lambda i, j, t: (j, 0, 0)),
        jax.ShapeDtypeStruct((N_DEV, FF_SLAB, D_MODEL), BF16), (FF_SLAB, D_MODEL), "dw_up")
    dx1, dy2, grads["norm_ffn_pre"], grads["norm_mix_post"] = _resnorm_bwd(
        dh2, x1, rep["norm_ffn_pre"], dx2, y2, rep["norm_mix_post"], "resnorm_bwd")
    dyn = _matmul(dy2, w_o, "nt", F32, tr, D_MODEL, D_MODEL, "d_yn")
    grads["w_o"] = _matmul(yn, dy2, "tn", BF16, D_MODEL, D_MODEL, tcon, "dw_o")
    token = early_grads(grads)
    dya, delta_a, doaug, grads["out_norm_a"], grads["out_norm_b"], dya4, dya16, delta4, delta16 = _outnorm_bwd(
        dyn, ya, yb, rep["out_norm_a"] + token, rep["out_norm_b"], _head_ones(), "outnorm_bwd")
    stats = {1: (dya, lse_a, delta_a), 4: (dya4, lse4, delta4), 16: (dya16, lse16, delta16)}
    parts = []
    for g, (_, r) in enumerate(DIL_CONFIGS):
        qside = [q_of(r)] + [own(a) for a in stats[r]]
        kside = [k_of(r), v_of(r)]
        (dq,) = _band_call("dq", r, qside, kside, biases[g], f"band_dq_r{r}")
        dk, dv = _band_call("dkv", r, kside, qside, biases[g], f"band_dkv_r{r}")
        parts.append((dq, dk, dv))
    dproj_a = _sum_cast(parts, "band_grad_sum")
    dqc, dkc, dvp = _mla_bwd(qaug, kcat, kvb, doaug, "mla_bwd")
    dproj_b, grads["w_uq"], grads["w_ukv"], grads["q_lat_norm"], grads["kv_lat_norm"] = _mla_prep_bwd(
        dqc, dkc, dvp, proj, cqn, ckvn, rep["q_lat_norm"], rep["kv_lat_norm"], w_uq, w_ukv, tabs, "mla_prep_bwd")
    dproj = jnp.concatenate([dproj_a, dproj_b], axis=1)
    grads["w_in"] = _matmul(dproj, h1, "tn", BF16, N_PROJ // 2, D_MODEL, tcon, "dw_in")
    token = late_grads(grads)
    dh1 = _matmul(dproj, w_in, "nn", F32, tr, D_MODEL, N_PROJ // 2, "d_h1")
    grad_x, grads["norm_mix_pre"] = _final_bwd(dh1, x, rep["norm_mix_pre"] + token, dx1, "final_bwd")
    loss = 0.5 / D_MODEL * loss_acc[0, 0]
    return loss, grad_x, grads


MESH = pl.DeviceIdType.MESH
HBM_SPEC = pl.BlockSpec(memory_space=pltpu.HBM)
SMALL_ROWS = 96
TRANSPOSED = ("w_in", "w_up", "w_uq")
BUF_SHAPES = {"w_up": (FF_SLAB, D_MODEL), "w_in": (D_IN // N_DEV, D_MODEL), "w_down": (D_FF // N_DEV, D_MODEL),
              "w_o": (D_MODEL // N_DEV, D_MODEL), "w_uq": (QK_NOPE + QK_ROPE, Q_LORA), "w_ukv": (KV_LORA, 128),
              "conv_w": (8, FF_SLAB)}
BUF_ORDER = tuple(BUF_SHAPES)
MIXING = ("w_in", "w_o", "w_uq", "w_ukv")
MIXER = ("w_up", "w_down", "conv_w")
EARLY_GRADS = ("w_up", "w_down", "w_o", "conv_w")
LATE_GRADS = ("w_in", "w_uq", "w_ukv")


def _all_gather(bufs, name):
    nb = len(bufs)

    def body(*refs):
        x_refs, out_refs = refs[:nb], refs[nb:2 * nb]
        send_sems, recv_sems, local_sems = refs[2 * nb:]
        x, y, c = lax.axis_index("x"), lax.axis_index("y"), lax.axis_index("c")
        me, sibling = (x, y, c), (x, y, 1 - c)
        chips = [(1 - x, y), (x, 1 - y), (1 - x, 1 - y)]

        def copy(b, k, block, to, own=False):
            px, py, pc = block
            slot = out_refs[b].at[4 * px + 2 * py + pc]
            return pltpu.make_async_remote_copy(
                src_ref=x_refs[b] if own else slot, dst_ref=slot,
                send_sem=send_sems.at[7 * b + k], recv_sem=recv_sems.at[7 * b + k], device_id=to, device_id_type=MESH)

        mine = [pltpu.make_async_copy(x_refs[b], out_refs[b].at[4 * x + 2 * y + c], local_sems.at[b]) for b in range(nb)]
        sends = []
        for b in range(nb):
            mine[b].start()
            first = [copy(b, 0, me, sibling, own=True)]
            first += [copy(b, 1 + j, me, (*chip, c), own=True) for j, chip in enumerate(chips)]
            for cp in first:
                cp.start()
            sends += first
        for j, chip in enumerate(chips):
            for b in range(nb):
                copy(b, 1 + j, (*chip, c), me).wait_recv()
                passed = copy(b, 4 + j, (*chip, c), sibling)
                passed.start()
                sends.append(passed)
        for b in range(nb):
            copy(b, 0, sibling, me).wait_recv()
            for j, chip in enumerate(chips):
                copy(b, 4 + j, (*chip, 1 - c), me).wait_recv()
        for cp in sends:
            cp.wait_send()
        for cp in mine:
            cp.wait()

    return pl.pallas_call(
        body, name=name,
        out_shape=[jax.ShapeDtypeStruct((N_DEV,) + p.shape, p.dtype) for p in bufs],
        in_specs=[HBM_SPEC] * nb, out_specs=[HBM_SPEC] * nb,
        scratch_shapes=[pltpu.SemaphoreType.DMA((7 * nb,)), pltpu.SemaphoreType.DMA((7 * nb,)),
                        pltpu.SemaphoreType.DMA((nb,))],
    )(*bufs)


def _grad_exchange(bigs, small, name):
    flips = [(fx, fy, fc) for fx in (0, 1) for fy in (0, 1) for fc in (0, 1)][1:]
    nb = len(bigs)

    def body(*refs):
        big_refs, small_ref = refs[:nb], refs[nb]
        rbig_refs, rsmall_ref = refs[nb + 1:2 * nb + 1], refs[2 * nb + 1]
        send_sems, recv_sems, local_sems = refs[2 * nb + 2:]
        x, y, c = lax.axis_index("x"), lax.axis_index("y"), lax.axis_index("c")
        my = 4 * x + 2 * y + c
        own = [pltpu.make_async_copy(big_refs[b].at[my], rbig_refs[b].at[my], local_sems.at[b]) for b in range(nb)]
        own.append(pltpu.make_async_copy(small_ref, rsmall_ref.at[my], local_sems.at[nb]))
        for cp in own:
            cp.start()
        copies = []
        for b in range(nb + 1):
            for k, (fx, fy, fc) in enumerate(flips):
                px = 1 - x if fx else x
                py = 1 - y if fy else y
                pc = 1 - c if fc else c
                src = small_ref if b == nb else big_refs[b].at[4 * px + 2 * py + pc]
                dst = rsmall_ref.at[my] if b == nb else rbig_refs[b].at[my]
                copies.append(pltpu.make_async_remote_copy(
                    src_ref=src, dst_ref=dst, send_sem=send_sems.at[7 * b + k], recv_sem=recv_sems.at[7 * b + k],
                    device_id=(px, py, pc), device_id_type=MESH))
        for cp in copies:
            cp.start()
        for cp in copies:
            cp.wait()
        for cp in own:
            cp.wait()

    nsem = 7 * (nb + 1)
    return pl.pallas_call(
        body, name=name,
        out_shape=[jax.ShapeDtypeStruct(b.shape, b.dtype) for b in bigs]
        + [jax.ShapeDtypeStruct((N_DEV,) + small.shape, small.dtype)],
        in_specs=[HBM_SPEC] * (nb + 1), out_specs=[HBM_SPEC] * (nb + 1),
        scratch_shapes=[pltpu.SemaphoreType.DMA((nsem,)), pltpu.SemaphoreType.DMA((nsem,)),
                        pltpu.SemaphoreType.DMA((nb + 1,))],
    )(*bigs, small)


SEM_SPEC = pl.BlockSpec(memory_space=pltpu.SEMAPHORE)
ANY_SPEC = pl.BlockSpec(memory_space=pl.ANY)
FLIPS = tuple((fx, fy, fc) for fx in (0, 1) for fy in (0, 1) for fc in (0, 1))[1:]


def _split_copies(src_refs, land_refs, send_sems, recv_sems, scatter):
    x, y, c = lax.axis_index("x"), lax.axis_index("y"), lax.axis_index("c")
    my = 4 * x + 2 * y + c
    copies = []
    for b, (src, land) in enumerate(zip(src_refs, land_refs)):
        for k, (fx, fy, fc) in enumerate(FLIPS):
            px = 1 - x if fx else x
            py = 1 - y if fy else y
            pc = 1 - c if fc else c
            copies.append(pltpu.make_async_remote_copy(
                src_ref=src.at[4 * px + 2 * py + pc] if scatter else src, dst_ref=land.at[my],
                send_sem=send_sems.at[7 * b + k], recv_sem=recv_sems.at[7 * b + k],
                device_id=(px, py, pc), device_id_type=MESH))
    return copies


def _exchange_start(srcs, scatter, name):
    nb = len(srcs)
    lands = [lax.empty(s.shape if scatter else (N_DEV,) + s.shape, s.dtype) for s in srcs]

    def body(*refs):
        src_refs, land_refs = refs[:nb], refs[nb:2 * nb]
        send_sems, recv_sems = refs[2 * nb], refs[2 * nb + 1]
        token = refs[-1]
        for cp in _split_copies(src_refs, land_refs, send_sems, recv_sems, scatter):
            cp.start()
        token[...] = jnp.zeros_like(token)

    hbm = lambda a: pltpu.HBM(a.shape, a.dtype)
    outs = pl.pallas_call(
        body, name=name,
        out_shape=(pltpu.SemaphoreType.DMA((7 * nb,)), pltpu.SemaphoreType.DMA((7 * nb,)),
                   *[hbm(a) for a in srcs], *[hbm(a) for a in lands], jax.ShapeDtypeStruct((8, 128), F32)),
        in_specs=[HBM_SPEC] * (2 * nb),
        out_specs=(SEM_SPEC, SEM_SPEC, *[HBM_SPEC] * (2 * nb), pl.BlockSpec(memory_space=pltpu.VMEM)),
        input_output_aliases={i: 2 + i for i in range(2 * nb)},
        compiler_params=pltpu.CompilerParams(has_side_effects=pltpu.SideEffectType.DATAFLOW_SIDE_EFFECTING),
    )(*[pltpu.with_memory_space_constraint(a, pltpu.HBM) for a in srcs],
      *[pltpu.with_memory_space_constraint(a, pltpu.HBM) for a in lands])
    return outs[0], outs[1], list(outs[2:2 + nb]), list(outs[2 + nb:2 + 2 * nb]), outs[-1]


def _exchange_wait(started, scatter, after, name):
    send_sems, recv_sems, srcs, lands, _ = started
    nb = len(srcs)

    def body(*refs):
        src_refs, land_refs = refs[:nb], refs[nb:2 * nb]
        for cp in _split_copies(src_refs, land_refs, refs[2 * nb], refs[2 * nb + 1], scatter):
            cp.wait_send()
            cp.wait_recv()

    hbm = lambda a: pltpu.HBM(a.shape, a.dtype)
    outs = pl.pallas_call(
        body, name=name,
        out_shape=(*[hbm(a) for a in srcs], *[hbm(a) for a in lands]),
        in_specs=[HBM_SPEC] * (2 * nb) + [SEM_SPEC, SEM_SPEC, ANY_SPEC],
        out_specs=tuple([HBM_SPEC] * (2 * nb)),
        input_output_aliases={i: i for i in range(2 * nb)},
        compiler_params=pltpu.CompilerParams(has_side_effects=pltpu.SideEffectType.DATAFLOW_SIDE_EFFECTING),
    )(*srcs, *lands, send_sems, recv_sems, after)
    return list(outs[:nb]), list(outs[nb:])


def _own_slot(land, own):
    my = 4 * lax.axis_index("x") + 2 * lax.axis_index("y") + lax.axis_index("c")
    return lax.dynamic_update_slice(land, own[None], (my,) + (0,) * own.ndim)


def _adamw(parts, w, m, v, name):
    rows, n = w.shape
    tm = rows if rows <= 384 else next(t for t in (256, 176) if rows % t == 0)
    assert rows % tm == 0

    def body(p_ref, w_ref, m_ref, v_ref, g_ref, d_ref, m2_ref, v2_ref):
        g = p_ref[0, :, 0:n].astype(F32)
        for s in range(1, N_DEV):
            g = g + p_ref[s, :, 0:n].astype(F32)
        g_ref[...] = g
        m2 = ADAM_B1 * m_ref[...] + (1.0 - ADAM_B1) * g
        v2 = ADAM_B2 * v_ref[...] + (1.0 - ADAM_B2) * jnp.square(g)
        m2_ref[...] = m2
        v2_ref[...] = v2
        m_hat = m2 / (1.0 - ADAM_B1 ** ADAM_STEP)
        v_hat = v2 / (1.0 - ADAM_B2 ** ADAM_STEP)
        d_ref[...] = -ADAM_LR * (m_hat / (jnp.sqrt(v_hat) + ADAM_EPS) + ADAM_WD * w_ref[...])

    row = pl.BlockSpec((tm, n), lambda i: (i, 0))
    return pl.pallas_call(
        body, name=name, grid=(rows // tm,),
        in_specs=[pl.BlockSpec((N_DEV, tm, parts.shape[2]), lambda i: (0, i, 0)), row, row, row],
        out_specs=[row] * 4,
        out_shape=[jax.ShapeDtypeStruct((rows, n), F32)] * 4,
        compiler_params=_params(("parallel",)),
    )(parts, w, m, v)


def _pack(flat_parts, rows):
    flat = jnp.concatenate(flat_parts, axis=-1)
    pad = rows * LANES - flat.shape[-1]
    flat = jnp.pad(flat, [(0, 0)] * (flat.ndim - 1) + [(0, pad)])
    return flat.reshape(flat.shape[:-1] + (rows, LANES))


def _unpack(packed, shapes):
    flat = packed.reshape(packed.shape[:-2] + (-1,))
    out, off = {}, 0
    for name, shape in shapes.items():
        n = int(np.prod(shape))
        out[name] = flat[..., off:off + n].reshape(flat.shape[:-1] + tuple(shape))
        off += n
    return out


def _pad_to(a, shape):
    return jnp.pad(a, [(0, t - d) for d, t in zip(a.shape, shape)])


def _pad_w_in(w):
    k = w.shape[1]
    z = lambda n: jnp.zeros((n, k), w.dtype)
    return jnp.concatenate([w[:COL_KR], z(64), w[COL_KR:], z(32)], axis=0)


def _unpad_w_in(w):
    return jnp.concatenate([w[:COL_KR], w[COL_KR + 64:COL_KR + 96]], axis=0)


def _assemble_weights(g, conv_b):
    half = N_DEV // 2
    cols = lambda a: a.transpose(1, 0, 2).reshape(a.shape[1], N_DEV * a.shape[2])
    make = {
        "w_in": lambda: _pad_w_in(g["w_in"].reshape(D_IN, D_MODEL)),
        "w_uq": lambda: _pad_to(g["w_uq"], (N_DEV, 128, Q_LORA)).reshape(N_DEV * 128, Q_LORA),
        "w_ukv": lambda: cols(g["w_ukv"]),
        "w_o": lambda: g["w_o"].reshape(D_MODEL, D_MODEL),
        "w_up": lambda: g["w_up"],
        "w_down": lambda: _pad_to(g["w_down"].reshape(half, FF_SHARD, D_MODEL),
                                  (half, FF_SLAB, D_MODEL)).reshape(half * FF_SLAB, D_MODEL),
        "conv_w": lambda: g["conv_w"][:, :3].reshape(2, half, 3, FF_SLAB).transpose(0, 2, 1, 3).reshape(2, 3, half * FF_SLAB),
    }
    fw = {n: make[n]() for n in g}
    if conv_b is not None:
        fw["conv_b"] = _pad_to(conv_b.reshape(2, 1, half, FF_SHARD), (2, 1, half, FF_SLAB)).reshape(2, 1, half * FF_SLAB)
    return fw


def _grad_bufs(grads, names):
    half = N_DEV // 2
    slabs = lambda a: a.reshape(a.shape[0], N_DEV, a.shape[1] // N_DEV).transpose(1, 0, 2)
    make = {
        "w_in": lambda: _unpad_w_in(grads["w_in"]).reshape((N_DEV,) + BUF_SHAPES["w_in"]),
        "w_uq": lambda: grads["w_uq"].reshape(N_DEV, 128, Q_LORA)[:, :QK_NOPE + QK_ROPE],
        "w_ukv": lambda: slabs(grads["w_ukv"]),
        "w_o": lambda: grads["w_o"].reshape((N_DEV,) + BUF_SHAPES["w_o"]),
        "w_up": lambda: grads["w_up"],
        "w_down": lambda: grads["w_down"].reshape(half, FF_SLAB, D_MODEL)[:, :FF_SHARD].reshape((N_DEV,) + BUF_SHAPES["w_down"]),
        "conv_w": lambda: _pad_to(grads["conv_w"].reshape(2, 3, half, FF_SLAB).transpose(0, 2, 1, 3).reshape(N_DEV, 3, FF_SLAB),
                                  (N_DEV,) + BUF_SHAPES["conv_w"]),
    }
    return [make[n]() if n == "conv_w" else make[n]().astype(BF16) for n in names]


def kernel(x, norm_mix_pre, w_in, q_lat_norm, w_uq, kv_lat_norm, w_ukv, out_norm_a, out_norm_b, w_o, norm_mix_post, norm_ffn_pre, w_up, conv_w, conv_b, w_down, norm_ffn_post, loss_target, m_norm_mix_pre, m_w_in, m_q_lat_norm, m_w_uq, m_kv_lat_norm, m_w_ukv, m_out_norm_a, m_out_norm_b, m_w_o, m_norm_mix_post, m_norm_ffn_pre, m_w_up, m_conv_w, m_conv_b, m_w_down, m_norm_ffn_post, v_norm_mix_pre, v_w_in, v_q_lat_norm, v_w_uq, v_kv_lat_norm, v_w_ukv, v_out_norm_a, v_out_norm_b, v_w_o, v_norm_mix_post, v_norm_ffn_pre, v_w_up, v_conv_w, v_conv_b, v_w_down, v_norm_ffn_post):
    given = dict(locals())
    shard = lambda a, n: a[0].T if n in TRANSPOSED else a[0]
    w = {n: shard(given[n], n) for n in WEIGHTS}
    m = {n: shard(given["m_" + n], n) for n in WEIGHTS}
    v = {n: shard(given["v_" + n], n) for n in WEIGHTS}
    rep_shapes = {n: w[n].shape for n in REPLICATED}

    buf = lambda n: _pad_to(w[n] if n == "conv_w" else w[n].astype(BF16), BUF_SHAPES[n])
    first = dict(zip(MIXING, _all_gather([buf(n) for n in MIXING], "weight_all_gather")))
    fw = _assemble_weights(first, None)
    tie = first["w_o"][0, 0, 0].astype(F32) * 0.0
    late_bufs = [buf(n) + tie.astype(w[n].dtype if n == "conv_w" else BF16) for n in MIXER]
    mixer_started = _exchange_start(late_bufs, False, "mixer_weights_start")
    rep = {n: given[n] for n in REPLICATED}
    rep["norm_mix_pre"] = rep["norm_mix_pre"] + mixer_started[4][0, 0]

    def mixer_weights(after):
        srcs, lands = _exchange_wait(mixer_started, False, after, "mixer_weights_wait")
        got = {n: _own_slot(land, own) for n, land, own in zip(MIXER, lands, srcs)}
        return _assemble_weights(got, conv_b)

    early = {}

    def early_grads(grads):
        early["started"] = _exchange_start(_grad_bufs(grads, EARLY_GRADS), True, "early_grads_start")
        return early["started"][4][0, 0]

    def late_grads(grads):
        early["late"] = _exchange_start(_grad_bufs(grads, LATE_GRADS), True, "late_grads_start")
        return early["late"][4][0, 0]

    loss_local, grad_x, grads = _local_step(x[0], loss_target[0], fw, rep, mixer_weights, early_grads, late_grads)

    grads["conv_b"] = grads["conv_b"].reshape(N_DEV, FF_SLAB)[:, :FF_SHARD]
    small = _pack([grads[n].reshape(-1) for n in REPLICATED] + [loss_local.reshape(1)], SMALL_ROWS)
    received_small = _grad_exchange([], small, "grad_exchange")[0]
    my = 4 * lax.axis_index("x") + 2 * lax.axis_index("y") + lax.axis_index("c")
    received = {}
    for names, key, tag in ((EARLY_GRADS, "started", "early_grads_wait"), (LATE_GRADS, "late", "late_grads_wait")):
        srcs, lands = _exchange_wait(early[key], True, received_small, tag)
        for n, land, src in zip(names, lands, srcs):
            received[n] = _own_slot(land, lax.dynamic_index_in_dim(src, my, 0, keepdims=False))
    results = [{}, {}, {}, {}]
    for n in BUF_ORDER:
        parts = received[n]
        if n == "conv_w":
            args = [_pad_to(t[n], BUF_SHAPES[n]) for t in (w, m, v)]
        else:
            args = [w[n], m[n], v[n]]
        outs = _adamw(parts, *args, f"adamw_{n}")
        for t in range(4):
            results[t][n] = outs[t][:w[n].shape[0], :w[n].shape[1]] if n == "conv_w" else outs[t]
    pk = lambda d: _pack([d[n].reshape(-1) for n in REPLICATED] + [jnp.zeros((1,), F32)], SMALL_ROWS)
    small_out = _adamw(received_small, pk(w), pk(m), pk(v), "adamw_replicated")
    rep_shapes["loss"] = (1,)
    for t in range(4):
        results[t].update(_unpack(small_out[t], rep_shapes))

    loss = results[0]["loss"][0]
    outs = [loss, grad_x[None]]
    for res in results:
        outs += [(res[n].T if n in TRANSPOSED else res[n])[None] for n in WEIGHTS]
    return tuple(outs)
```

```python
import math

import numpy as np
import jax
import jax.numpy as jnp
from jax import lax
from jax.experimental import pallas as pl
from jax.experimental.pallas import tpu as pltpu

F32 = jnp.float32
BF16 = jnp.bfloat16

D_MODEL = 1024
N_DEV = 8
WIDTH_A = 512
N_HEADS = 8
Q_LORA = 384
KV_LORA = 256
QK_ROPE = 32
QK_NOPE = 64
D_FF = 2816
FF_SHARD = 2 * D_FF // N_DEV
FF_SLAB = 768
DIL_CONFIGS = ((128, 1), (512, 4), (2048, 16))
BAND_HALF = 64
ROPE_BASE = 10000.0
EPS = 1e-6
NEG = -1e30
MLA_SCALE = (QK_NOPE + QK_ROPE) ** -0.5
SCALE_A = 0.125
LOG2E = 1.0 / math.log(2.0)
LN2 = math.log(2.0)
QSCALE_A = SCALE_A * LOG2E

COL_CQ = 3 * WIDTH_A
COL_CKV = COL_CQ + Q_LORA
COL_KR = COL_CKV + KV_LORA
N_PROJ = COL_KR + 128
N_LAT = N_PROJ - COL_CQ
D_IN = COL_KR + QK_ROPE

ADAM_LR = 0.001
ADAM_B1 = 0.9
ADAM_B2 = 0.999
ADAM_EPS = 1e-08
ADAM_WD = 0.01
ADAM_STEP = 10

LANES = 128
VMEM_LIMIT = 56 * 1024 * 1024

REPLICATED = ("norm_mix_pre", "q_lat_norm", "kv_lat_norm", "out_norm_a", "out_norm_b", "norm_mix_post",
              "norm_ffn_pre", "conv_b", "norm_ffn_post")
WEIGHTS = ("norm_mix_pre", "w_in", "q_lat_norm", "w_uq", "kv_lat_norm", "w_ukv", "out_norm_a", "out_norm_b", "w_o",
           "norm_mix_post", "norm_ffn_pre", "w_up", "conv_w", "conv_b", "w_down", "norm_ffn_post")


def _params(sem=None):
    return pltpu.CompilerParams(dimension_semantics=sem, vmem_limit_bytes=VMEM_LIMIT)


def _dot(a, b, dims):
    return lax.dot_general(a, b, (dims, ((), ())), preferred_element_type=F32)


NN = ((1,), (0,))
NT = ((1,), (1,))
TN = ((0,), (0,))


def _rstd(x):
    return lax.rsqrt(jnp.mean(x * x, axis=-1, keepdims=True) + EPS)


def _rms_bwd(dy, x, g):
    r = _rstd(x)
    z = x * r
    gy = dy * g
    dx = r * (gy - z * jnp.mean(gy * z, axis=-1, keepdims=True))
    return dx, dy * z


def _split_hi_lo(v):
    hi = v.astype(BF16)
    lo = (v - hi.astype(F32)).astype(BF16)
    return hi, lo


def _matmul(a, b, mode, out_dtype, tm, tn, tk, name):
    if mode == "nn":
        (m, k), n = a.shape, b.shape[1]
        a_spec = pl.BlockSpec((tm, tk), lambda i, j, s: (i, s))
        b_spec = pl.BlockSpec((tk, tn), lambda i, j, s: (s, j))
        dims = NN
    elif mode == "nt":
        (m, k), n = a.shape, b.shape[0]
        a_spec = pl.BlockSpec((tm, tk), lambda i, j, s: (i, s))
        b_spec = pl.BlockSpec((tn, tk), lambda i, j, s: (j, s))
        dims = NT
    else:
        (k, m), n = a.shape, b.shape[1]
        a_spec = pl.BlockSpec((tk, tm), lambda i, j, s: (s, i))
        b_spec = pl.BlockSpec((tk, tn), lambda i, j, s: (s, j))
        dims = TN
    assert m % tm == 0 and n % tn == 0 and k % tk == 0, (name, m, n, k, tm, tn, tk)
    return _matmul_core(a, b, dims, (m // tm, n // tn, k // tk), a_spec, b_spec,
                        pl.BlockSpec((tm, tn), lambda i, j, s: (i, j)), jax.ShapeDtypeStruct((m, n), out_dtype),
                        (tm, tn), name)


def _matmul_core(a, b, dims, grid, a_spec, b_spec, o_spec, out_sds, acc_shape, name):
    nk = grid[2]

    def body(a_ref, b_ref, o_ref, acc_ref):
        s = pl.program_id(2)

        @pl.when(s == 0)
        def _():
            acc_ref[...] = jnp.zeros_like(acc_ref)

        acc_ref[...] += _dot(a_ref[...].astype(BF16), b_ref[...].astype(BF16), dims)

        @pl.when(s == nk - 1)
        def _():
            o_ref[...] = acc_ref[...].astype(out_sds.dtype)

    return pl.pallas_call(
        body, name=name, grid=grid, in_specs=[a_spec, b_spec], out_specs=o_spec, out_shape=out_sds,
        scratch_shapes=[pltpu.VMEM(acc_shape, F32)],
        compiler_params=_params(("parallel", "parallel", "arbitrary")),
    )(a, b)


def _norm_matmul(x, g, w, name):
    s, k = x.shape
    tm = min(2048, s)
    nj, tn, _ = w.shape
    half = nj // 2
    w_spec = pl.BlockSpec((None, tn, k), lambda i, j: (j, 0, 0))
    o_spec = pl.BlockSpec((None, tm, tn), lambda i, j: (j // half, i, j % half))
    o_sds = jax.ShapeDtypeStruct((2, s, half * tn), BF16)

    def body(x_ref, g_ref, w_ref, o_ref, h_ref):
        @pl.when(pl.program_id(1) == 0)
        def _():
            xv = x_ref[...]
            h_ref[...] = (xv * _rstd(xv) * g_ref[...]).astype(BF16)

        o_ref[...] = _dot(h_ref[...], w_ref[...], NT).astype(BF16)

    return pl.pallas_call(
        body, name=name, grid=(s // tm, nj),
        in_specs=[pl.BlockSpec((tm, k), lambda i, j: (i, 0)),
                  pl.BlockSpec((1, k), lambda i, j: (0, 0)),
                  w_spec],
        out_specs=[o_spec, pl.BlockSpec((tm, k), lambda i, j: (i, 0))],
        out_shape=[o_sds, jax.ShapeDtypeStruct((s, k), BF16)],
        compiler_params=_params(("parallel", "arbitrary")),
    )(x, g, w)


def _band_bias(r):
    off = np.arange(256)[None, :] - BAND_HALF - np.arange(128)[:, None]
    slopes = np.exp2(-8.0 * np.arange(1, N_HEADS + 1, dtype=np.float32) / N_HEADS).astype(np.float32)
    dist = (np.abs(off) * r).astype(np.float32)
    bias = -slopes[:, None, None] * dist[None]
    bias = np.where((np.abs(off) <= BAND_HALF)[None], bias * np.float32(LOG2E), np.float32(NEG))
    return jnp.asarray(bias, F32)


def _band_call(mode, r, center, window, bias, name):
    seq = center[0][0].shape[0]
    tq = min(512, seq)
    nsub = tq // 128
    hb = tq // BAND_HALF
    nh = seq // BAND_HALF
    nc, nw = len(center), len(window)
    out_dtypes = {"fwd": (BF16, F32), "dq": (BF16,), "dkv": (BF16, BF16)}[mode]
    n_out = len(out_dtypes)

    def specs(col):
        return (pl.BlockSpec((BAND_HALF, WIDTH_A), lambda c, i: (jnp.maximum(i * hb - 1, 0), col(c))),
                pl.BlockSpec((tq, WIDTH_A), lambda c, i: (i, col(c))),
                pl.BlockSpec((BAND_HALF, WIDTH_A), lambda c, i: (jnp.minimum((i + 1) * hb, nh - 1), col(c))))

    cspec = pl.BlockSpec((tq, WIDTH_A), lambda c, i: (i, c))
    in_specs = [specs(col)[1] for _, col in center]
    operands = [a for a, _ in center]
    for a, col in window:
        in_specs += list(specs(col))
        operands += [a, a, a]
    in_specs.append(pl.BlockSpec((N_HEADS, 128, 256), lambda c, i: (0, 0, 0)))
    operands.append(bias)
    window = [a for a, _ in window]

    def aug_stat(base, stat_sw, lane, act, e0):
        hi, lo = _split_hi_lo(stat_sw)
        return jnp.where(act, base, jnp.where(lane == e0, -hi, jnp.where(lane == e0 + 1, -lo, jnp.zeros_like(hi))))

    def aug_ones(base, lane, e0):
        return jnp.where((lane == e0) | (lane == e0 + 1), jnp.ones_like(base), base)

    def body(*refs):
        c_refs = refs[:nc]
        w_refs = refs[nc:nc + 3 * nw]
        bias_ref = refs[nc + 3 * nw]
        o_refs = refs[nc + 3 * nw + 1:nc + 3 * nw + 1 + n_out]
        wins = refs[nc + 3 * nw + 1 + n_out:]
        i = pl.program_id(1)
        for t in range(nw):
            wins[t][0:BAND_HALF, :] = w_refs[3 * t][...]
            wins[t][BAND_HALF:BAND_HALF + tq, :] = w_refs[3 * t + 1][...]
            wins[t][BAND_HALF + tq:BAND_HALF + tq + BAND_HALF, :] = w_refs[3 * t + 2][...]

        def sub(j, carry):
            r0 = pl.multiple_of(j * 128, 128)
            wpos = i * tq + j * 128 - BAND_HALF + lax.broadcasted_iota(jnp.int32, (128, 256), 1)
            valid = (wpos >= 0) & (wpos < seq)
            lane_c = lax.broadcasted_iota(jnp.int32, (128, 128), 1)
            lane_w = lax.broadcasted_iota(jnp.int32, (256, 128), 1)
            heads = [(p, a) for p in range(4) for a in range(2)]
            first, last_ops = [], []
            for p, a in heads:
                cols = slice(p * 128, (p + 1) * 128)
                cs = [c[pl.ds(r0, 128), cols] for c in c_refs]
                ws = [w[pl.ds(r0, 256), cols] for w in wins]
                e0 = 64 if a == 0 else 0
                act_c = (lane_c < 64) if a == 0 else (lane_c >= 64)
                act_w = (lane_w < 64) if a == 0 else (lane_w >= 64)
                bias_a = bias_ref[2 * p + a]
                if mode == "fwd":
                    qa = jnp.where(act_c, cs[0] * QSCALE_A, jnp.zeros_like(cs[0]))
                    first.append((_dot(qa, ws[0], NT) + bias_a, None))
                    last_ops.append((ws[1],))
                elif mode == "dq":
                    q2, dy2, l2, d2 = cs
                    k2, v2 = ws
                    q_aug = aug_stat(q2 * QSCALE_A, pltpu.roll(l2, 64, 1), lane_c, act_c, e0)
                    dy_aug = aug_stat(dy2, pltpu.roll(d2, 64, 1), lane_c, act_c, e0)
                    first.append((_dot(q_aug, aug_ones(k2, lane_w, e0), NT) + bias_a,
                                  _dot(dy_aug, aug_ones(v2, lane_w, e0), NT)))
                    last_ops.append((k2,))
                else:
                    k2, v2 = cs
                    q2, dy2, l2, d2 = ws
                    q_aug = aug_stat(q2 * QSCALE_A, pltpu.roll(l2, 64, 1), lane_w, act_w, e0)
                    dy_aug = aug_stat(dy2, pltpu.roll(d2, 64, 1), lane_w, act_w, e0)
                    first.append((_dot(aug_ones(k2, lane_c, e0), q_aug, NT) + bias_a,
                                  _dot(aug_ones(v2, lane_c, e0), dy_aug, NT)))
                    last_ops.append((q_aug, dy_aug))
            mid = []
            for sc, dp in first:
                sc = jnp.where(valid, sc, NEG)
                if mode == "fwd":
                    m = jnp.max(sc, axis=-1, keepdims=True)
                    e = jnp.exp2(sc - m)
                    l = jnp.sum(e, axis=-1, keepdims=True)
                    mid.append((e.astype(BF16), l, m + jnp.log(l) * LOG2E))
                else:
                    pr = jnp.exp2(sc)
                    mid.append((pr.astype(BF16), (pr * dp).astype(BF16)))
            res = []
            for md, ops in zip(mid, last_ops):
                if mode == "fwd":
                    res.append((_dot(md[0], ops[0], NN) / md[1], jnp.broadcast_to(md[2], (128, 128))))
                elif mode == "dq":
                    res.append((_dot(md[1], ops[0], NN) * SCALE_A,))
                else:
                    res.append((_dot(md[1], ops[0], NN) * LN2, _dot(md[0], ops[1], NN)))
            for t in range(n_out):
                pairs = [jnp.where(lane_c < 64, res[2 * p][t], res[2 * p + 1][t]) for p in range(4)]
                o_refs[t][pl.ds(r0, 128), :] = jnp.concatenate(pairs, axis=1).astype(out_dtypes[t])
            return carry

        lax.fori_loop(0, nsub, sub, 0, unroll=True)

    outs = pl.pallas_call(
        body, name=name, grid=(r, seq // tq),
        in_specs=in_specs,
        out_specs=[cspec] * n_out,
        out_shape=[jax.ShapeDtypeStruct((seq, r * WIDTH_A), dt) for dt in out_dtypes],
        scratch_shapes=[pltpu.VMEM((tq + 2 * BAND_HALF, WIDTH_A), w.dtype) for w in window],
        compiler_params=_params(("parallel", "parallel")),
    )(*operands)
    return outs


def _slab_scratch(tm, w):
    return pltpu.VMEM((w // 128, tm, 128), F32)


def _put(scr, val):
    for j in range(scr.shape[0]):
        scr[j] = val[:, j * 128:(j + 1) * 128].astype(F32)


def _get(scr):
    return jnp.concatenate([scr[j] for j in range(scr.shape[0])], axis=1)


def _dilate_store(dst_ref, scr, r):
    nb, tm, _ = scr.shape
    w = nb * 128
    for c in range(r):
        for j in range(nb):
            dst_ref[:, c * w + j * 128:c * w + (j + 1) * 128] = scr[j, pl.ds(c, tm // r, stride=r), :].astype(dst_ref.dtype)


def _undilate(scr, src_ref, r):
    nb, tm, _ = scr.shape
    w = nb * 128
    for c in range(r):
        for j in range(nb):
            scr[j, pl.ds(c, tm // r, stride=r), :] = src_ref[:, c * w + j * 128:c * w + (j + 1) * 128].astype(F32)


def _dil_spec(tm, r, w):
    return pl.BlockSpec((tm // r, r * w), lambda i: (i, 0))


def _dil_shape(s, r, w, dtype):
    return jax.ShapeDtypeStruct((s // r, r * w), dtype)


def _in_proj(x, g, w, name):
    s, k = x.shape
    n = w.shape[0]
    tm = min(512, s)
    qkv = 3 * WIDTH_A

    def body(x_ref, g_ref, w_ref, o_ref, h_ref, d4_ref, d16_ref, scr):
        xv = x_ref[...]
        h = (xv * _rstd(xv) * g_ref[...]).astype(BF16)
        h_ref[...] = h
        acc = _dot(h, w_ref[...], NT)
        o_ref[...] = acc.astype(BF16)
        _put(scr, acc[:, 0:qkv])
        _dilate_store(d4_ref, scr, 4)
        _dilate_store(d16_ref, scr, 16)

    row = lambda c: pl.BlockSpec((tm, c), lambda i: (i, 0))
    return pl.pallas_call(
        body, name=name, grid=(s // tm,),
        in_specs=[row(k), pl.BlockSpec((1, k), lambda i: (0, 0)), pl.BlockSpec((n, k), lambda i: (0, 0))],
        out_specs=[row(n), row(k), _dil_spec(tm, 4, qkv), _dil_spec(tm, 16, qkv)],
        out_shape=[jax.ShapeDtypeStruct((s, n), BF16), jax.ShapeDtypeStruct((s, k), BF16),
                   _dil_shape(s, 4, qkv, BF16), _dil_shape(s, 16, qkv, BF16)],
        scratch_shapes=[_slab_scratch(tm, qkv)],
        compiler_params=_params(("parallel",)),
    )(x, g, w)


def _band_combine(os_, lses, name):
    s = os_[0].shape[0]
    tm = min(512, s)

    def body(o1, o4, o16, l1, l4, l16, ya_ref, lse_ref, lse4_ref, lse16_ref, so4, sl4, so16, sl16):
        _undilate(so4, o4, 4)
        _undilate(sl4, l4, 4)
        _undilate(so16, o16, 16)
        _undilate(sl16, l16, 16)
        a0, a1, a2 = l1[...], _get(sl4), _get(sl16)
        m = jnp.maximum(jnp.maximum(a0, a1), a2)
        e0, e1, e2 = jnp.exp2(a0 - m), jnp.exp2(a1 - m), jnp.exp2(a2 - m)
        den = e0 + e1 + e2
        ya_ref[...] = (e0 * o1[...] + e1 * _get(so4) + e2 * _get(so16)) / den
        lse = m + jnp.log(den) * LOG2E
        lse_ref[...] = lse
        _put(sl4, lse)
        _dilate_store(lse4_ref, sl4, 4)
        _dilate_store(lse16_ref, sl4, 16)

    nat = pl.BlockSpec((tm, WIDTH_A), lambda i: (i, 0))
    d4, d16 = _dil_spec(tm, 4, WIDTH_A), _dil_spec(tm, 16, WIDTH_A)
    return pl.pallas_call(
        body, name=name, grid=(s // tm,), in_specs=[nat, d4, d16] * 2, out_specs=[nat, nat, d4, d16],
        out_shape=[jax.ShapeDtypeStruct((s, WIDTH_A), F32)] * 2
        + [_dil_shape(s, 4, WIDTH_A, F32), _dil_shape(s, 16, WIDTH_A, F32)],
        scratch_shapes=[_slab_scratch(tm, WIDTH_A)] * 4,
        compiler_params=_params(("parallel",)),
    )(*os_, *lses)


def _rope_tables(s):
    pos = jnp.arange(s, dtype=F32)
    inv_freq = jnp.exp(-math.log(ROPE_BASE) * jnp.arange(0, QK_ROPE, 2, dtype=F32) / QK_ROPE)
    ang = pos[:, None] * inv_freq[None, :]
    cos, sin = jnp.cos(ang), jnp.sin(ang)
    one = jnp.ones((s, 64), F32)
    zero16 = jnp.zeros((s, 16), F32)
    c = jnp.concatenate([one, cos, cos, jnp.ones((s, 32), F32)], axis=1)
    sa = jnp.concatenate([jnp.zeros((s, 64), F32), -sin, zero16, jnp.zeros((s, 32), F32)], axis=1)
    sb = jnp.concatenate([jnp.zeros((s, 64), F32), zero16, sin, jnp.zeros((s, 32), F32)], axis=1)
    return c, sa, sb


def _rope_fwd(x, c, sa, sb):
    return x * c + pltpu.roll(x, 112, 1) * sa + pltpu.roll(x, 16, 1) * sb


def _rope_bwd(dy, c, sa, sb):
    return dy * c + pltpu.roll(dy * sa, 16, 1) + pltpu.roll(dy * sb, 112, 1)


def _mla_prep(proj, g_q, g_kv, w_uq, w_ukv, tabs, name):
    s = proj.shape[0]
    tm = min(512, s)
    width = N_HEADS * 128

    def body(lat_ref, gq_ref, gkv_ref, wq_ref, wkv_ref, c_ref, sa_ref, sb_ref,
             q_ref, k_ref, kv_ref, cqn_ref, ckvn_ref):
        c, sa, sb = c_ref[...], sa_ref[...], sb_ref[...]
        cq = lat_ref[:, 0:Q_LORA].astype(F32)
        cqn = (cq * _rstd(cq) * gq_ref[...]).astype(BF16)
        cqn_ref[...] = cqn
        q = _dot(cqn, wq_ref[...], NT)
        ckv = lat_ref[:, Q_LORA:Q_LORA + KV_LORA].astype(F32)
        ckvn = (ckv * _rstd(ckv) * gkv_ref[...]).astype(BF16)
        ckvn_ref[...] = ckvn
        kv = _dot(ckvn, wkv_ref[...], NN)
        lane = lax.broadcasted_iota(jnp.int32, (tm, 128), 1)
        krr = _rope_fwd(lat_ref[:, Q_LORA + KV_LORA:].astype(F32), c, sa, sb)
        krr = jnp.where((lane == 96) | (lane == 97), 1.0, krr)
        ones01 = jnp.where(lane < 2, 1.0, 0.0)
        for h in range(N_HEADS):
            cols = slice(h * 128, (h + 1) * 128)
            q_ref[:, cols] = (_rope_fwd(q[:, cols], c, sa, sb) * (MLA_SCALE * LOG2E)).astype(BF16)
            k_ref[:, cols] = jnp.where(lane < 64, kv[:, cols], krr).astype(BF16)
            kv_ref[:, cols] = jnp.where(lane < 64, ones01, kv[:, cols]).astype(BF16)

    row = lambda n: pl.BlockSpec((tm, n), lambda i: (i, 0))
    full = lambda a: pl.BlockSpec(a.shape, lambda i: (0, 0))
    tab = pl.BlockSpec((tm, 128), lambda i: (i, 0))
    return pl.pallas_call(
        body, name=name, grid=(s // tm,),
        in_specs=[pl.BlockSpec((tm, N_LAT), lambda i: (i, COL_CQ // N_LAT)),
                  full(g_q), full(g_kv), full(w_uq), full(w_ukv), tab, tab, tab],
        out_specs=[row(width), row(width), row(width), row(Q_LORA), row(KV_LORA)],
        out_shape=[jax.ShapeDtypeStruct((s, width), BF16)] * 3
        + [jax.ShapeDtypeStruct((s, Q_LORA), BF16), jax.ShapeDtypeStruct((s, KV_LORA), BF16)],
        compiler_params=_params(("parallel",)),
    )(proj, g_q, g_kv, w_uq, w_ukv, *tabs)


def _mla_fwd(qcat, kcat, kvb, name):
    s = qcat.shape[0]
    tq = min(1024, s)
    tk = min(1024, s)
    nkc = s // tk

    def body(q_ref, k_ref, v_ref, yb_ref, qaug_ref, m_ref, acc_ref):
        lane = lax.broadcasted_iota(jnp.int32, (tq, 128), 1)
        m_ref[...] = jnp.full((2, tq, 128), NEG, F32)
        acc_ref[...] = jnp.zeros((2, tq, 128), F32)

        def chunk(cidx, carry):
            k0 = pl.multiple_of(cidx * tk, tk)
            cols = [slice(a * 128, (a + 1) * 128) for a in range(2)]
            scs = [_dot(q_ref[:, c], k_ref[pl.ds(k0, tk), c], NT) for c in cols]
            prs, alphas = [], []
            for a, sc in enumerate(scs):
                m_prev = m_ref[a]
                m_new = jnp.maximum(m_prev, jnp.max(sc, axis=-1, keepdims=True))
                alphas.append(jnp.exp2(m_prev - m_new))
                prs.append(jnp.exp2(sc - jnp.tile(m_new, (1, tk // 128))).astype(BF16))
                m_ref[a] = m_new
            for a, c in enumerate(cols):
                acc_ref[a] = alphas[a] * acc_ref[a] + _dot(prs[a], v_ref[pl.ds(k0, tk), c], NN)
            return carry

        lax.fori_loop(0, nkc, chunk, 0)
        outs = []
        for a in range(2):
            cols = slice(a * 128, (a + 1) * 128)
            acc = acc_ref[a]
            l = acc[:, 0:1]
            outs.append(acc / l)
            hi, lo = _split_hi_lo(m_ref[a] + jnp.log(l) * LOG2E)
            qaug_ref[:, cols] = jnp.where(lane == 96, -hi, jnp.where(lane == 97, -lo, q_ref[:, cols]))
        yb_ref[...] = jnp.where(lane < 64, pltpu.roll(outs[0], 64, 1), outs[1])

    return pl.pallas_call(
        body, name=name, grid=(4, s // tq),
        in_specs=[pl.BlockSpec((tq, 256), lambda p, i: (i, p)),
                  pl.BlockSpec((s, 256), lambda p, i: (0, p)),
                  pl.BlockSpec((s, 256), lambda p, i: (0, p))],
        out_specs=[pl.BlockSpec((tq, 128), lambda p, i: (i, p)),
                   pl.BlockSpec((tq, 256), lambda p, i: (i, p))],
        out_shape=[jax.ShapeDtypeStruct((s, WIDTH_A), F32), jax.ShapeDtypeStruct((s, N_HEADS * 128), BF16)],
        scratch_shapes=[pltpu.VMEM((2, tq, 128), F32)] * 2,
        compiler_params=_params(("parallel", "parallel")),
    )(qcat, kcat, kvb)


def _mla_bwd(qaug, kcat, kvb, doaug, name):
    s = qaug.shape[0]
    tq = min(1024, s)
    tk = min(512, s)
    nqc = s // tq
    width = N_HEADS * 128

    def body(q_ref, do_ref, k_ref, v_ref, dq_ref, dk_acc, dv_acc):
        j = pl.program_id(1)

        @pl.when(j == 0)
        def _():
            dq_ref[...] = jnp.zeros_like(dq_ref)

        dk_acc[...] = jnp.zeros_like(dk_acc)
        dv_acc[...] = jnp.zeros_like(dv_acc)

        def chunk(cidx, carry):
            q0 = pl.multiple_of(cidx * tq, tq)
            cols = [slice(a * 128, (a + 1) * 128) for a in range(2)]
            qs = [q_ref[pl.ds(q0, tq), c] for c in cols]
            dos = [do_ref[pl.ds(q0, tq), c] for c in cols]
            kbs = [k_ref[:, c] for c in cols]
            sts = [_dot(kbs[a], qs[a], NT) for a in range(2)]
            dps = [_dot(v_ref[:, cols[a]], dos[a], NT) for a in range(2)]
            pts, dsts = [], []
            for a in range(2):
                pt = jnp.exp2(sts[a])
                pts.append(pt.astype(BF16))
                dsts.append((pt * dps[a]).astype(BF16))
            for a, c in enumerate(cols):
                dv_acc[:, c] += _dot(pts[a], dos[a], NN)
                dk_acc[:, c] += _dot(dsts[a], qs[a], NN)
                dq_ref[pl.ds(q0, tq), c] += _dot(dsts[a], kbs[a], TN)
            return carry

        lax.fori_loop(0, nqc, chunk, 0)

    return pl.pallas_call(
        body, name=name, grid=(N_HEADS // 2, s // tk),
        in_specs=[pl.BlockSpec((s, 256), lambda p, j: (0, p)),
                  pl.BlockSpec((s, 256), lambda p, j: (0, p)),
                  pl.BlockSpec((tk, 256), lambda p, j: (j, p)),
                  pl.BlockSpec((tk, 256), lambda p, j: (j, p))],
        out_specs=[pl.BlockSpec((s, 256), lambda p, j: (0, p)),
                   pl.BlockSpec((tk, 256), lambda p, j: (j, p)),
                   pl.BlockSpec((tk, 256), lambda p, j: (j, p))],
        out_shape=[jax.ShapeDtypeStruct((s, width), F32)] * 3,
        compiler_params=_params(("parallel", "arbitrary")),
    )(qaug, doaug, kcat, kvb)


def _mla_prep_bwd(dqc, dkc, dvp, proj, cqn, ckvn, g_q, g_kv, w_uq, w_ukv, tabs, dproj, name):
    s = proj.shape[0]
    tm = min(512, s)
    width = N_HEADS * 128

    def body(dq_ref, dk_ref, dv_ref, lat_ref, cqn_ref, ckvn_ref, gq_ref, gkv_ref, wq_ref, wkv_ref,
             c_ref, sa_ref, sb_ref, _dproj_in, dproj_ref, dwq_ref, dwkv_ref, dgq_ref, dgkv_ref):
        i = pl.program_id(0)

        @pl.when(i == 0)
        def _():
            dwq_ref[...] = jnp.zeros_like(dwq_ref)
            dwkv_ref[...] = jnp.zeros_like(dwkv_ref)
            dgq_ref[...] = jnp.zeros_like(dgq_ref)
            dgkv_ref[...] = jnp.zeros_like(dgkv_ref)

        c, sa, sb = c_ref[...], sa_ref[...], sb_ref[...]
        lane = lax.broadcasted_iota(jnp.int32, (tm, 128), 1)
        dkr = jnp.zeros((tm, 128), F32)
        dq_parts, dkv_parts = [], []
        for h in range(N_HEADS):
            cols = slice(h * 128, (h + 1) * 128)
            dq_parts.append(_rope_bwd(dq_ref[:, cols] * MLA_SCALE, c, sa, sb).astype(BF16))
            dkh = dk_ref[:, cols] * LN2
            dkr = dkr + dkh
            dkv_parts.append(jnp.where(lane < 64, dkh, dv_ref[:, cols]).astype(BF16))
        dq = jnp.concatenate(dq_parts, axis=1)
        dkv = jnp.concatenate(dkv_parts, axis=1)
        dkr = _rope_bwd(jnp.where((lane >= 64) & (lane < 96), dkr, 0.0), c, sa, sb)

        dcqn = _dot(dq, wq_ref[...], NN)
        dwq_ref[...] += _dot(dq, cqn_ref[...], TN)
        dcq, dgq = _rms_bwd(dcqn, lat_ref[:, 0:Q_LORA].astype(F32), gq_ref[...])
        dgq_ref[...] += jnp.sum(dgq, axis=0, keepdims=True)

        dckvn = _dot(dkv, wkv_ref[...], NT)
        dwkv_ref[...] += _dot(ckvn_ref[...], dkv, TN)
        dckv, dgkv = _rms_bwd(dckvn, lat_ref[:, Q_LORA:Q_LORA + KV_LORA].astype(F32), gkv_ref[...])
        dgkv_ref[...] += jnp.sum(dgkv, axis=0, keepdims=True)

        dproj_ref[:, 0:Q_LORA] = dcq.astype(BF16)
        dproj_ref[:, Q_LORA:Q_LORA + KV_LORA] = dckv.astype(BF16)
        dproj_ref[:, Q_LORA + KV_LORA:] = dkr.astype(BF16)

    row = lambda n: pl.BlockSpec((tm, n), lambda i: (i, 0))
    full = lambda a: pl.BlockSpec(a.shape, lambda i: (0, 0))
    tab = pl.BlockSpec((tm, 128), lambda i: (i, 0))
    return pl.pallas_call(
        body, name=name, grid=(s // tm,),
        in_specs=[row(width), row(width), row(width),
                  pl.BlockSpec((tm, N_LAT), lambda i: (i, COL_CQ // N_LAT)),
                  row(Q_LORA), row(KV_LORA), full(g_q), full(g_kv), full(w_uq), full(w_ukv), tab, tab, tab,
                  pl.BlockSpec(memory_space=pl.ANY)],
        out_specs=[pl.BlockSpec((tm, N_LAT), lambda i: (i, COL_CQ // N_LAT)), full(w_uq), full(w_ukv), full(g_q), full(g_kv)],
        out_shape=[jax.ShapeDtypeStruct(dproj.shape, BF16),
                   jax.ShapeDtypeStruct(w_uq.shape, F32), jax.ShapeDtypeStruct(w_ukv.shape, F32),
                   jax.ShapeDtypeStruct(g_q.shape, F32), jax.ShapeDtypeStruct(g_kv.shape, F32)],
        input_output_aliases={13: 0},
        compiler_params=_params(("arbitrary",)),
    )(dqc, dkc, dvp, proj, cqn, ckvn, g_q, g_kv, w_uq, w_ukv, *tabs, dproj)


def _mix_out(ya, yb, na, nb, w_o, g_post, x, name):
    s = x.shape[0]
    tm = min(512, s)

    def body(ya_ref, yb_ref, na_ref, nb_ref, w_ref, g_ref, x_ref, yn_ref, y2_ref, x1_ref):
        a, b = ya_ref[...], yb_ref[...]
        yn = jnp.concatenate([a * _rstd(a) * na_ref[...], b * _rstd(b) * nb_ref[...]], axis=1).astype(BF16)
        yn_ref[...] = yn
        y2 = _dot(yn, w_ref[...], NN)
        y2_ref[...] = y2
        x1_ref[...] = x_ref[...] + y2 * _rstd(y2) * g_ref[...]

    row = lambda n: pl.BlockSpec((tm, n), lambda i: (i, 0))
    full = lambda a: pl.BlockSpec(a.shape, lambda i: (0, 0))
    return pl.pallas_call(
        body, name=name, grid=(s // tm,),
        in_specs=[row(WIDTH_A), row(WIDTH_A), full(na), full(nb), full(w_o), full(g_post), row(D_MODEL)],
        out_specs=[row(D_MODEL)] * 3,
        out_shape=[jax.ShapeDtypeStruct((s, D_MODEL), BF16), jax.ShapeDtypeStruct((s, D_MODEL), F32),
                   jax.ShapeDtypeStruct((s, D_MODEL), F32)],
        compiler_params=_params(("parallel",)),
    )(ya, yb, na, nb, w_o, g_post, x)


def _head_ones():
    blk = np.kron(np.eye(N_HEADS, dtype=np.float32), np.ones((64, 64), np.float32))
    return jnp.asarray(blk, F32)


def _outnorm_bwd(dyn, ya, yb, na, nb, ones, name):
    s = ya.shape[0]
    tm = min(512, s)

    def body(dyn_ref, ya_ref, yb_ref, na_ref, nb_ref, ones_ref, dya_ref, da_ref, do_ref, dna_ref, dnb_ref,
             dya4_ref, dya16_ref, da4_ref, da16_ref, scr):
        i = pl.program_id(0)

        @pl.when(i == 0)
        def _():
            dna_ref[...] = jnp.zeros_like(dna_ref)
            dnb_ref[...] = jnp.zeros_like(dnb_ref)

        a, b = ya_ref[...], yb_ref[...]
        dya, dna = _rms_bwd(dyn_ref[:, 0:WIDTH_A], a, na_ref[...])
        dyb, dnb = _rms_bwd(dyn_ref[:, WIDTH_A:], b, nb_ref[...])
        dna_ref[...] += jnp.sum(dna, axis=0, keepdims=True)
        dnb_ref[...] += jnp.sum(dnb, axis=0, keepdims=True)
        dya_b = dya.astype(BF16)
        dya_ref[...] = dya_b
        hp = lax.Precision.HIGHEST
        delta_a = jnp.dot(dya_b.astype(F32) * a, ones_ref[...], precision=hp, preferred_element_type=F32)
        da_ref[...] = delta_a
        _put(scr, dya_b)
        _dilate_store(dya4_ref, scr, 4)
        _dilate_store(dya16_ref, scr, 16)
        _put(scr, delta_a)
        _dilate_store(da4_ref, scr, 4)
        _dilate_store(da16_ref, scr, 16)
        dyb_b = dyb.astype(BF16)
        db = jnp.dot(dyb_b.astype(F32) * b, ones_ref[...], precision=hp, preferred_element_type=F32)
        lane = lax.broadcasted_iota(jnp.int32, (tm, 128), 1)
        zero = jnp.zeros((tm, 128), BF16)
        for p in range(4):
            cols = slice(p * 128, (p + 1) * 128)
            dyp = dyb_b[:, cols]
            dbp = db[:, cols]
            for a_ in range(2):
                src = pltpu.roll(dyp.astype(F32), 64, 1).astype(BF16) if a_ == 0 else dyp
                dlt = dbp if a_ == 0 else pltpu.roll(dbp, 64, 1)
                hi, lo = _split_hi_lo(dlt)
                blk = jnp.where(lane >= 64, src, jnp.where(lane == 0, -hi, jnp.where(lane == 1, -lo, zero)))
                h = 2 * p + a_
                do_ref[:, h * 128:(h + 1) * 128] = blk

    row = lambda n: pl.BlockSpec((tm, n), lambda i: (i, 0))
    full = lambda a: pl.BlockSpec(a.shape, lambda i: (0, 0))
    return pl.pallas_call(
        body, name=name, grid=(s // tm,),
        in_specs=[row(D_MODEL), row(WIDTH_A), row(WIDTH_A), full(na), full(nb), full(ones)],
        out_specs=[row(WIDTH_A), row(WIDTH_A), row(N_HEADS * 128), full(na), full(nb),
                   _dil_spec(tm, 4, WIDTH_A), _dil_spec(tm, 16, WIDTH_A), _dil_spec(tm, 4, WIDTH_A), _dil_spec(tm, 16, WIDTH_A)],
        out_shape=[jax.ShapeDtypeStruct((s, WIDTH_A), BF16), jax.ShapeDtypeStruct((s, WIDTH_A), F32),
                   jax.ShapeDtypeStruct((s, N_HEADS * 128), BF16),
                   jax.ShapeDtypeStruct(na.shape, F32), jax.ShapeDtypeStruct(nb.shape, F32),
                   _dil_shape(s, 4, WIDTH_A, BF16), _dil_shape(s, 16, WIDTH_A, BF16),
                   _dil_shape(s, 4, WIDTH_A, F32), _dil_shape(s, 16, WIDTH_A, F32)],
        scratch_shapes=[_slab_scratch(tm, WIDTH_A)],
        compiler_params=_params(("arbitrary",)),
    )(dyn, ya, yb, na, nb, ones)


def _sum_cast(parts, name):
    s = parts[0][0].shape[0]
    tm = min(512, s)

    def body(*refs):
        o_ref, s4, s16 = refs[9:]
        for t in range(3):
            _undilate(s4, refs[3 + t], 4)
            _undilate(s16, refs[6 + t], 16)
            acc = refs[t][...] + _get(s4) + _get(s16)
            o_ref[:, t * WIDTH_A:(t + 1) * WIDTH_A] = acc.astype(BF16)

    nat = pl.BlockSpec((tm, WIDTH_A), lambda i: (i, 0))
    flat = [parts[g][t] for g in range(3) for t in range(3)]
    return pl.pallas_call(
        body, name=name, grid=(s // tm,),
        in_specs=[nat] * 3 + [_dil_spec(tm, 4, WIDTH_A)] * 3 + [_dil_spec(tm, 16, WIDTH_A)] * 3,
        out_specs=pl.BlockSpec((tm, 3 * WIDTH_A), lambda i: (i, 0)),
        out_shape=jax.ShapeDtypeStruct((s, N_PROJ), BF16),
        scratch_shapes=[_slab_scratch(tm, WIDTH_A)] * 2,
        compiler_params=_params(("parallel",)),
    )(*flat)


HALO = 16


def _gelu(x):
    k = math.sqrt(2.0 / math.pi)
    t = jnp.tanh(k * (x + 0.044715 * x * x * x))
    return 0.5 * x * (1.0 + t), t


def _gelu_grad(x, t):
    k = math.sqrt(2.0 / math.pi)
    return 0.5 * (1.0 + t) + 0.5 * x * (1.0 - t * t) * k * (1.0 + 3 * 0.044715 * x * x)


def _halo_specs(s, tm, tn, lead):
    nb = s // HALO
    hb = tm // HALO
    pre = (lead,) if lead else ()
    z = (0,) if lead else ()
    main = pl.BlockSpec(pre + (tm, tn), lambda j, i: z + (i, j))
    prev = pl.BlockSpec(pre + (HALO, tn), lambda j, i: z + (jnp.maximum(i * hb - 1, 0), j))
    nxt = pl.BlockSpec(pre + (HALO, tn), lambda j, i: z + (jnp.minimum((i + 1) * hb, nb - 1), j))
    return [prev, main, nxt]


def _fill_ext(ext, prev, main, nxt, i, tm, s):
    ext[0:HALO, :] = jnp.where(i > 0, prev.astype(F32), 0.0)
    ext[HALO:HALO + tm, :] = main.astype(F32)
    ext[HALO + tm:2 * HALO + tm, :] = jnp.where((i + 1) * tm < s, nxt.astype(F32), 0.0)


STRIP = 16


def _shifted(ref, row0):
    n = STRIP + 16
    win = ref[pl.ds(pl.multiple_of(row0 - 8, 8), n), :]
    return pltpu.roll(win, 1, 0)[8:8 + STRIP], win[8:8 + STRIP], pltpu.roll(win, n - 1, 0)[8:8 + STRIP]


def _conv3(e, row0, w_ref, b_ref, t):
    m1, c0, p1 = _shifted(e, row0)
    return w_ref[t, 0:1, :] * m1 + w_ref[t, 1:2, :] * c0 + w_ref[t, 2:3, :] * p1 + b_ref[t]


def _conv_gate(up, cw, cb, name):
    _, s, c = up.shape
    tm = min(1024, s)
    tn = FF_SLAB

    def body(up_p, up_m, up_n, w_ref, b_ref, a_ref, eg, ev):
        i = pl.program_id(1)
        _fill_ext(eg, up_p[0], up_m[0], up_n[0], i, tm, s)
        _fill_ext(ev, up_p[1], up_m[1], up_n[1], i, tm, s)

        def strip(t, carry):
            r0 = pl.multiple_of(t * STRIP, STRIP)
            g, _ = _gelu(_conv3(eg, HALO + r0, w_ref, b_ref, 0))
            a_ref[pl.ds(r0, STRIP), :] = (g * _conv3(ev, HALO + r0, w_ref, b_ref, 1)).astype(BF16)
            return carry

        lax.fori_loop(0, tm // STRIP, strip, 0, unroll=2)

    return pl.pallas_call(
        body, name=name, grid=(c // tn, s // tm),
        in_specs=_halo_specs(s, tm, tn, 2)
        + [pl.BlockSpec((2, 3, tn), lambda j, i: (0, 0, j)), pl.BlockSpec((2, 1, tn), lambda j, i: (0, 0, j))],
        out_specs=pl.BlockSpec((tm, tn), lambda j, i: (i, j)),
        out_shape=jax.ShapeDtypeStruct((s, c), BF16),
        scratch_shapes=[pltpu.VMEM((tm + 2 * HALO, tn), F32)] * 2,
        compiler_params=_params(("parallel", "parallel")),
    )(up, up, up, cw, cb)


def _conv_gate_bwd(up, da, cw, cb, name):
    _, s, c = up.shape
    tm = min(512, s)
    tn = FF_SLAB
    te = tm + HALO

    def body(up_p, up_m, up_n, da_p, da_m, da_n, w_ref, b_ref, dup_ref, dw_ref, db_ref, eg, ev, ed, dug, duv):
        i = pl.program_id(1)

        @pl.when(i == 0)
        def _():
            dw_ref[...] = jnp.zeros_like(dw_ref)
            db_ref[...] = jnp.zeros_like(db_ref)

        _fill_ext(eg, up_p[0], up_m[0], up_n[0], i, tm, s)
        _fill_ext(ev, up_p[1], up_m[1], up_n[1], i, tm, s)
        _fill_ext(ed, da_p[...], da_m[...], da_n[...], i, tm, s)
        o = HALO // 2

        def du_strip(t, carry):
            r0 = pl.multiple_of(t * STRIP, STRIP)
            ug = _conv3(eg, o + r0, w_ref, b_ref, 0)
            uv = _conv3(ev, o + r0, w_ref, b_ref, 1)
            gl, th = _gelu(ug)
            dav = ed[pl.ds(pl.multiple_of(o + r0, 8), STRIP), :]
            dug[pl.ds(r0, STRIP), :] = dav * uv * _gelu_grad(ug, th)
            duv[pl.ds(r0, STRIP), :] = dav * gl
            return carry

        lax.fori_loop(0, te // STRIP, du_strip, 0)

        def back(du, e, t):
            def strip(k, acc):
                r0 = pl.multiple_of(k * STRIP, STRIP)
                dm1, c0, dp1 = _shifted(du, o + r0)
                dup_ref[t, pl.ds(r0, STRIP), :] = (w_ref[t, 0:1, :] * dp1 + w_ref[t, 1:2, :] * c0
                                                   + w_ref[t, 2:3, :] * dm1).astype(BF16)
                um1, u0, up1 = _shifted(e, HALO + r0)
                fold = lambda a: a[0:8] + a[8:16]
                return (acc[0] + fold(um1 * c0), acc[1] + fold(u0 * c0), acc[2] + fold(up1 * c0), acc[3] + fold(c0))

            zero = jnp.zeros((8, tn), F32)
            acc = lax.fori_loop(0, tm // STRIP, strip, (zero, zero, zero, zero), unroll=2)
            for k in range(3):
                dw_ref[t, k:k + 1, :] += jnp.sum(acc[k], axis=0, keepdims=True)
            db_ref[t] += jnp.sum(acc[3], axis=0, keepdims=True)

        back(dug, eg, 0)
        back(duv, ev, 1)

    wspec = pl.BlockSpec((2, 3, tn), lambda j, i: (0, 0, j))
    bspec = pl.BlockSpec((2, 1, tn), lambda j, i: (0, 0, j))
    return pl.pallas_call(
        body, name=name, grid=(c // tn, s // tm),
        in_specs=_halo_specs(s, tm, tn, 2) + _halo_specs(s, tm, tn, 0) + [wspec, bspec],
        out_specs=[pl.BlockSpec((2, tm, tn), lambda j, i: (0, i, j)), wspec, bspec],
        out_shape=[jax.ShapeDtypeStruct((2, s, c), BF16), jax.ShapeDtypeStruct((2, 3, c), F32),
                   jax.ShapeDtypeStruct((2, 1, c), F32)],
        scratch_shapes=[pltpu.VMEM((tm + 2 * HALO, tn), F32)] * 3 + [pltpu.VMEM((te, tn), F32)] * 2,
        compiler_params=_params(("parallel", "arbitrary")),
    )(up, up, up, da, da, da, cw, cb)


def _ffn_out(a, w_down, g_post, x1, target, name):
    s = x1.shape[0]
    tm = min(512, s)

    def body(a_ref, w_ref, g_ref, x1_ref, t_ref, dy3_ref, dx2_ref, loss_ref, dg_ref):
        i = pl.program_id(0)

        @pl.when(i == 0)
        def _():
            loss_ref[...] = jnp.zeros_like(loss_ref)
            dg_ref[...] = jnp.zeros_like(dg_ref)

        y3 = _dot(a_ref[...], w_ref[...], NN)
        g = g_ref[...]
        x2 = x1_ref[...] + y3 * _rstd(y3) * g
        diff = x2 - t_ref[...]
        loss_ref[...] += jnp.sum(jnp.sum(diff * diff, axis=1, keepdims=True), axis=0, keepdims=True)
        dx2 = diff * (1.0 / D_MODEL)
        dx2_ref[...] = dx2
        dy3, dg = _rms_bwd(dx2, y3, g)
        dy3_ref[...] = dy3.astype(BF16)
        dg_ref[...] += jnp.sum(dg, axis=0, keepdims=True)

    row = lambda n: pl.BlockSpec((tm, n), lambda i: (i, 0))
    full = lambda t: pl.BlockSpec(t.shape, lambda i: (0, 0))
    return pl.pallas_call(
        body, name=name, grid=(s // tm,),
        in_specs=[row(a.shape[1]), full(w_down), full(g_post), row(D_MODEL), row(D_MODEL)],
        out_specs=[row(D_MODEL), row(D_MODEL), pl.BlockSpec((8, 128), lambda i: (0, 0)), full(g_post)],
        out_shape=[jax.ShapeDtypeStruct((s, D_MODEL), BF16), jax.ShapeDtypeStruct((s, D_MODEL), F32),
                   jax.ShapeDtypeStruct((8, 128), F32), jax.ShapeDtypeStruct(g_post.shape, F32)],
        compiler_params=_params(("arbitrary",)),
    )(a, w_down, g_post, x1, target)


def _resnorm_bwd(dh2, x1, g_ffn_pre, dx2, y2, g_mix_post, name):
    s = x1.shape[0]
    tm = min(512, s)

    def body(dh_ref, x1_ref, gf_ref, dx2_ref, y2_ref, gp_ref, dx1_ref, dy2_ref, dgf_ref, dgp_ref):
        i = pl.program_id(0)

        @pl.when(i == 0)
        def _():
            dgf_ref[...] = jnp.zeros_like(dgf_ref)
            dgp_ref[...] = jnp.zeros_like(dgp_ref)

        dn, dgf = _rms_bwd(dh_ref[...], x1_ref[...], gf_ref[...])
        dx1 = dx2_ref[...] + dn
        dx1_ref[...] = dx1
        dgf_ref[...] += jnp.sum(dgf, axis=0, keepdims=True)
        dy2, dgp = _rms_bwd(dx1, y2_ref[...], gp_ref[...])
        dy2_ref[...] = dy2.astype(BF16)
        dgp_ref[...] += jnp.sum(dgp, axis=0, keepdims=True)

    row = pl.BlockSpec((tm, D_MODEL), lambda i: (i, 0))
    full = pl.BlockSpec((1, D_MODEL), lambda i: (0, 0))
    return pl.pallas_call(
        body, name=name, grid=(s // tm,),
        in_specs=[row, row, full, row, row, full],
        out_specs=[row, row, full, full],
        out_shape=[jax.ShapeDtypeStruct((s, D_MODEL), F32), jax.ShapeDtypeStruct((s, D_MODEL), BF16),
                   jax.ShapeDtypeStruct((1, D_MODEL), F32), jax.ShapeDtypeStruct((1, D_MODEL), F32)],
        compiler_params=_params(("arbitrary",)),
    )(dh2, x1, g_ffn_pre, dx2, y2, g_mix_post)


def _final_bwd(dh1, x, g_pre, dx1, name):
    s = x.shape[0]
    tm = min(512, s)

    def body(dh_ref, x_ref, g_ref, dx1_ref, dx_ref, dg_ref):
        @pl.when(pl.program_id(0) == 0)
        def _():
            dg_ref[...] = jnp.zeros_like(dg_ref)

        dn, dg = _rms_bwd(dh_ref[...], x_ref[...], g_ref[...])
        dx_ref[...] = dx1_ref[...] + dn
        dg_ref[...] += jnp.sum(dg, axis=0, keepdims=True)

    row = pl.BlockSpec((tm, D_MODEL), lambda i: (i, 0))
    full = pl.BlockSpec((1, D_MODEL), lambda i: (0, 0))
    return pl.pallas_call(
        body, name=name, grid=(s // tm,),
        in_specs=[row, row, full, row], out_specs=[row, full],
        out_shape=[jax.ShapeDtypeStruct((s, D_MODEL), F32), jax.ShapeDtypeStruct((1, D_MODEL), F32)],
        compiler_params=_params(("arbitrary",)),
    )(dh1, x, g_pre, dx1)


def _local_step(x, target, fw, rep, mixer_weights, early_grads, late_grads):
    s = x.shape[0]
    tabs = _rope_tables(s)
    w_in, w_uq, w_ukv, w_o = (fw[n] for n in ("w_in", "w_uq", "w_ukv", "w_o"))
    tr = min(2048, s)
    tcon = min(2048, s)

    proj, h1, qkv4, qkv16 = _in_proj(x, rep["norm_mix_pre"], w_in, "in_proj")
    qkv = {1: proj, 4: qkv4, 16: qkv16}
    q_of = lambda r: (qkv[r], lambda c: 3 * c)
    k_of = lambda r: (qkv[r], lambda c: 3 * c + 1)
    v_of = lambda r: (qkv[r], lambda c: 3 * c + 2)
    own = lambda a: (a, lambda c: c)
    biases = [_band_bias(r) for _, r in DIL_CONFIGS]
    os_, lses = [], []
    for g, (_, r) in enumerate(DIL_CONFIGS):
        o, l = _band_call("fwd", r, [q_of(r)], [k_of(r), v_of(r)], biases[g], f"band_fwd_r{r}")
        os_.append(o)
        lses.append(l)
    ya, lse_a, lse4, lse16 = _band_combine(os_, lses, "band_combine")
    qcat, kcat, kvb, cqn, ckvn = _mla_prep(proj, rep["q_lat_norm"], rep["kv_lat_norm"], w_uq, w_ukv, tabs, "mla_prep")
    yb, qaug = _mla_fwd(qcat, kcat, kvb, "mla_fwd")
    yn, y2, x1 = _mix_out(ya, yb, rep["out_norm_a"], rep["out_norm_b"], w_o, rep["norm_mix_post"], x, "mix_out")
    mw = mixer_weights(x1)
    w_up, w_down, cw, cb = mw["w_up"], mw["w_down"], mw["conv_w"], mw["conv_b"]
    ff = w_down.shape[0]
    up, h2 = _norm_matmul(x1, rep["norm_ffn_pre"], w_up, "up_proj")
    act = _conv_gate(up, cw, cb, "conv_gate")
    dy3, dx2, loss_acc, dg_ffn_post = _ffn_out(act, w_down, rep["norm_ffn_post"], x1, target, "ffn_out")

    grads = {"norm_ffn_post": dg_ffn_post}
    dact = _matmul(dy3, w_down, "nt", BF16, tr, ff // 2, D_MODEL, "d_act")
    grads["w_down"] = _matmul(act, dy3, "tn", BF16, ff // 2, D_MODEL, tcon, "dw_down")
    dup, grads["conv_w"], grads["conv_b"] = _conv_gate_bwd(up, dact, cw, cb, "conv_gate_bwd")
    half = N_DEV // 2
    dh2 = _matmul_core(
        dup, w_up, NN, (s // tr, 1, N_DEV),
        pl.BlockSpec((None, tr, FF_SLAB), lambda i, j, t: (t // half, i, t % half)),
        pl.BlockSpec((None, FF_SLAB, D_MODEL), lambda i, j, t: (t, 0, 0)),
        pl.BlockSpec((tr, D_MODEL), lambda i, j, t: (i, 0)),
        jax.ShapeDtypeStruct((s, D_MODEL), F32), (tr, D_MODEL), "d_h2")
    grads["w_up"] = _matmul_core(
        dup, h2, TN, (1, N_DEV, s // tcon),
        pl.BlockSpec((None, tcon, FF_SLAB), lambda i, j, t: (j // half, t, j % half)),
        pl.BlockSpec((tcon, D_MODEL), lambda i, j, t: (t, 0)),
        pl.BlockSpec((None, FF_SLAB, D_MODEL), lambda i, j, t: (j, 0, 0)),
        jax.ShapeDtypeStruct((N_DEV, FF_SLAB, D_MODEL), BF16), (FF_SLAB, D_MODEL), "dw_up")
    dx1, dy2, grads["norm_ffn_pre"], grads["norm_mix_post"] = _resnorm_bwd(
        dh2, x1, rep["norm_ffn_pre"], dx2, y2, rep["norm_mix_post"], "resnorm_bwd")
    dyn = _matmul(dy2, w_o, "nt", F32, tr, D_MODEL, D_MODEL, "d_yn")
    grads["w_o"] = _matmul(yn, dy2, "tn", BF16, D_MODEL, D_MODEL, tcon, "dw_o")
    token = early_grads(grads)
    dya, delta_a, doaug, grads["out_norm_a"], grads["out_norm_b"], dya4, dya16, delta4, delta16 = _outnorm_bwd(
        dyn, ya, yb, rep["out_norm_a"] + token, rep["out_norm_b"], _head_ones(), "outnorm_bwd")
    stats = {1: (dya, lse_a, delta_a), 4: (dya4, lse4, delta4), 16: (dya16, lse16, delta16)}
    parts = []
    for g, (_, r) in enumerate(DIL_CONFIGS):
        qside = [q_of(r)] + [own(a) for a in stats[r]]
        kside = [k_of(r), v_of(r)]
        (dq,) = _band_call("dq", r, qside, kside, biases[g], f"band_dq_r{r}")
        dk, dv = _band_call("dkv", r, kside, qside, biases[g], f"band_dkv_r{r}")
        parts.append((dq, dk, dv))
    dproj_a = _sum_cast(parts, "band_grad_sum")
    dqc, dkc, dvp = _mla_bwd(qaug, kcat, kvb, doaug, "mla_bwd")
    dproj, grads["w_uq"], grads["w_ukv"], grads["q_lat_norm"], grads["kv_lat_norm"] = _mla_prep_bwd(
        dqc, dkc, dvp, proj, cqn, ckvn, rep["q_lat_norm"], rep["kv_lat_norm"], w_uq, w_ukv, tabs, dproj_a, "mla_prep_bwd")
    grads["w_in"] = _matmul(dproj, h1, "tn", BF16, N_PROJ // 2, D_MODEL, tcon, "dw_in")
    token = late_grads(grads)
    dh1 = _matmul(dproj, w_in, "nn", F32, tr, D_MODEL, N_PROJ // 2, "d_h1")
    grad_x, grads["norm_mix_pre"] = _final_bwd(dh1, x, rep["norm_mix_pre"] + token, dx1, "final_bwd")
    loss = 0.5 / D_MODEL * loss_acc[0, 0]
    return loss, grad_x, grads


MESH = pl.DeviceIdType.MESH
HBM_SPEC = pl.BlockSpec(memory_space=pltpu.HBM)
SMALL_ROWS = 96
TRANSPOSED = ("w_in", "w_up", "w_uq")
BUF_SHAPES = {"w_up": (FF_SLAB, D_MODEL), "w_in": (D_IN // N_DEV, D_MODEL), "w_down": (D_FF // N_DEV, D_MODEL),
              "w_o": (D_MODEL // N_DEV, D_MODEL), "w_uq": (QK_NOPE + QK_ROPE, Q_LORA), "w_ukv": (KV_LORA, 128),
              "conv_w": (8, FF_SLAB)}
BUF_ORDER = tuple(BUF_SHAPES)
MIXING = ("w_in", "w_o", "w_uq", "w_ukv")
MIXER = ("w_up", "w_down", "conv_w")
EARLY_GRADS = ("w_up", "w_down", "w_o", "conv_w")
LATE_GRADS = ("w_in", "w_uq", "w_ukv")


def _all_gather(bufs, name):
    nb = len(bufs)

    def body(*refs):
        x_refs, out_refs = refs[:nb], refs[nb:2 * nb]
        send_sems, recv_sems, local_sems = refs[2 * nb:]
        x, y, c = lax.axis_index("x"), lax.axis_index("y"), lax.axis_index("c")
        me, sibling = (x, y, c), (x, y, 1 - c)
        chips = [(1 - x, y), (x, 1 - y), (1 - x, 1 - y)]

        def copy(b, k, block, to, own=False):
            px, py, pc = block
            slot = out_refs[b].at[4 * px + 2 * py + pc]
            return pltpu.make_async_remote_copy(
                src_ref=x_refs[b] if own else slot, dst_ref=slot,
                send_sem=send_sems.at[7 * b + k], recv_sem=recv_sems.at[7 * b + k], device_id=to, device_id_type=MESH)

        mine = [pltpu.make_async_copy(x_refs[b], out_refs[b].at[4 * x + 2 * y + c], local_sems.at[b]) for b in range(nb)]
        sends = []
        for b in range(nb):
            mine[b].start()
            first = [copy(b, 0, me, sibling, own=True)]
            first += [copy(b, 1 + j, me, (*chip, c), own=True) for j, chip in enumerate(chips)]
            for cp in first:
                cp.start()
            sends += first
        for j, chip in enumerate(chips):
            for b in range(nb):
                copy(b, 1 + j, (*chip, c), me).wait_recv()
                passed = copy(b, 4 + j, (*chip, c), sibling)
                passed.start()
                sends.append(passed)
        for b in range(nb):
            copy(b, 0, sibling, me).wait_recv()
            for j, chip in enumerate(chips):
                copy(b, 4 + j, (*chip, 1 - c), me).wait_recv()
        for cp in sends:
            cp.wait_send()
        for cp in mine:
            cp.wait()

    return pl.pallas_call(
        body, name=name,
        out_shape=[jax.ShapeDtypeStruct((N_DEV,) + p.shape, p.dtype) for p in bufs],
        in_specs=[HBM_SPEC] * nb, out_specs=[HBM_SPEC] * nb,
        scratch_shapes=[pltpu.SemaphoreType.DMA((7 * nb,)), pltpu.SemaphoreType.DMA((7 * nb,)),
                        pltpu.SemaphoreType.DMA((nb,))],
    )(*bufs)


def _grad_exchange(bigs, small, name):
    flips = [(fx, fy, fc) for fx in (0, 1) for fy in (0, 1) for fc in (0, 1)][1:]
    nb = len(bigs)

    def body(*refs):
        big_refs, small_ref = refs[:nb], refs[nb]
        rbig_refs, rsmall_ref = refs[nb + 1:2 * nb + 1], refs[2 * nb + 1]
        send_sems, recv_sems, local_sems = refs[2 * nb + 2:]
        x, y, c = lax.axis_index("x"), lax.axis_index("y"), lax.axis_index("c")
        my = 4 * x + 2 * y + c
        own = [pltpu.make_async_copy(big_refs[b].at[my], rbig_refs[b].at[my], local_sems.at[b]) for b in range(nb)]
        own.append(pltpu.make_async_copy(small_ref, rsmall_ref.at[my], local_sems.at[nb]))
        for cp in own:
            cp.start()
        copies = []
        for b in range(nb + 1):
            for k, (fx, fy, fc) in enumerate(flips):
                px = 1 - x if fx else x
                py = 1 - y if fy else y
                pc = 1 - c if fc else c
                src = small_ref if b == nb else big_refs[b].at[4 * px + 2 * py + pc]
                dst = rsmall_ref.at[my] if b == nb else rbig_refs[b].at[my]
                copies.append(pltpu.make_async_remote_copy(
                    src_ref=src, dst_ref=dst, send_sem=send_sems.at[7 * b + k], recv_sem=recv_sems.at[7 * b + k],
                    device_id=(px, py, pc), device_id_type=MESH))
        for cp in copies:
            cp.start()
        for cp in copies:
            cp.wait()
        for cp in own:
            cp.wait()

    nsem = 7 * (nb + 1)
    return pl.pallas_call(
        body, name=name,
        out_shape=[jax.ShapeDtypeStruct(b.shape, b.dtype) for b in bigs]
        + [jax.ShapeDtypeStruct((N_DEV,) + small.shape, small.dtype)],
        in_specs=[HBM_SPEC] * (nb + 1), out_specs=[HBM_SPEC] * (nb + 1),
        scratch_shapes=[pltpu.SemaphoreType.DMA((nsem,)), pltpu.SemaphoreType.DMA((nsem,)),
                        pltpu.SemaphoreType.DMA((nb + 1,))],
    )(*bigs, small)


SEM_SPEC = pl.BlockSpec(memory_space=pltpu.SEMAPHORE)
ANY_SPEC = pl.BlockSpec(memory_space=pl.ANY)
FLIPS = tuple((fx, fy, fc) for fx in (0, 1) for fy in (0, 1) for fc in (0, 1))[1:]


def _split_copies(src_refs, land_refs, send_sems, recv_sems, scatter):
    x, y, c = lax.axis_index("x"), lax.axis_index("y"), lax.axis_index("c")
    my = 4 * x + 2 * y + c
    copies = []
    for b, (src, land) in enumerate(zip(src_refs, land_refs)):
        for k, (fx, fy, fc) in enumerate(FLIPS):
            px = 1 - x if fx else x
            py = 1 - y if fy else y
            pc = 1 - c if fc else c
            copies.append(pltpu.make_async_remote_copy(
                src_ref=src.at[4 * px + 2 * py + pc] if scatter else src, dst_ref=land.at[my],
                send_sem=send_sems.at[7 * b + k], recv_sem=recv_sems.at[7 * b + k],
                device_id=(px, py, pc), device_id_type=MESH))
    return copies


def _exchange_start(srcs, scatter, name):
    nb = len(srcs)
    lands = [lax.empty(s.shape if scatter else (N_DEV,) + s.shape, s.dtype) for s in srcs]

    def body(*refs):
        src_refs, land_refs = refs[:nb], refs[nb:2 * nb]
        send_sems, recv_sems = refs[2 * nb], refs[2 * nb + 1]
        token = refs[-1]
        for cp in _split_copies(src_refs, land_refs, send_sems, recv_sems, scatter):
            cp.start()
        token[...] = jnp.zeros_like(token)

    hbm = lambda a: pltpu.HBM(a.shape, a.dtype)
    outs = pl.pallas_call(
        body, name=name,
        out_shape=(pltpu.SemaphoreType.DMA((7 * nb,)), pltpu.SemaphoreType.DMA((7 * nb,)),
                   *[hbm(a) for a in srcs], *[hbm(a) for a in lands], jax.ShapeDtypeStruct((8, 128), F32)),
        in_specs=[HBM_SPEC] * (2 * nb),
        out_specs=(SEM_SPEC, SEM_SPEC, *[HBM_SPEC] * (2 * nb), pl.BlockSpec(memory_space=pltpu.VMEM)),
        input_output_aliases={i: 2 + i for i in range(2 * nb)},
        compiler_params=pltpu.CompilerParams(has_side_effects=pltpu.SideEffectType.DATAFLOW_SIDE_EFFECTING),
    )(*[pltpu.with_memory_space_constraint(a, pltpu.HBM) for a in srcs],
      *[pltpu.with_memory_space_constraint(a, pltpu.HBM) for a in lands])
    return outs[0], outs[1], list(outs[2:2 + nb]), list(outs[2 + nb:2 + 2 * nb]), outs[-1]


def _exchange_wait(started, scatter, after, name):
    send_sems, recv_sems, srcs, lands, _ = started
    nb = len(srcs)

    def body(*refs):
        src_refs, land_refs = refs[:nb], refs[nb:2 * nb]
        for cp in _split_copies(src_refs, land_refs, refs[2 * nb], refs[2 * nb + 1], scatter):
            cp.wait_send()
            cp.wait_recv()

    hbm = lambda a: pltpu.HBM(a.shape, a.dtype)
    outs = pl.pallas_call(
        body, name=name,
        out_shape=(*[hbm(a) for a in srcs], *[hbm(a) for a in lands]),
        in_specs=[HBM_SPEC] * (2 * nb) + [SEM_SPEC, SEM_SPEC, ANY_SPEC],
        out_specs=tuple([HBM_SPEC] * (2 * nb)),
        input_output_aliases={i: i for i in range(2 * nb)},
        compiler_params=pltpu.CompilerParams(has_side_effects=pltpu.SideEffectType.DATAFLOW_SIDE_EFFECTING),
    )(*srcs, *lands, send_sems, recv_sems, after)
    return list(outs[:nb]), list(outs[nb:])


def _own_slot(land, own):
    my = 4 * lax.axis_index("x") + 2 * lax.axis_index("y") + lax.axis_index("c")
    return lax.dynamic_update_slice(land, own[None], (my,) + (0,) * own.ndim)


def _adamw(parts, w, m, v, name):
    rows, n = w.shape
    tm = rows if rows <= 384 else next(t for t in (256, 176) if rows % t == 0)
    assert rows % tm == 0

    def body(p_ref, w_ref, m_ref, v_ref, g_ref, d_ref, m2_ref, v2_ref):
        g = p_ref[0, :, 0:n].astype(F32)
        for s in range(1, N_DEV):
            g = g + p_ref[s, :, 0:n].astype(F32)
        g_ref[...] = g
        m2 = ADAM_B1 * m_ref[...] + (1.0 - ADAM_B1) * g
        v2 = ADAM_B2 * v_ref[...] + (1.0 - ADAM_B2) * jnp.square(g)
        m2_ref[...] = m2
        v2_ref[...] = v2
        m_hat = m2 / (1.0 - ADAM_B1 ** ADAM_STEP)
        v_hat = v2 / (1.0 - ADAM_B2 ** ADAM_STEP)
        d_ref[...] = -ADAM_LR * (m_hat / (jnp.sqrt(v_hat) + ADAM_EPS) + ADAM_WD * w_ref[...])

    row = pl.BlockSpec((tm, n), lambda i: (i, 0))
    return pl.pallas_call(
        body, name=name, grid=(rows // tm,),
        in_specs=[pl.BlockSpec((N_DEV, tm, parts.shape[2]), lambda i: (0, i, 0)), row, row, row],
        out_specs=[row] * 4,
        out_shape=[jax.ShapeDtypeStruct((rows, n), F32)] * 4,
        compiler_params=_params(("parallel",)),
    )(parts, w, m, v)


def _pack(flat_parts, rows):
    flat = jnp.concatenate(flat_parts, axis=-1)
    pad = rows * LANES - flat.shape[-1]
    flat = jnp.pad(flat, [(0, 0)] * (flat.ndim - 1) + [(0, pad)])
    return flat.reshape(flat.shape[:-1] + (rows, LANES))


def _unpack(packed, shapes):
    flat = packed.reshape(packed.shape[:-2] + (-1,))
    out, off = {}, 0
    for name, shape in shapes.items():
        n = int(np.prod(shape))
        out[name] = flat[..., off:off + n].reshape(flat.shape[:-1] + tuple(shape))
        off += n
    return out


def _pad_to(a, shape):
    return jnp.pad(a, [(0, t - d) for d, t in zip(a.shape, shape)])


def _pad_w_in(w):
    k = w.shape[1]
    z = lambda n: jnp.zeros((n, k), w.dtype)
    return jnp.concatenate([w[:COL_KR], z(64), w[COL_KR:], z(32)], axis=0)


def _unpad_w_in(w):
    return jnp.concatenate([w[:COL_KR], w[COL_KR + 64:COL_KR + 96]], axis=0)


def _assemble_weights(g, conv_b):
    half = N_DEV // 2
    cols = lambda a: a.transpose(1, 0, 2).reshape(a.shape[1], N_DEV * a.shape[2])
    make = {
        "w_in": lambda: _pad_w_in(g["w_in"].reshape(D_IN, D_MODEL)),
        "w_uq": lambda: _pad_to(g["w_uq"], (N_DEV, 128, Q_LORA)).reshape(N_DEV * 128, Q_LORA),
        "w_ukv": lambda: cols(g["w_ukv"]),
        "w_o": lambda: g["w_o"].reshape(D_MODEL, D_MODEL),
        "w_up": lambda: g["w_up"],
        "w_down": lambda: _pad_to(g["w_down"].reshape(half, FF_SHARD, D_MODEL),
                                  (half, FF_SLAB, D_MODEL)).reshape(half * FF_SLAB, D_MODEL),
        "conv_w": lambda: g["conv_w"][:, :3].reshape(2, half, 3, FF_SLAB).transpose(0, 2, 1, 3).reshape(2, 3, half * FF_SLAB),
    }
    fw = {n: make[n]() for n in g}
    if conv_b is not None:
        fw["conv_b"] = _pad_to(conv_b.reshape(2, 1, half, FF_SHARD), (2, 1, half, FF_SLAB)).reshape(2, 1, half * FF_SLAB)
    return fw


def _grad_bufs(grads, names):
    half = N_DEV // 2
    slabs = lambda a: a.reshape(a.shape[0], N_DEV, a.shape[1] // N_DEV).transpose(1, 0, 2)
    make = {
        "w_in": lambda: _unpad_w_in(grads["w_in"]).reshape((N_DEV,) + BUF_SHAPES["w_in"]),
        "w_uq": lambda: grads["w_uq"].reshape(N_DEV, 128, Q_LORA)[:, :QK_NOPE + QK_ROPE],
        "w_ukv": lambda: slabs(grads["w_ukv"]),
        "w_o": lambda: grads["w_o"].reshape((N_DEV,) + BUF_SHAPES["w_o"]),
        "w_up": lambda: grads["w_up"],
        "w_down": lambda: grads["w_down"].reshape(half, FF_SLAB, D_MODEL)[:, :FF_SHARD].reshape((N_DEV,) + BUF_SHAPES["w_down"]),
        "conv_w": lambda: _pad_to(grads["conv_w"].reshape(2, 3, half, FF_SLAB).transpose(0, 2, 1, 3).reshape(N_DEV, 3, FF_SLAB),
                                  (N_DEV,) + BUF_SHAPES["conv_w"]),
    }
    return [make[n]() if n == "conv_w" else make[n]().astype(BF16) for n in names]


def kernel(x, norm_mix_pre, w_in, q_lat_norm, w_uq, kv_lat_norm, w_ukv, out_norm_a, out_norm_b, w_o, norm_mix_post, norm_ffn_pre, w_up, conv_w, conv_b, w_down, norm_ffn_post, loss_target, m_norm_mix_pre, m_w_in, m_q_lat_norm, m_w_uq, m_kv_lat_norm, m_w_ukv, m_out_norm_a, m_out_norm_b, m_w_o, m_norm_mix_post, m_norm_ffn_pre, m_w_up, m_conv_w, m_conv_b, m_w_down, m_norm_ffn_post, v_norm_mix_pre, v_w_in, v_q_lat_norm, v_w_uq, v_kv_lat_norm, v_w_ukv, v_out_norm_a, v_out_norm_b, v_w_o, v_norm_mix_post, v_norm_ffn_pre, v_w_up, v_conv_w, v_conv_b, v_w_down, v_norm_ffn_post):
    given = dict(locals())
    shard = lambda a, n: a[0].T if n in TRANSPOSED else a[0]
    w = {n: shard(given[n], n) for n in WEIGHTS}
    m = {n: shard(given["m_" + n], n) for n in WEIGHTS}
    v = {n: shard(given["v_" + n], n) for n in WEIGHTS}
    rep_shapes = {n: w[n].shape for n in REPLICATED}

    buf = lambda n: _pad_to(w[n] if n == "conv_w" else w[n].astype(BF16), BUF_SHAPES[n])
    first = dict(zip(MIXING, _all_gather([buf(n) for n in MIXING], "weight_all_gather")))
    fw = _assemble_weights(first, None)
    tie = first["w_o"][0, 0, 0].astype(F32) * 0.0
    late_bufs = [buf(n) + tie.astype(w[n].dtype if n == "conv_w" else BF16) for n in MIXER]
    mixer_started = _exchange_start(late_bufs, False, "mixer_weights_start")
    rep = {n: given[n] for n in REPLICATED}
    rep["norm_mix_pre"] = rep["norm_mix_pre"] + mixer_started[4][0, 0]

    def mixer_weights(after):
        srcs, lands = _exchange_wait(mixer_started, False, after, "mixer_weights_wait")
        got = {n: _own_slot(land, own) for n, land, own in zip(MIXER, lands, srcs)}
        return _assemble_weights(got, conv_b)

    early = {}

    def early_grads(grads):
        early["started"] = _exchange_start(_grad_bufs(grads, EARLY_GRADS), True, "early_grads_start")
        return early["started"][4][0, 0]

    def late_grads(grads):
        early["late"] = _exchange_start(_grad_bufs(grads, LATE_GRADS), True, "late_grads_start")
        return early["late"][4][0, 0]

    loss_local, grad_x, grads = _local_step(x[0], loss_target[0], fw, rep, mixer_weights, early_grads, late_grads)

    grads["conv_b"] = grads["conv_b"].reshape(N_DEV, FF_SLAB)[:, :FF_SHARD]
    small = _pack([grads[n].reshape(-1) for n in REPLICATED] + [loss_local.reshape(1)], SMALL_ROWS)
    received_small = _grad_exchange([], small, "grad_exchange")[0]
    my = 4 * lax.axis_index("x") + 2 * lax.axis_index("y") + lax.axis_index("c")
    received = {}
    for names, key, tag in ((EARLY_GRADS, "started", "early_grads_wait"), (LATE_GRADS, "late", "late_grads_wait")):
        srcs, lands = _exchange_wait(early[key], True, received_small, tag)
        for n, land, src in zip(names, lands, srcs):
            received[n] = _own_slot(land, lax.dynamic_index_in_dim(src, my, 0, keepdims=False))
    results = [{}, {}, {}, {}]
    for n in BUF_ORDER:
        parts = received[n]
        if n == "conv_w":
            args = [_pad_to(t[n], BUF_SHAPES[n]) for t in (w, m, v)]
        else:
            args = [w[n], m[n], v[n]]
        outs = _adamw(parts, *args, f"adamw_{n}")
        for t in range(4):
            results[t][n] = outs[t][:w[n].shape[0], :w[n].shape[1]] if n == "conv_w" else outs[t]
    pk = lambda d: _pack([d[n].reshape(-1) for n in REPLICATED] + [jnp.zeros((1,), F32)], SMALL_ROWS)
    small_out = _adamw(received_small, pk(w), pk(m), pk(v), "adamw_replicated")
    rep_shapes["loss"] = (1,)
    for t in range(4):
        results[t].update(_unpack(small_out[t], rep_shapes))

    loss = results[0]["loss"][0]
    outs = [loss, grad_x[None]]
    for res in results:
        outs += [(res[n].T if n in TRANSPOSED else res[n])[None] for n in WEIGHTS]
    return tuple(outs)
```

```python
import math

import numpy as np
import jax
import jax.numpy as jnp
from jax import lax
from jax.experimental import pallas as pl
from jax.experimental.pallas import tpu as pltpu

F32 = jnp.float32
BF16 = jnp.bfloat16

D_MODEL = 1024
N_DEV = 8
WIDTH_A = 512
N_HEADS = 8
Q_LORA = 384
KV_LORA = 256
QK_ROPE = 32
QK_NOPE = 64
D_FF = 2816
FF_SHARD = 2 * D_FF // N_DEV
FF_SLAB = 768
DIL_CONFIGS = ((128, 1), (512, 4), (2048, 16))
BAND_HALF = 64
ROPE_BASE = 10000.0
EPS = 1e-6
NEG = -1e30
MLA_SCALE = (QK_NOPE + QK_ROPE) ** -0.5
SCALE_A = 0.125
LOG2E = 1.0 / math.log(2.0)
LN2 = math.log(2.0)
QSCALE_A = SCALE_A * LOG2E

COL_CQ = 3 * WIDTH_A
COL_CKV = COL_CQ + Q_LORA
COL_KR = COL_CKV + KV_LORA
N_PROJ = COL_KR + 128
N_LAT = N_PROJ - COL_CQ
D_IN = COL_KR + QK_ROPE

ADAM_LR = 0.001
ADAM_B1 = 0.9
ADAM_B2 = 0.999
ADAM_EPS = 1e-08
ADAM_WD = 0.01
ADAM_STEP = 10

LANES = 128
VMEM_LIMIT = 56 * 1024 * 1024

REPLICATED = ("norm_mix_pre", "q_lat_norm", "kv_lat_norm", "out_norm_a", "out_norm_b", "norm_mix_post",
              "norm_ffn_pre", "conv_b", "norm_ffn_post")
WEIGHTS = ("norm_mix_pre", "w_in", "q_lat_norm", "w_uq", "kv_lat_norm", "w_ukv", "out_norm_a", "out_norm_b", "w_o",
           "norm_mix_post", "norm_ffn_pre", "w_up", "conv_w", "conv_b", "w_down", "norm_ffn_post")


def _params(sem=None):
    return pltpu.CompilerParams(dimension_semantics=sem, vmem_limit_bytes=VMEM_LIMIT)


def _dot(a, b, dims):
    return lax.dot_general(a, b, (dims, ((), ())), preferred_element_type=F32)


NN = ((1,), (0,))
NT = ((1,), (1,))
TN = ((0,), (0,))


def _rstd(x):
    return lax.rsqrt(jnp.mean(x * x, axis=-1, keepdims=True) + EPS)


def _rms_bwd(dy, x, g):
    r = _rstd(x)
    z = x * r
    gy = dy * g
    dx = r * (gy - z * jnp.mean(gy * z, axis=-1, keepdims=True))
    return dx, dy * z


def _split_hi_lo(v):
    hi = v.astype(BF16)
    lo = (v - hi.astype(F32)).astype(BF16)
    return hi, lo


def _matmul(a, b, mode, out_dtype, tm, tn, tk, name):
    if mode == "nn":
        (m, k), n = a.shape, b.shape[1]
        a_spec = pl.BlockSpec((tm, tk), lambda i, j, s: (i, s))
        b_spec = pl.BlockSpec((tk, tn), lambda i, j, s: (s, j))
        dims = NN
    elif mode == "nt":
        (m, k), n = a.shape, b.shape[0]
        a_spec = pl.BlockSpec((tm, tk), lambda i, j, s: (i, s))
        b_spec = pl.BlockSpec((tn, tk), lambda i, j, s: (j, s))
        dims = NT
    else:
        (k, m), n = a.shape, b.shape[1]
        a_spec = pl.BlockSpec((tk, tm), lambda i, j, s: (s, i))
        b_spec = pl.BlockSpec((tk, tn), lambda i, j, s: (s, j))
        dims = TN
    assert m % tm == 0 and n % tn == 0 and k % tk == 0, (name, m, n, k, tm, tn, tk)
    return _matmul_core(a, b, dims, (m // tm, n // tn, k // tk), a_spec, b_spec,
                        pl.BlockSpec((tm, tn), lambda i, j, s: (i, j)), jax.ShapeDtypeStruct((m, n), out_dtype),
                        (tm, tn), name)


def _matmul_core(a, b, dims, grid, a_spec, b_spec, o_spec, out_sds, acc_shape, name):
    nk = grid[2]

    def body(a_ref, b_ref, o_ref, acc_ref):
        s = pl.program_id(2)

        @pl.when(s == 0)
        def _():
            acc_ref[...] = jnp.zeros_like(acc_ref)

        acc_ref[...] += _dot(a_ref[...].astype(BF16), b_ref[...].astype(BF16), dims)

        @pl.when(s == nk - 1)
        def _():
            o_ref[...] = acc_ref[...].astype(out_sds.dtype)

    return pl.pallas_call(
        body, name=name, grid=grid, in_specs=[a_spec, b_spec], out_specs=o_spec, out_shape=out_sds,
        scratch_shapes=[pltpu.VMEM(acc_shape, F32)],
        compiler_params=_params(("parallel", "parallel", "arbitrary")),
    )(a, b)


def _norm_matmul(x, g, w, name):
    s, k = x.shape
    tm = min(2048, s)
    nj, tn, _ = w.shape
    half = nj // 2
    w_spec = pl.BlockSpec((None, tn, k), lambda i, j: (j, 0, 0))
    o_spec = pl.BlockSpec((None, tm, tn), lambda i, j: (j // half, i, j % half))
    o_sds = jax.ShapeDtypeStruct((2, s, half * tn), BF16)

    def body(x_ref, g_ref, w_ref, o_ref, h_ref):
        @pl.when(pl.program_id(1) == 0)
        def _():
            xv = x_ref[...]
            h_ref[...] = (xv * _rstd(xv) * g_ref[...]).astype(BF16)

        o_ref[...] = _dot(h_ref[...], w_ref[...], NT).astype(BF16)

    return pl.pallas_call(
        body, name=name, grid=(s // tm, nj),
        in_specs=[pl.BlockSpec((tm, k), lambda i, j: (i, 0)),
                  pl.BlockSpec((1, k), lambda i, j: (0, 0)),
                  w_spec],
        out_specs=[o_spec, pl.BlockSpec((tm, k), lambda i, j: (i, 0))],
        out_shape=[o_sds, jax.ShapeDtypeStruct((s, k), BF16)],
        compiler_params=_params(("parallel", "arbitrary")),
    )(x, g, w)


def _band_bias(r):
    off = np.arange(256)[None, :] - BAND_HALF - np.arange(128)[:, None]
    slopes = np.exp2(-8.0 * np.arange(1, N_HEADS + 1, dtype=np.float32) / N_HEADS).astype(np.float32)
    dist = (np.abs(off) * r).astype(np.float32)
    bias = -slopes[:, None, None] * dist[None]
    bias = np.where((np.abs(off) <= BAND_HALF)[None], bias * np.float32(LOG2E), np.float32(NEG))
    return jnp.asarray(bias, F32)


def _band_call(mode, r, center, window, bias, name):
    seq = center[0][0].shape[0]
    tq = min(512, seq)
    nsub = tq // 128
    hb = tq // BAND_HALF
    nh = seq // BAND_HALF
    nc, nw = len(center), len(window)
    out_dtypes = {"fwd": (BF16, F32), "dq": (BF16,), "dkv": (BF16, BF16)}[mode]
    n_out = len(out_dtypes)

    def specs(col):
        return (pl.BlockSpec((BAND_HALF, WIDTH_A), lambda c, i: (jnp.maximum(i * hb - 1, 0), col(c))),
                pl.BlockSpec((tq, WIDTH_A), lambda c, i: (i, col(c))),
                pl.BlockSpec((BAND_HALF, WIDTH_A), lambda c, i: (jnp.minimum((i + 1) * hb, nh - 1), col(c))))

    cspec = pl.BlockSpec((tq, WIDTH_A), lambda c, i: (i, c))
    in_specs = [specs(col)[1] for _, col in center]
    operands = [a for a, _ in center]
    for a, col in window:
        in_specs += list(specs(col))
        operands += [a, a, a]
    in_specs.append(pl.BlockSpec((N_HEADS, 128, 256), lambda c, i: (0, 0, 0)))
    operands.append(bias)
    window = [a for a, _ in window]

    def aug_stat(base, stat_sw, lane, act, e0):
        hi, lo = _split_hi_lo(stat_sw)
        return jnp.where(act, base, jnp.where(lane == e0, -hi, jnp.where(lane == e0 + 1, -lo, jnp.zeros_like(hi))))

    def aug_ones(base, lane, e0):
        return jnp.where((lane == e0) | (lane == e0 + 1), jnp.ones_like(base), base)

    def body(*refs):
        c_refs = refs[:nc]
        w_refs = refs[nc:nc + 3 * nw]
        bias_ref = refs[nc + 3 * nw]
        o_refs = refs[nc + 3 * nw + 1:nc + 3 * nw + 1 + n_out]
        wins = refs[nc + 3 * nw + 1 + n_out:]
        i = pl.program_id(1)
        for t in range(nw):
            wins[t][0:BAND_HALF, :] = w_refs[3 * t][...]
            wins[t][BAND_HALF:BAND_HALF + tq, :] = w_refs[3 * t + 1][...]
            wins[t][BAND_HALF + tq:BAND_HALF + tq + BAND_HALF, :] = w_refs[3 * t + 2][...]

        def sub(j, carry):
            r0 = pl.multiple_of(j * 128, 128)
            wpos = i * tq + j * 128 - BAND_HALF + lax.broadcasted_iota(jnp.int32, (128, 256), 1)
            valid = (wpos >= 0) & (wpos < seq)
            lane_c = lax.broadcasted_iota(jnp.int32, (128, 128), 1)
            lane_w = lax.broadcasted_iota(jnp.int32, (256, 128), 1)
            heads = [(p, a) for p in range(4) for a in range(2)]
            first, last_ops = [], []
            for p, a in heads:
                cols = slice(p * 128, (p + 1) * 128)
                cs = [c[pl.ds(r0, 128), cols] for c in c_refs]
                ws = [w[pl.ds(r0, 256), cols] for w in wins]
                e0 = 64 if a == 0 else 0
                act_c = (lane_c < 64) if a == 0 else (lane_c >= 64)
                act_w = (lane_w < 64) if a == 0 else (lane_w >= 64)
                bias_a = bias_ref[2 * p + a]
                if mode == "fwd":
                    qa = jnp.where(act_c, cs[0] * QSCALE_A, jnp.zeros_like(cs[0]))
                    first.append((_dot(qa, ws[0], NT) + bias_a, None))
                    last_ops.append((ws[1],))
                elif mode == "dq":
                    q2, dy2, l2, d2 = cs
                    k2, v2 = ws
                    q_aug = aug_stat(q2 * QSCALE_A, pltpu.roll(l2, 64, 1), lane_c, act_c, e0)
                    dy_aug = aug_stat(dy2, pltpu.roll(d2, 64, 1), lane_c, act_c, e0)
                    first.append((_dot(q_aug, aug_ones(k2, lane_w, e0), NT) + bias_a,
                                  _dot(dy_aug, aug_ones(v2, lane_w, e0), NT)))
                    last_ops.append((k2,))
                else:
                    k2, v2 = cs
                    q2, dy2, l2, d2 = ws
                    q_aug = aug_stat(q2 * QSCALE_A, pltpu.roll(l2, 64, 1), lane_w, act_w, e0)
                    dy_aug = aug_stat(dy2, pltpu.roll(d2, 64, 1), lane_w, act_w, e0)
                    first.append((_dot(aug_ones(k2, lane_c, e0), q_aug, NT) + bias_a,
                                  _dot(aug_ones(v2, lane_c, e0), dy_aug, NT)))
                    last_ops.append((q_aug, dy_aug))
            mid = []
            for sc, dp in first:
                sc = jnp.where(valid, sc, NEG)
                if mode == "fwd":
                    m = jnp.max(sc, axis=-1, keepdims=True)
                    e = jnp.exp2(sc - m)
                    l = jnp.sum(e, axis=-1, keepdims=True)
                    mid.append((e.astype(BF16), l, m + jnp.log(l) * LOG2E))
                else:
                    pr = jnp.exp2(sc)
                    mid.append((pr.astype(BF16), (pr * dp).astype(BF16)))
            res = []
            for md, ops in zip(mid, last_ops):
                if mode == "fwd":
                    res.append((_dot(md[0], ops[0], NN) / md[1], jnp.broadcast_to(md[2], (128, 128))))
                elif mode == "dq":
                    res.append((_dot(md[1], ops[0], NN) * SCALE_A,))
                else:
                    res.append((_dot(md[1], ops[0], NN) * LN2, _dot(md[0], ops[1], NN)))
            for t in range(n_out):
                pairs = [jnp.where(lane_c < 64, res[2 * p][t], res[2 * p + 1][t]) for p in range(4)]
                o_refs[t][pl.ds(r0, 128), :] = jnp.concatenate(pairs, axis=1).astype(out_dtypes[t])
            return carry

        lax.fori_loop(0, nsub, sub, 0, unroll=True)

    outs = pl.pallas_call(
        body, name=name, grid=(r, seq // tq),
        in_specs=in_specs,
        out_specs=[cspec] * n_out,
        out_shape=[jax.ShapeDtypeStruct((seq, r * WIDTH_A), dt) for dt in out_dtypes],
        scratch_shapes=[pltpu.VMEM((tq + 2 * BAND_HALF, WIDTH_A), w.dtype) for w in window],
        compiler_params=_params(("parallel", "parallel")),
    )(*operands)
    return outs


def _slab_scratch(tm, w):
    return pltpu.VMEM((w // 128, tm, 128), F32)


def _put(scr, val):
    for j in range(scr.shape[0]):
        scr[j] = val[:, j * 128:(j + 1) * 128].astype(F32)


def _get(scr):
    return jnp.concatenate([scr[j] for j in range(scr.shape[0])], axis=1)


def _dilate_store(dst_ref, scr, r):
    nb, tm, _ = scr.shape
    w = nb * 128
    for c in range(r):
        for j in range(nb):
            dst_ref[:, c * w + j * 128:c * w + (j + 1) * 128] = scr[j, pl.ds(c, tm // r, stride=r), :].astype(dst_ref.dtype)


def _undilate(scr, src_ref, r):
    nb, tm, _ = scr.shape
    w = nb * 128
    for c in range(r):
        for j in range(nb):
            scr[j, pl.ds(c, tm // r, stride=r), :] = src_ref[:, c * w + j * 128:c * w + (j + 1) * 128].astype(F32)


def _dil_spec(tm, r, w):
    return pl.BlockSpec((tm // r, r * w), lambda i: (i, 0))


def _dil_shape(s, r, w, dtype):
    return jax.ShapeDtypeStruct((s // r, r * w), dtype)


def _in_proj(x, g, w, name):
    s, k = x.shape
    n = w.shape[0]
    tm = min(512, s)
    qkv = 3 * WIDTH_A

    def body(x_ref, g_ref, w_ref, o_ref, h_ref, d4_ref, d16_ref, scr):
        xv = x_ref[...]
        h = (xv * _rstd(xv) * g_ref[...]).astype(BF16)
        h_ref[...] = h
        acc = _dot(h, w_ref[...], NT)
        o_ref[...] = acc.astype(BF16)
        _put(scr, acc[:, 0:qkv])
        _dilate_store(d4_ref, scr, 4)
        _dilate_store(d16_ref, scr, 16)

    row = lambda c: pl.BlockSpec((tm, c), lambda i: (i, 0))
    return pl.pallas_call(
        body, name=name, grid=(s // tm,),
        in_specs=[row(k), pl.BlockSpec((1, k), lambda i: (0, 0)), pl.BlockSpec((n, k), lambda i: (0, 0))],
        out_specs=[row(n), row(k), _dil_spec(tm, 4, qkv), _dil_spec(tm, 16, qkv)],
        out_shape=[jax.ShapeDtypeStruct((s, n), BF16), jax.ShapeDtypeStruct((s, k), BF16),
                   _dil_shape(s, 4, qkv, BF16), _dil_shape(s, 16, qkv, BF16)],
        scratch_shapes=[_slab_scratch(tm, qkv)],
        compiler_params=_params(("parallel",)),
    )(x, g, w)


def _band_combine(os_, lses, name):
    s = os_[0].shape[0]
    tm = min(512, s)

    def body(o1, o4, o16, l1, l4, l16, ya_ref, lse_ref, lse4_ref, lse16_ref, so4, sl4, so16, sl16):
        _undilate(so4, o4, 4)
        _undilate(sl4, l4, 4)
        _undilate(so16, o16, 16)
        _undilate(sl16, l16, 16)
        a0, a1, a2 = l1[...], _get(sl4), _get(sl16)
        m = jnp.maximum(jnp.maximum(a0, a1), a2)
        e0, e1, e2 = jnp.exp2(a0 - m), jnp.exp2(a1 - m), jnp.exp2(a2 - m)
        den = e0 + e1 + e2
        ya_ref[...] = (e0 * o1[...] + e1 * _get(so4) + e2 * _get(so16)) / den
        lse = m + jnp.log(den) * LOG2E
        lse_ref[...] = lse
        _put(sl4, lse)
        _dilate_store(lse4_ref, sl4, 4)
        _dilate_store(lse16_ref, sl4, 16)

    nat = pl.BlockSpec((tm, WIDTH_A), lambda i: (i, 0))
    d4, d16 = _dil_spec(tm, 4, WIDTH_A), _dil_spec(tm, 16, WIDTH_A)
    return pl.pallas_call(
        body, name=name, grid=(s // tm,), in_specs=[nat, d4, d16] * 2, out_specs=[nat, nat, d4, d16],
        out_shape=[jax.ShapeDtypeStruct((s, WIDTH_A), F32)] * 2
        + [_dil_shape(s, 4, WIDTH_A, F32), _dil_shape(s, 16, WIDTH_A, F32)],
        scratch_shapes=[_slab_scratch(tm, WIDTH_A)] * 4,
        compiler_params=_params(("parallel",)),
    )(*os_, *lses)


def _rope_tables(s):
    pos = jnp.arange(s, dtype=F32)
    inv_freq = jnp.exp(-math.log(ROPE_BASE) * jnp.arange(0, QK_ROPE, 2, dtype=F32) / QK_ROPE)
    ang = pos[:, None] * inv_freq[None, :]
    cos, sin = jnp.cos(ang), jnp.sin(ang)
    one = jnp.ones((s, 64), F32)
    zero16 = jnp.zeros((s, 16), F32)
    c = jnp.concatenate([one, cos, cos, jnp.ones((s, 32), F32)], axis=1)
    sa = jnp.concatenate([jnp.zeros((s, 64), F32), -sin, zero16, jnp.zeros((s, 32), F32)], axis=1)
    sb = jnp.concatenate([jnp.zeros((s, 64), F32), zero16, sin, jnp.zeros((s, 32), F32)], axis=1)
    return c, sa, sb


def _rope_fwd(x, c, sa, sb):
    return x * c + pltpu.roll(x, 112, 1) * sa + pltpu.roll(x, 16, 1) * sb


def _rope_bwd(dy, c, sa, sb):
    return dy * c + pltpu.roll(dy * sa, 16, 1) + pltpu.roll(dy * sb, 112, 1)


def _mla_prep(proj, g_q, g_kv, w_uq, w_ukv, tabs, name):
    s = proj.shape[0]
    tm = min(512, s)
    width = N_HEADS * 128

    def body(lat_ref, gq_ref, gkv_ref, wq_ref, wkv_ref, c_ref, sa_ref, sb_ref,
             q_ref, k_ref, kv_ref, cqn_ref, ckvn_ref):
        c, sa, sb = c_ref[...], sa_ref[...], sb_ref[...]
        cq = lat_ref[:, 0:Q_LORA].astype(F32)
        cqn = (cq * _rstd(cq) * gq_ref[...]).astype(BF16)
        cqn_ref[...] = cqn
        q = _dot(cqn, wq_ref[...], NT)
        ckv = lat_ref[:, Q_LORA:Q_LORA + KV_LORA].astype(F32)
        ckvn = (ckv * _rstd(ckv) * gkv_ref[...]).astype(BF16)
        ckvn_ref[...] = ckvn
        kv = _dot(ckvn, wkv_ref[...], NN)
        lane = lax.broadcasted_iota(jnp.int32, (tm, 128), 1)
        krr = _rope_fwd(lat_ref[:, Q_LORA + KV_LORA:].astype(F32), c, sa, sb)
        krr = jnp.where((lane == 96) | (lane == 97), 1.0, krr)
        ones01 = jnp.where(lane < 2, 1.0, 0.0)
        for h in range(N_HEADS):
            cols = slice(h * 128, (h + 1) * 128)
            q_ref[:, cols] = (_rope_fwd(q[:, cols], c, sa, sb) * (MLA_SCALE * LOG2E)).astype(BF16)
            k_ref[:, cols] = jnp.where(lane < 64, kv[:, cols], krr).astype(BF16)
            kv_ref[:, cols] = jnp.where(lane < 64, ones01, kv[:, cols]).astype(BF16)

    row = lambda n: pl.BlockSpec((tm, n), lambda i: (i, 0))
    full = lambda a: pl.BlockSpec(a.shape, lambda i: (0, 0))
    tab = pl.BlockSpec((tm, 128), lambda i: (i, 0))
    return pl.pallas_call(
        body, name=name, grid=(s // tm,),
        in_specs=[pl.BlockSpec((tm, N_LAT), lambda i: (i, COL_CQ // N_LAT)),
                  full(g_q), full(g_kv), full(w_uq), full(w_ukv), tab, tab, tab],
        out_specs=[row(width), row(width), row(width), row(Q_LORA), row(KV_LORA)],
        out_shape=[jax.ShapeDtypeStruct((s, width), BF16)] * 3
        + [jax.ShapeDtypeStruct((s, Q_LORA), BF16), jax.ShapeDtypeStruct((s, KV_LORA), BF16)],
        compiler_params=_params(("parallel",)),
    )(proj, g_q, g_kv, w_uq, w_ukv, *tabs)


def _mla_fwd(qcat, kcat, kvb, name):
    s = qcat.shape[0]
    tq = min(1024, s)
    tk = min(1024, s)
    nkc = s // tk

    def body(q_ref, k_ref, v_ref, yb_ref, qaug_ref, m_ref, acc_ref):
        lane = lax.broadcasted_iota(jnp.int32, (tq, 128), 1)
        m_ref[...] = jnp.full((2, tq, 128), NEG, F32)
        acc_ref[...] = jnp.zeros((2, tq, 128), F32)

        def chunk(cidx, carry):
            k0 = pl.multiple_of(cidx * tk, tk)
            cols = [slice(a * 128, (a + 1) * 128) for a in range(2)]
            scs = [_dot(q_ref[:, c], k_ref[pl.ds(k0, tk), c], NT) for c in cols]
            prs, alphas = [], []
            for a, sc in enumerate(scs):
                m_prev = m_ref[a]
                m_new = jnp.maximum(m_prev, jnp.max(sc, axis=-1, keepdims=True))
                alphas.append(jnp.exp2(m_prev - m_new))
                prs.append(jnp.exp2(sc - jnp.tile(m_new, (1, tk // 128))).astype(BF16))
                m_ref[a] = m_new
            for a, c in enumerate(cols):
                acc_ref[a] = alphas[a] * acc_ref[a] + _dot(prs[a], v_ref[pl.ds(k0, tk), c], NN)
            return carry

        lax.fori_loop(0, nkc, chunk, 0, unroll=2)
        outs = []
        for a in range(2):
            cols = slice(a * 128, (a + 1) * 128)
            acc = acc_ref[a]
            l = acc[:, 0:1]
            outs.append(acc / l)
            hi, lo = _split_hi_lo(m_ref[a] + jnp.log(l) * LOG2E)
            qaug_ref[:, cols] = jnp.where(lane == 96, -hi, jnp.where(lane == 97, -lo, q_ref[:, cols]))
        yb_ref[...] = jnp.where(lane < 64, pltpu.roll(outs[0], 64, 1), outs[1])

    return pl.pallas_call(
        body, name=name, grid=(4, s // tq),
        in_specs=[pl.BlockSpec((tq, 256), lambda p, i: (i, p)),
                  pl.BlockSpec((s, 256), lambda p, i: (0, p)),
                  pl.BlockSpec((s, 256), lambda p, i: (0, p))],
        out_specs=[pl.BlockSpec((tq, 128), lambda p, i: (i, p)),
                   pl.BlockSpec((tq, 256), lambda p, i: (i, p))],
        out_shape=[jax.ShapeDtypeStruct((s, WIDTH_A), F32), jax.ShapeDtypeStruct((s, N_HEADS * 128), BF16)],
        scratch_shapes=[pltpu.VMEM((2, tq, 128), F32)] * 2,
        compiler_params=_params(("parallel", "parallel")),
    )(qcat, kcat, kvb)


def _mla_bwd(qaug, kcat, kvb, doaug, name):
    s = qaug.shape[0]
    tq = min(1024, s)
    tk = min(512, s)
    nqc = s // tq
    width = N_HEADS * 128

    def body(q_ref, do_ref, k_ref, v_ref, dq_ref, dk_acc, dv_acc):
        j = pl.program_id(1)

        @pl.when(j == 0)
        def _():
            dq_ref[...] = jnp.zeros_like(dq_ref)

        dk_acc[...] = jnp.zeros_like(dk_acc)
        dv_acc[...] = jnp.zeros_like(dv_acc)

        def chunk(cidx, carry):
            q0 = pl.multiple_of(cidx * tq, tq)
            cols = [slice(a * 128, (a + 1) * 128) for a in range(2)]
            qs = [q_ref[pl.ds(q0, tq), c] for c in cols]
            dos = [do_ref[pl.ds(q0, tq), c] for c in cols]
            kbs = [k_ref[:, c] for c in cols]
            sts = [_dot(kbs[a], qs[a], NT) for a in range(2)]
            dps = [_dot(v_ref[:, cols[a]], dos[a], NT) for a in range(2)]
            pts, dsts = [], []
            for a in range(2):
                pt = jnp.exp2(sts[a])
                pts.append(pt.astype(BF16))
                dsts.append((pt * dps[a]).astype(BF16))
            for a, c in enumerate(cols):
                dv_acc[:, c] += _dot(pts[a], dos[a], NN)
                dk_acc[:, c] += _dot(dsts[a], qs[a], NN)
                dq_ref[pl.ds(q0, tq), c] += _dot(dsts[a], kbs[a], TN)
            return carry

        lax.fori_loop(0, nqc, chunk, 0)

    return pl.pallas_call(
        body, name=name, grid=(N_HEADS // 2, s // tk),
        in_specs=[pl.BlockSpec((s, 256), lambda p, j: (0, p)),
                  pl.BlockSpec((s, 256), lambda p, j: (0, p)),
                  pl.BlockSpec((tk, 256), lambda p, j: (j, p)),
                  pl.BlockSpec((tk, 256), lambda p, j: (j, p))],
        out_specs=[pl.BlockSpec((s, 256), lambda p, j: (0, p)),
                   pl.BlockSpec((tk, 256), lambda p, j: (j, p)),
                   pl.BlockSpec((tk, 256), lambda p, j: (j, p))],
        out_shape=[jax.ShapeDtypeStruct((s, width), F32)] * 3,
        compiler_params=_params(("parallel", "arbitrary")),
    )(qaug, doaug, kcat, kvb)


def _mla_prep_bwd(dqc, dkc, dvp, proj, cqn, ckvn, g_q, g_kv, w_uq, w_ukv, tabs, dproj, name):
    s = proj.shape[0]
    tm = min(512, s)
    width = N_HEADS * 128

    def body(dq_ref, dk_ref, dv_ref, lat_ref, cqn_ref, ckvn_ref, gq_ref, gkv_ref, wq_ref, wkv_ref,
             c_ref, sa_ref, sb_ref, _dproj_in, dproj_ref, dwq_ref, dwkv_ref, dgq_ref, dgkv_ref):
        i = pl.program_id(0)

        @pl.when(i == 0)
        def _():
            dwq_ref[...] = jnp.zeros_like(dwq_ref)
            dwkv_ref[...] = jnp.zeros_like(dwkv_ref)
            dgq_ref[...] = jnp.zeros_like(dgq_ref)
            dgkv_ref[...] = jnp.zeros_like(dgkv_ref)

        c, sa, sb = c_ref[...], sa_ref[...], sb_ref[...]
        lane = lax.broadcasted_iota(jnp.int32, (tm, 128), 1)
        dkr = jnp.zeros((tm, 128), F32)
        dq_parts, dkv_parts = [], []
        for h in range(N_HEADS):
            cols = slice(h * 128, (h + 1) * 128)
            dq_parts.append(_rope_bwd(dq_ref[:, cols] * MLA_SCALE, c, sa, sb).astype(BF16))
            dkh = dk_ref[:, cols] * LN2
            dkr = dkr + dkh
            dkv_parts.append(jnp.where(lane < 64, dkh, dv_ref[:, cols]).astype(BF16))
        dq = jnp.concatenate(dq_parts, axis=1)
        dkv = jnp.concatenate(dkv_parts, axis=1)
        dkr = _rope_bwd(jnp.where((lane >= 64) & (lane < 96), dkr, 0.0), c, sa, sb)

        dcqn = _dot(dq, wq_ref[...], NN)
        dwq_ref[...] += _dot(dq, cqn_ref[...], TN)
        dcq, dgq = _rms_bwd(dcqn, lat_ref[:, 0:Q_LORA].astype(F32), gq_ref[...])
        dgq_ref[...] += jnp.sum(dgq, axis=0, keepdims=True)

        dckvn = _dot(dkv, wkv_ref[...], NT)
        dwkv_ref[...] += _dot(ckvn_ref[...], dkv, TN)
        dckv, dgkv = _rms_bwd(dckvn, lat_ref[:, Q_LORA:Q_LORA + KV_LORA].astype(F32), gkv_ref[...])
        dgkv_ref[...] += jnp.sum(dgkv, axis=0, keepdims=True)

        dproj_ref[:, 0:Q_LORA] = dcq.astype(BF16)
        dproj_ref[:, Q_LORA:Q_LORA + KV_LORA] = dckv.astype(BF16)
        dproj_ref[:, Q_LORA + KV_LORA:] = dkr.astype(BF16)

    row = lambda n: pl.BlockSpec((tm, n), lambda i: (i, 0))
    full = lambda a: pl.BlockSpec(a.shape, lambda i: (0, 0))
    tab = pl.BlockSpec((tm, 128), lambda i: (i, 0))
    return pl.pallas_call(
        body, name=name, grid=(s // tm,),
        in_specs=[row(width), row(width), row(width),
                  pl.BlockSpec((tm, N_LAT), lambda i: (i, COL_CQ // N_LAT)),
                  row(Q_LORA), row(KV_LORA), full(g_q), full(g_kv), full(w_uq), full(w_ukv), tab, tab, tab,
                  pl.BlockSpec(memory_space=pl.ANY)],
        out_specs=[pl.BlockSpec((tm, N_LAT), lambda i: (i, COL_CQ // N_LAT)), full(w_uq), full(w_ukv), full(g_q), full(g_kv)],
        out_shape=[jax.ShapeDtypeStruct(dproj.shape, BF16),
                   jax.ShapeDtypeStruct(w_uq.shape, F32), jax.ShapeDtypeStruct(w_ukv.shape, F32),
                   jax.ShapeDtypeStruct(g_q.shape, F32), jax.ShapeDtypeStruct(g_kv.shape, F32)],
        input_output_aliases={13: 0},
        compiler_params=_params(("arbitrary",)),
    )(dqc, dkc, dvp, proj, cqn, ckvn, g_q, g_kv, w_uq, w_ukv, *tabs, dproj)


def _mix_out(ya, yb, na, nb, w_o, g_post, x, name):
    s = x.shape[0]
    tm = min(512, s)

    def body(ya_ref, yb_ref, na_ref, nb_ref, w_ref, g_ref, x_ref, yn_ref, y2_ref, x1_ref):
        a, b = ya_ref[...], yb_ref[...]
        yn = jnp.concatenate([a * _rstd(a) * na_ref[...], b * _rstd(b) * nb_ref[...]], axis=1).astype(BF16)
        yn_ref[...] = yn
        y2 = _dot(yn, w_ref[...], NN)
        y2_ref[...] = y2
        x1_ref[...] = x_ref[...] + y2 * _rstd(y2) * g_ref[...]

    row = lambda n: pl.BlockSpec((tm, n), lambda i: (i, 0))
    full = lambda a: pl.BlockSpec(a.shape, lambda i: (0, 0))
    return pl.pallas_call(
        body, name=name, grid=(s // tm,),
        in_specs=[row(WIDTH_A), row(WIDTH_A), full(na), full(nb), full(w_o), full(g_post), row(D_MODEL)],
        out_specs=[row(D_MODEL)] * 3,
        out_shape=[jax.ShapeDtypeStruct((s, D_MODEL), BF16), jax.ShapeDtypeStruct((s, D_MODEL), F32),
                   jax.ShapeDtypeStruct((s, D_MODEL), F32)],
        compiler_params=_params(("parallel",)),
    )(ya, yb, na, nb, w_o, g_post, x)


def _head_ones():
    blk = np.kron(np.eye(N_HEADS, dtype=np.float32), np.ones((64, 64), np.float32))
    return jnp.asarray(blk, F32)


def _outnorm_bwd(dyn, ya, yb, na, nb, ones, name):
    s = ya.shape[0]
    tm = min(512, s)

    def body(dyn_ref, ya_ref, yb_ref, na_ref, nb_ref, ones_ref, dya_ref, da_ref, do_ref, dna_ref, dnb_ref,
             dya4_ref, dya16_ref, da4_ref, da16_ref, scr):
        i = pl.program_id(0)

        @pl.when(i == 0)
        def _():
            dna_ref[...] = jnp.zeros_like(dna_ref)
            dnb_ref[...] = jnp.zeros_like(dnb_ref)

        a, b = ya_ref[...], yb_ref[...]
        dya, dna = _rms_bwd(dyn_ref[:, 0:WIDTH_A], a, na_ref[...])
        dyb, dnb = _rms_bwd(dyn_ref[:, WIDTH_A:], b, nb_ref[...])
        dna_ref[...] += jnp.sum(dna, axis=0, keepdims=True)
        dnb_ref[...] += jnp.sum(dnb, axis=0, keepdims=True)
        dya_b = dya.astype(BF16)
        dya_ref[...] = dya_b
        hp = lax.Precision.HIGHEST
        delta_a = jnp.dot(dya_b.astype(F32) * a, ones_ref[...], precision=hp, preferred_element_type=F32)
        da_ref[...] = delta_a
        _put(scr, dya_b)
        _dilate_store(dya4_ref, scr, 4)
        _dilate_store(dya16_ref, scr, 16)
        _put(scr, delta_a)
        _dilate_store(da4_ref, scr, 4)
        _dilate_store(da16_ref, scr, 16)
        dyb_b = dyb.astype(BF16)
        db = jnp.dot(dyb_b.astype(F32) * b, ones_ref[...], precision=hp, preferred_element_type=F32)
        lane = lax.broadcasted_iota(jnp.int32, (tm, 128), 1)
        zero = jnp.zeros((tm, 128), BF16)
        for p in range(4):
            cols = slice(p * 128, (p + 1) * 128)
            dyp = dyb_b[:, cols]
            dbp = db[:, cols]
            for a_ in range(2):
                src = pltpu.roll(dyp.astype(F32), 64, 1).astype(BF16) if a_ == 0 else dyp
                dlt = dbp if a_ == 0 else pltpu.roll(dbp, 64, 1)
                hi, lo = _split_hi_lo(dlt)
                blk = jnp.where(lane >= 64, src, jnp.where(lane == 0, -hi, jnp.where(lane == 1, -lo, zero)))
                h = 2 * p + a_
                do_ref[:, h * 128:(h + 1) * 128] = blk

    row = lambda n: pl.BlockSpec((tm, n), lambda i: (i, 0))
    full = lambda a: pl.BlockSpec(a.shape, lambda i: (0, 0))
    return pl.pallas_call(
        body, name=name, grid=(s // tm,),
        in_specs=[row(D_MODEL), row(WIDTH_A), row(WIDTH_A), full(na), full(nb), full(ones)],
        out_specs=[row(WIDTH_A), row(WIDTH_A), row(N_HEADS * 128), full(na), full(nb),
                   _dil_spec(tm, 4, WIDTH_A), _dil_spec(tm, 16, WIDTH_A), _dil_spec(tm, 4, WIDTH_A), _dil_spec(tm, 16, WIDTH_A)],
        out_shape=[jax.ShapeDtypeStruct((s, WIDTH_A), BF16), jax.ShapeDtypeStruct((s, WIDTH_A), F32),
                   jax.ShapeDtypeStruct((s, N_HEADS * 128), BF16),
                   jax.ShapeDtypeStruct(na.shape, F32), jax.ShapeDtypeStruct(nb.shape, F32),
                   _dil_shape(s, 4, WIDTH_A, BF16), _dil_shape(s, 16, WIDTH_A, BF16),
                   _dil_shape(s, 4, WIDTH_A, F32), _dil_shape(s, 16, WIDTH_A, F32)],
        scratch_shapes=[_slab_scratch(tm, WIDTH_A)],
        compiler_params=_params(("arbitrary",)),
    )(dyn, ya, yb, na, nb, ones)


def _sum_cast(parts, name):
    s = parts[0][0].shape[0]
    tm = min(512, s)

    def body(*refs):
        o_ref, s4, s16 = refs[9:]
        for t in range(3):
            _undilate(s4, refs[3 + t], 4)
            _undilate(s16, refs[6 + t], 16)
            acc = refs[t][...] + _get(s4) + _get(s16)
            o_ref[:, t * WIDTH_A:(t + 1) * WIDTH_A] = acc.astype(BF16)

    nat = pl.BlockSpec((tm, WIDTH_A), lambda i: (i, 0))
    flat = [parts[g][t] for g in range(3) for t in range(3)]
    return pl.pallas_call(
        body, name=name, grid=(s // tm,),
        in_specs=[nat] * 3 + [_dil_spec(tm, 4, WIDTH_A)] * 3 + [_dil_spec(tm, 16, WIDTH_A)] * 3,
        out_specs=pl.BlockSpec((tm, 3 * WIDTH_A), lambda i: (i, 0)),
        out_shape=jax.ShapeDtypeStruct((s, N_PROJ), BF16),
        scratch_shapes=[_slab_scratch(tm, WIDTH_A)] * 2,
        compiler_params=_params(("parallel",)),
    )(*flat)


HALO = 16


def _gelu(x):
    k = math.sqrt(2.0 / math.pi)
    t = jnp.tanh(k * (x + 0.044715 * x * x * x))
    return 0.5 * x * (1.0 + t), t


def _gelu_grad(x, t):
    k = math.sqrt(2.0 / math.pi)
    return 0.5 * (1.0 + t) + 0.5 * x * (1.0 - t * t) * k * (1.0 + 3 * 0.044715 * x * x)


def _halo_specs(s, tm, tn, lead):
    nb = s // HALO
    hb = tm // HALO
    pre = (lead,) if lead else ()
    z = (0,) if lead else ()
    main = pl.BlockSpec(pre + (tm, tn), lambda j, i: z + (i, j))
    prev = pl.BlockSpec(pre + (HALO, tn), lambda j, i: z + (jnp.maximum(i * hb - 1, 0), j))
    nxt = pl.BlockSpec(pre + (HALO, tn), lambda j, i: z + (jnp.minimum((i + 1) * hb, nb - 1), j))
    return [prev, main, nxt]


def _fill_ext(ext, prev, main, nxt, i, tm, s):
    ext[0:HALO, :] = jnp.where(i > 0, prev.astype(F32), 0.0)
    ext[HALO:HALO + tm, :] = main.astype(F32)
    ext[HALO + tm:2 * HALO + tm, :] = jnp.where((i + 1) * tm < s, nxt.astype(F32), 0.0)


STRIP = 16


def _shifted(ref, row0):
    n = STRIP + 16
    win = ref[pl.ds(pl.multiple_of(row0 - 8, 8), n), :]
    return pltpu.roll(win, 1, 0)[8:8 + STRIP], win[8:8 + STRIP], pltpu.roll(win, n - 1, 0)[8:8 + STRIP]


def _conv3(e, row0, w_ref, b_ref, t):
    m1, c0, p1 = _shifted(e, row0)
    return w_ref[t, 0:1, :] * m1 + w_ref[t, 1:2, :] * c0 + w_ref[t, 2:3, :] * p1 + b_ref[t]


def _conv_gate(up, cw, cb, name):
    _, s, c = up.shape
    tm = min(1024, s)
    tn = FF_SLAB

    def body(up_p, up_m, up_n, w_ref, b_ref, a_ref, eg, ev):
        i = pl.program_id(1)
        _fill_ext(eg, up_p[0], up_m[0], up_n[0], i, tm, s)
        _fill_ext(ev, up_p[1], up_m[1], up_n[1], i, tm, s)

        def strip(t, carry):
            r0 = pl.multiple_of(t * STRIP, STRIP)
            g, _ = _gelu(_conv3(eg, HALO + r0, w_ref, b_ref, 0))
            a_ref[pl.ds(r0, STRIP), :] = (g * _conv3(ev, HALO + r0, w_ref, b_ref, 1)).astype(BF16)
            return carry

        lax.fori_loop(0, tm // STRIP, strip, 0, unroll=2)

    return pl.pallas_call(
        body, name=name, grid=(c // tn, s // tm),
        in_specs=_halo_specs(s, tm, tn, 2)
        + [pl.BlockSpec((2, 3, tn), lambda j, i: (0, 0, j)), pl.BlockSpec((2, 1, tn), lambda j, i: (0, 0, j))],
        out_specs=pl.BlockSpec((tm, tn), lambda j, i: (i, j)),
        out_shape=jax.ShapeDtypeStruct((s, c), BF16),
        scratch_shapes=[pltpu.VMEM((tm + 2 * HALO, tn), F32)] * 2,
        compiler_params=_params(("parallel", "parallel")),
    )(up, up, up, cw, cb)


def _conv_gate_bwd(up, da, cw, cb, name):
    _, s, c = up.shape
    tm = min(512, s)
    tn = FF_SLAB
    te = tm + HALO

    def body(up_p, up_m, up_n, da_p, da_m, da_n, w_ref, b_ref, dup_ref, dw_ref, db_ref, eg, ev, ed, dug, duv):
        i = pl.program_id(1)

        @pl.when(i == 0)
        def _():
            dw_ref[...] = jnp.zeros_like(dw_ref)
            db_ref[...] = jnp.zeros_like(db_ref)

        _fill_ext(eg, up_p[0], up_m[0], up_n[0], i, tm, s)
        _fill_ext(ev, up_p[1], up_m[1], up_n[1], i, tm, s)
        _fill_ext(ed, da_p[...], da_m[...], da_n[...], i, tm, s)
        o = HALO // 2

        def du_strip(t, carry):
            r0 = pl.multiple_of(t * STRIP, STRIP)
            ug = _conv3(eg, o + r0, w_ref, b_ref, 0)
            uv = _conv3(ev, o + r0, w_ref, b_ref, 1)
            gl, th = _gelu(ug)
            dav = ed[pl.ds(pl.multiple_of(o + r0, 8), STRIP), :]
            dug[pl.ds(r0, STRIP), :] = dav * uv * _gelu_grad(ug, th)
            duv[pl.ds(r0, STRIP), :] = dav * gl
            return carry

        lax.fori_loop(0, te // STRIP, du_strip, 0, unroll=3)

        def back(du, e, t):
            def strip(k, acc):
                r0 = pl.multiple_of(k * STRIP, STRIP)
                dm1, c0, dp1 = _shifted(du, o + r0)
                dup_ref[t, pl.ds(r0, STRIP), :] = (w_ref[t, 0:1, :] * dp1 + w_ref[t, 1:2, :] * c0
                                                   + w_ref[t, 2:3, :] * dm1).astype(BF16)
                um1, u0, up1 = _shifted(e, HALO + r0)
                fold = lambda a: a[0:8] + a[8:16]
                return (acc[0] + fold(um1 * c0), acc[1] + fold(u0 * c0), acc[2] + fold(up1 * c0), acc[3] + fold(c0))

            zero = jnp.zeros((8, tn), F32)
            acc = lax.fori_loop(0, tm // STRIP, strip, (zero, zero, zero, zero), unroll=2)
            for k in range(3):
                dw_ref[t, k:k + 1, :] += jnp.sum(acc[k], axis=0, keepdims=True)
            db_ref[t] += jnp.sum(acc[3], axis=0, keepdims=True)

        back(dug, eg, 0)
        back(duv, ev, 1)

    wspec = pl.BlockSpec((2, 3, tn), lambda j, i: (0, 0, j))
    bspec = pl.BlockSpec((2, 1, tn), lambda j, i: (0, 0, j))
    return pl.pallas_call(
        body, name=name, grid=(c // tn, s // tm),
        in_specs=_halo_specs(s, tm, tn, 2) + _halo_specs(s, tm, tn, 0) + [wspec, bspec],
        out_specs=[pl.BlockSpec((2, tm, tn), lambda j, i: (0, i, j)), wspec, bspec],
        out_shape=[jax.ShapeDtypeStruct((2, s, c), BF16), jax.ShapeDtypeStruct((2, 3, c), F32),
                   jax.ShapeDtypeStruct((2, 1, c), F32)],
        scratch_shapes=[pltpu.VMEM((tm + 2 * HALO, tn), F32)] * 3 + [pltpu.VMEM((te, tn), F32)] * 2,
        compiler_params=_params(("parallel", "arbitrary")),
    )(up, up, up, da, da, da, cw, cb)


def _ffn_out(a, w_down, g_post, x1, target, name):
    s = x1.shape[0]
    tm = min(512, s)

    def body(a_ref, w_ref, g_ref, x1_ref, t_ref, dy3_ref, dx2_ref, loss_ref, dg_ref):
        i = pl.program_id(0)

        @pl.when(i == 0)
        def _():
            loss_ref[...] = jnp.zeros_like(loss_ref)
            dg_ref[...] = jnp.zeros_like(dg_ref)

        y3 = _dot(a_ref[...], w_ref[...], NN)
        g = g_ref[...]
        x2 = x1_ref[...] + y3 * _rstd(y3) * g
        diff = x2 - t_ref[...]
        loss_ref[...] += jnp.sum(jnp.sum(diff * diff, axis=1, keepdims=True), axis=0, keepdims=True)
        dx2 = diff * (1.0 / D_MODEL)
        dx2_ref[...] = dx2
        dy3, dg = _rms_bwd(dx2, y3, g)
        dy3_ref[...] = dy3.astype(BF16)
        dg_ref[...] += jnp.sum(dg, axis=0, keepdims=True)

    row = lambda n: pl.BlockSpec((tm, n), lambda i: (i, 0))
    full = lambda t: pl.BlockSpec(t.shape, lambda i: (0, 0))
    return pl.pallas_call(
        body, name=name, grid=(s // tm,),
        in_specs=[row(a.shape[1]), full(w_down), full(g_post), row(D_MODEL), row(D_MODEL)],
        out_specs=[row(D_MODEL), row(D_MODEL), pl.BlockSpec((8, 128), lambda i: (0, 0)), full(g_post)],
        out_shape=[jax.ShapeDtypeStruct((s, D_MODEL), BF16), jax.ShapeDtypeStruct((s, D_MODEL), F32),
                   jax.ShapeDtypeStruct((8, 128), F32), jax.ShapeDtypeStruct(g_post.shape, F32)],
        compiler_params=_params(("arbitrary",)),
    )(a, w_down, g_post, x1, target)


def _resnorm_bwd(dh2, x1, g_ffn_pre, dx2, y2, g_mix_post, name):
    s = x1.shape[0]
    tm = min(512, s)

    def body(dh_ref, x1_ref, gf_ref, dx2_ref, y2_ref, gp_ref, dx1_ref, dy2_ref, dgf_ref, dgp_ref):
        i = pl.program_id(0)

        @pl.when(i == 0)
        def _():
            dgf_ref[...] = jnp.zeros_like(dgf_ref)
            dgp_ref[...] = jnp.zeros_like(dgp_ref)

        dn, dgf = _rms_bwd(dh_ref[...], x1_ref[...], gf_ref[...])
        dx1 = dx2_ref[...] + dn
        dx1_ref[...] = dx1
        dgf_ref[...] += jnp.sum(dgf, axis=0, keepdims=True)
        dy2, dgp = _rms_bwd(dx1, y2_ref[...], gp_ref[...])
        dy2_ref[...] = dy2.astype(BF16)
        dgp_ref[...] += jnp.sum(dgp, axis=0, keepdims=True)

    row = pl.BlockSpec((tm, D_MODEL), lambda i: (i, 0))
    full = pl.BlockSpec((1, D_MODEL), lambda i: (0, 0))
    return pl.pallas_call(
        body, name=name, grid=(s // tm,),
        in_specs=[row, row, full, row, row, full],
        out_specs=[row, row, full, full],
        out_shape=[jax.ShapeDtypeStruct((s, D_MODEL), F32), jax.ShapeDtypeStruct((s, D_MODEL), BF16),
                   jax.ShapeDtypeStruct((1, D_MODEL), F32), jax.ShapeDtypeStruct((1, D_MODEL), F32)],
        compiler_params=_params(("arbitrary",)),
    )(dh2, x1, g_ffn_pre, dx2, y2, g_mix_post)


def _final_bwd(dh1, x, g_pre, dx1, name):
    s = x.shape[0]
    tm = min(512, s)

    def body(dh_ref, x_ref, g_ref, dx1_ref, dx_ref, dg_ref):
        @pl.when(pl.program_id(0) == 0)
        def _():
            dg_ref[...] = jnp.zeros_like(dg_ref)

        dn, dg = _rms_bwd(dh_ref[...], x_ref[...], g_ref[...])
        dx_ref[...] = dx1_ref[...] + dn
        dg_ref[...] += jnp.sum(dg, axis=0, keepdims=True)

    row = pl.BlockSpec((tm, D_MODEL), lambda i: (i, 0))
    full = pl.BlockSpec((1, D_MODEL), lambda i: (0, 0))
    return pl.pallas_call(
        body, name=name, grid=(s // tm,),
        in_specs=[row, row, full, row], out_specs=[row, full],
        out_shape=[jax.ShapeDtypeStruct((s, D_MODEL), F32), jax.ShapeDtypeStruct((1, D_MODEL), F32)],
        compiler_params=_params(("arbitrary",)),
    )(dh1, x, g_pre, dx1)


def _local_step(x, target, fw, rep, mixer_weights, early_grads, late_grads):
    s = x.shape[0]
    tabs = _rope_tables(s)
    w_in, w_uq, w_ukv, w_o = (fw[n] for n in ("w_in", "w_uq", "w_ukv", "w_o"))
    tr = min(2048, s)
    tcon = min(2048, s)

    proj, h1, qkv4, qkv16 = _in_proj(x, rep["norm_mix_pre"], w_in, "in_proj")
    qkv = {1: proj, 4: qkv4, 16: qkv16}
    q_of = lambda r: (qkv[r], lambda c: 3 * c)
    k_of = lambda r: (qkv[r], lambda c: 3 * c + 1)
    v_of = lambda r: (qkv[r], lambda c: 3 * c + 2)
    own = lambda a: (a, lambda c: c)
    biases = [_band_bias(r) for _, r in DIL_CONFIGS]
    os_, lses = [], []
    for g, (_, r) in enumerate(DIL_CONFIGS):
        o, l = _band_call("fwd", r, [q_of(r)], [k_of(r), v_of(r)], biases[g], f"band_fwd_r{r}")
        os_.append(o)
        lses.append(l)
    ya, lse_a, lse4, lse16 = _band_combine(os_, lses, "band_combine")
    qcat, kcat, kvb, cqn, ckvn = _mla_prep(proj, rep["q_lat_norm"], rep["kv_lat_norm"], w_uq, w_ukv, tabs, "mla_prep")
    yb, qaug = _mla_fwd(qcat, kcat, kvb, "mla_fwd")
    yn, y2, x1 = _mix_out(ya, yb, rep["out_norm_a"], rep["out_norm_b"], w_o, rep["norm_mix_post"], x, "mix_out")
    mw = mixer_weights(x1)
    w_up, w_down, cw, cb = mw["w_up"], mw["w_down"], mw["conv_w"], mw["conv_b"]
    ff = w_down.shape[0]
    up, h2 = _norm_matmul(x1, rep["norm_ffn_pre"], w_up, "up_proj")
    act = _conv_gate(up, cw, cb, "conv_gate")
    dy3, dx2, loss_acc, dg_ffn_post = _ffn_out(act, w_down, rep["norm_ffn_post"], x1, target, "ffn_out")

    grads = {"norm_ffn_post": dg_ffn_post}
    dact = _matmul(dy3, w_down, "nt", BF16, tr, ff // 2, D_MODEL, "d_act")
    grads["w_down"] = _matmul(act, dy3, "tn", BF16, ff // 2, D_MODEL, tcon, "dw_down")
    dup, grads["conv_w"], grads["conv_b"] = _conv_gate_bwd(up, dact, cw, cb, "conv_gate_bwd")
    half = N_DEV // 2
    dh2 = _matmul_core(
        dup, w_up, NN, (s // tr, 1, N_DEV),
        pl.BlockSpec((None, tr, FF_SLAB), lambda i, j, t: (t // half, i, t % half)),
        pl.BlockSpec((None, FF_SLAB, D_MODEL), lambda i, j, t: (t, 0, 0)),
        pl.BlockSpec((tr, D_MODEL), lambda i, j, t: (i, 0)),
        jax.ShapeDtypeStruct((s, D_MODEL), F32), (tr, D_MODEL), "d_h2")
    grads["w_up"] = _matmul_core(
        dup, h2, TN, (1, N_DEV, s // tcon),
        pl.BlockSpec((None, tcon, FF_SLAB), lambda i, j, t: (j // half, t, j % half)),
        pl.BlockSpec((tcon, D_MODEL), lambda i, j, t: (t, 0)),
        pl.BlockSpec((None, FF_SLAB, D_MODEL), lambda i, j, t: (j, 0, 0)),
        jax.ShapeDtypeStruct((N_DEV, FF_SLAB, D_MODEL), BF16), (FF_SLAB, D_MODEL), "dw_up")
    dx1, dy2, grads["norm_ffn_pre"], grads["norm_mix_post"] = _resnorm_bwd(
        dh2, x1, rep["norm_ffn_pre"], dx2, y2, rep["norm_mix_post"], "resnorm_bwd")
    dyn = _matmul(dy2, w_o, "nt", F32, tr, D_MODEL, D_MODEL, "d_yn")
    grads["w_o"] = _matmul(yn, dy2, "tn", BF16, D_MODEL, D_MODEL, tcon, "dw_o")
    token = early_grads(grads)
    dya, delta_a, doaug, grads["out_norm_a"], grads["out_norm_b"], dya4, dya16, delta4, delta16 = _outnorm_bwd(
        dyn, ya, yb, rep["out_norm_a"] + token, rep["out_norm_b"], _head_ones(), "outnorm_bwd")
    stats = {1: (dya, lse_a, delta_a), 4: (dya4, lse4, delta4), 16: (dya16, lse16, delta16)}
    parts = []
    for g, (_, r) in enumerate(DIL_CONFIGS):
        qside = [q_of(r)] + [own(a) for a in stats[r]]
        kside = [k_of(r), v_of(r)]
        (dq,) = _band_call("dq", r, qside, kside, biases[g], f"band_dq_r{r}")
        dk, dv = _band_call("dkv", r, kside, qside, biases[g], f"band_dkv_r{r}")
        parts.append((dq, dk, dv))
    dproj_a = _sum_cast(parts, "band_grad_sum")
    dqc, dkc, dvp = _mla_bwd(qaug, kcat, kvb, doaug, "mla_bwd")
    dproj, grads["w_uq"], grads["w_ukv"], grads["q_lat_norm"], grads["kv_lat_norm"] = _mla_prep_bwd(
        dqc, dkc, dvp, proj, cqn, ckvn, rep["q_lat_norm"], rep["kv_lat_norm"], w_uq, w_ukv, tabs, dproj_a, "mla_prep_bwd")
    grads["w_in"] = _matmul(dproj, h1, "tn", BF16, N_PROJ // 2, D_MODEL, tcon, "dw_in")
    token = late_grads(grads)
    dh1 = _matmul(dproj, w_in, "nn", F32, tr, D_MODEL, N_PROJ // 2, "d_h1")
    grad_x, grads["norm_mix_pre"] = _final_bwd(dh1, x, rep["norm_mix_pre"] + token, dx1, "final_bwd")
    loss = 0.5 / D_MODEL * loss_acc[0, 0]
    return loss, grad_x, grads


MESH = pl.DeviceIdType.MESH
HBM_SPEC = pl.BlockSpec(memory_space=pltpu.HBM)
SMALL_ROWS = 96
TRANSPOSED = ("w_in", "w_up", "w_uq")
BUF_SHAPES = {"w_up": (FF_SLAB, D_MODEL), "w_in": (D_IN // N_DEV, D_MODEL), "w_down": (D_FF // N_DEV, D_MODEL),
              "w_o": (D_MODEL // N_DEV, D_MODEL), "w_uq": (QK_NOPE + QK_ROPE, Q_LORA), "w_ukv": (KV_LORA, 128),
              "conv_w": (8, FF_SLAB)}
BUF_ORDER = tuple(BUF_SHAPES)
MIXING = ("w_in", "w_o", "w_uq", "w_ukv")
MIXER = ("w_up", "w_down", "conv_w")
EARLY_GRADS = ("w_up", "w_down", "w_o", "conv_w")
LATE_GRADS = ("w_in", "w_uq", "w_ukv")


def _all_gather(bufs, name):
    nb = len(bufs)

    def body(*refs):
        x_refs, out_refs = refs[:nb], refs[nb:2 * nb]
        send_sems, recv_sems, local_sems = refs[2 * nb:]
        x, y, c = lax.axis_index("x"), lax.axis_index("y"), lax.axis_index("c")
        me, sibling = (x, y, c), (x, y, 1 - c)
        chips = [(1 - x, y), (x, 1 - y), (1 - x, 1 - y)]

        def copy(b, k, block, to, own=False):
            px, py, pc = block
            slot = out_refs[b].at[4 * px + 2 * py + pc]
            return pltpu.make_async_remote_copy(
                src_ref=x_refs[b] if own else slot, dst_ref=slot,
                send_sem=send_sems.at[7 * b + k], recv_sem=recv_sems.at[7 * b + k], device_id=to, device_id_type=MESH)

        mine = [pltpu.make_async_copy(x_refs[b], out_refs[b].at[4 * x + 2 * y + c], local_sems.at[b]) for b in range(nb)]
        sends = []
        for b in range(nb):
            mine[b].start()
            first = [copy(b, 0, me, sibling, own=True)]
            first += [copy(b, 1 + j, me, (*chip, c), own=True) for j, chip in enumerate(chips)]
            for cp in first:
                cp.start()
            sends += first
        for j, chip in enumerate(chips):
            for b in range(nb):
                copy(b, 1 + j, (*chip, c), me).wait_recv()
                passed = copy(b, 4 + j, (*chip, c), sibling)
                passed.start()
                sends.append(passed)
        for b in range(nb):
            copy(b, 0, sibling, me).wait_recv()
            for j, chip in enumerate(chips):
                copy(b, 4 + j, (*chip, 1 - c), me).wait_recv()
        for cp in sends:
            cp.wait_send()
        for cp in mine:
            cp.wait()

    return pl.pallas_call(
        body, name=name,
        out_shape=[jax.ShapeDtypeStruct((N_DEV,) + p.shape, p.dtype) for p in bufs],
        in_specs=[HBM_SPEC] * nb, out_specs=[HBM_SPEC] * nb,
        scratch_shapes=[pltpu.SemaphoreType.DMA((7 * nb,)), pltpu.SemaphoreType.DMA((7 * nb,)),
                        pltpu.SemaphoreType.DMA((nb,))],
    )(*bufs)


def _grad_exchange(bigs, small, name):
    flips = [(fx, fy, fc) for fx in (0, 1) for fy in (0, 1) for fc in (0, 1)][1:]
    nb = len(bigs)

    def body(*refs):
        big_refs, small_ref = refs[:nb], refs[nb]
        rbig_refs, rsmall_ref = refs[nb + 1:2 * nb + 1], refs[2 * nb + 1]
        send_sems, recv_sems, local_sems = refs[2 * nb + 2:]
        x, y, c = lax.axis_index("x"), lax.axis_index("y"), lax.axis_index("c")
        my = 4 * x + 2 * y + c
        own = [pltpu.make_async_copy(big_refs[b].at[my], rbig_refs[b].at[my], local_sems.at[b]) for b in range(nb)]
        own.append(pltpu.make_async_copy(small_ref, rsmall_ref.at[my], local_sems.at[nb]))
        for cp in own:
            cp.start()
        copies = []
        for b in range(nb + 1):
            for k, (fx, fy, fc) in enumerate(flips):
                px = 1 - x if fx else x
                py = 1 - y if fy else y
                pc = 1 - c if fc else c
                src = small_ref if b == nb else big_refs[b].at[4 * px + 2 * py + pc]
                dst = rsmall_ref.at[my] if b == nb else rbig_refs[b].at[my]
                copies.append(pltpu.make_async_remote_copy(
                    src_ref=src, dst_ref=dst, send_sem=send_sems.at[7 * b + k], recv_sem=recv_sems.at[7 * b + k],
                    device_id=(px, py, pc), device_id_type=MESH))
        for cp in copies:
            cp.start()
        for cp in copies:
            cp.wait()
        for cp in own:
            cp.wait()

    nsem = 7 * (nb + 1)
    return pl.pallas_call(
        body, name=name,
        out_shape=[jax.ShapeDtypeStruct(b.shape, b.dtype) for b in bigs]
        + [jax.ShapeDtypeStruct((N_DEV,) + small.shape, small.dtype)],
        in_specs=[HBM_SPEC] * (nb + 1), out_specs=[HBM_SPEC] * (nb + 1),
        scratch_shapes=[pltpu.SemaphoreType.DMA((nsem,)), pltpu.SemaphoreType.DMA((nsem,)),
                        pltpu.SemaphoreType.DMA((nb + 1,))],
    )(*bigs, small)


SEM_SPEC = pl.BlockSpec(memory_space=pltpu.SEMAPHORE)
ANY_SPEC = pl.BlockSpec(memory_space=pl.ANY)
FLIPS = tuple((fx, fy, fc) for fx in (0, 1) for fy in (0, 1) for fc in (0, 1))[1:]


def _split_copies(src_refs, land_refs, send_sems, recv_sems, scatter):
    x, y, c = lax.axis_index("x"), lax.axis_index("y"), lax.axis_index("c")
    my = 4 * x + 2 * y + c
    copies = []
    for b, (src, land) in enumerate(zip(src_refs, land_refs)):
        for k, (fx, fy, fc) in enumerate(FLIPS):
            px = 1 - x if fx else x
            py = 1 - y if fy else y
            pc = 1 - c if fc else c
            copies.append(pltpu.make_async_remote_copy(
                src_ref=src.at[4 * px + 2 * py + pc] if scatter else src, dst_ref=land.at[my],
                send_sem=send_sems.at[7 * b + k], recv_sem=recv_sems.at[7 * b + k],
                device_id=(px, py, pc), device_id_type=MESH))
    return copies


def _exchange_start(srcs, scatter, name):
    nb = len(srcs)
    lands = [lax.empty(s.shape if scatter else (N_DEV,) + s.shape, s.dtype) for s in srcs]

    def body(*refs):
        src_refs, land_refs = refs[:nb], refs[nb:2 * nb]
        send_sems, recv_sems = refs[2 * nb], refs[2 * nb + 1]
        token = refs[-1]
        for cp in _split_copies(src_refs, land_refs, send_sems, recv_sems, scatter):
            cp.start()
        token[...] = jnp.zeros_like(token)

    hbm = lambda a: pltpu.HBM(a.shape, a.dtype)
    outs = pl.pallas_call(
        body, name=name,
        out_shape=(pltpu.SemaphoreType.DMA((7 * nb,)), pltpu.SemaphoreType.DMA((7 * nb,)),
                   *[hbm(a) for a in srcs], *[hbm(a) for a in lands], jax.ShapeDtypeStruct((8, 128), F32)),
        in_specs=[HBM_SPEC] * (2 * nb),
        out_specs=(SEM_SPEC, SEM_SPEC, *[HBM_SPEC] * (2 * nb), pl.BlockSpec(memory_space=pltpu.VMEM)),
        input_output_aliases={i: 2 + i for i in range(2 * nb)},
        compiler_params=pltpu.CompilerParams(has_side_effects=pltpu.SideEffectType.DATAFLOW_SIDE_EFFECTING),
    )(*[pltpu.with_memory_space_constraint(a, pltpu.HBM) for a in srcs],
      *[pltpu.with_memory_space_constraint(a, pltpu.HBM) for a in lands])
    return outs[0], outs[1], list(outs[2:2 + nb]), list(outs[2 + nb:2 + 2 * nb]), outs[-1]


def _exchange_wait(started, scatter, after, name):
    send_sems, recv_sems, srcs, lands, _ = started
    nb = len(srcs)

    def body(*refs):
        src_refs, land_refs = refs[:nb], refs[nb:2 * nb]
        for cp in _split_copies(src_refs, land_refs, refs[2 * nb], refs[2 * nb + 1], scatter):
            cp.wait_send()
            cp.wait_recv()

    hbm = lambda a: pltpu.HBM(a.shape, a.dtype)
    outs = pl.pallas_call(
        body, name=name,
        out_shape=(*[hbm(a) for a in srcs], *[hbm(a) for a in lands]),
        in_specs=[HBM_SPEC] * (2 * nb) + [SEM_SPEC, SEM_SPEC, ANY_SPEC],
        out_specs=tuple([HBM_SPEC] * (2 * nb)),
        input_output_aliases={i: i for i in range(2 * nb)},
        compiler_params=pltpu.CompilerParams(has_side_effects=pltpu.SideEffectType.DATAFLOW_SIDE_EFFECTING),
    )(*srcs, *lands, send_sems, recv_sems, after)
    return list(outs[:nb]), list(outs[nb:])


def _own_slot(land, own):
    my = 4 * lax.axis_index("x") + 2 * lax.axis_index("y") + lax.axis_index("c")
    return lax.dynamic_update_slice(land, own[None], (my,) + (0,) * own.ndim)


def _adamw(parts, w, m, v, name):
    rows, n = w.shape
    tm = rows if rows <= 384 else next(t for t in (256, 176) if rows % t == 0)
    assert rows % tm == 0

    def body(p_ref, w_ref, m_ref, v_ref, g_ref, d_ref, m2_ref, v2_ref):
        g = p_ref[0, :, 0:n].astype(F32)
        for s in range(1, N_DEV):
            g = g + p_ref[s, :, 0:n].astype(F32)
        g_ref[...] = g
        m2 = ADAM_B1 * m_ref[...] + (1.0 - ADAM_B1) * g
        v2 = ADAM_B2 * v_ref[...] + (1.0 - ADAM_B2) * jnp.square(g)
        m2_ref[...] = m2
        v2_ref[...] = v2
        m_hat = m2 / (1.0 - ADAM_B1 ** ADAM_STEP)
        v_hat = v2 / (1.0 - ADAM_B2 ** ADAM_STEP)
        d_ref[...] = -ADAM_LR * (m_hat / (jnp.sqrt(v_hat) + ADAM_EPS) + ADAM_WD * w_ref[...])

    row = pl.BlockSpec((tm, n), lambda i: (i, 0))
    return pl.pallas_call(
        body, name=name, grid=(rows // tm,),
        in_specs=[pl.BlockSpec((N_DEV, tm, parts.shape[2]), lambda i: (0, i, 0)), row, row, row],
        out_specs=[row] * 4,
        out_shape=[jax.ShapeDtypeStruct((rows, n), F32)] * 4,
        compiler_params=_params(("parallel",)),
    )(parts, w, m, v)


def _pack(flat_parts, rows):
    flat = jnp.concatenate(flat_parts, axis=-1)
    pad = rows * LANES - flat.shape[-1]
    flat = jnp.pad(flat, [(0, 0)] * (flat.ndim - 1) + [(0, pad)])
    return flat.reshape(flat.shape[:-1] + (rows, LANES))


def _unpack(packed, shapes):
    flat = packed.reshape(packed.shape[:-2] + (-1,))
    out, off = {}, 0
    for name, shape in shapes.items():
        n = int(np.prod(shape))
        out[name] = flat[..., off:off + n].reshape(flat.shape[:-1] + tuple(shape))
        off += n
    return out


def _pad_to(a, shape):
    return jnp.pad(a, [(0, t - d) for d, t in zip(a.shape, shape)])


def _pad_w_in(w):
    k = w.shape[1]
    z = lambda n: jnp.zeros((n, k), w.dtype)
    return jnp.concatenate([w[:COL_KR], z(64), w[COL_KR:], z(32)], axis=0)


def _unpad_w_in(w):
    return jnp.concatenate([w[:COL_KR], w[COL_KR + 64:COL_KR + 96]], axis=0)


def _assemble_weights(g, conv_b):
    half = N_DEV // 2
    cols = lambda a: a.transpose(1, 0, 2).reshape(a.shape[1], N_DEV * a.shape[2])
    make = {
        "w_in": lambda: _pad_w_in(g["w_in"].reshape(D_IN, D_MODEL)),
        "w_uq": lambda: _pad_to(g["w_uq"], (N_DEV, 128, Q_LORA)).reshape(N_DEV * 128, Q_LORA),
        "w_ukv": lambda: cols(g["w_ukv"]),
        "w_o": lambda: g["w_o"].reshape(D_MODEL, D_MODEL),
        "w_up": lambda: g["w_up"],
        "w_down": lambda: _pad_to(g["w_down"].reshape(half, FF_SHARD, D_MODEL),
                                  (half, FF_SLAB, D_MODEL)).reshape(half * FF_SLAB, D_MODEL),
        "conv_w": lambda: g["conv_w"][:, :3].reshape(2, half, 3, FF_SLAB).transpose(0, 2, 1, 3).reshape(2, 3, half * FF_SLAB),
    }
    fw = {n: make[n]() for n in g}
    if conv_b is not None:
        fw["conv_b"] = _pad_to(conv_b.reshape(2, 1, half, FF_SHARD), (2, 1, half, FF_SLAB)).reshape(2, 1, half * FF_SLAB)
    return fw


def _grad_bufs(grads, names):
    half = N_DEV // 2
    slabs = lambda a: a.reshape(a.shape[0], N_DEV, a.shape[1] // N_DEV).transpose(1, 0, 2)
    make = {
        "w_in": lambda: _unpad_w_in(grads["w_in"]).reshape((N_DEV,) + BUF_SHAPES["w_in"]),
        "w_uq": lambda: grads["w_uq"].reshape(N_DEV, 128, Q_LORA)[:, :QK_NOPE + QK_ROPE],
        "w_ukv": lambda: slabs(grads["w_ukv"]),
        "w_o": lambda: grads["w_o"].reshape((N_DEV,) + BUF_SHAPES["w_o"]),
        "w_up": lambda: grads["w_up"],
        "w_down": lambda: grads["w_down"].reshape(half, FF_SLAB, D_MODEL)[:, :FF_SHARD].reshape((N_DEV,) + BUF_SHAPES["w_down"]),
        "conv_w": lambda: _pad_to(grads["conv_w"].reshape(2, 3, half, FF_SLAB).transpose(0, 2, 1, 3).reshape(N_DEV, 3, FF_SLAB),
                                  (N_DEV,) + BUF_SHAPES["conv_w"]),
    }
    return [make[n]() if n == "conv_w" else make[n]().astype(BF16) for n in names]


def kernel(x, norm_mix_pre, w_in, q_lat_norm, w_uq, kv_lat_norm, w_ukv, out_norm_a, out_norm_b, w_o, norm_mix_post, norm_ffn_pre, w_up, conv_w, conv_b, w_down, norm_ffn_post, loss_target, m_norm_mix_pre, m_w_in, m_q_lat_norm, m_w_uq, m_kv_lat_norm, m_w_ukv, m_out_norm_a, m_out_norm_b, m_w_o, m_norm_mix_post, m_norm_ffn_pre, m_w_up, m_conv_w, m_conv_b, m_w_down, m_norm_ffn_post, v_norm_mix_pre, v_w_in, v_q_lat_norm, v_w_uq, v_kv_lat_norm, v_w_ukv, v_out_norm_a, v_out_norm_b, v_w_o, v_norm_mix_post, v_norm_ffn_pre, v_w_up, v_conv_w, v_conv_b, v_w_down, v_norm_ffn_post):
    given = dict(locals())
    shard = lambda a, n: a[0].T if n in TRANSPOSED else a[0]
    w = {n: shard(given[n], n) for n in WEIGHTS}
    m = {n: shard(given["m_" + n], n) for n in WEIGHTS}
    v = {n: shard(given["v_" + n], n) for n in WEIGHTS}
    rep_shapes = {n: w[n].shape for n in REPLICATED}

    buf = lambda n: _pad_to(w[n] if n == "conv_w" else w[n].astype(BF16), BUF_SHAPES[n])
    first = dict(zip(MIXING, _all_gather([buf(n) for n in MIXING], "weight_all_gather")))
    fw = _assemble_weights(first, None)
    tie = first["w_o"][0, 0, 0].astype(F32) * 0.0
    late_bufs = [buf(n) + tie.astype(w[n].dtype if n == "conv_w" else BF16) for n in MIXER]
    mixer_started = _exchange_start(late_bufs, False, "mixer_weights_start")
    rep = {n: given[n] for n in REPLICATED}
    rep["norm_mix_pre"] = rep["norm_mix_pre"] + mixer_started[4][0, 0]

    def mixer_weights(after):
        srcs, lands = _exchange_wait(mixer_started, False, after, "mixer_weights_wait")
        got = {n: _own_slot(land, own) for n, land, own in zip(MIXER, lands, srcs)}
        return _assemble_weights(got, conv_b)

    early = {}

    def early_grads(grads):
        early["started"] = _exchange_start(_grad_bufs(grads, EARLY_GRADS), True, "early_grads_start")
        return early["started"][4][0, 0]

    def late_grads(grads):
        early["late"] = _exchange_start(_grad_bufs(grads, LATE_GRADS), True, "late_grads_start")
        return early["late"][4][0, 0]

    loss_local, grad_x, grads = _local_step(x[0], loss_target[0], fw, rep, mixer_weights, early_grads, late_grads)

    grads["conv_b"] = grads["conv_b"].reshape(N_DEV, FF_SLAB)[:, :FF_SHARD]
    small = _pack([grads[n].reshape(-1) for n in REPLICATED] + [loss_local.reshape(1)], SMALL_ROWS)
    received_small = _grad_exchange([], small, "grad_exchange")[0]
    my = 4 * lax.axis_index("x") + 2 * lax.axis_index("y") + lax.axis_index("c")
    received = {}
    for names, key, tag in ((EARLY_GRADS, "started", "early_grads_wait"), (LATE_GRADS, "late", "late_grads_wait")):
        srcs, lands = _exchange_wait(early[key], True, received_small, tag)
        for n, land, src in zip(names, lands, srcs):
            received[n] = _own_slot(land, lax.dynamic_index_in_dim(src, my, 0, keepdims=False))
    results = [{}, {}, {}, {}]
    for n in BUF_ORDER:
        parts = received[n]
        if n == "conv_w":
            args = [_pad_to(t[n], BUF_SHAPES[n]) for t in (w, m, v)]
        else:
            args = [w[n], m[n], v[n]]
        outs = _adamw(parts, *args, f"adamw_{n}")
        for t in range(4):
            results[t][n] = outs[t][:w[n].shape[0], :w[n].shape[1]] if n == "conv_w" else outs[t]
    pk = lambda d: _pack([d[n].reshape(-1) for n in REPLICATED] + [jnp.zeros((1,), F32)], SMALL_ROWS)
    small_out = _adamw(received_small, pk(w), pk(m), pk(v), "adamw_replicated")
    rep_shapes["loss"] = (1,)
    for t in range(4):
        results[t].update(_unpack(small_out[t], rep_shapes))

    loss = results[0]["loss"][0]
    outs = [loss, grad_x[None]]
    for res in results:
        outs += [(res[n].T if n in TRANSPOSED else res[n])[None] for n in WEIGHTS]
    return tuple(outs)
```

```python
import math

import numpy as np
import jax
import jax.numpy as jnp
from jax import lax
from jax.experimental import pallas as pl
from jax.experimental.pallas import tpu as pltpu

F32 = jnp.float32
BF16 = jnp.bfloat16

D_MODEL = 1024
N_DEV = 8
WIDTH_A = 512
N_HEADS = 8
Q_LORA = 384
KV_LORA = 256
QK_ROPE = 32
QK_NOPE = 64
D_FF = 2816
FF_SHARD = 2 * D_FF // N_DEV
FF_SLAB = 768
DIL_CONFIGS = ((128, 1), (512, 4), (2048, 16))
BAND_HALF = 64
ROPE_BASE = 10000.0
EPS = 1e-6
NEG = -1e30
MLA_SCALE = (QK_NOPE + QK_ROPE) ** -0.5
SCALE_A = 0.125
LOG2E = 1.0 / math.log(2.0)
LN2 = math.log(2.0)
QSCALE_A = SCALE_A * LOG2E

COL_CQ = 3 * WIDTH_A
COL_CKV = COL_CQ + Q_LORA
COL_KR = COL_CKV + KV_LORA
N_PROJ = COL_KR + 128
N_LAT = N_PROJ - COL_CQ
D_IN = COL_KR + QK_ROPE

ADAM_LR = 0.001
ADAM_B1 = 0.9
ADAM_B2 = 0.999
ADAM_EPS = 1e-08
ADAM_WD = 0.01
ADAM_STEP = 10

LANES = 128
VMEM_LIMIT = 56 * 1024 * 1024

REPLICATED = ("norm_mix_pre", "q_lat_norm", "kv_lat_norm", "out_norm_a", "out_norm_b", "norm_mix_post",
              "norm_ffn_pre", "conv_b", "norm_ffn_post")
WEIGHTS = ("norm_mix_pre", "w_in", "q_lat_norm", "w_uq", "kv_lat_norm", "w_ukv", "out_norm_a", "out_norm_b", "w_o",
           "norm_mix_post", "norm_ffn_pre", "w_up", "conv_w", "conv_b", "w_down", "norm_ffn_post")


def _params(sem=None):
    return pltpu.CompilerParams(dimension_semantics=sem, vmem_limit_bytes=VMEM_LIMIT)


def _dot(a, b, dims):
    return lax.dot_general(a, b, (dims, ((), ())), preferred_element_type=F32)


NN = ((1,), (0,))
NT = ((1,), (1,))
TN = ((0,), (0,))


def _rstd(x):
    return lax.rsqrt(jnp.mean(x * x, axis=-1, keepdims=True) + EPS)


def _rms_bwd(dy, x, g):
    r = _rstd(x)
    z = x * r
    gy = dy * g
    dx = r * (gy - z * jnp.mean(gy * z, axis=-1, keepdims=True))
    return dx, dy * z


def _split_hi_lo(v):
    hi = v.astype(BF16)
    lo = (v - hi.astype(F32)).astype(BF16)
    return hi, lo


def _matmul(a, b, mode, out_dtype, tm, tn, tk, name):
    if mode == "nn":
        (m, k), n = a.shape, b.shape[1]
        a_spec = pl.BlockSpec((tm, tk), lambda i, j, s: (i, s))
        b_spec = pl.BlockSpec((tk, tn), lambda i, j, s: (s, j))
        dims = NN
    elif mode == "nt":
        (m, k), n = a.shape, b.shape[0]
        a_spec = pl.BlockSpec((tm, tk), lambda i, j, s: (i, s))
        b_spec = pl.BlockSpec((tn, tk), lambda i, j, s: (j, s))
        dims = NT
    else:
        (k, m), n = a.shape, b.shape[1]
        a_spec = pl.BlockSpec((tk, tm), lambda i, j, s: (s, i))
        b_spec = pl.BlockSpec((tk, tn), lambda i, j, s: (s, j))
        dims = TN
    assert m % tm == 0 and n % tn == 0 and k % tk == 0, (name, m, n, k, tm, tn, tk)
    return _matmul_core(a, b, dims, (m // tm, n // tn, k // tk), a_spec, b_spec,
                        pl.BlockSpec((tm, tn), lambda i, j, s: (i, j)), jax.ShapeDtypeStruct((m, n), out_dtype),
                        (tm, tn), name)


def _matmul_core(a, b, dims, grid, a_spec, b_spec, o_spec, out_sds, acc_shape, name):
    nk = grid[2]

    def body(a_ref, b_ref, o_ref, acc_ref):
        s = pl.program_id(2)

        @pl.when(s == 0)
        def _():
            acc_ref[...] = jnp.zeros_like(acc_ref)

        acc_ref[...] += _dot(a_ref[...].astype(BF16), b_ref[...].astype(BF16), dims)

        @pl.when(s == nk - 1)
        def _():
            o_ref[...] = acc_ref[...].astype(out_sds.dtype)

    return pl.pallas_call(
        body, name=name, grid=grid, in_specs=[a_spec, b_spec], out_specs=o_spec, out_shape=out_sds,
        scratch_shapes=[pltpu.VMEM(acc_shape, F32)],
        compiler_params=_params(("parallel", "parallel", "arbitrary")),
    )(a, b)


def _norm_matmul(x, g, w, name):
    s, k = x.shape
    tm = min(2048, s)
    nj, tn, _ = w.shape
    half = nj // 2
    w_spec = pl.BlockSpec((None, tn, k), lambda i, j: (j, 0, 0))
    o_spec = pl.BlockSpec((None, tm, tn), lambda i, j: (j // half, i, j % half))
    o_sds = jax.ShapeDtypeStruct((2, s, half * tn), BF16)

    def body(x_ref, g_ref, w_ref, o_ref, h_ref):
        @pl.when(pl.program_id(1) == 0)
        def _():
            xv = x_ref[...]
            h_ref[...] = (xv * _rstd(xv) * g_ref[...]).astype(BF16)

        o_ref[...] = _dot(h_ref[...], w_ref[...], NT).astype(BF16)

    return pl.pallas_call(
        body, name=name, grid=(s // tm, nj),
        in_specs=[pl.BlockSpec((tm, k), lambda i, j: (i, 0)),
                  pl.BlockSpec((1, k), lambda i, j: (0, 0)),
                  w_spec],
        out_specs=[o_spec, pl.BlockSpec((tm, k), lambda i, j: (i, 0))],
        out_shape=[o_sds, jax.ShapeDtypeStruct((s, k), BF16)],
        compiler_params=_params(("parallel", "arbitrary")),
    )(x, g, w)


def _band_bias(r):
    off = np.arange(256)[None, :] - BAND_HALF - np.arange(128)[:, None]
    slopes = np.exp2(-8.0 * np.arange(1, N_HEADS + 1, dtype=np.float32) / N_HEADS).astype(np.float32)
    dist = (np.abs(off) * r).astype(np.float32)
    bias = -slopes[:, None, None] * dist[None]
    bias = np.where((np.abs(off) <= BAND_HALF)[None], bias * np.float32(LOG2E), np.float32(NEG))
    return jnp.asarray(bias, F32)


def _band_call(mode, r, center, window, bias, name):
    seq = center[0][0].shape[0]
    tq = min(512, seq)
    nsub = tq // 128
    hb = tq // BAND_HALF
    nh = seq // BAND_HALF
    nc, nw = len(center), len(window)
    out_dtypes = {"fwd": (BF16, F32), "dq": (BF16,), "dkv": (BF16, BF16)}[mode]
    n_out = len(out_dtypes)

    def specs(col):
        return (pl.BlockSpec((BAND_HALF, WIDTH_A), lambda c, i: (jnp.maximum(i * hb - 1, 0), col(c))),
                pl.BlockSpec((tq, WIDTH_A), lambda c, i: (i, col(c))),
                pl.BlockSpec((BAND_HALF, WIDTH_A), lambda c, i: (jnp.minimum((i + 1) * hb, nh - 1), col(c))))

    cspec = pl.BlockSpec((tq, WIDTH_A), lambda c, i: (i, c))
    in_specs = [specs(col)[1] for _, col in center]
    operands = [a for a, _ in center]
    for a, col in window:
        in_specs += list(specs(col))
        operands += [a, a, a]
    in_specs.append(pl.BlockSpec((N_HEADS, 128, 256), lambda c, i: (0, 0, 0)))
    operands.append(bias)
    window = [a for a, _ in window]

    def aug_stat(base, stat_sw, lane, act, e0):
        hi, lo = _split_hi_lo(stat_sw)
        return jnp.where(act, base, jnp.where(lane == e0, -hi, jnp.where(lane == e0 + 1, -lo, jnp.zeros_like(hi))))

    def aug_ones(base, lane, e0):
        return jnp.where((lane == e0) | (lane == e0 + 1), jnp.ones_like(base), base)

    def body(*refs):
        c_refs = refs[:nc]
        w_refs = refs[nc:nc + 3 * nw]
        bias_ref = refs[nc + 3 * nw]
        o_refs = refs[nc + 3 * nw + 1:nc + 3 * nw + 1 + n_out]
        wins = refs[nc + 3 * nw + 1 + n_out:]
        i = pl.program_id(1)
        for t in range(nw):
            wins[t][0:BAND_HALF, :] = w_refs[3 * t][...]
            wins[t][BAND_HALF:BAND_HALF + tq, :] = w_refs[3 * t + 1][...]
            wins[t][BAND_HALF + tq:BAND_HALF + tq + BAND_HALF, :] = w_refs[3 * t + 2][...]

        def sub(j, carry):
            r0 = pl.multiple_of(j * 128, 128)
            wpos = i * tq + j * 128 - BAND_HALF + lax.broadcasted_iota(jnp.int32, (128, 256), 1)
            valid = (wpos >= 0) & (wpos < seq)
            lane_c = lax.broadcasted_iota(jnp.int32, (128, 128), 1)
            lane_w = lax.broadcasted_iota(jnp.int32, (256, 128), 1)
            heads = [(p, a) for p in range(4) for a in range(2)]
            first, last_ops = [], []
            for p, a in heads:
                cols = slice(p * 128, (p + 1) * 128)
                cs = [c[pl.ds(r0, 128), cols] for c in c_refs]
                ws = [w[pl.ds(r0, 256), cols] for w in wins]
                e0 = 64 if a == 0 else 0
                act_c = (lane_c < 64) if a == 0 else (lane_c >= 64)
                act_w = (lane_w < 64) if a == 0 else (lane_w >= 64)
                bias_a = bias_ref[2 * p + a]
                if mode == "fwd":
                    qa = jnp.where(act_c, cs[0] * QSCALE_A, jnp.zeros_like(cs[0]))
                    first.append((_dot(qa, ws[0], NT) + bias_a, None))
                    last_ops.append((ws[1],))
                elif mode == "dq":
                    q2, dy2, l2, d2 = cs
                    k2, v2 = ws
                    q_aug = aug_stat(q2 * QSCALE_A, pltpu.roll(l2, 64, 1), lane_c, act_c, e0)
                    dy_aug = aug_stat(dy2, pltpu.roll(d2, 64, 1), lane_c, act_c, e0)
                    first.append((_dot(q_aug, aug_ones(k2, lane_w, e0), NT) + bias_a,
                                  _dot(dy_aug, aug_ones(v2, lane_w, e0), NT)))
                    last_ops.append((k2,))
                else:
                    k2, v2 = cs
                    q2, dy2, l2, d2 = ws
                    q_aug = aug_stat(q2 * QSCALE_A, pltpu.roll(l2, 64, 1), lane_w, act_w, e0)
                    dy_aug = aug_stat(dy2, pltpu.roll(d2, 64, 1), lane_w, act_w, e0)
                    first.append((_dot(aug_ones(k2, lane_c, e0), q_aug, NT) + bias_a,
                                  _dot(aug_ones(v2, lane_c, e0), dy_aug, NT)))
                    last_ops.append((q_aug, dy_aug))
            mid = []
            for sc, dp in first:
                sc = jnp.where(valid, sc, NEG)
                if mode == "fwd":
                    m = jnp.max(sc, axis=-1, keepdims=True)
                    e = jnp.exp2(sc - m)
                    l = jnp.sum(e, axis=-1, keepdims=True)
                    mid.append((e.astype(BF16), l, m + jnp.log(l) * LOG2E))
                else:
                    pr = jnp.exp2(sc)
                    mid.append((pr.astype(BF16), (pr * dp).astype(BF16)))
            res = []
            for md, ops in zip(mid, last_ops):
                if mode == "fwd":
                    res.append((_dot(md[0], ops[0], NN) / md[1], jnp.broadcast_to(md[2], (128, 128))))
                elif mode == "dq":
                    res.append((_dot(md[1], ops[0], NN) * SCALE_A,))
                else:
                    res.append((_dot(md[1], ops[0], NN) * LN2, _dot(md[0], ops[1], NN)))
            for t in range(n_out):
                pairs = [jnp.where(lane_c < 64, res[2 * p][t], res[2 * p + 1][t]) for p in range(4)]
                o_refs[t][pl.ds(r0, 128), :] = jnp.concatenate(pairs, axis=1).astype(out_dtypes[t])
            return carry

        lax.fori_loop(0, nsub, sub, 0, unroll=True)

    outs = pl.pallas_call(
        body, name=name, grid=(r, seq // tq),
        in_specs=in_specs,
        out_specs=[cspec] * n_out,
        out_shape=[jax.ShapeDtypeStruct((seq, r * WIDTH_A), dt) for dt in out_dtypes],
        scratch_shapes=[pltpu.VMEM((tq + 2 * BAND_HALF, WIDTH_A), w.dtype) for w in window],
        compiler_params=_params(("parallel", "parallel")),
    )(*operands)
    return outs


def _slab_scratch(tm, w):
    return pltpu.VMEM((w // 128, tm, 128), F32)


def _put(scr, val):
    for j in range(scr.shape[0]):
        scr[j] = val[:, j * 128:(j + 1) * 128].astype(F32)


def _get(scr):
    return jnp.concatenate([scr[j] for j in range(scr.shape[0])], axis=1)


def _dilate_store(dst_ref, scr, r):
    nb, tm, _ = scr.shape
    w = nb * 128
    for c in range(r):
        for j in range(nb):
            dst_ref[:, c * w + j * 128:c * w + (j + 1) * 128] = scr[j, pl.ds(c, tm // r, stride=r), :].astype(dst_ref.dtype)


def _undilate(scr, src_ref, r):
    nb, tm, _ = scr.shape
    w = nb * 128
    for c in range(r):
        for j in range(nb):
            scr[j, pl.ds(c, tm // r, stride=r), :] = src_ref[:, c * w + j * 128:c * w + (j + 1) * 128].astype(F32)


def _dil_spec(tm, r, w):
    return pl.BlockSpec((tm // r, r * w), lambda i: (i, 0))


def _dil_shape(s, r, w, dtype):
    return jax.ShapeDtypeStruct((s // r, r * w), dtype)


def _in_proj(x, g, w, name):
    s, k = x.shape
    n = w.shape[0]
    tm = min(512, s)
    qkv = 3 * WIDTH_A

    def body(x_ref, g_ref, w_ref, o_ref, h_ref, d4_ref, d16_ref, scr):
        xv = x_ref[...]
        h = (xv * _rstd(xv) * g_ref[...]).astype(BF16)
        h_ref[...] = h
        acc = _dot(h, w_ref[...], NT)
        o_ref[...] = acc.astype(BF16)
        _put(scr, acc[:, 0:qkv])
        _dilate_store(d4_ref, scr, 4)
        _dilate_store(d16_ref, scr, 16)

    row = lambda c: pl.BlockSpec((tm, c), lambda i: (i, 0))
    return pl.pallas_call(
        body, name=name, grid=(s // tm,),
        in_specs=[row(k), pl.BlockSpec((1, k), lambda i: (0, 0)), pl.BlockSpec((n, k), lambda i: (0, 0))],
        out_specs=[row(n), row(k), _dil_spec(tm, 4, qkv), _dil_spec(tm, 16, qkv)],
        out_shape=[jax.ShapeDtypeStruct((s, n), BF16), jax.ShapeDtypeStruct((s, k), BF16),
                   _dil_shape(s, 4, qkv, BF16), _dil_shape(s, 16, qkv, BF16)],
        scratch_shapes=[_slab_scratch(tm, qkv)],
        compiler_params=_params(("parallel",)),
    )(x, g, w)


def _band_combine(os_, lses, name):
    s = os_[0].shape[0]
    tm = min(512, s)

    def body(o1, o4, o16, l1, l4, l16, ya_ref, lse_ref, lse4_ref, lse16_ref, so4, sl4, so16, sl16):
        _undilate(so4, o4, 4)
        _undilate(sl4, l4, 4)
        _undilate(so16, o16, 16)
        _undilate(sl16, l16, 16)
        a0, a1, a2 = l1[...], _get(sl4), _get(sl16)
        m = jnp.maximum(jnp.maximum(a0, a1), a2)
        e0, e1, e2 = jnp.exp2(a0 - m), jnp.exp2(a1 - m), jnp.exp2(a2 - m)
        den = e0 + e1 + e2
        ya_ref[...] = (e0 * o1[...] + e1 * _get(so4) + e2 * _get(so16)) / den
        lse = m + jnp.log(den) * LOG2E
        lse_ref[...] = lse
        _put(sl4, lse)
        _dilate_store(lse4_ref, sl4, 4)
        _dilate_store(lse16_ref, sl4, 16)

    nat = pl.BlockSpec((tm, WIDTH_A), lambda i: (i, 0))
    d4, d16 = _dil_spec(tm, 4, WIDTH_A), _dil_spec(tm, 16, WIDTH_A)
    return pl.pallas_call(
        body, name=name, grid=(s // tm,), in_specs=[nat, d4, d16] * 2, out_specs=[nat, nat, d4, d16],
        out_shape=[jax.ShapeDtypeStruct((s, WIDTH_A), F32)] * 2
        + [_dil_shape(s, 4, WIDTH_A, F32), _dil_shape(s, 16, WIDTH_A, F32)],
        scratch_shapes=[_slab_scratch(tm, WIDTH_A)] * 4,
        compiler_params=_params(("parallel",)),
    )(*os_, *lses)


def _rope_tables(s):
    pos = jnp.arange(s, dtype=F32)
    inv_freq = jnp.exp(-math.log(ROPE_BASE) * jnp.arange(0, QK_ROPE, 2, dtype=F32) / QK_ROPE)
    ang = pos[:, None] * inv_freq[None, :]
    cos, sin = jnp.cos(ang), jnp.sin(ang)
    one = jnp.ones((s, 64), F32)
    zero16 = jnp.zeros((s, 16), F32)
    c = jnp.concatenate([one, cos, cos, jnp.ones((s, 32), F32)], axis=1)
    sa = jnp.concatenate([jnp.zeros((s, 64), F32), -sin, zero16, jnp.zeros((s, 32), F32)], axis=1)
    sb = jnp.concatenate([jnp.zeros((s, 64), F32), zero16, sin, jnp.zeros((s, 32), F32)], axis=1)
    return c, sa, sb


def _rope_fwd(x, c, sa, sb):
    return x * c + pltpu.roll(x, 112, 1) * sa + pltpu.roll(x, 16, 1) * sb


def _rope_bwd(dy, c, sa, sb):
    return dy * c + pltpu.roll(dy * sa, 16, 1) + pltpu.roll(dy * sb, 112, 1)


def _mla_prep(proj, g_q, g_kv, w_uq, w_ukv, tabs, name):
    s = proj.shape[0]
    tm = min(512, s)
    width = N_HEADS * 128

    def body(lat_ref, gq_ref, gkv_ref, wq_ref, wkv_ref, c_ref, sa_ref, sb_ref,
             q_ref, k_ref, kv_ref, cqn_ref, ckvn_ref):
        c, sa, sb = c_ref[...], sa_ref[...], sb_ref[...]
        cq = lat_ref[:, 0:Q_LORA].astype(F32)
        cqn = (cq * _rstd(cq) * gq_ref[...]).astype(BF16)
        cqn_ref[...] = cqn
        q = _dot(cqn, wq_ref[...], NT)
        ckv = lat_ref[:, Q_LORA:Q_LORA + KV_LORA].astype(F32)
        ckvn = (ckv * _rstd(ckv) * gkv_ref[...]).astype(BF16)
        ckvn_ref[...] = ckvn
        kv = _dot(ckvn, wkv_ref[...], NN)
        lane = lax.broadcasted_iota(jnp.int32, (tm, 128), 1)
        krr = _rope_fwd(lat_ref[:, Q_LORA + KV_LORA:].astype(F32), c, sa, sb)
        krr = jnp.where((lane == 96) | (lane == 97), 1.0, krr)
        ones01 = jnp.where(lane < 2, 1.0, 0.0)
        for h in range(N_HEADS):
            cols = slice(h * 128, (h + 1) * 128)
            q_ref[:, cols] = (_rope_fwd(q[:, cols], c, sa, sb) * (MLA_SCALE * LOG2E)).astype(BF16)
            k_ref[:, cols] = jnp.where(lane < 64, kv[:, cols], krr).astype(BF16)
            kv_ref[:, cols] = jnp.where(lane < 64, ones01, kv[:, cols]).astype(BF16)

    row = lambda n: pl.BlockSpec((tm, n), lambda i: (i, 0))
    full = lambda a: pl.BlockSpec(a.shape, lambda i: (0, 0))
    tab = pl.BlockSpec((tm, 128), lambda i: (i, 0))
    return pl.pallas_call(
        body, name=name, grid=(s // tm,),
        in_specs=[pl.BlockSpec((tm, N_LAT), lambda i: (i, COL_CQ // N_LAT)),
                  full(g_q), full(g_kv), full(w_uq), full(w_ukv), tab, tab, tab],
        out_specs=[row(width), row(width), row(width), row(Q_LORA), row(KV_LORA)],
        out_shape=[jax.ShapeDtypeStruct((s, width), BF16)] * 3
        + [jax.ShapeDtypeStruct((s, Q_LORA), BF16), jax.ShapeDtypeStruct((s, KV_LORA), BF16)],
        compiler_params=_params(("parallel",)),
    )(proj, g_q, g_kv, w_uq, w_ukv, *tabs)


def _mla_fwd(qcat, kcat, kvb, name):
    s = qcat.shape[0]
    tq = min(1024, s)
    tk = min(1024, s)
    nkc = s // tk

    def body(q_ref, k_ref, v_ref, yb_ref, qaug_ref, m_ref, acc_ref):
        lane = lax.broadcasted_iota(jnp.int32, (tq, 128), 1)
        m_ref[...] = jnp.full((2, tq, 128), NEG, F32)
        acc_ref[...] = jnp.zeros((2, tq, 128), F32)

        def chunk(cidx, carry):
            k0 = pl.multiple_of(cidx * tk, tk)
            cols = [slice(a * 128, (a + 1) * 128) for a in range(2)]
            scs = [_dot(q_ref[:, c], k_ref[pl.ds(k0, tk), c], NT) for c in cols]
            prs, alphas = [], []
            for a, sc in enumerate(scs):
                m_prev = m_ref[a]
                m_new = jnp.maximum(m_prev, jnp.max(sc, axis=-1, keepdims=True))
                alphas.append(jnp.exp2(m_prev - m_new))
                prs.append(jnp.exp2(sc - jnp.tile(m_new, (1, tk // 128))).astype(BF16))
                m_ref[a] = m_new
            for a, c in enumerate(cols):
                acc_ref[a] = alphas[a] * acc_ref[a] + _dot(prs[a], v_ref[pl.ds(k0, tk), c], NN)
            return carry

        lax.fori_loop(0, nkc, chunk, 0, unroll=2)
        outs = []
        for a in range(2):
            cols = slice(a * 128, (a + 1) * 128)
            acc = acc_ref[a]
            l = acc[:, 0:1]
            outs.append(acc / l)
            hi, lo = _split_hi_lo(m_ref[a] + jnp.log(l) * LOG2E)
            qaug_ref[:, cols] = jnp.where(lane == 96, -hi, jnp.where(lane == 97, -lo, q_ref[:, cols]))
        yb_ref[...] = jnp.where(lane < 64, pltpu.roll(outs[0], 64, 1), outs[1])

    return pl.pallas_call(
        body, name=name, grid=(4, s // tq),
        in_specs=[pl.BlockSpec((tq, 256), lambda p, i: (i, p)),
                  pl.BlockSpec((s, 256), lambda p, i: (0, p)),
                  pl.BlockSpec((s, 256), lambda p, i: (0, p))],
        out_specs=[pl.BlockSpec((tq, 128), lambda p, i: (i, p)),
                   pl.BlockSpec((tq, 256), lambda p, i: (i, p))],
        out_shape=[jax.ShapeDtypeStruct((s, WIDTH_A), F32), jax.ShapeDtypeStruct((s, N_HEADS * 128), BF16)],
        scratch_shapes=[pltpu.VMEM((2, tq, 128), F32)] * 2,
        compiler_params=_params(("parallel", "parallel")),
    )(qcat, kcat, kvb)


def _mla_bwd(qaug, kcat, kvb, doaug, name):
    s = qaug.shape[0]
    tq = min(1024, s)
    tk = min(512, s)
    nqc = s // tq
    width = N_HEADS * 128

    def body(q_ref, do_ref, k_ref, v_ref, dq_ref, dk_acc, dv_acc):
        j = pl.program_id(1)

        @pl.when(j == 0)
        def _():
            dq_ref[...] = jnp.zeros_like(dq_ref)

        dk_acc[...] = jnp.zeros_like(dk_acc)
        dv_acc[...] = jnp.zeros_like(dv_acc)

        def chunk(cidx, carry):
            q0 = pl.multiple_of(cidx * tq, tq)
            cols = [slice(a * 128, (a + 1) * 128) for a in range(2)]
            qs = [q_ref[pl.ds(q0, tq), c] for c in cols]
            dos = [do_ref[pl.ds(q0, tq), c] for c in cols]
            kbs = [k_ref[:, c] for c in cols]
            sts = [_dot(kbs[a], qs[a], NT) for a in range(2)]
            dps = [_dot(v_ref[:, cols[a]], dos[a], NT) for a in range(2)]
            pts, dsts = [], []
            for a in range(2):
                pt = jnp.exp2(sts[a])
                pts.append(pt.astype(BF16))
                dsts.append((pt * dps[a]).astype(BF16))
            for a, c in enumerate(cols):
                dv_acc[:, c] += _dot(pts[a], dos[a], NN)
                dk_acc[:, c] += _dot(dsts[a], qs[a], NN)
                dq_ref[pl.ds(q0, tq), c] += _dot(dsts[a], kbs[a], TN)
            return carry

        lax.fori_loop(0, nqc, chunk, 0, unroll=2)

    return pl.pallas_call(
        body, name=name, grid=(N_HEADS // 2, s // tk),
        in_specs=[pl.BlockSpec((s, 256), lambda p, j: (0, p)),
                  pl.BlockSpec((s, 256), lambda p, j: (0, p)),
                  pl.BlockSpec((tk, 256), lambda p, j: (j, p)),
                  pl.BlockSpec((tk, 256), lambda p, j: (j, p))],
        out_specs=[pl.BlockSpec((s, 256), lambda p, j: (0, p)),
                   pl.BlockSpec((tk, 256), lambda p, j: (j, p)),
                   pl.BlockSpec((tk, 256), lambda p, j: (j, p))],
        out_shape=[jax.ShapeDtypeStruct((s, width), F32)] * 3,
        compiler_params=_params(("parallel", "arbitrary")),
    )(qaug, doaug, kcat, kvb)


def _mla_prep_bwd(dqc, dkc, dvp, proj, cqn, ckvn, g_q, g_kv, w_uq, w_ukv, tabs, dproj, name):
    s = proj.shape[0]
    tm = min(512, s)
    width = N_HEADS * 128

    def body(dq_ref, dk_ref, dv_ref, lat_ref, cqn_ref, ckvn_ref, gq_ref, gkv_ref, wq_ref, wkv_ref,
             c_ref, sa_ref, sb_ref, _dproj_in, dproj_ref, dwq_ref, dwkv_ref, dgq_ref, dgkv_ref):
        i = pl.program_id(0)

        @pl.when(i == 0)
        def _():
            dwq_ref[...] = jnp.zeros_like(dwq_ref)
            dwkv_ref[...] = jnp.zeros_like(dwkv_ref)
            dgq_ref[...] = jnp.zeros_like(dgq_ref)
            dgkv_ref[...] = jnp.zeros_like(dgkv_ref)

        c, sa, sb = c_ref[...], sa_ref[...], sb_ref[...]
        lane = lax.broadcasted_iota(jnp.int32, (tm, 128), 1)
        dkr = jnp.zeros((tm, 128), F32)
        dq_parts, dkv_parts = [], []
        for h in range(N_HEADS):
            cols = slice(h * 128, (h + 1) * 128)
            dq_parts.append(_rope_bwd(dq_ref[:, cols] * MLA_SCALE, c, sa, sb).astype(BF16))
            dkh = dk_ref[:, cols] * LN2
            dkr = dkr + dkh
            dkv_parts.append(jnp.where(lane < 64, dkh, dv_ref[:, cols]).astype(BF16))
        dq = jnp.concatenate(dq_parts, axis=1)
        dkv = jnp.concatenate(dkv_parts, axis=1)
        dkr = _rope_bwd(jnp.where((lane >= 64) & (lane < 96), dkr, 0.0), c, sa, sb)

        dcqn = _dot(dq, wq_ref[...], NN)
        dwq_ref[...] += _dot(dq, cqn_ref[...], TN)
        dcq, dgq = _rms_bwd(dcqn, lat_ref[:, 0:Q_LORA].astype(F32), gq_ref[...])
        dgq_ref[...] += jnp.sum(dgq, axis=0, keepdims=True)

        dckvn = _dot(dkv, wkv_ref[...], NT)
        dwkv_ref[...] += _dot(ckvn_ref[...], dkv, TN)
        dckv, dgkv = _rms_bwd(dckvn, lat_ref[:, Q_LORA:Q_LORA + KV_LORA].astype(F32), gkv_ref[...])
        dgkv_ref[...] += jnp.sum(dgkv, axis=0, keepdims=True)

        dproj_ref[:, 0:Q_LORA] = dcq.astype(BF16)
        dproj_ref[:, Q_LORA:Q_LORA + KV_LORA] = dckv.astype(BF16)
        dproj_ref[:, Q_LORA + KV_LORA:] = dkr.astype(BF16)

    row = lambda n: pl.BlockSpec((tm, n), lambda i: (i, 0))
    full = lambda a: pl.BlockSpec(a.shape, lambda i: (0, 0))
    tab = pl.BlockSpec((tm, 128), lambda i: (i, 0))
    return pl.pallas_call(
        body, name=name, grid=(s // tm,),
        in_specs=[row(width), row(width), row(width),
                  pl.BlockSpec((tm, N_LAT), lambda i: (i, COL_CQ // N_LAT)),
                  row(Q_LORA), row(KV_LORA), full(g_q), full(g_kv), full(w_uq), full(w_ukv), tab, tab, tab,
                  pl.BlockSpec(memory_space=pl.ANY)],
        out_specs=[pl.BlockSpec((tm, N_LAT), lambda i: (i, COL_CQ // N_LAT)), full(w_uq), full(w_ukv), full(g_q), full(g_kv)],
        out_shape=[jax.ShapeDtypeStruct(dproj.shape, BF16),
                   jax.ShapeDtypeStruct(w_uq.shape, F32), jax.ShapeDtypeStruct(w_ukv.shape, F32),
                   jax.ShapeDtypeStruct(g_q.shape, F32), jax.ShapeDtypeStruct(g_kv.shape, F32)],
        input_output_aliases={13: 0},
        compiler_params=_params(("arbitrary",)),
    )(dqc, dkc, dvp, proj, cqn, ckvn, g_q, g_kv, w_uq, w_ukv, *tabs, dproj)


def _mix_out(ya, yb, na, nb, w_o, g_post, x, name):
    s = x.shape[0]
    tm = min(512, s)

    def body(ya_ref, yb_ref, na_ref, nb_ref, w_ref, g_ref, x_ref, yn_ref, y2_ref, x1_ref):
        a, b = ya_ref[...], yb_ref[...]
        yn = jnp.concatenate([a * _rstd(a) * na_ref[...], b * _rstd(b) * nb_ref[...]], axis=1).astype(BF16)
        yn_ref[...] = yn
        y2 = _dot(yn, w_ref[...], NN)
        y2_ref[...] = y2
        x1_ref[...] = x_ref[...] + y2 * _rstd(y2) * g_ref[...]

    row = lambda n: pl.BlockSpec((tm, n), lambda i: (i, 0))
    full = lambda a: pl.BlockSpec(a.shape, lambda i: (0, 0))
    return pl.pallas_call(
        body, name=name, grid=(s // tm,),
        in_specs=[row(WIDTH_A), row(WIDTH_A), full(na), full(nb), full(w_o), full(g_post), row(D_MODEL)],
        out_specs=[row(D_MODEL)] * 3,
        out_shape=[jax.ShapeDtypeStruct((s, D_MODEL), BF16), jax.ShapeDtypeStruct((s, D_MODEL), F32),
                   jax.ShapeDtypeStruct((s, D_MODEL), F32)],
        compiler_params=_params(("parallel",)),
    )(ya, yb, na, nb, w_o, g_post, x)


def _head_ones():
    blk = np.kron(np.eye(N_HEADS, dtype=np.float32), np.ones((64, 64), np.float32))
    return jnp.asarray(blk, F32)


def _outnorm_bwd(dyn, ya, yb, na, nb, ones, name):
    s = ya.shape[0]
    tm = min(512, s)

    def body(dyn_ref, ya_ref, yb_ref, na_ref, nb_ref, ones_ref, dya_ref, da_ref, do_ref, dna_ref, dnb_ref,
             dya4_ref, dya16_ref, da4_ref, da16_ref, scr):
        i = pl.program_id(0)

        @pl.when(i == 0)
        def _():
            dna_ref[...] = jnp.zeros_like(dna_ref)
            dnb_ref[...] = jnp.zeros_like(dnb_ref)

        a, b = ya_ref[...], yb_ref[...]
        dya, dna = _rms_bwd(dyn_ref[:, 0:WIDTH_A], a, na_ref[...])
        dyb, dnb = _rms_bwd(dyn_ref[:, WIDTH_A:], b, nb_ref[...])
        dna_ref[...] += jnp.sum(dna, axis=0, keepdims=True)
        dnb_ref[...] += jnp.sum(dnb, axis=0, keepdims=True)
        dya_b = dya.astype(BF16)
        dya_ref[...] = dya_b
        hp = lax.Precision.HIGHEST
        delta_a = jnp.dot(dya_b.astype(F32) * a, ones_ref[...], precision=hp, preferred_element_type=F32)
        da_ref[...] = delta_a
        _put(scr, dya_b)
        _dilate_store(dya4_ref, scr, 4)
        _dilate_store(dya16_ref, scr, 16)
        _put(scr, delta_a)
        _dilate_store(da4_ref, scr, 4)
        _dilate_store(da16_ref, scr, 16)
        dyb_b = dyb.astype(BF16)
        db = jnp.dot(dyb_b.astype(F32) * b, ones_ref[...], precision=hp, preferred_element_type=F32)
        lane = lax.broadcasted_iota(jnp.int32, (tm, 128), 1)
        zero = jnp.zeros((tm, 128), BF16)
        for p in range(4):
            cols = slice(p * 128, (p + 1) * 128)
            dyp = dyb_b[:, cols]
            dbp = db[:, cols]
            for a_ in range(2):
                src = pltpu.roll(dyp.astype(F32), 64, 1).astype(BF16) if a_ == 0 else dyp
                dlt = dbp if a_ == 0 else pltpu.roll(dbp, 64, 1)
                hi, lo = _split_hi_lo(dlt)
                blk = jnp.where(lane >= 64, src, jnp.where(lane == 0, -hi, jnp.where(lane == 1, -lo, zero)))
                h = 2 * p + a_
                do_ref[:, h * 128:(h + 1) * 128] = blk

    row = lambda n: pl.BlockSpec((tm, n), lambda i: (i, 0))
    full = lambda a: pl.BlockSpec(a.shape, lambda i: (0, 0))
    return pl.pallas_call(
        body, name=name, grid=(s // tm,),
        in_specs=[row(D_MODEL), row(WIDTH_A), row(WIDTH_A), full(na), full(nb), full(ones)],
        out_specs=[row(WIDTH_A), row(WIDTH_A), row(N_HEADS * 128), full(na), full(nb),
                   _dil_spec(tm, 4, WIDTH_A), _dil_spec(tm, 16, WIDTH_A), _dil_spec(tm, 4, WIDTH_A), _dil_spec(tm, 16, WIDTH_A)],
        out_shape=[jax.ShapeDtypeStruct((s, WIDTH_A), BF16), jax.ShapeDtypeStruct((s, WIDTH_A), F32),
                   jax.ShapeDtypeStruct((s, N_HEADS * 128), BF16),
                   jax.ShapeDtypeStruct(na.shape, F32), jax.ShapeDtypeStruct(nb.shape, F32),
                   _dil_shape(s, 4, WIDTH_A, BF16), _dil_shape(s, 16, WIDTH_A, BF16),
                   _dil_shape(s, 4, WIDTH_A, F32), _dil_shape(s, 16, WIDTH_A, F32)],
        scratch_shapes=[_slab_scratch(tm, WIDTH_A)],
        compiler_params=_params(("arbitrary",)),
    )(dyn, ya, yb, na, nb, ones)


def _sum_cast(parts, name):
    s = parts[0][0].shape[0]
    tm = min(512, s)

    def body(*refs):
        o_ref, s4, s16 = refs[9:]
        for t in range(3):
            _undilate(s4, refs[3 + t], 4)
            _undilate(s16, refs[6 + t], 16)
            acc = refs[t][...] + _get(s4) + _get(s16)
            o_ref[:, t * WIDTH_A:(t + 1) * WIDTH_A] = acc.astype(BF16)

    nat = pl.BlockSpec((tm, WIDTH_A), lambda i: (i, 0))
    flat = [parts[g][t] for g in range(3) for t in range(3)]
    return pl.pallas_call(
        body, name=name, grid=(s // tm,),
        in_specs=[nat] * 3 + [_dil_spec(tm, 4, WIDTH_A)] * 3 + [_dil_spec(tm, 16, WIDTH_A)] * 3,
        out_specs=pl.BlockSpec((tm, 3 * WIDTH_A), lambda i: (i, 0)),
        out_shape=jax.ShapeDtypeStruct((s, N_PROJ), BF16),
        scratch_shapes=[_slab_scratch(tm, WIDTH_A)] * 2,
        compiler_params=_params(("parallel",)),
    )(*flat)


HALO = 16


def _gelu(x):
    k = math.sqrt(2.0 / math.pi)
    t = jnp.tanh(k * (x + 0.044715 * x * x * x))
    return 0.5 * x * (1.0 + t), t


def _gelu_grad(x, t):
    k = math.sqrt(2.0 / math.pi)
    return 0.5 * (1.0 + t) + 0.5 * x * (1.0 - t * t) * k * (1.0 + 3 * 0.044715 * x * x)


def _halo_specs(s, tm, tn, lead):
    nb = s // HALO
    hb = tm // HALO
    pre = (lead,) if lead else ()
    z = (0,) if lead else ()
    main = pl.BlockSpec(pre + (tm, tn), lambda j, i: z + (i, j))
    prev = pl.BlockSpec(pre + (HALO, tn), lambda j, i: z + (jnp.maximum(i * hb - 1, 0), j))
    nxt = pl.BlockSpec(pre + (HALO, tn), lambda j, i: z + (jnp.minimum((i + 1) * hb, nb - 1), j))
    return [prev, main, nxt]


def _fill_ext(ext, prev, main, nxt, i, tm, s):
    ext[0:HALO, :] = jnp.where(i > 0, prev.astype(F32), 0.0)
    ext[HALO:HALO + tm, :] = main.astype(F32)
    ext[HALO + tm:2 * HALO + tm, :] = jnp.where((i + 1) * tm < s, nxt.astype(F32), 0.0)


STRIP = 16


def _shifted(ref, row0):
    n = STRIP + 16
    win = ref[pl.ds(pl.multiple_of(row0 - 8, 8), n), :]
    return pltpu.roll(win, 1, 0)[8:8 + STRIP], win[8:8 + STRIP], pltpu.roll(win, n - 1, 0)[8:8 + STRIP]


def _conv3(e, row0, w_ref, b_ref, t):
    m1, c0, p1 = _shifted(e, row0)
    return w_ref[t, 0:1, :] * m1 + w_ref[t, 1:2, :] * c0 + w_ref[t, 2:3, :] * p1 + b_ref[t]


def _conv_gate(up, cw, cb, name):
    _, s, c = up.shape
    tm = min(1024, s)
    tn = FF_SLAB

    def body(up_p, up_m, up_n, w_ref, b_ref, a_ref, eg, ev):
        i = pl.program_id(1)
        _fill_ext(eg, up_p[0], up_m[0], up_n[0], i, tm, s)
        _fill_ext(ev, up_p[1], up_m[1], up_n[1], i, tm, s)

        def strip(t, carry):
            r0 = pl.multiple_of(t * STRIP, STRIP)
            g, _ = _gelu(_conv3(eg, HALO + r0, w_ref, b_ref, 0))
            a_ref[pl.ds(r0, STRIP), :] = (g * _conv3(ev, HALO + r0, w_ref, b_ref, 1)).astype(BF16)
            return carry

        lax.fori_loop(0, tm // STRIP, strip, 0, unroll=2)

    return pl.pallas_call(
        body, name=name, grid=(c // tn, s // tm),
        in_specs=_halo_specs(s, tm, tn, 2)
        + [pl.BlockSpec((2, 3, tn), lambda j, i: (0, 0, j)), pl.BlockSpec((2, 1, tn), lambda j, i: (0, 0, j))],
        out_specs=pl.BlockSpec((tm, tn), lambda j, i: (i, j)),
        out_shape=jax.ShapeDtypeStruct((s, c), BF16),
        scratch_shapes=[pltpu.VMEM((tm + 2 * HALO, tn), F32)] * 2,
        compiler_params=_params(("parallel", "parallel")),
    )(up, up, up, cw, cb)


def _conv_gate_bwd(up, da, cw, cb, name):
    _, s, c = up.shape
    tm = min(512, s)
    tn = FF_SLAB
    te = tm + HALO

    def body(up_p, up_m, up_n, da_p, da_m, da_n, w_ref, b_ref, dup_ref, dw_ref, db_ref, eg, ev, ed, dug, duv):
        i = pl.program_id(1)

        @pl.when(i == 0)
        def _():
            dw_ref[...] = jnp.zeros_like(dw_ref)
            db_ref[...] = jnp.zeros_like(db_ref)

        _fill_ext(eg, up_p[0], up_m[0], up_n[0], i, tm, s)
        _fill_ext(ev, up_p[1], up_m[1], up_n[1], i, tm, s)
        _fill_ext(ed, da_p[...], da_m[...], da_n[...], i, tm, s)
        o = HALO // 2

        def du_strip(t, carry):
            r0 = pl.multiple_of(t * STRIP, STRIP)
            ug = _conv3(eg, o + r0, w_ref, b_ref, 0)
            uv = _conv3(ev, o + r0, w_ref, b_ref, 1)
            gl, th = _gelu(ug)
            dav = ed[pl.ds(pl.multiple_of(o + r0, 8), STRIP), :]
            dug[pl.ds(r0, STRIP), :] = dav * uv * _gelu_grad(ug, th)
            duv[pl.ds(r0, STRIP), :] = dav * gl
            return carry

        lax.fori_loop(0, te // STRIP, du_strip, 0, unroll=3)

        def back(du, e, t):
            def strip(k, acc):
                r0 = pl.multiple_of(k * STRIP, STRIP)
                dm1, c0, dp1 = _shifted(du, o + r0)
                dup_ref[t, pl.ds(r0, STRIP), :] = (w_ref[t, 0:1, :] * dp1 + w_ref[t, 1:2, :] * c0
                                                   + w_ref[t, 2:3, :] * dm1).astype(BF16)
                um1, u0, up1 = _shifted(e, HALO + r0)
                fold = lambda a: a[0:8] + a[8:16]
                return (acc[0] + fold(um1 * c0), acc[1] + fold(u0 * c0), acc[2] + fold(up1 * c0), acc[3] + fold(c0))

            zero = jnp.zeros((8, tn), F32)
            acc = lax.fori_loop(0, tm // STRIP, strip, (zero, zero, zero, zero), unroll=2)
            for k in range(3):
                dw_ref[t, k:k + 1, :] += jnp.sum(acc[k], axis=0, keepdims=True)
            db_ref[t] += jnp.sum(acc[3], axis=0, keepdims=True)

        back(dug, eg, 0)
        back(duv, ev, 1)

    wspec = pl.BlockSpec((2, 3, tn), lambda j, i: (0, 0, j))
    bspec = pl.BlockSpec((2, 1, tn), lambda j, i: (0, 0, j))
    return pl.pallas_call(
        body, name=name, grid=(c // tn, s // tm),
        in_specs=_halo_specs(s, tm, tn, 2) + _halo_specs(s, tm, tn, 0) + [wspec, bspec],
        out_specs=[pl.BlockSpec((2, tm, tn), lambda j, i: (0, i, j)), wspec, bspec],
        out_shape=[jax.ShapeDtypeStruct((2, s, c), BF16), jax.ShapeDtypeStruct((2, 3, c), F32),
                   jax.ShapeDtypeStruct((2, 1, c), F32)],
        scratch_shapes=[pltpu.VMEM((tm + 2 * HALO, tn), F32)] * 3 + [pltpu.VMEM((te, tn), F32)] * 2,
        compiler_params=_params(("parallel", "arbitrary")),
    )(up, up, up, da, da, da, cw, cb)


def _ffn_out(a, w_down, g_post, x1, target, name):
    s = x1.shape[0]
    tm = min(512, s)

    def body(a_ref, w_ref, g_ref, x1_ref, t_ref, dy3_ref, dx2_ref, loss_ref, dg_ref):
        i = pl.program_id(0)

        @pl.when(i == 0)
        def _():
            loss_ref[...] = jnp.zeros_like(loss_ref)
            dg_ref[...] = jnp.zeros_like(dg_ref)

        y3 = _dot(a_ref[...], w_ref[...], NN)
        g = g_ref[...]
        x2 = x1_ref[...] + y3 * _rstd(y3) * g
        diff = x2 - t_ref[...]
        loss_ref[...] += jnp.sum(jnp.sum(diff * diff, axis=1, keepdims=True), axis=0, keepdims=True)
        dx2 = diff * (1.0 / D_MODEL)
        dx2_ref[...] = dx2
        dy3, dg = _rms_bwd(dx2, y3, g)
        dy3_ref[...] = dy3.astype(BF16)
        dg_ref[...] += jnp.sum(dg, axis=0, keepdims=True)

    row = lambda n: pl.BlockSpec((tm, n), lambda i: (i, 0))
    full = lambda t: pl.BlockSpec(t.shape, lambda i: (0, 0))
    return pl.pallas_call(
        body, name=name, grid=(s // tm,),
        in_specs=[row(a.shape[1]), full(w_down), full(g_post), row(D_MODEL), row(D_MODEL)],
        out_specs=[row(D_MODEL), row(D_MODEL), pl.BlockSpec((8, 128), lambda i: (0, 0)), full(g_post)],
        out_shape=[jax.ShapeDtypeStruct((s, D_MODEL), BF16), jax.ShapeDtypeStruct((s, D_MODEL), F32),
                   jax.ShapeDtypeStruct((8, 128), F32), jax.ShapeDtypeStruct(g_post.shape, F32)],
        compiler_params=_params(("arbitrary",)),
    )(a, w_down, g_post, x1, target)


def _resnorm_bwd(dh2, x1, g_ffn_pre, dx2, y2, g_mix_post, name):
    s = x1.shape[0]
    tm = min(512, s)

    def body(dh_ref, x1_ref, gf_ref, dx2_ref, y2_ref, gp_ref, dx1_ref, dy2_ref, dgf_ref, dgp_ref):
        i = pl.program_id(0)

        @pl.when(i == 0)
        def _():
            dgf_ref[...] = jnp.zeros_like(dgf_ref)
            dgp_ref[...] = jnp.zeros_like(dgp_ref)

        dn, dgf = _rms_bwd(dh_ref[...], x1_ref[...], gf_ref[...])
        dx1 = dx2_ref[...] + dn
        dx1_ref[...] = dx1
        dgf_ref[...] += jnp.sum(dgf, axis=0, keepdims=True)
        dy2, dgp = _rms_bwd(dx1, y2_ref[...], gp_ref[...])
        dy2_ref[...] = dy2.astype(BF16)
        dgp_ref[...] += jnp.sum(dgp, axis=0, keepdims=True)

    row = pl.BlockSpec((tm, D_MODEL), lambda i: (i, 0))
    full = pl.BlockSpec((1, D_MODEL), lambda i: (0, 0))
    return pl.pallas_call(
        body, name=name, grid=(s // tm,),
        in_specs=[row, row, full, row, row, full],
        out_specs=[row, row, full, full],
        out_shape=[jax.ShapeDtypeStruct((s, D_MODEL), F32), jax.ShapeDtypeStruct((s, D_MODEL), BF16),
                   jax.ShapeDtypeStruct((1, D_MODEL), F32), jax.ShapeDtypeStruct((1, D_MODEL), F32)],
        compiler_params=_params(("arbitrary",)),
    )(dh2, x1, g_ffn_pre, dx2, y2, g_mix_post)


def _final_bwd(dh1, x, g_pre, dx1, name):
    s = x.shape[0]
    tm = min(512, s)

    def body(dh_ref, x_ref, g_ref, dx1_ref, dx_ref, dg_ref):
        @pl.when(pl.program_id(0) == 0)
        def _():
            dg_ref[...] = jnp.zeros_like(dg_ref)

        dn, dg = _rms_bwd(dh_ref[...], x_ref[...], g_ref[...])
        dx_ref[...] = dx1_ref[...] + dn
        dg_ref[...] += jnp.sum(dg, axis=0, keepdims=True)

    row = pl.BlockSpec((tm, D_MODEL), lambda i: (i, 0))
    full = pl.BlockSpec((1, D_MODEL), lambda i: (0, 0))
    return pl.pallas_call(
        body, name=name, grid=(s // tm,),
        in_specs=[row, row, full, row], out_specs=[row, full],
        out_shape=[jax.ShapeDtypeStruct((s, D_MODEL), F32), jax.ShapeDtypeStruct((1, D_MODEL), F32)],
        compiler_params=_params(("arbitrary",)),
    )(dh1, x, g_pre, dx1)


def _local_step(x, target, fw, rep, mixer_weights, early_grads, late_grads):
    s = x.shape[0]
    tabs = _rope_tables(s)
    w_in, w_uq, w_ukv, w_o = (fw[n] for n in ("w_in", "w_uq", "w_ukv", "w_o"))
    tr = min(2048, s)
    tcon = min(2048, s)

    proj, h1, qkv4, qkv16 = _in_proj(x, rep["norm_mix_pre"], w_in, "in_proj")
    qkv = {1: proj, 4: qkv4, 16: qkv16}
    q_of = lambda r: (qkv[r], lambda c: 3 * c)
    k_of = lambda r: (qkv[r], lambda c: 3 * c + 1)
    v_of = lambda r: (qkv[r], lambda c: 3 * c + 2)
    own = lambda a: (a, lambda c: c)
    biases = [_band_bias(r) for _, r in DIL_CONFIGS]
    os_, lses = [], []
    for g, (_, r) in enumerate(DIL_CONFIGS):
        o, l = _band_call("fwd", r, [q_of(r)], [k_of(r), v_of(r)], biases[g], f"band_fwd_r{r}")
        os_.append(o)
        lses.append(l)
    ya, lse_a, lse4, lse16 = _band_combine(os_, lses, "band_combine")
    qcat, kcat, kvb, cqn, ckvn = _mla_prep(proj, rep["q_lat_norm"], rep["kv_lat_norm"], w_uq, w_ukv, tabs, "mla_prep")
    yb, qaug = _mla_fwd(qcat, kcat, kvb, "mla_fwd")
    yn, y2, x1 = _mix_out(ya, yb, rep["out_norm_a"], rep["out_norm_b"], w_o, rep["norm_mix_post"], x, "mix_out")
    mw = mixer_weights(x1)
    w_up, w_down, cw, cb = mw["w_up"], mw["w_down"], mw["conv_w"], mw["conv_b"]
    ff = w_down.shape[0]
    up, h2 = _norm_matmul(x1, rep["norm_ffn_pre"], w_up, "up_proj")
    act = _conv_gate(up, cw, cb, "conv_gate")
    dy3, dx2, loss_acc, dg_ffn_post = _ffn_out(act, w_down, rep["norm_ffn_post"], x1, target, "ffn_out")

    grads = {"norm_ffn_post": dg_ffn_post}
    dact = _matmul(dy3, w_down, "nt", BF16, tr, ff // 2, D_MODEL, "d_act")
    grads["w_down"] = _matmul(act, dy3, "tn", BF16, ff // 2, D_MODEL, tcon, "dw_down")
    dup, grads["conv_w"], grads["conv_b"] = _conv_gate_bwd(up, dact, cw, cb, "conv_gate_bwd")
    half = N_DEV // 2
    dh2 = _matmul_core(
        dup, w_up, NN, (s // tr, 1, N_DEV),
        pl.BlockSpec((None, tr, FF_SLAB), lambda i, j, t: (t // half, i, t % half)),
        pl.BlockSpec((None, FF_SLAB, D_MODEL), lambda i, j, t: (t, 0, 0)),
        pl.BlockSpec((tr, D_MODEL), lambda i, j, t: (i, 0)),
        jax.ShapeDtypeStruct((s, D_MODEL), F32), (tr, D_MODEL), "d_h2")
    grads["w_up"] = _matmul_core(
        dup, h2, TN, (1, N_DEV, s // tcon),
        pl.BlockSpec((None, tcon, FF_SLAB), lambda i, j, t: (j // half, t, j % half)),
        pl.BlockSpec((tcon, D_MODEL), lambda i, j, t: (t, 0)),
        pl.BlockSpec((None, FF_SLAB, D_MODEL), lambda i, j, t: (j, 0, 0)),
        jax.ShapeDtypeStruct((N_DEV, FF_SLAB, D_MODEL), BF16), (FF_SLAB, D_MODEL), "dw_up")
    dx1, dy2, grads["norm_ffn_pre"], grads["norm_mix_post"] = _resnorm_bwd(
        dh2, x1, rep["norm_ffn_pre"], dx2, y2, rep["norm_mix_post"], "resnorm_bwd")
    dyn = _matmul(dy2, w_o, "nt", F32, tr, D_MODEL, D_MODEL, "d_yn")
    grads["w_o"] = _matmul(yn, dy2, "tn", BF16, D_MODEL, D_MODEL, tcon, "dw_o")
    token = early_grads(grads)
    dya, delta_a, doaug, grads["out_norm_a"], grads["out_norm_b"], dya4, dya16, delta4, delta16 = _outnorm_bwd(
        dyn, ya, yb, rep["out_norm_a"] + token, rep["out_norm_b"], _head_ones(), "outnorm_bwd")
    stats = {1: (dya, lse_a, delta_a), 4: (dya4, lse4, delta4), 16: (dya16, lse16, delta16)}
    parts = []
    for g, (_, r) in enumerate(DIL_CONFIGS):
        qside = [q_of(r)] + [own(a) for a in stats[r]]
        kside = [k_of(r), v_of(r)]
        (dq,) = _band_call("dq", r, qside, kside, biases[g], f"band_dq_r{r}")
        dk, dv = _band_call("dkv", r, kside, qside, biases[g], f"band_dkv_r{r}")
        parts.append((dq, dk, dv))
    dproj_a = _sum_cast(parts, "band_grad_sum")
    dqc, dkc, dvp = _mla_bwd(qaug, kcat, kvb, doaug, "mla_bwd")
    dproj, grads["w_uq"], grads["w_ukv"], grads["q_lat_norm"], grads["kv_lat_norm"] = _mla_prep_bwd(
        dqc, dkc, dvp, proj, cqn, ckvn, rep["q_lat_norm"], rep["kv_lat_norm"], w_uq, w_ukv, tabs, dproj_a, "mla_prep_bwd")
    grads["w_in"] = _matmul(dproj, h1, "tn", BF16, N_PROJ // 2, D_MODEL, tcon, "dw_in")
    token = late_grads(grads)
    dh1 = _matmul(dproj, w_in, "nn", F32, tr, D_MODEL, N_PROJ // 2, "d_h1")
    grad_x, grads["norm_mix_pre"] = _final_bwd(dh1, x, rep["norm_mix_pre"] + token, dx1, "final_bwd")
    loss = 0.5 / D_MODEL * loss_acc[0, 0]
    return loss, grad_x, grads


MESH = pl.DeviceIdType.MESH
HBM_SPEC = pl.BlockSpec(memory_space=pltpu.HBM)
SMALL_ROWS = 96
TRANSPOSED = ("w_in", "w_up", "w_uq")
BUF_SHAPES = {"w_up": (FF_SLAB, D_MODEL), "w_in": (D_IN // N_DEV, D_MODEL), "w_down": (D_FF // N_DEV, D_MODEL),
              "w_o": (D_MODEL // N_DEV, D_MODEL), "w_uq": (QK_NOPE + QK_ROPE, Q_LORA), "w_ukv": (KV_LORA, 128),
              "conv_w": (8, FF_SLAB)}
BUF_ORDER = tuple(BUF_SHAPES)
MIXING = ("w_in", "w_o", "w_uq", "w_ukv")
MIXER = ("w_up", "w_down", "conv_w")
EARLY_GRADS = ("w_up", "w_down", "w_o", "conv_w")
LATE_GRADS = ("w_in", "w_uq", "w_ukv")


def _all_gather(bufs, name):
    nb = len(bufs)

    def body(*refs):
        x_refs, out_refs = refs[:nb], refs[nb:2 * nb]
        send_sems, recv_sems, local_sems = refs[2 * nb:]
        x, y, c = lax.axis_index("x"), lax.axis_index("y"), lax.axis_index("c")
        me, sibling = (x, y, c), (x, y, 1 - c)
        chips = [(1 - x, y), (x, 1 - y), (1 - x, 1 - y)]

        def copy(b, k, block, to, own=False):
            px, py, pc = block
            slot = out_refs[b].at[4 * px + 2 * py + pc]
            return pltpu.make_async_remote_copy(
                src_ref=x_refs[b] if own else slot, dst_ref=slot,
                send_sem=send_sems.at[7 * b + k], recv_sem=recv_sems.at[7 * b + k], device_id=to, device_id_type=MESH)

        mine = [pltpu.make_async_copy(x_refs[b], out_refs[b].at[4 * x + 2 * y + c], local_sems.at[b]) for b in range(nb)]
        sends = []
        for b in range(nb):
            mine[b].start()
            first = [copy(b, 0, me, sibling, own=True)]
            first += [copy(b, 1 + j, me, (*chip, c), own=True) for j, chip in enumerate(chips)]
            for cp in first:
                cp.start()
            sends += first
        for j, chip in enumerate(chips):
            for b in range(nb):
                copy(b, 1 + j, (*chip, c), me).wait_recv()
                passed = copy(b, 4 + j, (*chip, c), sibling)
                passed.start()
                sends.append(passed)
        for b in range(nb):
            copy(b, 0, sibling, me).wait_recv()
            for j, chip in enumerate(chips):
                copy(b, 4 + j, (*chip, 1 - c), me).wait_recv()
        for cp in sends:
            cp.wait_send()
        for cp in mine:
            cp.wait()

    return pl.pallas_call(
        body, name=name,
        out_shape=[jax.ShapeDtypeStruct((N_DEV,) + p.shape, p.dtype) for p in bufs],
        in_specs=[HBM_SPEC] * nb, out_specs=[HBM_SPEC] * nb,
        scratch_shapes=[pltpu.SemaphoreType.DMA((7 * nb,)), pltpu.SemaphoreType.DMA((7 * nb,)),
                        pltpu.SemaphoreType.DMA((nb,))],
    )(*bufs)


def _grad_exchange(bigs, small, name):
    flips = [(fx, fy, fc) for fx in (0, 1) for fy in (0, 1) for fc in (0, 1)][1:]
    nb = len(bigs)

    def body(*refs):
        big_refs, small_ref = refs[:nb], refs[nb]
        rbig_refs, rsmall_ref = refs[nb + 1:2 * nb + 1], refs[2 * nb + 1]
        send_sems, recv_sems, local_sems = refs[2 * nb + 2:]
        x, y, c = lax.axis_index("x"), lax.axis_index("y"), lax.axis_index("c")
        my = 4 * x + 2 * y + c
        own = [pltpu.make_async_copy(big_refs[b].at[my], rbig_refs[b].at[my], local_sems.at[b]) for b in range(nb)]
        own.append(pltpu.make_async_copy(small_ref, rsmall_ref.at[my], local_sems.at[nb]))
        for cp in own:
            cp.start()
        copies = []
        for b in range(nb + 1):
            for k, (fx, fy, fc) in enumerate(flips):
                px = 1 - x if fx else x
                py = 1 - y if fy else y
                pc = 1 - c if fc else c
                src = small_ref if b == nb else big_refs[b].at[4 * px + 2 * py + pc]
                dst = rsmall_ref.at[my] if b == nb else rbig_refs[b].at[my]
                copies.append(pltpu.make_async_remote_copy(
                    src_ref=src, dst_ref=dst, send_sem=send_sems.at[7 * b + k], recv_sem=recv_sems.at[7 * b + k],
                    device_id=(px, py, pc), device_id_type=MESH))
        for cp in copies:
            cp.start()
        for cp in copies:
            cp.wait()
        for cp in own:
            cp.wait()

    nsem = 7 * (nb + 1)
    return pl.pallas_call(
        body, name=name,
        out_shape=[jax.ShapeDtypeStruct(b.shape, b.dtype) for b in bigs]
        + [jax.ShapeDtypeStruct((N_DEV,) + small.shape, small.dtype)],
        in_specs=[HBM_SPEC] * (nb + 1), out_specs=[HBM_SPEC] * (nb + 1),
        scratch_shapes=[pltpu.SemaphoreType.DMA((nsem,)), pltpu.SemaphoreType.DMA((nsem,)),
                        pltpu.SemaphoreType.DMA((nb + 1,))],
    )(*bigs, small)


SEM_SPEC = pl.BlockSpec(memory_space=pltpu.SEMAPHORE)
ANY_SPEC = pl.BlockSpec(memory_space=pl.ANY)
FLIPS = tuple((fx, fy, fc) for fx in (0, 1) for fy in (0, 1) for fc in (0, 1))[1:]


def _split_copies(src_refs, land_refs, send_sems, recv_sems, scatter):
    x, y, c = lax.axis_index("x"), lax.axis_index("y"), lax.axis_index("c")
    my = 4 * x + 2 * y + c
    copies = []
    for b, (src, land) in enumerate(zip(src_refs, land_refs)):
        for k, (fx, fy, fc) in enumerate(FLIPS):
            px = 1 - x if fx else x
            py = 1 - y if fy else y
            pc = 1 - c if fc else c
            copies.append(pltpu.make_async_remote_copy(
                src_ref=src.at[4 * px + 2 * py + pc] if scatter else src, dst_ref=land.at[my],
                send_sem=send_sems.at[7 * b + k], recv_sem=recv_sems.at[7 * b + k],
                device_id=(px, py, pc), device_id_type=MESH))
    return copies


def _exchange_start(srcs, scatter, name):
    nb = len(srcs)
    lands = [lax.empty(s.shape if scatter else (N_DEV,) + s.shape, s.dtype) for s in srcs]

    def body(*refs):
        src_refs, land_refs = refs[:nb], refs[nb:2 * nb]
        send_sems, recv_sems = refs[2 * nb], refs[2 * nb + 1]
        token = refs[-1]
        for cp in _split_copies(src_refs, land_refs, send_sems, recv_sems, scatter):
            cp.start()
        token[...] = jnp.zeros_like(token)

    hbm = lambda a: pltpu.HBM(a.shape, a.dtype)
    outs = pl.pallas_call(
        body, name=name,
        out_shape=(pltpu.SemaphoreType.DMA((7 * nb,)), pltpu.SemaphoreType.DMA((7 * nb,)),
                   *[hbm(a) for a in srcs], *[hbm(a) for a in lands], jax.ShapeDtypeStruct((8, 128), F32)),
        in_specs=[HBM_SPEC] * (2 * nb),
        out_specs=(SEM_SPEC, SEM_SPEC, *[HBM_SPEC] * (2 * nb), pl.BlockSpec(memory_space=pltpu.VMEM)),
        input_output_aliases={i: 2 + i for i in range(2 * nb)},
        compiler_params=pltpu.CompilerParams(has_side_effects=pltpu.SideEffectType.DATAFLOW_SIDE_EFFECTING),
    )(*[pltpu.with_memory_space_constraint(a, pltpu.HBM) for a in srcs],
      *[pltpu.with_memory_space_constraint(a, pltpu.HBM) for a in lands])
    return outs[0], outs[1], list(outs[2:2 + nb]), list(outs[2 + nb:2 + 2 * nb]), outs[-1]


def _exchange_wait(started, scatter, after, name):
    send_sems, recv_sems, srcs, lands, _ = started
    nb = len(srcs)

    def body(*refs):
        src_refs, land_refs = refs[:nb], refs[nb:2 * nb]
        for cp in _split_copies(src_refs, land_refs, refs[2 * nb], refs[2 * nb + 1], scatter):
            cp.wait_send()
            cp.wait_recv()

    hbm = lambda a: pltpu.HBM(a.shape, a.dtype)
    outs = pl.pallas_call(
        body, name=name,
        out_shape=(*[hbm(a) for a in srcs], *[hbm(a) for a in lands]),
        in_specs=[HBM_SPEC] * (2 * nb) + [SEM_SPEC, SEM_SPEC, ANY_SPEC],
        out_specs=tuple([HBM_SPEC] * (2 * nb)),
        input_output_aliases={i: i for i in range(2 * nb)},
        compiler_params=pltpu.CompilerParams(has_side_effects=pltpu.SideEffectType.DATAFLOW_SIDE_EFFECTING),
    )(*srcs, *lands, send_sems, recv_sems, after)
    return list(outs[:nb]), list(outs[nb:])


def _own_slot(land, own):
    my = 4 * lax.axis_index("x") + 2 * lax.axis_index("y") + lax.axis_index("c")
    return lax.dynamic_update_slice(land, own[None], (my,) + (0,) * own.ndim)


def _adamw(parts, w, m, v, name):
    rows, n = w.shape
    tm = rows if rows <= 384 else next(t for t in (256, 176) if rows % t == 0)
    assert rows % tm == 0

    def body(p_ref, w_ref, m_ref, v_ref, g_ref, d_ref, m2_ref, v2_ref):
        g = p_ref[0, :, 0:n].astype(F32)
        for s in range(1, N_DEV):
            g = g + p_ref[s, :, 0:n].astype(F32)
        g_ref[...] = g
        m2 = ADAM_B1 * m_ref[...] + (1.0 - ADAM_B1) * g
        v2 = ADAM_B2 * v_ref[...] + (1.0 - ADAM_B2) * jnp.square(g)
        m2_ref[...] = m2
        v2_ref[...] = v2
        m_hat = m2 / (1.0 - ADAM_B1 ** ADAM_STEP)
        v_hat = v2 / (1.0 - ADAM_B2 ** ADAM_STEP)
        d_ref[...] = -ADAM_LR * (m_hat / (jnp.sqrt(v_hat) + ADAM_EPS) + ADAM_WD * w_ref[...])

    row = pl.BlockSpec((tm, n), lambda i: (i, 0))
    return pl.pallas_call(
        body, name=name, grid=(rows // tm,),
        in_specs=[pl.BlockSpec((N_DEV, tm, parts.shape[2]), lambda i: (0, i, 0)), row, row, row],
        out_specs=[row] * 4,
        out_shape=[jax.ShapeDtypeStruct((rows, n), F32)] * 4,
        compiler_params=_params(("parallel",)),
    )(parts, w, m, v)


def _pack(flat_parts, rows):
    flat = jnp.concatenate(flat_parts, axis=-1)
    pad = rows * LANES - flat.shape[-1]
    flat = jnp.pad(flat, [(0, 0)] * (flat.ndim - 1) + [(0, pad)])
    return flat.reshape(flat.shape[:-1] + (rows, LANES))


def _unpack(packed, shapes):
    flat = packed.reshape(packed.shape[:-2] + (-1,))
    out, off = {}, 0
    for name, shape in shapes.items():
        n = int(np.prod(shape))
        out[name] = flat[..., off:off + n].reshape(flat.shape[:-1] + tuple(shape))
        off += n
    return out


def _pad_to(a, shape):
    return jnp.pad(a, [(0, t - d) for d, t in zip(a.shape, shape)])


def _pad_w_in(w):
    k = w.shape[1]
    z = lambda n: jnp.zeros((n, k), w.dtype)
    return jnp.concatenate([w[:COL_KR], z(64), w[COL_KR:], z(32)], axis=0)


def _unpad_w_in(w):
    return jnp.concatenate([w[:COL_KR], w[COL_KR + 64:COL_KR + 96]], axis=0)


def _assemble_weights(g, conv_b):
    half = N_DEV // 2
    cols = lambda a: a.transpose(1, 0, 2).reshape(a.shape[1], N_DEV * a.shape[2])
    make = {
        "w_in": lambda: _pad_w_in(g["w_in"].reshape(D_IN, D_MODEL)),
        "w_uq": lambda: _pad_to(g["w_uq"], (N_DEV, 128, Q_LORA)).reshape(N_DEV * 128, Q_LORA),
        "w_ukv": lambda: cols(g["w_ukv"]),
        "w_o": lambda: g["w_o"].reshape(D_MODEL, D_MODEL),
        "w_up": lambda: g["w_up"],
        "w_down": lambda: _pad_to(g["w_down"].reshape(half, FF_SHARD, D_MODEL),
                                  (half, FF_SLAB, D_MODEL)).reshape(half * FF_SLAB, D_MODEL),
        "conv_w": lambda: g["conv_w"][:, :3].reshape(2, half, 3, FF_SLAB).transpose(0, 2, 1, 3).reshape(2, 3, half * FF_SLAB),
    }
    fw = {n: make[n]() for n in g}
    if conv_b is not None:
        fw["conv_b"] = _pad_to(conv_b.reshape(2, 1, half, FF_SHARD), (2, 1, half, FF_SLAB)).reshape(2, 1, half * FF_SLAB)
    return fw


def _grad_bufs(grads, names):
    half = N_DEV // 2
    slabs = lambda a: a.reshape(a.shape[0], N_DEV, a.shape[1] // N_DEV).transpose(1, 0, 2)
    make = {
        "w_in": lambda: _unpad_w_in(grads["w_in"]).reshape((N_DEV,) + BUF_SHAPES["w_in"]),
        "w_uq": lambda: grads["w_uq"].reshape(N_DEV, 128, Q_LORA)[:, :QK_NOPE + QK_ROPE],
        "w_ukv": lambda: slabs(grads["w_ukv"]),
        "w_o": lambda: grads["w_o"].reshape((N_DEV,) + BUF_SHAPES["w_o"]),
        "w_up": lambda: grads["w_up"],
        "w_down": lambda: grads["w_down"].reshape(half, FF_SLAB, D_MODEL)[:, :FF_SHARD].reshape((N_DEV,) + BUF_SHAPES["w_down"]),
        "conv_w": lambda: _pad_to(grads["conv_w"].reshape(2, 3, half, FF_SLAB).transpose(0, 2, 1, 3).reshape(N_DEV, 3, FF_SLAB),
                                  (N_DEV,) + BUF_SHAPES["conv_w"]),
    }
    return [make[n]() if n == "conv_w" else make[n]().astype(BF16) for n in names]


def kernel(x, norm_mix_pre, w_in, q_lat_norm, w_uq, kv_lat_norm, w_ukv, out_norm_a, out_norm_b, w_o, norm_mix_post, norm_ffn_pre, w_up, conv_w, conv_b, w_down, norm_ffn_post, loss_target, m_norm_mix_pre, m_w_in, m_q_lat_norm, m_w_uq, m_kv_lat_norm, m_w_ukv, m_out_norm_a, m_out_norm_b, m_w_o, m_norm_mix_post, m_norm_ffn_pre, m_w_up, m_conv_w, m_conv_b, m_w_down, m_norm_ffn_post, v_norm_mix_pre, v_w_in, v_q_lat_norm, v_w_uq, v_kv_lat_norm, v_w_ukv, v_out_norm_a, v_out_norm_b, v_w_o, v_norm_mix_post, v_norm_ffn_pre, v_w_up, v_conv_w, v_conv_b, v_w_down, v_norm_ffn_post):
    given = dict(locals())
    shard = lambda a, n: a[0].T if n in TRANSPOSED else a[0]
    w = {n: shard(given[n], n) for n in WEIGHTS}
    m = {n: shard(given["m_" + n], n) for n in WEIGHTS}
    v = {n: shard(given["v_" + n], n) for n in WEIGHTS}
    rep_shapes = {n: w[n].shape for n in REPLICATED}

    buf = lambda n: _pad_to(w[n] if n == "conv_w" else w[n].astype(BF16), BUF_SHAPES[n])
    first = dict(zip(MIXING, _all_gather([buf(n) for n in MIXING], "weight_all_gather")))
    fw = _assemble_weights(first, None)
    tie = first["w_o"][0, 0, 0].astype(F32) * 0.0
    late_bufs = [buf(n) + tie.astype(w[n].dtype if n == "conv_w" else BF16) for n in MIXER]
    mixer_started = _exchange_start(late_bufs, False, "mixer_weights_start")
    rep = {n: given[n] for n in REPLICATED}
    rep["norm_mix_pre"] = rep["norm_mix_pre"] + mixer_started[4][0, 0]

    def mixer_weights(after):
        srcs, lands = _exchange_wait(mixer_started, False, after, "mixer_weights_wait")
        got = {n: _own_slot(land, own) for n, land, own in zip(MIXER, lands, srcs)}
        return _assemble_weights(got, conv_b)

    early = {}

    def early_grads(grads):
        early["started"] = _exchange_start(_grad_bufs(grads, EARLY_GRADS), True, "early_grads_start")
        return early["started"][4][0, 0]

    def late_grads(grads):
        early["late"] = _exchange_start(_grad_bufs(grads, LATE_GRADS), True, "late_grads_start")
        return early["late"][4][0, 0]

    loss_local, grad_x, grads = _local_step(x[0], loss_target[0], fw, rep, mixer_weights, early_grads, late_grads)

    grads["conv_b"] = grads["conv_b"].reshape(N_DEV, FF_SLAB)[:, :FF_SHARD]
    small = _pack([grads[n].reshape(-1) for n in REPLICATED] + [loss_local.reshape(1)], SMALL_ROWS)
    received_small = _grad_exchange([], small, "grad_exchange")[0]
    my = 4 * lax.axis_index("x") + 2 * lax.axis_index("y") + lax.axis_index("c")
    received = {}
    for names, key, tag in ((EARLY_GRADS, "started", "early_grads_wait"), (LATE_GRADS, "late", "late_grads_wait")):
        srcs, lands = _exchange_wait(early[key], True, received_small, tag)
        for n, land, src in zip(names, lands, srcs):
            received[n] = _own_slot(land, lax.dynamic_index_in_dim(src, my, 0, keepdims=False))
    results = [{}, {}, {}, {}]
    for n in BUF_ORDER:
        parts = received[n]
        if n == "conv_w":
            args = [_pad_to(t[n], BUF_SHAPES[n]) for t in (w, m, v)]
        else:
            args = [w[n], m[n], v[n]]
        outs = _adamw(parts, *args, f"adamw_{n}")
        for t in range(4):
            results[t][n] = outs[t][:w[n].shape[0], :w[n].shape[1]] if n == "conv_w" else outs[t]
    pk = lambda d: _pack([d[n].reshape(-1) for n in REPLICATED] + [jnp.zeros((1,), F32)], SMALL_ROWS)
    small_out = _adamw(received_small, pk(w), pk(m), pk(v), "adamw_replicated")
    rep_shapes["loss"] = (1,)
    for t in range(4):
        results[t].update(_unpack(small_out[t], rep_shapes))

    loss = results[0]["loss"][0]
    outs = [loss, grad_x[None]]
    for res in results:
        outs += [(res[n].T if n in TRANSPOSED else res[n])[None] for n in WEIGHTS]
    return tuple(outs)
```
